```python
import jax, jax.numpy as jnp
from jax import lax
import numpy as np

D_MODEL = 2048
BATCH = 8
SEQ = 4096
DEPTH = 2

GRID_W = 64
CTX_LEN = 256
D_MIX = D_MODEL
RET_HEADS = 4
RET_DK = 128
RET_DV = 256
RET_QK_W = RET_HEADS * RET_DK
RET_W = RET_HEADS * RET_DV
RET_CHUNK = 128
CONV_W = 512
CONV_K = 31
NA_HEADS = 4
NA_DH = 128
NA_W = NA_HEADS * NA_DH
NA_ROWS = 8
NA_COLS = 16
D_FF = 5632
FFN_K = 3
D_IN = 2 * RET_QK_W + 2 * RET_W + 2 * CONV_W + 3 * NA_W
ROPE_BASE = 10000.0
EPS = 1e-6

kernel_name = "hybrid_retention_conformer_natten_dit"


def _rmsnorm(x, g):
    xf = x.astype(jnp.float32)
    y = xf * lax.rsqrt(jnp.mean(xf * xf, axis=-1, keepdims=True) + EPS)
    return (y * g.astype(jnp.float32)).astype(x.dtype)


def _layernorm(x, g, b):
    xf = x.astype(jnp.float32)
    mu = jnp.mean(xf, axis=-1, keepdims=True)
    var = jnp.mean(jnp.square(xf - mu), axis=-1, keepdims=True)
    y = (xf - mu) * lax.rsqrt(var + EPS)
    return (y * g.astype(jnp.float32) + b.astype(jnp.float32)).astype(x.dtype)


def _adaln(cond, w, b):
    m = jax.nn.silu(cond) @ w + b
    return jnp.split(m, 6, axis=-1)


def _modulate(h, shift, scale):
    return h * (1 + scale) + shift


def _depthwise_conv(x, w, b):
    y = lax.conv_general_dilated(x, w[:, None, :].astype(x.dtype), window_strides=(1,), padding="SAME",
                                 dimension_numbers=("NWC", "WIO", "NWC"), feature_group_count=x.shape[-1])
    return y + b.astype(x.dtype)


def _axial_rope(t):
    L, dh = t.shape[1], t.shape[-1]
    half = dh // 2
    nf = half // 2
    pos = jnp.arange(L)
    row = (pos // GRID_W).astype(jnp.float32)
    col = (pos % GRID_W).astype(jnp.float32)
    inv = ROPE_BASE ** (-jnp.arange(nf, dtype=jnp.float32) / nf)

    def rot(xa, p):
        ang = p[:, None] * inv[None, :]
        cos = jnp.cos(ang)[None, :, None, :]
        sin = jnp.sin(ang)[None, :, None, :]
        x1, x2 = xa[..., :nf], xa[..., nf:]
        return jnp.concatenate([x1 * cos - x2 * sin, x2 * cos + x1 * sin], axis=-1)

    return jnp.concatenate([rot(t[..., :half], row), rot(t[..., half:], col)], axis=-1)


def _retention_chunkwise(q, k, v, log_gamma, state0):
    b, L, h, _ = q.shape
    dv = v.shape[-1]
    n = L // RET_CHUNK

    def chunks(t):
        return t.reshape(b, n, RET_CHUNK, h, t.shape[-1]).transpose(1, 0, 3, 2, 4)

    idx = jnp.arange(RET_CHUNK, dtype=jnp.float32)
    diff = idx[:, None] - idx[None, :]
    lower = diff >= 0
    decay_in = jnp.where(lower, jnp.exp(jnp.where(lower, diff, 0.0) * log_gamma[:, None, None]), 0.0)
    xi = jnp.exp((idx + 1.0) * log_gamma[:, None])[None, :, :, None]
    zeta = jnp.exp((RET_CHUNK - 1.0 - idx) * log_gamma[:, None])[None, :, :, None]
    g_chunk = jnp.exp(RET_CHUNK * log_gamma)[None, :, None, None]

    def step(state, blk):
        qc, kc, vc = blk
        inner = jnp.einsum("bhid,bhjd->bhij", qc, kc) * decay_in[None]
        out = jnp.einsum("bhij,bhjv->bhiv", inner, vc) + jnp.einsum("bhid,bhdv->bhiv", qc, state) * xi
        state = state * g_chunk + jnp.einsum("bhjd,bhjv->bhdv", kc * zeta, vc)
        return state, out

    state, out = lax.scan(step, state0, (chunks(q), chunks(k), chunks(v)))
    return out.transpose(1, 0, 3, 2, 4).reshape(b, L, h, dv), state


def _bidirectional_retention(q_l, k_l, v_l, q_c, k_c, v_c, log_gamma):
    b = q_l.shape[0]
    zeros = jnp.zeros((b, RET_HEADS, RET_DK, RET_DV), jnp.float32)
    flip = lambda t: jnp.flip(t, axis=1)
    oc_f, s_f = _retention_chunkwise(q_c, k_c, v_c, log_gamma[0], zeros)
    ol_f, _ = _retention_chunkwise(q_l, k_l, v_l, log_gamma[0], s_f)
    oc_b, s_b = _retention_chunkwise(flip(q_c), flip(k_c), flip(v_c), log_gamma[1], zeros)
    ol_b, _ = _retention_chunkwise(flip(q_l), flip(k_l), flip(v_l), log_gamma[1], s_b)
    return ol_f + flip(ol_b), oc_f + flip(oc_b)


def _gated_group_norm(o, gate, g):
    b, L, h, dv = o.shape
    mu = jnp.mean(o, axis=-1, keepdims=True)
    var = jnp.mean(jnp.square(o - mu), axis=-1, keepdims=True)
    y = ((o - mu) * lax.rsqrt(var + EPS)).reshape(b, L, h * dv) * g.astype(jnp.float32)
    return y.astype(gate.dtype) * jax.nn.silu(gate)


def _retention_group(lq, lk, lv, lg, cq, ck, cv, cg, decay_logits, gn_g, with_ctx):
    def heads(t, d):
        return t.astype(jnp.float32).reshape(t.shape[0], t.shape[1], RET_HEADS, d)

    scale = RET_DK ** -0.5
    q_l = _axial_rope(heads(lq, RET_DK)) * scale
    k_l = _axial_rope(heads(lk, RET_DK))
    q_c = heads(cq, RET_DK) * scale
    k_c = heads(ck, RET_DK)
    log_gamma = jax.nn.log_sigmoid(decay_logits.astype(jnp.float32))
    o_l, o_c = _bidirectional_retention(q_l, k_l, heads(lv, RET_DV), q_c, k_c, heads(cv, RET_DV), log_gamma)
    out_l = _gated_group_norm(o_l, lg, gn_g)
    out_c = _gated_group_norm(o_c, cg, gn_g) if with_ctx else None
    return out_l, out_c


def _conv_group(a, b, dw_w, dw_b, ln_g, ln_b, pw):
    u = a * jax.nn.sigmoid(b)
    u = _depthwise_conv(u, dw_w, dw_b)
    u = jax.nn.silu(_layernorm(u, ln_g, ln_b))
    return u @ pw


def _na_latent(q, k, v, k_c, v_c, rpb):
    b, L, h, d = q.shape
    rows_n = L // GRID_W
    kh = min(NA_ROWS, rows_n)
    rows = jnp.arange(rows_n)
    cols = jnp.arange(GRID_W)
    key_rows = jnp.clip(rows - kh // 2, 0, rows_n - kh)[:, None] + jnp.arange(kh)[None, :]
    col_start = jnp.clip(cols - NA_COLS // 2, 0, GRID_W - NA_COLS)
    col_in = (cols[None, :] >= col_start[:, None]) & (cols[None, :] < col_start[:, None] + NA_COLS)
    row_off = key_rows - rows[:, None] + NA_ROWS - 1
    col_off = jnp.clip(cols[None, :] - cols[:, None] + NA_COLS - 1, 0, 2 * NA_COLS - 2)
    bias = rpb.astype(jnp.float32)[:, row_off[:, None, :, None], col_off[None, :, None, :]]
    qg = q.reshape(b, rows_n, GRID_W, h, d) * (NA_DH ** -0.5)
    kg = k.reshape(b, rows_n, GRID_W, h, d)[:, key_rows]
    vg = v.reshape(b, rows_n, GRID_W, h, d)[:, key_rows]
    s_lat = jnp.einsum("brqhd,brkwhd->bhrqkw", qg, kg).astype(jnp.float32) + bias[None]
    s_lat = jnp.where(col_in[:, None, :], s_lat, -jnp.inf)
    s_ctx = jnp.einsum("brqhd,bchd->bhrqc", qg, k_c).astype(jnp.float32)
    n_lat = kh * GRID_W
    p = jax.nn.softmax(jnp.concatenate([s_lat.reshape(b, h, rows_n, GRID_W, n_lat), s_ctx], axis=-1), axis=-1)
    p = p.astype(v.dtype)
    p_lat = p[..., :n_lat].reshape(b, h, rows_n, GRID_W, kh, GRID_W)
    out = jnp.einsum("bhrqkw,brkwhd->brqhd", p_lat, vg) + jnp.einsum("bhrqc,bchd->brqhd", p[..., n_lat:], v_c)
    return out.reshape(b, L, h * d)


def _na_context(q_c, k_c, v_c):
    b, lc, h, d = q_c.shape
    s = jnp.einsum("bqhd,bkhd->bhqk", q_c * (NA_DH ** -0.5), k_c).astype(jnp.float32)
    p = jax.nn.softmax(s, axis=-1).astype(v_c.dtype)
    return jnp.einsum("bhqk,bkhd->bqhd", p, v_c).reshape(b, lc, h * d)


def _token_mixers(p_l, p_c, decay_logits, gn_g, dw_w, dw_b, ln_g, ln_b, pw, rpb, with_ctx):
    sizes = [RET_QK_W, RET_QK_W, RET_W, RET_W, CONV_W, CONV_W, NA_W, NA_W, NA_W]
    cuts = [int(s) for s in np.cumsum(sizes)[:-1]]
    lq, lk, lv, lg, la, lb, nq, nk, nv = jnp.split(p_l, cuts, axis=-1)
    cq, ck, cv, cg, ca, cb, cnq, cnk, cnv = jnp.split(p_c, cuts, axis=-1)

    def na_heads(t):
        return t.reshape(t.shape[0], t.shape[1], NA_HEADS, NA_DH)

    ret_l, ret_c = _retention_group(lq, lk, lv, lg, cq, ck, cv, cg, decay_logits, gn_g, with_ctx)
    conv_l = _conv_group(la, lb, dw_w, dw_b, ln_g, ln_b, pw)
    k_c, v_c = na_heads(cnk), na_heads(cnv)
    na_l = _na_latent(na_heads(nq), na_heads(nk), na_heads(nv), k_c, v_c, rpb)
    out_l = jnp.concatenate([ret_l, conv_l, na_l], axis=-1)
    if not with_ctx:
        return out_l, None
    conv_c = _conv_group(ca, cb, dw_w, dw_b, ln_g, ln_b, pw)
    na_c = _na_context(na_heads(cnq), k_c, v_c)
    out_c = jnp.concatenate([ret_c, conv_c, na_c], axis=-1)
    return out_l, out_c


def _conv_ffn(h, up, dw_w, dw_b, down):
    u = _depthwise_conv(h @ up, dw_w, dw_b)
    val, gate = jnp.split(u, 2, axis=-1)
    return (jax.nn.silu(gate) * val) @ down


def _fwd_setup_inputs(seed: int = 0) -> dict:
    key = jax.random.key(seed)
    ks = jax.random.split(key, 24)
    f32 = jnp.float32

    def nrm(k, shape, scale):
        return jax.random.normal(k, shape, f32) * scale

    base_decay = np.log(2.0 ** (5 + np.arange(RET_HEADS)) - 1.0)
    return {
        "x": nrm(ks[0], (BATCH, SEQ, D_MODEL), 1.0),
        "c": nrm(ks[1], (BATCH, D_MODEL), 1.0),
        "ctx": nrm(ks[2], (BATCH, CTX_LEN, D_MODEL), 1.0),
        "c_ctx": nrm(ks[3], (D_MODEL,), 1.0),
        "w_ada": nrm(ks[4], (DEPTH, D_MODEL, 6 * D_MODEL), 0.5 * D_MODEL ** -0.5),
        "b_ada": nrm(ks[5], (DEPTH, 6 * D_MODEL), 0.02),
        "norm1_g": 1.0 + nrm(ks[6], (DEPTH, D_MODEL), 0.02),
        "w_in": nrm(ks[7], (DEPTH, D_MODEL, D_IN), D_MODEL ** -0.5),
        "ret_decay": jnp.asarray(base_decay, f32)[None, None, :] + nrm(ks[8], (DEPTH, 2, RET_HEADS), 0.05),
        "ret_gn_g": 1.0 + nrm(ks[9], (DEPTH, RET_W), 0.02),
        "conv_dw_w": nrm(ks[10], (DEPTH, CONV_K, CONV_W), CONV_K ** -0.5),
        "conv_dw_b": nrm(ks[11], (DEPTH, CONV_W), 0.02),
        "conv_ln_g": 1.0 + nrm(ks[12], (DEPTH, CONV_W), 0.02),
        "conv_ln_b": nrm(ks[13], (DEPTH, CONV_W), 0.02),
        "conv_pw": nrm(ks[14], (DEPTH, CONV_W, CONV_W), CONV_W ** -0.5),
        "na_rpb": nrm(ks[15], (DEPTH, NA_HEADS, 2 * NA_ROWS - 1, 2 * NA_COLS - 1), 0.05),
        "w_out": nrm(ks[16], (DEPTH, D_MIX, D_MODEL), D_MIX ** -0.5),
        "norm2_g": 1.0 + nrm(ks[17], (DEPTH, D_MODEL), 0.02),
        "ffn_up": nrm(ks[18], (DEPTH, D_MODEL, 2 * D_FF), D_MODEL ** -0.5),
        "ffn_dw_w": nrm(ks[19], (DEPTH, FFN_K, 2 * D_FF), FFN_K ** -0.5),
        "ffn_dw_b": nrm(ks[20], (DEPTH, 2 * D_FF), 0.02),
        "ffn_down": nrm(ks[21], (DEPTH, D_FF, D_MODEL), D_FF ** -0.5),
        "final_g": 1.0 + nrm(ks[22], (D_MODEL,), 0.02),
    }


def _fwd_reference(x, c, ctx, c_ctx, w_ada, b_ada, norm1_g, w_in, ret_decay, ret_gn_g, conv_dw_w, conv_dw_b,
              conv_ln_g, conv_ln_b, conv_pw, na_rpb, w_out, norm2_g, ffn_up, ffn_dw_w, ffn_dw_b, ffn_down, final_g):
    h_ctx = ctx
    for l in range(DEPTH):
        last = l == DEPTH - 1
        sh1, sc1, g1, sh2, sc2, g2 = [t[:, None, :] for t in _adaln(c, w_ada[l], b_ada[l])]
        csh1, csc1, cg1, csh2, csc2, cg2 = _adaln(c_ctx, w_ada[l], b_ada[l])
        hl = _modulate(_rmsnorm(x, norm1_g[l]), sh1, sc1)
        hc = _modulate(_rmsnorm(h_ctx, norm1_g[l]), csh1, csc1)
        mix_l, mix_c = _token_mixers(hl @ w_in[l], hc @ w_in[l], ret_decay[l], ret_gn_g[l], conv_dw_w[l],
                                     conv_dw_b[l], conv_ln_g[l], conv_ln_b[l], conv_pw[l], na_rpb[l],
                                     with_ctx=not last)
        x = x + g1 * (mix_l @ w_out[l])
        hl2 = _modulate(_rmsnorm(x, norm2_g[l]), sh2, sc2)
        x = x + g2 * _conv_ffn(hl2, ffn_up[l], ffn_dw_w[l], ffn_dw_b[l], ffn_down[l])
        if not last:
            h_ctx = h_ctx + cg1 * (mix_c @ w_out[l])
            hc2 = _modulate(_rmsnorm(h_ctx, norm2_g[l]), csh2, csc2)
            h_ctx = h_ctx + cg2 * _conv_ffn(hc2, ffn_up[l], ffn_dw_w[l], ffn_dw_b[l], ffn_down[l])
    return _rmsnorm(x, final_g)


import jax as _jax
import jax.numpy as _jnp

TWIN_FORMAT = 'train_step'
FWD_PARAMS = ['x', 'c', 'ctx', 'c_ctx', 'w_ada', 'b_ada', 'norm1_g', 'w_in', 'ret_decay', 'ret_gn_g', 'conv_dw_w', 'conv_dw_b', 'conv_ln_g', 'conv_ln_b', 'conv_pw', 'na_rpb', 'w_out', 'norm2_g', 'ffn_up', 'ffn_dw_w', 'ffn_dw_b', 'ffn_down', 'final_g']
TWIN_WEIGHTS = ['c_ctx', 'w_ada', 'b_ada', 'norm1_g', 'w_in', 'ret_decay', 'ret_gn_g', 'conv_dw_w', 'conv_dw_b', 'conv_ln_g', 'conv_ln_b', 'conv_pw', 'na_rpb', 'w_out', 'norm2_g', 'ffn_up', 'ffn_dw_w', 'ffn_dw_b', 'ffn_down', 'final_g']
TWIN_DIFF_INPUT = 'x'
TWIN_INPUTS = ['x', 'c', 'ctx', 'c_ctx', 'w_ada', 'b_ada', 'norm1_g', 'w_in', 'ret_decay', 'ret_gn_g', 'conv_dw_w', 'conv_dw_b', 'conv_ln_g', 'conv_ln_b', 'conv_pw', 'na_rpb', 'w_out', 'norm2_g', 'ffn_up', 'ffn_dw_w', 'ffn_dw_b', 'ffn_down', 'final_g', 'loss_target', 'm_c_ctx', 'm_w_ada', 'm_b_ada', 'm_norm1_g', 'm_w_in', 'm_ret_decay', 'm_ret_gn_g', 'm_conv_dw_w', 'm_conv_dw_b', 'm_conv_ln_g', 'm_conv_ln_b', 'm_conv_pw', 'm_na_rpb', 'm_w_out', 'm_norm2_g', 'm_ffn_up', 'm_ffn_dw_w', 'm_ffn_dw_b', 'm_ffn_down', 'm_final_g', 'v_c_ctx', 'v_w_ada', 'v_b_ada', 'v_norm1_g', 'v_w_in', 'v_ret_decay', 'v_ret_gn_g', 'v_conv_dw_w', 'v_conv_dw_b', 'v_conv_ln_g', 'v_conv_ln_b', 'v_conv_pw', 'v_na_rpb', 'v_w_out', 'v_norm2_g', 'v_ffn_up', 'v_ffn_dw_w', 'v_ffn_dw_b', 'v_ffn_down', 'v_final_g']
TWIN_OUTPUTS = ['loss', 'grad_x', 'grad_c_ctx', 'grad_w_ada', 'grad_b_ada', 'grad_norm1_g', 'grad_w_in', 'grad_ret_decay', 'grad_ret_gn_g', 'grad_conv_dw_w', 'grad_conv_dw_b', 'grad_conv_ln_g', 'grad_conv_ln_b', 'grad_conv_pw', 'grad_na_rpb', 'grad_w_out', 'grad_norm2_g', 'grad_ffn_up', 'grad_ffn_dw_w', 'grad_ffn_dw_b', 'grad_ffn_down', 'grad_final_g', 'delta_c_ctx', 'delta_w_ada', 'delta_b_ada', 'delta_norm1_g', 'delta_w_in', 'delta_ret_decay', 'delta_ret_gn_g', 'delta_conv_dw_w', 'delta_conv_dw_b', 'delta_conv_ln_g', 'delta_conv_ln_b', 'delta_conv_pw', 'delta_na_rpb', 'delta_w_out', 'delta_norm2_g', 'delta_ffn_up', 'delta_ffn_dw_w', 'delta_ffn_dw_b', 'delta_ffn_down', 'delta_final_g', 'new_m_c_ctx', 'new_m_w_ada', 'new_m_b_ada', 'new_m_norm1_g', 'new_m_w_in', 'new_m_ret_decay', 'new_m_ret_gn_g', 'new_m_conv_dw_w', 'new_m_conv_dw_b', 'new_m_conv_ln_g', 'new_m_conv_ln_b', 'new_m_conv_pw', 'new_m_na_rpb', 'new_m_w_out', 'new_m_norm2_g', 'new_m_ffn_up', 'new_m_ffn_dw_w', 'new_m_ffn_dw_b', 'new_m_ffn_down', 'new_m_final_g', 'new_v_c_ctx', 'new_v_w_ada', 'new_v_b_ada', 'new_v_norm1_g', 'new_v_w_in', 'new_v_ret_decay', 'new_v_ret_gn_g', 'new_v_conv_dw_w', 'new_v_conv_dw_b', 'new_v_conv_ln_g', 'new_v_conv_ln_b', 'new_v_conv_pw', 'new_v_na_rpb', 'new_v_w_out', 'new_v_norm2_g', 'new_v_ffn_up', 'new_v_ffn_dw_w', 'new_v_ffn_dw_b', 'new_v_ffn_down', 'new_v_final_g']
TWIN_LEAF_KINDS = {'loss': 'loss', 'grad_x': 'grad_x', 'grad_c_ctx': 'grad_w', 'grad_w_ada': 'grad_w', 'grad_b_ada': 'grad_w', 'grad_norm1_g': 'grad_w', 'grad_w_in': 'grad_w', 'grad_ret_decay': 'grad_w', 'grad_ret_gn_g': 'grad_w', 'grad_conv_dw_w': 'grad_w', 'grad_conv_dw_b': 'grad_w', 'grad_conv_ln_g': 'grad_w', 'grad_conv_ln_b': 'grad_w', 'grad_conv_pw': 'grad_w', 'grad_na_rpb': 'grad_w', 'grad_w_out': 'grad_w', 'grad_norm2_g': 'grad_w', 'grad_ffn_up': 'grad_w', 'grad_ffn_dw_w': 'grad_w', 'grad_ffn_dw_b': 'grad_w', 'grad_ffn_down': 'grad_w', 'grad_final_g': 'grad_w', 'delta_c_ctx': 'delta_w', 'delta_w_ada': 'delta_w', 'delta_b_ada': 'delta_w', 'delta_norm1_g': 'delta_w', 'delta_w_in': 'delta_w', 'delta_ret_decay': 'delta_w', 'delta_ret_gn_g': 'delta_w', 'delta_conv_dw_w': 'delta_w', 'delta_conv_dw_b': 'delta_w', 'delta_conv_ln_g': 'delta_w', 'delta_conv_ln_b': 'delta_w', 'delta_conv_pw': 'delta_w', 'delta_na_rpb': 'delta_w', 'delta_w_out': 'delta_w', 'delta_norm2_g': 'delta_w', 'delta_ffn_up': 'delta_w', 'delta_ffn_dw_w': 'delta_w', 'delta_ffn_dw_b': 'delta_w', 'delta_ffn_down': 'delta_w', 'delta_final_g': 'delta_w', 'new_m_c_ctx': 'new_m', 'new_m_w_ada': 'new_m', 'new_m_b_ada': 'new_m', 'new_m_norm1_g': 'new_m', 'new_m_w_in': 'new_m', 'new_m_ret_decay': 'new_m', 'new_m_ret_gn_g': 'new_m', 'new_m_conv_dw_w': 'new_m', 'new_m_conv_dw_b': 'new_m', 'new_m_conv_ln_g': 'new_m', 'new_m_conv_ln_b': 'new_m', 'new_m_conv_pw': 'new_m', 'new_m_na_rpb': 'new_m', 'new_m_w_out': 'new_m', 'new_m_norm2_g': 'new_m', 'new_m_ffn_up': 'new_m', 'new_m_ffn_dw_w': 'new_m', 'new_m_ffn_dw_b': 'new_m', 'new_m_ffn_down': 'new_m', 'new_m_final_g': 'new_m', 'new_v_c_ctx': 'new_v', 'new_v_w_ada': 'new_v', 'new_v_b_ada': 'new_v', 'new_v_norm1_g': 'new_v', 'new_v_w_in': 'new_v', 'new_v_ret_decay': 'new_v', 'new_v_ret_gn_g': 'new_v', 'new_v_conv_dw_w': 'new_v', 'new_v_conv_dw_b': 'new_v', 'new_v_conv_ln_g': 'new_v', 'new_v_conv_ln_b': 'new_v', 'new_v_conv_pw': 'new_v', 'new_v_na_rpb': 'new_v', 'new_v_w_out': 'new_v', 'new_v_norm2_g': 'new_v', 'new_v_ffn_up': 'new_v', 'new_v_ffn_dw_w': 'new_v', 'new_v_ffn_dw_b': 'new_v', 'new_v_ffn_down': 'new_v', 'new_v_final_g': 'new_v'}


def _forward(args):
    return _fwd_reference(*[args[k] for k in FWD_PARAMS])


def _output_shape():
    def fwd():
        inp = _fwd_setup_inputs(0)
        return _fwd_reference(*[inp[k] for k in FWD_PARAMS])
    out = _jax.eval_shape(fwd)
    return out.shape, out.dtype

N_MICROBATCH = 1
ADAM_LR = 0.001
ADAM_B1 = 0.9
ADAM_B2 = 0.999
ADAM_EPS = 1e-08
ADAM_WD = 0.01
ADAM_STEP = 10
PER_EXAMPLE_BATCH_AXIS = {'x': 0, 'c': 0, 'ctx': 0, 'loss_target': 0}
SHARED_INPUTS = []
_WEIGHT_DTYPES = {'c_ctx': _jnp.float32, 'w_ada': _jnp.float32, 'b_ada': _jnp.float32, 'norm1_g': _jnp.float32, 'w_in': _jnp.float32, 'ret_decay': _jnp.float32, 'ret_gn_g': _jnp.float32, 'conv_dw_w': _jnp.float32, 'conv_dw_b': _jnp.float32, 'conv_ln_g': _jnp.float32, 'conv_ln_b': _jnp.float32, 'conv_pw': _jnp.float32, 'na_rpb': _jnp.float32, 'w_out': _jnp.float32, 'norm2_g': _jnp.float32, 'ffn_up': _jnp.float32, 'ffn_dw_w': _jnp.float32, 'ffn_dw_b': _jnp.float32, 'ffn_down': _jnp.float32, 'final_g': _jnp.float32}
MOMENT_SCALE = {'c_ctx': 1.059353e-02, 'w_ada': 2.601774e-02, 'b_ada': 4.377345e-02, 'norm1_g': 2.637771e-02, 'w_in': 1.744138e-02, 'ret_decay': 1.052664e-01, 'ret_gn_g': 1.807775e-02, 'conv_dw_w': 1.675731e-02, 'conv_dw_b': 2.924684e-02, 'conv_ln_g': 1.976321e-02, 'conv_ln_b': 1.707260e-02, 'conv_pw': 1.628098e-02, 'na_rpb': 9.416517e-04, 'w_out': 1.523327e-02, 'norm2_g': 2.634869e-02, 'ffn_up': 1.153026e-02, 'ffn_dw_w': 1.151620e-02, 'ffn_dw_b': 1.070492e-02, 'ffn_down': 1.887091e-02, 'final_g': 1.598932e+01}


def _to_microbatches(a, axis):
    t = _jnp.moveaxis(a, axis, 0)
    t = t.reshape((N_MICROBATCH, t.shape[0] // N_MICROBATCH) + t.shape[1:])
    return _jnp.moveaxis(t, 1, axis + 1)


def setup_inputs(seed: int = 0) -> dict:
    inp = _fwd_setup_inputs(seed)
    key = _jax.random.fold_in(_jax.random.key(seed), 7919)
    shape, _ = _output_shape()
    out = dict(inp)
    out["loss_target"] = _jax.random.normal(_jax.random.fold_in(key, 0), shape, _jnp.float32)
    for i, name in enumerate(TWIN_WEIGHTS):
        w = inp[name].astype(_jnp.float32)
        if MOMENT_SCALE is None:
            s = _jnp.sqrt(_jnp.mean(_jnp.square(w)) + 1e-30)
        else:
            s = MOMENT_SCALE[name]
        km, kv = _jax.random.split(_jax.random.fold_in(key, i + 1))
        out[name] = w
        out["m_" + name] = s * _jax.random.normal(km, w.shape, _jnp.float32)
        out["v_" + name] = (s * s) * _jax.random.uniform(kv, w.shape, _jnp.float32, 0.5, 1.5)
    if N_MICROBATCH > 1:
        for name, axis in PER_EXAMPLE_BATCH_AXIS.items():
            out[name] = _to_microbatches(out[name], axis)
    return {'x': out['x'], 'c': out['c'], 'ctx': out['ctx'], 'c_ctx': out['c_ctx'], 'w_ada': out['w_ada'], 'b_ada': out['b_ada'], 'norm1_g': out['norm1_g'], 'w_in': out['w_in'], 'ret_decay': out['ret_decay'], 'ret_gn_g': out['ret_gn_g'], 'conv_dw_w': out['conv_dw_w'], 'conv_dw_b': out['conv_dw_b'], 'conv_ln_g': out['conv_ln_g'], 'conv_ln_b': out['conv_ln_b'], 'conv_pw': out['conv_pw'], 'na_rpb': out['na_rpb'], 'w_out': out['w_out'], 'norm2_g': out['norm2_g'], 'ffn_up': out['ffn_up'], 'ffn_dw_w': out['ffn_dw_w'], 'ffn_dw_b': out['ffn_dw_b'], 'ffn_down': out['ffn_down'], 'final_g': out['final_g'], 'loss_target': out['loss_target'], 'm_c_ctx': out['m_c_ctx'], 'm_w_ada': out['m_w_ada'], 'm_b_ada': out['m_b_ada'], 'm_norm1_g': out['m_norm1_g'], 'm_w_in': out['m_w_in'], 'm_ret_decay': out['m_ret_decay'], 'm_ret_gn_g': out['m_ret_gn_g'], 'm_conv_dw_w': out['m_conv_dw_w'], 'm_conv_dw_b': out['m_conv_dw_b'], 'm_conv_ln_g': out['m_conv_ln_g'], 'm_conv_ln_b': out['m_conv_ln_b'], 'm_conv_pw': out['m_conv_pw'], 'm_na_rpb': out['m_na_rpb'], 'm_w_out': out['m_w_out'], 'm_norm2_g': out['m_norm2_g'], 'm_ffn_up': out['m_ffn_up'], 'm_ffn_dw_w': out['m_ffn_dw_w'], 'm_ffn_dw_b': out['m_ffn_dw_b'], 'm_ffn_down': out['m_ffn_down'], 'm_final_g': out['m_final_g'], 'v_c_ctx': out['v_c_ctx'], 'v_w_ada': out['v_w_ada'], 'v_b_ada': out['v_b_ada'], 'v_norm1_g': out['v_norm1_g'], 'v_w_in': out['v_w_in'], 'v_ret_decay': out['v_ret_decay'], 'v_ret_gn_g': out['v_ret_gn_g'], 'v_conv_dw_w': out['v_conv_dw_w'], 'v_conv_dw_b': out['v_conv_dw_b'], 'v_conv_ln_g': out['v_conv_ln_g'], 'v_conv_ln_b': out['v_conv_ln_b'], 'v_conv_pw': out['v_conv_pw'], 'v_na_rpb': out['v_na_rpb'], 'v_w_out': out['v_w_out'], 'v_norm2_g': out['v_norm2_g'], 'v_ffn_up': out['v_ffn_up'], 'v_ffn_dw_w': out['v_ffn_dw_w'], 'v_ffn_dw_b': out['v_ffn_dw_b'], 'v_ffn_down': out['v_ffn_down'], 'v_final_g': out['v_final_g']}


def _loss(weights, diff, rest, loss_target):
    with _jax.named_scope("forward"):
        args = {**rest, TWIN_DIFF_INPUT: diff, **{k: w.astype(_WEIGHT_DTYPES[k]) for k, w in weights.items()}}
        y = _forward(args)
    with _jax.named_scope("loss_head"):
        err = _jnp.square(y.astype(_jnp.float32) - loss_target)
        return 0.5 * _jnp.sum(_jnp.mean(err, axis=-1)) if err.ndim else 0.5 * err


def _adamw(w, g, m, v):
    m = ADAM_B1 * m + (1.0 - ADAM_B1) * g
    v = ADAM_B2 * v + (1.0 - ADAM_B2) * _jnp.square(g)
    m_hat = m / (1.0 - ADAM_B1 ** ADAM_STEP)
    v_hat = v / (1.0 - ADAM_B2 ** ADAM_STEP)
    delta = -ADAM_LR * (m_hat / (_jnp.sqrt(v_hat) + ADAM_EPS) + ADAM_WD * w)
    return delta, m, v


def reference(x, c, ctx, c_ctx, w_ada, b_ada, norm1_g, w_in, ret_decay, ret_gn_g, conv_dw_w, conv_dw_b, conv_ln_g, conv_ln_b, conv_pw, na_rpb, w_out, norm2_g, ffn_up, ffn_dw_w, ffn_dw_b, ffn_down, final_g, loss_target, m_c_ctx, m_w_ada, m_b_ada, m_norm1_g, m_w_in, m_ret_decay, m_ret_gn_g, m_conv_dw_w, m_conv_dw_b, m_conv_ln_g, m_conv_ln_b, m_conv_pw, m_na_rpb, m_w_out, m_norm2_g, m_ffn_up, m_ffn_dw_w, m_ffn_dw_b, m_ffn_down, m_final_g, v_c_ctx, v_w_ada, v_b_ada, v_norm1_g, v_w_in, v_ret_decay, v_ret_gn_g, v_conv_dw_w, v_conv_dw_b, v_conv_ln_g, v_conv_ln_b, v_conv_pw, v_na_rpb, v_w_out, v_norm2_g, v_ffn_up, v_ffn_dw_w, v_ffn_dw_b, v_ffn_down, v_final_g):
    given = dict(x=x, c=c, ctx=ctx, c_ctx=c_ctx, w_ada=w_ada, b_ada=b_ada, norm1_g=norm1_g, w_in=w_in, ret_decay=ret_decay, ret_gn_g=ret_gn_g, conv_dw_w=conv_dw_w, conv_dw_b=conv_dw_b, conv_ln_g=conv_ln_g, conv_ln_b=conv_ln_b, conv_pw=conv_pw, na_rpb=na_rpb, w_out=w_out, norm2_g=norm2_g, ffn_up=ffn_up, ffn_dw_w=ffn_dw_w, ffn_dw_b=ffn_dw_b, ffn_down=ffn_down, final_g=final_g, loss_target=loss_target, m_c_ctx=m_c_ctx, m_w_ada=m_w_ada, m_b_ada=m_b_ada, m_norm1_g=m_norm1_g, m_w_in=m_w_in, m_ret_decay=m_ret_decay, m_ret_gn_g=m_ret_gn_g, m_conv_dw_w=m_conv_dw_w, m_conv_dw_b=m_conv_dw_b, m_conv_ln_g=m_conv_ln_g, m_conv_ln_b=m_conv_ln_b, m_conv_pw=m_conv_pw, m_na_rpb=m_na_rpb, m_w_out=m_w_out, m_norm2_g=m_norm2_g, m_ffn_up=m_ffn_up, m_ffn_dw_w=m_ffn_dw_w, m_ffn_dw_b=m_ffn_dw_b, m_ffn_down=m_ffn_down, m_final_g=m_final_g, v_c_ctx=v_c_ctx, v_w_ada=v_w_ada, v_b_ada=v_b_ada, v_norm1_g=v_norm1_g, v_w_in=v_w_in, v_ret_decay=v_ret_decay, v_ret_gn_g=v_ret_gn_g, v_conv_dw_w=v_conv_dw_w, v_conv_dw_b=v_conv_dw_b, v_conv_ln_g=v_conv_ln_g, v_conv_ln_b=v_conv_ln_b, v_conv_pw=v_conv_pw, v_na_rpb=v_na_rpb, v_w_out=v_w_out, v_norm2_g=v_norm2_g, v_ffn_up=v_ffn_up, v_ffn_dw_w=v_ffn_dw_w, v_ffn_dw_b=v_ffn_dw_b, v_ffn_down=v_ffn_down, v_final_g=v_final_g)
    weights = {n: given[n] for n in TWIN_WEIGHTS}
    shared = {n: given[n] for n in SHARED_INPUTS}
    per_example = {n: given[n] for n in ['x', 'c', 'ctx']}
    grad_fn = _jax.value_and_grad(_loss, argnums=(0, 1))

    def one_microbatch(ex, loss_target):
        ex = dict(ex)
        diff = ex.pop(TWIN_DIFF_INPUT)
        return grad_fn(weights, diff, {**shared, **ex}, loss_target)

    if N_MICROBATCH == 1:
        loss, (grad_w, grad_x) = one_microbatch(per_example, given["loss_target"])
    else:
        def body(carry, xs):
            loss_sum, grad_sum = carry
            l_k, (gw_k, gx_k) = one_microbatch(xs[0], xs[1])
            with _jax.named_scope("update"):
                return (loss_sum + l_k, _jax.tree.map(_jnp.add, grad_sum, gw_k)), gx_k

        init = (_jnp.zeros((), _jnp.float32), _jax.tree.map(_jnp.zeros_like, weights))
        (loss, grad_w), grad_x = _jax.lax.scan(body, init, (per_example, given["loss_target"]))
    with _jax.named_scope("update"):
        delta_w, new_m, new_v = {}, {}, {}
        for n in TWIN_WEIGHTS:
            delta_w[n], new_m[n], new_v[n] = _adamw(weights[n], grad_w[n], given["m_" + n], given["v_" + n])
    return (loss, grad_x, *[grad_w[n] for n in TWIN_WEIGHTS], *[delta_w[n] for n in TWIN_WEIGHTS],
            *[new_m[n] for n in TWIN_WEIGHTS], *[new_v[n] for n in TWIN_WEIGHTS])
```

```python
import collections
import functools

import numpy as np
import jax
import jax.numpy as jnp
from jax import lax
from jax.experimental import pallas as pl
from jax.experimental.pallas import tpu as pltpu

F32 = jnp.float32
BF16 = jnp.bfloat16
EPS = 1e-6
ROPE_BASE = 10000.0
NEG = -1e30
LANE = 128
VMEM_LIMIT = 56 * 1024 * 1024

ADAM_LR, ADAM_B1, ADAM_B2, ADAM_EPS, ADAM_WD, ADAM_STEP = 0.001, 0.9, 0.999, 1e-08, 0.01, 10

Cfg = collections.namedtuple(
    "Cfg", "D T TC GW RH RDK RDV CW CK NH NDH NAR NAC DFF TB")


def make_cfg(D=2048, T=4096, TC=256, RH=4, CW=512, NH=4, DFF=5632):
    return Cfg(D=D, T=T, TC=TC, GW=64, RH=RH, RDK=128, RDV=256, CW=CW, CK=31, NH=NH, NDH=128,
               NAR=8, NAC=16, DFF=DFF, TB=256)


def _offsets(cfg):
    sizes = [cfg.RH * cfg.RDK, cfg.RH * cfg.RDK, cfg.RH * cfg.RDV, cfg.RH * cfg.RDV, cfg.CW, cfg.CW,
             cfg.NH * cfg.NDH, cfg.NH * cfg.NDH, cfg.NH * cfg.NDH]
    offs = [0]
    for s in sizes:
        offs.append(offs[-1] + s)
    return dict(zip(["lq", "lk", "lv", "lg", "la", "lb", "nq", "nk", "nv", "end"], offs))


def _pc(body, **kw):
    return pl.pallas_call(body, **kw)


def _cp(sem=None):
    return pltpu.CompilerParams(dimension_semantics=sem, vmem_limit_bytes=VMEM_LIMIT)


def _dot(a, b, ca, cb):
    return lax.dot_general(a, b, (((ca,), (cb,)), ((), ())), preferred_element_type=F32)


def dot_nn(a, b):
    return _dot(a, b, 1, 0)


def dot_nt(a, b):
    return _dot(a, b, 1, 1)


def dot_tn(a, b):
    return _dot(a, b, 0, 0)


def _sigmoid(x):
    return 1.0 / (1.0 + jnp.exp(-x))


def _silu(x):
    return x * _sigmoid(x)


def _dsilu(x):
    s = _sigmoid(x)
    return s * (1.0 + x * (1.0 - s))


def matmul(name, a, b, *, contract, grid, a_spec, b_spec, out_shape, out_spec, nk, into=None):
    dot = {"nn": dot_nn, "nt": dot_nt, "tn": dot_tn}[contract]
    direct = nk > 1 and out_shape.dtype == F32
    kax = len(grid) - 1

    def body(a_ref, b_ref, *rest):
        o_ref, *scr = rest[1:] if into is not None else rest
        p = dot(a_ref[...].astype(BF16), b_ref[...].astype(BF16))
        if nk == 1:
            o_ref[...] = p.astype(o_ref.dtype)
            return
        acc = o_ref if direct else scr[0]
        k = pl.program_id(kax)

        @pl.when(k == 0)
        def _():
            acc[...] = p

        @pl.when(k > 0)
        def _():
            acc[...] += p

        if not direct:
            @pl.when(k == nk - 1)
            def _():
                o_ref[...] = acc[...].astype(o_ref.dtype)

    scratch = []
    if nk > 1 and not direct:
        blk = [s for s in out_spec.block_shape if s is not None]
        scratch = [pltpu.VMEM(tuple(blk), F32)]
    sem = ("parallel",) * kax + (("arbitrary",) if nk > 1 else ("parallel",))
    in_specs, args, alias = [a_spec, b_spec], (a, b), {}
    if into is not None:
        in_specs, args, alias = in_specs + [pl.BlockSpec(memory_space=pl.ANY)], (a, b, into), {2: 0}
    return _pc(body, name=name, grid=grid, in_specs=in_specs, out_specs=out_spec, out_shape=out_shape,
               scratch_shapes=scratch, input_output_aliases=alias, compiler_params=_cp(sem))(*args)


def mm_rowsharded(name, a, w4, l, out_dtype, tn):
    L, K = a.shape
    nch, _, Kb, N = w4.shape
    tm = 256

    def body(a_ref, w_ref, o_ref):
        acc = dot_nn(a_ref[:, 0:Kb], w_ref[0])
        for j in range(1, nch):
            acc += dot_nn(a_ref[:, j * Kb:(j + 1) * Kb], w_ref[j])
        o_ref[...] = acc.astype(o_ref.dtype)

    return _pc(body, name=name, grid=(N // tn, L // tm),
               in_specs=[pl.BlockSpec((tm, K), lambda n, m: (m, 0)),
                         pl.BlockSpec((nch, None, Kb, tn), lambda n, m: (0, l, 0, n))],
               out_specs=pl.BlockSpec((tm, tn), lambda n, m: (m, n)),
               out_shape=jax.ShapeDtypeStruct((L, N), out_dtype),
               compiler_params=_cp(("parallel", "parallel")))(a, w4)


def _region(cfg):
    nlat = cfg.T // cfg.TB
    return lambda i: jnp.minimum(i // nlat, 1)


def norm_mod_fwd(cfg, name, x, ng, sc, sh):
    L, D = x.shape
    TB = cfg.TB
    reg = _region(cfg)

    def body(x_ref, ng_ref, sc_ref, sh_ref, h_ref):
        xv = x_ref[...]
        r = lax.rsqrt(jnp.mean(xv * xv, axis=-1, keepdims=True) + EPS)
        n = xv * r * ng_ref[...]
        h_ref[...] = (n * (1.0 + sc_ref[...]) + sh_ref[...]).astype(h_ref.dtype)

    row = pl.BlockSpec((TB, D), lambda i: (i, 0))
    vec = pl.BlockSpec((1, D), lambda i: (0, 0))
    rvec = pl.BlockSpec((None, 1, D), lambda i: (reg(i), 0, 0))
    return _pc(body, name=name, grid=(L // TB,), in_specs=[row, vec, rvec, rvec], out_specs=row,
               out_shape=jax.ShapeDtypeStruct((L, D), BF16), compiler_params=_cp(("parallel",)))(x, ng, sc, sh)


def norm_mod_bwd(cfg, name, dh, x, ng, sc, dx_in):
    L, D = x.shape
    TB = cfg.TB
    nlat = cfg.T // TB
    reg = _region(cfg)

    def body(dh_ref, x_ref, ng_ref, sc_ref, dxi_ref, dx_ref, dsc_ref, dsh_ref, dng_ref):
        i = pl.program_id(0)
        xv = x_ref[...]
        r = lax.rsqrt(jnp.mean(xv * xv, axis=-1, keepdims=True) + EPS)
        xh = xv * r
        g = ng_ref[...]
        n = xh * g
        dh = dh_ref[...]
        dn = dh * (1.0 + sc_ref[...])
        dxh = dn * g
        dx = r * (dxh - xh * jnp.mean(dxh * xh, axis=-1, keepdims=True))
        dx_ref[...] = dxi_ref[...] + dx
        s_sh = jnp.sum(dh, axis=0, keepdims=True)
        s_sc = jnp.sum(dh * n, axis=0, keepdims=True)
        s_ng = jnp.sum(dn * xh, axis=0, keepdims=True)
        first = jnp.logical_or(i == 0, i == nlat)

        @pl.when(first)
        def _():
            dsh_ref[...] = s_sh
            dsc_ref[...] = s_sc

        @pl.when(jnp.logical_not(first))
        def _():
            dsh_ref[...] += s_sh
            dsc_ref[...] += s_sc

        @pl.when(i == 0)
        def _():
            dng_ref[...] = s_ng

        @pl.when(i > 0)
        def _():
            dng_ref[...] += s_ng

    row = pl.BlockSpec((TB, D), lambda i: (i, 0))
    vec = pl.BlockSpec((1, D), lambda i: (0, 0))
    rvec = pl.BlockSpec((None, 1, D), lambda i: (reg(i), 0, 0))
    return _pc(body, name=name, grid=(L // TB,), in_specs=[row, row, vec, rvec, row],
               out_specs=[row, rvec, rvec, vec],
               out_shape=[jax.ShapeDtypeStruct((L, D), F32), jax.ShapeDtypeStruct((2, 1, D), F32),
                          jax.ShapeDtypeStruct((2, 1, D), F32), jax.ShapeDtypeStruct((1, D), F32)],
               compiler_params=_cp(("arbitrary",)))(dh, x, ng, sc, dx_in)


def resid_fwd(cfg, name, x, y, g):
    L, D = x.shape
    TB = cfg.TB
    reg = _region(cfg)

    def body(x_ref, y_ref, g_ref, o_ref):
        o_ref[...] = x_ref[...] + g_ref[...] * y_ref[...]

    row = pl.BlockSpec((TB, D), lambda i: (i, 0))
    rvec = pl.BlockSpec((None, 1, D), lambda i: (reg(i), 0, 0))
    return _pc(body, name=name, grid=(L // TB,), in_specs=[row, row, rvec], out_specs=row,
               out_shape=jax.ShapeDtypeStruct((L, D), F32), compiler_params=_cp(("parallel",)))(x, y, g)


def resid_bwd(cfg, name, dxo, y, g):
    L, D = y.shape
    TB = cfg.TB
    nlat = cfg.T // TB
    reg = _region(cfg)

    def body(d_ref, y_ref, g_ref, dy_ref, dg_ref):
        i = pl.program_id(0)
        d = d_ref[...]
        dy_ref[...] = (d * g_ref[...]).astype(dy_ref.dtype)
        s = jnp.sum(d * y_ref[...], axis=0, keepdims=True)
        first = jnp.logical_or(i == 0, i == nlat)

        @pl.when(first)
        def _():
            dg_ref[...] = s

        @pl.when(jnp.logical_not(first))
        def _():
            dg_ref[...] += s

    row = pl.BlockSpec((TB, D), lambda i: (i, 0))
    rvec = pl.BlockSpec((None, 1, D), lambda i: (reg(i), 0, 0))
    return _pc(body, name=name, grid=(L // TB,), in_specs=[row, row, rvec], out_specs=[row, rvec],
               out_shape=[jax.ShapeDtypeStruct((L, D), BF16), jax.ShapeDtypeStruct((2, 1, D), F32)],
               compiler_params=_cp(("arbitrary",)))(dxo, y, g)


def final_loss(cfg, name, x, fg, tgt):
    L, D = x.shape
    TB = cfg.TB
    nlat = cfg.T // TB

    def body(x_ref, fg_ref, t_ref, ls_ref, dx_ref, dg_ref):
        i = pl.program_id(0)

        @pl.when(i == 0)
        def _():
            ls_ref[...] = jnp.zeros_like(ls_ref)
            dg_ref[...] = jnp.zeros_like(dg_ref)

        @pl.when(i < nlat)
        def _():
            xv = x_ref[...]
            r = lax.rsqrt(jnp.mean(xv * xv, axis=-1, keepdims=True) + EPS)
            xh = xv * r
            g = fg_ref[...]
            e = xh * g - t_ref[...]
            ls_ref[...] += 0.5 * jnp.sum(e * e) / D
            dy = e / D
            dg_ref[...] += jnp.sum(dy * xh, axis=0, keepdims=True)
            dxh = dy * g
            dx_ref[...] = r * (dxh - xh * jnp.mean(dxh * xh, axis=-1, keepdims=True))

        @pl.when(i >= nlat)
        def _():
            dx_ref[...] = jnp.zeros_like(dx_ref)

    row = pl.BlockSpec((TB, D), lambda i: (i, 0))
    trow = pl.BlockSpec((TB, D), lambda i: (jnp.minimum(i, nlat - 1), 0))
    vec = pl.BlockSpec((1, D), lambda i: (0, 0))
    return _pc(body, name=name, grid=(L // TB,), in_specs=[row, vec, trow],
               out_specs=[pl.BlockSpec((1, LANE), lambda i: (0, 0)), row, vec],
               out_shape=[jax.ShapeDtypeStruct((1, LANE), F32), jax.ShapeDtypeStruct((L, D), F32),
                          jax.ShapeDtypeStruct((1, D), F32)],
               compiler_params=_cp(("arbitrary",)))(x, fg, tgt)


def rope_tables(cfg):
    half = cfg.RDK // 2
    nf = half // 2
    pos = np.arange(cfg.T)
    row = (pos // cfg.GW).astype(np.float32)
    col = (pos % cfg.GW).astype(np.float32)
    inv = jnp.asarray(ROPE_BASE, F32) ** (-jnp.arange(nf, dtype=F32) / nf)
    ar = jnp.asarray(row)[:, None] * inv[None, :]
    ac = jnp.asarray(col)[:, None] * inv[None, :]
    cos = jnp.concatenate([jnp.cos(ar), jnp.cos(ar), jnp.cos(ac), jnp.cos(ac)], axis=1)
    sin = jnp.concatenate([-jnp.sin(ar), jnp.sin(ar), -jnp.sin(ac), jnp.sin(ac)], axis=1)
    cos = jnp.concatenate([cos, jnp.ones((cfg.TC, cfg.RDK), F32)], axis=0)
    sin = jnp.concatenate([sin, jnp.zeros((cfg.TC, cfg.RDK), F32)], axis=0)
    return cos, sin


def _swap32(t):
    lane = lax.broadcasted_iota(jnp.int32, t.shape, 1)
    return jnp.where((lane % 64) < 32, pltpu.roll(t, 96, 1), pltpu.roll(t, 32, 1))


def rope_fwd(cfg, name, P, cos, sin):
    L = P.shape[0]
    TB = cfg.TB
    off = _offsets(cfg)
    cq, ck = off["lq"] // LANE, off["lk"] // LANE
    scale = cfg.RDK ** -0.5

    def body(q_ref, k_ref, c_ref, s_ref, qo_ref, ko_ref):
        c = c_ref[...]
        s = s_ref[...]
        q = q_ref[...]
        k = k_ref[...]
        qo_ref[...] = (q * c + _swap32(q) * s) * scale
        ko_ref[...] = k * c + _swap32(k) * s

    tab = pl.BlockSpec((TB, LANE), lambda i, h: (i, 0))
    out = pl.BlockSpec((TB, LANE), lambda i, h: (i, h))
    shp = jax.ShapeDtypeStruct((L, cfg.RH * cfg.RDK), F32)
    return _pc(body, name=name, grid=(L // TB, cfg.RH),
               in_specs=[pl.BlockSpec((TB, LANE), lambda i, h: (i, cq + h)),
                         pl.BlockSpec((TB, LANE), lambda i, h: (i, ck + h)), tab, tab],
               out_specs=[out, out], out_shape=[shp, shp],
               compiler_params=_cp(("parallel", "parallel")))(P, P, cos, sin)


def rope_bwd(cfg, name, dq2, dk2, cos, sin):
    _, L, W = dq2.shape
    TB = cfg.TB
    scale = cfg.RDK ** -0.5

    def body(dq_ref, dk_ref, c_ref, s_ref, qo_ref, ko_ref):
        c = c_ref[...]
        s = s_ref[...]
        dq = dq_ref[0] + dq_ref[1]
        dk = dk_ref[0] + dk_ref[1]
        qo_ref[...] = ((dq * c - _swap32(dq) * s) * scale).astype(qo_ref.dtype)
        ko_ref[...] = (dk * c - _swap32(dk) * s).astype(ko_ref.dtype)

    tab = pl.BlockSpec((TB, LANE), lambda i, h: (i, 0))
    blk = pl.BlockSpec((2, TB, LANE), lambda i, h: (0, i, h))
    out = pl.BlockSpec((TB, LANE), lambda i, h: (i, h))
    shp = jax.ShapeDtypeStruct((L, W), BF16)
    return _pc(body, name=name, grid=(L // TB, cfg.RH), in_specs=[blk, blk, tab, tab],
               out_specs=[out, out], out_shape=[shp, shp],
               compiler_params=_cp(("parallel", "parallel")))(dq2, dk2, cos, sin)


def _ret_chunk_map(cfg):
    C = cfg.RDK
    n = (cfg.T + cfg.TC) // C
    nlat, nctx = cfg.T // C, cfg.TC // C

    def chunk(d, s):
        fwd = jnp.where(s < nctx, nlat + s, s - nctx)
        return jnp.where(d == 0, fwd, n - 1 - s)

    return n, chunk


def _ret_decay_terms(d, lam, C):
    ii = lax.broadcasted_iota(jnp.int32, (C, C), 0)
    jj = lax.broadcasted_iota(jnp.int32, (C, C), 1)
    diff = jnp.where(d == 0, ii - jj, jj - ii).astype(F32)
    dpos = jnp.maximum(diff, 0.0)
    Dm = jnp.where(diff >= 0, jnp.exp(dpos * lam), 0.0)
    ic = lax.broadcasted_iota(jnp.int32, (C, 1), 0).astype(F32)
    cxi = jnp.where(d == 0, ic + 1.0, C - ic)
    cze = jnp.where(d == 0, C - 1.0 - ic, ic)
    xi = jnp.exp(cxi * lam)
    ze = jnp.exp(cze * lam)
    g = jnp.exp(jnp.full((1, 1), C, F32) * lam)
    return dpos, Dm, cxi, cze, xi, ze, g


def retention_fwd(cfg, name, qr, kr, P, lam):
    L = P.shape[0]
    C, DV, RH = cfg.RDK, cfg.RDV, cfg.RH
    n, chunk = _ret_chunk_map(cfg)
    vc0 = _offsets(cfg)["lv"] // DV

    def body(lam_ref, q_ref, k_ref, v_ref, o_ref, st_ref, S):
        d, h, s = pl.program_id(0), pl.program_id(1), pl.program_id(2)
        lam_ = lam_ref[d, h]

        @pl.when(s == 0)
        def _():
            S[...] = jnp.zeros_like(S)

        _, Dm, _, _, xi, ze, g = _ret_decay_terms(d, lam_, C)
        k = k_ref[...]
        qb = q_ref[...].astype(BF16)
        kb = k.astype(BF16)
        vb = v_ref[...].astype(BF16)
        Sv = S[...]
        st_ref[...] = Sv
        A = dot_nt(qb, kb) * Dm
        o_ref[...] = dot_nn(A.astype(BF16), vb) + dot_nn(qb, Sv.astype(BF16)) * xi
        S[...] = Sv * g + dot_tn((k * ze).astype(BF16), vb)

    qspec = pl.BlockSpec((C, C), lambda d, h, s: (chunk(d, s), h))
    return _pc(body, name=name, grid=(2, RH, n),
               in_specs=[pl.BlockSpec(memory_space=pltpu.SMEM), qspec, qspec,
                         pl.BlockSpec((C, DV), lambda d, h, s: (chunk(d, s), vc0 + h))],
               out_specs=[pl.BlockSpec((None, C, DV), lambda d, h, s: (d, chunk(d, s), h)),
                          pl.BlockSpec((None, None, None, C, DV), lambda d, h, s: (d, h, s, 0, 0))],
               out_shape=[jax.ShapeDtypeStruct((2, L, RH * DV), F32),
                          jax.ShapeDtypeStruct((2, RH, n, C, DV), F32)],
               scratch_shapes=[pltpu.VMEM((C, DV), F32)],
               compiler_params=_cp(("parallel", "parallel", "arbitrary")))(lam, qr, kr, P)


def retention_bwd(cfg, name, qr, kr, P, lam, st, do):
    L = P.shape[0]
    C, DV, RH = cfg.RDK, cfg.RDV, cfg.RH
    n, chunk = _ret_chunk_map(cfg)
    vc0 = _offsets(cfg)["lv"] // DV

    def body(lam_ref, q_ref, k_ref, v_ref, st_ref, do_ref, dq_ref, dk_ref, dv_ref, dl_ref, dS):
        d, h, si = pl.program_id(0), pl.program_id(1), pl.program_id(2)
        lam_ = lam_ref[d, h]

        @pl.when(si == 0)
        def _():
            dS[...] = jnp.zeros_like(dS)
            dl_ref[...] = jnp.zeros_like(dl_ref)

        dpos, Dm, cxi, cze, xi, ze, g = _ret_decay_terms(d, lam_, C)
        k = k_ref[...]
        do = do_ref[...]
        qb = q_ref[...].astype(BF16)
        kb = k.astype(BF16)
        vb = v_ref[...].astype(BF16)
        dob = do.astype(BF16)
        Sn = st_ref[...]
        Snb = Sn.astype(BF16)
        dSn = dS[...]
        dSb = dSn.astype(BF16)
        A = dot_nt(qb, kb) * Dm
        dA = dot_nt(dob, vb)
        dQK = (dA * Dm).astype(BF16)
        kzb = (k * ze).astype(BF16)
        dv_ref[...] = dot_tn(A.astype(BF16), dob) + dot_nn(kzb, dSb)
        dkz = dot_nt(vb, dSb)
        dox = do * xi
        doxb = dox.astype(BF16)
        dq_ref[...] = dot_nn(dQK, kb) + dot_nt(doxb, Snb)
        dk_ref[...] = dot_tn(dQK, qb) + dkz * ze
        QS = dot_nn(qb, Snb)
        t = (jnp.sum(dA * A * dpos) + jnp.sum(do * QS * (cxi * xi)) + jnp.sum(k * dkz * (cze * ze)))
        t4 = jnp.sum(dSn * Sn, axis=0, keepdims=True)
        t4 = jnp.sum(t4 * (g * C), axis=1, keepdims=True)
        dl_ref[...] += t + t4
        dS[...] = g * dSn + dot_tn(qb, doxb)

    rs = lambda si: n - 1 - si
    qspec = pl.BlockSpec((C, C), lambda d, h, si: (chunk(d, rs(si)), h))
    vspec = pl.BlockSpec((C, DV), lambda d, h, si: (chunk(d, rs(si)), vc0 + h))
    dqs = pl.BlockSpec((None, C, C), lambda d, h, si: (d, chunk(d, rs(si)), h))
    return _pc(body, name=name, grid=(2, RH, n),
               in_specs=[pl.BlockSpec(memory_space=pltpu.SMEM), qspec, qspec, vspec,
                         pl.BlockSpec((None, None, None, C, DV), lambda d, h, si: (d, h, rs(si), 0, 0)),
                         pl.BlockSpec((C, DV), lambda d, h, si: (chunk(d, rs(si)), h))],
               out_specs=[dqs, dqs,
                          pl.BlockSpec((None, C, DV), lambda d, h, si: (d, chunk(d, rs(si)), h)),
                          pl.BlockSpec((None, None, 8, LANE), lambda d, h, si: (d, h, 0, 0))],
               out_shape=[jax.ShapeDtypeStruct((2, L, RH * C), F32), jax.ShapeDtypeStruct((2, L, RH * C), F32),
                          jax.ShapeDtypeStruct((2, L, RH * DV), F32), jax.ShapeDtypeStruct((2, RH, 8, LANE), F32)],
               scratch_shapes=[pltpu.VMEM((C, DV), F32)],
               compiler_params=_cp(("parallel", "parallel", "arbitrary")))(lam, qr, kr, P, st, do)


def ggn_fwd(cfg, name, o2, P, gn_g):
    L = P.shape[0]
    TB, DV, RH = cfg.TB, cfg.RDV, cfg.RH
    gc0 = _offsets(cfg)["lg"] // DV

    def body(o_ref, gate_ref, g_ref, out_ref):
        o = o_ref[0] + o_ref[1]
        mu = jnp.mean(o, axis=-1, keepdims=True)
        xc = o - mu
        var = jnp.mean(xc * xc, axis=-1, keepdims=True)
        y = xc * lax.rsqrt(var + EPS) * g_ref[...]
        out_ref[...] = (y * _silu(gate_ref[...])).astype(out_ref.dtype)

    return _pc(body, name=name, grid=(L // TB, RH),
               in_specs=[pl.BlockSpec((2, TB, DV), lambda i, h: (0, i, h)),
                         pl.BlockSpec((TB, DV), lambda i, h: (i, gc0 + h)),
                         pl.BlockSpec((1, DV), lambda i, h: (0, h))],
               out_specs=pl.BlockSpec((TB, DV), lambda i, h: (i, h)),
               out_shape=jax.ShapeDtypeStruct((L, RH * DV), BF16),
               compiler_params=_cp(("parallel", "parallel")))(o2, P, gn_g)


def ggn_bwd(cfg, name, dout, o2, P, gn_g, col0):
    L = P.shape[0]
    TB, DV, RH = cfg.TB, cfg.RDV, cfg.RH
    gc0 = _offsets(cfg)["lg"] // DV

    def body(d_ref, o_ref, gate_ref, g_ref, do_ref, dgate_ref, dg_ref):
        i = pl.program_id(1)
        o = o_ref[0] + o_ref[1]
        mu = jnp.mean(o, axis=-1, keepdims=True)
        xc = o - mu
        var = jnp.mean(xc * xc, axis=-1, keepdims=True)
        r = lax.rsqrt(var + EPS)
        y = xc * r
        g = g_ref[...]
        gate = gate_ref[...]
        d = d_ref[...]
        dgate_ref[...] = (d * (y * g) * _dsilu(gate)).astype(dgate_ref.dtype)
        dyg = d * _silu(gate)
        s = jnp.sum(dyg * y, axis=0, keepdims=True)

        @pl.when(i == 0)
        def _():
            dg_ref[...] = s

        @pl.when(i > 0)
        def _():
            dg_ref[...] += s

        dy = dyg * g
        do_ref[...] = r * (dy - jnp.mean(dy, axis=-1, keepdims=True)
                           - y * jnp.mean(dy * y, axis=-1, keepdims=True))

    blk = pl.BlockSpec((TB, DV), lambda h, i: (i, h))
    return _pc(body, name=name, grid=(RH, L // TB),
               in_specs=[pl.BlockSpec((TB, DV), lambda h, i: (i, col0 + h)),
                         pl.BlockSpec((2, TB, DV), lambda h, i: (0, i, h)),
                         pl.BlockSpec((TB, DV), lambda h, i: (i, gc0 + h)),
                         pl.BlockSpec((1, DV), lambda h, i: (0, h))],
               out_specs=[blk, blk, pl.BlockSpec((1, DV), lambda h, i: (0, h))],
               out_shape=[jax.ShapeDtypeStruct((L, RH * DV), F32), jax.ShapeDtypeStruct((L, RH * DV), BF16),
                          jax.ShapeDtypeStruct((1, RH * DV), F32)],
               compiler_params=_cp(("parallel", "arbitrary")))(dout, o2, P, gn_g)


def cast_cols(cfg, name, src, col0, ncols, width):
    L = src.shape[0] if src.ndim == 2 else src.shape[1]
    TB = cfg.TB
    if src.ndim == 3:
        def body(s_ref, o_ref):
            o_ref[...] = (s_ref[0] + s_ref[1]).astype(o_ref.dtype)
        spec = pl.BlockSpec((2, TB, width), lambda i, j: (0, i, col0 + j))
    else:
        def body(s_ref, o_ref):
            o_ref[...] = s_ref[...].astype(o_ref.dtype)
        spec = pl.BlockSpec((TB, width), lambda i, j: (i, col0 + j))
    return _pc(body, name=name, grid=(L // TB, ncols), in_specs=[spec],
               out_specs=pl.BlockSpec((TB, width), lambda i, j: (i, j)),
               out_shape=jax.ShapeDtypeStruct((L, ncols * width), BF16),
               compiler_params=_cp(("parallel", "parallel")))(src)


_CPAD = 16


def _conv_windows(cfg):
    T, TC, TB = cfg.T, cfg.TC, cfg.TB
    assert TC % TB == 0 and T % TB == 0 and cfg.CK // 2 < _CPAD
    return T // TB, [(T + j * TB, T + _CPAD + j * TB) for j in range(TC // TB)]


def _fill_padded(cfg, pb, get):
    T, TC, TB = cfg.T, cfg.TC, cfg.TB
    z = jnp.zeros((_CPAD, LANE), F32)
    pb[0:_CPAD, :] = z
    pb[_CPAD + T:2 * _CPAD + T, :] = z
    pb[2 * _CPAD + T + TC:3 * _CPAD + T + TC, :] = z

    def fill(i, c):
        r0 = pl.multiple_of(i * TB, TB)
        pb[pl.ds(r0 + _CPAD, TB), :] = get(r0)
        return c

    lax.fori_loop(0, T // TB, fill, 0)
    for j in range(TC // TB):
        pb[2 * _CPAD + T + j * TB:2 * _CPAD + T + (j + 1) * TB, :] = get(T + j * TB)


def _taps(win, TB):
    W = TB + 2 * _CPAD
    return lambda k: pltpu.roll(win, W - (k + 1), 0)[0:TB, :]


def glu_dwconv_fwd(cfg, name, P, w, b):
    L = P.shape[0]
    T, TC, TB, K = cfg.T, cfg.TC, cfg.TB, cfg.CK
    off = _offsets(cfg)
    ca, cb = off["la"] // LANE, off["lb"] // LANE
    nlat, ctx_tiles = _conv_windows(cfg)
    PBL = 3 * _CPAD + T + TC

    def body(a_ref, b_ref, w_ref, bias_ref, y_ref, pb):
        _fill_padded(cfg, pb, lambda r0: a_ref[pl.ds(r0, TB), :] * _sigmoid(b_ref[pl.ds(r0, TB), :]))
        wv = w_ref[...]
        bias = bias_ref[...]

        def tile(win):
            tap = _taps(win, TB)
            acc = jnp.zeros((TB, LANE), F32) + bias
            for k in range(K):
                acc = acc + wv[k:k + 1, :] * tap(k)
            return acc

        def lat(i, c):
            r0 = pl.multiple_of(i * TB, TB)
            y_ref[pl.ds(r0, TB), :] = tile(pb[pl.ds(r0, TB + 2 * _CPAD), :])
            return c

        lax.fori_loop(0, nlat, lat, 0)
        for r0, w0 in ctx_tiles:
            y_ref[r0:r0 + TB, :] = tile(pb[w0:w0 + TB + 2 * _CPAD, :])

    return _pc(body, name=name, grid=(cfg.CW // LANE,),
               in_specs=[pl.BlockSpec((L, LANE), lambda j: (0, ca + j)),
                         pl.BlockSpec((L, LANE), lambda j: (0, cb + j)),
                         pl.BlockSpec((32, LANE), lambda j: (0, j)),
                         pl.BlockSpec((1, LANE), lambda j: (0, j))],
               out_specs=pl.BlockSpec((L, LANE), lambda j: (0, j)),
               out_shape=jax.ShapeDtypeStruct((L, cfg.CW), F32),
               scratch_shapes=[pltpu.VMEM((PBL, LANE), F32)],
               compiler_params=_cp(("parallel",)))(P, P, w, b)


def glu_dwconv_bwd(cfg, name, P, w, dy):
    L = P.shape[0]
    T, TC, TB, K = cfg.T, cfg.TC, cfg.TB, cfg.CK
    off = _offsets(cfg)
    ca, cb = off["la"] // LANE, off["lb"] // LANE
    nlat, ctx_tiles = _conv_windows(cfg)
    PBL = 3 * _CPAD + T + TC

    def body(a_ref, b_ref, w_ref, dy_ref, da_ref, db_ref, dw_ref, dbias_ref, pbu, pbd):
        _fill_padded(cfg, pbu, lambda r0: a_ref[pl.ds(r0, TB), :] * _sigmoid(b_ref[pl.ds(r0, TB), :]))
        _fill_padded(cfg, pbd, lambda r0: dy_ref[pl.ds(r0, TB), :])
        wv = w_ref[...]
        dw_ref[...] = jnp.zeros_like(dw_ref)
        dbias_ref[...] = jnp.zeros_like(dbias_ref)

        def tile(r0, winu, wind):
            tapu = _taps(winu, TB)
            tapd = _taps(wind, TB)
            dyt = dy_ref[pl.ds(r0, TB), :]
            du = jnp.zeros((TB, LANE), F32)
            for k in range(K):
                du = du + wv[k:k + 1, :] * tapd(K - 1 - k)
                dw_ref[k:k + 1, :] += jnp.sum(dyt * tapu(k), axis=0, keepdims=True)
            dbias_ref[...] += jnp.sum(dyt, axis=0, keepdims=True)
            a = a_ref[pl.ds(r0, TB), :]
            sg = _sigmoid(b_ref[pl.ds(r0, TB), :])
            da_ref[pl.ds(r0, TB), :] = (du * sg).astype(da_ref.dtype)
            db_ref[pl.ds(r0, TB), :] = (du * a * sg * (1.0 - sg)).astype(db_ref.dtype)

        def lat(i, c):
            r0 = pl.multiple_of(i * TB, TB)
            tile(r0, pbu[pl.ds(r0, TB + 2 * _CPAD), :], pbd[pl.ds(r0, TB + 2 * _CPAD), :])
            return c

        lax.fori_loop(0, nlat, lat, 0)
        for r0, w0 in ctx_tiles:
            tile(r0, pbu[w0:w0 + TB + 2 * _CPAD, :], pbd[w0:w0 + TB + 2 * _CPAD, :])

    col = pl.BlockSpec((L, LANE), lambda j: (0, j))
    return _pc(body, name=name, grid=(cfg.CW // LANE,),
               in_specs=[pl.BlockSpec((L, LANE), lambda j: (0, ca + j)),
                         pl.BlockSpec((L, LANE), lambda j: (0, cb + j)),
                         pl.BlockSpec((32, LANE), lambda j: (0, j)), col],
               out_specs=[col, col, pl.BlockSpec((32, LANE), lambda j: (0, j)),
                          pl.BlockSpec((1, LANE), lambda j: (0, j))],
               out_shape=[jax.ShapeDtypeStruct((L, cfg.CW), BF16), jax.ShapeDtypeStruct((L, cfg.CW), BF16),
                          jax.ShapeDtypeStruct((32, cfg.CW), F32), jax.ShapeDtypeStruct((1, cfg.CW), F32)],
               scratch_shapes=[pltpu.VMEM((PBL, LANE), F32), pltpu.VMEM((PBL, LANE), F32)],
               compiler_params=_cp(("parallel",)))(P, P, w, dy)


def ln_silu_fwd(cfg, name, y, g, b):
    L, W = y.shape
    TB = cfg.TB

    def body(y_ref, g_ref, b_ref, o_ref):
        yv = y_ref[...]
        mu = jnp.mean(yv, axis=-1, keepdims=True)
        xc = yv - mu
        var = jnp.mean(xc * xc, axis=-1, keepdims=True)
        z = xc * lax.rsqrt(var + EPS) * g_ref[...] + b_ref[...]
        o_ref[...] = _silu(z).astype(o_ref.dtype)

    row = pl.BlockSpec((TB, W), lambda i: (i, 0))
    vec = pl.BlockSpec((1, W), lambda i: (0, 0))
    return _pc(body, name=name, grid=(L // TB,), in_specs=[row, vec, vec], out_specs=row,
               out_shape=jax.ShapeDtypeStruct((L, W), BF16), compiler_params=_cp(("parallel",)))(y, g, b)


def ln_silu_bwd(cfg, name, dact, y, g, b):
    L, W = y.shape
    TB = cfg.TB

    def body(d_ref, y_ref, g_ref, b_ref, dy_ref, dg_ref, db_ref):
        i = pl.program_id(0)
        yv = y_ref[...]
        mu = jnp.mean(yv, axis=-1, keepdims=True)
        xc = yv - mu
        var = jnp.mean(xc * xc, axis=-1, keepdims=True)
        r = lax.rsqrt(var + EPS)
        yh = xc * r
        g = g_ref[...]
        z = yh * g + b_ref[...]
        dz = d_ref[...] * _dsilu(z)
        sg = jnp.sum(dz * yh, axis=0, keepdims=True)
        sb = jnp.sum(dz, axis=0, keepdims=True)

        @pl.when(i == 0)
        def _():
            dg_ref[...] = sg
            db_ref[...] = sb

        @pl.when(i > 0)
        def _():
            dg_ref[...] += sg
            db_ref[...] += sb

        dh = dz * g
        dy_ref[...] = r * (dh - jnp.mean(dh, axis=-1, keepdims=True)
                           - yh * jnp.mean(dh * yh, axis=-1, keepdims=True))

    row = pl.BlockSpec((TB, W), lambda i: (i, 0))
    vec = pl.BlockSpec((1, W), lambda i: (0, 0))
    return _pc(body, name=name, grid=(L // TB,), in_specs=[row, row, vec, vec], out_specs=[row, vec, vec],
               out_shape=[jax.ShapeDtypeStruct((L, W), F32), jax.ShapeDtypeStruct((1, W), F32),
                          jax.ShapeDtypeStruct((1, W), F32)],
               compiler_params=_cp(("arbitrary",)))(dact, y, g, b)


def _na_geometry(cfg):
    R = cfg.T // cfg.GW
    nb = R // cfg.NAR
    assert nb >= 3 and cfg.GW == 64 and cfg.NAR == 8
    ks = [int(np.clip(8 * b - 4, 0, R - 16)) for b in range(nb)]
    return R, nb, ks


_NTAB = 18


def _split3(x):
    hi = x.astype(BF16)
    r = x - hi.astype(F32)
    mid = r.astype(BF16)
    lo = (r - mid.astype(F32)).astype(BF16)
    return hi, mid, lo


def _na_col_onehot(cfg):
    GW, NAC = cfg.GW, cfg.NAC
    qc = np.arange(GW)[:, None]
    kc = np.arange(GW)[None, :]
    cs = np.clip(qc - NAC // 2, 0, GW - NAC)
    vcol = (kc >= cs) & (kc < cs + NAC)
    dd = np.clip(kc - qc + NAC - 1, 0, 2 * NAC - 2)
    oh = (np.arange(LANE)[:, None, None] == dd[None]).astype(np.float32)
    z = np.zeros_like(oh)
    oda = np.concatenate([oh, z], axis=2).reshape(LANE, GW * LANE)
    odb = np.concatenate([z, oh], axis=2).reshape(LANE, GW * LANE)
    cm = np.where(np.concatenate([vcol, vcol], axis=1), 0.0, NEG).astype(np.float32).reshape(1, GW * LANE)
    return oda, odb, cm


def na_tables(cfg, name, rpb):
    NH, GW = cfg.NH, cfg.GW
    na = rpb.shape[1]
    oda, odb, cm = _na_col_onehot(cfg)
    rp = jnp.zeros((NH, _NTAB + 1, LANE), F32).at[:, 1:1 + na, :rpb.shape[2]].set(rpb.astype(F32))
    r0 = rp[:, :_NTAB].reshape(NH * _NTAB, LANE)
    r1 = rp[:, 1:].reshape(NH * _NTAB, LANE)
    a = np.arange(_NTAB) - 1
    rm0 = np.where((a >= 0) & (a < na), 0.0, NEG).astype(np.float32)
    rm1 = np.where((a + 1 >= 0) & (a + 1 < na), 0.0, NEG).astype(np.float32)
    half = (np.arange(GW * LANE) % LANE >= GW)[None, :]
    rmask = np.where(half, np.tile(rm1, NH)[:, None], np.tile(rm0, NH)[:, None]).astype(np.float32)
    tn = 2048
    rows = NH * _NTAB

    def body(r0_ref, r1_ref, a_ref, b_ref, cm_ref, rm_ref, o_ref):
        acc = cm_ref[...] + rm_ref[...]
        for t in _split3(r0_ref[...]):
            acc = acc + dot_nn(t, a_ref[...])
        for t in _split3(r1_ref[...]):
            acc = acc + dot_nn(t, b_ref[...])
        o_ref[...] = acc

    rs = pl.BlockSpec((rows, LANE), lambda n: (0, 0))
    out = _pc(body, name=name, grid=(GW * LANE // tn,),
              in_specs=[rs, rs, pl.BlockSpec((LANE, tn), lambda n: (0, n)), pl.BlockSpec((LANE, tn), lambda n: (0, n)),
                        pl.BlockSpec((1, tn), lambda n: (0, n)), pl.BlockSpec((rows, tn), lambda n: (0, n))],
              out_specs=pl.BlockSpec((rows, tn), lambda n: (0, n)),
              out_shape=jax.ShapeDtypeStruct((rows, GW * LANE), F32),
              compiler_params=_cp(("parallel",)))(r0, r1, jnp.asarray(oda, BF16), jnp.asarray(odb, BF16),
                                                  jnp.asarray(cm), jnp.asarray(rmask))
    return out.reshape(NH, _NTAB, GW, LANE)


def _na_tiles(cfg, b):
    R, nb, _ = _na_geometry(cfg)
    NAR = cfg.NAR
    ksb = jnp.clip(8 * b - 4, 0, R - 16)
    for i in range(8):
        qr = 8 * b + i
        ws = jnp.clip(qr - NAR // 2, 0, R - NAR)
        for J in range(8):
            kr0 = ksb + 2 * J
            row = jnp.clip(kr0 - qr + NAR - 1, -1, _NTAB - 2) + 1
            v0 = jnp.logical_and(kr0 >= ws, kr0 < ws + NAR)
            v1 = jnp.logical_and(kr0 + 1 >= ws, kr0 + 1 < ws + NAR)
            yield i, J, row, v0, v1


def _na_fill_bias(cfg, tab_ref, bias, b):
    GW = cfg.GW
    first = lax.broadcasted_iota(jnp.int32, (GW, LANE), 1) < GW
    for i, J, row, v0, v1 in _na_tiles(cfg, b):
        ok = jnp.where(first, v0.astype(jnp.int32), v1.astype(jnp.int32))
        bias[i * GW:(i + 1) * GW, J * LANE:(J + 1) * LANE] = jnp.where(ok > 0, tab_ref[row], NEG)


def _na_specs(cfg):
    R, nb, ks = _na_geometry(cfg)
    off = _offsets(cfg)
    TQ = 8 * cfg.GW
    KP = 4 * cfg.GW
    ks4 = [k // 4 for k in ks]
    lat_blocks = cfg.T // KP

    def ks4_of(b):
        return jnp.clip(2 * b - 1, 0, R // 4 - 4)

    assert all(int(np.clip(2 * b - 1, 0, R // 4 - 4)) == ks4[b] for b in range(nb))
    assert cfg.TC == KP

    def col(nm):
        c0 = off[nm] // LANE
        q = pl.BlockSpec((TQ, LANE), lambda h, b: (b, c0 + h))
        parts = [pl.BlockSpec((KP, LANE), functools.partial(lambda h, b, t: (ks4_of(b) + t, c0 + h), t=t))
                 for t in range(4)]
        ctx = pl.BlockSpec((KP, LANE), lambda h, b: (lat_blocks, c0 + h))
        return q, parts, ctx

    return nb, TQ, KP, ks4_of, col


def na_fwd(cfg, name, P, tab):
    nb, TQ, KP, ks4_of, col = _na_specs(cfg)
    NH = cfg.NH
    scale = cfg.NDH ** -0.5
    qs, _, _ = col("nq")
    _, kparts, kctx = col("nk")
    _, vparts, vctx = col("nv")

    def body(q_ref, k0, k1, k2, k3, kc_ref, v0, v1, v2, v3, vc_ref, tab_ref, o_ref, lse_ref, bias_ref):
        _na_fill_bias(cfg, tab_ref, bias_ref, pl.program_id(1))
        q = (q_ref[...] * scale).astype(BF16)
        kl = jnp.concatenate([k0[...], k1[...], k2[...], k3[...]], axis=0).astype(BF16)
        vl = jnp.concatenate([v0[...], v1[...], v2[...], v3[...]], axis=0).astype(BF16)
        kc = kc_ref[...].astype(BF16)
        vc = vc_ref[...].astype(BF16)
        sl = dot_nt(q, kl) + bias_ref[...]
        sc = dot_nt(q, kc)
        m = jnp.maximum(jnp.max(sl, axis=-1, keepdims=True), jnp.max(sc, axis=-1, keepdims=True))
        pl_ = jnp.exp(sl - m)
        pc = jnp.exp(sc - m)
        den = jnp.sum(pl_, axis=-1, keepdims=True) + jnp.sum(pc, axis=-1, keepdims=True)
        o = dot_nn(pl_.astype(BF16), vl) + dot_nn(pc.astype(BF16), vc)
        o_ref[...] = o / den
        lse_ref[...] = m + jnp.log(den)

    return _pc(body, name=name, grid=(NH, nb),
               in_specs=[qs, *kparts, kctx, *vparts, vctx,
                         pl.BlockSpec((None, _NTAB, cfg.GW, LANE), lambda h, b: (h, 0, 0, 0))],
               out_specs=[pl.BlockSpec((TQ, LANE), lambda h, b: (b, h)),
                          pl.BlockSpec((None, TQ, 1), lambda h, b: (h, b, 0))],
               out_shape=[jax.ShapeDtypeStruct((cfg.T, NH * LANE), F32),
                          jax.ShapeDtypeStruct((NH, cfg.T, 1), F32)],
               scratch_shapes=[pltpu.VMEM((TQ, 4 * KP), F32)],
               compiler_params=_cp(("parallel", "parallel")))(P, *([P] * 5), *([P] * 5), tab)


def na_bwd(cfg, name, P, tab, o, lse, dmix, dcol0):
    nb, TQ, KP, ks4_of, col = _na_specs(cfg)
    NH, GW = cfg.NH, cfg.GW
    L = P.shape[0]
    scale = cfg.NDH ** -0.5
    qs, _, _ = col("nq")
    _, kparts, kctx = col("nk")
    _, vparts, vctx = col("nv")

    def body(q_ref, k0, k1, k2, k3, kc_ref, v0, v1, v2, v3, vc_ref, tab_ref, o_ref, lse_ref, do_ref,
             dq_ref, dk_ref, dv_ref, dtab_ref, bias_ref):
        b = pl.program_id(1)

        @pl.when(b == 0)
        def _():
            dk_ref[...] = jnp.zeros_like(dk_ref)
            dv_ref[...] = jnp.zeros_like(dv_ref)
            dtab_ref[...] = jnp.zeros_like(dtab_ref)

        _na_fill_bias(cfg, tab_ref, bias_ref, b)

        q = (q_ref[...] * scale).astype(BF16)
        kl = jnp.concatenate([k0[...], k1[...], k2[...], k3[...]], axis=0).astype(BF16)
        vl = jnp.concatenate([v0[...], v1[...], v2[...], v3[...]], axis=0).astype(BF16)
        kc = kc_ref[...].astype(BF16)
        vc = vc_ref[...].astype(BF16)
        lse = lse_ref[...]
        do = do_ref[...]
        dob = do.astype(BF16)
        p_l = jnp.exp(dot_nt(q, kl) + bias_ref[...] - lse)
        p_c = jnp.exp(dot_nt(q, kc) - lse)
        delta = jnp.sum(do * o_ref[...], axis=-1, keepdims=True)
        ds_l = p_l * (dot_nt(dob, vl) - delta)
        ds_c = p_c * (dot_nt(dob, vc) - delta)
        dslb = ds_l.astype(BF16)
        dscb = ds_c.astype(BF16)
        dq_ref[...] = ((dot_nn(dslb, kl) + dot_nn(dscb, kc)) * scale).astype(dq_ref.dtype)
        r0 = pl.multiple_of(ks4_of(b) * KP, KP)
        dk_ref[pl.ds(r0, 4 * KP), :] += dot_tn(dslb, q)
        dv_ref[pl.ds(r0, 4 * KP), :] += dot_tn(p_l.astype(BF16), dob)
        dk_ref[cfg.T:cfg.T + KP, :] += dot_tn(dscb, q)
        dv_ref[cfg.T:cfg.T + KP, :] += dot_tn(p_c.astype(BF16), dob)
        bias_ref[...] = ds_l
        for i, J, row, _, _ in _na_tiles(cfg, b):
            dtab_ref[row] += bias_ref[i * GW:(i + 1) * GW, J * LANE:(J + 1) * LANE]

    full = pl.BlockSpec((L, LANE), lambda h, b: (0, h))
    tabs = pl.BlockSpec((None, _NTAB, GW, LANE), lambda h, b: (h, 0, 0, 0))
    return _pc(body, name=name, grid=(NH, nb),
               in_specs=[qs, *kparts, kctx, *vparts, vctx, tabs,
                         pl.BlockSpec((TQ, LANE), lambda h, b: (b, h)),
                         pl.BlockSpec((None, TQ, 1), lambda h, b: (h, b, 0)),
                         pl.BlockSpec((TQ, LANE), lambda h, b: (b, dcol0 + h))],
               out_specs=[pl.BlockSpec((TQ, LANE), lambda h, b: (b, h)), full, full, tabs],
               out_shape=[jax.ShapeDtypeStruct((cfg.T, NH * LANE), BF16),
                          jax.ShapeDtypeStruct((L, NH * LANE), F32), jax.ShapeDtypeStruct((L, NH * LANE), F32),
                          jax.ShapeDtypeStruct((NH, _NTAB, GW, LANE), F32)],
               scratch_shapes=[pltpu.VMEM((TQ, 4 * KP), F32)],
               compiler_params=_cp(("parallel", "arbitrary")))(
                   P, *([P] * 5), *([P] * 5), tab, o, lse, dmix)


def na_ctx_fwd(cfg, name, P):
    off = _offsets(cfg)
    TC, NH = cfg.TC, cfg.NH
    rb = cfg.T // TC
    scale = cfg.NDH ** -0.5

    def body(q_ref, k_ref, v_ref, o_ref, lse_ref):
        q = (q_ref[...] * scale).astype(BF16)
        s = dot_nt(q, k_ref[...].astype(BF16))
        m = jnp.max(s, axis=-1, keepdims=True)
        p = jnp.exp(s - m)
        den = jnp.sum(p, axis=-1, keepdims=True)
        o_ref[...] = dot_nn(p.astype(BF16), v_ref[...].astype(BF16)) / den
        lse_ref[...] = m + jnp.log(den)

    spec = lambda nm: pl.BlockSpec((TC, LANE), functools.partial(lambda h, c0: (rb, c0 + h), c0=off[nm] // LANE))
    return _pc(body, name=name, grid=(NH,), in_specs=[spec("nq"), spec("nk"), spec("nv")],
               out_specs=[pl.BlockSpec((TC, LANE), lambda h: (0, h)), pl.BlockSpec((None, TC, 1), lambda h: (h, 0, 0))],
               out_shape=[jax.ShapeDtypeStruct((TC, NH * LANE), F32), jax.ShapeDtypeStruct((NH, TC, 1), F32)],
               compiler_params=_cp(("parallel",)))(P, P, P)


def na_ctx_bwd(cfg, name, P, o, lse, dmix, dcol0, dk_in, dv_in):
    off = _offsets(cfg)
    TC, NH = cfg.TC, cfg.NH
    rb = cfg.T // TC
    scale = cfg.NDH ** -0.5

    def body(q_ref, k_ref, v_ref, o_ref, lse_ref, do_ref, dki_ref, dvi_ref, dq_ref, dk_ref, dv_ref):
        q = (q_ref[...] * scale).astype(BF16)
        kb = k_ref[...].astype(BF16)
        vb = v_ref[...].astype(BF16)
        do = do_ref[...]
        dob = do.astype(BF16)
        p = jnp.exp(dot_nt(q, kb) - lse_ref[...])
        delta = jnp.sum(do * o_ref[...], axis=-1, keepdims=True)
        ds = (p * (dot_nt(dob, vb) - delta)).astype(BF16)
        dq_ref[...] = (dot_nn(ds, kb) * scale).astype(dq_ref.dtype)
        dk_ref[...] = (dki_ref[...] + dot_tn(ds, q)).astype(dk_ref.dtype)
        dv_ref[...] = (dvi_ref[...] + dot_tn(p.astype(BF16), dob)).astype(dv_ref.dtype)

    spec = lambda nm: pl.BlockSpec((TC, LANE), functools.partial(lambda h, c0: (rb, c0 + h), c0=off[nm] // LANE))
    hb = pl.BlockSpec((TC, LANE), lambda h: (0, h))
    ctxrow = pl.BlockSpec((TC, LANE), lambda h: (rb, h))
    shp = jax.ShapeDtypeStruct((TC, NH * LANE), BF16)
    return _pc(body, name=name, grid=(NH,),
               in_specs=[spec("nq"), spec("nk"), spec("nv"), hb, pl.BlockSpec((None, TC, 1), lambda h: (h, 0, 0)),
                         pl.BlockSpec((TC, LANE), lambda h: (rb, dcol0 + h)), ctxrow, ctxrow],
               out_specs=[hb, hb, hb], out_shape=[shp, shp, shp],
               compiler_params=_cp(("parallel",)))(P, P, P, o, lse, dmix, dk_in, dv_in)


def na_rpb_grad(cfg, name, dtab):
    NH, GW = cfg.NH, cfg.GW
    na, nd = 2 * cfg.NAR - 1, 2 * cfg.NAC - 1
    oda, odb, _ = _na_col_onehot(cfg)
    E = np.concatenate([oda.T, odb.T], axis=1)
    rows = NH * _NTAB

    def body(z_ref, e_ref, o_ref):
        zv = z_ref[...]
        hi = zv.astype(BF16)
        lo = (zv - hi.astype(F32)).astype(BF16)
        e = e_ref[...]
        o_ref[...] = dot_nn(hi, e) + dot_nn(lo, e)

    g = _pc(body, name=name, out_shape=jax.ShapeDtypeStruct((rows, 2 * LANE), F32),
            compiler_params=_cp())(dtab.reshape(rows, GW * LANE), jnp.asarray(E, BF16))
    g = g.reshape(NH, _NTAB, 2, LANE)
    return g[:, 1:1 + na, 0, :nd] + g[:, 0:na, 1, :nd]


def _seq_tiles(cfg):
    T, TC, TB = cfg.T, cfg.TC, cfg.TB
    tiles = []
    for i in range((T + TC) // TB):
        r0 = i * TB
        tiles.append((r0, r0 == 0 or r0 == T, r0 + TB == T or r0 + TB == T + TC))
    return tiles


def _shift3(ref_get, r0, TB, start, end, width):
    cur = ref_get(r0, TB)
    rowi = lax.broadcasted_iota(jnp.int32, (TB, width), 0)
    up = pltpu.roll(cur, 1, 0)
    prev = jnp.zeros((1, width), F32) if start else ref_get(r0 - 8, 8)[7:8, :]
    up = jnp.where(rowi == 0, prev, up)
    dn = pltpu.roll(cur, TB - 1, 0)
    nxt = jnp.zeros((1, width), F32) if end else ref_get(r0 + TB, 8)[0:1, :]
    dn = jnp.where(rowi == TB - 1, nxt, dn)
    return up, cur, dn


def ffn_act_fwd(cfg, name, U2, w, b):
    _, L, DFF = U2.shape
    TB = cfg.TB
    tiles = _seq_tiles(cfg)

    def body(u_ref, w_ref, b_ref, a_ref):
        def plane(p, r0, st, en):
            up, cur, dn = _shift3(lambda r, n: u_ref[p, r:r + n, :], r0, TB, st, en, LANE)
            wv = w_ref[p]
            return wv[0:1, :] * up + wv[1:2, :] * cur + wv[2:3, :] * dn + b_ref[p]

        for r0, st, en in tiles:
            val = plane(0, r0, st, en)
            gate = plane(1, r0, st, en)
            a_ref[r0:r0 + TB, :] = (_silu(gate) * val).astype(a_ref.dtype)

    return _pc(body, name=name, grid=(DFF // LANE,),
               in_specs=[pl.BlockSpec((2, L, LANE), lambda j: (0, 0, j)),
                         pl.BlockSpec((2, 8, LANE), lambda j: (0, 0, j)),
                         pl.BlockSpec((2, 1, LANE), lambda j: (0, 0, j))],
               out_specs=pl.BlockSpec((L, LANE), lambda j: (0, j)),
               out_shape=jax.ShapeDtypeStruct((L, DFF), BF16),
               compiler_params=_cp(("parallel",)))(U2, w, b)


def ffn_act_bwd(cfg, name, U2, w, b, dA):
    _, L, DFF = U2.shape
    TB = cfg.TB
    tiles = _seq_tiles(cfg)

    def body(u_ref, w_ref, b_ref, da_ref, du_ref, dw_ref, db_ref, dbuf):
        dw_ref[...] = jnp.zeros_like(dw_ref)
        db_ref[...] = jnp.zeros_like(db_ref)
        for r0, st, en in tiles:
            shifted = []
            pre = []
            for p in range(2):
                up, cur, dn = _shift3(lambda r, n: u_ref[p, r:r + n, :], r0, TB, st, en, LANE)
                wv = w_ref[p]
                shifted.append((up, cur, dn))
                pre.append(wv[0:1, :] * up + wv[1:2, :] * cur + wv[2:3, :] * dn + b_ref[p])
            val, gate = pre
            da = da_ref[r0:r0 + TB, :]
            dpre = (da * _silu(gate), da * val * _dsilu(gate))
            for p in range(2):
                dbuf[p, r0:r0 + TB, :] = dpre[p]
                for k in range(3):
                    dw_ref[p, k:k + 1, :] += jnp.sum(dpre[p] * shifted[p][k], axis=0, keepdims=True)
                db_ref[p] += jnp.sum(dpre[p], axis=0, keepdims=True)
        for r0, st, en in tiles:
            for p in range(2):
                up, cur, dn = _shift3(lambda r, n: dbuf[p, r:r + n, :], r0, TB, st, en, LANE)
                wv = w_ref[p]
                du_ref[p, r0:r0 + TB, :] = (wv[0:1, :] * dn + wv[1:2, :] * cur + wv[2:3, :] * up).astype(du_ref.dtype)

    blk = pl.BlockSpec((2, L, LANE), lambda j: (0, 0, j))
    wspec = pl.BlockSpec((2, 8, LANE), lambda j: (0, 0, j))
    bspec = pl.BlockSpec((2, 1, LANE), lambda j: (0, 0, j))
    return _pc(body, name=name, grid=(DFF // LANE,),
               in_specs=[blk, wspec, bspec, pl.BlockSpec((L, LANE), lambda j: (0, j))],
               out_specs=[blk, wspec, bspec],
               out_shape=[jax.ShapeDtypeStruct((2, L, DFF), BF16), jax.ShapeDtypeStruct((2, 8, DFF), F32),
                          jax.ShapeDtypeStruct((2, 1, DFF), F32)],
               scratch_shapes=[pltpu.VMEM((2, L, LANE), F32)],
               compiler_params=_cp(("parallel",)))(U2, w, b, dA)


def _tm(L, parts):
    assert L % parts == 0
    return L // parts


def layer_fwd(cfg, l, XS, mod, wts, small, tabs):
    L, D = XS.shape
    off = _offsets(cfg)
    DIN = off["end"]
    Win4, Wout4, Wup4, Wdn4, Wpw4 = wts["w_in"], wts["w_out"], wts["ffn_up"], wts["ffn_down"], wts["conv_pw"]
    nbi = Win4.shape[3]
    tmA = _tm(L, 4)
    sv = {}
    sv["XS"] = XS
    h1 = norm_mod_fwd(cfg, f"norm1_fwd_{l}", XS, small["norm1_g"], mod["sc1"], mod["sh1"])
    P = matmul(f"mm_in_{l}", h1, Win4, contract="nn", grid=(4, L // tmA),
               a_spec=pl.BlockSpec((tmA, D), lambda n, m: (m, 0)),
               b_spec=pl.BlockSpec((None, None, D, nbi), lambda n, m: (n, l, 0, 0)),
               out_shape=jax.ShapeDtypeStruct((L, DIN), F32),
               out_spec=pl.BlockSpec((tmA, nbi), lambda n, m: (m, n)), nk=1)
    qr, kr = rope_fwd(cfg, f"rope_fwd_{l}", P, tabs["cos"], tabs["sin"])
    o2, st = retention_fwd(cfg, f"ret_fwd_{l}", qr, kr, P, small["lam"])
    ret = ggn_fwd(cfg, f"ggn_fwd_{l}", o2, P, small["ret_gn_g"])
    ycv = glu_dwconv_fwd(cfg, f"dwconv_fwd_{l}", P, small["conv_dw_w"], small["conv_dw_b"])
    act = ln_silu_fwd(cfg, f"ln_silu_fwd_{l}", ycv, small["conv_ln_g"], small["conv_ln_b"])
    cv = mm_rowsharded(f"mm_pw_{l}", act, Wpw4, l, BF16, cfg.CW)
    bias = na_tables(cfg, f"na_tables_{l}", small["na_rpb"])
    na_l, lse = na_fwd(cfg, f"na_fwd_{l}", P, bias)
    na_c, lse_c = na_ctx_fwd(cfg, f"na_ctx_fwd_{l}", P)
    mix = jnp.concatenate([ret, cv, jnp.concatenate([na_l, na_c], axis=0).astype(BF16)], axis=1)
    Y1 = mm_rowsharded(f"mm_out_{l}", mix, Wout4, l, F32, D)
    XM = resid_fwd(cfg, f"resid1_fwd_{l}", XS, Y1, mod["g1"])
    h2 = norm_mod_fwd(cfg, f"norm2_fwd_{l}", XM, small["norm2_g"], mod["sc2"], mod["sh2"])
    nbu = Wup4.shape[3]
    tnu = nbu // 2
    U2 = matmul(f"mm_up_{l}", h2, Wup4, contract="nn", grid=(8, L // tmA),
                a_spec=pl.BlockSpec((tmA, D), lambda n, m: (m, 0)),
                b_spec=pl.BlockSpec((None, None, D, tnu), lambda n, m: (n // 2, l, 0, n % 2)),
                out_shape=jax.ShapeDtypeStruct((2, L, cfg.DFF), F32),
                out_spec=pl.BlockSpec((None, tmA, tnu), lambda n, m: (n // 4, m, n % 4)), nk=1)
    A = ffn_act_fwd(cfg, f"ffn_act_fwd_{l}", U2, small["ffn_dw_w"], small["ffn_dw_b"])
    Y2 = mm_rowsharded(f"mm_down_{l}", A, Wdn4, l, F32, D // 2)
    XO = resid_fwd(cfg, f"resid2_fwd_{l}", XM, Y2, mod["g2"])
    sv.update(h1=h1, P=P, qr=qr, kr=kr, o2=o2, st=st, ycv=ycv, act=act, bias=bias, na_l=na_l, lse=lse,
              na_c=na_c, lse_c=lse_c, mix=mix, Y1=Y1, XM=XM, h2=h2, U2=U2, A=A, Y2=Y2)
    return XO, sv


def layer_bwd(cfg, l, dXO, sv, mod, wts, small, tabs, gbuf):
    L, D = dXO.shape
    off = _offsets(cfg)
    DIN = off["end"]
    Win4, Wout4, Wup4, Wdn4, Wpw4 = wts["w_in"], wts["w_out"], wts["ffn_up"], wts["ffn_down"], wts["conv_pw"]
    tmA, tmB = _tm(L, 4), _tm(L, 8)
    depth = Win4.shape[1]
    gb, gs, dm = {}, {}, {}
    P = sv["P"]
    dY2, dm["g2"] = resid_bwd(cfg, f"resid2_bwd_{l}", dXO, sv["Y2"], mod["g2"])
    nbd = Wdn4.shape[2]
    dA = matmul(f"mm_down_da_{l}", dY2, Wdn4, contract="nt", grid=(4, L // tmA),
                a_spec=pl.BlockSpec((tmA, D), lambda j, m: (m, 0)),
                b_spec=pl.BlockSpec((None, None, nbd, D), lambda j, m: (j, l, 0, 0)),
                out_shape=jax.ShapeDtypeStruct((L, cfg.DFF), F32),
                out_spec=pl.BlockSpec((tmA, nbd), lambda j, m: (m, j)), nk=1)
    gb["ffn_down"] = matmul(f"mm_down_dw_{l}", sv["A"], dY2, contract="tn", grid=(4, L // tmB),
                            a_spec=pl.BlockSpec((tmB, nbd), lambda j, m: (m, j)),
                            b_spec=pl.BlockSpec((tmB, D), lambda j, m: (m, 0)),
                            out_shape=jax.ShapeDtypeStruct((depth, 4, nbd, D), F32), into=gbuf.get("ffn_down"),
                            out_spec=pl.BlockSpec((None, None, nbd, D), lambda j, m: (l, j, 0, 0)), nk=L // tmB)
    dU2, dfw, dfb = ffn_act_bwd(cfg, f"ffn_act_bwd_{l}", sv["U2"], small["ffn_dw_w"], small["ffn_dw_b"], dA)
    gs["ffn_dw_w"], gs["ffn_dw_b"] = dfw, dfb
    nbu = Wup4.shape[3]
    tnu = nbu // 2
    dH2 = matmul(f"mm_up_dh_{l}", dU2, Wup4, contract="nt", grid=(L // tmA, 8),
                 a_spec=pl.BlockSpec((None, tmA, tnu), lambda m, n: (n // 4, m, n % 4)),
                 b_spec=pl.BlockSpec((None, None, D, tnu), lambda m, n: (n // 2, l, 0, n % 2)),
                 out_shape=jax.ShapeDtypeStruct((L, D), F32),
                 out_spec=pl.BlockSpec((tmA, D), lambda m, n: (m, 0)), nk=8)
    gb["ffn_up"] = matmul(f"mm_up_dw_{l}", sv["h2"], dU2, contract="tn", grid=(8, L // tmB),
                          a_spec=pl.BlockSpec((tmB, D), lambda n, m: (m, 0)),
                          b_spec=pl.BlockSpec((None, tmB, tnu), lambda n, m: (n // 4, m, n % 4)),
                          out_shape=jax.ShapeDtypeStruct((depth, 4, D, nbu), F32), into=gbuf.get("ffn_up"),
                          out_spec=pl.BlockSpec((None, None, D, tnu), lambda n, m: (l, n // 2, 0, n % 2)), nk=L // tmB)
    dXM, dm["sc2"], dm["sh2"], gs["norm2_g"] = norm_mod_bwd(
        cfg, f"norm2_bwd_{l}", dH2, sv["XM"], small["norm2_g"], mod["sc2"], dXO)
    dY1, dm["g1"] = resid_bwd(cfg, f"resid1_bwd_{l}", dXM, sv["Y1"], mod["g1"])
    nbo = Wout4.shape[2]
    dmix = matmul(f"mm_out_dmix_{l}", dY1, Wout4, contract="nt", grid=(4, L // tmA),
                  a_spec=pl.BlockSpec((tmA, D), lambda j, m: (m, 0)),
                  b_spec=pl.BlockSpec((None, None, nbo, D), lambda j, m: (j, l, 0, 0)),
                  out_shape=jax.ShapeDtypeStruct((L, D), F32),
                  out_spec=pl.BlockSpec((tmA, nbo), lambda j, m: (m, j)), nk=1)
    gb["w_out"] = matmul(f"mm_out_dw_{l}", sv["mix"], dY1, contract="tn", grid=(4, L // tmB),
                         a_spec=pl.BlockSpec((tmB, nbo), lambda j, m: (m, j)),
                         b_spec=pl.BlockSpec((tmB, D), lambda j, m: (m, 0)),
                         out_shape=jax.ShapeDtypeStruct((depth, 4, nbo, D), F32), into=gbuf.get("w_out"),
                         out_spec=pl.BlockSpec((None, None, nbo, D), lambda j, m: (l, j, 0, 0)), nk=L // tmB)
    RW = cfg.RH * cfg.RDV
    do, dlg, gs["ret_gn_g"] = ggn_bwd(cfg, f"ggn_bwd_{l}", dmix, sv["o2"], P, small["ret_gn_g"], 0)
    dq2, dk2, dv2, dlam = retention_bwd(cfg, f"ret_bwd_{l}", sv["qr"], sv["kr"], P, small["lam"], sv["st"], do)
    gs["lam"] = dlam[:, :, 0, 0]
    dlq, dlk = rope_bwd(cfg, f"rope_bwd_{l}", dq2, dk2, tabs["cos"], tabs["sin"])
    dlv = cast_cols(cfg, f"ret_dv_{l}", dv2, 0, cfg.RH, cfg.RDV)
    dcv = cast_cols(cfg, f"conv_dcv_{l}", dmix, RW // LANE, cfg.CW // LANE, LANE)
    nbp = Wpw4.shape[2]
    dact = matmul(f"mm_pw_dact_{l}", dcv, Wpw4, contract="nt", grid=(4, L // tmA),
                  a_spec=pl.BlockSpec((tmA, cfg.CW), lambda j, m: (m, 0)),
                  b_spec=pl.BlockSpec((None, None, nbp, cfg.CW), lambda j, m: (j, l, 0, 0)),
                  out_shape=jax.ShapeDtypeStruct((L, cfg.CW), F32),
                  out_spec=pl.BlockSpec((tmA, nbp), lambda j, m: (m, j)), nk=1)
    gb["conv_pw"] = matmul(f"mm_pw_dw_{l}", sv["act"], dcv, contract="tn", grid=(4, L // tmB),
                           a_spec=pl.BlockSpec((tmB, nbp), lambda j, m: (m, j)),
                           b_spec=pl.BlockSpec((tmB, cfg.CW), lambda j, m: (m, 0)),
                           out_shape=jax.ShapeDtypeStruct((depth, 4, nbp, cfg.CW), F32), into=gbuf.get("conv_pw"),
                           out_spec=pl.BlockSpec((None, None, nbp, cfg.CW), lambda j, m: (l, j, 0, 0)), nk=L // tmB)
    dycv, gs["conv_ln_g"], gs["conv_ln_b"] = ln_silu_bwd(
        cfg, f"ln_silu_bwd_{l}", dact, sv["ycv"], small["conv_ln_g"], small["conv_ln_b"])
    dla, dlb, gs["conv_dw_w"], gs["conv_dw_b"] = glu_dwconv_bwd(cfg, f"dwconv_bwd_{l}", P, small["conv_dw_w"], dycv)
    nac0 = (RW + cfg.CW) // LANE
    dnq_l, dnk, dnv, dsb = na_bwd(cfg, f"na_bwd_{l}", P, sv["bias"], sv["na_l"], sv["lse"], dmix, nac0)
    dnq_c, dnk_c, dnv_c = na_ctx_bwd(cfg, f"na_ctx_bwd_{l}", P, sv["na_c"], sv["lse_c"], dmix, nac0, dnk, dnv)
    gs["na_rpb"] = na_rpb_grad(cfg, f"na_rpb_{l}", dsb)
    dnq = jnp.concatenate([dnq_l, dnq_c], axis=0)
    dnk = jnp.concatenate([dnk[:cfg.T].astype(BF16), dnk_c], axis=0)
    dnv = jnp.concatenate([dnv[:cfg.T].astype(BF16), dnv_c], axis=0)
    dP = jnp.concatenate([dlq, dlk, dlv, dlg, dla, dlb, dnq, dnk, dnv], axis=1)
    nbi = Win4.shape[3]
    dH1 = matmul(f"mm_in_dh_{l}", dP, Win4, contract="nt", grid=(L // tmA, 4),
                 a_spec=pl.BlockSpec((tmA, nbi), lambda m, n: (m, n)),
                 b_spec=pl.BlockSpec((None, None, D, nbi), lambda m, n: (n, l, 0, 0)),
                 out_shape=jax.ShapeDtypeStruct((L, D), F32),
                 out_spec=pl.BlockSpec((tmA, D), lambda m, n: (m, 0)), nk=4)
    gb["w_in"] = matmul(f"mm_in_dw_{l}", sv["h1"], dP, contract="tn", grid=(4, L // tmB),
                        a_spec=pl.BlockSpec((tmB, D), lambda n, m: (m, 0)),
                        b_spec=pl.BlockSpec((tmB, nbi), lambda n, m: (m, n)),
                        out_shape=jax.ShapeDtypeStruct((depth, 4, D, nbi), F32), into=gbuf.get("w_in"),
                        out_spec=pl.BlockSpec((None, None, D, nbi), lambda n, m: (l, n, 0, 0)), nk=L // tmB)
    dXS, dm["sc1"], dm["sh1"], gs["norm1_g"] = norm_mod_bwd(
        cfg, f"norm1_bwd_{l}", dH1, sv["XS"], small["norm1_g"], mod["sc1"], dXM)
    return dXS, gb, gs, dm


def _layer_small(cfg, l, sp):
    DFF = cfg.DFF
    fw = sp["ffn_dw_w"][l].reshape(3, 2, DFF).transpose(1, 0, 2)
    fw = jnp.concatenate([fw, jnp.zeros((2, 5, DFF), F32)], axis=1)
    cw = jnp.concatenate([sp["conv_dw_w"][l], jnp.zeros((32 - cfg.CK, cfg.CW), F32)], axis=0)
    return dict(
        norm1_g=sp["norm1_g"][l][None], norm2_g=sp["norm2_g"][l][None],
        lam=jax.nn.log_sigmoid(sp["ret_decay"][l]), ret_gn_g=sp["ret_gn_g"][l][None],
        conv_dw_w=cw, conv_dw_b=sp["conv_dw_b"][l][None], conv_ln_g=sp["conv_ln_g"][l][None],
        conv_ln_b=sp["conv_ln_b"][l][None], na_rpb=sp["na_rpb"][l],
        ffn_dw_w=fw, ffn_dw_b=sp["ffn_dw_b"][l].reshape(2, 1, DFF))


def local_step(cfg, x, ctx, tgt, mods, wts, sp):
    depth = sp["norm1_g"].shape[0]
    cos, sin = rope_tables(cfg)
    tabs = dict(cos=cos, sin=sin)
    XS = jnp.concatenate([x, ctx], axis=0)
    smalls = [_layer_small(cfg, l, sp) for l in range(depth)]
    saves = []
    for l in range(depth):
        XS, sv = layer_fwd(cfg, l, XS, mods[l], wts, smalls[l], tabs)
        saves.append(sv)
    ls, dX, dfg = final_loss(cfg, "final_loss", XS, sp["final_g"][None], tgt)
    gb, gss, dms = {}, [None] * depth, [None] * depth
    for l in reversed(range(depth)):
        dX, gb, gss[l], dms[l] = layer_bwd(cfg, l, dX, saves[l], mods[l], wts, smalls[l], tabs, gb)
    return ls[0, 0], dX[:cfg.T], gb, gss, dms, dfg[0]


MESH = pl.DeviceIdType.MESH
N_DEV = 8
N_CHIP = 4
BIG = ("w_in", "w_out", "ffn_up", "ffn_down", "conv_pw")
_ANY = pl.BlockSpec(memory_space=pl.ANY)


def _place():
    x, y, c = lax.axis_index("x"), lax.axis_index("y"), lax.axis_index("c")
    chips = [(1 - x, y), (x, 1 - y), (1 - x, 1 - y)]
    return x, y, c, chips


def allgather8(name, xs):
    m_per, n = xs.shape

    def body(x_ref, out_ref, send_sems, recv_sems, local_sem):
        x, y, c, chips = _place()
        me, sibling = (x, y, c), (x, y, 1 - c)

        def rows(px, py, pc):
            return out_ref.at[pl.ds((4 * px + 2 * py + pc) * m_per, m_per), :]

        def copy(k, block, to, src=None):
            return pltpu.make_async_remote_copy(
                src_ref=rows(*block) if src is None else src, dst_ref=rows(*block),
                send_sem=send_sems.at[k], recv_sem=recv_sems.at[k], device_id=to, device_id_type=MESH)

        mine = pltpu.make_async_copy(x_ref, rows(*me), local_sem)
        mine.start()
        first = [copy(0, me, sibling, src=x_ref)]
        first += [copy(1 + j, me, (*chip, c), src=x_ref) for j, chip in enumerate(chips)]
        for cp in first:
            cp.start()
        passed = [copy(4 + j, (*chip, c), sibling) for j, chip in enumerate(chips)]
        for j, chip in enumerate(chips):
            copy(1 + j, (*chip, c), me).wait_recv()
            passed[j].start()
        copy(0, sibling, me).wait_recv()
        for j, chip in enumerate(chips):
            copy(4 + j, (*chip, 1 - c), me).wait_recv()
        for cp in first + passed:
            cp.wait_send()
        mine.wait()

    return _pc(body, name=name, out_shape=jax.ShapeDtypeStruct((N_DEV * m_per, n), xs.dtype),
               in_specs=[pl.BlockSpec(memory_space=pltpu.VMEM)], out_specs=pl.BlockSpec(memory_space=pltpu.VMEM),
               scratch_shapes=[pltpu.SemaphoreType.DMA((7,)), pltpu.SemaphoreType.DMA((7,)), pltpu.SemaphoreType.DMA],
               compiler_params=pltpu.CompilerParams(vmem_limit_bytes=VMEM_LIMIT))(xs)


def allgather_weights(name, bufs):
    n = len(bufs)

    def body(*refs):
        outs = refs[n:2 * n]
        send_sems, recv_sems = refs[2 * n:]
        x, y, c, chips = _place()
        j0 = 2 * x + y
        sibling = (x, y, 1 - c)

        def blk(a, chip, layer):
            return outs[a].at[2 * chip[0] + chip[1], layer]

        sent = []
        for a in range(n):
            for t, chip in enumerate(chips):
                cp = pltpu.make_async_remote_copy(
                    src_ref=outs[a].at[j0, c], dst_ref=outs[a].at[j0, c], send_sem=send_sems.at[3 * a + t],
                    recv_sem=recv_sems.at[3 * a + t], device_id=(*chip, c), device_id_type=MESH)
                cp.start()
                sent.append(cp)
        for a in range(n):
            for t, chip in enumerate(chips):
                k = 3 * a + t
                pltpu.make_async_remote_copy(
                    src_ref=blk(a, chip, c), dst_ref=blk(a, chip, c), send_sem=send_sems.at[k],
                    recv_sem=recv_sems.at[k], device_id=(x, y, c), device_id_type=MESH).wait_recv()
                fw = pltpu.make_async_remote_copy(
                    src_ref=blk(a, chip, c), dst_ref=blk(a, chip, c), send_sem=send_sems.at[3 * n + k],
                    recv_sem=recv_sems.at[3 * n + k], device_id=sibling, device_id_type=MESH)
                fw.start()
                sent.append(fw)
        for a in range(n):
            for t, chip in enumerate(chips):
                k = 3 * n + 3 * a + t
                pltpu.make_async_remote_copy(
                    src_ref=blk(a, chip, 1 - c), dst_ref=blk(a, chip, 1 - c), send_sem=send_sems.at[k],
                    recv_sem=recv_sems.at[k], device_id=(x, y, c), device_id_type=MESH).wait_recv()
        for cp in sent:
            cp.wait_send()

    return _pc(body, name=name, out_shape=[jax.ShapeDtypeStruct(b.shape, b.dtype) for b in bufs],
               in_specs=[_ANY] * n, out_specs=[_ANY] * n, input_output_aliases={a: a for a in range(n)},
               scratch_shapes=[pltpu.SemaphoreType.DMA((6 * n,)), pltpu.SemaphoreType.DMA((6 * n,))])(*bufs)


def exchange_halves(name, grads):
    n = len(grads)

    def body(*refs):
        ins, outs = refs[:n], refs[n:2 * n]
        send_sems, recv_sems = refs[2 * n:]
        x, y, c, _ = _place()
        cps = [pltpu.make_async_remote_copy(
            src_ref=ins[a].at[1 - c], dst_ref=outs[a], send_sem=send_sems.at[a], recv_sem=recv_sems.at[a],
            device_id=(x, y, 1 - c), device_id_type=MESH) for a in range(n)]
        for cp in cps:
            cp.start()
        for cp in cps:
            cp.wait()

    return _pc(body, name=name, out_shape=[jax.ShapeDtypeStruct(g.shape[1:], g.dtype) for g in grads],
               in_specs=[_ANY] * n, out_specs=[_ANY] * n,
               scratch_shapes=[pltpu.SemaphoreType.DMA((n,)), pltpu.SemaphoreType.DMA((n,))])(*grads)


def scatter_slices(name, parts, lands):
    n = len(parts)

    def body(*refs):
        ins, outs = refs[:n], refs[2 * n:3 * n]
        send_sems, recv_sems = refs[3 * n:]
        x, y, c, chips = _place()
        j0 = 2 * x + y
        cps = []
        for a in range(n):
            for t, chip in enumerate(chips):
                jt = 2 * chip[0] + chip[1]
                cps.append(pltpu.make_async_remote_copy(
                    src_ref=ins[a].at[jt], dst_ref=outs[a].at[j0], send_sem=send_sems.at[3 * a + t],
                    recv_sem=recv_sems.at[3 * a + t], device_id=(*chip, c), device_id_type=MESH))
        for cp in cps:
            cp.start()
        k = 0
        for a in range(n):
            for t, chip in enumerate(chips):
                jt = 2 * chip[0] + chip[1]
                pltpu.make_async_remote_copy(
                    src_ref=outs[a].at[jt], dst_ref=outs[a].at[jt], send_sem=send_sems.at[k],
                    recv_sem=recv_sems.at[k], device_id=(x, y, c), device_id_type=MESH).wait_recv()
                k += 1
        for cp in cps:
            cp.wait_send()

    return _pc(body, name=name, out_shape=[jax.ShapeDtypeStruct(p.shape, p.dtype) for p in lands],
               in_specs=[_ANY] * (2 * n), out_specs=[_ANY] * n,
               input_output_aliases={n + a: a for a in range(n)},
               scratch_shapes=[pltpu.SemaphoreType.DMA((3 * n,)), pltpu.SemaphoreType.DMA((3 * n,))])(*parts, *lands)


def share_layers(name, bufs):
    n = len(bufs)

    def body(*refs):
        outs = refs[n:2 * n]
        send_sems, recv_sems = refs[2 * n:]
        x, y, c, _ = _place()
        cps = [pltpu.make_async_remote_copy(
            src_ref=outs[a].at[c], dst_ref=outs[a].at[c], send_sem=send_sems.at[a], recv_sem=recv_sems.at[a],
            device_id=(x, y, 1 - c), device_id_type=MESH) for a in range(n)]
        for cp in cps:
            cp.start()
        for a in range(n):
            pltpu.make_async_remote_copy(
                src_ref=outs[a].at[1 - c], dst_ref=outs[a].at[1 - c], send_sem=send_sems.at[a],
                recv_sem=recv_sems.at[a], device_id=(x, y, c), device_id_type=MESH).wait_recv()
        for cp in cps:
            cp.wait_send()

    return _pc(body, name=name, out_shape=[jax.ShapeDtypeStruct(b.shape, b.dtype) for b in bufs],
               in_specs=[_ANY] * n, out_specs=[_ANY] * n, input_output_aliases={a: a for a in range(n)},
               scratch_shapes=[pltpu.SemaphoreType.DMA((n,)), pltpu.SemaphoreType.DMA((n,))])(*bufs)


def _row_tile(R, C, nbytes=1 << 20):
    t = 8
    while t * 2 <= R and R % (t * 2) == 0 and t * 2 * C * 4 <= nbytes:
        t *= 2
    assert R % t == 0
    return t


def to_bf16_block(name, w, chip_arr):
    _, R, C = w.shape
    tr = _row_tile(R, C)

    def body(j_ref, w_ref, o_ref):
        o_ref[...] = w_ref[...].astype(o_ref.dtype)

    gs = pltpu.PrefetchScalarGridSpec(
        num_scalar_prefetch=1, grid=(2, R // tr),
        in_specs=[pl.BlockSpec((None, tr, C), lambda l, i, j_ref: (l, i, 0))],
        out_specs=pl.BlockSpec((None, None, tr, C), lambda l, i, j_ref: (j_ref[0], l, i, 0)))
    return _pc(body, name=name, grid_spec=gs, out_shape=jax.ShapeDtypeStruct((N_CHIP,) + w.shape, BF16),
               compiler_params=_cp(("parallel", "parallel")))(chip_arr, w)


def add_own_layer(name, g, ra, c_arr):
    _, _, R, C = g.shape
    tr = _row_tile(R, C)

    def body(c_ref, g_ref, r_ref, o_ref):
        o_ref[...] = (g_ref[...] + r_ref[...]).astype(o_ref.dtype)

    gs = pltpu.PrefetchScalarGridSpec(
        num_scalar_prefetch=1, grid=(4, R // tr),
        in_specs=[pl.BlockSpec((None, None, tr, C), lambda j, i, c_ref: (c_ref[0], j, i, 0)),
                  pl.BlockSpec((None, tr, C), lambda j, i, c_ref: (j, i, 0))],
        out_specs=pl.BlockSpec((None, tr, C), lambda j, i, c_ref: (j, i, 0)))
    return _pc(body, name=name, grid_spec=gs, out_shape=jax.ShapeDtypeStruct(ra.shape, BF16),
               compiler_params=_cp(("parallel", "parallel")))(c_arr, g, ra)


def own_row(name, part, chip_arr):
    _, R, C = part.shape
    tr = _row_tile(R, C)

    def body(j_ref, p_ref, o_ref):
        o_ref[...] = p_ref[...]

    gs = pltpu.PrefetchScalarGridSpec(
        num_scalar_prefetch=1, grid=(R // tr,),
        in_specs=[pl.BlockSpec((None, tr, C), lambda i, j_ref: (j_ref[0], i, 0))],
        out_specs=pl.BlockSpec((None, tr, C), lambda i, j_ref: (j_ref[0], i, 0)))
    return _pc(body, name=name, grid_spec=gs, out_shape=jax.ShapeDtypeStruct(part.shape, part.dtype),
               compiler_params=_cp(("parallel",)))(chip_arr, part)


def sum_leading(name, g, plane=None):
    n, R, C = g.shape
    tr = _row_tile(R, C, nbytes=(1 << 21) // n)

    def body(*refs):
        g_ref, o_ref = refs[-2:]
        acc = g_ref[0].astype(F32)
        for j in range(1, n):
            acc = acc + g_ref[j].astype(F32)
        o_ref[...] = acc

    if plane is None:
        return _pc(body, name=name, grid=(R // tr,), in_specs=[pl.BlockSpec((n, tr, C), lambda i: (0, i, 0))],
                   out_specs=pl.BlockSpec((tr, C), lambda i: (i, 0)), out_shape=jax.ShapeDtypeStruct((R, C), F32),
                   compiler_params=_cp(("parallel",)))(g)
    count, idx = plane
    gs = pltpu.PrefetchScalarGridSpec(
        num_scalar_prefetch=1, grid=(R // tr,),
        in_specs=[pl.BlockSpec((n, tr, C), lambda i, p_ref: (0, i, 0))],
        out_specs=pl.BlockSpec((None, tr, C), lambda i, p_ref: (p_ref[0], i, 0)))
    return _pc(body, name=name, grid_spec=gs, out_shape=jax.ShapeDtypeStruct((count, R, C), F32),
               compiler_params=_cp(("parallel",)))(idx, g)


def adamw(name, w, g, m, v):
    R, C = w.shape
    tr = _row_tile(R, C)

    def body(w_ref, g_ref, m_ref, v_ref, d_ref, mo_ref, vo_ref):
        gv = g_ref[...]
        mn = ADAM_B1 * m_ref[...] + (1.0 - ADAM_B1) * gv
        vn = ADAM_B2 * v_ref[...] + (1.0 - ADAM_B2) * (gv * gv)
        m_hat = mn / (1.0 - ADAM_B1 ** ADAM_STEP)
        v_hat = vn / (1.0 - ADAM_B2 ** ADAM_STEP)
        d_ref[...] = -ADAM_LR * (m_hat / (jnp.sqrt(v_hat) + ADAM_EPS) + ADAM_WD * w_ref[...])
        mo_ref[...] = mn
        vo_ref[...] = vn

    spec = pl.BlockSpec((tr, C), lambda i: (i, 0))
    shp = jax.ShapeDtypeStruct((R, C), F32)
    return _pc(body, name=name, grid=(R // tr,), in_specs=[spec] * 4, out_specs=[spec] * 3,
               out_shape=[shp] * 3, compiler_params=_cp(("parallel",)))(w, g, m, v)


_ADA_TN = 512


def adaln_fwd(name, cond, w, b):
    _, D, N = w.shape
    tn = min(_ADA_TN, N)

    def body(c_ref, w_ref, b_ref, o_ref):
        s = _silu(c_ref[...]).astype(BF16)
        o_ref[...] = dot_nn(s, w_ref[...].astype(BF16)) + b_ref[...]

    return _pc(body, name=name, grid=(2, N // tn),
               in_specs=[pl.BlockSpec((16, D), lambda l, n: (0, 0)),
                         pl.BlockSpec((None, D, tn), lambda l, n: (l, 0, n)),
                         pl.BlockSpec((None, 1, tn), lambda l, n: (l, 0, n))],
               out_specs=pl.BlockSpec((None, 16, tn), lambda l, n: (l, 0, n)),
               out_shape=jax.ShapeDtypeStruct((2, 16, N), F32),
               compiler_params=_cp(("parallel", "parallel")))(cond, w, b)


def adaln_bwd(name, cond, w, dm):
    _, D, N = w.shape
    tn = min(_ADA_TN, N)

    def body(c_ref, w_ref, dm_ref, gw_ref, ds_ref):
        first = jnp.logical_and(pl.program_id(0) == 0, pl.program_id(1) == 0)
        s = _silu(c_ref[...]).astype(BF16)
        dmb = dm_ref[...].astype(BF16)
        gw_ref[...] = dot_tn(s, dmb)
        p = dot_nt(dmb, w_ref[...].astype(BF16))

        @pl.when(first)
        def _():
            ds_ref[...] = p

        @pl.when(jnp.logical_not(first))
        def _():
            ds_ref[...] += p

    return _pc(body, name=name, grid=(2, N // tn),
               in_specs=[pl.BlockSpec((16, D), lambda l, n: (0, 0)),
                         pl.BlockSpec((None, D, tn), lambda l, n: (l, 0, n)),
                         pl.BlockSpec((None, 16, tn), lambda l, n: (l, 0, n))],
               out_specs=[pl.BlockSpec((None, D, tn), lambda l, n: (l, 0, n)),
                          pl.BlockSpec((16, D), lambda l, n: (0, 0))],
               out_shape=[jax.ShapeDtypeStruct((2, D, N), F32), jax.ShapeDtypeStruct((16, D), F32)],
               compiler_params=_cp(("arbitrary", "arbitrary")))(cond, w, dm)


def cctx_grad(name, parts, c_ctx):
    def body(p_ref, c_ref, o_ref):
        acc = p_ref[0]
        for j in range(1, N_CHIP):
            acc = acc + p_ref[j]
        o_ref[...] = acc * _dsilu(c_ref[...])

    return _pc(body, name=name, out_shape=jax.ShapeDtypeStruct(c_ctx.shape, F32))(parts, c_ctx)


def _pack(arrs, rows_mult=8):
    flat = jnp.concatenate([a.reshape(-1) for a in arrs])
    n = flat.shape[0]
    unit = rows_mult * LANE
    tot = -(-n // unit) * unit
    return jnp.concatenate([flat, jnp.zeros((tot - n,), F32)]).reshape(tot // LANE, LANE)


def _unpack(flat, shapes):
    out, o = [], 0
    for s in shapes:
        n = int(np.prod(s))
        out.append(flat[o:o + n].reshape(s))
        o += n
    return out


MOD_NAMES = ("sh1", "sc1", "g1", "sh2", "sc2", "g2")


def kernel(x, c, ctx, c_ctx, w_ada, b_ada, norm1_g, w_in, ret_decay, ret_gn_g, conv_dw_w, conv_dw_b, conv_ln_g, conv_ln_b, conv_pw, na_rpb, w_out, norm2_g, ffn_up, ffn_dw_w, ffn_dw_b, ffn_down, final_g, loss_target, m_c_ctx, m_w_ada, m_b_ada, m_norm1_g, m_w_in, m_ret_decay, m_ret_gn_g, m_conv_dw_w, m_conv_dw_b, m_conv_ln_g, m_conv_ln_b, m_conv_pw, m_na_rpb, m_w_out, m_norm2_g, m_ffn_up, m_ffn_dw_w, m_ffn_dw_b, m_ffn_down, m_final_g, v_c_ctx, v_w_ada, v_b_ada, v_norm1_g, v_w_in, v_ret_decay, v_ret_gn_g, v_conv_dw_w, v_conv_dw_b, v_conv_ln_g, v_conv_ln_b, v_conv_pw, v_na_rpb, v_w_out, v_norm2_g, v_ffn_up, v_ffn_dw_w, v_ffn_dw_b, v_ffn_down, v_final_g):
    cfg = make_cfg(D=x.shape[2], T=x.shape[1], TC=ctx.shape[1], RH=ret_decay.shape[2], CW=conv_dw_b.shape[1],
                   NH=na_rpb.shape[1], DFF=ffn_dw_b.shape[1] // 2)
    D, T = cfg.D, cfg.T
    W = dict(c_ctx=c_ctx, w_ada=w_ada, b_ada=b_ada, norm1_g=norm1_g, w_in=w_in, ret_decay=ret_decay, ret_gn_g=ret_gn_g,
             conv_dw_w=conv_dw_w, conv_dw_b=conv_dw_b, conv_ln_g=conv_ln_g, conv_ln_b=conv_ln_b, conv_pw=conv_pw,
             na_rpb=na_rpb, w_out=w_out, norm2_g=norm2_g, ffn_up=ffn_up, ffn_dw_w=ffn_dw_w, ffn_dw_b=ffn_dw_b,
             ffn_down=ffn_down, final_g=final_g)
    Mo = dict(c_ctx=m_c_ctx, w_ada=m_w_ada, b_ada=m_b_ada, norm1_g=m_norm1_g, w_in=m_w_in, ret_decay=m_ret_decay,
              ret_gn_g=m_ret_gn_g, conv_dw_w=m_conv_dw_w, conv_dw_b=m_conv_dw_b, conv_ln_g=m_conv_ln_g,
              conv_ln_b=m_conv_ln_b, conv_pw=m_conv_pw, na_rpb=m_na_rpb, w_out=m_w_out, norm2_g=m_norm2_g,
              ffn_up=m_ffn_up, ffn_dw_w=m_ffn_dw_w, ffn_dw_b=m_ffn_dw_b, ffn_down=m_ffn_down, final_g=m_final_g)
    Vo = dict(c_ctx=v_c_ctx, w_ada=v_w_ada, b_ada=v_b_ada, norm1_g=v_norm1_g, w_in=v_w_in, ret_decay=v_ret_decay,
              ret_gn_g=v_ret_gn_g, conv_dw_w=v_conv_dw_w, conv_dw_b=v_conv_dw_b, conv_ln_g=v_conv_ln_g,
              conv_ln_b=v_conv_ln_b, conv_pw=v_conv_pw, na_rpb=v_na_rpb, w_out=v_w_out, norm2_g=v_norm2_g,
              ffn_up=v_ffn_up, ffn_dw_w=v_ffn_dw_w, ffn_dw_b=v_ffn_dw_b, ffn_down=v_ffn_down, final_g=v_final_g)
    order = list(W)
    xi, yi, ci = lax.axis_index("x"), lax.axis_index("y"), lax.axis_index("c")
    chip = 2 * xi + yi
    dev = 4 * xi + 2 * yi + ci
    NA = w_ada.shape[2]
    ncw, nfw = conv_dw_w.shape[2], ffn_dw_w.shape[2]

    g_in = allgather8("ag_small_in", _pack([c[0], conv_dw_w, ffn_dw_w])).reshape(N_DEV, -1)
    c8 = g_in[:, :D]
    by_chip = g_in[0::2, D:]
    cw_parts, fw_parts = [], []
    for j in range(N_CHIP):
        a, b = _unpack(by_chip[j], [conv_dw_w.shape, ffn_dw_w.shape])
        cw_parts.append(a)
        fw_parts.append(b)
    conv_dw_w_full = jnp.concatenate(cw_parts, axis=2)
    ffn_dw_w_full = jnp.concatenate(fw_parts, axis=2)
    cond = jnp.concatenate([c8, c_ctx[None], jnp.zeros((16 - N_DEV - 1, D), F32)], axis=0)

    b_sh = lax.dynamic_slice(b_ada, (0, chip * NA), (2, NA)).reshape(2, 1, NA)
    m_sh = adaln_fwd("adaln_fwd", cond, w_ada, b_sh)
    m_all = allgather8("ag_mod", m_sh.reshape(2 * 16, NA)).reshape(N_DEV, 2, 16, NA)[0::2]
    m_all = m_all.transpose(1, 2, 0, 3).reshape(2, 16, N_CHIP * NA)
    mods = []
    for l in range(2):
        lat = lax.dynamic_slice(m_all[l], (dev, 0), (1, N_CHIP * NA))[0]
        cx = m_all[l, N_DEV]
        mods.append({nm: jnp.stack([lat[k * D:(k + 1) * D], cx[k * D:(k + 1) * D]], 0)[:, None, :]
                     for k, nm in enumerate(MOD_NAMES)})

    c_arr = jnp.reshape(ci, (1,)).astype(jnp.int32)
    chip_arr = jnp.reshape(chip, (1,)).astype(jnp.int32)
    wts = dict(zip(BIG, allgather_weights(
        "ag_weights", [to_bf16_block(f"to_bf16_{nm}", W[nm], chip_arr) for nm in BIG])))

    sp = dict(norm1_g=norm1_g, norm2_g=norm2_g, ret_decay=ret_decay, ret_gn_g=ret_gn_g, conv_dw_w=conv_dw_w_full,
              conv_dw_b=conv_dw_b, conv_ln_g=conv_ln_g, conv_ln_b=conv_ln_b, na_rpb=na_rpb, ffn_dw_w=ffn_dw_w_full,
              ffn_dw_b=ffn_dw_b, final_g=final_g)
    loss_l, gx, gb, gss, dms, dfg = local_step(cfg, x[0], ctx[0], loss_target[0], mods, wts, sp)
    loss = lax.psum(loss_l, ("x", "y", "c"))

    dmseg = jnp.stack([jnp.stack([jnp.concatenate([dms[l][nm][r, 0] for nm in MOD_NAMES]) for r in range(2)])
                       for l in range(2)])
    gsm = dict(
        norm1_g=jnp.stack([gss[l]["norm1_g"][0] for l in range(2)]),
        ret_decay=jnp.stack([gss[l]["lam"] * jax.nn.sigmoid(-ret_decay[l]) for l in range(2)]),
        ret_gn_g=jnp.stack([gss[l]["ret_gn_g"][0] for l in range(2)]),
        conv_dw_w=jnp.stack([gss[l]["conv_dw_w"][:cfg.CK] for l in range(2)]),
        conv_dw_b=jnp.stack([gss[l]["conv_dw_b"][0] for l in range(2)]),
        conv_ln_g=jnp.stack([gss[l]["conv_ln_g"][0] for l in range(2)]),
        conv_ln_b=jnp.stack([gss[l]["conv_ln_b"][0] for l in range(2)]),
        na_rpb=jnp.stack([gss[l]["na_rpb"] for l in range(2)]),
        norm2_g=jnp.stack([gss[l]["norm2_g"][0] for l in range(2)]),
        ffn_dw_w=jnp.stack([gss[l]["ffn_dw_w"][:, :3].transpose(1, 0, 2).reshape(3, 2 * cfg.DFF) for l in range(2)]),
        ffn_dw_b=jnp.stack([gss[l]["ffn_dw_b"].reshape(-1) for l in range(2)]),
        final_g=dfg)
    snames = list(gsm)
    sshapes = [dmseg.shape] + [gsm[nm].shape for nm in snames]
    packed = _pack([dmseg] + [gsm[nm] for nm in snames])
    g_all = allgather8("ag_small_grads", packed).reshape(N_DEV, packed.shape[0], LANE)
    summed = sum_leading("sum_small_grads", g_all).reshape(-1)
    dm_sum, *gsum = _unpack(summed, sshapes)
    gfull = dict(zip(snames, gsum))
    ndm = int(np.prod(dmseg.shape))
    dm_all = g_all.reshape(N_DEV, -1)[:, :ndm].reshape(N_DEV, 2, 2, 6 * D)
    gfull["b_ada"] = sum_leading("sum_b_ada", dm_all.transpose(0, 2, 1, 3).reshape(2 * N_DEV, 2 * 6 * D // LANE, LANE)
                                 ).reshape(2, 6 * D)

    dm16 = jnp.concatenate([dm_all[:, :, 0].transpose(1, 0, 2), dm_sum[:, 1][:, None],
                            jnp.zeros((2, 16 - N_DEV - 1, 6 * D), F32)], axis=1)
    dm16 = lax.dynamic_slice(dm16, (0, 0, chip * NA), (2, 16, NA))
    gfull["w_ada"], ds16 = adaln_bwd("adaln_bwd", cond, w_ada, dm16)
    ds_all = allgather8("ag_dsilu", ds16[8:16]).reshape(N_DEV, 8, D)[0::2, 0:1]
    gfull["c_ctx"] = cctx_grad("cctx_grad", ds_all, c_ctx[None])[0]
    gfull["conv_dw_w"] = lax.dynamic_slice(gfull["conv_dw_w"], (0, 0, chip * ncw), (2, cfg.CK, ncw))
    gfull["ffn_dw_w"] = lax.dynamic_slice(gfull["ffn_dw_w"], (0, 0, chip * nfw), (2, 3, nfw))

    from_sib = exchange_halves("rs_exchange", [gb[nm] for nm in BIG])
    part = [add_own_layer(f"rs_add_{nm}", gb[nm], r, c_arr) for nm, r in zip(BIG, from_sib)]
    lands = [own_row(f"rs_own_{nm}", p, chip_arr) for nm, p in zip(BIG, part)]
    landed = scatter_slices("rs_scatter", part, lands)
    done = [sum_leading(f"rs_sum_{nm}", p, plane=(2, c_arr)) for nm, p in zip(BIG, landed)]
    for nm, gfin in zip(BIG, share_layers("rs_share", done)):
        gfull[nm] = gfin

    delta, new_m, new_v = {}, {}, {}
    bigs = ("w_ada",) + BIG
    for nm in bigs:
        shp = W[nm].shape
        v2 = lambda a: a.reshape(-1, shp[-1])
        d_, m_, v_ = adamw(f"adamw_{nm}", v2(W[nm]), v2(gfull[nm]), v2(Mo[nm]), v2(Vo[nm]))
        delta[nm], new_m[nm], new_v[nm] = d_.reshape(shp), m_.reshape(shp), v_.reshape(shp)
    smalls = [nm for nm in order if nm not in bigs]
    shapes = [W[nm].shape for nm in smalls]
    d_, m_, v_ = adamw("adamw_small", _pack([W[nm] for nm in smalls]), _pack([gfull[nm] for nm in smalls]),
                       _pack([Mo[nm] for nm in smalls]), _pack([Vo[nm] for nm in smalls]))
    for nm, a, b, e in zip(smalls, _unpack(d_.reshape(-1), shapes), _unpack(m_.reshape(-1), shapes),
                           _unpack(v_.reshape(-1), shapes)):
        delta[nm], new_m[nm], new_v[nm] = a, b, e
    return (loss, gx[None], *[gfull[nm] for nm in order], *[delta[nm] for nm in order],
            *[new_m[nm] for nm in order], *[new_v[nm] for nm in order])
```

```python
import collections
import functools

import numpy as np
import jax
import jax.numpy as jnp
from jax import lax
from jax.experimental import pallas as pl
from jax.experimental.pallas import tpu as pltpu

F32 = jnp.float32
BF16 = jnp.bfloat16
EPS = 1e-6
ROPE_BASE = 10000.0
NEG = -1e30
LANE = 128
VMEM_LIMIT = 56 * 1024 * 1024

ADAM_LR, ADAM_B1, ADAM_B2, ADAM_EPS, ADAM_WD, ADAM_STEP = 0.001, 0.9, 0.999, 1e-08, 0.01, 10

Cfg = collections.namedtuple(
    "Cfg", "D T TC GW RH RDK RDV CW CK NH NDH NAR NAC DFF TB")


def make_cfg(D=2048, T=4096, TC=256, RH=4, CW=512, NH=4, DFF=5632):
    return Cfg(D=D, T=T, TC=TC, GW=64, RH=RH, RDK=128, RDV=256, CW=CW, CK=31, NH=NH, NDH=128,
               NAR=8, NAC=16, DFF=DFF, TB=256)


def _offsets(cfg):
    sizes = [cfg.RH * cfg.RDK, cfg.RH * cfg.RDK, cfg.RH * cfg.RDV, cfg.RH * cfg.RDV, cfg.CW, cfg.CW,
             cfg.NH * cfg.NDH, cfg.NH * cfg.NDH, cfg.NH * cfg.NDH]
    offs = [0]
    for s in sizes:
        offs.append(offs[-1] + s)
    return dict(zip(["lq", "lk", "lv", "lg", "la", "lb", "nq", "nk", "nv", "end"], offs))


def _pc(body, **kw):
    return pl.pallas_call(body, **kw)


def _cp(sem=None):
    return pltpu.CompilerParams(dimension_semantics=sem, vmem_limit_bytes=VMEM_LIMIT)


def _dot(a, b, ca, cb):
    return lax.dot_general(a, b, (((ca,), (cb,)), ((), ())), preferred_element_type=F32)


def dot_nn(a, b):
    return _dot(a, b, 1, 0)


def dot_nt(a, b):
    return _dot(a, b, 1, 1)


def dot_tn(a, b):
    return _dot(a, b, 0, 0)


def _sigmoid(x):
    return 1.0 / (1.0 + jnp.exp(-x))


def _silu(x):
    return x * _sigmoid(x)


def _dsilu(x):
    s = _sigmoid(x)
    return s * (1.0 + x * (1.0 - s))


def matmul(name, a, b, *, contract, grid, a_spec, b_spec, out_shape, out_spec, nk, into=None):
    dot = {"nn": dot_nn, "nt": dot_nt, "tn": dot_tn}[contract]
    direct = nk > 1 and out_shape.dtype == F32
    kax = len(grid) - 1

    def body(a_ref, b_ref, *rest):
        o_ref, *scr = rest[1:] if into is not None else rest
        p = dot(a_ref[...].astype(BF16), b_ref[...].astype(BF16))
        if nk == 1:
            o_ref[...] = p.astype(o_ref.dtype)
            return
        acc = o_ref if direct else scr[0]
        k = pl.program_id(kax)

        @pl.when(k == 0)
        def _():
            acc[...] = p

        @pl.when(k > 0)
        def _():
            acc[...] += p

        if not direct:
            @pl.when(k == nk - 1)
            def _():
                o_ref[...] = acc[...].astype(o_ref.dtype)

    scratch = []
    if nk > 1 and not direct:
        blk = [s for s in out_spec.block_shape if s is not None]
        scratch = [pltpu.VMEM(tuple(blk), F32)]
    sem = ("parallel",) * kax + (("arbitrary",) if nk > 1 else ("parallel",))
    in_specs, args, alias = [a_spec, b_spec], (a, b), {}
    if into is not None:
        in_specs, args, alias = in_specs + [pl.BlockSpec(memory_space=pl.ANY)], (a, b, into), {2: 0}
    return _pc(body, name=name, grid=grid, in_specs=in_specs, out_specs=out_spec, out_shape=out_shape,
               scratch_shapes=scratch, input_output_aliases=alias, compiler_params=_cp(sem))(*args)


_WG_ROWS = 1024


def wgrad(cfg, name, a, dc, a_spec, dc_spec, out_shape, out_spec, ntiles, into):
    T, TC = cfg.T, cfg.TC
    tml = min(_WG_ROWS, T)
    nl = T // tml

    def body(al_ref, ac_ref, dl_ref, dcx_ref, *rest):
        o_ref = rest[-1]
        m = pl.program_id(1)

        @pl.when(m == 0)
        def _():
            o_ref[...] = dot_tn(al_ref[...], dl_ref[...])

        @pl.when(jnp.logical_and(m > 0, m < nl))
        def _():
            o_ref[...] += dot_tn(al_ref[...], dl_ref[...])

        @pl.when(m == nl)
        def _():
            o_ref[...] += dot_tn(ac_ref[...], dcx_ref[...])

    lat = lambda m: jnp.minimum(m, nl - 1)
    ctx = lambda m: T // TC
    in_specs = [a_spec(tml, lat), a_spec(TC, ctx), dc_spec(tml, lat), dc_spec(TC, ctx)]
    args, alias = (a, a, dc, dc), {}
    if into is not None:
        in_specs, args, alias = in_specs + [pl.BlockSpec(memory_space=pl.ANY)], args + (into,), {4: 0}
    return _pc(body, name=name, grid=(ntiles, nl + 1), in_specs=in_specs, out_specs=out_spec, out_shape=out_shape,
               input_output_aliases=alias, compiler_params=_cp(("parallel", "arbitrary")))(*args)


def mm_rowsharded(name, a, w4, l, out_dtype, tn):
    L, K = a.shape
    nch, _, Kb, N = w4.shape
    tm = 256

    def body(a_ref, w_ref, o_ref):
        acc = dot_nn(a_ref[:, 0:Kb], w_ref[0])
        for j in range(1, nch):
            acc += dot_nn(a_ref[:, j * Kb:(j + 1) * Kb], w_ref[j])
        o_ref[...] = acc.astype(o_ref.dtype)

    return _pc(body, name=name, grid=(N // tn, L // tm),
               in_specs=[pl.BlockSpec((tm, K), lambda n, m: (m, 0)),
                         pl.BlockSpec((nch, None, Kb, tn), lambda n, m: (0, l, 0, n))],
               out_specs=pl.BlockSpec((tm, tn), lambda n, m: (m, n)),
               out_shape=jax.ShapeDtypeStruct((L, N), out_dtype),
               compiler_params=_cp(("parallel", "parallel")))(a, w4)


def _region(cfg):
    nlat = cfg.T // cfg.TB
    return lambda i: jnp.minimum(i // nlat, 1)


def norm_mod_fwd(cfg, name, x, ng, sc, sh):
    L, D = x.shape
    TB = cfg.TB
    reg = _region(cfg)

    def body(x_ref, ng_ref, sc_ref, sh_ref, h_ref):
        xv = x_ref[...]
        r = lax.rsqrt(jnp.mean(xv * xv, axis=-1, keepdims=True) + EPS)
        n = xv * r * ng_ref[...]
        h_ref[...] = (n * (1.0 + sc_ref[...]) + sh_ref[...]).astype(h_ref.dtype)

    row = pl.BlockSpec((TB, D), lambda i: (i, 0))
    vec = pl.BlockSpec((1, D), lambda i: (0, 0))
    rvec = pl.BlockSpec((None, 1, D), lambda i: (reg(i), 0, 0))
    return _pc(body, name=name, grid=(L // TB,), in_specs=[row, vec, rvec, rvec], out_specs=row,
               out_shape=jax.ShapeDtypeStruct((L, D), BF16), compiler_params=_cp(("parallel",)))(x, ng, sc, sh)


def norm_mod_bwd(cfg, name, dh, x, ng, sc, dx_in):
    L, D = x.shape
    TB = cfg.TB
    nlat = cfg.T // TB
    reg = _region(cfg)

    def body(dh_ref, x_ref, ng_ref, sc_ref, dxi_ref, dx_ref, dsc_ref, dsh_ref, dng_ref):
        i = pl.program_id(0)
        xv = x_ref[...]
        r = lax.rsqrt(jnp.mean(xv * xv, axis=-1, keepdims=True) + EPS)
        xh = xv * r
        g = ng_ref[...]
        n = xh * g
        dh = dh_ref[...]
        dn = dh * (1.0 + sc_ref[...])
        dxh = dn * g
        dx = r * (dxh - xh * jnp.mean(dxh * xh, axis=-1, keepdims=True))
        dx_ref[...] = dxi_ref[...] + dx
        s_sh = jnp.sum(dh, axis=0, keepdims=True)
        s_sc = jnp.sum(dh * n, axis=0, keepdims=True)
        s_ng = jnp.sum(dn * xh, axis=0, keepdims=True)
        first = jnp.logical_or(i == 0, i == nlat)

        @pl.when(first)
        def _():
            dsh_ref[...] = s_sh
            dsc_ref[...] = s_sc

        @pl.when(jnp.logical_not(first))
        def _():
            dsh_ref[...] += s_sh
            dsc_ref[...] += s_sc

        @pl.when(i == 0)
        def _():
            dng_ref[...] = s_ng

        @pl.when(i > 0)
        def _():
            dng_ref[...] += s_ng

    row = pl.BlockSpec((TB, D), lambda i: (i, 0))
    vec = pl.BlockSpec((1, D), lambda i: (0, 0))
    rvec = pl.BlockSpec((None, 1, D), lambda i: (reg(i), 0, 0))
    return _pc(body, name=name, grid=(L // TB,), in_specs=[row, row, vec, rvec, row],
               out_specs=[row, rvec, rvec, vec],
               out_shape=[jax.ShapeDtypeStruct((L, D), F32), jax.ShapeDtypeStruct((2, 1, D), F32),
                          jax.ShapeDtypeStruct((2, 1, D), F32), jax.ShapeDtypeStruct((1, D), F32)],
               compiler_params=_cp(("arbitrary",)))(dh, x, ng, sc, dx_in)


def resid_fwd(cfg, name, x, y, g):
    L, D = x.shape
    TB = cfg.TB
    reg = _region(cfg)

    def body(x_ref, y_ref, g_ref, o_ref):
        o_ref[...] = x_ref[...] + g_ref[...] * y_ref[...]

    row = pl.BlockSpec((TB, D), lambda i: (i, 0))
    rvec = pl.BlockSpec((None, 1, D), lambda i: (reg(i), 0, 0))
    return _pc(body, name=name, grid=(L // TB,), in_specs=[row, row, rvec], out_specs=row,
               out_shape=jax.ShapeDtypeStruct((L, D), F32), compiler_params=_cp(("parallel",)))(x, y, g)


def resid_bwd(cfg, name, dxo, y, g):
    L, D = y.shape
    TB = cfg.TB
    nlat = cfg.T // TB
    reg = _region(cfg)

    def body(d_ref, y_ref, g_ref, dy_ref, dg_ref):
        i = pl.program_id(0)
        d = d_ref[...]
        dy_ref[...] = (d * g_ref[...]).astype(dy_ref.dtype)
        s = jnp.sum(d * y_ref[...], axis=0, keepdims=True)
        first = jnp.logical_or(i == 0, i == nlat)

        @pl.when(first)
        def _():
            dg_ref[...] = s

        @pl.when(jnp.logical_not(first))
        def _():
            dg_ref[...] += s

    row = pl.BlockSpec((TB, D), lambda i: (i, 0))
    rvec = pl.BlockSpec((None, 1, D), lambda i: (reg(i), 0, 0))
    return _pc(body, name=name, grid=(L // TB,), in_specs=[row, row, rvec], out_specs=[row, rvec],
               out_shape=[jax.ShapeDtypeStruct((L, D), BF16), jax.ShapeDtypeStruct((2, 1, D), F32)],
               compiler_params=_cp(("arbitrary",)))(dxo, y, g)


def final_loss(cfg, name, x, fg, tgt):
    L, D = x.shape
    TB = cfg.TB
    nlat = cfg.T // TB

    def body(x_ref, fg_ref, t_ref, ls_ref, dx_ref, dg_ref):
        i = pl.program_id(0)

        @pl.when(i == 0)
        def _():
            ls_ref[...] = jnp.zeros_like(ls_ref)
            dg_ref[...] = jnp.zeros_like(dg_ref)

        @pl.when(i < nlat)
        def _():
            xv = x_ref[...]
            r = lax.rsqrt(jnp.mean(xv * xv, axis=-1, keepdims=True) + EPS)
            xh = xv * r
            g = fg_ref[...]
            e = xh * g - t_ref[...]
            ls_ref[...] += 0.5 * jnp.sum(e * e) / D
            dy = e / D
            dg_ref[...] += jnp.sum(dy * xh, axis=0, keepdims=True)
            dxh = dy * g
            dx_ref[...] = r * (dxh - xh * jnp.mean(dxh * xh, axis=-1, keepdims=True))

        @pl.when(i >= nlat)
        def _():
            dx_ref[...] = jnp.zeros_like(dx_ref)

    row = pl.BlockSpec((TB, D), lambda i: (i, 0))
    trow = pl.BlockSpec((TB, D), lambda i: (jnp.minimum(i, nlat - 1), 0))
    vec = pl.BlockSpec((1, D), lambda i: (0, 0))
    return _pc(body, name=name, grid=(L // TB,), in_specs=[row, vec, trow],
               out_specs=[pl.BlockSpec((1, LANE), lambda i: (0, 0)), row, vec],
               out_shape=[jax.ShapeDtypeStruct((1, LANE), F32), jax.ShapeDtypeStruct((L, D), F32),
                          jax.ShapeDtypeStruct((1, D), F32)],
               compiler_params=_cp(("arbitrary",)))(x, fg, tgt)


def rope_tables(cfg):
    half = cfg.RDK // 2
    nf = half // 2
    pos = np.arange(cfg.T)
    row = (pos // cfg.GW).astype(np.float32)
    col = (pos % cfg.GW).astype(np.float32)
    inv = jnp.asarray(ROPE_BASE, F32) ** (-jnp.arange(nf, dtype=F32) / nf)
    ar = jnp.asarray(row)[:, None] * inv[None, :]
    ac = jnp.asarray(col)[:, None] * inv[None, :]
    cos = jnp.concatenate([jnp.cos(ar), jnp.cos(ar), jnp.cos(ac), jnp.cos(ac)], axis=1)
    sin = jnp.concatenate([-jnp.sin(ar), jnp.sin(ar), -jnp.sin(ac), jnp.sin(ac)], axis=1)
    cos = jnp.concatenate([cos, jnp.ones((cfg.TC, cfg.RDK), F32)], axis=0)
    sin = jnp.concatenate([sin, jnp.zeros((cfg.TC, cfg.RDK), F32)], axis=0)
    return cos, sin


def _swap32(t):
    lane = lax.broadcasted_iota(jnp.int32, t.shape, 1)
    return jnp.where((lane % 64) < 32, pltpu.roll(t, 96, 1), pltpu.roll(t, 32, 1))


def rope_fwd(cfg, name, P, cos, sin):
    L = P.shape[0]
    TB = cfg.TB
    off = _offsets(cfg)
    cq, ck = off["lq"] // LANE, off["lk"] // LANE
    scale = cfg.RDK ** -0.5

    def body(q_ref, k_ref, c_ref, s_ref, qo_ref, ko_ref):
        c = c_ref[...]
        s = s_ref[...]
        q = q_ref[...]
        k = k_ref[...]
        qo_ref[...] = (q * c + _swap32(q) * s) * scale
        ko_ref[...] = k * c + _swap32(k) * s

    tab = pl.BlockSpec((TB, LANE), lambda i, h: (i, 0))
    out = pl.BlockSpec((TB, LANE), lambda i, h: (i, h))
    shp = jax.ShapeDtypeStruct((L, cfg.RH * cfg.RDK), F32)
    return _pc(body, name=name, grid=(L // TB, cfg.RH),
               in_specs=[pl.BlockSpec((TB, LANE), lambda i, h: (i, cq + h)),
                         pl.BlockSpec((TB, LANE), lambda i, h: (i, ck + h)), tab, tab],
               out_specs=[out, out], out_shape=[shp, shp],
               compiler_params=_cp(("parallel", "parallel")))(P, P, cos, sin)


def rope_bwd(cfg, name, dq2, dk2, cos, sin):
    L, W = dq2[0].shape
    TB = cfg.TB
    scale = cfg.RDK ** -0.5

    def body(dqf_ref, dqb_ref, dkf_ref, dkb_ref, c_ref, s_ref, qo_ref, ko_ref):
        c = c_ref[...]
        s = s_ref[...]
        dq = dqf_ref[...] + dqb_ref[...]
        dk = dkf_ref[...] + dkb_ref[...]
        qo_ref[...] = ((dq * c - _swap32(dq) * s) * scale).astype(qo_ref.dtype)
        ko_ref[...] = (dk * c - _swap32(dk) * s).astype(ko_ref.dtype)

    tab = pl.BlockSpec((TB, LANE), lambda i, h: (i, 0))
    blk = pl.BlockSpec((TB, LANE), lambda i, h: (i, h))
    shp = jax.ShapeDtypeStruct((L, W), BF16)
    return _pc(body, name=name, grid=(L // TB, cfg.RH), in_specs=[blk, blk, blk, blk, tab, tab],
               out_specs=[blk, blk], out_shape=[shp, shp],
               compiler_params=_cp(("parallel", "parallel")))(*dq2, *dk2, cos, sin)


def _ret_chunk_map(cfg):
    C = cfg.RDK
    n = (cfg.T + cfg.TC) // C
    nlat, nctx = cfg.T // C, cfg.TC // C

    def chunk(d, s):
        if d == 0:
            return jnp.where(s < nctx, nlat + s, s - nctx)
        return n - 1 - s

    return n, chunk


def _ret_decay_terms(d, lam, C):
    ii = lax.broadcasted_iota(jnp.int32, (C, C), 0)
    jj = lax.broadcasted_iota(jnp.int32, (C, C), 1)
    diff = (ii - jj if d == 0 else jj - ii).astype(F32)
    dpos = jnp.maximum(diff, 0.0)
    Dm = jnp.where(diff >= 0, jnp.exp(dpos * lam), 0.0)
    ic = lax.broadcasted_iota(jnp.int32, (C, 1), 0).astype(F32)
    cxi = ic + 1.0 if d == 0 else C - ic
    cze = C - 1.0 - ic if d == 0 else ic
    xi = jnp.exp(cxi * lam)
    ze = jnp.exp(cze * lam)
    g = jnp.exp(jnp.full((1, 1), C, F32) * lam)
    return dpos, Dm, cxi, cze, xi, ze, g


def retention_fwd(cfg, name, qr, kr, P, lam):
    L = P.shape[0]
    C, DV, RH = cfg.RDK, cfg.RDV, cfg.RH
    n, chunk = _ret_chunk_map(cfg)

    def body(lam_ref, qf_ref, qb_ref, kf_ref, kb_ref, vf_ref, vb_ref, of_ref, ob_ref, st_ref, S):
        s = pl.program_id(0)

        @pl.when(s == 0)
        def _():
            S[...] = jnp.zeros_like(S)

        for d, (q_ref, k_ref, v_ref, o_ref) in enumerate(((qf_ref, kf_ref, vf_ref, of_ref),
                                                          (qb_ref, kb_ref, vb_ref, ob_ref))):
            for h in range(RH):
                _, Dm, _, _, xi, ze, g = _ret_decay_terms(d, lam_ref[d, h], C)
                k = k_ref[:, h * C:(h + 1) * C]
                qb = q_ref[:, h * C:(h + 1) * C].astype(BF16)
                kb = k.astype(BF16)
                vb = v_ref[:, h * DV:(h + 1) * DV].astype(BF16)
                Sv = S[d, h]
                st_ref[d, h] = Sv
                A = dot_nt(qb, kb) * Dm
                o_ref[:, h * DV:(h + 1) * DV] = dot_nn(A.astype(BF16), vb) + dot_nn(qb, Sv.astype(BF16)) * xi
                S[d, h] = Sv * g + dot_tn((k * ze).astype(BF16), vb)

    def spec(w, col, d):
        return pl.BlockSpec((C, w), lambda s: (chunk(d, s), col))

    W, WV = RH * C, RH * DV
    return _pc(body, name=name, grid=(n,),
               in_specs=[pl.BlockSpec(memory_space=pltpu.SMEM), spec(W, 0, 0), spec(W, 0, 1), spec(W, 0, 0),
                         spec(W, 0, 1), spec(WV, 1, 0), spec(WV, 1, 1)],
               out_specs=[spec(WV, 0, 0), spec(WV, 0, 1),
                          pl.BlockSpec((2, RH, None, C, DV), lambda s: (0, 0, s, 0, 0))],
               out_shape=[jax.ShapeDtypeStruct((L, WV), F32), jax.ShapeDtypeStruct((L, WV), F32),
                          jax.ShapeDtypeStruct((2, RH, n, C, DV), F32)],
               scratch_shapes=[pltpu.VMEM((2, RH, C, DV), F32)],
               compiler_params=_cp(("arbitrary",)))(lam, qr, qr, kr, kr, P, P)


def retention_bwd(cfg, name, qr, kr, P, lam, st, do):
    L = P.shape[0]
    C, DV, RH = cfg.RDK, cfg.RDV, cfg.RH
    n, chunk = _ret_chunk_map(cfg)

    def body(lam_ref, qf_ref, qb_ref, kf_ref, kb_ref, vf_ref, vb_ref, st_ref, dof_ref, dob_ref,
             dqf_ref, dqb_ref, dkf_ref, dkb_ref, dvf_ref, dvb_ref, dl_ref, dS):
        si = pl.program_id(0)

        @pl.when(si == 0)
        def _():
            dS[...] = jnp.zeros_like(dS)
            dl_ref[...] = jnp.zeros_like(dl_ref)

        dirs = ((qf_ref, kf_ref, vf_ref, dof_ref, dqf_ref, dkf_ref, dvf_ref),
                (qb_ref, kb_ref, vb_ref, dob_ref, dqb_ref, dkb_ref, dvb_ref))
        for d, (q_ref, k_ref, v_ref, do_ref, dq_ref, dk_ref, dv_ref) in enumerate(dirs):
            for h in range(RH):
                dpos, Dm, cxi, cze, xi, ze, g = _ret_decay_terms(d, lam_ref[d, h], C)
                hk = slice(h * C, (h + 1) * C)
                hv = slice(h * DV, (h + 1) * DV)
                k = k_ref[:, hk]
                do = do_ref[:, hv]
                qb = q_ref[:, hk].astype(BF16)
                kb = k.astype(BF16)
                vb = v_ref[:, hv].astype(BF16)
                dob = do.astype(BF16)
                Sn = st_ref[d, h]
                Snb = Sn.astype(BF16)
                dSn = dS[d, h]
                dSb = dSn.astype(BF16)
                A = dot_nt(qb, kb) * Dm
                dA = dot_nt(dob, vb)
                dQK = (dA * Dm).astype(BF16)
                kzb = (k * ze).astype(BF16)
                dv_ref[:, hv] = dot_tn(A.astype(BF16), dob) + dot_nn(kzb, dSb)
                dkz = dot_nt(vb, dSb)
                doxb = (do * xi).astype(BF16)
                dq_ref[:, hk] = dot_nn(dQK, kb) + dot_nt(doxb, Snb)
                dk_ref[:, hk] = dot_tn(dQK, qb) + dkz * ze
                QS = dot_nn(qb, Snb)
                t = (jnp.sum(dA * A * dpos) + jnp.sum(do * QS * (cxi * xi)) + jnp.sum(k * dkz * (cze * ze)))
                t4 = jnp.sum(dSn * Sn, axis=0, keepdims=True)
                t4 = jnp.sum(t4 * (g * C), axis=1, keepdims=True)
                dl_ref[d, h] += t + t4
                dS[d, h] = g * dSn + dot_tn(qb, doxb)

    def spec(w, col, d):
        return pl.BlockSpec((C, w), lambda si: (chunk(d, n - 1 - si), col))

    W, WV = RH * C, RH * DV
    return _pc(body, name=name, grid=(n,),
               in_specs=[pl.BlockSpec(memory_space=pltpu.SMEM), spec(W, 0, 0), spec(W, 0, 1), spec(W, 0, 0),
                         spec(W, 0, 1), spec(WV, 1, 0), spec(WV, 1, 1),
                         pl.BlockSpec((2, RH, None, C, DV), lambda si: (0, 0, n - 1 - si, 0, 0)),
                         spec(WV, 0, 0), spec(WV, 0, 1)],
               out_specs=[spec(W, 0, 0), spec(W, 0, 1), spec(W, 0, 0), spec(W, 0, 1), spec(WV, 0, 0), spec(WV, 0, 1),
                          pl.BlockSpec((2, RH, 8, LANE), lambda si: (0, 0, 0, 0))],
               out_shape=[jax.ShapeDtypeStruct((L, W), F32)] * 4 + [jax.ShapeDtypeStruct((L, WV), F32)] * 2
               + [jax.ShapeDtypeStruct((2, RH, 8, LANE), F32)],
               scratch_shapes=[pltpu.VMEM((2, RH, C, DV), F32)],
               compiler_params=_cp(("arbitrary",)))(lam, qr, qr, kr, kr, P, P, st, do, do)


def add_cast(cfg, name, a, b):
    L, W = a.shape
    TB = cfg.TB

    def body(a_ref, b_ref, o_ref):
        o_ref[...] = (a_ref[...] + b_ref[...]).astype(o_ref.dtype)

    spec = pl.BlockSpec((TB, W), lambda i: (i, 0))
    return _pc(body, name=name, grid=(L // TB,), in_specs=[spec, spec], out_specs=spec,
               out_shape=jax.ShapeDtypeStruct((L, W), BF16), compiler_params=_cp(("parallel",)))(a, b)


def ggn_fwd(cfg, name, o2, P, gn_g):
    L = P.shape[0]
    TB, DV, RH = cfg.TB, cfg.RDV, cfg.RH
    gc0 = _offsets(cfg)["lg"] // DV

    def body(of_ref, ob_ref, gate_ref, g_ref, out_ref):
        o = of_ref[...] + ob_ref[...]
        mu = jnp.mean(o, axis=-1, keepdims=True)
        xc = o - mu
        var = jnp.mean(xc * xc, axis=-1, keepdims=True)
        y = xc * lax.rsqrt(var + EPS) * g_ref[...]
        out_ref[...] = (y * _silu(gate_ref[...])).astype(out_ref.dtype)

    blk = pl.BlockSpec((TB, DV), lambda i, h: (i, h))
    return _pc(body, name=name, grid=(L // TB, RH),
               in_specs=[blk, blk, pl.BlockSpec((TB, DV), lambda i, h: (i, gc0 + h)),
                         pl.BlockSpec((1, DV), lambda i, h: (0, h))],
               out_specs=blk, out_shape=jax.ShapeDtypeStruct((L, RH * DV), BF16),
               compiler_params=_cp(("parallel", "parallel")))(*o2, P, gn_g)


def ggn_bwd(cfg, name, dout, o2, P, gn_g, col0):
    L = P.shape[0]
    TB, DV, RH = cfg.TB, cfg.RDV, cfg.RH
    gc0 = _offsets(cfg)["lg"] // DV

    def body(d_ref, of_ref, ob_ref, gate_ref, g_ref, do_ref, dgate_ref, dg_ref):
        i = pl.program_id(1)
        o = of_ref[...] + ob_ref[...]
        mu = jnp.mean(o, axis=-1, keepdims=True)
        xc = o - mu
        var = jnp.mean(xc * xc, axis=-1, keepdims=True)
        r = lax.rsqrt(var + EPS)
        y = xc * r
        g = g_ref[...]
        gate = gate_ref[...]
        d = d_ref[...]
        dgate_ref[...] = (d * (y * g) * _dsilu(gate)).astype(dgate_ref.dtype)
        dyg = d * _silu(gate)
        s = jnp.sum(dyg * y, axis=0, keepdims=True)

        @pl.when(i == 0)
        def _():
            dg_ref[...] = s

        @pl.when(i > 0)
        def _():
            dg_ref[...] += s

        dy = dyg * g
        do_ref[...] = r * (dy - jnp.mean(dy, axis=-1, keepdims=True)
                           - y * jnp.mean(dy * y, axis=-1, keepdims=True))

    blk = pl.BlockSpec((TB, DV), lambda h, i: (i, h))
    return _pc(body, name=name, grid=(RH, L // TB),
               in_specs=[pl.BlockSpec((TB, DV), lambda h, i: (i, col0 + h)), blk, blk,
                         pl.BlockSpec((TB, DV), lambda h, i: (i, gc0 + h)),
                         pl.BlockSpec((1, DV), lambda h, i: (0, h))],
               out_specs=[blk, blk, pl.BlockSpec((1, DV), lambda h, i: (0, h))],
               out_shape=[jax.ShapeDtypeStruct((L, RH * DV), F32), jax.ShapeDtypeStruct((L, RH * DV), BF16),
                          jax.ShapeDtypeStruct((1, RH * DV), F32)],
               compiler_params=_cp(("parallel", "arbitrary")))(dout, *o2, P, gn_g)


def cast_cols(cfg, name, src, col0, ncols, width):
    L = src.shape[0]
    TB = cfg.TB

    def body(s_ref, o_ref):
        o_ref[...] = s_ref[...].astype(o_ref.dtype)

    spec = pl.BlockSpec((TB, width), lambda i, j: (i, col0 + j))
    return _pc(body, name=name, grid=(L // TB, ncols), in_specs=[spec],
               out_specs=pl.BlockSpec((TB, width), lambda i, j: (i, j)),
               out_shape=jax.ShapeDtypeStruct((L, ncols * width), BF16),
               compiler_params=_cp(("parallel", "parallel")))(src)


_CPAD = 16


def _conv_windows(cfg):
    T, TC, TB = cfg.T, cfg.TC, cfg.TB
    assert TC % TB == 0 and T % TB == 0 and cfg.CK // 2 < _CPAD
    return T // TB, [(T + j * TB, T + _CPAD + j * TB) for j in range(TC // TB)]


def _fill_padded(cfg, pb, get):
    T, TC, TB = cfg.T, cfg.TC, cfg.TB
    z = jnp.zeros((_CPAD, LANE), F32)
    pb[0:_CPAD, :] = z
    pb[_CPAD + T:2 * _CPAD + T, :] = z
    pb[2 * _CPAD + T + TC:3 * _CPAD + T + TC, :] = z

    def fill(i, c):
        r0 = pl.multiple_of(i * TB, TB)
        pb[pl.ds(r0 + _CPAD, TB), :] = get(r0)
        return c

    lax.fori_loop(0, T // TB, fill, 0)
    for j in range(TC // TB):
        pb[2 * _CPAD + T + j * TB:2 * _CPAD + T + (j + 1) * TB, :] = get(T + j * TB)


def _taps(win, TB):
    W = TB + 2 * _CPAD
    return lambda k: pltpu.roll(win, W - (k + 1), 0)[0:TB, :]


def glu_dwconv_fwd(cfg, name, P, w, b):
    L = P.shape[0]
    T, TC, TB, K = cfg.T, cfg.TC, cfg.TB, cfg.CK
    off = _offsets(cfg)
    ca, cb = off["la"] // LANE, off["lb"] // LANE
    nlat, ctx_tiles = _conv_windows(cfg)
    PBL = 3 * _CPAD + T + TC

    def body(a_ref, b_ref, w_ref, bias_ref, y_ref, pb):
        _fill_padded(cfg, pb, lambda r0: a_ref[pl.ds(r0, TB), :] * _sigmoid(b_ref[pl.ds(r0, TB), :]))
        wv = w_ref[...]
        bias = bias_ref[...]

        def tile(win):
            tap = _taps(win, TB)
            acc = jnp.zeros((TB, LANE), F32) + bias
            for k in range(K):
                acc = acc + wv[k:k + 1, :] * tap(k)
            return acc

        def lat(i, c):
            r0 = pl.multiple_of(i * TB, TB)
            y_ref[pl.ds(r0, TB), :] = tile(pb[pl.ds(r0, TB + 2 * _CPAD), :])
            return c

        lax.fori_loop(0, nlat, lat, 0)
        for r0, w0 in ctx_tiles:
            y_ref[r0:r0 + TB, :] = tile(pb[w0:w0 + TB + 2 * _CPAD, :])

    return _pc(body, name=name, grid=(cfg.CW // LANE,),
               in_specs=[pl.BlockSpec((L, LANE), lambda j: (0, ca + j)),
                         pl.BlockSpec((L, LANE), lambda j: (0, cb + j)),
                         pl.BlockSpec((32, LANE), lambda j: (0, j)),
                         pl.BlockSpec((1, LANE), lambda j: (0, j))],
               out_specs=pl.BlockSpec((L, LANE), lambda j: (0, j)),
               out_shape=jax.ShapeDtypeStruct((L, cfg.CW), F32),
               scratch_shapes=[pltpu.VMEM((PBL, LANE), F32)],
               compiler_params=_cp(("parallel",)))(P, P, w, b)


def glu_dwconv_bwd(cfg, name, P, w, dy):
    L = P.shape[0]
    T, TC, TB, K = cfg.T, cfg.TC, cfg.TB, cfg.CK
    off = _offsets(cfg)
    ca, cb = off["la"] // LANE, off["lb"] // LANE
    nlat, ctx_tiles = _conv_windows(cfg)
    PBL = 3 * _CPAD + T + TC

    def body(a_ref, b_ref, w_ref, dy_ref, da_ref, db_ref, dw_ref, dbias_ref, pbu, pbd):
        _fill_padded(cfg, pbu, lambda r0: a_ref[pl.ds(r0, TB), :] * _sigmoid(b_ref[pl.ds(r0, TB), :]))
        _fill_padded(cfg, pbd, lambda r0: dy_ref[pl.ds(r0, TB), :])
        wv = w_ref[...]
        dw_ref[...] = jnp.zeros_like(dw_ref)
        dbias_ref[...] = jnp.zeros_like(dbias_ref)

        def tile(r0, winu, wind):
            tapu = _taps(winu, TB)
            tapd = _taps(wind, TB)
            dyt = dy_ref[pl.ds(r0, TB), :]
            du = jnp.zeros((TB, LANE), F32)
            for k in range(K):
                du = du + wv[k:k + 1, :] * tapd(K - 1 - k)
                dw_ref[k:k + 1, :] += jnp.sum(dyt * tapu(k), axis=0, keepdims=True)
            dbias_ref[...] += jnp.sum(dyt, axis=0, keepdims=True)
            a = a_ref[pl.ds(r0, TB), :]
            sg = _sigmoid(b_ref[pl.ds(r0, TB), :])
            da_ref[pl.ds(r0, TB), :] = (du * sg).astype(da_ref.dtype)
            db_ref[pl.ds(r0, TB), :] = (du * a * sg * (1.0 - sg)).astype(db_ref.dtype)

        def lat(i, c):
            r0 = pl.multiple_of(i * TB, TB)
            tile(r0, pbu[pl.ds(r0, TB + 2 * _CPAD), :], pbd[pl.ds(r0, TB + 2 * _CPAD), :])
            return c

        lax.fori_loop(0, nlat, lat, 0)
        for r0, w0 in ctx_tiles:
            tile(r0, pbu[w0:w0 + TB + 2 * _CPAD, :], pbd[w0:w0 + TB + 2 * _CPAD, :])

    col = pl.BlockSpec((L, LANE), lambda j: (0, j))
    return _pc(body, name=name, grid=(cfg.CW // LANE,),
               in_specs=[pl.BlockSpec((L, LANE), lambda j: (0, ca + j)),
                         pl.BlockSpec((L, LANE), lambda j: (0, cb + j)),
                         pl.BlockSpec((32, LANE), lambda j: (0, j)), col],
               out_specs=[col, col, pl.BlockSpec((32, LANE), lambda j: (0, j)),
                          pl.BlockSpec((1, LANE), lambda j: (0, j))],
               out_shape=[jax.ShapeDtypeStruct((L, cfg.CW), BF16), jax.ShapeDtypeStruct((L, cfg.CW), BF16),
                          jax.ShapeDtypeStruct((32, cfg.CW), F32), jax.ShapeDtypeStruct((1, cfg.CW), F32)],
               scratch_shapes=[pltpu.VMEM((PBL, LANE), F32), pltpu.VMEM((PBL, LANE), F32)],
               compiler_params=_cp(("parallel",)))(P, P, w, dy)


def ln_silu_fwd(cfg, name, y, g, b):
    L, W = y.shape
    TB = cfg.TB

    def body(y_ref, g_ref, b_ref, o_ref):
        yv = y_ref[...]
        mu = jnp.mean(yv, axis=-1, keepdims=True)
        xc = yv - mu
        var = jnp.mean(xc * xc, axis=-1, keepdims=True)
        z = xc * lax.rsqrt(var + EPS) * g_ref[...] + b_ref[...]
        o_ref[...] = _silu(z).astype(o_ref.dtype)

    row = pl.BlockSpec((TB, W), lambda i: (i, 0))
    vec = pl.BlockSpec((1, W), lambda i: (0, 0))
    return _pc(body, name=name, grid=(L // TB,), in_specs=[row, vec, vec], out_specs=row,
               out_shape=jax.ShapeDtypeStruct((L, W), BF16), compiler_params=_cp(("parallel",)))(y, g, b)


def ln_silu_bwd(cfg, name, dact, y, g, b):
    L, W = y.shape
    TB = cfg.TB

    def body(d_ref, y_ref, g_ref, b_ref, dy_ref, dg_ref, db_ref):
        i = pl.program_id(0)
        yv = y_ref[...]
        mu = jnp.mean(yv, axis=-1, keepdims=True)
        xc = yv - mu
        var = jnp.mean(xc * xc, axis=-1, keepdims=True)
        r = lax.rsqrt(var + EPS)
        yh = xc * r
        g = g_ref[...]
        z = yh * g + b_ref[...]
        dz = d_ref[...] * _dsilu(z)
        sg = jnp.sum(dz * yh, axis=0, keepdims=True)
        sb = jnp.sum(dz, axis=0, keepdims=True)

        @pl.when(i == 0)
        def _():
            dg_ref[...] = sg
            db_ref[...] = sb

        @pl.when(i > 0)
        def _():
            dg_ref[...] += sg
            db_ref[...] += sb

        dh = dz * g
        dy_ref[...] = r * (dh - jnp.mean(dh, axis=-1, keepdims=True)
                           - yh * jnp.mean(dh * yh, axis=-1, keepdims=True))

    row = pl.BlockSpec((TB, W), lambda i: (i, 0))
    vec = pl.BlockSpec((1, W), lambda i: (0, 0))
    return _pc(body, name=name, grid=(L // TB,), in_specs=[row, row, vec, vec], out_specs=[row, vec, vec],
               out_shape=[jax.ShapeDtypeStruct((L, W), F32), jax.ShapeDtypeStruct((1, W), F32),
                          jax.ShapeDtypeStruct((1, W), F32)],
               compiler_params=_cp(("arbitrary",)))(dact, y, g, b)


def _na_geometry(cfg):
    R = cfg.T // cfg.GW
    nb = R // cfg.NAR
    assert nb >= 3 and cfg.GW == 64 and cfg.NAR == 8
    ks = [int(np.clip(8 * b - 4, 0, R - 16)) for b in range(nb)]
    return R, nb, ks


_NTAB = 18


def _split3(x):
    hi = x.astype(BF16)
    r = x - hi.astype(F32)
    mid = r.astype(BF16)
    lo = (r - mid.astype(F32)).astype(BF16)
    return hi, mid, lo


def _na_col_onehot(cfg):
    GW, NAC = cfg.GW, cfg.NAC
    qc = np.arange(GW)[:, None]
    kc = np.arange(GW)[None, :]
    cs = np.clip(qc - NAC // 2, 0, GW - NAC)
    vcol = (kc >= cs) & (kc < cs + NAC)
    dd = np.clip(kc - qc + NAC - 1, 0, 2 * NAC - 2)
    oh = (np.arange(LANE)[:, None, None] == dd[None]).astype(np.float32)
    z = np.zeros_like(oh)
    oda = np.concatenate([oh, z], axis=2).reshape(LANE, GW * LANE)
    odb = np.concatenate([z, oh], axis=2).reshape(LANE, GW * LANE)
    cm = np.where(np.concatenate([vcol, vcol], axis=1), 0.0, NEG).astype(np.float32).reshape(1, GW * LANE)
    return oda, odb, cm


def na_tables(cfg, name, rpb):
    NH, GW = cfg.NH, cfg.GW
    na = rpb.shape[1]
    oda, odb, cm = _na_col_onehot(cfg)
    rp = jnp.zeros((NH, _NTAB + 1, LANE), F32).at[:, 1:1 + na, :rpb.shape[2]].set(rpb.astype(F32))
    r0 = rp[:, :_NTAB].reshape(NH * _NTAB, LANE)
    r1 = rp[:, 1:].reshape(NH * _NTAB, LANE)
    a = np.arange(_NTAB) - 1
    rm0 = np.where((a >= 0) & (a < na), 0.0, NEG).astype(np.float32)
    rm1 = np.where((a + 1 >= 0) & (a + 1 < na), 0.0, NEG).astype(np.float32)
    half = (np.arange(GW * LANE) % LANE >= GW)[None, :]
    rmask = np.where(half, np.tile(rm1, NH)[:, None], np.tile(rm0, NH)[:, None]).astype(np.float32)
    tn = 2048
    rows = NH * _NTAB

    def body(r0_ref, r1_ref, a_ref, b_ref, cm_ref, rm_ref, o_ref):
        acc = cm_ref[...] + rm_ref[...]
        for t in _split3(r0_ref[...]):
            acc = acc + dot_nn(t, a_ref[...])
        for t in _split3(r1_ref[...]):
            acc = acc + dot_nn(t, b_ref[...])
        o_ref[...] = acc

    rs = pl.BlockSpec((rows, LANE), lambda n: (0, 0))
    out = _pc(body, name=name, grid=(GW * LANE // tn,),
              in_specs=[rs, rs, pl.BlockSpec((LANE, tn), lambda n: (0, n)), pl.BlockSpec((LANE, tn), lambda n: (0, n)),
                        pl.BlockSpec((1, tn), lambda n: (0, n)), pl.BlockSpec((rows, tn), lambda n: (0, n))],
              out_specs=pl.BlockSpec((rows, tn), lambda n: (0, n)),
              out_shape=jax.ShapeDtypeStruct((rows, GW * LANE), F32),
              compiler_params=_cp(("parallel",)))(r0, r1, jnp.asarray(oda, BF16), jnp.asarray(odb, BF16),
                                                  jnp.asarray(cm), jnp.asarray(rmask))
    return out.reshape(NH, _NTAB, GW, LANE)


def _na_tiles(cfg, b):
    R, nb, _ = _na_geometry(cfg)
    NAR = cfg.NAR
    ksb = jnp.clip(8 * b - 4, 0, R - 16)
    for i in range(8):
        qr = 8 * b + i
        ws = jnp.clip(qr - NAR // 2, 0, R - NAR)
        for J in range(8):
            kr0 = ksb + 2 * J
            row = jnp.clip(kr0 - qr + NAR - 1, -1, _NTAB - 2) + 1
            v0 = jnp.logical_and(kr0 >= ws, kr0 < ws + NAR)
            v1 = jnp.logical_and(kr0 + 1 >= ws, kr0 + 1 < ws + NAR)
            yield i, J, row, v0, v1


def _na_fill_bias(cfg, tab_ref, bias, b):
    GW = cfg.GW
    first = lax.broadcasted_iota(jnp.int32, (GW, LANE), 1) < GW
    for i, J, row, v0, v1 in _na_tiles(cfg, b):
        ok = jnp.where(first, v0.astype(jnp.int32), v1.astype(jnp.int32))
        bias[i * GW:(i + 1) * GW, J * LANE:(J + 1) * LANE] = jnp.where(ok > 0, tab_ref[row], NEG)


def _na_specs(cfg):
    R, nb, ks = _na_geometry(cfg)
    off = _offsets(cfg)
    TQ = 8 * cfg.GW
    KP = 4 * cfg.GW
    ks4 = [k // 4 for k in ks]
    lat_blocks = cfg.T // KP

    def ks4_of(b):
        return jnp.clip(2 * b - 1, 0, R // 4 - 4)

    assert all(int(np.clip(2 * b - 1, 0, R // 4 - 4)) == ks4[b] for b in range(nb))
    assert cfg.TC == KP

    def col(nm):
        c0 = off[nm] // LANE
        q = pl.BlockSpec((TQ, LANE), lambda h, b: (b, c0 + h))
        parts = [pl.BlockSpec((KP, LANE), functools.partial(lambda h, b, t: (ks4_of(b) + t, c0 + h), t=t))
                 for t in range(4)]
        ctx = pl.BlockSpec((KP, LANE), lambda h, b: (lat_blocks, c0 + h))
        return q, parts, ctx

    return nb, TQ, KP, ks4_of, col


def na_fwd(cfg, name, P, tab):
    nb, TQ, KP, ks4_of, col = _na_specs(cfg)
    NH = cfg.NH
    scale = cfg.NDH ** -0.5
    qs, _, _ = col("nq")
    _, kparts, kctx = col("nk")
    _, vparts, vctx = col("nv")

    def body(q_ref, k0, k1, k2, k3, kc_ref, v0, v1, v2, v3, vc_ref, tab_ref, o_ref, lse_ref, bias_ref):
        _na_fill_bias(cfg, tab_ref, bias_ref, pl.program_id(1))
        q = (q_ref[...] * scale).astype(BF16)
        kl = jnp.concatenate([k0[...], k1[...], k2[...], k3[...]], axis=0).astype(BF16)
        vl = jnp.concatenate([v0[...], v1[...], v2[...], v3[...]], axis=0).astype(BF16)
        kc = kc_ref[...].astype(BF16)
        vc = vc_ref[...].astype(BF16)
        sl = dot_nt(q, kl) + bias_ref[...]
        sc = dot_nt(q, kc)
        m = jnp.maximum(jnp.max(sl, axis=-1, keepdims=True), jnp.max(sc, axis=-1, keepdims=True))
        pl_ = jnp.exp(sl - m)
        pc = jnp.exp(sc - m)
        den = jnp.sum(pl_, axis=-1, keepdims=True) + jnp.sum(pc, axis=-1, keepdims=True)
        o = dot_nn(pl_.astype(BF16), vl) + dot_nn(pc.astype(BF16), vc)
        o_ref[...] = o / den
        lse_ref[...] = m + jnp.log(den)

    return _pc(body, name=name, grid=(NH, nb),
               in_specs=[qs, *kparts, kctx, *vparts, vctx,
                         pl.BlockSpec((None, _NTAB, cfg.GW, LANE), lambda h, b: (h, 0, 0, 0))],
               out_specs=[pl.BlockSpec((TQ, LANE), lambda h, b: (b, h)),
                          pl.BlockSpec((None, TQ, 1), lambda h, b: (h, b, 0))],
               out_shape=[jax.ShapeDtypeStruct((cfg.T, NH * LANE), F32),
                          jax.ShapeDtypeStruct((NH, cfg.T, 1), F32)],
               scratch_shapes=[pltpu.VMEM((TQ, 4 * KP), F32)],
               compiler_params=_cp(("parallel", "parallel")))(P, *([P] * 5), *([P] * 5), tab)


def na_bwd(cfg, name, P, tab, o, lse, dmix, dcol0):
    nb, TQ, KP, ks4_of, col = _na_specs(cfg)
    NH, GW = cfg.NH, cfg.GW
    L = P.shape[0]
    scale = cfg.NDH ** -0.5
    qs, _, _ = col("nq")
    _, kparts, kctx = col("nk")
    _, vparts, vctx = col("nv")

    def body(q_ref, k0, k1, k2, k3, kc_ref, v0, v1, v2, v3, vc_ref, tab_ref, o_ref, lse_ref, do_ref,
             dq_ref, dk_ref, dv_ref, dtab_ref, bias_ref):
        b = pl.program_id(1)

        @pl.when(b == 0)
        def _():
            dk_ref[...] = jnp.zeros_like(dk_ref)
            dv_ref[...] = jnp.zeros_like(dv_ref)
            dtab_ref[...] = jnp.zeros_like(dtab_ref)

        _na_fill_bias(cfg, tab_ref, bias_ref, b)

        q = (q_ref[...] * scale).astype(BF16)
        kl = jnp.concatenate([k0[...], k1[...], k2[...], k3[...]], axis=0).astype(BF16)
        vl = jnp.concatenate([v0[...], v1[...], v2[...], v3[...]], axis=0).astype(BF16)
        kc = kc_ref[...].astype(BF16)
        vc = vc_ref[...].astype(BF16)
        lse = lse_ref[...]
        do = do_ref[...]
        dob = do.astype(BF16)
        p_l = jnp.exp(dot_nt(q, kl) + bias_ref[...] - lse)
        p_c = jnp.exp(dot_nt(q, kc) - lse)
        delta = jnp.sum(do * o_ref[...], axis=-1, keepdims=True)
        ds_l = p_l * (dot_nt(dob, vl) - delta)
        ds_c = p_c * (dot_nt(dob, vc) - delta)
        dslb = ds_l.astype(BF16)
        dscb = ds_c.astype(BF16)
        dq_ref[...] = ((dot_nn(dslb, kl) + dot_nn(dscb, kc)) * scale).astype(dq_ref.dtype)
        r0 = pl.multiple_of(ks4_of(b) * KP, KP)
        dk_ref[pl.ds(r0, 4 * KP), :] += dot_tn(dslb, q)
        dv_ref[pl.ds(r0, 4 * KP), :] += dot_tn(p_l.astype(BF16), dob)
        dk_ref[cfg.T:cfg.T + KP, :] += dot_tn(dscb, q)
        dv_ref[cfg.T:cfg.T + KP, :] += dot_tn(p_c.astype(BF16), dob)
        bias_ref[...] = ds_l
        for i, J, row, _, _ in _na_tiles(cfg, b):
            dtab_ref[row] += bias_ref[i * GW:(i + 1) * GW, J * LANE:(J + 1) * LANE]

    full = pl.BlockSpec((L, LANE), lambda h, b: (0, h))
    tabs = pl.BlockSpec((None, _NTAB, GW, LANE), lambda h, b: (h, 0, 0, 0))
    return _pc(body, name=name, grid=(NH, nb),
               in_specs=[qs, *kparts, kctx, *vparts, vctx, tabs,
                         pl.BlockSpec((TQ, LANE), lambda h, b: (b, h)),
                         pl.BlockSpec((None, TQ, 1), lambda h, b: (h, b, 0)),
                         pl.BlockSpec((TQ, LANE), lambda h, b: (b, dcol0 + h))],
               out_specs=[pl.BlockSpec((TQ, LANE), lambda h, b: (b, h)), full, full, tabs],
               out_shape=[jax.ShapeDtypeStruct((cfg.T, NH * LANE), BF16),
                          jax.ShapeDtypeStruct((L, NH * LANE), F32), jax.ShapeDtypeStruct((L, NH * LANE), F32),
                          jax.ShapeDtypeStruct((NH, _NTAB, GW, LANE), F32)],
               scratch_shapes=[pltpu.VMEM((TQ, 4 * KP), F32)],
               compiler_params=_cp(("parallel", "arbitrary")))(
                   P, *([P] * 5), *([P] * 5), tab, o, lse, dmix)


def na_ctx_fwd(cfg, name, P):
    off = _offsets(cfg)
    TC, NH = cfg.TC, cfg.NH
    rb = cfg.T // TC
    scale = cfg.NDH ** -0.5

    def body(q_ref, k_ref, v_ref, o_ref, lse_ref):
        q = (q_ref[...] * scale).astype(BF16)
        s = dot_nt(q, k_ref[...].astype(BF16))
        m = jnp.max(s, axis=-1, keepdims=True)
        p = jnp.exp(s - m)
        den = jnp.sum(p, axis=-1, keepdims=True)
        o_ref[...] = dot_nn(p.astype(BF16), v_ref[...].astype(BF16)) / den
        lse_ref[...] = m + jnp.log(den)

    spec = lambda nm: pl.BlockSpec((TC, LANE), functools.partial(lambda h, c0: (rb, c0 + h), c0=off[nm] // LANE))
    return _pc(body, name=name, grid=(NH,), in_specs=[spec("nq"), spec("nk"), spec("nv")],
               out_specs=[pl.BlockSpec((TC, LANE), lambda h: (0, h)), pl.BlockSpec((None, TC, 1), lambda h: (h, 0, 0))],
               out_shape=[jax.ShapeDtypeStruct((TC, NH * LANE), F32), jax.ShapeDtypeStruct((NH, TC, 1), F32)],
               compiler_params=_cp(("parallel",)))(P, P, P)


def na_ctx_bwd(cfg, name, P, o, lse, dmix, dcol0, dk_in, dv_in):
    off = _offsets(cfg)
    TC, NH = cfg.TC, cfg.NH
    rb = cfg.T // TC
    scale = cfg.NDH ** -0.5

    def body(q_ref, k_ref, v_ref, o_ref, lse_ref, do_ref, dki_ref, dvi_ref, dq_ref, dk_ref, dv_ref):
        q = (q_ref[...] * scale).astype(BF16)
        kb = k_ref[...].astype(BF16)
        vb = v_ref[...].astype(BF16)
        do = do_ref[...]
        dob = do.astype(BF16)
        p = jnp.exp(dot_nt(q, kb) - lse_ref[...])
        delta = jnp.sum(do * o_ref[...], axis=-1, keepdims=True)
        ds = (p * (dot_nt(dob, vb) - delta)).astype(BF16)
        dq_ref[...] = (dot_nn(ds, kb) * scale).astype(dq_ref.dtype)
        dk_ref[...] = (dki_ref[...] + dot_tn(ds, q)).astype(dk_ref.dtype)
        dv_ref[...] = (dvi_ref[...] + dot_tn(p.astype(BF16), dob)).astype(dv_ref.dtype)

    spec = lambda nm: pl.BlockSpec((TC, LANE), functools.partial(lambda h, c0: (rb, c0 + h), c0=off[nm] // LANE))
    hb = pl.BlockSpec((TC, LANE), lambda h: (0, h))
    ctxrow = pl.BlockSpec((TC, LANE), lambda h: (rb, h))
    shp = jax.ShapeDtypeStruct((TC, NH * LANE), BF16)
    return _pc(body, name=name, grid=(NH,),
               in_specs=[spec("nq"), spec("nk"), spec("nv"), hb, pl.BlockSpec((None, TC, 1), lambda h: (h, 0, 0)),
                         pl.BlockSpec((TC, LANE), lambda h: (rb, dcol0 + h)), ctxrow, ctxrow],
               out_specs=[hb, hb, hb], out_shape=[shp, shp, shp],
               compiler_params=_cp(("parallel",)))(P, P, P, o, lse, dmix, dk_in, dv_in)


def na_rpb_grad(cfg, name, dtab):
    NH, GW = cfg.NH, cfg.GW
    na, nd = 2 * cfg.NAR - 1, 2 * cfg.NAC - 1
    oda, odb, _ = _na_col_onehot(cfg)
    E = np.concatenate([oda.T, odb.T], axis=1)
    rows = NH * _NTAB

    def body(z_ref, e_ref, o_ref):
        zv = z_ref[...]
        hi = zv.astype(BF16)
        lo = (zv - hi.astype(F32)).astype(BF16)
        e = e_ref[...]
        o_ref[...] = dot_nn(hi, e) + dot_nn(lo, e)

    g = _pc(body, name=name, out_shape=jax.ShapeDtypeStruct((rows, 2 * LANE), F32),
            compiler_params=_cp())(dtab.reshape(rows, GW * LANE), jnp.asarray(E, BF16))
    g = g.reshape(NH, _NTAB, 2, LANE)
    return g[:, 1:1 + na, 0, :nd] + g[:, 0:na, 1, :nd]


def _seq_tiles(cfg):
    T, TC, TB = cfg.T, cfg.TC, cfg.TB
    tiles = []
    for i in range((T + TC) // TB):
        r0 = i * TB
        tiles.append((r0, r0 == 0 or r0 == T, r0 + TB == T or r0 + TB == T + TC))
    return tiles


def _shift3(ref_get, r0, TB, start, end, width):
    cur = ref_get(r0, TB)
    if start or end:
        rowi = lax.broadcasted_iota(jnp.int32, (TB, width), 0)
    up = jnp.where(rowi == 0, 0.0, pltpu.roll(cur, 1, 0)) if start else ref_get(r0 - 1, TB)
    dn = jnp.where(rowi == TB - 1, 0.0, pltpu.roll(cur, TB - 1, 0)) if end else ref_get(r0 + 1, TB)
    return up, cur, dn


def ffn_act_fwd(cfg, name, U2, w, b):
    _, L, DFF = U2.shape
    TB = cfg.TB
    tiles = _seq_tiles(cfg)

    def body(u_ref, w_ref, b_ref, a_ref):
        def plane(p, r0, st, en):
            up, cur, dn = _shift3(lambda r, n: u_ref[p, r:r + n, :], r0, TB, st, en, LANE)
            wv = w_ref[p]
            return wv[0:1, :] * up + wv[1:2, :] * cur + wv[2:3, :] * dn + b_ref[p]

        for r0, st, en in tiles:
            val = plane(0, r0, st, en)
            gate = plane(1, r0, st, en)
            a_ref[r0:r0 + TB, :] = (_silu(gate) * val).astype(a_ref.dtype)

    return _pc(body, name=name, grid=(DFF // LANE,),
               in_specs=[pl.BlockSpec((2, L, LANE), lambda j: (0, 0, j)),
                         pl.BlockSpec((2, 8, LANE), lambda j: (0, 0, j)),
                         pl.BlockSpec((2, 1, LANE), lambda j: (0, 0, j))],
               out_specs=pl.BlockSpec((L, LANE), lambda j: (0, j)),
               out_shape=jax.ShapeDtypeStruct((L, DFF), BF16),
               compiler_params=_cp(("parallel",)))(U2, w, b)


def ffn_act_bwd(cfg, name, U2, w, b, dA):
    _, L, DFF = U2.shape
    TB = cfg.TB
    tiles = _seq_tiles(cfg)

    def body(u_ref, w_ref, b_ref, da_ref, du_ref, dw_ref, db_ref, dbuf):
        dw_ref[...] = jnp.zeros_like(dw_ref)
        db_ref[...] = jnp.zeros_like(db_ref)
        for r0, st, en in tiles:
            shifted = []
            pre = []
            for p in range(2):
                up, cur, dn = _shift3(lambda r, n: u_ref[p, r:r + n, :], r0, TB, st, en, LANE)
                wv = w_ref[p]
                shifted.append((up, cur, dn))
                pre.append(wv[0:1, :] * up + wv[1:2, :] * cur + wv[2:3, :] * dn + b_ref[p])
            val, gate = pre
            da = da_ref[r0:r0 + TB, :]
            dpre = (da * _silu(gate), da * val * _dsilu(gate))
            for p in range(2):
                dbuf[p, r0:r0 + TB, :] = dpre[p]
                for k in range(3):
                    dw_ref[p, k:k + 1, :] += jnp.sum(dpre[p] * shifted[p][k], axis=0, keepdims=True)
                db_ref[p] += jnp.sum(dpre[p], axis=0, keepdims=True)
        for r0, st, en in tiles:
            for p in range(2):
                up, cur, dn = _shift3(lambda r, n: dbuf[p, r:r + n, :], r0, TB, st, en, LANE)
                wv = w_ref[p]
                du_ref[p, r0:r0 + TB, :] = (wv[0:1, :] * dn + wv[1:2, :] * cur + wv[2:3, :] * up).astype(du_ref.dtype)

    blk = pl.BlockSpec((2, L, LANE), lambda j: (0, 0, j))
    wspec = pl.BlockSpec((2, 8, LANE), lambda j: (0, 0, j))
    bspec = pl.BlockSpec((2, 1, LANE), lambda j: (0, 0, j))
    return _pc(body, name=name, grid=(DFF // LANE,),
               in_specs=[blk, wspec, bspec, pl.BlockSpec((L, LANE), lambda j: (0, j))],
               out_specs=[blk, wspec, bspec],
               out_shape=[jax.ShapeDtypeStruct((2, L, DFF), BF16), jax.ShapeDtypeStruct((2, 8, DFF), F32),
                          jax.ShapeDtypeStruct((2, 1, DFF), F32)],
               scratch_shapes=[pltpu.VMEM((2, L, LANE), F32)],
               compiler_params=_cp(("parallel",)))(U2, w, b, dA)


def _tm(L, parts):
    assert L % parts == 0
    return L // parts


def layer_fwd(cfg, l, XS, mod, wts, small, tabs):
    L, D = XS.shape
    off = _offsets(cfg)
    DIN = off["end"]
    Win4, Wout4, Wup4, Wdn4, Wpw4 = wts["w_in"], wts["w_out"], wts["ffn_up"], wts["ffn_down"], wts["conv_pw"]
    nbi = Win4.shape[3]
    tmA = _tm(L, 4)
    sv = {}
    sv["XS"] = XS
    h1 = norm_mod_fwd(cfg, f"norm1_fwd_{l}", XS, small["norm1_g"], mod["sc1"], mod["sh1"])
    P = matmul(f"mm_in_{l}", h1, Win4, contract="nn", grid=(4, L // tmA),
               a_spec=pl.BlockSpec((tmA, D), lambda n, m: (m, 0)),
               b_spec=pl.BlockSpec((None, None, D, nbi), lambda n, m: (n, l, 0, 0)),
               out_shape=jax.ShapeDtypeStruct((L, DIN), F32),
               out_spec=pl.BlockSpec((tmA, nbi), lambda n, m: (m, n)), nk=1)
    qr, kr = rope_fwd(cfg, f"rope_fwd_{l}", P, tabs["cos"], tabs["sin"])
    o_f, o_b, st = retention_fwd(cfg, f"ret_fwd_{l}", qr, kr, P, small["lam"])
    o2 = (o_f, o_b)
    ret = ggn_fwd(cfg, f"ggn_fwd_{l}", o2, P, small["ret_gn_g"])
    ycv = glu_dwconv_fwd(cfg, f"dwconv_fwd_{l}", P, small["conv_dw_w"], small["conv_dw_b"])
    act = ln_silu_fwd(cfg, f"ln_silu_fwd_{l}", ycv, small["conv_ln_g"], small["conv_ln_b"])
    cv = mm_rowsharded(f"mm_pw_{l}", act, Wpw4, l, BF16, cfg.CW)
    bias = na_tables(cfg, f"na_tables_{l}", small["na_rpb"])
    na_l, lse = na_fwd(cfg, f"na_fwd_{l}", P, bias)
    na_c, lse_c = na_ctx_fwd(cfg, f"na_ctx_fwd_{l}", P)
    mix = jnp.concatenate([ret, cv, jnp.concatenate([na_l, na_c], axis=0).astype(BF16)], axis=1)
    Y1 = mm_rowsharded(f"mm_out_{l}", mix, Wout4, l, F32, D)
    XM = resid_fwd(cfg, f"resid1_fwd_{l}", XS, Y1, mod["g1"])
    h2 = norm_mod_fwd(cfg, f"norm2_fwd_{l}", XM, small["norm2_g"], mod["sc2"], mod["sh2"])
    nbu = Wup4.shape[3]
    tnu = nbu // 2
    U2 = matmul(f"mm_up_{l}", h2, Wup4, contract="nn", grid=(8, L // tmA),
                a_spec=pl.BlockSpec((tmA, D), lambda n, m: (m, 0)),
                b_spec=pl.BlockSpec((None, None, D, tnu), lambda n, m: (n // 2, l, 0, n % 2)),
                out_shape=jax.ShapeDtypeStruct((2, L, cfg.DFF), F32),
                out_spec=pl.BlockSpec((None, tmA, tnu), lambda n, m: (n // 4, m, n % 4)), nk=1)
    A = ffn_act_fwd(cfg, f"ffn_act_fwd_{l}", U2, small["ffn_dw_w"], small["ffn_dw_b"])
    Y2 = mm_rowsharded(f"mm_down_{l}", A, Wdn4, l, F32, D // 2)
    XO = resid_fwd(cfg, f"resid2_fwd_{l}", XM, Y2, mod["g2"])
    sv.update(h1=h1, P=P, qr=qr, kr=kr, o2=o2, st=st, ycv=ycv, act=act, bias=bias, na_l=na_l, lse=lse,
              na_c=na_c, lse_c=lse_c, mix=mix, Y1=Y1, XM=XM, h2=h2, U2=U2, A=A, Y2=Y2)
    return XO, sv


def layer_bwd(cfg, l, dXO, sv, mod, wts, small, tabs, gbuf):
    L, D = dXO.shape
    off = _offsets(cfg)
    DIN = off["end"]
    Win4, Wout4, Wup4, Wdn4, Wpw4 = wts["w_in"], wts["w_out"], wts["ffn_up"], wts["ffn_down"], wts["conv_pw"]
    tmA, tmB = _tm(L, 4), _tm(L, 8)
    depth = Win4.shape[1]
    gb, gs, dm = {}, {}, {}
    P = sv["P"]
    dY2, dm["g2"] = resid_bwd(cfg, f"resid2_bwd_{l}", dXO, sv["Y2"], mod["g2"])
    nbd = Wdn4.shape[2]
    dA = matmul(f"mm_down_da_{l}", dY2, Wdn4, contract="nt", grid=(4, L // tmA),
                a_spec=pl.BlockSpec((tmA, D), lambda j, m: (m, 0)),
                b_spec=pl.BlockSpec((None, None, nbd, D), lambda j, m: (j, l, 0, 0)),
                out_shape=jax.ShapeDtypeStruct((L, cfg.DFF), F32),
                out_spec=pl.BlockSpec((tmA, nbd), lambda j, m: (m, j)), nk=1)
    gb["ffn_down"] = wgrad(cfg, f"mm_down_dw_{l}", sv["A"], dY2,
                           lambda rb, ri: pl.BlockSpec((rb, nbd), lambda j, m: (ri(m), j)),
                           lambda rb, ri: pl.BlockSpec((rb, D), lambda j, m: (ri(m), 0)),
                           jax.ShapeDtypeStruct((depth, 4, nbd, D), F32),
                           pl.BlockSpec((None, None, nbd, D), lambda j, m: (l, j, 0, 0)), 4, gbuf.get("ffn_down"))
    dU2, dfw, dfb = ffn_act_bwd(cfg, f"ffn_act_bwd_{l}", sv["U2"], small["ffn_dw_w"], small["ffn_dw_b"], dA)
    gs["ffn_dw_w"], gs["ffn_dw_b"] = dfw, dfb
    nbu = Wup4.shape[3]
    tnu = nbu // 2
    dH2 = matmul(f"mm_up_dh_{l}", dU2, Wup4, contract="nt", grid=(L // tmA, 8),
                 a_spec=pl.BlockSpec((None, tmA, tnu), lambda m, n: (n // 4, m, n % 4)),
                 b_spec=pl.BlockSpec((None, None, D, tnu), lambda m, n: (n // 2, l, 0, n % 2)),
                 out_shape=jax.ShapeDtypeStruct((L, D), F32),
                 out_spec=pl.BlockSpec((tmA, D), lambda m, n: (m, 0)), nk=8)
    gb["ffn_up"] = wgrad(cfg, f"mm_up_dw_{l}", sv["h2"], dU2,
                         lambda rb, ri: pl.BlockSpec((rb, D), lambda n, m: (ri(m), 0)),
                         lambda rb, ri: pl.BlockSpec((None, rb, tnu), lambda n, m: (n // 4, ri(m), n % 4)),
                         jax.ShapeDtypeStruct((depth, 4, D, nbu), F32),
                         pl.BlockSpec((None, None, D, tnu), lambda n, m: (l, n // 2, 0, n % 2)), 8, gbuf.get("ffn_up"))
    dXM, dm["sc2"], dm["sh2"], gs["norm2_g"] = norm_mod_bwd(
        cfg, f"norm2_bwd_{l}", dH2, sv["XM"], small["norm2_g"], mod["sc2"], dXO)
    dY1, dm["g1"] = resid_bwd(cfg, f"resid1_bwd_{l}", dXM, sv["Y1"], mod["g1"])
    nbo = Wout4.shape[2]
    dmix = matmul(f"mm_out_dmix_{l}", dY1, Wout4, contract="nt", grid=(4, L // tmA),
                  a_spec=pl.BlockSpec((tmA, D), lambda j, m: (m, 0)),
                  b_spec=pl.BlockSpec((None, None, nbo, D), lambda j, m: (j, l, 0, 0)),
                  out_shape=jax.ShapeDtypeStruct((L, D), F32),
                  out_spec=pl.BlockSpec((tmA, nbo), lambda j, m: (m, j)), nk=1)
    gb["w_out"] = wgrad(cfg, f"mm_out_dw_{l}", sv["mix"], dY1,
                        lambda rb, ri: pl.BlockSpec((rb, nbo), lambda j, m: (ri(m), j)),
                        lambda rb, ri: pl.BlockSpec((rb, D), lambda j, m: (ri(m), 0)),
                        jax.ShapeDtypeStruct((depth, 4, nbo, D), F32),
                        pl.BlockSpec((None, None, nbo, D), lambda j, m: (l, j, 0, 0)), 4, gbuf.get("w_out"))
    RW = cfg.RH * cfg.RDV
    do, dlg, gs["ret_gn_g"] = ggn_bwd(cfg, f"ggn_bwd_{l}", dmix, sv["o2"], P, small["ret_gn_g"], 0)
    dqf, dqb, dkf, dkb, dvf, dvb, dlam = retention_bwd(
        cfg, f"ret_bwd_{l}", sv["qr"], sv["kr"], P, small["lam"], sv["st"], do)
    gs["lam"] = dlam[:, :, 0, 0]
    dlq, dlk = rope_bwd(cfg, f"rope_bwd_{l}", (dqf, dqb), (dkf, dkb), tabs["cos"], tabs["sin"])
    dlv = add_cast(cfg, f"ret_dv_{l}", dvf, dvb)
    dcv = cast_cols(cfg, f"conv_dcv_{l}", dmix, RW // LANE, cfg.CW // LANE, LANE)
    nbp = Wpw4.shape[2]
    dact = matmul(f"mm_pw_dact_{l}", dcv, Wpw4, contract="nt", grid=(4, L // tmA),
                  a_spec=pl.BlockSpec((tmA, cfg.CW), lambda j, m: (m, 0)),
                  b_spec=pl.BlockSpec((None, None, nbp, cfg.CW), lambda j, m: (j, l, 0, 0)),
                  out_shape=jax.ShapeDtypeStruct((L, cfg.CW), F32),
                  out_spec=pl.BlockSpec((tmA, nbp), lambda j, m: (m, j)), nk=1)
    gb["conv_pw"] = wgrad(cfg, f"mm_pw_dw_{l}", sv["act"], dcv,
                          lambda rb, ri: pl.BlockSpec((rb, nbp), lambda j, m: (ri(m), j)),
                          lambda rb, ri: pl.BlockSpec((rb, cfg.CW), lambda j, m: (ri(m), 0)),
                          jax.ShapeDtypeStruct((depth, 4, nbp, cfg.CW), F32),
                          pl.BlockSpec((None, None, nbp, cfg.CW), lambda j, m: (l, j, 0, 0)), 4, gbuf.get("conv_pw"))
    dycv, gs["conv_ln_g"], gs["conv_ln_b"] = ln_silu_bwd(
        cfg, f"ln_silu_bwd_{l}", dact, sv["ycv"], small["conv_ln_g"], small["conv_ln_b"])
    dla, dlb, gs["conv_dw_w"], gs["conv_dw_b"] = glu_dwconv_bwd(cfg, f"dwconv_bwd_{l}", P, small["conv_dw_w"], dycv)
    nac0 = (RW + cfg.CW) // LANE
    dnq_l, dnk, dnv, dsb = na_bwd(cfg, f"na_bwd_{l}", P, sv["bias"], sv["na_l"], sv["lse"], dmix, nac0)
    dnq_c, dnk_c, dnv_c = na_ctx_bwd(cfg, f"na_ctx_bwd_{l}", P, sv["na_c"], sv["lse_c"], dmix, nac0, dnk, dnv)
    gs["na_rpb"] = na_rpb_grad(cfg, f"na_rpb_{l}", dsb)
    dnq = jnp.concatenate([dnq_l, dnq_c], axis=0)
    dnk = jnp.concatenate([dnk[:cfg.T].astype(BF16), dnk_c], axis=0)
    dnv = jnp.concatenate([dnv[:cfg.T].astype(BF16), dnv_c], axis=0)
    dP = jnp.concatenate([dlq, dlk, dlv, dlg, dla, dlb, dnq, dnk, dnv], axis=1)
    nbi = Win4.shape[3]
    dH1 = matmul(f"mm_in_dh_{l}", dP, Win4, contract="nt", grid=(L // tmA, 4),
                 a_spec=pl.BlockSpec((tmA, nbi), lambda m, n: (m, n)),
                 b_spec=pl.BlockSpec((None, None, D, nbi), lambda m, n: (n, l, 0, 0)),
                 out_shape=jax.ShapeDtypeStruct((L, D), F32),
                 out_spec=pl.BlockSpec((tmA, D), lambda m, n: (m, 0)), nk=4)
    gb["w_in"] = wgrad(cfg, f"mm_in_dw_{l}", sv["h1"], dP,
                       lambda rb, ri: pl.BlockSpec((rb, D), lambda n, m: (ri(m), 0)),
                       lambda rb, ri: pl.BlockSpec((rb, nbi), lambda n, m: (ri(m), n)),
                       jax.ShapeDtypeStruct((depth, 4, D, nbi), F32),
                       pl.BlockSpec((None, None, D, nbi), lambda n, m: (l, n, 0, 0)), 4, gbuf.get("w_in"))
    dXS, dm["sc1"], dm["sh1"], gs["norm1_g"] = norm_mod_bwd(
        cfg, f"norm1_bwd_{l}", dH1, sv["XS"], small["norm1_g"], mod["sc1"], dXM)
    return dXS, gb, gs, dm


def _layer_small(cfg, l, sp):
    DFF = cfg.DFF
    fw = sp["ffn_dw_w"][l].reshape(3, 2, DFF).transpose(1, 0, 2)
    fw = jnp.concatenate([fw, jnp.zeros((2, 5, DFF), F32)], axis=1)
    cw = jnp.concatenate([sp["conv_dw_w"][l], jnp.zeros((32 - cfg.CK, cfg.CW), F32)], axis=0)
    return dict(
        norm1_g=sp["norm1_g"][l][None], norm2_g=sp["norm2_g"][l][None],
        lam=jax.nn.log_sigmoid(sp["ret_decay"][l]), ret_gn_g=sp["ret_gn_g"][l][None],
        conv_dw_w=cw, conv_dw_b=sp["conv_dw_b"][l][None], conv_ln_g=sp["conv_ln_g"][l][None],
        conv_ln_b=sp["conv_ln_b"][l][None], na_rpb=sp["na_rpb"][l],
        ffn_dw_w=fw, ffn_dw_b=sp["ffn_dw_b"][l].reshape(2, 1, DFF))


def local_step(cfg, x, ctx, tgt, mods, wts, sp):
    depth = sp["norm1_g"].shape[0]
    cos, sin = rope_tables(cfg)
    tabs = dict(cos=cos, sin=sin)
    XS = jnp.concatenate([x, ctx], axis=0)
    smalls = [_layer_small(cfg, l, sp) for l in range(depth)]
    saves = []
    for l in range(depth):
        XS, sv = layer_fwd(cfg, l, XS, mods[l], wts, smalls[l], tabs)
        saves.append(sv)
    ls, dX, dfg = final_loss(cfg, "final_loss", XS, sp["final_g"][None], tgt)
    gb, gss, dms = {}, [None] * depth, [None] * depth
    for l in reversed(range(depth)):
        dX, gb, gss[l], dms[l] = layer_bwd(cfg, l, dX, saves[l], mods[l], wts, smalls[l], tabs, gb)
    return ls[0, 0], dX[:cfg.T], gb, gss, dms, dfg[0]


MESH = pl.DeviceIdType.MESH
N_DEV = 8
N_CHIP = 4
BIG = ("w_in", "w_out", "ffn_up", "ffn_down", "conv_pw")
_ANY = pl.BlockSpec(memory_space=pl.ANY)


def _place():
    x, y, c = lax.axis_index("x"), lax.axis_index("y"), lax.axis_index("c")
    chips = [(1 - x, y), (x, 1 - y), (1 - x, 1 - y)]
    return x, y, c, chips


def allgather8(name, xs):
    m_per, n = xs.shape

    def body(x_ref, out_ref, send_sems, recv_sems, local_sem):
        x, y, c, chips = _place()
        me, sibling = (x, y, c), (x, y, 1 - c)

        def rows(px, py, pc):
            return out_ref.at[pl.ds((4 * px + 2 * py + pc) * m_per, m_per), :]

        def copy(k, block, to, src=None):
            return pltpu.make_async_remote_copy(
                src_ref=rows(*block) if src is None else src, dst_ref=rows(*block),
                send_sem=send_sems.at[k], recv_sem=recv_sems.at[k], device_id=to, device_id_type=MESH)

        mine = pltpu.make_async_copy(x_ref, rows(*me), local_sem)
        mine.start()
        first = [copy(0, me, sibling, src=x_ref)]
        first += [copy(1 + j, me, (*chip, c), src=x_ref) for j, chip in enumerate(chips)]
        for cp in first:
            cp.start()
        passed = [copy(4 + j, (*chip, c), sibling) for j, chip in enumerate(chips)]
        for j, chip in enumerate(chips):
            copy(1 + j, (*chip, c), me).wait_recv()
            passed[j].start()
        copy(0, sibling, me).wait_recv()
        for j, chip in enumerate(chips):
            copy(4 + j, (*chip, 1 - c), me).wait_recv()
        for cp in first + passed:
            cp.wait_send()
        mine.wait()

    return _pc(body, name=name, out_shape=jax.ShapeDtypeStruct((N_DEV * m_per, n), xs.dtype),
               in_specs=[pl.BlockSpec(memory_space=pltpu.VMEM)], out_specs=pl.BlockSpec(memory_space=pltpu.VMEM),
               scratch_shapes=[pltpu.SemaphoreType.DMA((7,)), pltpu.SemaphoreType.DMA((7,)), pltpu.SemaphoreType.DMA],
               compiler_params=pltpu.CompilerParams(vmem_limit_bytes=VMEM_LIMIT))(xs)


def allgather_weights(name, bufs):
    n = len(bufs)

    def body(*refs):
        outs = refs[n:2 * n]
        send_sems, recv_sems = refs[2 * n:]
        x, y, c, chips = _place()
        j0 = 2 * x + y
        sibling = (x, y, 1 - c)

        def blk(a, chip, layer):
            return outs[a].at[2 * chip[0] + chip[1], layer]

        sent = []
        for a in range(n):
            for t, chip in enumerate(chips):
                cp = pltpu.make_async_remote_copy(
                    src_ref=outs[a].at[j0, c], dst_ref=outs[a].at[j0, c], send_sem=send_sems.at[3 * a + t],
                    recv_sem=recv_sems.at[3 * a + t], device_id=(*chip, c), device_id_type=MESH)
                cp.start()
                sent.append(cp)
        for a in range(n):
            for t, chip in enumerate(chips):
                k = 3 * a + t
                pltpu.make_async_remote_copy(
                    src_ref=blk(a, chip, c), dst_ref=blk(a, chip, c), send_sem=send_sems.at[k],
                    recv_sem=recv_sems.at[k], device_id=(x, y, c), device_id_type=MESH).wait_recv()
                fw = pltpu.make_async_remote_copy(
                    src_ref=blk(a, chip, c), dst_ref=blk(a, chip, c), send_sem=send_sems.at[3 * n + k],
                    recv_sem=recv_sems.at[3 * n + k], device_id=sibling, device_id_type=MESH)
                fw.start()
                sent.append(fw)
        for a in range(n):
            for t, chip in enumerate(chips):
                k = 3 * n + 3 * a + t
                pltpu.make_async_remote_copy(
                    src_ref=blk(a, chip, 1 - c), dst_ref=blk(a, chip, 1 - c), send_sem=send_sems.at[k],
                    recv_sem=recv_sems.at[k], device_id=(x, y, c), device_id_type=MESH).wait_recv()
        for cp in sent:
            cp.wait_send()

    return _pc(body, name=name, out_shape=[jax.ShapeDtypeStruct(b.shape, b.dtype) for b in bufs],
               in_specs=[_ANY] * n, out_specs=[_ANY] * n, input_output_aliases={a: a for a in range(n)},
               scratch_shapes=[pltpu.SemaphoreType.DMA((6 * n,)), pltpu.SemaphoreType.DMA((6 * n,))])(*bufs)


def exchange_halves(name, grads):
    n = len(grads)

    def body(*refs):
        ins, outs = refs[:n], refs[n:2 * n]
        send_sems, recv_sems = refs[2 * n:]
        x, y, c, _ = _place()
        cps = [pltpu.make_async_remote_copy(
            src_ref=ins[a].at[1 - c], dst_ref=outs[a], send_sem=send_sems.at[a], recv_sem=recv_sems.at[a],
            device_id=(x, y, 1 - c), device_id_type=MESH) for a in range(n)]
        for cp in cps:
            cp.start()
        for cp in cps:
            cp.wait()

    return _pc(body, name=name, out_shape=[jax.ShapeDtypeStruct(g.shape[1:], g.dtype) for g in grads],
               in_specs=[_ANY] * n, out_specs=[_ANY] * n,
               scratch_shapes=[pltpu.SemaphoreType.DMA((n,)), pltpu.SemaphoreType.DMA((n,))])(*grads)


def scatter_slices(name, parts, lands):
    n = len(parts)

    def body(*refs):
        ins, outs = refs[:n], refs[2 * n:3 * n]
        send_sems, recv_sems = refs[3 * n:]
        x, y, c, chips = _place()
        j0 = 2 * x + y
        cps = []
        for a in range(n):
            for t, chip in enumerate(chips):
                jt = 2 * chip[0] + chip[1]
                cps.append(pltpu.make_async_remote_copy(
                    src_ref=ins[a].at[jt], dst_ref=outs[a].at[j0], send_sem=send_sems.at[3 * a + t],
                    recv_sem=recv_sems.at[3 * a + t], device_id=(*chip, c), device_id_type=MESH))
        for cp in cps:
            cp.start()
        k = 0
        for a in range(n):
            for t, chip in enumerate(chips):
                jt = 2 * chip[0] + chip[1]
                pltpu.make_async_remote_copy(
                    src_ref=outs[a].at[jt], dst_ref=outs[a].at[jt], send_sem=send_sems.at[k],
                    recv_sem=recv_sems.at[k], device_id=(x, y, c), device_id_type=MESH).wait_recv()
                k += 1
        for cp in cps:
            cp.wait_send()

    return _pc(body, name=name, out_shape=[jax.ShapeDtypeStruct(p.shape, p.dtype) for p in lands],
               in_specs=[_ANY] * (2 * n), out_specs=[_ANY] * n,
               input_output_aliases={n + a: a for a in range(n)},
               scratch_shapes=[pltpu.SemaphoreType.DMA((3 * n,)), pltpu.SemaphoreType.DMA((3 * n,))])(*parts, *lands)


def share_layers(name, bufs):
    n = len(bufs)

    def body(*refs):
        outs = refs[n:2 * n]
        send_sems, recv_sems = refs[2 * n:]
        x, y, c, _ = _place()
        cps = [pltpu.make_async_remote_copy(
            src_ref=outs[a].at[c], dst_ref=outs[a].at[c], send_sem=send_sems.at[a], recv_sem=recv_sems.at[a],
            device_id=(x, y, 1 - c), device_id_type=MESH) for a in range(n)]
        for cp in cps:
            cp.start()
        for a in range(n):
            pltpu.make_async_remote_copy(
                src_ref=outs[a].at[1 - c], dst_ref=outs[a].at[1 - c], send_sem=send_sems.at[a],
                recv_sem=recv_sems.at[a], device_id=(x, y, c), device_id_type=MESH).wait_recv()
        for cp in cps:
            cp.wait_send()

    return _pc(body, name=name, out_shape=[jax.ShapeDtypeStruct(b.shape, b.dtype) for b in bufs],
               in_specs=[_ANY] * n, out_specs=[_ANY] * n, input_output_aliases={a: a for a in range(n)},
               scratch_shapes=[pltpu.SemaphoreType.DMA((n,)), pltpu.SemaphoreType.DMA((n,))])(*bufs)


def _row_tile(R, C, nbytes=1 << 20):
    t = 8
    while t * 2 <= R and R % (t * 2) == 0 and t * 2 * C * 4 <= nbytes:
        t *= 2
    assert R % t == 0
    return t


def to_bf16_block(name, w, chip_arr):
    _, R, C = w.shape
    tr = _row_tile(R, C)

    def body(j_ref, w_ref, o_ref):
        o_ref[...] = w_ref[...].astype(o_ref.dtype)

    gs = pltpu.PrefetchScalarGridSpec(
        num_scalar_prefetch=1, grid=(2, R // tr),
        in_specs=[pl.BlockSpec((None, tr, C), lambda l, i, j_ref: (l, i, 0))],
        out_specs=pl.BlockSpec((None, None, tr, C), lambda l, i, j_ref: (j_ref[0], l, i, 0)))
    return _pc(body, name=name, grid_spec=gs, out_shape=jax.ShapeDtypeStruct((N_CHIP,) + w.shape, BF16),
               compiler_params=_cp(("parallel", "parallel")))(chip_arr, w)


def add_own_layer(name, g, ra, c_arr):
    _, _, R, C = g.shape
    tr = _row_tile(R, C)

    def body(c_ref, g_ref, r_ref, o_ref):
        o_ref[...] = (g_ref[...] + r_ref[...]).astype(o_ref.dtype)

    gs = pltpu.PrefetchScalarGridSpec(
        num_scalar_prefetch=1, grid=(4, R // tr),
        in_specs=[pl.BlockSpec((None, None, tr, C), lambda j, i, c_ref: (c_ref[0], j, i, 0)),
                  pl.BlockSpec((None, tr, C), lambda j, i, c_ref: (j, i, 0))],
        out_specs=pl.BlockSpec((None, tr, C), lambda j, i, c_ref: (j, i, 0)))
    return _pc(body, name=name, grid_spec=gs, out_shape=jax.ShapeDtypeStruct(ra.shape, BF16),
               compiler_params=_cp(("parallel", "parallel")))(c_arr, g, ra)


def own_row(name, part, chip_arr):
    _, R, C = part.shape
    tr = _row_tile(R, C)

    def body(j_ref, p_ref, o_ref):
        o_ref[...] = p_ref[...]

    gs = pltpu.PrefetchScalarGridSpec(
        num_scalar_prefetch=1, grid=(R // tr,),
        in_specs=[pl.BlockSpec((None, tr, C), lambda i, j_ref: (j_ref[0], i, 0))],
        out_specs=pl.BlockSpec((None, tr, C), lambda i, j_ref: (j_ref[0], i, 0)))
    return _pc(body, name=name, grid_spec=gs, out_shape=jax.ShapeDtypeStruct(part.shape, part.dtype),
               compiler_params=_cp(("parallel",)))(chip_arr, part)


def sum_leading(name, g, plane=None):
    n, R, C = g.shape
    tr = _row_tile(R, C, nbytes=(1 << 21) // n)

    def body(*refs):
        g_ref, o_ref = refs[-2:]
        acc = g_ref[0].astype(F32)
        for j in range(1, n):
            acc = acc + g_ref[j].astype(F32)
        o_ref[...] = acc

    if plane is None:
        return _pc(body, name=name, grid=(R // tr,), in_specs=[pl.BlockSpec((n, tr, C), lambda i: (0, i, 0))],
                   out_specs=pl.BlockSpec((tr, C), lambda i: (i, 0)), out_shape=jax.ShapeDtypeStruct((R, C), F32),
                   compiler_params=_cp(("parallel",)))(g)
    count, idx = plane
    gs = pltpu.PrefetchScalarGridSpec(
        num_scalar_prefetch=1, grid=(R // tr,),
        in_specs=[pl.BlockSpec((n, tr, C), lambda i, p_ref: (0, i, 0))],
        out_specs=pl.BlockSpec((None, tr, C), lambda i, p_ref: (p_ref[0], i, 0)))
    return _pc(body, name=name, grid_spec=gs, out_shape=jax.ShapeDtypeStruct((count, R, C), F32),
               compiler_params=_cp(("parallel",)))(idx, g)


def adamw(name, w, g, m, v):
    R, C = w.shape
    tr = _row_tile(R, C)

    def body(w_ref, g_ref, m_ref, v_ref, d_ref, mo_ref, vo_ref):
        gv = g_ref[...]
        mn = ADAM_B1 * m_ref[...] + (1.0 - ADAM_B1) * gv
        vn = ADAM_B2 * v_ref[...] + (1.0 - ADAM_B2) * (gv * gv)
        m_hat = mn / (1.0 - ADAM_B1 ** ADAM_STEP)
        v_hat = vn / (1.0 - ADAM_B2 ** ADAM_STEP)
        d_ref[...] = -ADAM_LR * (m_hat / (jnp.sqrt(v_hat) + ADAM_EPS) + ADAM_WD * w_ref[...])
        mo_ref[...] = mn
        vo_ref[...] = vn

    spec = pl.BlockSpec((tr, C), lambda i: (i, 0))
    shp = jax.ShapeDtypeStruct((R, C), F32)
    return _pc(body, name=name, grid=(R // tr,), in_specs=[spec] * 4, out_specs=[spec] * 3,
               out_shape=[shp] * 3, compiler_params=_cp(("parallel",)))(w, g, m, v)


_ADA_TN = 512


def adaln_fwd(name, cond, w, b):
    _, D, N = w.shape
    tn = min(_ADA_TN, N)

    def body(c_ref, w_ref, b_ref, o_ref):
        s = _silu(c_ref[...]).astype(BF16)
        o_ref[...] = dot_nn(s, w_ref[...].astype(BF16)) + b_ref[...]

    return _pc(body, name=name, grid=(2, N // tn),
               in_specs=[pl.BlockSpec((16, D), lambda l, n: (0, 0)),
                         pl.BlockSpec((None, D, tn), lambda l, n: (l, 0, n)),
                         pl.BlockSpec((None, 1, tn), lambda l, n: (l, 0, n))],
               out_specs=pl.BlockSpec((None, 16, tn), lambda l, n: (l, 0, n)),
               out_shape=jax.ShapeDtypeStruct((2, 16, N), F32),
               compiler_params=_cp(("parallel", "parallel")))(cond, w, b)


def adaln_bwd(name, cond, w, dm):
    _, D, N = w.shape
    tn = min(_ADA_TN, N)

    def body(c_ref, w_ref, dm_ref, gw_ref, ds_ref):
        first = jnp.logical_and(pl.program_id(0) == 0, pl.program_id(1) == 0)
        s = _silu(c_ref[...]).astype(BF16)
        dmb = dm_ref[...].astype(BF16)
        gw_ref[...] = dot_tn(s, dmb)
        p = dot_nt(dmb, w_ref[...].astype(BF16))

        @pl.when(first)
        def _():
            ds_ref[...] = p

        @pl.when(jnp.logical_not(first))
        def _():
            ds_ref[...] += p

    return _pc(body, name=name, grid=(2, N // tn),
               in_specs=[pl.BlockSpec((16, D), lambda l, n: (0, 0)),
                         pl.BlockSpec((None, D, tn), lambda l, n: (l, 0, n)),
                         pl.BlockSpec((None, 16, tn), lambda l, n: (l, 0, n))],
               out_specs=[pl.BlockSpec((None, D, tn), lambda l, n: (l, 0, n)),
                          pl.BlockSpec((16, D), lambda l, n: (0, 0))],
               out_shape=[jax.ShapeDtypeStruct((2, D, N), F32), jax.ShapeDtypeStruct((16, D), F32)],
               compiler_params=_cp(("arbitrary", "arbitrary")))(cond, w, dm)


def cctx_grad(name, parts, c_ctx):
    def body(p_ref, c_ref, o_ref):
        acc = p_ref[0]
        for j in range(1, N_CHIP):
            acc = acc + p_ref[j]
        o_ref[...] = acc * _dsilu(c_ref[...])

    return _pc(body, name=name, out_shape=jax.ShapeDtypeStruct(c_ctx.shape, F32))(parts, c_ctx)


def _pack(arrs, rows_mult=8):
    flat = jnp.concatenate([a.reshape(-1) for a in arrs])
    n = flat.shape[0]
    unit = rows_mult * LANE
    tot = -(-n // unit) * unit
    return jnp.concatenate([flat, jnp.zeros((tot - n,), F32)]).reshape(tot // LANE, LANE)


def _unpack(flat, shapes):
    out, o = [], 0
    for s in shapes:
        n = int(np.prod(s))
        out.append(flat[o:o + n].reshape(s))
        o += n
    return out


MOD_NAMES = ("sh1", "sc1", "g1", "sh2", "sc2", "g2")


def kernel(x, c, ctx, c_ctx, w_ada, b_ada, norm1_g, w_in, ret_decay, ret_gn_g, conv_dw_w, conv_dw_b, conv_ln_g, conv_ln_b, conv_pw, na_rpb, w_out, norm2_g, ffn_up, ffn_dw_w, ffn_dw_b, ffn_down, final_g, loss_target, m_c_ctx, m_w_ada, m_b_ada, m_norm1_g, m_w_in, m_ret_decay, m_ret_gn_g, m_conv_dw_w, m_conv_dw_b, m_conv_ln_g, m_conv_ln_b, m_conv_pw, m_na_rpb, m_w_out, m_norm2_g, m_ffn_up, m_ffn_dw_w, m_ffn_dw_b, m_ffn_down, m_final_g, v_c_ctx, v_w_ada, v_b_ada, v_norm1_g, v_w_in, v_ret_decay, v_ret_gn_g, v_conv_dw_w, v_conv_dw_b, v_conv_ln_g, v_conv_ln_b, v_conv_pw, v_na_rpb, v_w_out, v_norm2_g, v_ffn_up, v_ffn_dw_w, v_ffn_dw_b, v_ffn_down, v_final_g):
    cfg = make_cfg(D=x.shape[2], T=x.shape[1], TC=ctx.shape[1], RH=ret_decay.shape[2], CW=conv_dw_b.shape[1],
                   NH=na_rpb.shape[1], DFF=ffn_dw_b.shape[1] // 2)
    D, T = cfg.D, cfg.T
    W = dict(c_ctx=c_ctx, w_ada=w_ada, b_ada=b_ada, norm1_g=norm1_g, w_in=w_in, ret_decay=ret_decay, ret_gn_g=ret_gn_g,
             conv_dw_w=conv_dw_w, conv_dw_b=conv_dw_b, conv_ln_g=conv_ln_g, conv_ln_b=conv_ln_b, conv_pw=conv_pw,
             na_rpb=na_rpb, w_out=w_out, norm2_g=norm2_g, ffn_up=ffn_up, ffn_dw_w=ffn_dw_w, ffn_dw_b=ffn_dw_b,
             ffn_down=ffn_down, final_g=final_g)
    Mo = dict(c_ctx=m_c_ctx, w_ada=m_w_ada, b_ada=m_b_ada, norm1_g=m_norm1_g, w_in=m_w_in, ret_decay=m_ret_decay,
              ret_gn_g=m_ret_gn_g, conv_dw_w=m_conv_dw_w, conv_dw_b=m_conv_dw_b, conv_ln_g=m_conv_ln_g,
              conv_ln_b=m_conv_ln_b, conv_pw=m_conv_pw, na_rpb=m_na_rpb, w_out=m_w_out, norm2_g=m_norm2_g,
              ffn_up=m_ffn_up, ffn_dw_w=m_ffn_dw_w, ffn_dw_b=m_ffn_dw_b, ffn_down=m_ffn_down, final_g=m_final_g)
    Vo = dict(c_ctx=v_c_ctx, w_ada=v_w_ada, b_ada=v_b_ada, norm1_g=v_norm1_g, w_in=v_w_in, ret_decay=v_ret_decay,
              ret_gn_g=v_ret_gn_g, conv_dw_w=v_conv_dw_w, conv_dw_b=v_conv_dw_b, conv_ln_g=v_conv_ln_g,
              conv_ln_b=v_conv_ln_b, conv_pw=v_conv_pw, na_rpb=v_na_rpb, w_out=v_w_out, norm2_g=v_norm2_g,
              ffn_up=v_ffn_up, ffn_dw_w=v_ffn_dw_w, ffn_dw_b=v_ffn_dw_b, ffn_down=v_ffn_down, final_g=v_final_g)
    order = list(W)
    xi, yi, ci = lax.axis_index("x"), lax.axis_index("y"), lax.axis_index("c")
    chip = 2 * xi + yi
    dev = 4 * xi + 2 * yi + ci
    NA = w_ada.shape[2]
    ncw, nfw = conv_dw_w.shape[2], ffn_dw_w.shape[2]

    g_in = allgather8("ag_small_in", _pack([c[0], conv_dw_w, ffn_dw_w])).reshape(N_DEV, -1)
    c8 = g_in[:, :D]
    by_chip = g_in[0::2, D:]
    cw_parts, fw_parts = [], []
    for j in range(N_CHIP):
        a, b = _unpack(by_chip[j], [conv_dw_w.shape, ffn_dw_w.shape])
        cw_parts.append(a)
        fw_parts.append(b)
    conv_dw_w_full = jnp.concatenate(cw_parts, axis=2)
    ffn_dw_w_full = jnp.concatenate(fw_parts, axis=2)
    cond = jnp.concatenate([c8, c_ctx[None], jnp.zeros((16 - N_DEV - 1, D), F32)], axis=0)

    b_sh = lax.dynamic_slice(b_ada, (0, chip * NA), (2, NA)).reshape(2, 1, NA)
    m_sh = adaln_fwd("adaln_fwd", cond, w_ada, b_sh)
    m_all = allgather8("ag_mod", m_sh.reshape(2 * 16, NA)).reshape(N_DEV, 2, 16, NA)[0::2]
    m_all = m_all.transpose(1, 2, 0, 3).reshape(2, 16, N_CHIP * NA)
    mods = []
    for l in range(2):
        lat = lax.dynamic_slice(m_all[l], (dev, 0), (1, N_CHIP * NA))[0]
        cx = m_all[l, N_DEV]
        mods.append({nm: jnp.stack([lat[k * D:(k + 1) * D], cx[k * D:(k + 1) * D]], 0)[:, None, :]
                     for k, nm in enumerate(MOD_NAMES)})

    c_arr = jnp.reshape(ci, (1,)).astype(jnp.int32)
    chip_arr = jnp.reshape(chip, (1,)).astype(jnp.int32)
    wts = dict(zip(BIG, allgather_weights(
        "ag_weights", [to_bf16_block(f"to_bf16_{nm}", W[nm], chip_arr) for nm in BIG])))

    sp = dict(norm1_g=norm1_g, norm2_g=norm2_g, ret_decay=ret_decay, ret_gn_g=ret_gn_g, conv_dw_w=conv_dw_w_full,
              conv_dw_b=conv_dw_b, conv_ln_g=conv_ln_g, conv_ln_b=conv_ln_b, na_rpb=na_rpb, ffn_dw_w=ffn_dw_w_full,
              ffn_dw_b=ffn_dw_b, final_g=final_g)
    loss_l, gx, gb, gss, dms, dfg = local_step(cfg, x[0], ctx[0], loss_target[0], mods, wts, sp)
    loss = lax.psum(loss_l, ("x", "y", "c"))

    dmseg = jnp.stack([jnp.stack([jnp.concatenate([dms[l][nm][r, 0] for nm in MOD_NAMES]) for r in range(2)])
                       for l in range(2)])
    gsm = dict(
        norm1_g=jnp.stack([gss[l]["norm1_g"][0] for l in range(2)]),
        ret_decay=jnp.stack([gss[l]["lam"] * jax.nn.sigmoid(-ret_decay[l]) for l in range(2)]),
        ret_gn_g=jnp.stack([gss[l]["ret_gn_g"][0] for l in range(2)]),
        conv_dw_w=jnp.stack([gss[l]["conv_dw_w"][:cfg.CK] for l in range(2)]),
        conv_dw_b=jnp.stack([gss[l]["conv_dw_b"][0] for l in range(2)]),
        conv_ln_g=jnp.stack([gss[l]["conv_ln_g"][0] for l in range(2)]),
        conv_ln_b=jnp.stack([gss[l]["conv_ln_b"][0] for l in range(2)]),
        na_rpb=jnp.stack([gss[l]["na_rpb"] for l in range(2)]),
        norm2_g=jnp.stack([gss[l]["norm2_g"][0] for l in range(2)]),
        ffn_dw_w=jnp.stack([gss[l]["ffn_dw_w"][:, :3].transpose(1, 0, 2).reshape(3, 2 * cfg.DFF) for l in range(2)]),
        ffn_dw_b=jnp.stack([gss[l]["ffn_dw_b"].reshape(-1) for l in range(2)]),
        final_g=dfg)
    snames = list(gsm)
    sshapes = [dmseg.shape] + [gsm[nm].shape for nm in snames]
    packed = _pack([dmseg] + [gsm[nm] for nm in snames])
    g_all = allgather8("ag_small_grads", packed).reshape(N_DEV, packed.shape[0], LANE)
    summed = sum_leading("sum_small_grads", g_all).reshape(-1)
    dm_sum, *gsum = _unpack(summed, sshapes)
    gfull = dict(zip(snames, gsum))
    ndm = int(np.prod(dmseg.shape))
    dm_all = g_all.reshape(N_DEV, -1)[:, :ndm].reshape(N_DEV, 2, 2, 6 * D)
    gfull["b_ada"] = sum_leading("sum_b_ada", dm_all.transpose(0, 2, 1, 3).reshape(2 * N_DEV, 2 * 6 * D // LANE, LANE)
                                 ).reshape(2, 6 * D)

    dm16 = jnp.concatenate([dm_all[:, :, 0].transpose(1, 0, 2), dm_sum[:, 1][:, None],
                            jnp.zeros((2, 16 - N_DEV - 1, 6 * D), F32)], axis=1)
    dm16 = lax.dynamic_slice(dm16, (0, 0, chip * NA), (2, 16, NA))
    gfull["w_ada"], ds16 = adaln_bwd("adaln_bwd", cond, w_ada, dm16)
    ds_all = allgather8("ag_dsilu", ds16[8:16]).reshape(N_DEV, 8, D)[0::2, 0:1]
    gfull["c_ctx"] = cctx_grad("cctx_grad", ds_all, c_ctx[None])[0]
    gfull["conv_dw_w"] = lax.dynamic_slice(gfull["conv_dw_w"], (0, 0, chip * ncw), (2, cfg.CK, ncw))
    gfull["ffn_dw_w"] = lax.dynamic_slice(gfull["ffn_dw_w"], (0, 0, chip * nfw), (2, 3, nfw))

    from_sib = exchange_halves("rs_exchange", [gb[nm] for nm in BIG])
    part = [add_own_layer(f"rs_add_{nm}", gb[nm], r, c_arr) for nm, r in zip(BIG, from_sib)]
    lands = [own_row(f"rs_own_{nm}", p, chip_arr) for nm, p in zip(BIG, part)]
    landed = scatter_slices("rs_scatter", part, lands)
    done = [sum_leading(f"rs_sum_{nm}", p, plane=(2, c_arr)) for nm, p in zip(BIG, landed)]
    for nm, gfin in zip(BIG, share_layers("rs_share", done)):
        gfull[nm] = gfin

    delta, new_m, new_v = {}, {}, {}
    bigs = ("w_ada",) + BIG
    for nm in bigs:
        shp = W[nm].shape
        v2 = lambda a: a.reshape(-1, shp[-1])
        d_, m_, v_ = adamw(f"adamw_{nm}", v2(W[nm]), v2(gfull[nm]), v2(Mo[nm]), v2(Vo[nm]))
        delta[nm], new_m[nm], new_v[nm] = d_.reshape(shp), m_.reshape(shp), v_.reshape(shp)
    smalls = [nm for nm in order if nm not in bigs]
    shapes = [W[nm].shape for nm in smalls]
    d_, m_, v_ = adamw("adamw_small", _pack([W[nm] for nm in smalls]), _pack([gfull[nm] for nm in smalls]),
                       _pack([Mo[nm] for nm in smalls]), _pack([Vo[nm] for nm in smalls]))
    for nm, a, b, e in zip(smalls, _unpack(d_.reshape(-1), shapes), _unpack(m_.reshape(-1), shapes),
                           _unpack(v_.reshape(-1), shapes)):
        delta[nm], new_m[nm], new_v[nm] = a, b, e
    return (loss, gx[None], *[gfull[nm] for nm in order], *[delta[nm] for nm in order],
            *[new_m[nm] for nm in order], *[new_v[nm] for nm in order])
```

```python
import collections
import functools

import numpy as np
import jax
import jax.numpy as jnp
from jax import lax
from jax.experimental import pallas as pl
from jax.experimental.pallas import tpu as pltpu

F32 = jnp.float32
BF16 = jnp.bfloat16
EPS = 1e-6
ROPE_BASE = 10000.0
NEG = -1e30
LANE = 128
VMEM_LIMIT = 56 * 1024 * 1024

ADAM_LR, ADAM_B1, ADAM_B2, ADAM_EPS, ADAM_WD, ADAM_STEP = 0.001, 0.9, 0.999, 1e-08, 0.01, 10

Cfg = collections.namedtuple(
    "Cfg", "D T TC GW RH RDK RDV CW CK NH NDH NAR NAC DFF TB")


def make_cfg(D=2048, T=4096, TC=256, RH=4, CW=512, NH=4, DFF=5632):
    return Cfg(D=D, T=T, TC=TC, GW=64, RH=RH, RDK=128, RDV=256, CW=CW, CK=31, NH=NH, NDH=128,
               NAR=8, NAC=16, DFF=DFF, TB=256)


def _offsets(cfg):
    sizes = [cfg.RH * cfg.RDK, cfg.RH * cfg.RDK, cfg.RH * cfg.RDV, cfg.RH * cfg.RDV, cfg.CW, cfg.CW,
             cfg.NH * cfg.NDH, cfg.NH * cfg.NDH, cfg.NH * cfg.NDH]
    offs = [0]
    for s in sizes:
        offs.append(offs[-1] + s)
    return dict(zip(["lq", "lk", "lv", "lg", "la", "lb", "nq", "nk", "nv", "end"], offs))


def _pc(body, **kw):
    return pl.pallas_call(body, **kw)


def _cp(sem=None):
    return pltpu.CompilerParams(dimension_semantics=sem, vmem_limit_bytes=VMEM_LIMIT)


def _dot(a, b, ca, cb):
    return lax.dot_general(a, b, (((ca,), (cb,)), ((), ())), preferred_element_type=F32)


def dot_nn(a, b):
    return _dot(a, b, 1, 0)


def dot_nt(a, b):
    return _dot(a, b, 1, 1)


def dot_tn(a, b):
    return _dot(a, b, 0, 0)


def _sigmoid(x):
    return 1.0 / (1.0 + jnp.exp(-x))


def _silu(x):
    return x * _sigmoid(x)


def _dsilu(x):
    s = _sigmoid(x)
    return s * (1.0 + x * (1.0 - s))


def matmul(name, a, b, *, contract, grid, a_spec, b_spec, out_shape, out_spec, nk, into=None):
    dot = {"nn": dot_nn, "nt": dot_nt, "tn": dot_tn}[contract]
    direct = nk > 1 and out_shape.dtype == F32
    kax = len(grid) - 1

    def body(a_ref, b_ref, *rest):
        o_ref, *scr = rest[1:] if into is not None else rest
        p = dot(a_ref[...].astype(BF16), b_ref[...].astype(BF16))
        if nk == 1:
            o_ref[...] = p.astype(o_ref.dtype)
            return
        acc = o_ref if direct else scr[0]
        k = pl.program_id(kax)

        @pl.when(k == 0)
        def _():
            acc[...] = p

        @pl.when(k > 0)
        def _():
            acc[...] += p

        if not direct:
            @pl.when(k == nk - 1)
            def _():
                o_ref[...] = acc[...].astype(o_ref.dtype)

    scratch = []
    if nk > 1 and not direct:
        blk = [s for s in out_spec.block_shape if s is not None]
        scratch = [pltpu.VMEM(tuple(blk), F32)]
    sem = ("parallel",) * kax + (("arbitrary",) if nk > 1 else ("parallel",))
    in_specs, args, alias = [a_spec, b_spec], (a, b), {}
    if into is not None:
        in_specs, args, alias = in_specs + [pl.BlockSpec(memory_space=pl.ANY)], (a, b, into), {2: 0}
    return _pc(body, name=name, grid=grid, in_specs=in_specs, out_specs=out_spec, out_shape=out_shape,
               scratch_shapes=scratch, input_output_aliases=alias, compiler_params=_cp(sem))(*args)


_WG_ROWS = 1024


def wgrad(cfg, name, a, dc, a_spec, dc_spec, out_shape, out_spec, ntiles, into):
    T, TC = cfg.T, cfg.TC
    tml = min(_WG_ROWS, T)
    nl = T // tml

    def body(al_ref, ac_ref, dl_ref, dcx_ref, *rest):
        o_ref, acc = rest[-2:]
        m = pl.program_id(1)

        @pl.when(m == 0)
        def _():
            acc[...] = dot_tn(al_ref[...], dl_ref[...])

        @pl.when(jnp.logical_and(m > 0, m < nl))
        def _():
            acc[...] += dot_tn(al_ref[...], dl_ref[...])

        @pl.when(m == nl)
        def _():
            o_ref[...] = (acc[...] + dot_tn(ac_ref[...], dcx_ref[...])).astype(o_ref.dtype)

    lat = lambda m: jnp.minimum(m, nl - 1)
    ctx = lambda m: T // TC
    in_specs = [a_spec(tml, lat), a_spec(TC, ctx), dc_spec(tml, lat), dc_spec(TC, ctx)]
    args, alias = (a, a, dc, dc), {}
    if into is not None:
        in_specs, args, alias = in_specs + [pl.BlockSpec(memory_space=pl.ANY)], args + (into,), {4: 0}
    blk = tuple(s for s in out_spec.block_shape if s is not None)
    return _pc(body, name=name, grid=(ntiles, nl + 1), in_specs=in_specs, out_specs=out_spec, out_shape=out_shape,
               scratch_shapes=[pltpu.VMEM(blk, F32)], input_output_aliases=alias,
               compiler_params=_cp(("parallel", "arbitrary")))(*args)


def mm_rowsharded(name, a, w4, l, out_dtype, tn):
    L, K = a.shape
    nch, _, Kb, N = w4.shape
    tm = 256

    def body(a_ref, w_ref, o_ref):
        acc = dot_nn(a_ref[:, 0:Kb], w_ref[0])
        for j in range(1, nch):
            acc += dot_nn(a_ref[:, j * Kb:(j + 1) * Kb], w_ref[j])
        o_ref[...] = acc.astype(o_ref.dtype)

    return _pc(body, name=name, grid=(N // tn, L // tm),
               in_specs=[pl.BlockSpec((tm, K), lambda n, m: (m, 0)),
                         pl.BlockSpec((nch, None, Kb, tn), lambda n, m: (0, l, 0, n))],
               out_specs=pl.BlockSpec((tm, tn), lambda n, m: (m, n)),
               out_shape=jax.ShapeDtypeStruct((L, N), out_dtype),
               compiler_params=_cp(("parallel", "parallel")))(a, w4)


def _region(cfg):
    nlat = cfg.T // cfg.TB
    return lambda i: jnp.minimum(i // nlat, 1)


def norm_mod_fwd(cfg, name, x, ng, sc, sh):
    L, D = x.shape
    TB = cfg.TB
    reg = _region(cfg)

    def body(x_ref, ng_ref, sc_ref, sh_ref, h_ref):
        xv = x_ref[...]
        r = lax.rsqrt(jnp.mean(xv * xv, axis=-1, keepdims=True) + EPS)
        n = xv * r * ng_ref[...]
        h_ref[...] = (n * (1.0 + sc_ref[...]) + sh_ref[...]).astype(h_ref.dtype)

    row = pl.BlockSpec((TB, D), lambda i: (i, 0))
    vec = pl.BlockSpec((1, D), lambda i: (0, 0))
    rvec = pl.BlockSpec((None, 1, D), lambda i: (reg(i), 0, 0))
    return _pc(body, name=name, grid=(L // TB,), in_specs=[row, vec, rvec, rvec], out_specs=row,
               out_shape=jax.ShapeDtypeStruct((L, D), BF16), compiler_params=_cp(("parallel",)))(x, ng, sc, sh)


def norm_mod_bwd(cfg, name, dh, x, ng, sc, dx_in):
    L, D = x.shape
    TB = cfg.TB
    nlat = cfg.T // TB
    reg = _region(cfg)

    def body(dh_ref, x_ref, ng_ref, sc_ref, dxi_ref, dx_ref, dsc_ref, dsh_ref, dng_ref):
        i = pl.program_id(0)
        xv = x_ref[...]
        r = lax.rsqrt(jnp.mean(xv * xv, axis=-1, keepdims=True) + EPS)
        xh = xv * r
        g = ng_ref[...]
        n = xh * g
        dh = dh_ref[...]
        dn = dh * (1.0 + sc_ref[...])
        dxh = dn * g
        dx = r * (dxh - xh * jnp.mean(dxh * xh, axis=-1, keepdims=True))
        dx_ref[...] = dxi_ref[...] + dx
        s_sh = jnp.sum(dh, axis=0, keepdims=True)
        s_sc = jnp.sum(dh * n, axis=0, keepdims=True)
        s_ng = jnp.sum(dn * xh, axis=0, keepdims=True)
        first = jnp.logical_or(i == 0, i == nlat)

        @pl.when(first)
        def _():
            dsh_ref[...] = s_sh
            dsc_ref[...] = s_sc

        @pl.when(jnp.logical_not(first))
        def _():
            dsh_ref[...] += s_sh
            dsc_ref[...] += s_sc

        @pl.when(i == 0)
        def _():
            dng_ref[...] = s_ng

        @pl.when(i > 0)
        def _():
            dng_ref[...] += s_ng

    row = pl.BlockSpec((TB, D), lambda i: (i, 0))
    vec = pl.BlockSpec((1, D), lambda i: (0, 0))
    rvec = pl.BlockSpec((None, 1, D), lambda i: (reg(i), 0, 0))
    return _pc(body, name=name, grid=(L // TB,), in_specs=[row, row, vec, rvec, row],
               out_specs=[row, rvec, rvec, vec],
               out_shape=[jax.ShapeDtypeStruct((L, D), F32), jax.ShapeDtypeStruct((2, 1, D), F32),
                          jax.ShapeDtypeStruct((2, 1, D), F32), jax.ShapeDtypeStruct((1, D), F32)],
               compiler_params=_cp(("arbitrary",)))(dh, x, ng, sc, dx_in)


def resid_fwd(cfg, name, x, y, g):
    L, D = x.shape
    TB = cfg.TB
    reg = _region(cfg)

    def body(x_ref, y_ref, g_ref, o_ref):
        o_ref[...] = x_ref[...] + g_ref[...] * y_ref[...]

    row = pl.BlockSpec((TB, D), lambda i: (i, 0))
    rvec = pl.BlockSpec((None, 1, D), lambda i: (reg(i), 0, 0))
    return _pc(body, name=name, grid=(L // TB,), in_specs=[row, row, rvec], out_specs=row,
               out_shape=jax.ShapeDtypeStruct((L, D), F32), compiler_params=_cp(("parallel",)))(x, y, g)


def resid_bwd(cfg, name, dxo, y, g):
    L, D = y.shape
    TB = cfg.TB
    nlat = cfg.T // TB
    reg = _region(cfg)

    def body(d_ref, y_ref, g_ref, dy_ref, dg_ref):
        i = pl.program_id(0)
        d = d_ref[...]
        dy_ref[...] = (d * g_ref[...]).astype(dy_ref.dtype)
        s = jnp.sum(d * y_ref[...], axis=0, keepdims=True)
        first = jnp.logical_or(i == 0, i == nlat)

        @pl.when(first)
        def _():
            dg_ref[...] = s

        @pl.when(jnp.logical_not(first))
        def _():
            dg_ref[...] += s

    row = pl.BlockSpec((TB, D), lambda i: (i, 0))
    rvec = pl.BlockSpec((None, 1, D), lambda i: (reg(i), 0, 0))
    return _pc(body, name=name, grid=(L // TB,), in_specs=[row, row, rvec], out_specs=[row, rvec],
               out_shape=[jax.ShapeDtypeStruct((L, D), BF16), jax.ShapeDtypeStruct((2, 1, D), F32)],
               compiler_params=_cp(("arbitrary",)))(dxo, y, g)


def final_loss(cfg, name, x, fg, tgt):
    L, D = x.shape
    TB = cfg.TB
    nlat = cfg.T // TB

    def body(x_ref, fg_ref, t_ref, ls_ref, dx_ref, dg_ref):
        i = pl.program_id(0)

        @pl.when(i == 0)
        def _():
            ls_ref[...] = jnp.zeros_like(ls_ref)
            dg_ref[...] = jnp.zeros_like(dg_ref)

        @pl.when(i < nlat)
        def _():
            xv = x_ref[...]
            r = lax.rsqrt(jnp.mean(xv * xv, axis=-1, keepdims=True) + EPS)
            xh = xv * r
            g = fg_ref[...]
            e = xh * g - t_ref[...]
            ls_ref[...] += 0.5 * jnp.sum(e * e) / D
            dy = e / D
            dg_ref[...] += jnp.sum(dy * xh, axis=0, keepdims=True)
            dxh = dy * g
            dx_ref[...] = r * (dxh - xh * jnp.mean(dxh * xh, axis=-1, keepdims=True))

        @pl.when(i >= nlat)
        def _():
            dx_ref[...] = jnp.zeros_like(dx_ref)

    row = pl.BlockSpec((TB, D), lambda i: (i, 0))
    trow = pl.BlockSpec((TB, D), lambda i: (jnp.minimum(i, nlat - 1), 0))
    vec = pl.BlockSpec((1, D), lambda i: (0, 0))
    return _pc(body, name=name, grid=(L // TB,), in_specs=[row, vec, trow],
               out_specs=[pl.BlockSpec((1, LANE), lambda i: (0, 0)), row, vec],
               out_shape=[jax.ShapeDtypeStruct((1, LANE), F32), jax.ShapeDtypeStruct((L, D), F32),
                          jax.ShapeDtypeStruct((1, D), F32)],
               compiler_params=_cp(("arbitrary",)))(x, fg, tgt)


def rope_tables(cfg):
    half = cfg.RDK // 2
    nf = half // 2
    pos = np.arange(cfg.T)
    row = (pos // cfg.GW).astype(np.float32)
    col = (pos % cfg.GW).astype(np.float32)
    inv = jnp.asarray(ROPE_BASE, F32) ** (-jnp.arange(nf, dtype=F32) / nf)
    ar = jnp.asarray(row)[:, None] * inv[None, :]
    ac = jnp.asarray(col)[:, None] * inv[None, :]
    cos = jnp.concatenate([jnp.cos(ar), jnp.cos(ar), jnp.cos(ac), jnp.cos(ac)], axis=1)
    sin = jnp.concatenate([-jnp.sin(ar), jnp.sin(ar), -jnp.sin(ac), jnp.sin(ac)], axis=1)
    cos = jnp.concatenate([cos, jnp.ones((cfg.TC, cfg.RDK), F32)], axis=0)
    sin = jnp.concatenate([sin, jnp.zeros((cfg.TC, cfg.RDK), F32)], axis=0)
    return cos, sin


def _swap32(t):
    lane = lax.broadcasted_iota(jnp.int32, t.shape, 1)
    return jnp.where((lane % 64) < 32, pltpu.roll(t, 96, 1), pltpu.roll(t, 32, 1))


def rope_fwd(cfg, name, P, cos, sin):
    L = P.shape[0]
    TB = cfg.TB
    off = _offsets(cfg)
    cq, ck = off["lq"] // LANE, off["lk"] // LANE
    scale = cfg.RDK ** -0.5

    def body(q_ref, k_ref, c_ref, s_ref, qo_ref, ko_ref):
        c = c_ref[...]
        s = s_ref[...]
        q = q_ref[...]
        k = k_ref[...]
        qo_ref[...] = (q * c + _swap32(q) * s) * scale
        ko_ref[...] = k * c + _swap32(k) * s

    tab = pl.BlockSpec((TB, LANE), lambda i, h: (i, 0))
    out = pl.BlockSpec((TB, LANE), lambda i, h: (i, h))
    shp = jax.ShapeDtypeStruct((L, cfg.RH * cfg.RDK), F32)
    return _pc(body, name=name, grid=(L // TB, cfg.RH),
               in_specs=[pl.BlockSpec((TB, LANE), lambda i, h: (i, cq + h)),
                         pl.BlockSpec((TB, LANE), lambda i, h: (i, ck + h)), tab, tab],
               out_specs=[out, out], out_shape=[shp, shp],
               compiler_params=_cp(("parallel", "parallel")))(P, P, cos, sin)


def rope_bwd(cfg, name, dq2, dk2, cos, sin):
    L, W = dq2[0].shape
    TB = cfg.TB
    scale = cfg.RDK ** -0.5

    def body(dqf_ref, dqb_ref, dkf_ref, dkb_ref, c_ref, s_ref, qo_ref, ko_ref):
        c = c_ref[...]
        s = s_ref[...]
        dq = dqf_ref[...] + dqb_ref[...]
        dk = dkf_ref[...] + dkb_ref[...]
        qo_ref[...] = ((dq * c - _swap32(dq) * s) * scale).astype(qo_ref.dtype)
        ko_ref[...] = (dk * c - _swap32(dk) * s).astype(ko_ref.dtype)

    tab = pl.BlockSpec((TB, LANE), lambda i, h: (i, 0))
    blk = pl.BlockSpec((TB, LANE), lambda i, h: (i, h))
    shp = jax.ShapeDtypeStruct((L, W), BF16)
    return _pc(body, name=name, grid=(L // TB, cfg.RH), in_specs=[blk, blk, blk, blk, tab, tab],
               out_specs=[blk, blk], out_shape=[shp, shp],
               compiler_params=_cp(("parallel", "parallel")))(*dq2, *dk2, cos, sin)


def _ret_chunk_map(cfg):
    C = cfg.RDK
    n = (cfg.T + cfg.TC) // C
    nlat, nctx = cfg.T // C, cfg.TC // C

    def chunk(d, s):
        if d == 0:
            return jnp.where(s < nctx, nlat + s, s - nctx)
        return n - 1 - s

    return n, chunk


def _ret_decay_terms(d, lam, C):
    ii = lax.broadcasted_iota(jnp.int32, (C, C), 0)
    jj = lax.broadcasted_iota(jnp.int32, (C, C), 1)
    diff = (ii - jj if d == 0 else jj - ii).astype(F32)
    dpos = jnp.maximum(diff, 0.0)
    Dm = jnp.where(diff >= 0, jnp.exp(dpos * lam), 0.0)
    ic = lax.broadcasted_iota(jnp.int32, (C, 1), 0).astype(F32)
    cxi = ic + 1.0 if d == 0 else C - ic
    cze = C - 1.0 - ic if d == 0 else ic
    xi = jnp.exp(cxi * lam)
    ze = jnp.exp(cze * lam)
    g = jnp.exp(jnp.full((1, 1), C, F32) * lam)
    return dpos, Dm, cxi, cze, xi, ze, g


def retention_fwd(cfg, name, qr, kr, P, lam):
    L = P.shape[0]
    C, DV, RH = cfg.RDK, cfg.RDV, cfg.RH
    n, chunk = _ret_chunk_map(cfg)

    def body(lam_ref, qf_ref, qb_ref, kf_ref, kb_ref, vf_ref, vb_ref, of_ref, ob_ref, st_ref, S):
        s = pl.program_id(0)

        @pl.when(s == 0)
        def _():
            S[...] = jnp.zeros_like(S)

        for d, (q_ref, k_ref, v_ref, o_ref) in enumerate(((qf_ref, kf_ref, vf_ref, of_ref),
                                                          (qb_ref, kb_ref, vb_ref, ob_ref))):
            for h in range(RH):
                _, Dm, _, _, xi, ze, g = _ret_decay_terms(d, lam_ref[d, h], C)
                k = k_ref[:, h * C:(h + 1) * C]
                qb = q_ref[:, h * C:(h + 1) * C].astype(BF16)
                kb = k.astype(BF16)
                vb = v_ref[:, h * DV:(h + 1) * DV].astype(BF16)
                Sv = S[d, h]
                st_ref[d, h] = Sv
                A = dot_nt(qb, kb) * Dm
                o_ref[:, h * DV:(h + 1) * DV] = dot_nn(A.astype(BF16), vb) + dot_nn(qb, Sv.astype(BF16)) * xi
                S[d, h] = Sv * g + dot_tn((k * ze).astype(BF16), vb)

    def spec(w, col, d):
        return pl.BlockSpec((C, w), lambda s: (chunk(d, s), col))

    W, WV = RH * C, RH * DV
    return _pc(body, name=name, grid=(n,),
               in_specs=[pl.BlockSpec(memory_space=pltpu.SMEM), spec(W, 0, 0), spec(W, 0, 1), spec(W, 0, 0),
                         spec(W, 0, 1), spec(WV, 1, 0), spec(WV, 1, 1)],
               out_specs=[spec(WV, 0, 0), spec(WV, 0, 1),
                          pl.BlockSpec((2, RH, None, C, DV), lambda s: (0, 0, s, 0, 0))],
               out_shape=[jax.ShapeDtypeStruct((L, WV), F32), jax.ShapeDtypeStruct((L, WV), F32),
                          jax.ShapeDtypeStruct((2, RH, n, C, DV), F32)],
               scratch_shapes=[pltpu.VMEM((2, RH, C, DV), F32)],
               compiler_params=_cp(("arbitrary",)))(lam, qr, qr, kr, kr, P, P)


def retention_bwd(cfg, name, qr, kr, P, lam, st, do):
    L = P.shape[0]
    C, DV, RH = cfg.RDK, cfg.RDV, cfg.RH
    n, chunk = _ret_chunk_map(cfg)

    def body(lam_ref, qf_ref, qb_ref, kf_ref, kb_ref, vf_ref, vb_ref, st_ref, dof_ref, dob_ref,
             dqf_ref, dqb_ref, dkf_ref, dkb_ref, dvf_ref, dvb_ref, dl_ref, dS):
        si = pl.program_id(0)

        @pl.when(si == 0)
        def _():
            dS[...] = jnp.zeros_like(dS)
            dl_ref[...] = jnp.zeros_like(dl_ref)

        dirs = ((qf_ref, kf_ref, vf_ref, dof_ref, dqf_ref, dkf_ref, dvf_ref),
                (qb_ref, kb_ref, vb_ref, dob_ref, dqb_ref, dkb_ref, dvb_ref))
        for d, (q_ref, k_ref, v_ref, do_ref, dq_ref, dk_ref, dv_ref) in enumerate(dirs):
            for h in range(RH):
                dpos, Dm, cxi, cze, xi, ze, g = _ret_decay_terms(d, lam_ref[d, h], C)
                hk = slice(h * C, (h + 1) * C)
                hv = slice(h * DV, (h + 1) * DV)
                k = k_ref[:, hk]
                do = do_ref[:, hv]
                qb = q_ref[:, hk].astype(BF16)
                kb = k.astype(BF16)
                vb = v_ref[:, hv].astype(BF16)
                dob = do.astype(BF16)
                Sn = st_ref[d, h]
                Snb = Sn.astype(BF16)
                dSn = dS[d, h]
                dSb = dSn.astype(BF16)
                A = dot_nt(qb, kb) * Dm
                dA = dot_nt(dob, vb)
                dQK = (dA * Dm).astype(BF16)
                kzb = (k * ze).astype(BF16)
                dv_ref[:, hv] = dot_tn(A.astype(BF16), dob) + dot_nn(kzb, dSb)
                dkz = dot_nt(vb, dSb)
                doxb = (do * xi).astype(BF16)
                dq_ref[:, hk] = dot_nn(dQK, kb) + dot_nt(doxb, Snb)
                dk_ref[:, hk] = dot_tn(dQK, qb) + dkz * ze
                QS = dot_nn(qb, Snb)
                t = (jnp.sum(dA * A * dpos) + jnp.sum(do * QS * (cxi * xi)) + jnp.sum(k * dkz * (cze * ze)))
                t4 = jnp.sum(dSn * Sn, axis=0, keepdims=True)
                t4 = jnp.sum(t4 * (g * C), axis=1, keepdims=True)
                dl_ref[d, h] += t + t4
                dS[d, h] = g * dSn + dot_tn(qb, doxb)

    def spec(w, col, d):
        return pl.BlockSpec((C, w), lambda si: (chunk(d, n - 1 - si), col))

    W, WV = RH * C, RH * DV
    return _pc(body, name=name, grid=(n,),
               in_specs=[pl.BlockSpec(memory_space=pltpu.SMEM), spec(W, 0, 0), spec(W, 0, 1), spec(W, 0, 0),
                         spec(W, 0, 1), spec(WV, 1, 0), spec(WV, 1, 1),
                         pl.BlockSpec((2, RH, None, C, DV), lambda si: (0, 0, n - 1 - si, 0, 0)),
                         spec(WV, 0, 0), spec(WV, 0, 1)],
               out_specs=[spec(W, 0, 0), spec(W, 0, 1), spec(W, 0, 0), spec(W, 0, 1), spec(WV, 0, 0), spec(WV, 0, 1),
                          pl.BlockSpec((2, RH, 8, LANE), lambda si: (0, 0, 0, 0))],
               out_shape=[jax.ShapeDtypeStruct((L, W), F32)] * 4 + [jax.ShapeDtypeStruct((L, WV), F32)] * 2
               + [jax.ShapeDtypeStruct((2, RH, 8, LANE), F32)],
               scratch_shapes=[pltpu.VMEM((2, RH, C, DV), F32)],
               compiler_params=_cp(("arbitrary",)))(lam, qr, qr, kr, kr, P, P, st, do, do)


def add_cast(cfg, name, a, b):
    L, W = a.shape
    TB = cfg.TB

    def body(a_ref, b_ref, o_ref):
        o_ref[...] = (a_ref[...] + b_ref[...]).astype(o_ref.dtype)

    spec = pl.BlockSpec((TB, W), lambda i: (i, 0))
    return _pc(body, name=name, grid=(L // TB,), in_specs=[spec, spec], out_specs=spec,
               out_shape=jax.ShapeDtypeStruct((L, W), BF16), compiler_params=_cp(("parallel",)))(a, b)


def ggn_fwd(cfg, name, o2, P, gn_g):
    L = P.shape[0]
    TB, DV, RH = cfg.TB, cfg.RDV, cfg.RH
    gc0 = _offsets(cfg)["lg"] // DV

    def body(of_ref, ob_ref, gate_ref, g_ref, out_ref):
        o = of_ref[...] + ob_ref[...]
        mu = jnp.mean(o, axis=-1, keepdims=True)
        xc = o - mu
        var = jnp.mean(xc * xc, axis=-1, keepdims=True)
        y = xc * lax.rsqrt(var + EPS) * g_ref[...]
        out_ref[...] = (y * _silu(gate_ref[...])).astype(out_ref.dtype)

    blk = pl.BlockSpec((TB, DV), lambda i, h: (i, h))
    return _pc(body, name=name, grid=(L // TB, RH),
               in_specs=[blk, blk, pl.BlockSpec((TB, DV), lambda i, h: (i, gc0 + h)),
                         pl.BlockSpec((1, DV), lambda i, h: (0, h))],
               out_specs=blk, out_shape=jax.ShapeDtypeStruct((L, RH * DV), BF16),
               compiler_params=_cp(("parallel", "parallel")))(*o2, P, gn_g)


def ggn_bwd(cfg, name, dout, o2, P, gn_g, col0):
    L = P.shape[0]
    TB, DV, RH = cfg.TB, cfg.RDV, cfg.RH
    gc0 = _offsets(cfg)["lg"] // DV

    def body(d_ref, of_ref, ob_ref, gate_ref, g_ref, do_ref, dgate_ref, dg_ref):
        i = pl.program_id(1)
        o = of_ref[...] + ob_ref[...]
        mu = jnp.mean(o, axis=-1, keepdims=True)
        xc = o - mu
        var = jnp.mean(xc * xc, axis=-1, keepdims=True)
        r = lax.rsqrt(var + EPS)
        y = xc * r
        g = g_ref[...]
        gate = gate_ref[...]
        d = d_ref[...]
        dgate_ref[...] = (d * (y * g) * _dsilu(gate)).astype(dgate_ref.dtype)
        dyg = d * _silu(gate)
        s = jnp.sum(dyg * y, axis=0, keepdims=True)

        @pl.when(i == 0)
        def _():
            dg_ref[...] = s

        @pl.when(i > 0)
        def _():
            dg_ref[...] += s

        dy = dyg * g
        do_ref[...] = r * (dy - jnp.mean(dy, axis=-1, keepdims=True)
                           - y * jnp.mean(dy * y, axis=-1, keepdims=True))

    blk = pl.BlockSpec((TB, DV), lambda h, i: (i, h))
    return _pc(body, name=name, grid=(RH, L // TB),
               in_specs=[pl.BlockSpec((TB, DV), lambda h, i: (i, col0 + h)), blk, blk,
                         pl.BlockSpec((TB, DV), lambda h, i: (i, gc0 + h)),
                         pl.BlockSpec((1, DV), lambda h, i: (0, h))],
               out_specs=[blk, blk, pl.BlockSpec((1, DV), lambda h, i: (0, h))],
               out_shape=[jax.ShapeDtypeStruct((L, RH * DV), F32), jax.ShapeDtypeStruct((L, RH * DV), BF16),
                          jax.ShapeDtypeStruct((1, RH * DV), F32)],
               compiler_params=_cp(("parallel", "arbitrary")))(dout, *o2, P, gn_g)


def cast_cols(cfg, name, src, col0, ncols, width):
    L = src.shape[0]
    TB = cfg.TB

    def body(s_ref, o_ref):
        o_ref[...] = s_ref[...].astype(o_ref.dtype)

    spec = pl.BlockSpec((TB, width), lambda i, j: (i, col0 + j))
    return _pc(body, name=name, grid=(L // TB, ncols), in_specs=[spec],
               out_specs=pl.BlockSpec((TB, width), lambda i, j: (i, j)),
               out_shape=jax.ShapeDtypeStruct((L, ncols * width), BF16),
               compiler_params=_cp(("parallel", "parallel")))(src)


_CPAD = 16


def _conv_windows(cfg):
    T, TC, TB = cfg.T, cfg.TC, cfg.TB
    assert TC % TB == 0 and T % TB == 0 and cfg.CK // 2 < _CPAD
    return T // TB, [(T + j * TB, T + _CPAD + j * TB) for j in range(TC // TB)]


def _fill_padded(cfg, pb, get):
    T, TC, TB = cfg.T, cfg.TC, cfg.TB
    z = jnp.zeros((_CPAD, LANE), F32)
    pb[0:_CPAD, :] = z
    pb[_CPAD + T:2 * _CPAD + T, :] = z
    pb[2 * _CPAD + T + TC:3 * _CPAD + T + TC, :] = z

    def fill(i, c):
        r0 = pl.multiple_of(i * TB, TB)
        pb[pl.ds(r0 + _CPAD, TB), :] = get(r0)
        return c

    lax.fori_loop(0, T // TB, fill, 0)
    for j in range(TC // TB):
        pb[2 * _CPAD + T + j * TB:2 * _CPAD + T + (j + 1) * TB, :] = get(T + j * TB)


def _taps(win, TB):
    W = TB + 2 * _CPAD
    return lambda k: pltpu.roll(win, W - (k + 1), 0)[0:TB, :]


def glu_dwconv_fwd(cfg, name, P, w, b):
    L = P.shape[0]
    T, TC, TB, K = cfg.T, cfg.TC, cfg.TB, cfg.CK
    off = _offsets(cfg)
    ca, cb = off["la"] // LANE, off["lb"] // LANE
    nlat, ctx_tiles = _conv_windows(cfg)
    PBL = 3 * _CPAD + T + TC

    def body(a_ref, b_ref, w_ref, bias_ref, y_ref, pb):
        _fill_padded(cfg, pb, lambda r0: a_ref[pl.ds(r0, TB), :] * _sigmoid(b_ref[pl.ds(r0, TB), :]))
        wv = w_ref[...]
        bias = bias_ref[...]

        def tile(win):
            tap = _taps(win, TB)
            acc = jnp.zeros((TB, LANE), F32) + bias
            for k in range(K):
                acc = acc + wv[k:k + 1, :] * tap(k)
            return acc

        def lat(i, c):
            r0 = pl.multiple_of(i * TB, TB)
            y_ref[pl.ds(r0, TB), :] = tile(pb[pl.ds(r0, TB + 2 * _CPAD), :])
            return c

        lax.fori_loop(0, nlat, lat, 0)
        for r0, w0 in ctx_tiles:
            y_ref[r0:r0 + TB, :] = tile(pb[w0:w0 + TB + 2 * _CPAD, :])

    return _pc(body, name=name, grid=(cfg.CW // LANE,),
               in_specs=[pl.BlockSpec((L, LANE), lambda j: (0, ca + j)),
                         pl.BlockSpec((L, LANE), lambda j: (0, cb + j)),
                         pl.BlockSpec((32, LANE), lambda j: (0, j)),
                         pl.BlockSpec((1, LANE), lambda j: (0, j))],
               out_specs=pl.BlockSpec((L, LANE), lambda j: (0, j)),
               out_shape=jax.ShapeDtypeStruct((L, cfg.CW), F32),
               scratch_shapes=[pltpu.VMEM((PBL, LANE), F32)],
               compiler_params=_cp(("parallel",)))(P, P, w, b)


def glu_dwconv_bwd(cfg, name, P, w, dy):
    L = P.shape[0]
    T, TC, TB, K = cfg.T, cfg.TC, cfg.TB, cfg.CK
    off = _offsets(cfg)
    ca, cb = off["la"] // LANE, off["lb"] // LANE
    nlat, ctx_tiles = _conv_windows(cfg)
    PBL = 3 * _CPAD + T + TC

    def body(a_ref, b_ref, w_ref, dy_ref, da_ref, db_ref, dw_ref, dbias_ref, pbu, pbd):
        _fill_padded(cfg, pbu, lambda r0: a_ref[pl.ds(r0, TB), :] * _sigmoid(b_ref[pl.ds(r0, TB), :]))
        _fill_padded(cfg, pbd, lambda r0: dy_ref[pl.ds(r0, TB), :])
        wv = w_ref[...]
        dw_ref[...] = jnp.zeros_like(dw_ref)
        dbias_ref[...] = jnp.zeros_like(dbias_ref)

        def tile(r0, winu, wind):
            tapu = _taps(winu, TB)
            tapd = _taps(wind, TB)
            dyt = dy_ref[pl.ds(r0, TB), :]
            du = jnp.zeros((TB, LANE), F32)
            for k in range(K):
                du = du + wv[k:k + 1, :] * tapd(K - 1 - k)
                dw_ref[k:k + 1, :] += jnp.sum(dyt * tapu(k), axis=0, keepdims=True)
            dbias_ref[...] += jnp.sum(dyt, axis=0, keepdims=True)
            a = a_ref[pl.ds(r0, TB), :]
            sg = _sigmoid(b_ref[pl.ds(r0, TB), :])
            da_ref[pl.ds(r0, TB), :] = (du * sg).astype(da_ref.dtype)
            db_ref[pl.ds(r0, TB), :] = (du * a * sg * (1.0 - sg)).astype(db_ref.dtype)

        def lat(i, c):
            r0 = pl.multiple_of(i * TB, TB)
            tile(r0, pbu[pl.ds(r0, TB + 2 * _CPAD), :], pbd[pl.ds(r0, TB + 2 * _CPAD), :])
            return c

        lax.fori_loop(0, nlat, lat, 0)
        for r0, w0 in ctx_tiles:
            tile(r0, pbu[w0:w0 + TB + 2 * _CPAD, :], pbd[w0:w0 + TB + 2 * _CPAD, :])

    col = pl.BlockSpec((L, LANE), lambda j: (0, j))
    return _pc(body, name=name, grid=(cfg.CW // LANE,),
               in_specs=[pl.BlockSpec((L, LANE), lambda j: (0, ca + j)),
                         pl.BlockSpec((L, LANE), lambda j: (0, cb + j)),
                         pl.BlockSpec((32, LANE), lambda j: (0, j)), col],
               out_specs=[col, col, pl.BlockSpec((32, LANE), lambda j: (0, j)),
                          pl.BlockSpec((1, LANE), lambda j: (0, j))],
               out_shape=[jax.ShapeDtypeStruct((L, cfg.CW), BF16), jax.ShapeDtypeStruct((L, cfg.CW), BF16),
                          jax.ShapeDtypeStruct((32, cfg.CW), F32), jax.ShapeDtypeStruct((1, cfg.CW), F32)],
               scratch_shapes=[pltpu.VMEM((PBL, LANE), F32), pltpu.VMEM((PBL, LANE), F32)],
               compiler_params=_cp(("parallel",)))(P, P, w, dy)


def ln_silu_fwd(cfg, name, y, g, b):
    L, W = y.shape
    TB = cfg.TB

    def body(y_ref, g_ref, b_ref, o_ref):
        yv = y_ref[...]
        mu = jnp.mean(yv, axis=-1, keepdims=True)
        xc = yv - mu
        var = jnp.mean(xc * xc, axis=-1, keepdims=True)
        z = xc * lax.rsqrt(var + EPS) * g_ref[...] + b_ref[...]
        o_ref[...] = _silu(z).astype(o_ref.dtype)

    row = pl.BlockSpec((TB, W), lambda i: (i, 0))
    vec = pl.BlockSpec((1, W), lambda i: (0, 0))
    return _pc(body, name=name, grid=(L // TB,), in_specs=[row, vec, vec], out_specs=row,
               out_shape=jax.ShapeDtypeStruct((L, W), BF16), compiler_params=_cp(("parallel",)))(y, g, b)


def ln_silu_bwd(cfg, name, dact, y, g, b):
    L, W = y.shape
    TB = cfg.TB

    def body(d_ref, y_ref, g_ref, b_ref, dy_ref, dg_ref, db_ref):
        i = pl.program_id(0)
        yv = y_ref[...]
        mu = jnp.mean(yv, axis=-1, keepdims=True)
        xc = yv - mu
        var = jnp.mean(xc * xc, axis=-1, keepdims=True)
        r = lax.rsqrt(var + EPS)
        yh = xc * r
        g = g_ref[...]
        z = yh * g + b_ref[...]
        dz = d_ref[...] * _dsilu(z)
        sg = jnp.sum(dz * yh, axis=0, keepdims=True)
        sb = jnp.sum(dz, axis=0, keepdims=True)

        @pl.when(i == 0)
        def _():
            dg_ref[...] = sg
            db_ref[...] = sb

        @pl.when(i > 0)
        def _():
            dg_ref[...] += sg
            db_ref[...] += sb

        dh = dz * g
        dy_ref[...] = r * (dh - jnp.mean(dh, axis=-1, keepdims=True)
                           - yh * jnp.mean(dh * yh, axis=-1, keepdims=True))

    row = pl.BlockSpec((TB, W), lambda i: (i, 0))
    vec = pl.BlockSpec((1, W), lambda i: (0, 0))
    return _pc(body, name=name, grid=(L // TB,), in_specs=[row, row, vec, vec], out_specs=[row, vec, vec],
               out_shape=[jax.ShapeDtypeStruct((L, W), F32), jax.ShapeDtypeStruct((1, W), F32),
                          jax.ShapeDtypeStruct((1, W), F32)],
               compiler_params=_cp(("arbitrary",)))(dact, y, g, b)


def _na_geometry(cfg):
    R = cfg.T // cfg.GW
    nb = R // cfg.NAR
    assert nb >= 3 and cfg.GW == 64 and cfg.NAR == 8
    ks = [int(np.clip(8 * b - 4, 0, R - 16)) for b in range(nb)]
    return R, nb, ks


_NTAB = 18


def _split3(x):
    hi = x.astype(BF16)
    r = x - hi.astype(F32)
    mid = r.astype(BF16)
    lo = (r - mid.astype(F32)).astype(BF16)
    return hi, mid, lo


def _na_col_onehot(cfg):
    GW, NAC = cfg.GW, cfg.NAC
    qc = np.arange(GW)[:, None]
    kc = np.arange(GW)[None, :]
    cs = np.clip(qc - NAC // 2, 0, GW - NAC)
    vcol = (kc >= cs) & (kc < cs + NAC)
    dd = np.clip(kc - qc + NAC - 1, 0, 2 * NAC - 2)
    oh = (np.arange(LANE)[:, None, None] == dd[None]).astype(np.float32)
    z = np.zeros_like(oh)
    oda = np.concatenate([oh, z], axis=2).reshape(LANE, GW * LANE)
    odb = np.concatenate([z, oh], axis=2).reshape(LANE, GW * LANE)
    cm = np.where(np.concatenate([vcol, vcol], axis=1), 0.0, NEG).astype(np.float32).reshape(1, GW * LANE)
    return oda, odb, cm


def na_tables(cfg, name, rpb):
    NH, GW = cfg.NH, cfg.GW
    na = rpb.shape[1]
    oda, odb, cm = _na_col_onehot(cfg)
    rp = jnp.zeros((NH, _NTAB + 1, LANE), F32).at[:, 1:1 + na, :rpb.shape[2]].set(rpb.astype(F32))
    r0 = rp[:, :_NTAB].reshape(NH * _NTAB, LANE)
    r1 = rp[:, 1:].reshape(NH * _NTAB, LANE)
    a = np.arange(_NTAB) - 1
    rm0 = np.where((a >= 0) & (a < na), 0.0, NEG).astype(np.float32)
    rm1 = np.where((a + 1 >= 0) & (a + 1 < na), 0.0, NEG).astype(np.float32)
    half = (np.arange(GW * LANE) % LANE >= GW)[None, :]
    rmask = np.where(half, np.tile(rm1, NH)[:, None], np.tile(rm0, NH)[:, None]).astype(np.float32)
    tn = 2048
    rows = NH * _NTAB

    def body(r0_ref, r1_ref, a_ref, b_ref, cm_ref, rm_ref, o_ref):
        acc = cm_ref[...] + rm_ref[...]
        for t in _split3(r0_ref[...]):
            acc = acc + dot_nn(t, a_ref[...])
        for t in _split3(r1_ref[...]):
            acc = acc + dot_nn(t, b_ref[...])
        o_ref[...] = acc

    rs = pl.BlockSpec((rows, LANE), lambda n: (0, 0))
    out = _pc(body, name=name, grid=(GW * LANE // tn,),
              in_specs=[rs, rs, pl.BlockSpec((LANE, tn), lambda n: (0, n)), pl.BlockSpec((LANE, tn), lambda n: (0, n)),
                        pl.BlockSpec((1, tn), lambda n: (0, n)), pl.BlockSpec((rows, tn), lambda n: (0, n))],
              out_specs=pl.BlockSpec((rows, tn), lambda n: (0, n)),
              out_shape=jax.ShapeDtypeStruct((rows, GW * LANE), F32),
              compiler_params=_cp(("parallel",)))(r0, r1, jnp.asarray(oda, BF16), jnp.asarray(odb, BF16),
                                                  jnp.asarray(cm), jnp.asarray(rmask))
    return out.reshape(NH, _NTAB, GW, LANE)


def _na_tiles(cfg, b):
    R, nb, _ = _na_geometry(cfg)
    NAR = cfg.NAR
    ksb = jnp.clip(8 * b - 4, 0, R - 16)
    for i in range(8):
        qr = 8 * b + i
        ws = jnp.clip(qr - NAR // 2, 0, R - NAR)
        for J in range(8):
            kr0 = ksb + 2 * J
            row = jnp.clip(kr0 - qr + NAR - 1, -1, _NTAB - 2) + 1
            v0 = jnp.logical_and(kr0 >= ws, kr0 < ws + NAR)
            v1 = jnp.logical_and(kr0 + 1 >= ws, kr0 + 1 < ws + NAR)
            yield i, J, row, v0, v1


def _na_fill_bias(cfg, tab_ref, bias, b):
    GW = cfg.GW
    first = lax.broadcasted_iota(jnp.int32, (GW, LANE), 1) < GW
    for i, J, row, v0, v1 in _na_tiles(cfg, b):
        ok = jnp.where(first, v0.astype(jnp.int32), v1.astype(jnp.int32))
        bias[i * GW:(i + 1) * GW, J * LANE:(J + 1) * LANE] = jnp.where(ok > 0, tab_ref[row], NEG)


def _na_specs(cfg):
    R, nb, ks = _na_geometry(cfg)
    off = _offsets(cfg)
    TQ = 8 * cfg.GW
    KP = 4 * cfg.GW
    ks4 = [k // 4 for k in ks]
    lat_blocks = cfg.T // KP

    def ks4_of(b):
        return jnp.clip(2 * b - 1, 0, R // 4 - 4)

    assert all(int(np.clip(2 * b - 1, 0, R // 4 - 4)) == ks4[b] for b in range(nb))
    assert cfg.TC == KP

    def col(nm):
        c0 = off[nm] // LANE
        q = pl.BlockSpec((TQ, LANE), lambda h, b: (b, c0 + h))
        parts = [pl.BlockSpec((KP, LANE), functools.partial(lambda h, b, t: (ks4_of(b) + t, c0 + h), t=t))
                 for t in range(4)]
        ctx = pl.BlockSpec((KP, LANE), lambda h, b: (lat_blocks, c0 + h))
        return q, parts, ctx

    return nb, TQ, KP, ks4_of, col


def na_fwd(cfg, name, P, tab):
    nb, TQ, KP, ks4_of, col = _na_specs(cfg)
    NH = cfg.NH
    scale = cfg.NDH ** -0.5
    qs, _, _ = col("nq")
    _, kparts, kctx = col("nk")
    _, vparts, vctx = col("nv")

    def body(q_ref, k0, k1, k2, k3, kc_ref, v0, v1, v2, v3, vc_ref, tab_ref, o_ref, lse_ref, bias_ref):
        _na_fill_bias(cfg, tab_ref, bias_ref, pl.program_id(1))
        q = (q_ref[...] * scale).astype(BF16)
        kl = jnp.concatenate([k0[...], k1[...], k2[...], k3[...]], axis=0).astype(BF16)
        vl = jnp.concatenate([v0[...], v1[...], v2[...], v3[...]], axis=0).astype(BF16)
        kc = kc_ref[...].astype(BF16)
        vc = vc_ref[...].astype(BF16)
        sl = dot_nt(q, kl) + bias_ref[...]
        sc = dot_nt(q, kc)
        m = jnp.maximum(jnp.max(sl, axis=-1, keepdims=True), jnp.max(sc, axis=-1, keepdims=True))
        pl_ = jnp.exp(sl - m)
        pc = jnp.exp(sc - m)
        den = jnp.sum(pl_, axis=-1, keepdims=True) + jnp.sum(pc, axis=-1, keepdims=True)
        o = dot_nn(pl_.astype(BF16), vl) + dot_nn(pc.astype(BF16), vc)
        o_ref[...] = o / den
        lse_ref[...] = m + jnp.log(den)

    return _pc(body, name=name, grid=(NH, nb),
               in_specs=[qs, *kparts, kctx, *vparts, vctx,
                         pl.BlockSpec((None, _NTAB, cfg.GW, LANE), lambda h, b: (h, 0, 0, 0))],
               out_specs=[pl.BlockSpec((TQ, LANE), lambda h, b: (b, h)),
                          pl.BlockSpec((None, TQ, 1), lambda h, b: (h, b, 0))],
               out_shape=[jax.ShapeDtypeStruct((cfg.T, NH * LANE), F32),
                          jax.ShapeDtypeStruct((NH, cfg.T, 1), F32)],
               scratch_shapes=[pltpu.VMEM((TQ, 4 * KP), F32)],
               compiler_params=_cp(("parallel", "parallel")))(P, *([P] * 5), *([P] * 5), tab)


def na_bwd(cfg, name, P, tab, o, lse, dmix, dcol0):
    nb, TQ, KP, ks4_of, col = _na_specs(cfg)
    NH, GW = cfg.NH, cfg.GW
    L = P.shape[0]
    scale = cfg.NDH ** -0.5
    qs, _, _ = col("nq")
    _, kparts, kctx = col("nk")
    _, vparts, vctx = col("nv")

    def body(q_ref, k0, k1, k2, k3, kc_ref, v0, v1, v2, v3, vc_ref, tab_ref, o_ref, lse_ref, do_ref,
             dq_ref, dk_ref, dv_ref, dtab_ref, bias_ref):
        b = pl.program_id(1)

        @pl.when(b == 0)
        def _():
            dk_ref[...] = jnp.zeros_like(dk_ref)
            dv_ref[...] = jnp.zeros_like(dv_ref)
            dtab_ref[...] = jnp.zeros_like(dtab_ref)

        _na_fill_bias(cfg, tab_ref, bias_ref, b)

        q = (q_ref[...] * scale).astype(BF16)
        kl = jnp.concatenate([k0[...], k1[...], k2[...], k3[...]], axis=0).astype(BF16)
        vl = jnp.concatenate([v0[...], v1[...], v2[...], v3[...]], axis=0).astype(BF16)
        kc = kc_ref[...].astype(BF16)
        vc = vc_ref[...].astype(BF16)
        lse = lse_ref[...]
        do = do_ref[...]
        dob = do.astype(BF16)
        p_l = jnp.exp(dot_nt(q, kl) + bias_ref[...] - lse)
        p_c = jnp.exp(dot_nt(q, kc) - lse)
        delta = jnp.sum(do * o_ref[...], axis=-1, keepdims=True)
        ds_l = p_l * (dot_nt(dob, vl) - delta)
        ds_c = p_c * (dot_nt(dob, vc) - delta)
        dslb = ds_l.astype(BF16)
        dscb = ds_c.astype(BF16)
        dq_ref[...] = ((dot_nn(dslb, kl) + dot_nn(dscb, kc)) * scale).astype(dq_ref.dtype)
        r0 = pl.multiple_of(ks4_of(b) * KP, KP)
        dk_ref[pl.ds(r0, 4 * KP), :] += dot_tn(dslb, q)
        dv_ref[pl.ds(r0, 4 * KP), :] += dot_tn(p_l.astype(BF16), dob)
        dk_ref[cfg.T:cfg.T + KP, :] += dot_tn(dscb, q)
        dv_ref[cfg.T:cfg.T + KP, :] += dot_tn(p_c.astype(BF16), dob)
        bias_ref[...] = ds_l
        for i, J, row, _, _ in _na_tiles(cfg, b):
            dtab_ref[row] += bias_ref[i * GW:(i + 1) * GW, J * LANE:(J + 1) * LANE]

    full = pl.BlockSpec((L, LANE), lambda h, b: (0, h))
    tabs = pl.BlockSpec((None, _NTAB, GW, LANE), lambda h, b: (h, 0, 0, 0))
    return _pc(body, name=name, grid=(NH, nb),
               in_specs=[qs, *kparts, kctx, *vparts, vctx, tabs,
                         pl.BlockSpec((TQ, LANE), lambda h, b: (b, h)),
                         pl.BlockSpec((None, TQ, 1), lambda h, b: (h, b, 0)),
                         pl.BlockSpec((TQ, LANE), lambda h, b: (b, dcol0 + h))],
               out_specs=[pl.BlockSpec((TQ, LANE), lambda h, b: (b, h)), full, full, tabs],
               out_shape=[jax.ShapeDtypeStruct((cfg.T, NH * LANE), BF16),
                          jax.ShapeDtypeStruct((L, NH * LANE), F32), jax.ShapeDtypeStruct((L, NH * LANE), F32),
                          jax.ShapeDtypeStruct((NH, _NTAB, GW, LANE), F32)],
               scratch_shapes=[pltpu.VMEM((TQ, 4 * KP), F32)],
               compiler_params=_cp(("parallel", "arbitrary")))(
                   P, *([P] * 5), *([P] * 5), tab, o, lse, dmix)


def na_ctx_fwd(cfg, name, P):
    off = _offsets(cfg)
    TC, NH = cfg.TC, cfg.NH
    rb = cfg.T // TC
    scale = cfg.NDH ** -0.5

    def body(q_ref, k_ref, v_ref, o_ref, lse_ref):
        q = (q_ref[...] * scale).astype(BF16)
        s = dot_nt(q, k_ref[...].astype(BF16))
        m = jnp.max(s, axis=-1, keepdims=True)
        p = jnp.exp(s - m)
        den = jnp.sum(p, axis=-1, keepdims=True)
        o_ref[...] = dot_nn(p.astype(BF16), v_ref[...].astype(BF16)) / den
        lse_ref[...] = m + jnp.log(den)

    spec = lambda nm: pl.BlockSpec((TC, LANE), functools.partial(lambda h, c0: (rb, c0 + h), c0=off[nm] // LANE))
    return _pc(body, name=name, grid=(NH,), in_specs=[spec("nq"), spec("nk"), spec("nv")],
               out_specs=[pl.BlockSpec((TC, LANE), lambda h: (0, h)), pl.BlockSpec((None, TC, 1), lambda h: (h, 0, 0))],
               out_shape=[jax.ShapeDtypeStruct((TC, NH * LANE), F32), jax.ShapeDtypeStruct((NH, TC, 1), F32)],
               compiler_params=_cp(("parallel",)))(P, P, P)


def na_ctx_bwd(cfg, name, P, o, lse, dmix, dcol0, dk_in, dv_in):
    off = _offsets(cfg)
    TC, NH = cfg.TC, cfg.NH
    rb = cfg.T // TC
    scale = cfg.NDH ** -0.5

    def body(q_ref, k_ref, v_ref, o_ref, lse_ref, do_ref, dki_ref, dvi_ref, dq_ref, dk_ref, dv_ref):
        q = (q_ref[...] * scale).astype(BF16)
        kb = k_ref[...].astype(BF16)
        vb = v_ref[...].astype(BF16)
        do = do_ref[...]
        dob = do.astype(BF16)
        p = jnp.exp(dot_nt(q, kb) - lse_ref[...])
        delta = jnp.sum(do * o_ref[...], axis=-1, keepdims=True)
        ds = (p * (dot_nt(dob, vb) - delta)).astype(BF16)
        dq_ref[...] = (dot_nn(ds, kb) * scale).astype(dq_ref.dtype)
        dk_ref[...] = (dki_ref[...] + dot_tn(ds, q)).astype(dk_ref.dtype)
        dv_ref[...] = (dvi_ref[...] + dot_tn(p.astype(BF16), dob)).astype(dv_ref.dtype)

    spec = lambda nm: pl.BlockSpec((TC, LANE), functools.partial(lambda h, c0: (rb, c0 + h), c0=off[nm] // LANE))
    hb = pl.BlockSpec((TC, LANE), lambda h: (0, h))
    ctxrow = pl.BlockSpec((TC, LANE), lambda h: (rb, h))
    shp = jax.ShapeDtypeStruct((TC, NH * LANE), BF16)
    return _pc(body, name=name, grid=(NH,),
               in_specs=[spec("nq"), spec("nk"), spec("nv"), hb, pl.BlockSpec((None, TC, 1), lambda h: (h, 0, 0)),
                         pl.BlockSpec((TC, LANE), lambda h: (rb, dcol0 + h)), ctxrow, ctxrow],
               out_specs=[hb, hb, hb], out_shape=[shp, shp, shp],
               compiler_params=_cp(("parallel",)))(P, P, P, o, lse, dmix, dk_in, dv_in)


def na_rpb_grad(cfg, name, dtab):
    NH, GW = cfg.NH, cfg.GW
    na, nd = 2 * cfg.NAR - 1, 2 * cfg.NAC - 1
    oda, odb, _ = _na_col_onehot(cfg)
    E = np.concatenate([oda.T, odb.T], axis=1)
    rows = NH * _NTAB

    def body(z_ref, e_ref, o_ref):
        zv = z_ref[...]
        hi = zv.astype(BF16)
        lo = (zv - hi.astype(F32)).astype(BF16)
        e = e_ref[...]
        o_ref[...] = dot_nn(hi, e) + dot_nn(lo, e)

    g = _pc(body, name=name, out_shape=jax.ShapeDtypeStruct((rows, 2 * LANE), F32),
            compiler_params=_cp())(dtab.reshape(rows, GW * LANE), jnp.asarray(E, BF16))
    g = g.reshape(NH, _NTAB, 2, LANE)
    return g[:, 1:1 + na, 0, :nd] + g[:, 0:na, 1, :nd]


def _seq_tiles(cfg):
    T, TC, TB = cfg.T, cfg.TC, cfg.TB
    tiles = []
    for i in range((T + TC) // TB):
        r0 = i * TB
        tiles.append((r0, r0 == 0 or r0 == T, r0 + TB == T or r0 + TB == T + TC))
    return tiles


def _shift3(ref_get, r0, TB, start, end, width):
    cur = ref_get(r0, TB)
    if start or end:
        rowi = lax.broadcasted_iota(jnp.int32, (TB, width), 0)
    up = jnp.where(rowi == 0, 0.0, pltpu.roll(cur, 1, 0)) if start else ref_get(r0 - 1, TB)
    dn = jnp.where(rowi == TB - 1, 0.0, pltpu.roll(cur, TB - 1, 0)) if end else ref_get(r0 + 1, TB)
    return up, cur, dn


def ffn_act_fwd(cfg, name, U2, w, b):
    _, L, DFF = U2.shape
    TB = cfg.TB
    tiles = _seq_tiles(cfg)

    def body(u_ref, w_ref, b_ref, a_ref):
        def plane(p, r0, st, en):
            up, cur, dn = _shift3(lambda r, n: u_ref[p, r:r + n, :], r0, TB, st, en, LANE)
            wv = w_ref[p]
            return wv[0:1, :] * up + wv[1:2, :] * cur + wv[2:3, :] * dn + b_ref[p]

        for r0, st, en in tiles:
            val = plane(0, r0, st, en)
            gate = plane(1, r0, st, en)
            a_ref[r0:r0 + TB, :] = (_silu(gate) * val).astype(a_ref.dtype)

    return _pc(body, name=name, grid=(DFF // LANE,),
               in_specs=[pl.BlockSpec((2, L, LANE), lambda j: (0, 0, j)),
                         pl.BlockSpec((2, 8, LANE), lambda j: (0, 0, j)),
                         pl.BlockSpec((2, 1, LANE), lambda j: (0, 0, j))],
               out_specs=pl.BlockSpec((L, LANE), lambda j: (0, j)),
               out_shape=jax.ShapeDtypeStruct((L, DFF), BF16),
               compiler_params=_cp(("parallel",)))(U2, w, b)


def ffn_act_bwd(cfg, name, U2, w, b, dA):
    _, L, DFF = U2.shape
    TB = cfg.TB
    tiles = _seq_tiles(cfg)

    def body(u_ref, w_ref, b_ref, da_ref, du_ref, dw_ref, db_ref, dbuf):
        dw_ref[...] = jnp.zeros_like(dw_ref)
        db_ref[...] = jnp.zeros_like(db_ref)
        for r0, st, en in tiles:
            shifted = []
            pre = []
            for p in range(2):
                up, cur, dn = _shift3(lambda r, n: u_ref[p, r:r + n, :], r0, TB, st, en, LANE)
                wv = w_ref[p]
                shifted.append((up, cur, dn))
                pre.append(wv[0:1, :] * up + wv[1:2, :] * cur + wv[2:3, :] * dn + b_ref[p])
            val, gate = pre
            da = da_ref[r0:r0 + TB, :]
            dpre = (da * _silu(gate), da * val * _dsilu(gate))
            for p in range(2):
                dbuf[p, r0:r0 + TB, :] = dpre[p]
                for k in range(3):
                    dw_ref[p, k:k + 1, :] += jnp.sum(dpre[p] * shifted[p][k], axis=0, keepdims=True)
                db_ref[p] += jnp.sum(dpre[p], axis=0, keepdims=True)
        for r0, st, en in tiles:
            for p in range(2):
                up, cur, dn = _shift3(lambda r, n: dbuf[p, r:r + n, :], r0, TB, st, en, LANE)
                wv = w_ref[p]
                du_ref[p, r0:r0 + TB, :] = (wv[0:1, :] * dn + wv[1:2, :] * cur + wv[2:3, :] * up).astype(du_ref.dtype)

    blk = pl.BlockSpec((2, L, LANE), lambda j: (0, 0, j))
    wspec = pl.BlockSpec((2, 8, LANE), lambda j: (0, 0, j))
    bspec = pl.BlockSpec((2, 1, LANE), lambda j: (0, 0, j))
    return _pc(body, name=name, grid=(DFF // LANE,),
               in_specs=[blk, wspec, bspec, pl.BlockSpec((L, LANE), lambda j: (0, j))],
               out_specs=[blk, wspec, bspec],
               out_shape=[jax.ShapeDtypeStruct((2, L, DFF), BF16), jax.ShapeDtypeStruct((2, 8, DFF), F32),
                          jax.ShapeDtypeStruct((2, 1, DFF), F32)],
               scratch_shapes=[pltpu.VMEM((2, L, LANE), F32)],
               compiler_params=_cp(("parallel",)))(U2, w, b, dA)


def _tm(L, parts):
    assert L % parts == 0
    return L // parts


def layer_fwd(cfg, l, XS, mod, wts, small, tabs):
    L, D = XS.shape
    off = _offsets(cfg)
    DIN = off["end"]
    Win4, Wout4, Wup4, Wdn4, Wpw4 = wts["w_in"], wts["w_out"], wts["ffn_up"], wts["ffn_down"], wts["conv_pw"]
    nbi = Win4.shape[3]
    tmA = _tm(L, 4)
    sv = {}
    sv["XS"] = XS
    h1 = norm_mod_fwd(cfg, f"norm1_fwd_{l}", XS, small["norm1_g"], mod["sc1"], mod["sh1"])
    P = matmul(f"mm_in_{l}", h1, Win4, contract="nn", grid=(4, L // tmA),
               a_spec=pl.BlockSpec((tmA, D), lambda n, m: (m, 0)),
               b_spec=pl.BlockSpec((None, None, D, nbi), lambda n, m: (n, l, 0, 0)),
               out_shape=jax.ShapeDtypeStruct((L, DIN), F32),
               out_spec=pl.BlockSpec((tmA, nbi), lambda n, m: (m, n)), nk=1)
    qr, kr = rope_fwd(cfg, f"rope_fwd_{l}", P, tabs["cos"], tabs["sin"])
    o_f, o_b, st = retention_fwd(cfg, f"ret_fwd_{l}", qr, kr, P, small["lam"])
    o2 = (o_f, o_b)
    ret = ggn_fwd(cfg, f"ggn_fwd_{l}", o2, P, small["ret_gn_g"])
    ycv = glu_dwconv_fwd(cfg, f"dwconv_fwd_{l}", P, small["conv_dw_w"], small["conv_dw_b"])
    act = ln_silu_fwd(cfg, f"ln_silu_fwd_{l}", ycv, small["conv_ln_g"], small["conv_ln_b"])
    cv = mm_rowsharded(f"mm_pw_{l}", act, Wpw4, l, BF16, cfg.CW)
    bias = na_tables(cfg, f"na_tables_{l}", small["na_rpb"])
    na_l, lse = na_fwd(cfg, f"na_fwd_{l}", P, bias)
    na_c, lse_c = na_ctx_fwd(cfg, f"na_ctx_fwd_{l}", P)
    mix = jnp.concatenate([ret, cv, jnp.concatenate([na_l, na_c], axis=0).astype(BF16)], axis=1)
    Y1 = mm_rowsharded(f"mm_out_{l}", mix, Wout4, l, F32, D)
    XM = resid_fwd(cfg, f"resid1_fwd_{l}", XS, Y1, mod["g1"])
    h2 = norm_mod_fwd(cfg, f"norm2_fwd_{l}", XM, small["norm2_g"], mod["sc2"], mod["sh2"])
    nbu = Wup4.shape[3]
    tnu = nbu // 2
    U2 = matmul(f"mm_up_{l}", h2, Wup4, contract="nn", grid=(8, L // tmA),
                a_spec=pl.BlockSpec((tmA, D), lambda n, m: (m, 0)),
                b_spec=pl.BlockSpec((None, None, D, tnu), lambda n, m: (n // 2, l, 0, n % 2)),
                out_shape=jax.ShapeDtypeStruct((2, L, cfg.DFF), F32),
                out_spec=pl.BlockSpec((None, tmA, tnu), lambda n, m: (n // 4, m, n % 4)), nk=1)
    A = ffn_act_fwd(cfg, f"ffn_act_fwd_{l}", U2, small["ffn_dw_w"], small["ffn_dw_b"])
    Y2 = mm_rowsharded(f"mm_down_{l}", A, Wdn4, l, F32, D // 2)
    XO = resid_fwd(cfg, f"resid2_fwd_{l}", XM, Y2, mod["g2"])
    sv.update(h1=h1, P=P, qr=qr, kr=kr, o2=o2, st=st, ycv=ycv, act=act, bias=bias, na_l=na_l, lse=lse,
              na_c=na_c, lse_c=lse_c, mix=mix, Y1=Y1, XM=XM, h2=h2, U2=U2, A=A, Y2=Y2)
    return XO, sv


def layer_bwd(cfg, l, dXO, sv, mod, wts, small, tabs, gbuf):
    L, D = dXO.shape
    off = _offsets(cfg)
    DIN = off["end"]
    Win4, Wout4, Wup4, Wdn4, Wpw4 = wts["w_in"], wts["w_out"], wts["ffn_up"], wts["ffn_down"], wts["conv_pw"]
    tmA, tmB = _tm(L, 4), _tm(L, 8)
    depth = Win4.shape[1]
    gb, gs, dm = {}, {}, {}
    P = sv["P"]
    dY2, dm["g2"] = resid_bwd(cfg, f"resid2_bwd_{l}", dXO, sv["Y2"], mod["g2"])
    nbd = Wdn4.shape[2]
    dA = matmul(f"mm_down_da_{l}", dY2, Wdn4, contract="nt", grid=(4, L // tmA),
                a_spec=pl.BlockSpec((tmA, D), lambda j, m: (m, 0)),
                b_spec=pl.BlockSpec((None, None, nbd, D), lambda j, m: (j, l, 0, 0)),
                out_shape=jax.ShapeDtypeStruct((L, cfg.DFF), F32),
                out_spec=pl.BlockSpec((tmA, nbd), lambda j, m: (m, j)), nk=1)
    gb["ffn_down"] = wgrad(cfg, f"mm_down_dw_{l}", sv["A"], dY2,
                           lambda rb, ri: pl.BlockSpec((rb, nbd), lambda j, m: (ri(m), j)),
                           lambda rb, ri: pl.BlockSpec((rb, D), lambda j, m: (ri(m), 0)),
                           jax.ShapeDtypeStruct((depth, 4, nbd, D), BF16),
                           pl.BlockSpec((None, None, nbd, D), lambda j, m: (l, j, 0, 0)), 4, gbuf.get("ffn_down"))
    dU2, dfw, dfb = ffn_act_bwd(cfg, f"ffn_act_bwd_{l}", sv["U2"], small["ffn_dw_w"], small["ffn_dw_b"], dA)
    gs["ffn_dw_w"], gs["ffn_dw_b"] = dfw, dfb
    nbu = Wup4.shape[3]
    tnu = nbu // 2
    dH2 = matmul(f"mm_up_dh_{l}", dU2, Wup4, contract="nt", grid=(L // tmA, 8),
                 a_spec=pl.BlockSpec((None, tmA, tnu), lambda m, n: (n // 4, m, n % 4)),
                 b_spec=pl.BlockSpec((None, None, D, tnu), lambda m, n: (n // 2, l, 0, n % 2)),
                 out_shape=jax.ShapeDtypeStruct((L, D), F32),
                 out_spec=pl.BlockSpec((tmA, D), lambda m, n: (m, 0)), nk=8)
    gb["ffn_up"] = wgrad(cfg, f"mm_up_dw_{l}", sv["h2"], dU2,
                         lambda rb, ri: pl.BlockSpec((rb, D), lambda n, m: (ri(m), 0)),
                         lambda rb, ri: pl.BlockSpec((None, rb, tnu), lambda n, m: (n // 4, ri(m), n % 4)),
                         jax.ShapeDtypeStruct((depth, 4, D, nbu), BF16),
                         pl.BlockSpec((None, None, D, tnu), lambda n, m: (l, n // 2, 0, n % 2)), 8, gbuf.get("ffn_up"))
    dXM, dm["sc2"], dm["sh2"], gs["norm2_g"] = norm_mod_bwd(
        cfg, f"norm2_bwd_{l}", dH2, sv["XM"], small["norm2_g"], mod["sc2"], dXO)
    dY1, dm["g1"] = resid_bwd(cfg, f"resid1_bwd_{l}", dXM, sv["Y1"], mod["g1"])
    nbo = Wout4.shape[2]
    dmix = matmul(f"mm_out_dmix_{l}", dY1, Wout4, contract="nt", grid=(4, L // tmA),
                  a_spec=pl.BlockSpec((tmA, D), lambda j, m: (m, 0)),
                  b_spec=pl.BlockSpec((None, None, nbo, D), lambda j, m: (j, l, 0, 0)),
                  out_shape=jax.ShapeDtypeStruct((L, D), F32),
                  out_spec=pl.BlockSpec((tmA, nbo), lambda j, m: (m, j)), nk=1)
    gb["w_out"] = wgrad(cfg, f"mm_out_dw_{l}", sv["mix"], dY1,
                        lambda rb, ri: pl.BlockSpec((rb, nbo), lambda j, m: (ri(m), j)),
                        lambda rb, ri: pl.BlockSpec((rb, D), lambda j, m: (ri(m), 0)),
                        jax.ShapeDtypeStruct((depth, 4, nbo, D), BF16),
                        pl.BlockSpec((None, None, nbo, D), lambda j, m: (l, j, 0, 0)), 4, gbuf.get("w_out"))
    RW = cfg.RH * cfg.RDV
    do, dlg, gs["ret_gn_g"] = ggn_bwd(cfg, f"ggn_bwd_{l}", dmix, sv["o2"], P, small["ret_gn_g"], 0)
    dqf, dqb, dkf, dkb, dvf, dvb, dlam = retention_bwd(
        cfg, f"ret_bwd_{l}", sv["qr"], sv["kr"], P, small["lam"], sv["st"], do)
    gs["lam"] = dlam[:, :, 0, 0]
    dlq, dlk = rope_bwd(cfg, f"rope_bwd_{l}", (dqf, dqb), (dkf, dkb), tabs["cos"], tabs["sin"])
    dlv = add_cast(cfg, f"ret_dv_{l}", dvf, dvb)
    dcv = cast_cols(cfg, f"conv_dcv_{l}", dmix, RW // LANE, cfg.CW // LANE, LANE)
    nbp = Wpw4.shape[2]
    dact = matmul(f"mm_pw_dact_{l}", dcv, Wpw4, contract="nt", grid=(4, L // tmA),
                  a_spec=pl.BlockSpec((tmA, cfg.CW), lambda j, m: (m, 0)),
                  b_spec=pl.BlockSpec((None, None, nbp, cfg.CW), lambda j, m: (j, l, 0, 0)),
                  out_shape=jax.ShapeDtypeStruct((L, cfg.CW), F32),
                  out_spec=pl.BlockSpec((tmA, nbp), lambda j, m: (m, j)), nk=1)
    gb["conv_pw"] = wgrad(cfg, f"mm_pw_dw_{l}", sv["act"], dcv,
                          lambda rb, ri: pl.BlockSpec((rb, nbp), lambda j, m: (ri(m), j)),
                          lambda rb, ri: pl.BlockSpec((rb, cfg.CW), lambda j, m: (ri(m), 0)),
                          jax.ShapeDtypeStruct((depth, 4, nbp, cfg.CW), BF16),
                          pl.BlockSpec((None, None, nbp, cfg.CW), lambda j, m: (l, j, 0, 0)), 4, gbuf.get("conv_pw"))
    dycv, gs["conv_ln_g"], gs["conv_ln_b"] = ln_silu_bwd(
        cfg, f"ln_silu_bwd_{l}", dact, sv["ycv"], small["conv_ln_g"], small["conv_ln_b"])
    dla, dlb, gs["conv_dw_w"], gs["conv_dw_b"] = glu_dwconv_bwd(cfg, f"dwconv_bwd_{l}", P, small["conv_dw_w"], dycv)
    nac0 = (RW + cfg.CW) // LANE
    dnq_l, dnk, dnv, dsb = na_bwd(cfg, f"na_bwd_{l}", P, sv["bias"], sv["na_l"], sv["lse"], dmix, nac0)
    dnq_c, dnk_c, dnv_c = na_ctx_bwd(cfg, f"na_ctx_bwd_{l}", P, sv["na_c"], sv["lse_c"], dmix, nac0, dnk, dnv)
    gs["na_rpb"] = na_rpb_grad(cfg, f"na_rpb_{l}", dsb)
    dnq = jnp.concatenate([dnq_l, dnq_c], axis=0)
    dnk = jnp.concatenate([dnk[:cfg.T].astype(BF16), dnk_c], axis=0)
    dnv = jnp.concatenate([dnv[:cfg.T].astype(BF16), dnv_c], axis=0)
    dP = jnp.concatenate([dlq, dlk, dlv, dlg, dla, dlb, dnq, dnk, dnv], axis=1)
    nbi = Win4.shape[3]
    dH1 = matmul(f"mm_in_dh_{l}", dP, Win4, contract="nt", grid=(L // tmA, 4),
                 a_spec=pl.BlockSpec((tmA, nbi), lambda m, n: (m, n)),
                 b_spec=pl.BlockSpec((None, None, D, nbi), lambda m, n: (n, l, 0, 0)),
                 out_shape=jax.ShapeDtypeStruct((L, D), F32),
                 out_spec=pl.BlockSpec((tmA, D), lambda m, n: (m, 0)), nk=4)
    gb["w_in"] = wgrad(cfg, f"mm_in_dw_{l}", sv["h1"], dP,
                       lambda rb, ri: pl.BlockSpec((rb, D), lambda n, m: (ri(m), 0)),
                       lambda rb, ri: pl.BlockSpec((rb, nbi), lambda n, m: (ri(m), n)),
                       jax.ShapeDtypeStruct((depth, 4, D, nbi), BF16),
                       pl.BlockSpec((None, None, D, nbi), lambda n, m: (l, n, 0, 0)), 4, gbuf.get("w_in"))
    dXS, dm["sc1"], dm["sh1"], gs["norm1_g"] = norm_mod_bwd(
        cfg, f"norm1_bwd_{l}", dH1, sv["XS"], small["norm1_g"], mod["sc1"], dXM)
    return dXS, gb, gs, dm


def _layer_small(cfg, l, sp):
    DFF = cfg.DFF
    fw = sp["ffn_dw_w"][l].reshape(3, 2, DFF).transpose(1, 0, 2)
    fw = jnp.concatenate([fw, jnp.zeros((2, 5, DFF), F32)], axis=1)
    cw = jnp.concatenate([sp["conv_dw_w"][l], jnp.zeros((32 - cfg.CK, cfg.CW), F32)], axis=0)
    return dict(
        norm1_g=sp["norm1_g"][l][None], norm2_g=sp["norm2_g"][l][None],
        lam=jax.nn.log_sigmoid(sp["ret_decay"][l]), ret_gn_g=sp["ret_gn_g"][l][None],
        conv_dw_w=cw, conv_dw_b=sp["conv_dw_b"][l][None], conv_ln_g=sp["conv_ln_g"][l][None],
        conv_ln_b=sp["conv_ln_b"][l][None], na_rpb=sp["na_rpb"][l],
        ffn_dw_w=fw, ffn_dw_b=sp["ffn_dw_b"][l].reshape(2, 1, DFF))


def local_step(cfg, x, ctx, tgt, mods, wts, sp):
    depth = sp["norm1_g"].shape[0]
    cos, sin = rope_tables(cfg)
    tabs = dict(cos=cos, sin=sin)
    XS = jnp.concatenate([x, ctx], axis=0)
    smalls = [_layer_small(cfg, l, sp) for l in range(depth)]
    saves, lw = [], []
    for l in range(depth):
        lw.append(wts(l, XS))
        XS, sv = layer_fwd(cfg, l, XS, mods[l], lw[l], smalls[l], tabs)
        saves.append(sv)
    ls, dX, dfg = final_loss(cfg, "final_loss", XS, sp["final_g"][None], tgt)
    gb, gss, dms = {}, [None] * depth, [None] * depth
    for l in reversed(range(depth)):
        dX, gb, gss[l], dms[l] = layer_bwd(cfg, l, dX, saves[l], mods[l], lw[l], smalls[l], tabs, gb)
    return ls[0, 0], dX[:cfg.T], gb, gss, dms, dfg[0]


MESH = pl.DeviceIdType.MESH
N_DEV = 8
N_CHIP = 4
BIG = ("w_in", "w_out", "ffn_up", "ffn_down", "conv_pw")
_ANY = pl.BlockSpec(memory_space=pl.ANY)


def _place():
    x, y, c = lax.axis_index("x"), lax.axis_index("y"), lax.axis_index("c")
    chips = [(1 - x, y), (x, 1 - y), (1 - x, 1 - y)]
    return x, y, c, chips


def allgather8(name, xs):
    m_per, n = xs.shape

    def body(x_ref, out_ref, send_sems, recv_sems, local_sem):
        x, y, c, chips = _place()
        me, sibling = (x, y, c), (x, y, 1 - c)

        def rows(px, py, pc):
            return out_ref.at[pl.ds((4 * px + 2 * py + pc) * m_per, m_per), :]

        def copy(k, block, to, src=None):
            return pltpu.make_async_remote_copy(
                src_ref=rows(*block) if src is None else src, dst_ref=rows(*block),
                send_sem=send_sems.at[k], recv_sem=recv_sems.at[k], device_id=to, device_id_type=MESH)

        mine = pltpu.make_async_copy(x_ref, rows(*me), local_sem)
        mine.start()
        first = [copy(0, me, sibling, src=x_ref)]
        first += [copy(1 + j, me, (*chip, c), src=x_ref) for j, chip in enumerate(chips)]
        for cp in first:
            cp.start()
        passed = [copy(4 + j, (*chip, c), sibling) for j, chip in enumerate(chips)]
        for j, chip in enumerate(chips):
            copy(1 + j, (*chip, c), me).wait_recv()
            passed[j].start()
        copy(0, sibling, me).wait_recv()
        for j, chip in enumerate(chips):
            copy(4 + j, (*chip, 1 - c), me).wait_recv()
        for cp in first + passed:
            cp.wait_send()
        mine.wait()

    return _pc(body, name=name, out_shape=jax.ShapeDtypeStruct((N_DEV * m_per, n), xs.dtype),
               in_specs=[pl.BlockSpec(memory_space=pltpu.VMEM)], out_specs=pl.BlockSpec(memory_space=pltpu.VMEM),
               scratch_shapes=[pltpu.SemaphoreType.DMA((7,)), pltpu.SemaphoreType.DMA((7,)), pltpu.SemaphoreType.DMA],
               compiler_params=pltpu.CompilerParams(vmem_limit_bytes=VMEM_LIMIT))(xs)


def _wpiece(ref, layer, chip_idx, half):
    rh = ref.shape[2] // 2
    return ref.at[chip_idx, layer, pl.ds(half * rh, rh)]


def _wcopy(ref, layer, chip_idx, half, send_sems, recv_sems, k, to):
    piece = _wpiece(ref, layer, chip_idx, half)
    return pltpu.make_async_remote_copy(src_ref=piece, dst_ref=piece, send_sem=send_sems.at[k],
                                        recv_sem=recv_sems.at[k], device_id=to, device_id_type=MESH)


def _w_ici_sends(outs, layer, send_sems, recv_sems):
    x, y, c, chips = _place()
    return [_wcopy(outs[a], layer, 2 * x + y, c, send_sems, recv_sems, 3 * a + t, (*chip, c))
            for a in range(len(outs)) for t, chip in enumerate(chips)]


def _w_ici_landed(outs, layer, send_sems, recv_sems):
    x, y, c, chips = _place()
    return [_wcopy(outs[a], layer, 2 * chip[0] + chip[1], c, send_sems, recv_sems, 3 * a + t, (x, y, c))
            for a in range(len(outs)) for t, chip in enumerate(chips)]


def _w_forward(outs, layer, send_sems, recv_sems, base):
    x, y, c, chips = _place()
    n = len(outs)
    sends = [_wcopy(outs[a], layer, 2 * chip[0] + chip[1], c, send_sems, recv_sems, base + 3 * a + t, (x, y, 1 - c))
             for a in range(n) for t, chip in enumerate(chips)]
    recvs = [_wcopy(outs[a], layer, 2 * chip[0] + chip[1], 1 - c, send_sems, recv_sems, base + 3 * a + t, (x, y, c))
             for a in range(n) for t, chip in enumerate(chips)]
    return sends, recvs


def allgather_layer(name, bufs, layer):
    n = len(bufs)

    def body(*refs):
        outs = refs[n:2 * n]
        send_sems, recv_sems = refs[2 * n:]
        sent = _w_ici_sends(outs, layer, send_sems, recv_sems)
        for cp in sent:
            cp.start()
        fwd, from_sib = _w_forward(outs, layer, send_sems, recv_sems, 3 * n)
        for landed, fw in zip(_w_ici_landed(outs, layer, send_sems, recv_sems), fwd):
            landed.wait_recv()
            fw.start()
        for cp in from_sib:
            cp.wait_recv()
        for cp in sent + fwd:
            cp.wait_send()

    return _pc(body, name=name, out_shape=[jax.ShapeDtypeStruct(b.shape, b.dtype) for b in bufs],
               in_specs=[_ANY] * n, out_specs=[_ANY] * n, input_output_aliases={a: a for a in range(n)},
               scratch_shapes=[pltpu.SemaphoreType.DMA((6 * n,)), pltpu.SemaphoreType.DMA((6 * n,))])(*bufs)


_HBM = pl.BlockSpec(memory_space=pltpu.HBM)
_SEM = pl.BlockSpec(memory_space=pltpu.SEMAPHORE)
_EFFECT = pltpu.SideEffectType.DATAFLOW_SIDE_EFFECTING


def allgather_layer_start(name, bufs, layer):
    n = len(bufs)

    def body(*refs):
        send_sems, recv_sems = refs[n:n + 2]
        outs = refs[n + 2:2 * n + 2]
        token = refs[2 * n + 2]
        for cp in _w_ici_sends(outs, layer, send_sems, recv_sems):
            cp.start()
        token[...] = jnp.zeros_like(token)

    res = _pc(body, name=name,
              out_shape=(pltpu.SemaphoreType.DMA((3 * n,)), pltpu.SemaphoreType.DMA((3 * n,)),
                         *[pltpu.HBM(b.shape, b.dtype) for b in bufs], jax.ShapeDtypeStruct((8, LANE), F32)),
              in_specs=[_HBM] * n, out_specs=(_SEM, _SEM, *([_HBM] * n), pl.BlockSpec(memory_space=pltpu.VMEM)),
              input_output_aliases={a: a + 2 for a in range(n)},
              compiler_params=pltpu.CompilerParams(has_side_effects=_EFFECT))(
                  *[pltpu.with_memory_space_constraint(b, pltpu.HBM) for b in bufs])
    return res[0], res[1], list(res[2:2 + n]), res[2 + n]


def allgather_layer_wait(name, bufs, send_sems, recv_sems, after, layer):
    n = len(bufs)

    def body(*refs):
        ins = refs[:n]
        send_sems, recv_sems = refs[n:n + 2]
        for cp in _w_ici_sends(ins, layer, send_sems, recv_sems):
            cp.wait_send()
        for cp in _w_ici_landed(ins, layer, send_sems, recv_sems):
            cp.wait_recv()

    return _pc(body, name=name, out_shape=tuple(pltpu.HBM(b.shape, b.dtype) for b in bufs),
               in_specs=[_HBM] * n + [_SEM, _SEM, _ANY], out_specs=tuple([_HBM] * n),
               input_output_aliases={a: a for a in range(n)},
               compiler_params=pltpu.CompilerParams(has_side_effects=_EFFECT))(*bufs, send_sems, recv_sems, after)


def forward_halves(name, bufs, layer):
    n = len(bufs)

    def body(*refs):
        outs = refs[n:2 * n]
        send_sems, recv_sems = refs[2 * n:]
        fwd, from_sib = _w_forward(outs, layer, send_sems, recv_sems, 0)
        for cp in fwd:
            cp.start()
        for cp in from_sib:
            cp.wait_recv()
        for cp in fwd:
            cp.wait_send()

    return _pc(body, name=name, out_shape=[jax.ShapeDtypeStruct(b.shape, b.dtype) for b in bufs],
               in_specs=[_ANY] * n, out_specs=[_ANY] * n, input_output_aliases={a: a for a in range(n)},
               scratch_shapes=[pltpu.SemaphoreType.DMA((3 * n,)), pltpu.SemaphoreType.DMA((3 * n,))])(*bufs)


def exchange_halves(name, grads):
    n = len(grads)

    def body(*refs):
        ins, outs = refs[:n], refs[n:2 * n]
        send_sems, recv_sems = refs[2 * n:]
        x, y, c, _ = _place()
        cps = [pltpu.make_async_remote_copy(
            src_ref=ins[a].at[1 - c], dst_ref=outs[a], send_sem=send_sems.at[a], recv_sem=recv_sems.at[a],
            device_id=(x, y, 1 - c), device_id_type=MESH) for a in range(n)]
        for cp in cps:
            cp.start()
        for cp in cps:
            cp.wait()

    return _pc(body, name=name, out_shape=[jax.ShapeDtypeStruct(g.shape[1:], g.dtype) for g in grads],
               in_specs=[_ANY] * n, out_specs=[_ANY] * n,
               scratch_shapes=[pltpu.SemaphoreType.DMA((n,)), pltpu.SemaphoreType.DMA((n,))])(*grads)


def scatter_slices(name, parts, lands):
    n = len(parts)

    def body(*refs):
        ins, outs = refs[:n], refs[2 * n:3 * n]
        send_sems, recv_sems = refs[3 * n:]
        x, y, c, chips = _place()
        j0 = 2 * x + y
        cps = []
        for a in range(n):
            for t, chip in enumerate(chips):
                jt = 2 * chip[0] + chip[1]
                cps.append(pltpu.make_async_remote_copy(
                    src_ref=ins[a].at[jt], dst_ref=outs[a].at[j0], send_sem=send_sems.at[3 * a + t],
                    recv_sem=recv_sems.at[3 * a + t], device_id=(*chip, c), device_id_type=MESH))
        for cp in cps:
            cp.start()
        k = 0
        for a in range(n):
            for t, chip in enumerate(chips):
                jt = 2 * chip[0] + chip[1]
                pltpu.make_async_remote_copy(
                    src_ref=outs[a].at[jt], dst_ref=outs[a].at[jt], send_sem=send_sems.at[k],
                    recv_sem=recv_sems.at[k], device_id=(x, y, c), device_id_type=MESH).wait_recv()
                k += 1
        for cp in cps:
            cp.wait_send()

    return _pc(body, name=name, out_shape=[jax.ShapeDtypeStruct(p.shape, p.dtype) for p in lands],
               in_specs=[_ANY] * (2 * n), out_specs=[_ANY] * n,
               input_output_aliases={n + a: a for a in range(n)},
               scratch_shapes=[pltpu.SemaphoreType.DMA((3 * n,)), pltpu.SemaphoreType.DMA((3 * n,))])(*parts, *lands)


def share_layers(name, bufs):
    n = len(bufs)

    def body(*refs):
        outs = refs[n:2 * n]
        send_sems, recv_sems = refs[2 * n:]
        x, y, c, _ = _place()
        cps = [pltpu.make_async_remote_copy(
            src_ref=outs[a].at[c], dst_ref=outs[a].at[c], send_sem=send_sems.at[a], recv_sem=recv_sems.at[a],
            device_id=(x, y, 1 - c), device_id_type=MESH) for a in range(n)]
        for cp in cps:
            cp.start()
        for a in range(n):
            pltpu.make_async_remote_copy(
                src_ref=outs[a].at[1 - c], dst_ref=outs[a].at[1 - c], send_sem=send_sems.at[a],
                recv_sem=recv_sems.at[a], device_id=(x, y, c), device_id_type=MESH).wait_recv()
        for cp in cps:
            cp.wait_send()

    return _pc(body, name=name, out_shape=[jax.ShapeDtypeStruct(b.shape, b.dtype) for b in bufs],
               in_specs=[_ANY] * n, out_specs=[_ANY] * n, input_output_aliases={a: a for a in range(n)},
               scratch_shapes=[pltpu.SemaphoreType.DMA((n,)), pltpu.SemaphoreType.DMA((n,))])(*bufs)


def _row_tile(R, C, nbytes=1 << 20):
    t = 8
    while t * 2 <= R and R % (t * 2) == 0 and t * 2 * C * 4 <= nbytes:
        t *= 2
    assert R % t == 0
    return t


def to_bf16_block(name, w, chip_arr, layer):
    _, R, C = w.shape
    tr = _row_tile(R, C)

    def body(j_ref, w_ref, o_ref):
        o_ref[...] = w_ref[...].astype(o_ref.dtype)

    gs = pltpu.PrefetchScalarGridSpec(
        num_scalar_prefetch=1, grid=(R // tr,),
        in_specs=[pl.BlockSpec((None, tr, C), lambda i, j_ref: (layer, i, 0))],
        out_specs=pl.BlockSpec((None, None, tr, C), lambda i, j_ref: (j_ref[0], layer, i, 0)))
    return _pc(body, name=name, grid_spec=gs, out_shape=jax.ShapeDtypeStruct((N_CHIP,) + w.shape, BF16),
               compiler_params=_cp(("parallel",)))(chip_arr, w)


def add_own_layer(name, g, ra, c_arr):
    _, _, R, C = g.shape
    tr = _row_tile(R, C)

    def body(c_ref, g_ref, r_ref, o_ref):
        o_ref[...] = (g_ref[...].astype(F32) + r_ref[...].astype(F32)).astype(o_ref.dtype)

    gs = pltpu.PrefetchScalarGridSpec(
        num_scalar_prefetch=1, grid=(4, R // tr),
        in_specs=[pl.BlockSpec((None, None, tr, C), lambda j, i, c_ref: (c_ref[0], j, i, 0)),
                  pl.BlockSpec((None, tr, C), lambda j, i, c_ref: (j, i, 0))],
        out_specs=pl.BlockSpec((None, tr, C), lambda j, i, c_ref: (j, i, 0)))
    return _pc(body, name=name, grid_spec=gs, out_shape=jax.ShapeDtypeStruct(ra.shape, BF16),
               compiler_params=_cp(("parallel", "parallel")))(c_arr, g, ra)


def own_row(name, part, chip_arr):
    _, R, C = part.shape
    tr = _row_tile(R, C)

    def body(j_ref, p_ref, o_ref):
        o_ref[...] = p_ref[...]

    gs = pltpu.PrefetchScalarGridSpec(
        num_scalar_prefetch=1, grid=(R // tr,),
        in_specs=[pl.BlockSpec((None, tr, C), lambda i, j_ref: (j_ref[0], i, 0))],
        out_specs=pl.BlockSpec((None, tr, C), lambda i, j_ref: (j_ref[0], i, 0)))
    return _pc(body, name=name, grid_spec=gs, out_shape=jax.ShapeDtypeStruct(part.shape, part.dtype),
               compiler_params=_cp(("parallel",)))(chip_arr, part)


def sum_leading(name, g, plane=None):
    n, R, C = g.shape
    tr = _row_tile(R, C, nbytes=(1 << 21) // n)

    def body(*refs):
        g_ref, o_ref = refs[-2:]
        acc = g_ref[0].astype(F32)
        for j in range(1, n):
            acc = acc + g_ref[j].astype(F32)
        o_ref[...] = acc

    if plane is None:
        return _pc(body, name=name, grid=(R // tr,), in_specs=[pl.BlockSpec((n, tr, C), lambda i: (0, i, 0))],
                   out_specs=pl.BlockSpec((tr, C), lambda i: (i, 0)), out_shape=jax.ShapeDtypeStruct((R, C), F32),
                   compiler_params=_cp(("parallel",)))(g)
    count, idx = plane
    gs = pltpu.PrefetchScalarGridSpec(
        num_scalar_prefetch=1, grid=(R // tr,),
        in_specs=[pl.BlockSpec((n, tr, C), lambda i, p_ref: (0, i, 0))],
        out_specs=pl.BlockSpec((None, tr, C), lambda i, p_ref: (p_ref[0], i, 0)))
    return _pc(body, name=name, grid_spec=gs, out_shape=jax.ShapeDtypeStruct((count, R, C), F32),
               compiler_params=_cp(("parallel",)))(idx, g)


def adamw(name, w, g, m, v):
    R, C = w.shape
    tr = _row_tile(R, C)

    def body(w_ref, g_ref, m_ref, v_ref, d_ref, mo_ref, vo_ref):
        gv = g_ref[...]
        mn = ADAM_B1 * m_ref[...] + (1.0 - ADAM_B1) * gv
        vn = ADAM_B2 * v_ref[...] + (1.0 - ADAM_B2) * (gv * gv)
        m_hat = mn / (1.0 - ADAM_B1 ** ADAM_STEP)
        v_hat = vn / (1.0 - ADAM_B2 ** ADAM_STEP)
        d_ref[...] = -ADAM_LR * (m_hat / (jnp.sqrt(v_hat) + ADAM_EPS) + ADAM_WD * w_ref[...])
        mo_ref[...] = mn
        vo_ref[...] = vn

    spec = pl.BlockSpec((tr, C), lambda i: (i, 0))
    shp = jax.ShapeDtypeStruct((R, C), F32)
    return _pc(body, name=name, grid=(R // tr,), in_specs=[spec] * 4, out_specs=[spec] * 3,
               out_shape=[shp] * 3, compiler_params=_cp(("parallel",)))(w, g, m, v)


_ADA_TN = 512


def adaln_fwd(name, cond, w, b):
    _, D, N = w.shape
    tn = min(_ADA_TN, N)

    def body(c_ref, w_ref, b_ref, o_ref):
        s = _silu(c_ref[...]).astype(BF16)
        o_ref[...] = dot_nn(s, w_ref[...].astype(BF16)) + b_ref[...]

    return _pc(body, name=name, grid=(2, N // tn),
               in_specs=[pl.BlockSpec((16, D), lambda l, n: (0, 0)),
                         pl.BlockSpec((None, D, tn), lambda l, n: (l, 0, n)),
                         pl.BlockSpec((None, 1, tn), lambda l, n: (l, 0, n))],
               out_specs=pl.BlockSpec((None, 16, tn), lambda l, n: (l, 0, n)),
               out_shape=jax.ShapeDtypeStruct((2, 16, N), F32),
               compiler_params=_cp(("parallel", "parallel")))(cond, w, b)


def adaln_bwd(name, cond, w, dm):
    _, D, N = w.shape
    tn = min(_ADA_TN, N)

    def body(c_ref, w_ref, dm_ref, gw_ref, ds_ref):
        first = jnp.logical_and(pl.program_id(0) == 0, pl.program_id(1) == 0)
        s = _silu(c_ref[...]).astype(BF16)
        dmb = dm_ref[...].astype(BF16)
        gw_ref[...] = dot_tn(s, dmb)
        p = dot_nt(dmb, w_ref[...].astype(BF16))

        @pl.when(first)
        def _():
            ds_ref[...] = p

        @pl.when(jnp.logical_not(first))
        def _():
            ds_ref[...] += p

    return _pc(body, name=name, grid=(2, N // tn),
               in_specs=[pl.BlockSpec((16, D), lambda l, n: (0, 0)),
                         pl.BlockSpec((None, D, tn), lambda l, n: (l, 0, n)),
                         pl.BlockSpec((None, 16, tn), lambda l, n: (l, 0, n))],
               out_specs=[pl.BlockSpec((None, D, tn), lambda l, n: (l, 0, n)),
                          pl.BlockSpec((16, D), lambda l, n: (0, 0))],
               out_shape=[jax.ShapeDtypeStruct((2, D, N), F32), jax.ShapeDtypeStruct((16, D), F32)],
               compiler_params=_cp(("arbitrary", "arbitrary")))(cond, w, dm)


def cctx_grad(name, parts, c_ctx):
    def body(p_ref, c_ref, o_ref):
        acc = p_ref[0]
        for j in range(1, N_CHIP):
            acc = acc + p_ref[j]
        o_ref[...] = acc * _dsilu(c_ref[...])

    return _pc(body, name=name, out_shape=jax.ShapeDtypeStruct(c_ctx.shape, F32))(parts, c_ctx)


def _pack(arrs, rows_mult=8):
    flat = jnp.concatenate([a.reshape(-1) for a in arrs])
    n = flat.shape[0]
    unit = rows_mult * LANE
    tot = -(-n // unit) * unit
    return jnp.concatenate([flat, jnp.zeros((tot - n,), F32)]).reshape(tot // LANE, LANE)


def _unpack(flat, shapes):
    out, o = [], 0
    for s in shapes:
        n = int(np.prod(s))
        out.append(flat[o:o + n].reshape(s))
        o += n
    return out


MOD_NAMES = ("sh1", "sc1", "g1", "sh2", "sc2", "g2")


def kernel(x, c, ctx, c_ctx, w_ada, b_ada, norm1_g, w_in, ret_decay, ret_gn_g, conv_dw_w, conv_dw_b, conv_ln_g, conv_ln_b, conv_pw, na_rpb, w_out, norm2_g, ffn_up, ffn_dw_w, ffn_dw_b, ffn_down, final_g, loss_target, m_c_ctx, m_w_ada, m_b_ada, m_norm1_g, m_w_in, m_ret_decay, m_ret_gn_g, m_conv_dw_w, m_conv_dw_b, m_conv_ln_g, m_conv_ln_b, m_conv_pw, m_na_rpb, m_w_out, m_norm2_g, m_ffn_up, m_ffn_dw_w, m_ffn_dw_b, m_ffn_down, m_final_g, v_c_ctx, v_w_ada, v_b_ada, v_norm1_g, v_w_in, v_ret_decay, v_ret_gn_g, v_conv_dw_w, v_conv_dw_b, v_conv_ln_g, v_conv_ln_b, v_conv_pw, v_na_rpb, v_w_out, v_norm2_g, v_ffn_up, v_ffn_dw_w, v_ffn_dw_b, v_ffn_down, v_final_g):
    cfg = make_cfg(D=x.shape[2], T=x.shape[1], TC=ctx.shape[1], RH=ret_decay.shape[2], CW=conv_dw_b.shape[1],
                   NH=na_rpb.shape[1], DFF=ffn_dw_b.shape[1] // 2)
    D, T = cfg.D, cfg.T
    W = dict(c_ctx=c_ctx, w_ada=w_ada, b_ada=b_ada, norm1_g=norm1_g, w_in=w_in, ret_decay=ret_decay, ret_gn_g=ret_gn_g,
             conv_dw_w=conv_dw_w, conv_dw_b=conv_dw_b, conv_ln_g=conv_ln_g, conv_ln_b=conv_ln_b, conv_pw=conv_pw,
             na_rpb=na_rpb, w_out=w_out, norm2_g=norm2_g, ffn_up=ffn_up, ffn_dw_w=ffn_dw_w, ffn_dw_b=ffn_dw_b,
             ffn_down=ffn_down, final_g=final_g)
    Mo = dict(c_ctx=m_c_ctx, w_ada=m_w_ada, b_ada=m_b_ada, norm1_g=m_norm1_g, w_in=m_w_in, ret_decay=m_ret_decay,
              ret_gn_g=m_ret_gn_g, conv_dw_w=m_conv_dw_w, conv_dw_b=m_conv_dw_b, conv_ln_g=m_conv_ln_g,
              conv_ln_b=m_conv_ln_b, conv_pw=m_conv_pw, na_rpb=m_na_rpb, w_out=m_w_out, norm2_g=m_norm2_g,
              ffn_up=m_ffn_up, ffn_dw_w=m_ffn_dw_w, ffn_dw_b=m_ffn_dw_b, ffn_down=m_ffn_down, final_g=m_final_g)
    Vo = dict(c_ctx=v_c_ctx, w_ada=v_w_ada, b_ada=v_b_ada, norm1_g=v_norm1_g, w_in=v_w_in, ret_decay=v_ret_decay,
              ret_gn_g=v_ret_gn_g, conv_dw_w=v_conv_dw_w, conv_dw_b=v_conv_dw_b, conv_ln_g=v_conv_ln_g,
              conv_ln_b=v_conv_ln_b, conv_pw=v_conv_pw, na_rpb=v_na_rpb, w_out=v_w_out, norm2_g=v_norm2_g,
              ffn_up=v_ffn_up, ffn_dw_w=v_ffn_dw_w, ffn_dw_b=v_ffn_dw_b, ffn_down=v_ffn_down, final_g=v_final_g)
    order = list(W)
    xi, yi, ci = lax.axis_index("x"), lax.axis_index("y"), lax.axis_index("c")
    chip = 2 * xi + yi
    dev = 4 * xi + 2 * yi + ci
    NA = w_ada.shape[2]
    ncw, nfw = conv_dw_w.shape[2], ffn_dw_w.shape[2]

    g_in = allgather8("ag_small_in", _pack([c[0], conv_dw_w, ffn_dw_w])).reshape(N_DEV, -1)
    c8 = g_in[:, :D]
    by_chip = g_in[0::2, D:]
    cw_parts, fw_parts = [], []
    for j in range(N_CHIP):
        a, b = _unpack(by_chip[j], [conv_dw_w.shape, ffn_dw_w.shape])
        cw_parts.append(a)
        fw_parts.append(b)
    conv_dw_w_full = jnp.concatenate(cw_parts, axis=2)
    ffn_dw_w_full = jnp.concatenate(fw_parts, axis=2)
    cond = jnp.concatenate([c8, c_ctx[None], jnp.zeros((16 - N_DEV - 1, D), F32)], axis=0)

    b_sh = lax.dynamic_slice(b_ada, (0, chip * NA), (2, NA)).reshape(2, 1, NA)
    m_sh = adaln_fwd("adaln_fwd", cond, w_ada, b_sh)
    m_all = allgather8("ag_mod", m_sh.reshape(2 * 16, NA)).reshape(N_DEV, 2, 16, NA)[0::2]
    m_all = m_all.transpose(1, 2, 0, 3).reshape(2, 16, N_CHIP * NA)
    mods = []
    for l in range(2):
        lat = lax.dynamic_slice(m_all[l], (dev, 0), (1, N_CHIP * NA))[0]
        cx = m_all[l, N_DEV]
        mods.append({nm: jnp.stack([lat[k * D:(k + 1) * D], cx[k * D:(k + 1) * D]], 0)[:, None, :]
                     for k, nm in enumerate(MOD_NAMES)})

    c_arr = jnp.reshape(ci, (1,)).astype(jnp.int32)
    chip_arr = jnp.reshape(chip, (1,)).astype(jnp.int32)
    w0 = allgather_layer("ag_weights_0", [to_bf16_block(f"to_bf16_{nm}_0", W[nm], chip_arr, 0) for nm in BIG], 0)
    s_sem, r_sem, w1, token = allgather_layer_start(
        "ag_weights_1_start", [to_bf16_block(f"to_bf16_{nm}_1", W[nm], chip_arr, 1) for nm in BIG], 1)
    mods[0] = {**mods[0], "sc1": mods[0]["sc1"] + token[0, 0]}

    def wts(l, stream):
        if l == 0:
            return dict(zip(BIG, w0))
        landed = allgather_layer_wait("ag_weights_1_wait", w1, s_sem, r_sem, stream, 1)
        return dict(zip(BIG, forward_halves("ag_weights_1_fwd", list(landed), 1)))

    sp = dict(norm1_g=norm1_g, norm2_g=norm2_g, ret_decay=ret_decay, ret_gn_g=ret_gn_g, conv_dw_w=conv_dw_w_full,
              conv_dw_b=conv_dw_b, conv_ln_g=conv_ln_g, conv_ln_b=conv_ln_b, na_rpb=na_rpb, ffn_dw_w=ffn_dw_w_full,
              ffn_dw_b=ffn_dw_b, final_g=final_g)
    loss_l, gx, gb, gss, dms, dfg = local_step(cfg, x[0], ctx[0], loss_target[0], mods, wts, sp)
    loss = lax.psum(loss_l, ("x", "y", "c"))

    dmseg = jnp.stack([jnp.stack([jnp.concatenate([dms[l][nm][r, 0] for nm in MOD_NAMES]) for r in range(2)])
                       for l in range(2)])
    gsm = dict(
        norm1_g=jnp.stack([gss[l]["norm1_g"][0] for l in range(2)]),
        ret_decay=jnp.stack([gss[l]["lam"] * jax.nn.sigmoid(-ret_decay[l]) for l in range(2)]),
        ret_gn_g=jnp.stack([gss[l]["ret_gn_g"][0] for l in range(2)]),
        conv_dw_w=jnp.stack([gss[l]["conv_dw_w"][:cfg.CK] for l in range(2)]),
        conv_dw_b=jnp.stack([gss[l]["conv_dw_b"][0] for l in range(2)]),
        conv_ln_g=jnp.stack([gss[l]["conv_ln_g"][0] for l in range(2)]),
        conv_ln_b=jnp.stack([gss[l]["conv_ln_b"][0] for l in range(2)]),
        na_rpb=jnp.stack([gss[l]["na_rpb"] for l in range(2)]),
        norm2_g=jnp.stack([gss[l]["norm2_g"][0] for l in range(2)]),
        ffn_dw_w=jnp.stack([gss[l]["ffn_dw_w"][:, :3].transpose(1, 0, 2).reshape(3, 2 * cfg.DFF) for l in range(2)]),
        ffn_dw_b=jnp.stack([gss[l]["ffn_dw_b"].reshape(-1) for l in range(2)]),
        final_g=dfg)
    snames = list(gsm)
    sshapes = [dmseg.shape] + [gsm[nm].shape for nm in snames]
    packed = _pack([dmseg] + [gsm[nm] for nm in snames])
    g_all = allgather8("ag_small_grads", packed).reshape(N_DEV, packed.shape[0], LANE)
    summed = sum_leading("sum_small_grads", g_all).reshape(-1)
    dm_sum, *gsum = _unpack(summed, sshapes)
    gfull = dict(zip(snames, gsum))
    ndm = int(np.prod(dmseg.shape))
    dm_all = g_all.reshape(N_DEV, -1)[:, :ndm].reshape(N_DEV, 2, 2, 6 * D)
    gfull["b_ada"] = sum_leading("sum_b_ada", dm_all.transpose(0, 2, 1, 3).reshape(2 * N_DEV, 2 * 6 * D // LANE, LANE)
                                 ).reshape(2, 6 * D)

    dm16 = jnp.concatenate([dm_all[:, :, 0].transpose(1, 0, 2), dm_sum[:, 1][:, None],
                            jnp.zeros((2, 16 - N_DEV - 1, 6 * D), F32)], axis=1)
    dm16 = lax.dynamic_slice(dm16, (0, 0, chip * NA), (2, 16, NA))
    gfull["w_ada"], ds16 = adaln_bwd("adaln_bwd", cond, w_ada, dm16)
    ds_all = allgather8("ag_dsilu", ds16[8:16]).reshape(N_DEV, 8, D)[0::2, 0:1]
    gfull["c_ctx"] = cctx_grad("cctx_grad", ds_all, c_ctx[None])[0]
    gfull["conv_dw_w"] = lax.dynamic_slice(gfull["conv_dw_w"], (0, 0, chip * ncw), (2, cfg.CK, ncw))
    gfull["ffn_dw_w"] = lax.dynamic_slice(gfull["ffn_dw_w"], (0, 0, chip * nfw), (2, 3, nfw))

    from_sib = exchange_halves("rs_exchange", [gb[nm] for nm in BIG])
    part = [add_own_layer(f"rs_add_{nm}", gb[nm], r, c_arr) for nm, r in zip(BIG, from_sib)]
    lands = [own_row(f"rs_own_{nm}", p, chip_arr) for nm, p in zip(BIG, part)]
    landed = scatter_slices("rs_scatter", part, lands)
    done = [sum_leading(f"rs_sum_{nm}", p, plane=(2, c_arr)) for nm, p in zip(BIG, landed)]
    for nm, gfin in zip(BIG, share_layers("rs_share", done)):
        gfull[nm] = gfin

    delta, new_m, new_v = {}, {}, {}
    bigs = ("w_ada",) + BIG
    for nm in bigs:
        shp = W[nm].shape
        v2 = lambda a: a.reshape(-1, shp[-1])
        d_, m_, v_ = adamw(f"adamw_{nm}", v2(W[nm]), v2(gfull[nm]), v2(Mo[nm]), v2(Vo[nm]))
        delta[nm], new_m[nm], new_v[nm] = d_.reshape(shp), m_.reshape(shp), v_.reshape(shp)
    smalls = [nm for nm in order if nm not in bigs]
    shapes = [W[nm].shape for nm in smalls]
    d_, m_, v_ = adamw("adamw_small", _pack([W[nm] for nm in smalls]), _pack([gfull[nm] for nm in smalls]),
                       _pack([Mo[nm] for nm in smalls]), _pack([Vo[nm] for nm in smalls]))
    for nm, a, b, e in zip(smalls, _unpack(d_.reshape(-1), shapes), _unpack(m_.reshape(-1), shapes),
                           _unpack(v_.reshape(-1), shapes)):
        delta[nm], new_m[nm], new_v[nm] = a, b, e
    return (loss, gx[None], *[gfull[nm] for nm in order], *[delta[nm] for nm in order],
            *[new_m[nm] for nm in order], *[new_v[nm] for nm in order])
```

```python
import collections
import functools

import numpy as np
import jax
import jax.numpy as jnp
from jax import lax
from jax.experimental import pallas as pl
from jax.experimental.pallas import tpu as pltpu

F32 = jnp.float32
BF16 = jnp.bfloat16
EPS = 1e-6
ROPE_BASE = 10000.0
NEG = -1e30
LANE = 128
VMEM_LIMIT = 56 * 1024 * 1024

ADAM_LR, ADAM_B1, ADAM_B2, ADAM_EPS, ADAM_WD, ADAM_STEP = 0.001, 0.9, 0.999, 1e-08, 0.01, 10

Cfg = collections.namedtuple(
    "Cfg", "D T TC GW RH RDK RDV CW CK NH NDH NAR NAC DFF TB")


def make_cfg(D=2048, T=4096, TC=256, RH=4, CW=512, NH=4, DFF=5632):
    return Cfg(D=D, T=T, TC=TC, GW=64, RH=RH, RDK=128, RDV=256, CW=CW, CK=31, NH=NH, NDH=128,
               NAR=8, NAC=16, DFF=DFF, TB=256)


def _offsets(cfg):
    sizes = [cfg.RH * cfg.RDK, cfg.RH * cfg.RDK, cfg.RH * cfg.RDV, cfg.RH * cfg.RDV, cfg.CW, cfg.CW,
             cfg.NH * cfg.NDH, cfg.NH * cfg.NDH, cfg.NH * cfg.NDH]
    offs = [0]
    for s in sizes:
        offs.append(offs[-1] + s)
    return dict(zip(["lq", "lk", "lv", "lg", "la", "lb", "nq", "nk", "nv", "end"], offs))


def _pc(body, **kw):
    return pl.pallas_call(body, **kw)


def _cp(sem=None):
    return pltpu.CompilerParams(dimension_semantics=sem, vmem_limit_bytes=VMEM_LIMIT)


def _dot(a, b, ca, cb):
    return lax.dot_general(a, b, (((ca,), (cb,)), ((), ())), preferred_element_type=F32)


def dot_nn(a, b):
    return _dot(a, b, 1, 0)


def dot_nt(a, b):
    return _dot(a, b, 1, 1)


def dot_tn(a, b):
    return _dot(a, b, 0, 0)


def _sigmoid(x):
    return 1.0 / (1.0 + jnp.exp(-x))


def _silu(x):
    return x * _sigmoid(x)


def _dsilu(x):
    s = _sigmoid(x)
    return s * (1.0 + x * (1.0 - s))


def matmul(name, a, b, *, contract, grid, a_spec, b_spec, out_shape, out_spec, nk, into=None):
    dot = {"nn": dot_nn, "nt": dot_nt, "tn": dot_tn}[contract]
    direct = nk > 1 and out_shape.dtype == F32
    kax = len(grid) - 1

    def body(a_ref, b_ref, *rest):
        o_ref, *scr = rest[1:] if into is not None else rest
        p = dot(a_ref[...].astype(BF16), b_ref[...].astype(BF16))
        if nk == 1:
            o_ref[...] = p.astype(o_ref.dtype)
            return
        acc = o_ref if direct else scr[0]
        k = pl.program_id(kax)

        @pl.when(k == 0)
        def _():
            acc[...] = p

        @pl.when(k > 0)
        def _():
            acc[...] += p

        if not direct:
            @pl.when(k == nk - 1)
            def _():
                o_ref[...] = acc[...].astype(o_ref.dtype)

    scratch = []
    if nk > 1 and not direct:
        blk = [s for s in out_spec.block_shape if s is not None]
        scratch = [pltpu.VMEM(tuple(blk), F32)]
    sem = ("parallel",) * kax + (("arbitrary",) if nk > 1 else ("parallel",))
    in_specs, args, alias = [a_spec, b_spec], (a, b), {}
    if into is not None:
        in_specs, args, alias = in_specs + [pl.BlockSpec(memory_space=pl.ANY)], (a, b, into), {2: 0}
    return _pc(body, name=name, grid=grid, in_specs=in_specs, out_specs=out_spec, out_shape=out_shape,
               scratch_shapes=scratch, input_output_aliases=alias, compiler_params=_cp(sem))(*args)


_WG_ROWS = 1024


def wgrad(cfg, name, a, dc, a_spec, dc_spec, out_shape, out_spec, ntiles, into):
    T, TC = cfg.T, cfg.TC
    tml = min(_WG_ROWS, T)
    nl = T // tml

    def body(al_ref, ac_ref, dl_ref, dcx_ref, *rest):
        o_ref, acc = rest[-2:]
        m = pl.program_id(1)

        @pl.when(m == 0)
        def _():
            acc[...] = dot_tn(al_ref[...], dl_ref[...])

        @pl.when(jnp.logical_and(m > 0, m < nl))
        def _():
            acc[...] += dot_tn(al_ref[...], dl_ref[...])

        @pl.when(m == nl)
        def _():
            o_ref[...] = (acc[...] + dot_tn(ac_ref[...], dcx_ref[...])).astype(o_ref.dtype)

    lat = lambda m: jnp.minimum(m, nl - 1)
    ctx = lambda m: T // TC
    in_specs = [a_spec(tml, lat), a_spec(TC, ctx), dc_spec(tml, lat), dc_spec(TC, ctx)]
    args, alias = (a, a, dc, dc), {}
    if into is not None:
        in_specs, args, alias = in_specs + [pl.BlockSpec(memory_space=pl.ANY)], args + (into,), {4: 0}
    blk = tuple(s for s in out_spec.block_shape if s is not None)
    return _pc(body, name=name, grid=(ntiles, nl + 1), in_specs=in_specs, out_specs=out_spec, out_shape=out_shape,
               scratch_shapes=[pltpu.VMEM(blk, F32)], input_output_aliases=alias,
               compiler_params=_cp(("parallel", "arbitrary")))(*args)


def mm_rowsharded(name, a, w4, l, out_dtype, tn):
    L, K = a.shape
    nch, _, Kb, N = w4.shape
    tm = 256

    def body(a_ref, w_ref, o_ref):
        acc = dot_nn(a_ref[:, 0:Kb], w_ref[0])
        for j in range(1, nch):
            acc += dot_nn(a_ref[:, j * Kb:(j + 1) * Kb], w_ref[j])
        o_ref[...] = acc.astype(o_ref.dtype)

    return _pc(body, name=name, grid=(N // tn, L // tm),
               in_specs=[pl.BlockSpec((tm, K), lambda n, m: (m, 0)),
                         pl.BlockSpec((nch, None, Kb, tn), lambda n, m: (0, l, 0, n))],
               out_specs=pl.BlockSpec((tm, tn), lambda n, m: (m, n)),
               out_shape=jax.ShapeDtypeStruct((L, N), out_dtype),
               compiler_params=_cp(("parallel", "parallel")))(a, w4)


def _region(cfg):
    nlat = cfg.T // cfg.TB
    return lambda i: jnp.minimum(i // nlat, 1)


def norm_mod_fwd(cfg, name, x, ng, sc, sh):
    L, D = x.shape
    TB = cfg.TB
    reg = _region(cfg)

    def body(x_ref, ng_ref, sc_ref, sh_ref, h_ref):
        xv = x_ref[...]
        r = lax.rsqrt(jnp.mean(xv * xv, axis=-1, keepdims=True) + EPS)
        n = xv * r * ng_ref[...]
        h_ref[...] = (n * (1.0 + sc_ref[...]) + sh_ref[...]).astype(h_ref.dtype)

    row = pl.BlockSpec((TB, D), lambda i: (i, 0))
    vec = pl.BlockSpec((1, D), lambda i: (0, 0))
    rvec = pl.BlockSpec((None, 1, D), lambda i: (reg(i), 0, 0))
    return _pc(body, name=name, grid=(L // TB,), in_specs=[row, vec, rvec, rvec], out_specs=row,
               out_shape=jax.ShapeDtypeStruct((L, D), BF16), compiler_params=_cp(("parallel",)))(x, ng, sc, sh)


def norm_mod_bwd(cfg, name, dh, x, ng, sc, dx_in):
    L, D = x.shape
    TB = cfg.TB
    nlat = cfg.T // TB
    reg = _region(cfg)

    def body(dh_ref, x_ref, ng_ref, sc_ref, dxi_ref, dx_ref, dsc_ref, dsh_ref, dng_ref):
        i = pl.program_id(0)
        xv = x_ref[...]
        r = lax.rsqrt(jnp.mean(xv * xv, axis=-1, keepdims=True) + EPS)
        xh = xv * r
        g = ng_ref[...]
        n = xh * g
        dh = dh_ref[...]
        dn = dh * (1.0 + sc_ref[...])
        dxh = dn * g
        dx = r * (dxh - xh * jnp.mean(dxh * xh, axis=-1, keepdims=True))
        dx_ref[...] = dxi_ref[...] + dx
        s_sh = jnp.sum(dh, axis=0, keepdims=True)
        s_sc = jnp.sum(dh * n, axis=0, keepdims=True)
        s_ng = jnp.sum(dn * xh, axis=0, keepdims=True)
        first = jnp.logical_or(i == 0, i == nlat)

        @pl.when(first)
        def _():
            dsh_ref[...] = s_sh
            dsc_ref[...] = s_sc

        @pl.when(jnp.logical_not(first))
        def _():
            dsh_ref[...] += s_sh
            dsc_ref[...] += s_sc

        @pl.when(i == 0)
        def _():
            dng_ref[...] = s_ng

        @pl.when(i > 0)
        def _():
            dng_ref[...] += s_ng

    row = pl.BlockSpec((TB, D), lambda i: (i, 0))
    vec = pl.BlockSpec((1, D), lambda i: (0, 0))
    rvec = pl.BlockSpec((None, 1, D), lambda i: (reg(i), 0, 0))
    return _pc(body, name=name, grid=(L // TB,), in_specs=[row, row, vec, rvec, row],
               out_specs=[row, rvec, rvec, vec],
               out_shape=[jax.ShapeDtypeStruct((L, D), F32), jax.ShapeDtypeStruct((2, 1, D), F32),
                          jax.ShapeDtypeStruct((2, 1, D), F32), jax.ShapeDtypeStruct((1, D), F32)],
               compiler_params=_cp(("arbitrary",)))(dh, x, ng, sc, dx_in)


def resid_fwd(cfg, name, x, y, g):
    L, D = x.shape
    TB = cfg.TB
    reg = _region(cfg)

    def body(x_ref, y_ref, g_ref, o_ref):
        o_ref[...] = x_ref[...] + g_ref[...] * y_ref[...]

    row = pl.BlockSpec((TB, D), lambda i: (i, 0))
    rvec = pl.BlockSpec((None, 1, D), lambda i: (reg(i), 0, 0))
    return _pc(body, name=name, grid=(L // TB,), in_specs=[row, row, rvec], out_specs=row,
               out_shape=jax.ShapeDtypeStruct((L, D), F32), compiler_params=_cp(("parallel",)))(x, y, g)


def resid_bwd(cfg, name, dxo, y, g):
    L, D = y.shape
    TB = cfg.TB
    nlat = cfg.T // TB
    reg = _region(cfg)

    def body(d_ref, y_ref, g_ref, dy_ref, dg_ref):
        i = pl.program_id(0)
        d = d_ref[...]
        dy_ref[...] = (d * g_ref[...]).astype(dy_ref.dtype)
        s = jnp.sum(d * y_ref[...], axis=0, keepdims=True)
        first = jnp.logical_or(i == 0, i == nlat)

        @pl.when(first)
        def _():
            dg_ref[...] = s

        @pl.when(jnp.logical_not(first))
        def _():
            dg_ref[...] += s

    row = pl.BlockSpec((TB, D), lambda i: (i, 0))
    rvec = pl.BlockSpec((None, 1, D), lambda i: (reg(i), 0, 0))
    return _pc(body, name=name, grid=(L // TB,), in_specs=[row, row, rvec], out_specs=[row, rvec],
               out_shape=[jax.ShapeDtypeStruct((L, D), BF16), jax.ShapeDtypeStruct((2, 1, D), F32)],
               compiler_params=_cp(("arbitrary",)))(dxo, y, g)


def final_loss(cfg, name, x, fg, tgt):
    L, D = x.shape
    TB = cfg.TB
    nlat = cfg.T // TB

    def body(x_ref, fg_ref, t_ref, ls_ref, dx_ref, dg_ref):
        i = pl.program_id(0)

        @pl.when(i == 0)
        def _():
            ls_ref[...] = jnp.zeros_like(ls_ref)
            dg_ref[...] = jnp.zeros_like(dg_ref)

        @pl.when(i < nlat)
        def _():
            xv = x_ref[...]
            r = lax.rsqrt(jnp.mean(xv * xv, axis=-1, keepdims=True) + EPS)
            xh = xv * r
            g = fg_ref[...]
            e = xh * g - t_ref[...]
            ls_ref[...] += 0.5 * jnp.sum(e * e) / D
            dy = e / D
            dg_ref[...] += jnp.sum(dy * xh, axis=0, keepdims=True)
            dxh = dy * g
            dx_ref[...] = r * (dxh - xh * jnp.mean(dxh * xh, axis=-1, keepdims=True))

        @pl.when(i >= nlat)
        def _():
            dx_ref[...] = jnp.zeros_like(dx_ref)

    row = pl.BlockSpec((TB, D), lambda i: (i, 0))
    trow = pl.BlockSpec((TB, D), lambda i: (jnp.minimum(i, nlat - 1), 0))
    vec = pl.BlockSpec((1, D), lambda i: (0, 0))
    return _pc(body, name=name, grid=(L // TB,), in_specs=[row, vec, trow],
               out_specs=[pl.BlockSpec((1, LANE), lambda i: (0, 0)), row, vec],
               out_shape=[jax.ShapeDtypeStruct((1, LANE), F32), jax.ShapeDtypeStruct((L, D), F32),
                          jax.ShapeDtypeStruct((1, D), F32)],
               compiler_params=_cp(("arbitrary",)))(x, fg, tgt)


def rope_tables(cfg):
    half = cfg.RDK // 2
    nf = half // 2
    pos = np.arange(cfg.T)
    row = (pos // cfg.GW).astype(np.float32)
    col = (pos % cfg.GW).astype(np.float32)
    inv = jnp.asarray(ROPE_BASE, F32) ** (-jnp.arange(nf, dtype=F32) / nf)
    ar = jnp.asarray(row)[:, None] * inv[None, :]
    ac = jnp.asarray(col)[:, None] * inv[None, :]
    cos = jnp.concatenate([jnp.cos(ar), jnp.cos(ar), jnp.cos(ac), jnp.cos(ac)], axis=1)
    sin = jnp.concatenate([-jnp.sin(ar), jnp.sin(ar), -jnp.sin(ac), jnp.sin(ac)], axis=1)
    cos = jnp.concatenate([cos, jnp.ones((cfg.TC, cfg.RDK), F32)], axis=0)
    sin = jnp.concatenate([sin, jnp.zeros((cfg.TC, cfg.RDK), F32)], axis=0)
    return cos, sin


def _swap32(t):
    lane = lax.broadcasted_iota(jnp.int32, t.shape, 1)
    return jnp.where((lane % 64) < 32, pltpu.roll(t, 96, 1), pltpu.roll(t, 32, 1))


def rope_fwd(cfg, name, P, cos, sin):
    L = P.shape[0]
    TB = cfg.TB
    off = _offsets(cfg)
    cq, ck = off["lq"] // LANE, off["lk"] // LANE
    scale = cfg.RDK ** -0.5

    def body(q_ref, k_ref, c_ref, s_ref, qo_ref, ko_ref):
        c = c_ref[...]
        s = s_ref[...]
        q = q_ref[...]
        k = k_ref[...]
        qo_ref[...] = (q * c + _swap32(q) * s) * scale
        ko_ref[...] = k * c + _swap32(k) * s

    tab = pl.BlockSpec((TB, LANE), lambda i, h: (i, 0))
    out = pl.BlockSpec((TB, LANE), lambda i, h: (i, h))
    shp = jax.ShapeDtypeStruct((L, cfg.RH * cfg.RDK), F32)
    return _pc(body, name=name, grid=(L // TB, cfg.RH),
               in_specs=[pl.BlockSpec((TB, LANE), lambda i, h: (i, cq + h)),
                         pl.BlockSpec((TB, LANE), lambda i, h: (i, ck + h)), tab, tab],
               out_specs=[out, out], out_shape=[shp, shp],
               compiler_params=_cp(("parallel", "parallel")))(P, P, cos, sin)


def rope_bwd(cfg, name, dq2, dk2, cos, sin):
    L, W = dq2[0].shape
    TB = cfg.TB
    scale = cfg.RDK ** -0.5

    def body(dqf_ref, dqb_ref, dkf_ref, dkb_ref, c_ref, s_ref, qo_ref, ko_ref):
        c = c_ref[...]
        s = s_ref[...]
        dq = dqf_ref[...] + dqb_ref[...]
        dk = dkf_ref[...] + dkb_ref[...]
        qo_ref[...] = ((dq * c - _swap32(dq) * s) * scale).astype(qo_ref.dtype)
        ko_ref[...] = (dk * c - _swap32(dk) * s).astype(ko_ref.dtype)

    tab = pl.BlockSpec((TB, LANE), lambda i, h: (i, 0))
    blk = pl.BlockSpec((TB, LANE), lambda i, h: (i, h))
    shp = jax.ShapeDtypeStruct((L, W), BF16)
    return _pc(body, name=name, grid=(L // TB, cfg.RH), in_specs=[blk, blk, blk, blk, tab, tab],
               out_specs=[blk, blk], out_shape=[shp, shp],
               compiler_params=_cp(("parallel", "parallel")))(*dq2, *dk2, cos, sin)


def _ret_chunk_map(cfg):
    C = cfg.RDK
    n = (cfg.T + cfg.TC) // C
    nlat, nctx = cfg.T // C, cfg.TC // C

    def chunk(d, s):
        if d == 0:
            return jnp.where(s < nctx, nlat + s, s - nctx)
        return n - 1 - s

    return n, chunk


def _ret_decay_terms(d, lam, C):
    ii = lax.broadcasted_iota(jnp.int32, (C, C), 0)
    jj = lax.broadcasted_iota(jnp.int32, (C, C), 1)
    diff = (ii - jj if d == 0 else jj - ii).astype(F32)
    dpos = jnp.maximum(diff, 0.0)
    Dm = jnp.where(diff >= 0, jnp.exp(dpos * lam), 0.0)
    ic = lax.broadcasted_iota(jnp.int32, (C, 1), 0).astype(F32)
    cxi = ic + 1.0 if d == 0 else C - ic
    cze = C - 1.0 - ic if d == 0 else ic
    xi = jnp.exp(cxi * lam)
    ze = jnp.exp(cze * lam)
    g = jnp.exp(jnp.full((1, 1), C, F32) * lam)
    return dpos, Dm, cxi, cze, xi, ze, g


def retention_fwd(cfg, name, qr, kr, P, lam):
    L = P.shape[0]
    C, DV, RH = cfg.RDK, cfg.RDV, cfg.RH
    n, chunk = _ret_chunk_map(cfg)

    def body(lam_ref, qf_ref, qb_ref, kf_ref, kb_ref, vf_ref, vb_ref, of_ref, ob_ref, st_ref, S):
        s = pl.program_id(0)

        @pl.when(s == 0)
        def _():
            S[...] = jnp.zeros_like(S)

        for d, (q_ref, k_ref, v_ref, o_ref) in enumerate(((qf_ref, kf_ref, vf_ref, of_ref),
                                                          (qb_ref, kb_ref, vb_ref, ob_ref))):
            for h in range(RH):
                _, Dm, _, _, xi, ze, g = _ret_decay_terms(d, lam_ref[d, h], C)
                k = k_ref[:, h * C:(h + 1) * C]
                qb = q_ref[:, h * C:(h + 1) * C].astype(BF16)
                kb = k.astype(BF16)
                vb = v_ref[:, h * DV:(h + 1) * DV].astype(BF16)
                Sv = S[d, h]
                st_ref[d, h] = Sv
                A = dot_nt(qb, kb) * Dm
                o_ref[:, h * DV:(h + 1) * DV] = dot_nn(A.astype(BF16), vb) + dot_nn(qb, Sv.astype(BF16)) * xi
                S[d, h] = Sv * g + dot_tn((k * ze).astype(BF16), vb)

    def spec(w, col, d):
        return pl.BlockSpec((C, w), lambda s: (chunk(d, s), col))

    W, WV = RH * C, RH * DV
    return _pc(body, name=name, grid=(n,),
               in_specs=[pl.BlockSpec(memory_space=pltpu.SMEM), spec(W, 0, 0), spec(W, 0, 1), spec(W, 0, 0),
                         spec(W, 0, 1), spec(WV, 1, 0), spec(WV, 1, 1)],
               out_specs=[spec(WV, 0, 0), spec(WV, 0, 1),
                          pl.BlockSpec((2, RH, None, C, DV), lambda s: (0, 0, s, 0, 0))],
               out_shape=[jax.ShapeDtypeStruct((L, WV), F32), jax.ShapeDtypeStruct((L, WV), F32),
                          jax.ShapeDtypeStruct((2, RH, n, C, DV), F32)],
               scratch_shapes=[pltpu.VMEM((2, RH, C, DV), F32)],
               compiler_params=_cp(("arbitrary",)))(lam, qr, qr, kr, kr, P, P)


def retention_bwd(cfg, name, qr, kr, P, lam, st, do):
    L = P.shape[0]
    C, DV, RH = cfg.RDK, cfg.RDV, cfg.RH
    n, chunk = _ret_chunk_map(cfg)

    def body(lam_ref, qf_ref, qb_ref, kf_ref, kb_ref, vf_ref, vb_ref, st_ref, dof_ref, dob_ref,
             dqf_ref, dqb_ref, dkf_ref, dkb_ref, dvf_ref, dvb_ref, dl_ref, dS):
        si = pl.program_id(0)

        @pl.when(si == 0)
        def _():
            dS[...] = jnp.zeros_like(dS)
            dl_ref[...] = jnp.zeros_like(dl_ref)

        dirs = ((qf_ref, kf_ref, vf_ref, dof_ref, dqf_ref, dkf_ref, dvf_ref),
                (qb_ref, kb_ref, vb_ref, dob_ref, dqb_ref, dkb_ref, dvb_ref))
        for d, (q_ref, k_ref, v_ref, do_ref, dq_ref, dk_ref, dv_ref) in enumerate(dirs):
            for h in range(RH):
                dpos, Dm, cxi, cze, xi, ze, g = _ret_decay_terms(d, lam_ref[d, h], C)
                hk = slice(h * C, (h + 1) * C)
                hv = slice(h * DV, (h + 1) * DV)
                k = k_ref[:, hk]
                do = do_ref[:, hv]
                qb = q_ref[:, hk].astype(BF16)
                kb = k.astype(BF16)
                vb = v_ref[:, hv].astype(BF16)
                dob = do.astype(BF16)
                Sn = st_ref[d, h]
                Snb = Sn.astype(BF16)
                dSn = dS[d, h]
                dSb = dSn.astype(BF16)
                A = dot_nt(qb, kb) * Dm
                dA = dot_nt(dob, vb)
                dQK = (dA * Dm).astype(BF16)
                kzb = (k * ze).astype(BF16)
                dv_ref[:, hv] = dot_tn(A.astype(BF16), dob) + dot_nn(kzb, dSb)
                dkz = dot_nt(vb, dSb)
                doxb = (do * xi).astype(BF16)
                dq_ref[:, hk] = dot_nn(dQK, kb) + dot_nt(doxb, Snb)
                dk_ref[:, hk] = dot_tn(dQK, qb) + dkz * ze
                QS = dot_nn(qb, Snb)
                t = (jnp.sum(dA * A * dpos) + jnp.sum(do * QS * (cxi * xi)) + jnp.sum(k * dkz * (cze * ze)))
                t4 = jnp.sum(dSn * Sn, axis=0, keepdims=True)
                t4 = jnp.sum(t4 * (g * C), axis=1, keepdims=True)
                dl_ref[d, h] += t + t4
                dS[d, h] = g * dSn + dot_tn(qb, doxb)

    def spec(w, col, d):
        return pl.BlockSpec((C, w), lambda si: (chunk(d, n - 1 - si), col))

    W, WV = RH * C, RH * DV
    return _pc(body, name=name, grid=(n,),
               in_specs=[pl.BlockSpec(memory_space=pltpu.SMEM), spec(W, 0, 0), spec(W, 0, 1), spec(W, 0, 0),
                         spec(W, 0, 1), spec(WV, 1, 0), spec(WV, 1, 1),
                         pl.BlockSpec((2, RH, None, C, DV), lambda si: (0, 0, n - 1 - si, 0, 0)),
                         spec(WV, 0, 0), spec(WV, 0, 1)],
               out_specs=[spec(W, 0, 0), spec(W, 0, 1), spec(W, 0, 0), spec(W, 0, 1), spec(WV, 0, 0), spec(WV, 0, 1),
                          pl.BlockSpec((2, RH, 8, LANE), lambda si: (0, 0, 0, 0))],
               out_shape=[jax.ShapeDtypeStruct((L, W), F32)] * 4 + [jax.ShapeDtypeStruct((L, WV), F32)] * 2
               + [jax.ShapeDtypeStruct((2, RH, 8, LANE), F32)],
               scratch_shapes=[pltpu.VMEM((2, RH, C, DV), F32)],
               compiler_params=_cp(("arbitrary",)))(lam, qr, qr, kr, kr, P, P, st, do, do)


def add_cast(cfg, name, a, b):
    L, W = a.shape
    TB = cfg.TB

    def body(a_ref, b_ref, o_ref):
        o_ref[...] = (a_ref[...] + b_ref[...]).astype(o_ref.dtype)

    spec = pl.BlockSpec((TB, W), lambda i: (i, 0))
    return _pc(body, name=name, grid=(L // TB,), in_specs=[spec, spec], out_specs=spec,
               out_shape=jax.ShapeDtypeStruct((L, W), BF16), compiler_params=_cp(("parallel",)))(a, b)


def ggn_fwd(cfg, name, o2, P, gn_g):
    L = P.shape[0]
    TB, DV, RH = cfg.TB, cfg.RDV, cfg.RH
    gc0 = _offsets(cfg)["lg"] // DV

    def body(of_ref, ob_ref, gate_ref, g_ref, out_ref):
        o = of_ref[...] + ob_ref[...]
        mu = jnp.mean(o, axis=-1, keepdims=True)
        xc = o - mu
        var = jnp.mean(xc * xc, axis=-1, keepdims=True)
        y = xc * lax.rsqrt(var + EPS) * g_ref[...]
        out_ref[...] = (y * _silu(gate_ref[...])).astype(out_ref.dtype)

    blk = pl.BlockSpec((TB, DV), lambda i, h: (i, h))
    return _pc(body, name=name, grid=(L // TB, RH),
               in_specs=[blk, blk, pl.BlockSpec((TB, DV), lambda i, h: (i, gc0 + h)),
                         pl.BlockSpec((1, DV), lambda i, h: (0, h))],
               out_specs=blk, out_shape=jax.ShapeDtypeStruct((L, RH * DV), BF16),
               compiler_params=_cp(("parallel", "parallel")))(*o2, P, gn_g)


def ggn_bwd(cfg, name, dout, o2, P, gn_g, col0):
    L = P.shape[0]
    TB, DV, RH = cfg.TB, cfg.RDV, cfg.RH
    gc0 = _offsets(cfg)["lg"] // DV

    def body(d_ref, of_ref, ob_ref, gate_ref, g_ref, do_ref, dgate_ref, dg_ref):
        i = pl.program_id(1)
        o = of_ref[...] + ob_ref[...]
        mu = jnp.mean(o, axis=-1, keepdims=True)
        xc = o - mu
        var = jnp.mean(xc * xc, axis=-1, keepdims=True)
        r = lax.rsqrt(var + EPS)
        y = xc * r
        g = g_ref[...]
        gate = gate_ref[...]
        d = d_ref[...]
        dgate_ref[...] = (d * (y * g) * _dsilu(gate)).astype(dgate_ref.dtype)
        dyg = d * _silu(gate)
        s = jnp.sum(dyg * y, axis=0, keepdims=True)

        @pl.when(i == 0)
        def _():
            dg_ref[...] = s

        @pl.when(i > 0)
        def _():
            dg_ref[...] += s

        dy = dyg * g
        do_ref[...] = r * (dy - jnp.mean(dy, axis=-1, keepdims=True)
                           - y * jnp.mean(dy * y, axis=-1, keepdims=True))

    blk = pl.BlockSpec((TB, DV), lambda h, i: (i, h))
    return _pc(body, name=name, grid=(RH, L // TB),
               in_specs=[pl.BlockSpec((TB, DV), lambda h, i: (i, col0 + h)), blk, blk,
                         pl.BlockSpec((TB, DV), lambda h, i: (i, gc0 + h)),
                         pl.BlockSpec((1, DV), lambda h, i: (0, h))],
               out_specs=[blk, blk, pl.BlockSpec((1, DV), lambda h, i: (0, h))],
               out_shape=[jax.ShapeDtypeStruct((L, RH * DV), F32), jax.ShapeDtypeStruct((L, RH * DV), BF16),
                          jax.ShapeDtypeStruct((1, RH * DV), F32)],
               compiler_params=_cp(("parallel", "arbitrary")))(dout, *o2, P, gn_g)


def cast_cols(cfg, name, src, col0, ncols, width):
    L = src.shape[0]
    TB = cfg.TB

    def body(s_ref, o_ref):
        o_ref[...] = s_ref[...].astype(o_ref.dtype)

    spec = pl.BlockSpec((TB, width), lambda i, j: (i, col0 + j))
    return _pc(body, name=name, grid=(L // TB, ncols), in_specs=[spec],
               out_specs=pl.BlockSpec((TB, width), lambda i, j: (i, j)),
               out_shape=jax.ShapeDtypeStruct((L, ncols * width), BF16),
               compiler_params=_cp(("parallel", "parallel")))(src)


_CPAD = 16


def _conv_windows(cfg):
    T, TC, TB = cfg.T, cfg.TC, cfg.TB
    assert TC % TB == 0 and T % TB == 0 and cfg.CK // 2 < _CPAD
    return T // TB, [(T + j * TB, T + _CPAD + j * TB) for j in range(TC // TB)]


def _fill_padded(cfg, pb, get):
    T, TC, TB = cfg.T, cfg.TC, cfg.TB
    z = jnp.zeros((_CPAD, LANE), F32)
    pb[0:_CPAD, :] = z
    pb[_CPAD + T:2 * _CPAD + T, :] = z
    pb[2 * _CPAD + T + TC:3 * _CPAD + T + TC, :] = z

    def fill(i, c):
        r0 = pl.multiple_of(i * TB, TB)
        pb[pl.ds(r0 + _CPAD, TB), :] = get(r0)
        return c

    lax.fori_loop(0, T // TB, fill, 0)
    for j in range(TC // TB):
        pb[2 * _CPAD + T + j * TB:2 * _CPAD + T + (j + 1) * TB, :] = get(T + j * TB)


def _taps(win, TB):
    W = TB + 2 * _CPAD
    return lambda k: pltpu.roll(win, W - (k + 1), 0)[0:TB, :]


def glu_dwconv_fwd(cfg, name, P, w, b):
    L = P.shape[0]
    T, TC, TB, K = cfg.T, cfg.TC, cfg.TB, cfg.CK
    off = _offsets(cfg)
    ca, cb = off["la"] // LANE, off["lb"] // LANE
    nlat, ctx_tiles = _conv_windows(cfg)
    PBL = 3 * _CPAD + T + TC

    def body(a_ref, b_ref, w_ref, bias_ref, y_ref, pb):
        _fill_padded(cfg, pb, lambda r0: a_ref[pl.ds(r0, TB), :] * _sigmoid(b_ref[pl.ds(r0, TB), :]))
        wv = w_ref[...]
        bias = bias_ref[...]

        def tile(win):
            tap = _taps(win, TB)
            acc = jnp.zeros((TB, LANE), F32) + bias
            for k in range(K):
                acc = acc + wv[k:k + 1, :] * tap(k)
            return acc

        def lat(i, c):
            r0 = pl.multiple_of(i * TB, TB)
            y_ref[pl.ds(r0, TB), :] = tile(pb[pl.ds(r0, TB + 2 * _CPAD), :])
            return c

        lax.fori_loop(0, nlat, lat, 0)
        for r0, w0 in ctx_tiles:
            y_ref[r0:r0 + TB, :] = tile(pb[w0:w0 + TB + 2 * _CPAD, :])

    return _pc(body, name=name, grid=(cfg.CW // LANE,),
               in_specs=[pl.BlockSpec((L, LANE), lambda j: (0, ca + j)),
                         pl.BlockSpec((L, LANE), lambda j: (0, cb + j)),
                         pl.BlockSpec((32, LANE), lambda j: (0, j)),
                         pl.BlockSpec((1, LANE), lambda j: (0, j))],
               out_specs=pl.BlockSpec((L, LANE), lambda j: (0, j)),
               out_shape=jax.ShapeDtypeStruct((L, cfg.CW), F32),
               scratch_shapes=[pltpu.VMEM((PBL, LANE), F32)],
               compiler_params=_cp(("parallel",)))(P, P, w, b)


def glu_dwconv_bwd(cfg, name, P, w, dy):
    L = P.shape[0]
    T, TC, TB, K = cfg.T, cfg.TC, cfg.TB, cfg.CK
    off = _offsets(cfg)
    ca, cb = off["la"] // LANE, off["lb"] // LANE
    nlat, ctx_tiles = _conv_windows(cfg)
    PBL = 3 * _CPAD + T + TC

    def body(a_ref, b_ref, w_ref, dy_ref, da_ref, db_ref, dw_ref, dbias_ref, pbu, pbd):
        _fill_padded(cfg, pbu, lambda r0: a_ref[pl.ds(r0, TB), :] * _sigmoid(b_ref[pl.ds(r0, TB), :]))
        _fill_padded(cfg, pbd, lambda r0: dy_ref[pl.ds(r0, TB), :])
        wv = w_ref[...]
        dw_ref[...] = jnp.zeros_like(dw_ref)
        dbias_ref[...] = jnp.zeros_like(dbias_ref)

        def tile(r0, winu, wind):
            tapu = _taps(winu, TB)
            tapd = _taps(wind, TB)
            dyt = dy_ref[pl.ds(r0, TB), :]
            du = jnp.zeros((TB, LANE), F32)
            for k in range(K):
                du = du + wv[k:k + 1, :] * tapd(K - 1 - k)
                dw_ref[k:k + 1, :] += jnp.sum(dyt * tapu(k), axis=0, keepdims=True)
            dbias_ref[...] += jnp.sum(dyt, axis=0, keepdims=True)
            a = a_ref[pl.ds(r0, TB), :]
            sg = _sigmoid(b_ref[pl.ds(r0, TB), :])
            da_ref[pl.ds(r0, TB), :] = (du * sg).astype(da_ref.dtype)
            db_ref[pl.ds(r0, TB), :] = (du * a * sg * (1.0 - sg)).astype(db_ref.dtype)

        def lat(i, c):
            r0 = pl.multiple_of(i * TB, TB)
            tile(r0, pbu[pl.ds(r0, TB + 2 * _CPAD), :], pbd[pl.ds(r0, TB + 2 * _CPAD), :])
            return c

        lax.fori_loop(0, nlat, lat, 0)
        for r0, w0 in ctx_tiles:
            tile(r0, pbu[w0:w0 + TB + 2 * _CPAD, :], pbd[w0:w0 + TB + 2 * _CPAD, :])

    col = pl.BlockSpec((L, LANE), lambda j: (0, j))
    return _pc(body, name=name, grid=(cfg.CW // LANE,),
               in_specs=[pl.BlockSpec((L, LANE), lambda j: (0, ca + j)),
                         pl.BlockSpec((L, LANE), lambda j: (0, cb + j)),
                         pl.BlockSpec((32, LANE), lambda j: (0, j)), col],
               out_specs=[col, col, pl.BlockSpec((32, LANE), lambda j: (0, j)),
                          pl.BlockSpec((1, LANE), lambda j: (0, j))],
               out_shape=[jax.ShapeDtypeStruct((L, cfg.CW), BF16), jax.ShapeDtypeStruct((L, cfg.CW), BF16),
                          jax.ShapeDtypeStruct((32, cfg.CW), F32), jax.ShapeDtypeStruct((1, cfg.CW), F32)],
               scratch_shapes=[pltpu.VMEM((PBL, LANE), F32), pltpu.VMEM((PBL, LANE), F32)],
               compiler_params=_cp(("parallel",)))(P, P, w, dy)


def ln_silu_fwd(cfg, name, y, g, b):
    L, W = y.shape
    TB = cfg.TB

    def body(y_ref, g_ref, b_ref, o_ref):
        yv = y_ref[...]
        mu = jnp.mean(yv, axis=-1, keepdims=True)
        xc = yv - mu
        var = jnp.mean(xc * xc, axis=-1, keepdims=True)
        z = xc * lax.rsqrt(var + EPS) * g_ref[...] + b_ref[...]
        o_ref[...] = _silu(z).astype(o_ref.dtype)

    row = pl.BlockSpec((TB, W), lambda i: (i, 0))
    vec = pl.BlockSpec((1, W), lambda i: (0, 0))
    return _pc(body, name=name, grid=(L // TB,), in_specs=[row, vec, vec], out_specs=row,
               out_shape=jax.ShapeDtypeStruct((L, W), BF16), compiler_params=_cp(("parallel",)))(y, g, b)


def ln_silu_bwd(cfg, name, dact, y, g, b):
    L, W = y.shape
    TB = cfg.TB

    def body(d_ref, y_ref, g_ref, b_ref, dy_ref, dg_ref, db_ref):
        i = pl.program_id(0)
        yv = y_ref[...]
        mu = jnp.mean(yv, axis=-1, keepdims=True)
        xc = yv - mu
        var = jnp.mean(xc * xc, axis=-1, keepdims=True)
        r = lax.rsqrt(var + EPS)
        yh = xc * r
        g = g_ref[...]
        z = yh * g + b_ref[...]
        dz = d_ref[...] * _dsilu(z)
        sg = jnp.sum(dz * yh, axis=0, keepdims=True)
        sb = jnp.sum(dz, axis=0, keepdims=True)

        @pl.when(i == 0)
        def _():
            dg_ref[...] = sg
            db_ref[...] = sb

        @pl.when(i > 0)
        def _():
            dg_ref[...] += sg
            db_ref[...] += sb

        dh = dz * g
        dy_ref[...] = r * (dh - jnp.mean(dh, axis=-1, keepdims=True)
                           - yh * jnp.mean(dh * yh, axis=-1, keepdims=True))

    row = pl.BlockSpec((TB, W), lambda i: (i, 0))
    vec = pl.BlockSpec((1, W), lambda i: (0, 0))
    return _pc(body, name=name, grid=(L // TB,), in_specs=[row, row, vec, vec], out_specs=[row, vec, vec],
               out_shape=[jax.ShapeDtypeStruct((L, W), F32), jax.ShapeDtypeStruct((1, W), F32),
                          jax.ShapeDtypeStruct((1, W), F32)],
               compiler_params=_cp(("arbitrary",)))(dact, y, g, b)


def _na_geometry(cfg):
    R = cfg.T // cfg.GW
    nb = R // cfg.NAR
    assert nb >= 3 and cfg.GW == 64 and cfg.NAR == 8
    ks = [int(np.clip(8 * b - 4, 0, R - 16)) for b in range(nb)]
    return R, nb, ks


_NTAB = 18


def _split3(x):
    hi = x.astype(BF16)
    r = x - hi.astype(F32)
    mid = r.astype(BF16)
    lo = (r - mid.astype(F32)).astype(BF16)
    return hi, mid, lo


def _na_col_onehot(cfg):
    GW, NAC = cfg.GW, cfg.NAC
    qc = np.arange(GW)[:, None]
    kc = np.arange(GW)[None, :]
    cs = np.clip(qc - NAC // 2, 0, GW - NAC)
    vcol = (kc >= cs) & (kc < cs + NAC)
    dd = np.clip(kc - qc + NAC - 1, 0, 2 * NAC - 2)
    oh = (np.arange(LANE)[:, None, None] == dd[None]).astype(np.float32)
    z = np.zeros_like(oh)
    oda = np.concatenate([oh, z], axis=2).reshape(LANE, GW * LANE)
    odb = np.concatenate([z, oh], axis=2).reshape(LANE, GW * LANE)
    cm = np.where(np.concatenate([vcol, vcol], axis=1), 0.0, NEG).astype(np.float32).reshape(1, GW * LANE)
    return oda, odb, cm


def na_tables(cfg, name, rpb):
    NH, GW = cfg.NH, cfg.GW
    na = rpb.shape[1]
    oda, odb, cm = _na_col_onehot(cfg)
    rp = jnp.zeros((NH, _NTAB + 1, LANE), F32).at[:, 1:1 + na, :rpb.shape[2]].set(rpb.astype(F32))
    r0 = rp[:, :_NTAB].reshape(NH * _NTAB, LANE)
    r1 = rp[:, 1:].reshape(NH * _NTAB, LANE)
    a = np.arange(_NTAB) - 1
    rm0 = np.where((a >= 0) & (a < na), 0.0, NEG).astype(np.float32)
    rm1 = np.where((a + 1 >= 0) & (a + 1 < na), 0.0, NEG).astype(np.float32)
    half = (np.arange(GW * LANE) % LANE >= GW)[None, :]
    rmask = np.where(half, np.tile(rm1, NH)[:, None], np.tile(rm0, NH)[:, None]).astype(np.float32)
    tn = 2048
    rows = NH * _NTAB

    def body(r0_ref, r1_ref, a_ref, b_ref, cm_ref, rm_ref, o_ref):
        acc = cm_ref[...] + rm_ref[...]
        for t in _split3(r0_ref[...]):
            acc = acc + dot_nn(t, a_ref[...])
        for t in _split3(r1_ref[...]):
            acc = acc + dot_nn(t, b_ref[...])
        o_ref[...] = acc

    rs = pl.BlockSpec((rows, LANE), lambda n: (0, 0))
    out = _pc(body, name=name, grid=(GW * LANE // tn,),
              in_specs=[rs, rs, pl.BlockSpec((LANE, tn), lambda n: (0, n)), pl.BlockSpec((LANE, tn), lambda n: (0, n)),
                        pl.BlockSpec((1, tn), lambda n: (0, n)), pl.BlockSpec((rows, tn), lambda n: (0, n))],
              out_specs=pl.BlockSpec((rows, tn), lambda n: (0, n)),
              out_shape=jax.ShapeDtypeStruct((rows, GW * LANE), F32),
              compiler_params=_cp(("parallel",)))(r0, r1, jnp.asarray(oda, BF16), jnp.asarray(odb, BF16),
                                                  jnp.asarray(cm), jnp.asarray(rmask))
    return out.reshape(NH, _NTAB, GW, LANE)


def _na_tiles(cfg, b):
    R, nb, _ = _na_geometry(cfg)
    NAR = cfg.NAR
    ksb = jnp.clip(8 * b - 4, 0, R - 16)
    for i in range(8):
        qr = 8 * b + i
        ws = jnp.clip(qr - NAR // 2, 0, R - NAR)
        for J in range(8):
            kr0 = ksb + 2 * J
            row = jnp.clip(kr0 - qr + NAR - 1, -1, _NTAB - 2) + 1
            v0 = jnp.logical_and(kr0 >= ws, kr0 < ws + NAR)
            v1 = jnp.logical_and(kr0 + 1 >= ws, kr0 + 1 < ws + NAR)
            yield i, J, row, v0, v1


def _na_fill_bias(cfg, tab_ref, bias, b):
    GW = cfg.GW
    first = lax.broadcasted_iota(jnp.int32, (GW, LANE), 1) < GW
    for i, J, row, v0, v1 in _na_tiles(cfg, b):
        ok = jnp.where(first, v0.astype(jnp.int32), v1.astype(jnp.int32))
        bias[i * GW:(i + 1) * GW, J * LANE:(J + 1) * LANE] = jnp.where(ok > 0, tab_ref[row], NEG)


def _na_specs(cfg):
    R, nb, ks = _na_geometry(cfg)
    off = _offsets(cfg)
    TQ = 8 * cfg.GW
    KP = 4 * cfg.GW
    ks4 = [k // 4 for k in ks]
    lat_blocks = cfg.T // KP

    def ks4_of(b):
        return jnp.clip(2 * b - 1, 0, R // 4 - 4)

    assert all(int(np.clip(2 * b - 1, 0, R // 4 - 4)) == ks4[b] for b in range(nb))
    assert cfg.TC == KP

    def col(nm):
        c0 = off[nm] // LANE
        q = pl.BlockSpec((TQ, LANE), lambda h, b: (b, c0 + h))
        parts = [pl.BlockSpec((KP, LANE), functools.partial(lambda h, b, t: (ks4_of(b) + t, c0 + h), t=t))
                 for t in range(4)]
        ctx = pl.BlockSpec((KP, LANE), lambda h, b: (lat_blocks, c0 + h))
        return q, parts, ctx

    return nb, TQ, KP, ks4_of, col


def na_fwd(cfg, name, P, tab):
    nb, TQ, KP, ks4_of, col = _na_specs(cfg)
    NH = cfg.NH
    scale = cfg.NDH ** -0.5
    qs, _, _ = col("nq")
    _, kparts, kctx = col("nk")
    _, vparts, vctx = col("nv")

    def body(q_ref, k0, k1, k2, k3, kc_ref, v0, v1, v2, v3, vc_ref, tab_ref, o_ref, lse_ref, bias_ref):
        _na_fill_bias(cfg, tab_ref, bias_ref, pl.program_id(1))
        q = (q_ref[...] * scale).astype(BF16)
        kl = jnp.concatenate([k0[...], k1[...], k2[...], k3[...]], axis=0).astype(BF16)
        vl = jnp.concatenate([v0[...], v1[...], v2[...], v3[...]], axis=0).astype(BF16)
        kc = kc_ref[...].astype(BF16)
        vc = vc_ref[...].astype(BF16)
        sl = dot_nt(q, kl) + bias_ref[...]
        sc = dot_nt(q, kc)
        m = jnp.maximum(jnp.max(sl, axis=-1, keepdims=True), jnp.max(sc, axis=-1, keepdims=True))
        pl_ = jnp.exp(sl - m)
        pc = jnp.exp(sc - m)
        den = jnp.sum(pl_, axis=-1, keepdims=True) + jnp.sum(pc, axis=-1, keepdims=True)
        o = dot_nn(pl_.astype(BF16), vl) + dot_nn(pc.astype(BF16), vc)
        o_ref[...] = o / den
        lse_ref[...] = m + jnp.log(den)

    return _pc(body, name=name, grid=(NH, nb),
               in_specs=[qs, *kparts, kctx, *vparts, vctx,
                         pl.BlockSpec((None, _NTAB, cfg.GW, LANE), lambda h, b: (h, 0, 0, 0))],
               out_specs=[pl.BlockSpec((TQ, LANE), lambda h, b: (b, h)),
                          pl.BlockSpec((None, TQ, 1), lambda h, b: (h, b, 0))],
               out_shape=[jax.ShapeDtypeStruct((cfg.T, NH * LANE), F32),
                          jax.ShapeDtypeStruct((NH, cfg.T, 1), F32)],
               scratch_shapes=[pltpu.VMEM((TQ, 4 * KP), F32)],
               compiler_params=_cp(("parallel", "parallel")))(P, *([P] * 5), *([P] * 5), tab)


def na_bwd(cfg, name, P, tab, o, lse, dmix, dcol0):
    nb, TQ, KP, ks4_of, col = _na_specs(cfg)
    NH, GW = cfg.NH, cfg.GW
    L = P.shape[0]
    scale = cfg.NDH ** -0.5
    qs, _, _ = col("nq")
    _, kparts, kctx = col("nk")
    _, vparts, vctx = col("nv")

    def body(q_ref, k0, k1, k2, k3, kc_ref, v0, v1, v2, v3, vc_ref, tab_ref, o_ref, lse_ref, do_ref,
             dq_ref, dk_ref, dv_ref, dtab_ref, bias_ref):
        b = pl.program_id(1)

        @pl.when(b == 0)
        def _():
            dk_ref[...] = jnp.zeros_like(dk_ref)
            dv_ref[...] = jnp.zeros_like(dv_ref)
            dtab_ref[...] = jnp.zeros_like(dtab_ref)

        _na_fill_bias(cfg, tab_ref, bias_ref, b)

        q = (q_ref[...] * scale).astype(BF16)
        kl = jnp.concatenate([k0[...], k1[...], k2[...], k3[...]], axis=0).astype(BF16)
        vl = jnp.concatenate([v0[...], v1[...], v2[...], v3[...]], axis=0).astype(BF16)
        kc = kc_ref[...].astype(BF16)
        vc = vc_ref[...].astype(BF16)
        lse = lse_ref[...]
        do = do_ref[...]
        dob = do.astype(BF16)
        p_l = jnp.exp(dot_nt(q, kl) + bias_ref[...] - lse)
        p_c = jnp.exp(dot_nt(q, kc) - lse)
        delta = jnp.sum(do * o_ref[...], axis=-1, keepdims=True)
        ds_l = p_l * (dot_nt(dob, vl) - delta)
        ds_c = p_c * (dot_nt(dob, vc) - delta)
        dslb = ds_l.astype(BF16)
        dscb = ds_c.astype(BF16)
        dq_ref[...] = ((dot_nn(dslb, kl) + dot_nn(dscb, kc)) * scale).astype(dq_ref.dtype)
        r0 = pl.multiple_of(ks4_of(b) * KP, KP)
        dk_ref[pl.ds(r0, 4 * KP), :] += dot_tn(dslb, q)
        dv_ref[pl.ds(r0, 4 * KP), :] += dot_tn(p_l.astype(BF16), dob)
        dk_ref[cfg.T:cfg.T + KP, :] += dot_tn(dscb, q)
        dv_ref[cfg.T:cfg.T + KP, :] += dot_tn(p_c.astype(BF16), dob)
        bias_ref[...] = ds_l
        for i, J, row, _, _ in _na_tiles(cfg, b):
            dtab_ref[row] += bias_ref[i * GW:(i + 1) * GW, J * LANE:(J + 1) * LANE]

    full = pl.BlockSpec((L, LANE), lambda h, b: (0, h))
    tabs = pl.BlockSpec((None, _NTAB, GW, LANE), lambda h, b: (h, 0, 0, 0))
    return _pc(body, name=name, grid=(NH, nb),
               in_specs=[qs, *kparts, kctx, *vparts, vctx, tabs,
                         pl.BlockSpec((TQ, LANE), lambda h, b: (b, h)),
                         pl.BlockSpec((None, TQ, 1), lambda h, b: (h, b, 0)),
                         pl.BlockSpec((TQ, LANE), lambda h, b: (b, dcol0 + h))],
               out_specs=[pl.BlockSpec((TQ, LANE), lambda h, b: (b, h)), full, full, tabs],
               out_shape=[jax.ShapeDtypeStruct((cfg.T, NH * LANE), BF16),
                          jax.ShapeDtypeStruct((L, NH * LANE), F32), jax.ShapeDtypeStruct((L, NH * LANE), F32),
                          jax.ShapeDtypeStruct((NH, _NTAB, GW, LANE), F32)],
               scratch_shapes=[pltpu.VMEM((TQ, 4 * KP), F32)],
               compiler_params=_cp(("parallel", "arbitrary")))(
                   P, *([P] * 5), *([P] * 5), tab, o, lse, dmix)


def na_ctx_fwd(cfg, name, P):
    off = _offsets(cfg)
    TC, NH = cfg.TC, cfg.NH
    rb = cfg.T // TC
    scale = cfg.NDH ** -0.5

    def body(q_ref, k_ref, v_ref, o_ref, lse_ref):
        q = (q_ref[...] * scale).astype(BF16)
        s = dot_nt(q, k_ref[...].astype(BF16))
        m = jnp.max(s, axis=-1, keepdims=True)
        p = jnp.exp(s - m)
        den = jnp.sum(p, axis=-1, keepdims=True)
        o_ref[...] = dot_nn(p.astype(BF16), v_ref[...].astype(BF16)) / den
        lse_ref[...] = m + jnp.log(den)

    spec = lambda nm: pl.BlockSpec((TC, LANE), functools.partial(lambda h, c0: (rb, c0 + h), c0=off[nm] // LANE))
    return _pc(body, name=name, grid=(NH,), in_specs=[spec("nq"), spec("nk"), spec("nv")],
               out_specs=[pl.BlockSpec((TC, LANE), lambda h: (0, h)), pl.BlockSpec((None, TC, 1), lambda h: (h, 0, 0))],
               out_shape=[jax.ShapeDtypeStruct((TC, NH * LANE), F32), jax.ShapeDtypeStruct((NH, TC, 1), F32)],
               compiler_params=_cp(("parallel",)))(P, P, P)


def na_ctx_bwd(cfg, name, P, o, lse, dmix, dcol0, dk_in, dv_in):
    off = _offsets(cfg)
    TC, NH = cfg.TC, cfg.NH
    rb = cfg.T // TC
    scale = cfg.NDH ** -0.5

    def body(q_ref, k_ref, v_ref, o_ref, lse_ref, do_ref, dki_ref, dvi_ref, dq_ref, dk_ref, dv_ref):
        q = (q_ref[...] * scale).astype(BF16)
        kb = k_ref[...].astype(BF16)
        vb = v_ref[...].astype(BF16)
        do = do_ref[...]
        dob = do.astype(BF16)
        p = jnp.exp(dot_nt(q, kb) - lse_ref[...])
        delta = jnp.sum(do * o_ref[...], axis=-1, keepdims=True)
        ds = (p * (dot_nt(dob, vb) - delta)).astype(BF16)
        dq_ref[...] = (dot_nn(ds, kb) * scale).astype(dq_ref.dtype)
        dk_ref[...] = (dki_ref[...] + dot_tn(ds, q)).astype(dk_ref.dtype)
        dv_ref[...] = (dvi_ref[...] + dot_tn(p.astype(BF16), dob)).astype(dv_ref.dtype)

    spec = lambda nm: pl.BlockSpec((TC, LANE), functools.partial(lambda h, c0: (rb, c0 + h), c0=off[nm] // LANE))
    hb = pl.BlockSpec((TC, LANE), lambda h: (0, h))
    ctxrow = pl.BlockSpec((TC, LANE), lambda h: (rb, h))
    shp = jax.ShapeDtypeStruct((TC, NH * LANE), BF16)
    return _pc(body, name=name, grid=(NH,),
               in_specs=[spec("nq"), spec("nk"), spec("nv"), hb, pl.BlockSpec((None, TC, 1), lambda h: (h, 0, 0)),
                         pl.BlockSpec((TC, LANE), lambda h: (rb, dcol0 + h)), ctxrow, ctxrow],
               out_specs=[hb, hb, hb], out_shape=[shp, shp, shp],
               compiler_params=_cp(("parallel",)))(P, P, P, o, lse, dmix, dk_in, dv_in)


def na_rpb_grad(cfg, name, dtab):
    NH, GW = cfg.NH, cfg.GW
    na, nd = 2 * cfg.NAR - 1, 2 * cfg.NAC - 1
    oda, odb, _ = _na_col_onehot(cfg)
    E = np.concatenate([oda.T, odb.T], axis=1)
    rows = NH * _NTAB

    def body(z_ref, e_ref, o_ref):
        zv = z_ref[...]
        hi = zv.astype(BF16)
        lo = (zv - hi.astype(F32)).astype(BF16)
        e = e_ref[...]
        o_ref[...] = dot_nn(hi, e) + dot_nn(lo, e)

    g = _pc(body, name=name, out_shape=jax.ShapeDtypeStruct((rows, 2 * LANE), F32),
            compiler_params=_cp())(dtab.reshape(rows, GW * LANE), jnp.asarray(E, BF16))
    g = g.reshape(NH, _NTAB, 2, LANE)
    return g[:, 1:1 + na, 0, :nd] + g[:, 0:na, 1, :nd]


def _seq_tiles(cfg):
    T, TC, TB = cfg.T, cfg.TC, cfg.TB
    tiles = []
    for i in range((T + TC) // TB):
        r0 = i * TB
        tiles.append((r0, r0 == 0 or r0 == T, r0 + TB == T or r0 + TB == T + TC))
    return tiles


def _shift3(ref_get, r0, TB, start, end, width):
    cur = ref_get(r0, TB)
    if start or end:
        rowi = lax.broadcasted_iota(jnp.int32, (TB, width), 0)
    up = jnp.where(rowi == 0, 0.0, pltpu.roll(cur, 1, 0)) if start else ref_get(r0 - 1, TB)
    dn = jnp.where(rowi == TB - 1, 0.0, pltpu.roll(cur, TB - 1, 0)) if end else ref_get(r0 + 1, TB)
    return up, cur, dn


def ffn_act_fwd(cfg, name, U2, w, b):
    _, L, DFF = U2.shape
    TB = cfg.TB
    tiles = _seq_tiles(cfg)

    def body(u_ref, w_ref, b_ref, a_ref):
        def plane(p, r0, st, en):
            up, cur, dn = _shift3(lambda r, n: u_ref[p, r:r + n, :], r0, TB, st, en, LANE)
            wv = w_ref[p]
            return wv[0:1, :] * up + wv[1:2, :] * cur + wv[2:3, :] * dn + b_ref[p]

        for r0, st, en in tiles:
            val = plane(0, r0, st, en)
            gate = plane(1, r0, st, en)
            a_ref[r0:r0 + TB, :] = (_silu(gate) * val).astype(a_ref.dtype)

    return _pc(body, name=name, grid=(DFF // LANE,),
               in_specs=[pl.BlockSpec((2, L, LANE), lambda j: (0, 0, j)),
                         pl.BlockSpec((2, 8, LANE), lambda j: (0, 0, j)),
                         pl.BlockSpec((2, 1, LANE), lambda j: (0, 0, j))],
               out_specs=pl.BlockSpec((L, LANE), lambda j: (0, j)),
               out_shape=jax.ShapeDtypeStruct((L, DFF), BF16),
               compiler_params=_cp(("parallel",)))(U2, w, b)


def ffn_act_bwd(cfg, name, U2, w, b, dA):
    _, L, DFF = U2.shape
    TB = cfg.TB
    tiles = _seq_tiles(cfg)

    def body(u_ref, w_ref, b_ref, da_ref, du_ref, dw_ref, db_ref, dbuf):
        dw_ref[...] = jnp.zeros_like(dw_ref)
        db_ref[...] = jnp.zeros_like(db_ref)
        for r0, st, en in tiles:
            shifted = []
            pre = []
            for p in range(2):
                up, cur, dn = _shift3(lambda r, n: u_ref[p, r:r + n, :], r0, TB, st, en, LANE)
                wv = w_ref[p]
                shifted.append((up, cur, dn))
                pre.append(wv[0:1, :] * up + wv[1:2, :] * cur + wv[2:3, :] * dn + b_ref[p])
            val, gate = pre
            da = da_ref[r0:r0 + TB, :]
            dpre = (da * _silu(gate), da * val * _dsilu(gate))
            for p in range(2):
                dbuf[p, r0:r0 + TB, :] = dpre[p]
                for k in range(3):
                    dw_ref[p, k:k + 1, :] += jnp.sum(dpre[p] * shifted[p][k], axis=0, keepdims=True)
                db_ref[p] += jnp.sum(dpre[p], axis=0, keepdims=True)
        for r0, st, en in tiles:
            for p in range(2):
                up, cur, dn = _shift3(lambda r, n: dbuf[p, r:r + n, :], r0, TB, st, en, LANE)
                wv = w_ref[p]
                du_ref[p, r0:r0 + TB, :] = (wv[0:1, :] * dn + wv[1:2, :] * cur + wv[2:3, :] * up).astype(du_ref.dtype)

    blk = pl.BlockSpec((2, L, LANE), lambda j: (0, 0, j))
    wspec = pl.BlockSpec((2, 8, LANE), lambda j: (0, 0, j))
    bspec = pl.BlockSpec((2, 1, LANE), lambda j: (0, 0, j))
    return _pc(body, name=name, grid=(DFF // LANE,),
               in_specs=[blk, wspec, bspec, pl.BlockSpec((L, LANE), lambda j: (0, j))],
               out_specs=[blk, wspec, bspec],
               out_shape=[jax.ShapeDtypeStruct((2, L, DFF), BF16), jax.ShapeDtypeStruct((2, 8, DFF), F32),
                          jax.ShapeDtypeStruct((2, 1, DFF), F32)],
               scratch_shapes=[pltpu.VMEM((2, L, LANE), F32)],
               compiler_params=_cp(("parallel",)))(U2, w, b, dA)


def _tm(L, parts):
    assert L % parts == 0
    return L // parts


def layer_fwd(cfg, l, XS, mod, wts, small, tabs):
    L, D = XS.shape
    off = _offsets(cfg)
    DIN = off["end"]
    Win4, Wout4, Wup4, Wdn4, Wpw4 = wts["w_in"], wts["w_out"], wts["ffn_up"], wts["ffn_down"], wts["conv_pw"]
    nbi = Win4.shape[3]
    tmA = _tm(L, 4)
    sv = {}
    sv["XS"] = XS
    h1 = norm_mod_fwd(cfg, f"norm1_fwd_{l}", XS, small["norm1_g"], mod["sc1"], mod["sh1"])
    P = matmul(f"mm_in_{l}", h1, Win4, contract="nn", grid=(4, L // tmA),
               a_spec=pl.BlockSpec((tmA, D), lambda n, m: (m, 0)),
               b_spec=pl.BlockSpec((None, None, D, nbi), lambda n, m: (n, l, 0, 0)),
               out_shape=jax.ShapeDtypeStruct((L, DIN), F32),
               out_spec=pl.BlockSpec((tmA, nbi), lambda n, m: (m, n)), nk=1)
    qr, kr = rope_fwd(cfg, f"rope_fwd_{l}", P, tabs["cos"], tabs["sin"])
    o_f, o_b, st = retention_fwd(cfg, f"ret_fwd_{l}", qr, kr, P, small["lam"])
    o2 = (o_f, o_b)
    ret = ggn_fwd(cfg, f"ggn_fwd_{l}", o2, P, small["ret_gn_g"])
    ycv = glu_dwconv_fwd(cfg, f"dwconv_fwd_{l}", P, small["conv_dw_w"], small["conv_dw_b"])
    act = ln_silu_fwd(cfg, f"ln_silu_fwd_{l}", ycv, small["conv_ln_g"], small["conv_ln_b"])
    cv = mm_rowsharded(f"mm_pw_{l}", act, Wpw4, l, BF16, cfg.CW)
    bias = na_tables(cfg, f"na_tables_{l}", small["na_rpb"])
    na_l, lse = na_fwd(cfg, f"na_fwd_{l}", P, bias)
    na_c, lse_c = na_ctx_fwd(cfg, f"na_ctx_fwd_{l}", P)
    mix = jnp.concatenate([ret, cv, jnp.concatenate([na_l, na_c], axis=0).astype(BF16)], axis=1)
    Y1 = mm_rowsharded(f"mm_out_{l}", mix, Wout4, l, F32, D)
    XM = resid_fwd(cfg, f"resid1_fwd_{l}", XS, Y1, mod["g1"])
    h2 = norm_mod_fwd(cfg, f"norm2_fwd_{l}", XM, small["norm2_g"], mod["sc2"], mod["sh2"])
    nbu = Wup4.shape[3]
    tnu = nbu // 2
    U2 = matmul(f"mm_up_{l}", h2, Wup4, contract="nn", grid=(8, L // tmA),
                a_spec=pl.BlockSpec((tmA, D), lambda n, m: (m, 0)),
                b_spec=pl.BlockSpec((None, None, D, tnu), lambda n, m: (n // 2, l, 0, n % 2)),
                out_shape=jax.ShapeDtypeStruct((2, L, cfg.DFF), F32),
                out_spec=pl.BlockSpec((None, tmA, tnu), lambda n, m: (n // 4, m, n % 4)), nk=1)
    A = ffn_act_fwd(cfg, f"ffn_act_fwd_{l}", U2, small["ffn_dw_w"], small["ffn_dw_b"])
    Y2 = mm_rowsharded(f"mm_down_{l}", A, Wdn4, l, F32, D // 2)
    XO = resid_fwd(cfg, f"resid2_fwd_{l}", XM, Y2, mod["g2"])
    sv.update(h1=h1, P=P, qr=qr, kr=kr, o2=o2, st=st, ycv=ycv, act=act, bias=bias, na_l=na_l, lse=lse,
              na_c=na_c, lse_c=lse_c, mix=mix, Y1=Y1, XM=XM, h2=h2, U2=U2, A=A, Y2=Y2)
    return XO, sv


def layer_bwd(cfg, l, dXO, sv, mod, wts, small, tabs, gbuf):
    L, D = dXO.shape
    off = _offsets(cfg)
    DIN = off["end"]
    Win4, Wout4, Wup4, Wdn4, Wpw4 = wts["w_in"], wts["w_out"], wts["ffn_up"], wts["ffn_down"], wts["conv_pw"]
    tmA, tmB = _tm(L, 4), _tm(L, 8)
    depth = Win4.shape[1]
    gb, gs, dm = {}, {}, {}
    P = sv["P"]
    dY2, dm["g2"] = resid_bwd(cfg, f"resid2_bwd_{l}", dXO, sv["Y2"], mod["g2"])
    nbd = Wdn4.shape[2]
    dA = matmul(f"mm_down_da_{l}", dY2, Wdn4, contract="nt", grid=(4, L // tmA),
                a_spec=pl.BlockSpec((tmA, D), lambda j, m: (m, 0)),
                b_spec=pl.BlockSpec((None, None, nbd, D), lambda j, m: (j, l, 0, 0)),
                out_shape=jax.ShapeDtypeStruct((L, cfg.DFF), F32),
                out_spec=pl.BlockSpec((tmA, nbd), lambda j, m: (m, j)), nk=1)
    gb["ffn_down"] = wgrad(cfg, f"mm_down_dw_{l}", sv["A"], dY2,
                           lambda rb, ri: pl.BlockSpec((rb, nbd), lambda j, m: (ri(m), j)),
                           lambda rb, ri: pl.BlockSpec((rb, D), lambda j, m: (ri(m), 0)),
                           jax.ShapeDtypeStruct((depth, 4, nbd, D), BF16),
                           pl.BlockSpec((None, None, nbd, D), lambda j, m: (l, j, 0, 0)), 4, gbuf.get("ffn_down"))
    dU2, dfw, dfb = ffn_act_bwd(cfg, f"ffn_act_bwd_{l}", sv["U2"], small["ffn_dw_w"], small["ffn_dw_b"], dA)
    gs["ffn_dw_w"], gs["ffn_dw_b"] = dfw, dfb
    nbu = Wup4.shape[3]
    tnu = nbu // 2
    dH2 = matmul(f"mm_up_dh_{l}", dU2, Wup4, contract="nt", grid=(L // tmA, 8),
                 a_spec=pl.BlockSpec((None, tmA, tnu), lambda m, n: (n // 4, m, n % 4)),
                 b_spec=pl.BlockSpec((None, None, D, tnu), lambda m, n: (n // 2, l, 0, n % 2)),
                 out_shape=jax.ShapeDtypeStruct((L, D), F32),
                 out_spec=pl.BlockSpec((tmA, D), lambda m, n: (m, 0)), nk=8)
    gb["ffn_up"] = wgrad(cfg, f"mm_up_dw_{l}", sv["h2"], dU2,
                         lambda rb, ri: pl.BlockSpec((rb, D), lambda n, m: (ri(m), 0)),
                         lambda rb, ri: pl.BlockSpec((None, rb, tnu), lambda n, m: (n // 4, ri(m), n % 4)),
                         jax.ShapeDtypeStruct((depth, 4, D, nbu), BF16),
                         pl.BlockSpec((None, None, D, tnu), lambda n, m: (l, n // 2, 0, n % 2)), 8, gbuf.get("ffn_up"))
    dXM, dm["sc2"], dm["sh2"], gs["norm2_g"] = norm_mod_bwd(
        cfg, f"norm2_bwd_{l}", dH2, sv["XM"], small["norm2_g"], mod["sc2"], dXO)
    dY1, dm["g1"] = resid_bwd(cfg, f"resid1_bwd_{l}", dXM, sv["Y1"], mod["g1"])
    nbo = Wout4.shape[2]
    dmix = matmul(f"mm_out_dmix_{l}", dY1, Wout4, contract="nt", grid=(4, L // tmA),
                  a_spec=pl.BlockSpec((tmA, D), lambda j, m: (m, 0)),
                  b_spec=pl.BlockSpec((None, None, nbo, D), lambda j, m: (j, l, 0, 0)),
                  out_shape=jax.ShapeDtypeStruct((L, D), F32),
                  out_spec=pl.BlockSpec((tmA, nbo), lambda j, m: (m, j)), nk=1)
    gb["w_out"] = wgrad(cfg, f"mm_out_dw_{l}", sv["mix"], dY1,
                        lambda rb, ri: pl.BlockSpec((rb, nbo), lambda j, m: (ri(m), j)),
                        lambda rb, ri: pl.BlockSpec((rb, D), lambda j, m: (ri(m), 0)),
                        jax.ShapeDtypeStruct((depth, 4, nbo, D), BF16),
                        pl.BlockSpec((None, None, nbo, D), lambda j, m: (l, j, 0, 0)), 4, gbuf.get("w_out"))
    RW = cfg.RH * cfg.RDV
    do, dlg, gs["ret_gn_g"] = ggn_bwd(cfg, f"ggn_bwd_{l}", dmix, sv["o2"], P, small["ret_gn_g"], 0)
    dqf, dqb, dkf, dkb, dvf, dvb, dlam = retention_bwd(
        cfg, f"ret_bwd_{l}", sv["qr"], sv["kr"], P, small["lam"], sv["st"], do)
    gs["lam"] = dlam[:, :, 0, 0]
    dlq, dlk = rope_bwd(cfg, f"rope_bwd_{l}", (dqf, dqb), (dkf, dkb), tabs["cos"], tabs["sin"])
    dlv = add_cast(cfg, f"ret_dv_{l}", dvf, dvb)
    dcv = cast_cols(cfg, f"conv_dcv_{l}", dmix, RW // LANE, cfg.CW // LANE, LANE)
    nbp = Wpw4.shape[2]
    dact = matmul(f"mm_pw_dact_{l}", dcv, Wpw4, contract="nt", grid=(4, L // tmA),
                  a_spec=pl.BlockSpec((tmA, cfg.CW), lambda j, m: (m, 0)),
                  b_spec=pl.BlockSpec((None, None, nbp, cfg.CW), lambda j, m: (j, l, 0, 0)),
                  out_shape=jax.ShapeDtypeStruct((L, cfg.CW), F32),
                  out_spec=pl.BlockSpec((tmA, nbp), lambda j, m: (m, j)), nk=1)
    gb["conv_pw"] = wgrad(cfg, f"mm_pw_dw_{l}", sv["act"], dcv,
                          lambda rb, ri: pl.BlockSpec((rb, nbp), lambda j, m: (ri(m), j)),
                          lambda rb, ri: pl.BlockSpec((rb, cfg.CW), lambda j, m: (ri(m), 0)),
                          jax.ShapeDtypeStruct((depth, 4, nbp, cfg.CW), BF16),
                          pl.BlockSpec((None, None, nbp, cfg.CW), lambda j, m: (l, j, 0, 0)), 4, gbuf.get("conv_pw"))
    dycv, gs["conv_ln_g"], gs["conv_ln_b"] = ln_silu_bwd(
        cfg, f"ln_silu_bwd_{l}", dact, sv["ycv"], small["conv_ln_g"], small["conv_ln_b"])
    dla, dlb, gs["conv_dw_w"], gs["conv_dw_b"] = glu_dwconv_bwd(cfg, f"dwconv_bwd_{l}", P, small["conv_dw_w"], dycv)
    nac0 = (RW + cfg.CW) // LANE
    dnq_l, dnk, dnv, dsb = na_bwd(cfg, f"na_bwd_{l}", P, sv["bias"], sv["na_l"], sv["lse"], dmix, nac0)
    dnq_c, dnk_c, dnv_c = na_ctx_bwd(cfg, f"na_ctx_bwd_{l}", P, sv["na_c"], sv["lse_c"], dmix, nac0, dnk, dnv)
    gs["na_rpb"] = na_rpb_grad(cfg, f"na_rpb_{l}", dsb)
    dnq = jnp.concatenate([dnq_l, dnq_c], axis=0)
    dnk = jnp.concatenate([dnk[:cfg.T].astype(BF16), dnk_c], axis=0)
    dnv = jnp.concatenate([dnv[:cfg.T].astype(BF16), dnv_c], axis=0)
    dP = jnp.concatenate([dlq, dlk, dlv, dlg, dla, dlb, dnq, dnk, dnv], axis=1)
    nbi = Win4.shape[3]
    dH1 = matmul(f"mm_in_dh_{l}", dP, Win4, contract="nt", grid=(L // tmA, 4),
                 a_spec=pl.BlockSpec((tmA, nbi), lambda m, n: (m, n)),
                 b_spec=pl.BlockSpec((None, None, D, nbi), lambda m, n: (n, l, 0, 0)),
                 out_shape=jax.ShapeDtypeStruct((L, D), F32),
                 out_spec=pl.BlockSpec((tmA, D), lambda m, n: (m, 0)), nk=4)
    gb["w_in"] = wgrad(cfg, f"mm_in_dw_{l}", sv["h1"], dP,
                       lambda rb, ri: pl.BlockSpec((rb, D), lambda n, m: (ri(m), 0)),
                       lambda rb, ri: pl.BlockSpec((rb, nbi), lambda n, m: (ri(m), n)),
                       jax.ShapeDtypeStruct((depth, 4, D, nbi), BF16),
                       pl.BlockSpec((None, None, D, nbi), lambda n, m: (l, n, 0, 0)), 4, gbuf.get("w_in"))
    dXS, dm["sc1"], dm["sh1"], gs["norm1_g"] = norm_mod_bwd(
        cfg, f"norm1_bwd_{l}", dH1, sv["XS"], small["norm1_g"], mod["sc1"], dXM)
    return dXS, gb, gs, dm


def _layer_small(cfg, l, sp):
    DFF = cfg.DFF
    fw = sp["ffn_dw_w"][l].reshape(3, 2, DFF).transpose(1, 0, 2)
    fw = jnp.concatenate([fw, jnp.zeros((2, 5, DFF), F32)], axis=1)
    cw = jnp.concatenate([sp["conv_dw_w"][l], jnp.zeros((32 - cfg.CK, cfg.CW), F32)], axis=0)
    return dict(
        norm1_g=sp["norm1_g"][l][None], norm2_g=sp["norm2_g"][l][None],
        lam=jax.nn.log_sigmoid(sp["ret_decay"][l]), ret_gn_g=sp["ret_gn_g"][l][None],
        conv_dw_w=cw, conv_dw_b=sp["conv_dw_b"][l][None], conv_ln_g=sp["conv_ln_g"][l][None],
        conv_ln_b=sp["conv_ln_b"][l][None], na_rpb=sp["na_rpb"][l],
        ffn_dw_w=fw, ffn_dw_b=sp["ffn_dw_b"][l].reshape(2, 1, DFF))


def local_step(cfg, x, ctx, tgt, mods, wts, sp, grads_ready=lambda l, gb: None):
    depth = sp["norm1_g"].shape[0]
    cos, sin = rope_tables(cfg)
    tabs = dict(cos=cos, sin=sin)
    XS = jnp.concatenate([x, ctx], axis=0)
    smalls = [_layer_small(cfg, l, sp) for l in range(depth)]
    saves, lw = [], []
    for l in range(depth):
        lw.append(wts(l, XS))
        XS, sv = layer_fwd(cfg, l, XS, mods[l], lw[l], smalls[l], tabs)
        saves.append(sv)
    ls, dX, dfg = final_loss(cfg, "final_loss", XS, sp["final_g"][None], tgt)
    gb, gss, dms = {}, [None] * depth, [None] * depth
    token = None
    for l in reversed(range(depth)):
        mod = mods[l] if token is None else {**mods[l], "g2": mods[l]["g2"] + token}
        dX, gb, gss[l], dms[l] = layer_bwd(cfg, l, dX, saves[l], mod, lw[l], smalls[l], tabs, gb)
        token = grads_ready(l, gb)
    return ls[0, 0], dX[:cfg.T], gb, gss, dms, dfg[0]


MESH = pl.DeviceIdType.MESH
N_DEV = 8
N_CHIP = 4
BIG = ("w_in", "w_out", "ffn_up", "ffn_down", "conv_pw")
_ANY = pl.BlockSpec(memory_space=pl.ANY)


def _place():
    x, y, c = lax.axis_index("x"), lax.axis_index("y"), lax.axis_index("c")
    chips = [(1 - x, y), (x, 1 - y), (1 - x, 1 - y)]
    return x, y, c, chips


def allgather8(name, xs):
    m_per, n = xs.shape

    def body(x_ref, out_ref, send_sems, recv_sems, local_sem):
        x, y, c, chips = _place()
        me, sibling = (x, y, c), (x, y, 1 - c)

        def rows(px, py, pc):
            return out_ref.at[pl.ds((4 * px + 2 * py + pc) * m_per, m_per), :]

        def copy(k, block, to, src=None):
            return pltpu.make_async_remote_copy(
                src_ref=rows(*block) if src is None else src, dst_ref=rows(*block),
                send_sem=send_sems.at[k], recv_sem=recv_sems.at[k], device_id=to, device_id_type=MESH)

        mine = pltpu.make_async_copy(x_ref, rows(*me), local_sem)
        mine.start()
        first = [copy(0, me, sibling, src=x_ref)]
        first += [copy(1 + j, me, (*chip, c), src=x_ref) for j, chip in enumerate(chips)]
        for cp in first:
            cp.start()
        passed = [copy(4 + j, (*chip, c), sibling) for j, chip in enumerate(chips)]
        for j, chip in enumerate(chips):
            copy(1 + j, (*chip, c), me).wait_recv()
            passed[j].start()
        copy(0, sibling, me).wait_recv()
        for j, chip in enumerate(chips):
            copy(4 + j, (*chip, 1 - c), me).wait_recv()
        for cp in first + passed:
            cp.wait_send()
        mine.wait()

    return _pc(body, name=name, out_shape=jax.ShapeDtypeStruct((N_DEV * m_per, n), xs.dtype),
               in_specs=[pl.BlockSpec(memory_space=pltpu.VMEM)], out_specs=pl.BlockSpec(memory_space=pltpu.VMEM),
               scratch_shapes=[pltpu.SemaphoreType.DMA((7,)), pltpu.SemaphoreType.DMA((7,)), pltpu.SemaphoreType.DMA],
               compiler_params=pltpu.CompilerParams(vmem_limit_bytes=VMEM_LIMIT))(xs)


def _wpiece(ref, layer, chip_idx, half):
    rh = ref.shape[2] // 2
    return ref.at[chip_idx, layer, pl.ds(half * rh, rh)]


def _wcopy(ref, layer, chip_idx, half, send_sems, recv_sems, k, to):
    piece = _wpiece(ref, layer, chip_idx, half)
    return pltpu.make_async_remote_copy(src_ref=piece, dst_ref=piece, send_sem=send_sems.at[k],
                                        recv_sem=recv_sems.at[k], device_id=to, device_id_type=MESH)


def _w_ici_sends(outs, layer, send_sems, recv_sems):
    x, y, c, chips = _place()
    return [_wcopy(outs[a], layer, 2 * x + y, c, send_sems, recv_sems, 3 * a + t, (*chip, c))
            for a in range(len(outs)) for t, chip in enumerate(chips)]


def _w_ici_landed(outs, layer, send_sems, recv_sems):
    x, y, c, chips = _place()
    return [_wcopy(outs[a], layer, 2 * chip[0] + chip[1], c, send_sems, recv_sems, 3 * a + t, (x, y, c))
            for a in range(len(outs)) for t, chip in enumerate(chips)]


def _w_forward(outs, layer, send_sems, recv_sems, base):
    x, y, c, chips = _place()
    n = len(outs)
    sends = [_wcopy(outs[a], layer, 2 * chip[0] + chip[1], c, send_sems, recv_sems, base + 3 * a + t, (x, y, 1 - c))
             for a in range(n) for t, chip in enumerate(chips)]
    recvs = [_wcopy(outs[a], layer, 2 * chip[0] + chip[1], 1 - c, send_sems, recv_sems, base + 3 * a + t, (x, y, c))
             for a in range(n) for t, chip in enumerate(chips)]
    return sends, recvs


def allgather_layer(name, bufs, layer):
    n = len(bufs)

    def body(*refs):
        outs = refs[n:2 * n]
        send_sems, recv_sems = refs[2 * n:]
        sent = _w_ici_sends(outs, layer, send_sems, recv_sems)
        for cp in sent:
            cp.start()
        fwd, from_sib = _w_forward(outs, layer, send_sems, recv_sems, 3 * n)
        for landed, fw in zip(_w_ici_landed(outs, layer, send_sems, recv_sems), fwd):
            landed.wait_recv()
            fw.start()
        for cp in from_sib:
            cp.wait_recv()
        for cp in sent + fwd:
            cp.wait_send()

    return _pc(body, name=name, out_shape=[jax.ShapeDtypeStruct(b.shape, b.dtype) for b in bufs],
               in_specs=[_ANY] * n, out_specs=[_ANY] * n, input_output_aliases={a: a for a in range(n)},
               scratch_shapes=[pltpu.SemaphoreType.DMA((6 * n,)), pltpu.SemaphoreType.DMA((6 * n,))])(*bufs)


_HBM = pl.BlockSpec(memory_space=pltpu.HBM)
_SEM = pl.BlockSpec(memory_space=pltpu.SEMAPHORE)
_EFFECT = pltpu.SideEffectType.DATAFLOW_SIDE_EFFECTING


def allgather_layer_start(name, bufs, layer, after):
    n = len(bufs)

    def body(*refs):
        send_sems, recv_sems = refs[n + 1:n + 3]
        outs = refs[n + 3:2 * n + 3]
        token = refs[2 * n + 3]
        for cp in _w_ici_sends(outs, layer, send_sems, recv_sems):
            cp.start()
        token[...] = jnp.zeros_like(token)

    res = _pc(body, name=name,
              out_shape=(pltpu.SemaphoreType.DMA((3 * n,)), pltpu.SemaphoreType.DMA((3 * n,)),
                         *[pltpu.HBM(b.shape, b.dtype) for b in bufs], jax.ShapeDtypeStruct((8, LANE), F32)),
              in_specs=[_HBM] * n + [_ANY],
              out_specs=(_SEM, _SEM, *([_HBM] * n), pl.BlockSpec(memory_space=pltpu.VMEM)),
              input_output_aliases={a: a + 2 for a in range(n)},
              compiler_params=pltpu.CompilerParams(has_side_effects=_EFFECT))(
                  *[pltpu.with_memory_space_constraint(b, pltpu.HBM) for b in bufs], after)
    return res[0], res[1], list(res[2:2 + n]), res[2 + n]


def allgather_layer_wait(name, bufs, send_sems, recv_sems, after, layer):
    n = len(bufs)

    def body(*refs):
        ins = refs[:n]
        send_sems, recv_sems = refs[n:n + 2]
        for cp in _w_ici_sends(ins, layer, send_sems, recv_sems):
            cp.wait_send()
        for cp in _w_ici_landed(ins, layer, send_sems, recv_sems):
            cp.wait_recv()

    return _pc(body, name=name, out_shape=tuple(pltpu.HBM(b.shape, b.dtype) for b in bufs),
               in_specs=[_HBM] * n + [_SEM, _SEM, _ANY], out_specs=tuple([_HBM] * n),
               input_output_aliases={a: a for a in range(n)},
               compiler_params=pltpu.CompilerParams(has_side_effects=_EFFECT))(*bufs, send_sems, recv_sems, after)


def forward_halves(name, bufs, layer):
    n = len(bufs)

    def body(*refs):
        outs = refs[n:2 * n]
        send_sems, recv_sems = refs[2 * n:]
        fwd, from_sib = _w_forward(outs, layer, send_sems, recv_sems, 0)
        for cp in fwd:
            cp.start()
        for cp in from_sib:
            cp.wait_recv()
        for cp in fwd:
            cp.wait_send()

    return _pc(body, name=name, out_shape=[jax.ShapeDtypeStruct(b.shape, b.dtype) for b in bufs],
               in_specs=[_ANY] * n, out_specs=[_ANY] * n, input_output_aliases={a: a for a in range(n)},
               scratch_shapes=[pltpu.SemaphoreType.DMA((3 * n,)), pltpu.SemaphoreType.DMA((3 * n,))])(*bufs)


def exchange_rows(name, grads, layer):
    n = len(grads)

    def body(*refs):
        ins, outs = refs[:n], refs[n:2 * n]
        send_sems, recv_sems = refs[2 * n:]
        x, y, c, _ = _place()
        cps = []
        for a in range(n):
            rh = ins[a].shape[2] // 2
            cps.append(pltpu.make_async_remote_copy(
                src_ref=ins[a].at[layer, pl.ds(0, N_CHIP), pl.ds((1 - c) * rh, rh)], dst_ref=outs[a],
                send_sem=send_sems.at[a], recv_sem=recv_sems.at[a], device_id=(x, y, 1 - c), device_id_type=MESH))
        for cp in cps:
            cp.start()
        for cp in cps:
            cp.wait()

    return _pc(body, name=name,
               out_shape=[jax.ShapeDtypeStruct((N_CHIP, g.shape[2] // 2, g.shape[3]), g.dtype) for g in grads],
               in_specs=[_ANY] * n, out_specs=[_ANY] * n,
               scratch_shapes=[pltpu.SemaphoreType.DMA((n,)), pltpu.SemaphoreType.DMA((n,))])(*grads)


def _scatter_sends(parts, lands, send_sems, recv_sems):
    x, y, c, chips = _place()
    return [pltpu.make_async_remote_copy(
        src_ref=parts[a].at[2 * chip[0] + chip[1]], dst_ref=lands[a].at[2 * x + y], send_sem=send_sems.at[3 * a + t],
        recv_sem=recv_sems.at[3 * a + t], device_id=(*chip, c), device_id_type=MESH)
        for a in range(len(parts)) for t, chip in enumerate(chips)]


def _scatter_landed(lands, send_sems, recv_sems):
    x, y, c, chips = _place()
    return [pltpu.make_async_remote_copy(
        src_ref=lands[a].at[2 * chip[0] + chip[1]], dst_ref=lands[a].at[2 * chip[0] + chip[1]],
        send_sem=send_sems.at[3 * a + t], recv_sem=recv_sems.at[3 * a + t], device_id=(x, y, c), device_id_type=MESH)
        for a in range(len(lands)) for t, chip in enumerate(chips)]


def scatter_slices(name, parts, lands):
    n = len(parts)

    def body(*refs):
        ins, outs = refs[:n], refs[2 * n:3 * n]
        send_sems, recv_sems = refs[3 * n:]
        cps = _scatter_sends(ins, outs, send_sems, recv_sems)
        for cp in cps:
            cp.start()
        for cp in _scatter_landed(outs, send_sems, recv_sems):
            cp.wait_recv()
        for cp in cps:
            cp.wait_send()

    return _pc(body, name=name, out_shape=[jax.ShapeDtypeStruct(p.shape, p.dtype) for p in lands],
               in_specs=[_ANY] * (2 * n), out_specs=[_ANY] * n,
               input_output_aliases={n + a: a for a in range(n)},
               scratch_shapes=[pltpu.SemaphoreType.DMA((3 * n,)), pltpu.SemaphoreType.DMA((3 * n,))])(*parts, *lands)


def scatter_slices_start(name, parts, lands):
    n = len(parts)

    def body(*refs):
        send_sems, recv_sems = refs[2 * n:2 * n + 2]
        p_out, l_out = refs[2 * n + 2:3 * n + 2], refs[3 * n + 2:4 * n + 2]
        token = refs[4 * n + 2]
        for cp in _scatter_sends(p_out, l_out, send_sems, recv_sems):
            cp.start()
        token[...] = jnp.zeros_like(token)

    both = list(parts) + list(lands)
    res = _pc(body, name=name,
              out_shape=(pltpu.SemaphoreType.DMA((3 * n,)), pltpu.SemaphoreType.DMA((3 * n,)),
                         *[pltpu.HBM(b.shape, b.dtype) for b in both], jax.ShapeDtypeStruct((8, LANE), F32)),
              in_specs=[_HBM] * (2 * n),
              out_specs=(_SEM, _SEM, *([_HBM] * (2 * n)), pl.BlockSpec(memory_space=pltpu.VMEM)),
              input_output_aliases={a: a + 2 for a in range(2 * n)},
              compiler_params=pltpu.CompilerParams(has_side_effects=_EFFECT))(
                  *[pltpu.with_memory_space_constraint(b, pltpu.HBM) for b in both])
    return res[0], res[1], list(res[2:2 + n]), list(res[2 + n:2 + 2 * n]), res[2 + 2 * n]


def scatter_slices_wait(name, parts, lands, send_sems, recv_sems, after):
    n = len(parts)

    def body(*refs):
        p_in, l_in = refs[:n], refs[n:2 * n]
        send_sems, recv_sems = refs[2 * n:2 * n + 2]
        for cp in _scatter_sends(p_in, l_in, send_sems, recv_sems):
            cp.wait_send()
        for cp in _scatter_landed(l_in, send_sems, recv_sems):
            cp.wait_recv()

    both = list(parts) + list(lands)
    res = _pc(body, name=name, out_shape=tuple(pltpu.HBM(b.shape, b.dtype) for b in both),
              in_specs=[_HBM] * (2 * n) + [_SEM, _SEM, _ANY], out_specs=tuple([_HBM] * (2 * n)),
              input_output_aliases={a: a for a in range(2 * n)},
              compiler_params=pltpu.CompilerParams(has_side_effects=_EFFECT))(*both, send_sems, recv_sems, after)
    return list(res[n:])


def share_rows(name, bufs):
    n = len(bufs)

    def body(*refs):
        outs = refs[n:2 * n]
        send_sems, recv_sems = refs[2 * n:]
        x, y, c, _ = _place()

        def half(a, h):
            return outs[a].at[pl.ds(0, 2), h]

        cps = [pltpu.make_async_remote_copy(
            src_ref=half(a, c), dst_ref=half(a, c), send_sem=send_sems.at[a], recv_sem=recv_sems.at[a],
            device_id=(x, y, 1 - c), device_id_type=MESH) for a in range(n)]
        for cp in cps:
            cp.start()
        for a in range(n):
            pltpu.make_async_remote_copy(
                src_ref=half(a, 1 - c), dst_ref=half(a, 1 - c), send_sem=send_sems.at[a],
                recv_sem=recv_sems.at[a], device_id=(x, y, c), device_id_type=MESH).wait_recv()
        for cp in cps:
            cp.wait_send()

    return _pc(body, name=name, out_shape=[jax.ShapeDtypeStruct(b.shape, b.dtype) for b in bufs],
               in_specs=[_ANY] * n, out_specs=[_ANY] * n, input_output_aliases={a: a for a in range(n)},
               scratch_shapes=[pltpu.SemaphoreType.DMA((n,)), pltpu.SemaphoreType.DMA((n,))])(*bufs)


def _row_tile(R, C, nbytes=1 << 20):
    t = 8
    while t * 2 <= R and R % (t * 2) == 0 and t * 2 * C * 4 <= nbytes:
        t *= 2
    assert R % t == 0
    return t


def to_bf16_block(name, w, chip_arr, layer):
    _, R, C = w.shape
    tr = _row_tile(R, C)

    def body(j_ref, w_ref, o_ref):
        o_ref[...] = w_ref[...].astype(o_ref.dtype)

    gs = pltpu.PrefetchScalarGridSpec(
        num_scalar_prefetch=1, grid=(R // tr,),
        in_specs=[pl.BlockSpec((None, tr, C), lambda i, j_ref: (layer, i, 0))],
        out_specs=pl.BlockSpec((None, None, tr, C), lambda i, j_ref: (j_ref[0], layer, i, 0)))
    return _pc(body, name=name, grid_spec=gs, out_shape=jax.ShapeDtypeStruct((N_CHIP,) + w.shape, BF16),
               compiler_params=_cp(("parallel",)))(chip_arr, w)


def add_rows(name, g, ra, c_arr, layer):
    _, _, R, C = g.shape
    rh = R // 2
    tr = _row_tile(rh, C)
    nb = rh // tr

    def body(c_ref, g_ref, r_ref, o_ref):
        o_ref[...] = (g_ref[...].astype(F32) + r_ref[...].astype(F32)).astype(o_ref.dtype)

    gs = pltpu.PrefetchScalarGridSpec(
        num_scalar_prefetch=1, grid=(N_CHIP, nb),
        in_specs=[pl.BlockSpec((None, None, tr, C), lambda j, i, c_ref: (layer, j, c_ref[0] * nb + i, 0)),
                  pl.BlockSpec((None, tr, C), lambda j, i, c_ref: (j, i, 0))],
        out_specs=pl.BlockSpec((None, tr, C), lambda j, i, c_ref: (j, i, 0)))
    return _pc(body, name=name, grid_spec=gs, out_shape=jax.ShapeDtypeStruct(ra.shape, BF16),
               compiler_params=_cp(("parallel", "parallel")))(c_arr, g, ra)


def sum_rows_into(name, landed, c_arr, layer, into):
    n, rh, C = landed.shape
    tr = _row_tile(rh, C, nbytes=1 << 19)

    def body(*refs):
        g_ref, o_ref = refs[1], refs[-1]
        acc = g_ref[0].astype(F32)
        for j in range(1, n):
            acc = acc + g_ref[j].astype(F32)
        o_ref[...] = acc

    in_specs, args, alias = [pl.BlockSpec((n, tr, C), lambda i, c_ref: (0, i, 0))], (c_arr, landed), {}
    if into is not None:
        in_specs, args, alias = in_specs + [_ANY], args + (into,), {2: 0}
    gs = pltpu.PrefetchScalarGridSpec(
        num_scalar_prefetch=1, grid=(rh // tr,), in_specs=in_specs,
        out_specs=pl.BlockSpec((None, None, tr, C), lambda i, c_ref: (layer, c_ref[0], i, 0)))
    return _pc(body, name=name, grid_spec=gs, out_shape=jax.ShapeDtypeStruct((2, 2, rh, C), F32),
               input_output_aliases=alias, compiler_params=_cp(("parallel",)))(*args)


def own_row(name, part, chip_arr):
    _, R, C = part.shape
    tr = _row_tile(R, C)

    def body(j_ref, p_ref, o_ref):
        o_ref[...] = p_ref[...]

    gs = pltpu.PrefetchScalarGridSpec(
        num_scalar_prefetch=1, grid=(R // tr,),
        in_specs=[pl.BlockSpec((None, tr, C), lambda i, j_ref: (j_ref[0], i, 0))],
        out_specs=pl.BlockSpec((None, tr, C), lambda i, j_ref: (j_ref[0], i, 0)))
    return _pc(body, name=name, grid_spec=gs, out_shape=jax.ShapeDtypeStruct(part.shape, part.dtype),
               compiler_params=_cp(("parallel",)))(chip_arr, part)


def sum_leading(name, g, plane=None):
    n, R, C = g.shape
    tr = _row_tile(R, C, nbytes=(1 << 21) // n)

    def body(*refs):
        g_ref, o_ref = refs[-2:]
        acc = g_ref[0].astype(F32)
        for j in range(1, n):
            acc = acc + g_ref[j].astype(F32)
        o_ref[...] = acc

    if plane is None:
        return _pc(body, name=name, grid=(R // tr,), in_specs=[pl.BlockSpec((n, tr, C), lambda i: (0, i, 0))],
                   out_specs=pl.BlockSpec((tr, C), lambda i: (i, 0)), out_shape=jax.ShapeDtypeStruct((R, C), F32),
                   compiler_params=_cp(("parallel",)))(g)
    count, idx = plane
    gs = pltpu.PrefetchScalarGridSpec(
        num_scalar_prefetch=1, grid=(R // tr,),
        in_specs=[pl.BlockSpec((n, tr, C), lambda i, p_ref: (0, i, 0))],
        out_specs=pl.BlockSpec((None, tr, C), lambda i, p_ref: (p_ref[0], i, 0)))
    return _pc(body, name=name, grid_spec=gs, out_shape=jax.ShapeDtypeStruct((count, R, C), F32),
               compiler_params=_cp(("parallel",)))(idx, g)


def adamw(name, w, g, m, v):
    R, C = w.shape
    tr = _row_tile(R, C)

    def body(w_ref, g_ref, m_ref, v_ref, d_ref, mo_ref, vo_ref):
        gv = g_ref[...]
        mn = ADAM_B1 * m_ref[...] + (1.0 - ADAM_B1) * gv
        vn = ADAM_B2 * v_ref[...] + (1.0 - ADAM_B2) * (gv * gv)
        m_hat = mn / (1.0 - ADAM_B1 ** ADAM_STEP)
        v_hat = vn / (1.0 - ADAM_B2 ** ADAM_STEP)
        d_ref[...] = -ADAM_LR * (m_hat / (jnp.sqrt(v_hat) + ADAM_EPS) + ADAM_WD * w_ref[...])
        mo_ref[...] = mn
        vo_ref[...] = vn

    spec = pl.BlockSpec((tr, C), lambda i: (i, 0))
    shp = jax.ShapeDtypeStruct((R, C), F32)
    return _pc(body, name=name, grid=(R // tr,), in_specs=[spec] * 4, out_specs=[spec] * 3,
               out_shape=[shp] * 3, compiler_params=_cp(("parallel",)))(w, g, m, v)


_ADA_TN = 512


def adaln_fwd(name, cond, w, b):
    _, D, N = w.shape
    tn = min(_ADA_TN, N)

    def body(c_ref, w_ref, b_ref, o_ref):
        s = _silu(c_ref[...]).astype(BF16)
        o_ref[...] = dot_nn(s, w_ref[...].astype(BF16)) + b_ref[...]

    return _pc(body, name=name, grid=(2, N // tn),
               in_specs=[pl.BlockSpec((16, D), lambda l, n: (0, 0)),
                         pl.BlockSpec((None, D, tn), lambda l, n: (l, 0, n)),
                         pl.BlockSpec((None, 1, tn), lambda l, n: (l, 0, n))],
               out_specs=pl.BlockSpec((None, 16, tn), lambda l, n: (l, 0, n)),
               out_shape=jax.ShapeDtypeStruct((2, 16, N), F32),
               compiler_params=_cp(("parallel", "parallel")))(cond, w, b)


def adaln_bwd(name, cond, w, dm):
    _, D, N = w.shape
    tn = min(_ADA_TN, N)

    def body(c_ref, w_ref, dm_ref, gw_ref, ds_ref):
        first = jnp.logical_and(pl.program_id(0) == 0, pl.program_id(1) == 0)
        s = _silu(c_ref[...]).astype(BF16)
        dmb = dm_ref[...].astype(BF16)
        gw_ref[...] = dot_tn(s, dmb)
        p = dot_nt(dmb, w_ref[...].astype(BF16))

        @pl.when(first)
        def _():
            ds_ref[...] = p

        @pl.when(jnp.logical_not(first))
        def _():
            ds_ref[...] += p

    return _pc(body, name=name, grid=(2, N // tn),
               in_specs=[pl.BlockSpec((16, D), lambda l, n: (0, 0)),
                         pl.BlockSpec((None, D, tn), lambda l, n: (l, 0, n)),
                         pl.BlockSpec((None, 16, tn), lambda l, n: (l, 0, n))],
               out_specs=[pl.BlockSpec((None, D, tn), lambda l, n: (l, 0, n)),
                          pl.BlockSpec((16, D), lambda l, n: (0, 0))],
               out_shape=[jax.ShapeDtypeStruct((2, D, N), F32), jax.ShapeDtypeStruct((16, D), F32)],
               compiler_params=_cp(("arbitrary", "arbitrary")))(cond, w, dm)


def cctx_grad(name, parts, c_ctx):
    def body(p_ref, c_ref, o_ref):
        acc = p_ref[0]
        for j in range(1, N_CHIP):
            acc = acc + p_ref[j]
        o_ref[...] = acc * _dsilu(c_ref[...])

    return _pc(body, name=name, out_shape=jax.ShapeDtypeStruct(c_ctx.shape, F32))(parts, c_ctx)


def _pack(arrs, rows_mult=8):
    flat = jnp.concatenate([a.reshape(-1) for a in arrs])
    n = flat.shape[0]
    unit = rows_mult * LANE
    tot = -(-n // unit) * unit
    return jnp.concatenate([flat, jnp.zeros((tot - n,), F32)]).reshape(tot // LANE, LANE)


def _unpack(flat, shapes):
    out, o = [], 0
    for s in shapes:
        n = int(np.prod(s))
        out.append(flat[o:o + n].reshape(s))
        o += n
    return out


MOD_NAMES = ("sh1", "sc1", "g1", "sh2", "sc2", "g2")


def kernel(x, c, ctx, c_ctx, w_ada, b_ada, norm1_g, w_in, ret_decay, ret_gn_g, conv_dw_w, conv_dw_b, conv_ln_g, conv_ln_b, conv_pw, na_rpb, w_out, norm2_g, ffn_up, ffn_dw_w, ffn_dw_b, ffn_down, final_g, loss_target, m_c_ctx, m_w_ada, m_b_ada, m_norm1_g, m_w_in, m_ret_decay, m_ret_gn_g, m_conv_dw_w, m_conv_dw_b, m_conv_ln_g, m_conv_ln_b, m_conv_pw, m_na_rpb, m_w_out, m_norm2_g, m_ffn_up, m_ffn_dw_w, m_ffn_dw_b, m_ffn_down, m_final_g, v_c_ctx, v_w_ada, v_b_ada, v_norm1_g, v_w_in, v_ret_decay, v_ret_gn_g, v_conv_dw_w, v_conv_dw_b, v_conv_ln_g, v_conv_ln_b, v_conv_pw, v_na_rpb, v_w_out, v_norm2_g, v_ffn_up, v_ffn_dw_w, v_ffn_dw_b, v_ffn_down, v_final_g):
    cfg = make_cfg(D=x.shape[2], T=x.shape[1], TC=ctx.shape[1], RH=ret_decay.shape[2], CW=conv_dw_b.shape[1],
                   NH=na_rpb.shape[1], DFF=ffn_dw_b.shape[1] // 2)
    D, T = cfg.D, cfg.T
    W = dict(c_ctx=c_ctx, w_ada=w_ada, b_ada=b_ada, norm1_g=norm1_g, w_in=w_in, ret_decay=ret_decay, ret_gn_g=ret_gn_g,
             conv_dw_w=conv_dw_w, conv_dw_b=conv_dw_b, conv_ln_g=conv_ln_g, conv_ln_b=conv_ln_b, conv_pw=conv_pw,
             na_rpb=na_rpb, w_out=w_out, norm2_g=norm2_g, ffn_up=ffn_up, ffn_dw_w=ffn_dw_w, ffn_dw_b=ffn_dw_b,
             ffn_down=ffn_down, final_g=final_g)
    Mo = dict(c_ctx=m_c_ctx, w_ada=m_w_ada, b_ada=m_b_ada, norm1_g=m_norm1_g, w_in=m_w_in, ret_decay=m_ret_decay,
              ret_gn_g=m_ret_gn_g, conv_dw_w=m_conv_dw_w, conv_dw_b=m_conv_dw_b, conv_ln_g=m_conv_ln_g,
              conv_ln_b=m_conv_ln_b, conv_pw=m_conv_pw, na_rpb=m_na_rpb, w_out=m_w_out, norm2_g=m_norm2_g,
              ffn_up=m_ffn_up, ffn_dw_w=m_ffn_dw_w, ffn_dw_b=m_ffn_dw_b, ffn_down=m_ffn_down, final_g=m_final_g)
    Vo = dict(c_ctx=v_c_ctx, w_ada=v_w_ada, b_ada=v_b_ada, norm1_g=v_norm1_g, w_in=v_w_in, ret_decay=v_ret_decay,
              ret_gn_g=v_ret_gn_g, conv_dw_w=v_conv_dw_w, conv_dw_b=v_conv_dw_b, conv_ln_g=v_conv_ln_g,
              conv_ln_b=v_conv_ln_b, conv_pw=v_conv_pw, na_rpb=v_na_rpb, w_out=v_w_out, norm2_g=v_norm2_g,
              ffn_up=v_ffn_up, ffn_dw_w=v_ffn_dw_w, ffn_dw_b=v_ffn_dw_b, ffn_down=v_ffn_down, final_g=v_final_g)
    order = list(W)
    xi, yi, ci = lax.axis_index("x"), lax.axis_index("y"), lax.axis_index("c")
    chip = 2 * xi + yi
    dev = 4 * xi + 2 * yi + ci
    NA = w_ada.shape[2]
    ncw, nfw = conv_dw_w.shape[2], ffn_dw_w.shape[2]

    g_in = allgather8("ag_small_in", _pack([c[0], conv_dw_w, ffn_dw_w])).reshape(N_DEV, -1)
    c8 = g_in[:, :D]
    by_chip = g_in[0::2, D:]
    cw_parts, fw_parts = [], []
    for j in range(N_CHIP):
        a, b = _unpack(by_chip[j], [conv_dw_w.shape, ffn_dw_w.shape])
        cw_parts.append(a)
        fw_parts.append(b)
    conv_dw_w_full = jnp.concatenate(cw_parts, axis=2)
    ffn_dw_w_full = jnp.concatenate(fw_parts, axis=2)
    cond = jnp.concatenate([c8, c_ctx[None], jnp.zeros((16 - N_DEV - 1, D), F32)], axis=0)

    b_sh = lax.dynamic_slice(b_ada, (0, chip * NA), (2, NA)).reshape(2, 1, NA)
    m_sh = adaln_fwd("adaln_fwd", cond, w_ada, b_sh)
    m_all = allgather8("ag_mod", m_sh.reshape(2 * 16, NA)).reshape(N_DEV, 2, 16, NA)[0::2]
    m_all = m_all.transpose(1, 2, 0, 3).reshape(2, 16, N_CHIP * NA)
    mods = []
    for l in range(2):
        lat = lax.dynamic_slice(m_all[l], (dev, 0), (1, N_CHIP * NA))[0]
        cx = m_all[l, N_DEV]
        mods.append({nm: jnp.stack([lat[k * D:(k + 1) * D], cx[k * D:(k + 1) * D]], 0)[:, None, :]
                     for k, nm in enumerate(MOD_NAMES)})

    c_arr = jnp.reshape(ci, (1,)).astype(jnp.int32)
    chip_arr = jnp.reshape(chip, (1,)).astype(jnp.int32)
    w0 = allgather_layer("ag_weights_0", [to_bf16_block(f"to_bf16_{nm}_0", W[nm], chip_arr, 0) for nm in BIG], 0)
    s_sem, r_sem, w1, token = allgather_layer_start(
        "ag_weights_1_start", [to_bf16_block(f"to_bf16_{nm}_1", W[nm], chip_arr, 1) for nm in BIG], 1, w0[0])
    mods[0] = {**mods[0], "sc1": mods[0]["sc1"] + token[0, 0]}

    def wts(l, stream):
        if l == 0:
            return dict(zip(BIG, w0))
        landed = allgather_layer_wait("ag_weights_1_wait", w1, s_sem, r_sem, stream, 1)
        return dict(zip(BIG, forward_halves("ag_weights_1_fwd", list(landed), 1)))

    sp = dict(norm1_g=norm1_g, norm2_g=norm2_g, ret_decay=ret_decay, ret_gn_g=ret_gn_g, conv_dw_w=conv_dw_w_full,
              conv_dw_b=conv_dw_b, conv_ln_g=conv_ln_g, conv_ln_b=conv_ln_b, na_rpb=na_rpb, ffn_dw_w=ffn_dw_w_full,
              ffn_dw_b=ffn_dw_b, final_g=final_g)
    depth = norm1_g.shape[0]
    flying = {}

    def reduce_rows(l, gb):
        from_sib = exchange_rows(f"rs_exchange_{l}", [gb[nm] for nm in BIG], l)
        part = [add_rows(f"rs_add_{nm}_{l}", gb[nm], r, c_arr, l) for nm, r in zip(BIG, from_sib)]
        return part, [own_row(f"rs_own_{nm}_{l}", p, chip_arr) for nm, p in zip(BIG, part)]

    def grads_ready(l, gb):
        if l != depth - 1 or depth == 1:
            return None
        part, lands = reduce_rows(l, gb)
        s_sem, r_sem, part, lands, tok = scatter_slices_start(f"rs_scatter_{l}_start", part, lands)
        flying.update(part=part, lands=lands, sems=(s_sem, r_sem))
        return tok[0, 0]

    loss_l, gx, gb, gss, dms, dfg = local_step(cfg, x[0], ctx[0], loss_target[0], mods, wts, sp, grads_ready)
    loss = lax.psum(loss_l, ("x", "y", "c"))

    dmseg = jnp.stack([jnp.stack([jnp.concatenate([dms[l][nm][r, 0] for nm in MOD_NAMES]) for r in range(2)])
                       for l in range(2)])
    gsm = dict(
        norm1_g=jnp.stack([gss[l]["norm1_g"][0] for l in range(2)]),
        ret_decay=jnp.stack([gss[l]["lam"] * jax.nn.sigmoid(-ret_decay[l]) for l in range(2)]),
        ret_gn_g=jnp.stack([gss[l]["ret_gn_g"][0] for l in range(2)]),
        conv_dw_w=jnp.stack([gss[l]["conv_dw_w"][:cfg.CK] for l in range(2)]),
        conv_dw_b=jnp.stack([gss[l]["conv_dw_b"][0] for l in range(2)]),
        conv_ln_g=jnp.stack([gss[l]["conv_ln_g"][0] for l in range(2)]),
        conv_ln_b=jnp.stack([gss[l]["conv_ln_b"][0] for l in range(2)]),
        na_rpb=jnp.stack([gss[l]["na_rpb"] for l in range(2)]),
        norm2_g=jnp.stack([gss[l]["norm2_g"][0] for l in range(2)]),
        ffn_dw_w=jnp.stack([gss[l]["ffn_dw_w"][:, :3].transpose(1, 0, 2).reshape(3, 2 * cfg.DFF) for l in range(2)]),
        ffn_dw_b=jnp.stack([gss[l]["ffn_dw_b"].reshape(-1) for l in range(2)]),
        final_g=dfg)
    snames = list(gsm)
    sshapes = [dmseg.shape] + [gsm[nm].shape for nm in snames]
    packed = _pack([dmseg] + [gsm[nm] for nm in snames])
    g_all = allgather8("ag_small_grads", packed).reshape(N_DEV, packed.shape[0], LANE)
    summed = sum_leading("sum_small_grads", g_all).reshape(-1)
    dm_sum, *gsum = _unpack(summed, sshapes)
    gfull = dict(zip(snames, gsum))
    ndm = int(np.prod(dmseg.shape))
    dm_all = g_all.reshape(N_DEV, -1)[:, :ndm].reshape(N_DEV, 2, 2, 6 * D)
    gfull["b_ada"] = sum_leading("sum_b_ada", dm_all.transpose(0, 2, 1, 3).reshape(2 * N_DEV, 2 * 6 * D // LANE, LANE)
                                 ).reshape(2, 6 * D)

    dm16 = jnp.concatenate([dm_all[:, :, 0].transpose(1, 0, 2), dm_sum[:, 1][:, None],
                            jnp.zeros((2, 16 - N_DEV - 1, 6 * D), F32)], axis=1)
    dm16 = lax.dynamic_slice(dm16, (0, 0, chip * NA), (2, 16, NA))
    gfull["w_ada"], ds16 = adaln_bwd("adaln_bwd", cond, w_ada, dm16)
    ds_all = allgather8("ag_dsilu", ds16[8:16]).reshape(N_DEV, 8, D)[0::2, 0:1]
    gfull["c_ctx"] = cctx_grad("cctx_grad", ds_all, c_ctx[None])[0]
    gfull["conv_dw_w"] = lax.dynamic_slice(gfull["conv_dw_w"], (0, 0, chip * ncw), (2, cfg.CK, ncw))
    gfull["ffn_dw_w"] = lax.dynamic_slice(gfull["ffn_dw_w"], (0, 0, chip * nfw), (2, 3, nfw))

    fin = [None] * len(BIG)
    for l in reversed(range(depth)):
        if l == depth - 1 and flying:
            landed = scatter_slices_wait(f"rs_scatter_{l}_wait", flying["part"], flying["lands"], *flying["sems"], gx)
        else:
            landed = scatter_slices(f"rs_scatter_{l}", *reduce_rows(l, gb))
        fin = [sum_rows_into(f"rs_sum_{nm}_{l}", p, c_arr, l, f) for nm, p, f in zip(BIG, landed, fin)]
    for nm, gfin in zip(BIG, share_rows("rs_share", fin)):
        gfull[nm] = gfin.reshape(W[nm].shape)

    delta, new_m, new_v = {}, {}, {}
    bigs = ("w_ada",) + BIG
    for nm in bigs:
        shp = W[nm].shape
        v2 = lambda a: a.reshape(-1, shp[-1])
        d_, m_, v_ = adamw(f"adamw_{nm}", v2(W[nm]), v2(gfull[nm]), v2(Mo[nm]), v2(Vo[nm]))
        delta[nm], new_m[nm], new_v[nm] = d_.reshape(shp), m_.reshape(shp), v_.reshape(shp)
    smalls = [nm for nm in order if nm not in bigs]
    shapes = [W[nm].shape for nm in smalls]
    d_, m_, v_ = adamw("adamw_small", _pack([W[nm] for nm in smalls]), _pack([gfull[nm] for nm in smalls]),
                       _pack([Mo[nm] for nm in smalls]), _pack([Vo[nm] for nm in smalls]))
    for nm, a, b, e in zip(smalls, _unpack(d_.reshape(-1), shapes), _unpack(m_.reshape(-1), shapes),
                           _unpack(v_.reshape(-1), shapes)):
        delta[nm], new_m[nm], new_v[nm] = a, b, e
    return (loss, gx[None], *[gfull[nm] for nm in order], *[delta[nm] for nm in order],
            *[new_m[nm] for nm in order], *[new_v[nm] for nm in order])
```

```python
import collections
import functools

import numpy as np
import jax
import jax.numpy as jnp
from jax import lax
from jax.experimental import pallas as pl
from jax.experimental.pallas import tpu as pltpu

F32 = jnp.float32
BF16 = jnp.bfloat16
EPS = 1e-6
ROPE_BASE = 10000.0
NEG = -1e30
LANE = 128
VMEM_LIMIT = 56 * 1024 * 1024

ADAM_LR, ADAM_B1, ADAM_B2, ADAM_EPS, ADAM_WD, ADAM_STEP = 0.001, 0.9, 0.999, 1e-08, 0.01, 10

Cfg = collections.namedtuple(
    "Cfg", "D T TC GW RH RDK RDV CW CK NH NDH NAR NAC DFF TB")


def make_cfg(D=2048, T=4096, TC=256, RH=4, CW=512, NH=4, DFF=5632):
    return Cfg(D=D, T=T, TC=TC, GW=64, RH=RH, RDK=128, RDV=256, CW=CW, CK=31, NH=NH, NDH=128,
               NAR=8, NAC=16, DFF=DFF, TB=256)


def _offsets(cfg):
    sizes = [cfg.RH * cfg.RDK, cfg.RH * cfg.RDK, cfg.RH * cfg.RDV, cfg.RH * cfg.RDV, cfg.CW, cfg.CW,
             cfg.NH * cfg.NDH, cfg.NH * cfg.NDH, cfg.NH * cfg.NDH]
    offs = [0]
    for s in sizes:
        offs.append(offs[-1] + s)
    return dict(zip(["lq", "lk", "lv", "lg", "la", "lb", "nq", "nk", "nv", "end"], offs))


def _pc(body, **kw):
    return pl.pallas_call(body, **kw)


def _cp(sem=None):
    return pltpu.CompilerParams(dimension_semantics=sem, vmem_limit_bytes=VMEM_LIMIT)


def _dot(a, b, ca, cb):
    return lax.dot_general(a, b, (((ca,), (cb,)), ((), ())), preferred_element_type=F32)


def dot_nn(a, b):
    return _dot(a, b, 1, 0)


def dot_nt(a, b):
    return _dot(a, b, 1, 1)


def dot_tn(a, b):
    return _dot(a, b, 0, 0)


def _sigmoid(x):
    return 1.0 / (1.0 + jnp.exp(-x))


def _silu(x):
    return x * _sigmoid(x)


def _dsilu(x):
    s = _sigmoid(x)
    return s * (1.0 + x * (1.0 - s))


def matmul(name, a, b, *, contract, grid, a_spec, b_spec, out_shape, out_spec, nk, into=None):
    dot = {"nn": dot_nn, "nt": dot_nt, "tn": dot_tn}[contract]
    direct = nk > 1 and out_shape.dtype == F32
    kax = len(grid) - 1

    def body(a_ref, b_ref, *rest):
        o_ref, *scr = rest[1:] if into is not None else rest
        p = dot(a_ref[...].astype(BF16), b_ref[...].astype(BF16))
        if nk == 1:
            o_ref[...] = p.astype(o_ref.dtype)
            return
        acc = o_ref if direct else scr[0]
        k = pl.program_id(kax)

        @pl.when(k == 0)
        def _():
            acc[...] = p

        @pl.when(k > 0)
        def _():
            acc[...] += p

        if not direct:
            @pl.when(k == nk - 1)
            def _():
                o_ref[...] = acc[...].astype(o_ref.dtype)

    scratch = []
    if nk > 1 and not direct:
        blk = [s for s in out_spec.block_shape if s is not None]
        scratch = [pltpu.VMEM(tuple(blk), F32)]
    sem = ("parallel",) * kax + (("arbitrary",) if nk > 1 else ("parallel",))
    in_specs, args, alias = [a_spec, b_spec], (a, b), {}
    if into is not None:
        in_specs, args, alias = in_specs + [pl.BlockSpec(memory_space=pl.ANY)], (a, b, into), {2: 0}
    return _pc(body, name=name, grid=grid, in_specs=in_specs, out_specs=out_spec, out_shape=out_shape,
               scratch_shapes=scratch, input_output_aliases=alias, compiler_params=_cp(sem))(*args)


_WG_ROWS = 1024


def wgrad(cfg, name, a, dc, a_spec, dc_spec, out_shape, out_spec, ntiles, into):
    T, TC = cfg.T, cfg.TC
    tml = min(_WG_ROWS, T)
    nl = T // tml

    def body(al_ref, ac_ref, dl_ref, dcx_ref, *rest):
        o_ref, acc = rest[-2:]
        m = pl.program_id(1)

        @pl.when(m == 0)
        def _():
            acc[...] = dot_tn(al_ref[...], dl_ref[...])

        @pl.when(jnp.logical_and(m > 0, m < nl))
        def _():
            acc[...] += dot_tn(al_ref[...], dl_ref[...])

        @pl.when(m == nl)
        def _():
            o_ref[...] = (acc[...] + dot_tn(ac_ref[...], dcx_ref[...])).astype(o_ref.dtype)

    lat = lambda m: jnp.minimum(m, nl - 1)
    ctx = lambda m: T // TC
    in_specs = [a_spec(tml, lat), a_spec(TC, ctx), dc_spec(tml, lat), dc_spec(TC, ctx)]
    args, alias = (a, a, dc, dc), {}
    if into is not None:
        in_specs, args, alias = in_specs + [pl.BlockSpec(memory_space=pl.ANY)], args + (into,), {4: 0}
    blk = tuple(s for s in out_spec.block_shape if s is not None)
    return _pc(body, name=name, grid=(ntiles, nl + 1), in_specs=in_specs, out_specs=out_spec, out_shape=out_shape,
               scratch_shapes=[pltpu.VMEM(blk, F32)], input_output_aliases=alias,
               compiler_params=_cp(("parallel", "arbitrary")))(*args)


def mm_rowsharded(name, a, w4, l, out_dtype, tn):
    L, K = a.shape
    nch, _, Kb, N = w4.shape
    tm = 256

    def body(a_ref, w_ref, o_ref):
        acc = dot_nn(a_ref[:, 0:Kb], w_ref[0])
        for j in range(1, nch):
            acc += dot_nn(a_ref[:, j * Kb:(j + 1) * Kb], w_ref[j])
        o_ref[...] = acc.astype(o_ref.dtype)

    return _pc(body, name=name, grid=(N // tn, L // tm),
               in_specs=[pl.BlockSpec((tm, K), lambda n, m: (m, 0)),
                         pl.BlockSpec((nch, None, Kb, tn), lambda n, m: (0, l, 0, n))],
               out_specs=pl.BlockSpec((tm, tn), lambda n, m: (m, n)),
               out_shape=jax.ShapeDtypeStruct((L, N), out_dtype),
               compiler_params=_cp(("parallel", "parallel")))(a, w4)


def _region(cfg):
    nlat = cfg.T // cfg.TB
    return lambda i: jnp.minimum(i // nlat, 1)


def norm_mod_fwd(cfg, name, x, ng, sc, sh):
    L, D = x.shape
    TB = cfg.TB
    reg = _region(cfg)

    def body(x_ref, ng_ref, sc_ref, sh_ref, h_ref):
        xv = x_ref[...]
        r = lax.rsqrt(jnp.mean(xv * xv, axis=-1, keepdims=True) + EPS)
        n = xv * r * ng_ref[...]
        h_ref[...] = (n * (1.0 + sc_ref[...]) + sh_ref[...]).astype(h_ref.dtype)

    row = pl.BlockSpec((TB, D), lambda i: (i, 0))
    vec = pl.BlockSpec((1, D), lambda i: (0, 0))
    rvec = pl.BlockSpec((None, 1, D), lambda i: (reg(i), 0, 0))
    return _pc(body, name=name, grid=(L // TB,), in_specs=[row, vec, rvec, rvec], out_specs=row,
               out_shape=jax.ShapeDtypeStruct((L, D), BF16), compiler_params=_cp(("parallel",)))(x, ng, sc, sh)


def norm_mod_bwd(cfg, name, dh, x, ng, sc, dx_in):
    L, D = x.shape
    TB = cfg.TB
    nlat = cfg.T // TB
    reg = _region(cfg)

    def body(dh_ref, x_ref, ng_ref, sc_ref, dxi_ref, dx_ref, dsc_ref, dsh_ref, dng_ref):
        i = pl.program_id(0)
        xv = x_ref[...]
        r = lax.rsqrt(jnp.mean(xv * xv, axis=-1, keepdims=True) + EPS)
        xh = xv * r
        g = ng_ref[...]
        n = xh * g
        dh = dh_ref[...]
        dn = dh * (1.0 + sc_ref[...])
        dxh = dn * g
        dx = r * (dxh - xh * jnp.mean(dxh * xh, axis=-1, keepdims=True))
        dx_ref[...] = dxi_ref[...] + dx
        s_sh = jnp.sum(dh, axis=0, keepdims=True)
        s_sc = jnp.sum(dh * n, axis=0, keepdims=True)
        s_ng = jnp.sum(dn * xh, axis=0, keepdims=True)
        first = jnp.logical_or(i == 0, i == nlat)

        @pl.when(first)
        def _():
            dsh_ref[...] = s_sh
            dsc_ref[...] = s_sc

        @pl.when(jnp.logical_not(first))
        def _():
            dsh_ref[...] += s_sh
            dsc_ref[...] += s_sc

        @pl.when(i == 0)
        def _():
            dng_ref[...] = s_ng

        @pl.when(i > 0)
        def _():
            dng_ref[...] += s_ng

    row = pl.BlockSpec((TB, D), lambda i: (i, 0))
    vec = pl.BlockSpec((1, D), lambda i: (0, 0))
    rvec = pl.BlockSpec((None, 1, D), lambda i: (reg(i), 0, 0))
    return _pc(body, name=name, grid=(L // TB,), in_specs=[row, row, vec, rvec, row],
               out_specs=[row, rvec, rvec, vec],
               out_shape=[jax.ShapeDtypeStruct((L, D), F32), jax.ShapeDtypeStruct((2, 1, D), F32),
                          jax.ShapeDtypeStruct((2, 1, D), F32), jax.ShapeDtypeStruct((1, D), F32)],
               compiler_params=_cp(("arbitrary",)))(dh, x, ng, sc, dx_in)


def resid_fwd(cfg, name, x, y, g):
    L, D = x.shape
    TB = cfg.TB
    reg = _region(cfg)

    def body(x_ref, y_ref, g_ref, o_ref):
        o_ref[...] = x_ref[...] + g_ref[...] * y_ref[...]

    row = pl.BlockSpec((TB, D), lambda i: (i, 0))
    rvec = pl.BlockSpec((None, 1, D), lambda i: (reg(i), 0, 0))
    return _pc(body, name=name, grid=(L // TB,), in_specs=[row, row, rvec], out_specs=row,
               out_shape=jax.ShapeDtypeStruct((L, D), F32), compiler_params=_cp(("parallel",)))(x, y, g)


def resid_bwd(cfg, name, dxo, y, g):
    L, D = y.shape
    TB = cfg.TB
    nlat = cfg.T // TB
    reg = _region(cfg)

    def body(d_ref, y_ref, g_ref, dy_ref, dg_ref):
        i = pl.program_id(0)
        d = d_ref[...]
        dy_ref[...] = (d * g_ref[...]).astype(dy_ref.dtype)
        s = jnp.sum(d * y_ref[...], axis=0, keepdims=True)
        first = jnp.logical_or(i == 0, i == nlat)

        @pl.when(first)
        def _():
            dg_ref[...] = s

        @pl.when(jnp.logical_not(first))
        def _():
            dg_ref[...] += s

    row = pl.BlockSpec((TB, D), lambda i: (i, 0))
    rvec = pl.BlockSpec((None, 1, D), lambda i: (reg(i), 0, 0))
    return _pc(body, name=name, grid=(L // TB,), in_specs=[row, row, rvec], out_specs=[row, rvec],
               out_shape=[jax.ShapeDtypeStruct((L, D), BF16), jax.ShapeDtypeStruct((2, 1, D), F32)],
               compiler_params=_cp(("arbitrary",)))(dxo, y, g)


def final_loss(cfg, name, x, fg, tgt):
    L, D = x.shape
    TB = cfg.TB
    nlat = cfg.T // TB

    def body(x_ref, fg_ref, t_ref, ls_ref, dx_ref, dg_ref):
        i = pl.program_id(0)

        @pl.when(i == 0)
        def _():
            ls_ref[...] = jnp.zeros_like(ls_ref)
            dg_ref[...] = jnp.zeros_like(dg_ref)

        @pl.when(i < nlat)
        def _():
            xv = x_ref[...]
            r = lax.rsqrt(jnp.mean(xv * xv, axis=-1, keepdims=True) + EPS)
            xh = xv * r
            g = fg_ref[...]
            e = xh * g - t_ref[...]
            ls_ref[...] += 0.5 * jnp.sum(e * e) / D
            dy = e / D
            dg_ref[...] += jnp.sum(dy * xh, axis=0, keepdims=True)
            dxh = dy * g
            dx_ref[...] = r * (dxh - xh * jnp.mean(dxh * xh, axis=-1, keepdims=True))

        @pl.when(i >= nlat)
        def _():
            dx_ref[...] = jnp.zeros_like(dx_ref)

    row = pl.BlockSpec((TB, D), lambda i: (i, 0))
    trow = pl.BlockSpec((TB, D), lambda i: (jnp.minimum(i, nlat - 1), 0))
    vec = pl.BlockSpec((1, D), lambda i: (0, 0))
    return _pc(body, name=name, grid=(L // TB,), in_specs=[row, vec, trow],
               out_specs=[pl.BlockSpec((1, LANE), lambda i: (0, 0)), row, vec],
               out_shape=[jax.ShapeDtypeStruct((1, LANE), F32), jax.ShapeDtypeStruct((L, D), F32),
                          jax.ShapeDtypeStruct((1, D), F32)],
               compiler_params=_cp(("arbitrary",)))(x, fg, tgt)


def rope_tables(cfg):
    half = cfg.RDK // 2
    nf = half // 2
    pos = np.arange(cfg.T)
    row = (pos // cfg.GW).astype(np.float32)
    col = (pos % cfg.GW).astype(np.float32)
    inv = jnp.asarray(ROPE_BASE, F32) ** (-jnp.arange(nf, dtype=F32) / nf)
    ar = jnp.asarray(row)[:, None] * inv[None, :]
    ac = jnp.asarray(col)[:, None] * inv[None, :]
    cos = jnp.concatenate([jnp.cos(ar), jnp.cos(ar), jnp.cos(ac), jnp.cos(ac)], axis=1)
    sin = jnp.concatenate([-jnp.sin(ar), jnp.sin(ar), -jnp.sin(ac), jnp.sin(ac)], axis=1)
    cos = jnp.concatenate([cos, jnp.ones((cfg.TC, cfg.RDK), F32)], axis=0)
    sin = jnp.concatenate([sin, jnp.zeros((cfg.TC, cfg.RDK), F32)], axis=0)
    return cos, sin


def _swap32(t):
    lane = lax.broadcasted_iota(jnp.int32, t.shape, 1)
    return jnp.where((lane % 64) < 32, pltpu.roll(t, 96, 1), pltpu.roll(t, 32, 1))


def rope_fwd(cfg, name, P, cos, sin):
    L = P.shape[0]
    TB = cfg.TB
    off = _offsets(cfg)
    cq, ck = off["lq"] // LANE, off["lk"] // LANE
    scale = cfg.RDK ** -0.5

    def body(q_ref, k_ref, c_ref, s_ref, qo_ref, ko_ref):
        c = c_ref[...]
        s = s_ref[...]
        q = q_ref[...]
        k = k_ref[...]
        qo_ref[...] = (q * c + _swap32(q) * s) * scale
        ko_ref[...] = k * c + _swap32(k) * s

    tab = pl.BlockSpec((TB, LANE), lambda i, h: (i, 0))
    out = pl.BlockSpec((TB, LANE), lambda i, h: (i, h))
    shp = jax.ShapeDtypeStruct((L, cfg.RH * cfg.RDK), F32)
    return _pc(body, name=name, grid=(L // TB, cfg.RH),
               in_specs=[pl.BlockSpec((TB, LANE), lambda i, h: (i, cq + h)),
                         pl.BlockSpec((TB, LANE), lambda i, h: (i, ck + h)), tab, tab],
               out_specs=[out, out], out_shape=[shp, shp],
               compiler_params=_cp(("parallel", "parallel")))(P, P, cos, sin)


def rope_bwd(cfg, name, dq2, dk2, cos, sin):
    L, W = dq2[0].shape
    TB = cfg.TB
    scale = cfg.RDK ** -0.5

    def body(dqf_ref, dqb_ref, dkf_ref, dkb_ref, c_ref, s_ref, qo_ref, ko_ref):
        c = c_ref[...]
        s = s_ref[...]
        dq = dqf_ref[...] + dqb_ref[...]
        dk = dkf_ref[...] + dkb_ref[...]
        qo_ref[...] = ((dq * c - _swap32(dq) * s) * scale).astype(qo_ref.dtype)
        ko_ref[...] = (dk * c - _swap32(dk) * s).astype(ko_ref.dtype)

    tab = pl.BlockSpec((TB, LANE), lambda i, h: (i, 0))
    blk = pl.BlockSpec((TB, LANE), lambda i, h: (i, h))
    shp = jax.ShapeDtypeStruct((L, W), BF16)
    return _pc(body, name=name, grid=(L // TB, cfg.RH), in_specs=[blk, blk, blk, blk, tab, tab],
               out_specs=[blk, blk], out_shape=[shp, shp],
               compiler_params=_cp(("parallel", "parallel")))(*dq2, *dk2, cos, sin)


def _ret_chunk_map(cfg):
    C = cfg.RDK
    n = (cfg.T + cfg.TC) // C
    nlat, nctx = cfg.T // C, cfg.TC // C

    def chunk(d, s):
        if d == 0:
            return jnp.where(s < nctx, nlat + s, s - nctx)
        return n - 1 - s

    return n, chunk


def _ret_decay_terms(d, lam, C):
    ii = lax.broadcasted_iota(jnp.int32, (C, C), 0)
    jj = lax.broadcasted_iota(jnp.int32, (C, C), 1)
    diff = (ii - jj if d == 0 else jj - ii).astype(F32)
    dpos = jnp.maximum(diff, 0.0)
    Dm = jnp.where(diff >= 0, jnp.exp(dpos * lam), 0.0)
    ic = lax.broadcasted_iota(jnp.int32, (C, 1), 0).astype(F32)
    cxi = ic + 1.0 if d == 0 else C - ic
    cze = C - 1.0 - ic if d == 0 else ic
    xi = jnp.exp(cxi * lam)
    ze = jnp.exp(cze * lam)
    g = jnp.exp(jnp.full((1, 1), C, F32) * lam)
    return dpos, Dm, cxi, cze, xi, ze, g


def retention_fwd(cfg, name, qr, kr, P, lam):
    L = P.shape[0]
    C, DV, RH = cfg.RDK, cfg.RDV, cfg.RH
    n, chunk = _ret_chunk_map(cfg)

    def body(lam_ref, qf_ref, qb_ref, kf_ref, kb_ref, vf_ref, vb_ref, of_ref, ob_ref, st_ref, S):
        s = pl.program_id(0)

        @pl.when(s == 0)
        def _():
            S[...] = jnp.zeros_like(S)

        for d, (q_ref, k_ref, v_ref, o_ref) in enumerate(((qf_ref, kf_ref, vf_ref, of_ref),
                                                          (qb_ref, kb_ref, vb_ref, ob_ref))):
            for h in range(RH):
                _, Dm, _, _, xi, ze, g = _ret_decay_terms(d, lam_ref[d, h], C)
                k = k_ref[:, h * C:(h + 1) * C]
                qb = q_ref[:, h * C:(h + 1) * C].astype(BF16)
                kb = k.astype(BF16)
                vb = v_ref[:, h * DV:(h + 1) * DV].astype(BF16)
                Sv = S[d, h]
                st_ref[d, h] = Sv
                A = dot_nt(qb, kb) * Dm
                o_ref[:, h * DV:(h + 1) * DV] = dot_nn(A.astype(BF16), vb) + dot_nn(qb, Sv.astype(BF16)) * xi
                S[d, h] = Sv * g + dot_tn((k * ze).astype(BF16), vb)

    def spec(w, col, d):
        return pl.BlockSpec((C, w), lambda s: (chunk(d, s), col))

    W, WV = RH * C, RH * DV
    return _pc(body, name=name, grid=(n,),
               in_specs=[pl.BlockSpec(memory_space=pltpu.SMEM), spec(W, 0, 0), spec(W, 0, 1), spec(W, 0, 0),
                         spec(W, 0, 1), spec(WV, 1, 0), spec(WV, 1, 1)],
               out_specs=[spec(WV, 0, 0), spec(WV, 0, 1),
                          pl.BlockSpec((2, RH, None, C, DV), lambda s: (0, 0, s, 0, 0))],
               out_shape=[jax.ShapeDtypeStruct((L, WV), F32), jax.ShapeDtypeStruct((L, WV), F32),
                          jax.ShapeDtypeStruct((2, RH, n, C, DV), F32)],
               scratch_shapes=[pltpu.VMEM((2, RH, C, DV), F32)],
               compiler_params=_cp(("arbitrary",)))(lam, qr, qr, kr, kr, P, P)


def retention_bwd(cfg, name, qr, kr, P, lam, st, do):
    L = P.shape[0]
    C, DV, RH = cfg.RDK, cfg.RDV, cfg.RH
    n, chunk = _ret_chunk_map(cfg)

    def body(lam_ref, qf_ref, qb_ref, kf_ref, kb_ref, vf_ref, vb_ref, st_ref, dof_ref, dob_ref,
             dqf_ref, dqb_ref, dkf_ref, dkb_ref, dvf_ref, dvb_ref, dl_ref, dS):
        si = pl.program_id(0)

        @pl.when(si == 0)
        def _():
            dS[...] = jnp.zeros_like(dS)
            dl_ref[...] = jnp.zeros_like(dl_ref)

        dirs = ((qf_ref, kf_ref, vf_ref, dof_ref, dqf_ref, dkf_ref, dvf_ref),
                (qb_ref, kb_ref, vb_ref, dob_ref, dqb_ref, dkb_ref, dvb_ref))
        for d, (q_ref, k_ref, v_ref, do_ref, dq_ref, dk_ref, dv_ref) in enumerate(dirs):
            for h in range(RH):
                dpos, Dm, cxi, cze, xi, ze, g = _ret_decay_terms(d, lam_ref[d, h], C)
                hk = slice(h * C, (h + 1) * C)
                hv = slice(h * DV, (h + 1) * DV)
                k = k_ref[:, hk]
                do = do_ref[:, hv]
                qb = q_ref[:, hk].astype(BF16)
                kb = k.astype(BF16)
                vb = v_ref[:, hv].astype(BF16)
                dob = do.astype(BF16)
                Sn = st_ref[d, h]
                Snb = Sn.astype(BF16)
                dSn = dS[d, h]
                dSb = dSn.astype(BF16)
                A = dot_nt(qb, kb) * Dm
                dA = dot_nt(dob, vb)
                dQK = (dA * Dm).astype(BF16)
                kzb = (k * ze).astype(BF16)
                dv_ref[:, hv] = dot_tn(A.astype(BF16), dob) + dot_nn(kzb, dSb)
                dkz = dot_nt(vb, dSb)
                doxb = (do * xi).astype(BF16)
                dq_ref[:, hk] = dot_nn(dQK, kb) + dot_nt(doxb, Snb)
                dk_ref[:, hk] = dot_tn(dQK, qb) + dkz * ze
                QS = dot_nn(qb, Snb)
                t = (jnp.sum(dA * A * dpos) + jnp.sum(do * QS * (cxi * xi)) + jnp.sum(k * dkz * (cze * ze)))
                t4 = jnp.sum(dSn * Sn, axis=0, keepdims=True)
                t4 = jnp.sum(t4 * (g * C), axis=1, keepdims=True)
                dl_ref[d, h] += t + t4
                dS[d, h] = g * dSn + dot_tn(qb, doxb)

    def spec(w, col, d):
        return pl.BlockSpec((C, w), lambda si: (chunk(d, n - 1 - si), col))

    W, WV = RH * C, RH * DV
    return _pc(body, name=name, grid=(n,),
               in_specs=[pl.BlockSpec(memory_space=pltpu.SMEM), spec(W, 0, 0), spec(W, 0, 1), spec(W, 0, 0),
                         spec(W, 0, 1), spec(WV, 1, 0), spec(WV, 1, 1),
                         pl.BlockSpec((2, RH, None, C, DV), lambda si: (0, 0, n - 1 - si, 0, 0)),
                         spec(WV, 0, 0), spec(WV, 0, 1)],
               out_specs=[spec(W, 0, 0), spec(W, 0, 1), spec(W, 0, 0), spec(W, 0, 1), spec(WV, 0, 0), spec(WV, 0, 1),
                          pl.BlockSpec((2, RH, 8, LANE), lambda si: (0, 0, 0, 0))],
               out_shape=[jax.ShapeDtypeStruct((L, W), F32)] * 4 + [jax.ShapeDtypeStruct((L, WV), F32)] * 2
               + [jax.ShapeDtypeStruct((2, RH, 8, LANE), F32)],
               scratch_shapes=[pltpu.VMEM((2, RH, C, DV), F32)],
               compiler_params=_cp(("arbitrary",)))(lam, qr, qr, kr, kr, P, P, st, do, do)


def add_cast(cfg, name, a, b):
    L, W = a.shape
    TB = cfg.TB

    def body(a_ref, b_ref, o_ref):
        o_ref[...] = (a_ref[...] + b_ref[...]).astype(o_ref.dtype)

    spec = pl.BlockSpec((TB, W), lambda i: (i, 0))
    return _pc(body, name=name, grid=(L // TB,), in_specs=[spec, spec], out_specs=spec,
               out_shape=jax.ShapeDtypeStruct((L, W), BF16), compiler_params=_cp(("parallel",)))(a, b)


def ggn_fwd(cfg, name, o2, P, gn_g):
    L = P.shape[0]
    TB, DV, RH = cfg.TB, cfg.RDV, cfg.RH
    gc0 = _offsets(cfg)["lg"] // DV

    def body(of_ref, ob_ref, gate_ref, g_ref, out_ref):
        o = of_ref[...] + ob_ref[...]
        mu = jnp.mean(o, axis=-1, keepdims=True)
        xc = o - mu
        var = jnp.mean(xc * xc, axis=-1, keepdims=True)
        y = xc * lax.rsqrt(var + EPS) * g_ref[...]
        out_ref[...] = (y * _silu(gate_ref[...])).astype(out_ref.dtype)

    blk = pl.BlockSpec((TB, DV), lambda i, h: (i, h))
    return _pc(body, name=name, grid=(L // TB, RH),
               in_specs=[blk, blk, pl.BlockSpec((TB, DV), lambda i, h: (i, gc0 + h)),
                         pl.BlockSpec((1, DV), lambda i, h: (0, h))],
               out_specs=blk, out_shape=jax.ShapeDtypeStruct((L, RH * DV), BF16),
               compiler_params=_cp(("parallel", "parallel")))(*o2, P, gn_g)


def ggn_bwd(cfg, name, dout, o2, P, gn_g, col0):
    L = P.shape[0]
    TB, DV, RH = cfg.TB, cfg.RDV, cfg.RH
    gc0 = _offsets(cfg)["lg"] // DV

    def body(d_ref, of_ref, ob_ref, gate_ref, g_ref, do_ref, dgate_ref, dg_ref):
        i = pl.program_id(1)
        o = of_ref[...] + ob_ref[...]
        mu = jnp.mean(o, axis=-1, keepdims=True)
        xc = o - mu
        var = jnp.mean(xc * xc, axis=-1, keepdims=True)
        r = lax.rsqrt(var + EPS)
        y = xc * r
        g = g_ref[...]
        gate = gate_ref[...]
        d = d_ref[...]
        dgate_ref[...] = (d * (y * g) * _dsilu(gate)).astype(dgate_ref.dtype)
        dyg = d * _silu(gate)
        s = jnp.sum(dyg * y, axis=0, keepdims=True)

        @pl.when(i == 0)
        def _():
            dg_ref[...] = s

        @pl.when(i > 0)
        def _():
            dg_ref[...] += s

        dy = dyg * g
        do_ref[...] = r * (dy - jnp.mean(dy, axis=-1, keepdims=True)
                           - y * jnp.mean(dy * y, axis=-1, keepdims=True))

    blk = pl.BlockSpec((TB, DV), lambda h, i: (i, h))
    return _pc(body, name=name, grid=(RH, L // TB),
               in_specs=[pl.BlockSpec((TB, DV), lambda h, i: (i, col0 + h)), blk, blk,
                         pl.BlockSpec((TB, DV), lambda h, i: (i, gc0 + h)),
                         pl.BlockSpec((1, DV), lambda h, i: (0, h))],
               out_specs=[blk, blk, pl.BlockSpec((1, DV), lambda h, i: (0, h))],
               out_shape=[jax.ShapeDtypeStruct((L, RH * DV), F32), jax.ShapeDtypeStruct((L, RH * DV), BF16),
                          jax.ShapeDtypeStruct((1, RH * DV), F32)],
               compiler_params=_cp(("parallel", "arbitrary")))(dout, *o2, P, gn_g)


def cast_cols(cfg, name, src, col0, ncols, width):
    L = src.shape[0]
    TB = cfg.TB

    def body(s_ref, o_ref):
        o_ref[...] = s_ref[...].astype(o_ref.dtype)

    spec = pl.BlockSpec((TB, width), lambda i, j: (i, col0 + j))
    return _pc(body, name=name, grid=(L // TB, ncols), in_specs=[spec],
               out_specs=pl.BlockSpec((TB, width), lambda i, j: (i, j)),
               out_shape=jax.ShapeDtypeStruct((L, ncols * width), BF16),
               compiler_params=_cp(("parallel", "parallel")))(src)


_CPAD = 16


def _conv_windows(cfg):
    T, TC, TB = cfg.T, cfg.TC, cfg.TB
    assert TC % TB == 0 and T % TB == 0 and cfg.CK // 2 < _CPAD
    return T // TB, [(T + j * TB, T + _CPAD + j * TB) for j in range(TC // TB)]


def _fill_padded(cfg, pb, get):
    T, TC, TB = cfg.T, cfg.TC, cfg.TB
    z = jnp.zeros((_CPAD, LANE), F32)
    pb[0:_CPAD, :] = z
    pb[_CPAD + T:2 * _CPAD + T, :] = z
    pb[2 * _CPAD + T + TC:3 * _CPAD + T + TC, :] = z

    def fill(i, c):
        r0 = pl.multiple_of(i * TB, TB)
        pb[pl.ds(r0 + _CPAD, TB), :] = get(r0)
        return c

    lax.fori_loop(0, T // TB, fill, 0)
    for j in range(TC // TB):
        pb[2 * _CPAD + T + j * TB:2 * _CPAD + T + (j + 1) * TB, :] = get(T + j * TB)


def _taps(win, TB):
    W = TB + 2 * _CPAD
    return lambda k: pltpu.roll(win, W - (k + 1), 0)[0:TB, :]


def glu_dwconv_fwd(cfg, name, P, w, b):
    L = P.shape[0]
    T, TC, TB, K = cfg.T, cfg.TC, cfg.TB, cfg.CK
    off = _offsets(cfg)
    ca, cb = off["la"] // LANE, off["lb"] // LANE
    nlat, ctx_tiles = _conv_windows(cfg)
    PBL = 3 * _CPAD + T + TC

    def body(a_ref, b_ref, w_ref, bias_ref, y_ref, pb):
        _fill_padded(cfg, pb, lambda r0: a_ref[pl.ds(r0, TB), :] * _sigmoid(b_ref[pl.ds(r0, TB), :]))
        wv = w_ref[...]
        bias = bias_ref[...]

        def tile(win):
            tap = _taps(win, TB)
            acc = jnp.zeros((TB, LANE), F32) + bias
            for k in range(K):
                acc = acc + wv[k:k + 1, :] * tap(k)
            return acc

        def lat(i, c):
            r0 = pl.multiple_of(i * TB, TB)
            y_ref[pl.ds(r0, TB), :] = tile(pb[pl.ds(r0, TB + 2 * _CPAD), :])
            return c

        lax.fori_loop(0, nlat, lat, 0)
        for r0, w0 in ctx_tiles:
            y_ref[r0:r0 + TB, :] = tile(pb[w0:w0 + TB + 2 * _CPAD, :])

    return _pc(body, name=name, grid=(cfg.CW // LANE,),
               in_specs=[pl.BlockSpec((L, LANE), lambda j: (0, ca + j)),
                         pl.BlockSpec((L, LANE), lambda j: (0, cb + j)),
                         pl.BlockSpec((32, LANE), lambda j: (0, j)),
                         pl.BlockSpec((1, LANE), lambda j: (0, j))],
               out_specs=pl.BlockSpec((L, LANE), lambda j: (0, j)),
               out_shape=jax.ShapeDtypeStruct((L, cfg.CW), F32),
               scratch_shapes=[pltpu.VMEM((PBL, LANE), F32)],
               compiler_params=_cp(("parallel",)))(P, P, w, b)


def glu_dwconv_bwd(cfg, name, P, w, dy):
    L = P.shape[0]
    T, TC, TB, K = cfg.T, cfg.TC, cfg.TB, cfg.CK
    off = _offsets(cfg)
    ca, cb = off["la"] // LANE, off["lb"] // LANE
    nlat, ctx_tiles = _conv_windows(cfg)
    PBL = 3 * _CPAD + T + TC

    def body(a_ref, b_ref, w_ref, dy_ref, da_ref, db_ref, dw_ref, dbias_ref, pbu, pbd):
        _fill_padded(cfg, pbu, lambda r0: a_ref[pl.ds(r0, TB), :] * _sigmoid(b_ref[pl.ds(r0, TB), :]))
        _fill_padded(cfg, pbd, lambda r0: dy_ref[pl.ds(r0, TB), :])
        wv = w_ref[...]
        dw_ref[...] = jnp.zeros_like(dw_ref)
        dbias_ref[...] = jnp.zeros_like(dbias_ref)

        def tile(r0, winu, wind):
            tapu = _taps(winu, TB)
            tapd = _taps(wind, TB)
            dyt = dy_ref[pl.ds(r0, TB), :]
            du = jnp.zeros((TB, LANE), F32)
            for k in range(K):
                du = du + wv[k:k + 1, :] * tapd(K - 1 - k)
                dw_ref[k:k + 1, :] += jnp.sum(dyt * tapu(k), axis=0, keepdims=True)
            dbias_ref[...] += jnp.sum(dyt, axis=0, keepdims=True)
            a = a_ref[pl.ds(r0, TB), :]
            sg = _sigmoid(b_ref[pl.ds(r0, TB), :])
            da_ref[pl.ds(r0, TB), :] = (du * sg).astype(da_ref.dtype)
            db_ref[pl.ds(r0, TB), :] = (du * a * sg * (1.0 - sg)).astype(db_ref.dtype)

        def lat(i, c):
            r0 = pl.multiple_of(i * TB, TB)
            tile(r0, pbu[pl.ds(r0, TB + 2 * _CPAD), :], pbd[pl.ds(r0, TB + 2 * _CPAD), :])
            return c

        lax.fori_loop(0, nlat, lat, 0)
        for r0, w0 in ctx_tiles:
            tile(r0, pbu[w0:w0 + TB + 2 * _CPAD, :], pbd[w0:w0 + TB + 2 * _CPAD, :])

    col = pl.BlockSpec((L, LANE), lambda j: (0, j))
    return _pc(body, name=name, grid=(cfg.CW // LANE,),
               in_specs=[pl.BlockSpec((L, LANE), lambda j: (0, ca + j)),
                         pl.BlockSpec((L, LANE), lambda j: (0, cb + j)),
                         pl.BlockSpec((32, LANE), lambda j: (0, j)), col],
               out_specs=[col, col, pl.BlockSpec((32, LANE), lambda j: (0, j)),
                          pl.BlockSpec((1, LANE), lambda j: (0, j))],
               out_shape=[jax.ShapeDtypeStruct((L, cfg.CW), BF16), jax.ShapeDtypeStruct((L, cfg.CW), BF16),
                          jax.ShapeDtypeStruct((32, cfg.CW), F32), jax.ShapeDtypeStruct((1, cfg.CW), F32)],
               scratch_shapes=[pltpu.VMEM((PBL, LANE), F32), pltpu.VMEM((PBL, LANE), F32)],
               compiler_params=_cp(("parallel",)))(P, P, w, dy)


def ln_silu_fwd(cfg, name, y, g, b):
    L, W = y.shape
    TB = cfg.TB

    def body(y_ref, g_ref, b_ref, o_ref):
        yv = y_ref[...]
        mu = jnp.mean(yv, axis=-1, keepdims=True)
        xc = yv - mu
        var = jnp.mean(xc * xc, axis=-1, keepdims=True)
        z = xc * lax.rsqrt(var + EPS) * g_ref[...] + b_ref[...]
        o_ref[...] = _silu(z).astype(o_ref.dtype)

    row = pl.BlockSpec((TB, W), lambda i: (i, 0))
    vec = pl.BlockSpec((1, W), lambda i: (0, 0))
    return _pc(body, name=name, grid=(L // TB,), in_specs=[row, vec, vec], out_specs=row,
               out_shape=jax.ShapeDtypeStruct((L, W), BF16), compiler_params=_cp(("parallel",)))(y, g, b)


def ln_silu_bwd(cfg, name, dact, y, g, b):
    L, W = y.shape
    TB = cfg.TB

    def body(d_ref, y_ref, g_ref, b_ref, dy_ref, dg_ref, db_ref):
        i = pl.program_id(0)
        yv = y_ref[...]
        mu = jnp.mean(yv, axis=-1, keepdims=True)
        xc = yv - mu
        var = jnp.mean(xc * xc, axis=-1, keepdims=True)
        r = lax.rsqrt(var + EPS)
        yh = xc * r
        g = g_ref[...]
        z = yh * g + b_ref[...]
        dz = d_ref[...] * _dsilu(z)
        sg = jnp.sum(dz * yh, axis=0, keepdims=True)
        sb = jnp.sum(dz, axis=0, keepdims=True)

        @pl.when(i == 0)
        def _():
            dg_ref[...] = sg
            db_ref[...] = sb

        @pl.when(i > 0)
        def _():
            dg_ref[...] += sg
            db_ref[...] += sb

        dh = dz * g
        dy_ref[...] = r * (dh - jnp.mean(dh, axis=-1, keepdims=True)
                           - yh * jnp.mean(dh * yh, axis=-1, keepdims=True))

    row = pl.BlockSpec((TB, W), lambda i: (i, 0))
    vec = pl.BlockSpec((1, W), lambda i: (0, 0))
    return _pc(body, name=name, grid=(L // TB,), in_specs=[row, row, vec, vec], out_specs=[row, vec, vec],
               out_shape=[jax.ShapeDtypeStruct((L, W), F32), jax.ShapeDtypeStruct((1, W), F32),
                          jax.ShapeDtypeStruct((1, W), F32)],
               compiler_params=_cp(("arbitrary",)))(dact, y, g, b)


def _na_geometry(cfg):
    R = cfg.T // cfg.GW
    nb = R // cfg.NAR
    assert nb >= 3 and cfg.GW == 64 and cfg.NAR == 8
    ks = [int(np.clip(8 * b - 4, 0, R - 16)) for b in range(nb)]
    return R, nb, ks


_NTAB = 18


def _split3(x):
    hi = x.astype(BF16)
    r = x - hi.astype(F32)
    mid = r.astype(BF16)
    lo = (r - mid.astype(F32)).astype(BF16)
    return hi, mid, lo


def _na_col_onehot(cfg):
    GW, NAC = cfg.GW, cfg.NAC
    qc = np.arange(GW)[:, None]
    kc = np.arange(GW)[None, :]
    cs = np.clip(qc - NAC // 2, 0, GW - NAC)
    vcol = (kc >= cs) & (kc < cs + NAC)
    dd = np.clip(kc - qc + NAC - 1, 0, 2 * NAC - 2)
    oh = (np.arange(LANE)[:, None, None] == dd[None]).astype(np.float32)
    z = np.zeros_like(oh)
    oda = np.concatenate([oh, z], axis=2).reshape(LANE, GW * LANE)
    odb = np.concatenate([z, oh], axis=2).reshape(LANE, GW * LANE)
    cm = np.where(np.concatenate([vcol, vcol], axis=1), 0.0, NEG).astype(np.float32).reshape(1, GW * LANE)
    return oda, odb, cm


def na_tables(cfg, name, rpb):
    NH, GW = cfg.NH, cfg.GW
    na = rpb.shape[1]
    oda, odb, cm = _na_col_onehot(cfg)
    rp = jnp.zeros((NH, _NTAB + 1, LANE), F32).at[:, 1:1 + na, :rpb.shape[2]].set(rpb.astype(F32))
    r0 = rp[:, :_NTAB].reshape(NH * _NTAB, LANE)
    r1 = rp[:, 1:].reshape(NH * _NTAB, LANE)
    a = np.arange(_NTAB) - 1
    rm0 = np.where((a >= 0) & (a < na), 0.0, NEG).astype(np.float32)
    rm1 = np.where((a + 1 >= 0) & (a + 1 < na), 0.0, NEG).astype(np.float32)
    half = (np.arange(GW * LANE) % LANE >= GW)[None, :]
    rmask = np.where(half, np.tile(rm1, NH)[:, None], np.tile(rm0, NH)[:, None]).astype(np.float32)
    tn = 2048
    rows = NH * _NTAB

    def body(r0_ref, r1_ref, a_ref, b_ref, cm_ref, rm_ref, o_ref):
        acc = cm_ref[...] + rm_ref[...]
        for t in _split3(r0_ref[...]):
            acc = acc + dot_nn(t, a_ref[...])
        for t in _split3(r1_ref[...]):
            acc = acc + dot_nn(t, b_ref[...])
        o_ref[...] = acc

    rs = pl.BlockSpec((rows, LANE), lambda n: (0, 0))
    out = _pc(body, name=name, grid=(GW * LANE // tn,),
              in_specs=[rs, rs, pl.BlockSpec((LANE, tn), lambda n: (0, n)), pl.BlockSpec((LANE, tn), lambda n: (0, n)),
                        pl.BlockSpec((1, tn), lambda n: (0, n)), pl.BlockSpec((rows, tn), lambda n: (0, n))],
              out_specs=pl.BlockSpec((rows, tn), lambda n: (0, n)),
              out_shape=jax.ShapeDtypeStruct((rows, GW * LANE), F32),
              compiler_params=_cp(("parallel",)))(r0, r1, jnp.asarray(oda, BF16), jnp.asarray(odb, BF16),
                                                  jnp.asarray(cm), jnp.asarray(rmask))
    return out.reshape(NH, _NTAB, GW, LANE)


def _na_tiles(cfg, b):
    R, nb, _ = _na_geometry(cfg)
    NAR = cfg.NAR
    ksb = jnp.clip(8 * b - 4, 0, R - 16)
    for i in range(8):
        qr = 8 * b + i
        ws = jnp.clip(qr - NAR // 2, 0, R - NAR)
        for J in range(8):
            kr0 = ksb + 2 * J
            row = jnp.clip(kr0 - qr + NAR - 1, -1, _NTAB - 2) + 1
            v0 = jnp.logical_and(kr0 >= ws, kr0 < ws + NAR)
            v1 = jnp.logical_and(kr0 + 1 >= ws, kr0 + 1 < ws + NAR)
            yield i, J, row, v0, v1


def _na_fill_bias(cfg, tab_ref, bias, b):
    GW = cfg.GW
    first = lax.broadcasted_iota(jnp.int32, (GW, LANE), 1) < GW
    for i, J, row, v0, v1 in _na_tiles(cfg, b):
        ok = jnp.where(first, v0.astype(jnp.int32), v1.astype(jnp.int32))
        bias[i * GW:(i + 1) * GW, J * LANE:(J + 1) * LANE] = jnp.where(ok > 0, tab_ref[row], NEG)


def _na_specs(cfg):
    R, nb, ks = _na_geometry(cfg)
    off = _offsets(cfg)
    TQ = 8 * cfg.GW
    KP = 4 * cfg.GW
    ks4 = [k // 4 for k in ks]
    lat_blocks = cfg.T // KP

    def ks4_of(b):
        return jnp.clip(2 * b - 1, 0, R // 4 - 4)

    assert all(int(np.clip(2 * b - 1, 0, R // 4 - 4)) == ks4[b] for b in range(nb))
    assert cfg.TC == KP

    def col(nm):
        c0 = off[nm] // LANE
        q = pl.BlockSpec((TQ, LANE), lambda h, b: (b, c0 + h))
        parts = [pl.BlockSpec((KP, LANE), functools.partial(lambda h, b, t: (ks4_of(b) + t, c0 + h), t=t))
                 for t in range(4)]
        ctx = pl.BlockSpec((KP, LANE), lambda h, b: (lat_blocks, c0 + h))
        return q, parts, ctx

    return nb, TQ, KP, ks4_of, col


def na_fwd(cfg, name, P, tab):
    nb, TQ, KP, ks4_of, col = _na_specs(cfg)
    NH = cfg.NH
    scale = cfg.NDH ** -0.5
    qs, _, _ = col("nq")
    _, kparts, kctx = col("nk")
    _, vparts, vctx = col("nv")

    def body(q_ref, k0, k1, k2, k3, kc_ref, v0, v1, v2, v3, vc_ref, tab_ref, o_ref, lse_ref, bias_ref):
        _na_fill_bias(cfg, tab_ref, bias_ref, pl.program_id(1))
        q = (q_ref[...] * scale).astype(BF16)
        kl = jnp.concatenate([k0[...], k1[...], k2[...], k3[...]], axis=0).astype(BF16)
        vl = jnp.concatenate([v0[...], v1[...], v2[...], v3[...]], axis=0).astype(BF16)
        kc = kc_ref[...].astype(BF16)
        vc = vc_ref[...].astype(BF16)
        sl = dot_nt(q, kl) + bias_ref[...]
        sc = dot_nt(q, kc)
        m = jnp.maximum(jnp.max(sl, axis=-1, keepdims=True), jnp.max(sc, axis=-1, keepdims=True))
        pl_ = jnp.exp(sl - m)
        pc = jnp.exp(sc - m)
        den = jnp.sum(pl_, axis=-1, keepdims=True) + jnp.sum(pc, axis=-1, keepdims=True)
        o = dot_nn(pl_.astype(BF16), vl) + dot_nn(pc.astype(BF16), vc)
        o_ref[...] = o / den
        lse_ref[...] = m + jnp.log(den)

    return _pc(body, name=name, grid=(NH, nb),
               in_specs=[qs, *kparts, kctx, *vparts, vctx,
                         pl.BlockSpec((None, _NTAB, cfg.GW, LANE), lambda h, b: (h, 0, 0, 0))],
               out_specs=[pl.BlockSpec((TQ, LANE), lambda h, b: (b, h)),
                          pl.BlockSpec((None, TQ, 1), lambda h, b: (h, b, 0))],
               out_shape=[jax.ShapeDtypeStruct((cfg.T, NH * LANE), F32),
                          jax.ShapeDtypeStruct((NH, cfg.T, 1), F32)],
               scratch_shapes=[pltpu.VMEM((TQ, 4 * KP), F32)],
               compiler_params=_cp(("parallel", "parallel")))(P, *([P] * 5), *([P] * 5), tab)


def na_bwd(cfg, name, P, tab, o, lse, dmix, dcol0):
    nb, TQ, KP, ks4_of, col = _na_specs(cfg)
    NH, GW = cfg.NH, cfg.GW
    L = P.shape[0]
    scale = cfg.NDH ** -0.5
    qs, _, _ = col("nq")
    _, kparts, kctx = col("nk")
    _, vparts, vctx = col("nv")

    def body(q_ref, k0, k1, k2, k3, kc_ref, v0, v1, v2, v3, vc_ref, tab_ref, o_ref, lse_ref, do_ref,
             dq_ref, dk_ref, dv_ref, dtab_ref, bias_ref):
        b = pl.program_id(1)

        @pl.when(b == 0)
        def _():
            dk_ref[...] = jnp.zeros_like(dk_ref)
            dv_ref[...] = jnp.zeros_like(dv_ref)
            dtab_ref[...] = jnp.zeros_like(dtab_ref)

        _na_fill_bias(cfg, tab_ref, bias_ref, b)

        q = (q_ref[...] * scale).astype(BF16)
        kl = jnp.concatenate([k0[...], k1[...], k2[...], k3[...]], axis=0).astype(BF16)
        vl = jnp.concatenate([v0[...], v1[...], v2[...], v3[...]], axis=0).astype(BF16)
        kc = kc_ref[...].astype(BF16)
        vc = vc_ref[...].astype(BF16)
        lse = lse_ref[...]
        do = do_ref[...]
        dob = do.astype(BF16)
        p_l = jnp.exp(dot_nt(q, kl) + bias_ref[...] - lse)
        p_c = jnp.exp(dot_nt(q, kc) - lse)
        delta = jnp.sum(do * o_ref[...], axis=-1, keepdims=True)
        ds_l = p_l * (dot_nt(dob, vl) - delta)
        ds_c = p_c * (dot_nt(dob, vc) - delta)
        dslb = ds_l.astype(BF16)
        dscb = ds_c.astype(BF16)
        dq_ref[...] = ((dot_nn(dslb, kl) + dot_nn(dscb, kc)) * scale).astype(dq_ref.dtype)
        r0 = pl.multiple_of(ks4_of(b) * KP, KP)
        dk_ref[pl.ds(r0, 4 * KP), :] += dot_tn(dslb, q)
        dv_ref[pl.ds(r0, 4 * KP), :] += dot_tn(p_l.astype(BF16), dob)
        dk_ref[cfg.T:cfg.T + KP, :] += dot_tn(dscb, q)
        dv_ref[cfg.T:cfg.T + KP, :] += dot_tn(p_c.astype(BF16), dob)
        bias_ref[...] = ds_l
        for i, J, row, _, _ in _na_tiles(cfg, b):
            dtab_ref[row] += bias_ref[i * GW:(i + 1) * GW, J * LANE:(J + 1) * LANE]

    full = pl.BlockSpec((L, LANE), lambda h, b: (0, h))
    tabs = pl.BlockSpec((None, _NTAB, GW, LANE), lambda h, b: (h, 0, 0, 0))
    return _pc(body, name=name, grid=(NH, nb),
               in_specs=[qs, *kparts, kctx, *vparts, vctx, tabs,
                         pl.BlockSpec((TQ, LANE), lambda h, b: (b, h)),
                         pl.BlockSpec((None, TQ, 1), lambda h, b: (h, b, 0)),
                         pl.BlockSpec((TQ, LANE), lambda h, b: (b, dcol0 + h))],
               out_specs=[pl.BlockSpec((TQ, LANE), lambda h, b: (b, h)), full, full, tabs],
               out_shape=[jax.ShapeDtypeStruct((cfg.T, NH * LANE), BF16),
                          jax.ShapeDtypeStruct((L, NH * LANE), F32), jax.ShapeDtypeStruct((L, NH * LANE), F32),
                          jax.ShapeDtypeStruct((NH, _NTAB, GW, LANE), F32)],
               scratch_shapes=[pltpu.VMEM((TQ, 4 * KP), F32)],
               compiler_params=_cp(("parallel", "arbitrary")))(
                   P, *([P] * 5), *([P] * 5), tab, o, lse, dmix)


def na_ctx_fwd(cfg, name, P):
    off = _offsets(cfg)
    TC, NH = cfg.TC, cfg.NH
    rb = cfg.T // TC
    scale = cfg.NDH ** -0.5

    def body(q_ref, k_ref, v_ref, o_ref, lse_ref):
        q = (q_ref[...] * scale).astype(BF16)
        s = dot_nt(q, k_ref[...].astype(BF16))
        m = jnp.max(s, axis=-1, keepdims=True)
        p = jnp.exp(s - m)
        den = jnp.sum(p, axis=-1, keepdims=True)
        o_ref[...] = dot_nn(p.astype(BF16), v_ref[...].astype(BF16)) / den
        lse_ref[...] = m + jnp.log(den)

    spec = lambda nm: pl.BlockSpec((TC, LANE), functools.partial(lambda h, c0: (rb, c0 + h), c0=off[nm] // LANE))
    return _pc(body, name=name, grid=(NH,), in_specs=[spec("nq"), spec("nk"), spec("nv")],
               out_specs=[pl.BlockSpec((TC, LANE), lambda h: (0, h)), pl.BlockSpec((None, TC, 1), lambda h: (h, 0, 0))],
               out_shape=[jax.ShapeDtypeStruct((TC, NH * LANE), F32), jax.ShapeDtypeStruct((NH, TC, 1), F32)],
               compiler_params=_cp(("parallel",)))(P, P, P)


def na_ctx_bwd(cfg, name, P, o, lse, dmix, dcol0, dk_in, dv_in):
    off = _offsets(cfg)
    TC, NH = cfg.TC, cfg.NH
    rb = cfg.T // TC
    scale = cfg.NDH ** -0.5

    def body(q_ref, k_ref, v_ref, o_ref, lse_ref, do_ref, dki_ref, dvi_ref, dq_ref, dk_ref, dv_ref):
        q = (q_ref[...] * scale).astype(BF16)
        kb = k_ref[...].astype(BF16)
        vb = v_ref[...].astype(BF16)
        do = do_ref[...]
        dob = do.astype(BF16)
        p = jnp.exp(dot_nt(q, kb) - lse_ref[...])
        delta = jnp.sum(do * o_ref[...], axis=-1, keepdims=True)
        ds = (p * (dot_nt(dob, vb) - delta)).astype(BF16)
        dq_ref[...] = (dot_nn(ds, kb) * scale).astype(dq_ref.dtype)
        dk_ref[...] = (dki_ref[...] + dot_tn(ds, q)).astype(dk_ref.dtype)
        dv_ref[...] = (dvi_ref[...] + dot_tn(p.astype(BF16), dob)).astype(dv_ref.dtype)

    spec = lambda nm: pl.BlockSpec((TC, LANE), functools.partial(lambda h, c0: (rb, c0 + h), c0=off[nm] // LANE))
    hb = pl.BlockSpec((TC, LANE), lambda h: (0, h))
    ctxrow = pl.BlockSpec((TC, LANE), lambda h: (rb, h))
    shp = jax.ShapeDtypeStruct((TC, NH * LANE), BF16)
    return _pc(body, name=name, grid=(NH,),
               in_specs=[spec("nq"), spec("nk"), spec("nv"), hb, pl.BlockSpec((None, TC, 1), lambda h: (h, 0, 0)),
                         pl.BlockSpec((TC, LANE), lambda h: (rb, dcol0 + h)), ctxrow, ctxrow],
               out_specs=[hb, hb, hb], out_shape=[shp, shp, shp],
               compiler_params=_cp(("parallel",)))(P, P, P, o, lse, dmix, dk_in, dv_in)


def na_rpb_grad(cfg, name, dtab):
    NH, GW = cfg.NH, cfg.GW
    na, nd = 2 * cfg.NAR - 1, 2 * cfg.NAC - 1
    oda, odb, _ = _na_col_onehot(cfg)
    E = np.concatenate([oda.T, odb.T], axis=1)
    rows = NH * _NTAB

    def body(z_ref, e_ref, o_ref):
        zv = z_ref[...]
        hi = zv.astype(BF16)
        lo = (zv - hi.astype(F32)).astype(BF16)
        e = e_ref[...]
        o_ref[...] = dot_nn(hi, e) + dot_nn(lo, e)

    g = _pc(body, name=name, out_shape=jax.ShapeDtypeStruct((rows, 2 * LANE), F32),
            compiler_params=_cp())(dtab.reshape(rows, GW * LANE), jnp.asarray(E, BF16))
    g = g.reshape(NH, _NTAB, 2, LANE)
    return g[:, 1:1 + na, 0, :nd] + g[:, 0:na, 1, :nd]


def _seq_tiles(cfg):
    T, TC, TB = cfg.T, cfg.TC, cfg.TB
    tiles = []
    for i in range((T + TC) // TB):
        r0 = i * TB
        tiles.append((r0, r0 == 0 or r0 == T, r0 + TB == T or r0 + TB == T + TC))
    return tiles


def _shift3(ref_get, r0, TB, start, end, width):
    cur = ref_get(r0, TB)
    if start or end:
        rowi = lax.broadcasted_iota(jnp.int32, (TB, width), 0)
    up = jnp.where(rowi == 0, 0.0, pltpu.roll(cur, 1, 0)) if start else ref_get(r0 - 1, TB)
    dn = jnp.where(rowi == TB - 1, 0.0, pltpu.roll(cur, TB - 1, 0)) if end else ref_get(r0 + 1, TB)
    return up, cur, dn


def ffn_act_fwd(cfg, name, U2, w, b):
    _, L, DFF = U2.shape
    TB = cfg.TB
    tiles = _seq_tiles(cfg)

    def body(u_ref, w_ref, b_ref, a_ref):
        def plane(p, r0, st, en):
            up, cur, dn = _shift3(lambda r, n: u_ref[p, r:r + n, :], r0, TB, st, en, LANE)
            wv = w_ref[p]
            return wv[0:1, :] * up + wv[1:2, :] * cur + wv[2:3, :] * dn + b_ref[p]

        for r0, st, en in tiles:
            val = plane(0, r0, st, en)
            gate = plane(1, r0, st, en)
            a_ref[r0:r0 + TB, :] = (_silu(gate) * val).astype(a_ref.dtype)

    return _pc(body, name=name, grid=(DFF // LANE,),
               in_specs=[pl.BlockSpec((2, L, LANE), lambda j: (0, 0, j)),
                         pl.BlockSpec((2, 8, LANE), lambda j: (0, 0, j)),
                         pl.BlockSpec((2, 1, LANE), lambda j: (0, 0, j))],
               out_specs=pl.BlockSpec((L, LANE), lambda j: (0, j)),
               out_shape=jax.ShapeDtypeStruct((L, DFF), BF16),
               compiler_params=_cp(("parallel",)))(U2, w, b)


def ffn_act_bwd(cfg, name, U2, w, b, dA):
    _, L, DFF = U2.shape
    TB = cfg.TB
    tiles = _seq_tiles(cfg)

    def body(u_ref, w_ref, b_ref, da_ref, du_ref, dw_ref, db_ref, dbuf):
        dw_ref[...] = jnp.zeros_like(dw_ref)
        db_ref[...] = jnp.zeros_like(db_ref)
        for r0, st, en in tiles:
            shifted = []
            pre = []
            for p in range(2):
                up, cur, dn = _shift3(lambda r, n: u_ref[p, r:r + n, :], r0, TB, st, en, LANE)
                wv = w_ref[p]
                shifted.append((up, cur, dn))
                pre.append(wv[0:1, :] * up + wv[1:2, :] * cur + wv[2:3, :] * dn + b_ref[p])
            val, gate = pre
            da = da_ref[r0:r0 + TB, :]
            dpre = (da * _silu(gate), da * val * _dsilu(gate))
            for p in range(2):
                dbuf[p, r0:r0 + TB, :] = dpre[p]
                for k in range(3):
                    dw_ref[p, k:k + 1, :] += jnp.sum(dpre[p] * shifted[p][k], axis=0, keepdims=True)
                db_ref[p] += jnp.sum(dpre[p], axis=0, keepdims=True)
        for r0, st, en in tiles:
            for p in range(2):
                up, cur, dn = _shift3(lambda r, n: dbuf[p, r:r + n, :], r0, TB, st, en, LANE)
                wv = w_ref[p]
                du_ref[p, r0:r0 + TB, :] = (wv[0:1, :] * dn + wv[1:2, :] * cur + wv[2:3, :] * up).astype(du_ref.dtype)

    blk = pl.BlockSpec((2, L, LANE), lambda j: (0, 0, j))
    wspec = pl.BlockSpec((2, 8, LANE), lambda j: (0, 0, j))
    bspec = pl.BlockSpec((2, 1, LANE), lambda j: (0, 0, j))
    return _pc(body, name=name, grid=(DFF // LANE,),
               in_specs=[blk, wspec, bspec, pl.BlockSpec((L, LANE), lambda j: (0, j))],
               out_specs=[blk, wspec, bspec],
               out_shape=[jax.ShapeDtypeStruct((2, L, DFF), BF16), jax.ShapeDtypeStruct((2, 8, DFF), F32),
                          jax.ShapeDtypeStruct((2, 1, DFF), F32)],
               scratch_shapes=[pltpu.VMEM((2, L, LANE), F32)],
               compiler_params=_cp(("parallel",)))(U2, w, b, dA)


def _tm(L, parts):
    assert L % parts == 0
    return L // parts


def layer_fwd(cfg, l, XS, mod, wts, small, tabs):
    L, D = XS.shape
    off = _offsets(cfg)
    DIN = off["end"]
    tmA = _tm(L, 4)
    sv = {"XS": XS, "W": {}}

    def weight(name, after):
        sv["W"][name], tok = wts(name, after)
        return sv["W"][name], tok

    Win4, _ = weight("w_in", XS)
    nbi = Win4.shape[3]
    h1 = norm_mod_fwd(cfg, f"norm1_fwd_{l}", XS, small["norm1_g"], mod["sc1"], mod["sh1"])
    P = matmul(f"mm_in_{l}", h1, Win4, contract="nn", grid=(4, L // tmA),
               a_spec=pl.BlockSpec((tmA, D), lambda n, m: (m, 0)),
               b_spec=pl.BlockSpec((None, None, D, nbi), lambda n, m: (n, l, 0, 0)),
               out_shape=jax.ShapeDtypeStruct((L, DIN), F32),
               out_spec=pl.BlockSpec((tmA, nbi), lambda n, m: (m, n)), nk=1)
    qr, kr = rope_fwd(cfg, f"rope_fwd_{l}", P, tabs["cos"], tabs["sin"])
    o_f, o_b, st = retention_fwd(cfg, f"ret_fwd_{l}", qr, kr, P, small["lam"])
    o2 = (o_f, o_b)
    ret = ggn_fwd(cfg, f"ggn_fwd_{l}", o2, P, small["ret_gn_g"])
    ycv = glu_dwconv_fwd(cfg, f"dwconv_fwd_{l}", P, small["conv_dw_w"], small["conv_dw_b"])
    act = ln_silu_fwd(cfg, f"ln_silu_fwd_{l}", ycv, small["conv_ln_g"], small["conv_ln_b"])
    Wpw4, _ = weight("conv_pw", act)
    cv = mm_rowsharded(f"mm_pw_{l}", act, Wpw4, l, BF16, cfg.CW)
    bias = na_tables(cfg, f"na_tables_{l}", small["na_rpb"])
    na_l, lse = na_fwd(cfg, f"na_fwd_{l}", P, bias)
    na_c, lse_c = na_ctx_fwd(cfg, f"na_ctx_fwd_{l}", P)
    mix = jnp.concatenate([ret, cv, jnp.concatenate([na_l, na_c], axis=0).astype(BF16)], axis=1)
    Wout4, tok = weight("w_out", mix)
    Y1 = mm_rowsharded(f"mm_out_{l}", mix, Wout4, l, F32, D)
    XM = resid_fwd(cfg, f"resid1_fwd_{l}", XS, Y1, mod["g1"] if tok is None else mod["g1"] + tok)
    h2 = norm_mod_fwd(cfg, f"norm2_fwd_{l}", XM, small["norm2_g"], mod["sc2"], mod["sh2"])
    Wup4, _ = weight("ffn_up", h2)
    nbu = Wup4.shape[3]
    tnu = nbu // 2
    U2 = matmul(f"mm_up_{l}", h2, Wup4, contract="nn", grid=(8, L // tmA),
                a_spec=pl.BlockSpec((tmA, D), lambda n, m: (m, 0)),
                b_spec=pl.BlockSpec((None, None, D, tnu), lambda n, m: (n // 2, l, 0, n % 2)),
                out_shape=jax.ShapeDtypeStruct((2, L, cfg.DFF), F32),
                out_spec=pl.BlockSpec((None, tmA, tnu), lambda n, m: (n // 4, m, n % 4)), nk=1)
    A = ffn_act_fwd(cfg, f"ffn_act_fwd_{l}", U2, small["ffn_dw_w"], small["ffn_dw_b"])
    Wdn4, _ = weight("ffn_down", A)
    Y2 = mm_rowsharded(f"mm_down_{l}", A, Wdn4, l, F32, D // 2)
    XO = resid_fwd(cfg, f"resid2_fwd_{l}", XM, Y2, mod["g2"])
    sv.update(h1=h1, P=P, qr=qr, kr=kr, o2=o2, st=st, ycv=ycv, act=act, bias=bias, na_l=na_l, lse=lse,
              na_c=na_c, lse_c=lse_c, mix=mix, Y1=Y1, XM=XM, h2=h2, U2=U2, A=A, Y2=Y2)
    return XO, sv


GRAD_GROUPS = (("ffn_down", "ffn_up"), ("w_out", "conv_pw", "w_in"))


def layer_bwd(cfg, l, dXO, sv, mod, wts, small, tabs, gbuf, ready):
    L, D = dXO.shape
    off = _offsets(cfg)
    DIN = off["end"]
    Win4, Wout4, Wup4, Wdn4, Wpw4 = wts["w_in"], wts["w_out"], wts["ffn_up"], wts["ffn_down"], wts["conv_pw"]
    tmA, tmB = _tm(L, 4), _tm(L, 8)
    depth = Win4.shape[1]
    gb, gs, dm = {}, {}, {}
    P = sv["P"]
    dY2, dm["g2"] = resid_bwd(cfg, f"resid2_bwd_{l}", dXO, sv["Y2"], mod["g2"])
    nbd = Wdn4.shape[2]
    dA = matmul(f"mm_down_da_{l}", dY2, Wdn4, contract="nt", grid=(4, L // tmA),
                a_spec=pl.BlockSpec((tmA, D), lambda j, m: (m, 0)),
                b_spec=pl.BlockSpec((None, None, nbd, D), lambda j, m: (j, l, 0, 0)),
                out_shape=jax.ShapeDtypeStruct((L, cfg.DFF), F32),
                out_spec=pl.BlockSpec((tmA, nbd), lambda j, m: (m, j)), nk=1)
    gb["ffn_down"] = wgrad(cfg, f"mm_down_dw_{l}", sv["A"], dY2,
                           lambda rb, ri: pl.BlockSpec((rb, nbd), lambda j, m: (ri(m), j)),
                           lambda rb, ri: pl.BlockSpec((rb, D), lambda j, m: (ri(m), 0)),
                           jax.ShapeDtypeStruct((depth, 4, nbd, D), BF16),
                           pl.BlockSpec((None, None, nbd, D), lambda j, m: (l, j, 0, 0)), 4, gbuf.get("ffn_down"))
    dU2, dfw, dfb = ffn_act_bwd(cfg, f"ffn_act_bwd_{l}", sv["U2"], small["ffn_dw_w"], small["ffn_dw_b"], dA)
    gs["ffn_dw_w"], gs["ffn_dw_b"] = dfw, dfb
    nbu = Wup4.shape[3]
    tnu = nbu // 2
    dH2 = matmul(f"mm_up_dh_{l}", dU2, Wup4, contract="nt", grid=(L // tmA, 8),
                 a_spec=pl.BlockSpec((None, tmA, tnu), lambda m, n: (n // 4, m, n % 4)),
                 b_spec=pl.BlockSpec((None, None, D, tnu), lambda m, n: (n // 2, l, 0, n % 2)),
                 out_shape=jax.ShapeDtypeStruct((L, D), F32),
                 out_spec=pl.BlockSpec((tmA, D), lambda m, n: (m, 0)), nk=8)
    gb["ffn_up"] = wgrad(cfg, f"mm_up_dw_{l}", sv["h2"], dU2,
                         lambda rb, ri: pl.BlockSpec((rb, D), lambda n, m: (ri(m), 0)),
                         lambda rb, ri: pl.BlockSpec((None, rb, tnu), lambda n, m: (n // 4, ri(m), n % 4)),
                         jax.ShapeDtypeStruct((depth, 4, D, nbu), BF16),
                         pl.BlockSpec((None, None, D, tnu), lambda n, m: (l, n // 2, 0, n % 2)), 8, gbuf.get("ffn_up"))
    dXM, dm["sc2"], dm["sh2"], gs["norm2_g"] = norm_mod_bwd(
        cfg, f"norm2_bwd_{l}", dH2, sv["XM"], small["norm2_g"], mod["sc2"], dXO)
    tok = ready(GRAD_GROUPS[0], gb)
    dY1, dm["g1"] = resid_bwd(cfg, f"resid1_bwd_{l}", dXM, sv["Y1"], mod["g1"] if tok is None else mod["g1"] + tok)
    nbo = Wout4.shape[2]
    dmix = matmul(f"mm_out_dmix_{l}", dY1, Wout4, contract="nt", grid=(4, L // tmA),
                  a_spec=pl.BlockSpec((tmA, D), lambda j, m: (m, 0)),
                  b_spec=pl.BlockSpec((None, None, nbo, D), lambda j, m: (j, l, 0, 0)),
                  out_shape=jax.ShapeDtypeStruct((L, D), F32),
                  out_spec=pl.BlockSpec((tmA, nbo), lambda j, m: (m, j)), nk=1)
    gb["w_out"] = wgrad(cfg, f"mm_out_dw_{l}", sv["mix"], dY1,
                        lambda rb, ri: pl.BlockSpec((rb, nbo), lambda j, m: (ri(m), j)),
                        lambda rb, ri: pl.BlockSpec((rb, D), lambda j, m: (ri(m), 0)),
                        jax.ShapeDtypeStruct((depth, 4, nbo, D), BF16),
                        pl.BlockSpec((None, None, nbo, D), lambda j, m: (l, j, 0, 0)), 4, gbuf.get("w_out"))
    RW = cfg.RH * cfg.RDV
    do, dlg, gs["ret_gn_g"] = ggn_bwd(cfg, f"ggn_bwd_{l}", dmix, sv["o2"], P, small["ret_gn_g"], 0)
    dqf, dqb, dkf, dkb, dvf, dvb, dlam = retention_bwd(
        cfg, f"ret_bwd_{l}", sv["qr"], sv["kr"], P, small["lam"], sv["st"], do)
    gs["lam"] = dlam[:, :, 0, 0]
    dlq, dlk = rope_bwd(cfg, f"rope_bwd_{l}", (dqf, dqb), (dkf, dkb), tabs["cos"], tabs["sin"])
    dlv = add_cast(cfg, f"ret_dv_{l}", dvf, dvb)
    dcv = cast_cols(cfg, f"conv_dcv_{l}", dmix, RW // LANE, cfg.CW // LANE, LANE)
    nbp = Wpw4.shape[2]
    dact = matmul(f"mm_pw_dact_{l}", dcv, Wpw4, contract="nt", grid=(4, L // tmA),
                  a_spec=pl.BlockSpec((tmA, cfg.CW), lambda j, m: (m, 0)),
                  b_spec=pl.BlockSpec((None, None, nbp, cfg.CW), lambda j, m: (j, l, 0, 0)),
                  out_shape=jax.ShapeDtypeStruct((L, cfg.CW), F32),
                  out_spec=pl.BlockSpec((tmA, nbp), lambda j, m: (m, j)), nk=1)
    gb["conv_pw"] = wgrad(cfg, f"mm_pw_dw_{l}", sv["act"], dcv,
                          lambda rb, ri: pl.BlockSpec((rb, nbp), lambda j, m: (ri(m), j)),
                          lambda rb, ri: pl.BlockSpec((rb, cfg.CW), lambda j, m: (ri(m), 0)),
                          jax.ShapeDtypeStruct((depth, 4, nbp, cfg.CW), BF16),
                          pl.BlockSpec((None, None, nbp, cfg.CW), lambda j, m: (l, j, 0, 0)), 4, gbuf.get("conv_pw"))
    dycv, gs["conv_ln_g"], gs["conv_ln_b"] = ln_silu_bwd(
        cfg, f"ln_silu_bwd_{l}", dact, sv["ycv"], small["conv_ln_g"], small["conv_ln_b"])
    dla, dlb, gs["conv_dw_w"], gs["conv_dw_b"] = glu_dwconv_bwd(cfg, f"dwconv_bwd_{l}", P, small["conv_dw_w"], dycv)
    nac0 = (RW + cfg.CW) // LANE
    dnq_l, dnk, dnv, dsb = na_bwd(cfg, f"na_bwd_{l}", P, sv["bias"], sv["na_l"], sv["lse"], dmix, nac0)
    dnq_c, dnk_c, dnv_c = na_ctx_bwd(cfg, f"na_ctx_bwd_{l}", P, sv["na_c"], sv["lse_c"], dmix, nac0, dnk, dnv)
    gs["na_rpb"] = na_rpb_grad(cfg, f"na_rpb_{l}", dsb)
    dnq = jnp.concatenate([dnq_l, dnq_c], axis=0)
    dnk = jnp.concatenate([dnk[:cfg.T].astype(BF16), dnk_c], axis=0)
    dnv = jnp.concatenate([dnv[:cfg.T].astype(BF16), dnv_c], axis=0)
    dP = jnp.concatenate([dlq, dlk, dlv, dlg, dla, dlb, dnq, dnk, dnv], axis=1)
    nbi = Win4.shape[3]
    dH1 = matmul(f"mm_in_dh_{l}", dP, Win4, contract="nt", grid=(L // tmA, 4),
                 a_spec=pl.BlockSpec((tmA, nbi), lambda m, n: (m, n)),
                 b_spec=pl.BlockSpec((None, None, D, nbi), lambda m, n: (n, l, 0, 0)),
                 out_shape=jax.ShapeDtypeStruct((L, D), F32),
                 out_spec=pl.BlockSpec((tmA, D), lambda m, n: (m, 0)), nk=4)
    gb["w_in"] = wgrad(cfg, f"mm_in_dw_{l}", sv["h1"], dP,
                       lambda rb, ri: pl.BlockSpec((rb, D), lambda n, m: (ri(m), 0)),
                       lambda rb, ri: pl.BlockSpec((rb, nbi), lambda n, m: (ri(m), n)),
                       jax.ShapeDtypeStruct((depth, 4, D, nbi), BF16),
                       pl.BlockSpec((None, None, D, nbi), lambda n, m: (l, n, 0, 0)), 4, gbuf.get("w_in"))
    dXS, dm["sc1"], dm["sh1"], gs["norm1_g"] = norm_mod_bwd(
        cfg, f"norm1_bwd_{l}", dH1, sv["XS"], small["norm1_g"], mod["sc1"], dXM)
    return dXS, gb, gs, dm, ready(GRAD_GROUPS[1], gb)


def _layer_small(cfg, l, sp):
    DFF = cfg.DFF
    fw = sp["ffn_dw_w"][l].reshape(3, 2, DFF).transpose(1, 0, 2)
    fw = jnp.concatenate([fw, jnp.zeros((2, 5, DFF), F32)], axis=1)
    cw = jnp.concatenate([sp["conv_dw_w"][l], jnp.zeros((32 - cfg.CK, cfg.CW), F32)], axis=0)
    return dict(
        norm1_g=sp["norm1_g"][l][None], norm2_g=sp["norm2_g"][l][None],
        lam=jax.nn.log_sigmoid(sp["ret_decay"][l]), ret_gn_g=sp["ret_gn_g"][l][None],
        conv_dw_w=cw, conv_dw_b=sp["conv_dw_b"][l][None], conv_ln_g=sp["conv_ln_g"][l][None],
        conv_ln_b=sp["conv_ln_b"][l][None], na_rpb=sp["na_rpb"][l],
        ffn_dw_w=fw, ffn_dw_b=sp["ffn_dw_b"][l].reshape(2, 1, DFF))


def local_step(cfg, x, ctx, tgt, mods, wts, sp, grads_ready=lambda l, names, gb: None):
    depth = sp["norm1_g"].shape[0]
    cos, sin = rope_tables(cfg)
    tabs = dict(cos=cos, sin=sin)
    XS = jnp.concatenate([x, ctx], axis=0)
    smalls = [_layer_small(cfg, l, sp) for l in range(depth)]
    saves = []
    for l in range(depth):
        XS, sv = layer_fwd(cfg, l, XS, mods[l], functools.partial(wts, l), smalls[l], tabs)
        saves.append(sv)
    ls, dX, dfg = final_loss(cfg, "final_loss", XS, sp["final_g"][None], tgt)
    gb, gss, dms = {}, [None] * depth, [None] * depth
    token = None
    for l in reversed(range(depth)):
        mod = mods[l] if token is None else {**mods[l], "g2": mods[l]["g2"] + token}
        dX, gb, gss[l], dms[l], token = layer_bwd(cfg, l, dX, saves[l], mod, saves[l]["W"], smalls[l], tabs, gb,
                                                  functools.partial(grads_ready, l))
    return ls[0, 0], dX[:cfg.T], gb, gss, dms, dfg[0]


MESH = pl.DeviceIdType.MESH
N_DEV = 8
N_CHIP = 4
BIG = ("w_in", "w_out", "ffn_up", "ffn_down", "conv_pw")
_ANY = pl.BlockSpec(memory_space=pl.ANY)


def _place():
    x, y, c = lax.axis_index("x"), lax.axis_index("y"), lax.axis_index("c")
    chips = [(1 - x, y), (x, 1 - y), (1 - x, 1 - y)]
    return x, y, c, chips


def allgather8(name, xs):
    m_per, n = xs.shape

    def body(x_ref, out_ref, send_sems, recv_sems, local_sem):
        x, y, c, chips = _place()
        me, sibling = (x, y, c), (x, y, 1 - c)

        def rows(px, py, pc):
            return out_ref.at[pl.ds((4 * px + 2 * py + pc) * m_per, m_per), :]

        def copy(k, block, to, src=None):
            return pltpu.make_async_remote_copy(
                src_ref=rows(*block) if src is None else src, dst_ref=rows(*block),
                send_sem=send_sems.at[k], recv_sem=recv_sems.at[k], device_id=to, device_id_type=MESH)

        mine = pltpu.make_async_copy(x_ref, rows(*me), local_sem)
        mine.start()
        first = [copy(0, me, sibling, src=x_ref)]
        first += [copy(1 + j, me, (*chip, c), src=x_ref) for j, chip in enumerate(chips)]
        for cp in first:
            cp.start()
        passed = [copy(4 + j, (*chip, c), sibling) for j, chip in enumerate(chips)]
        for j, chip in enumerate(chips):
            copy(1 + j, (*chip, c), me).wait_recv()
            passed[j].start()
        copy(0, sibling, me).wait_recv()
        for j, chip in enumerate(chips):
            copy(4 + j, (*chip, 1 - c), me).wait_recv()
        for cp in first + passed:
            cp.wait_send()
        mine.wait()

    return _pc(body, name=name, out_shape=jax.ShapeDtypeStruct((N_DEV * m_per, n), xs.dtype),
               in_specs=[pl.BlockSpec(memory_space=pltpu.VMEM)], out_specs=pl.BlockSpec(memory_space=pltpu.VMEM),
               scratch_shapes=[pltpu.SemaphoreType.DMA((7,)), pltpu.SemaphoreType.DMA((7,)), pltpu.SemaphoreType.DMA],
               compiler_params=pltpu.CompilerParams(vmem_limit_bytes=VMEM_LIMIT))(xs)


def _wpiece(ref, layer, chip_idx, half):
    rh = ref.shape[2] // 2
    return ref.at[chip_idx, layer, pl.ds(half * rh, rh)]


def _wcopy(ref, layer, chip_idx, half, send_sems, recv_sems, k, to):
    piece = _wpiece(ref, layer, chip_idx, half)
    return pltpu.make_async_remote_copy(src_ref=piece, dst_ref=piece, send_sem=send_sems.at[k],
                                        recv_sem=recv_sems.at[k], device_id=to, device_id_type=MESH)


def _w_ici_sends(outs, layer, send_sems, recv_sems):
    x, y, c, chips = _place()
    return [_wcopy(outs[a], layer, 2 * x + y, c, send_sems, recv_sems, 3 * a + t, (*chip, c))
            for a in range(len(outs)) for t, chip in enumerate(chips)]


def _w_ici_landed(outs, layer, send_sems, recv_sems):
    x, y, c, chips = _place()
    return [_wcopy(outs[a], layer, 2 * chip[0] + chip[1], c, send_sems, recv_sems, 3 * a + t, (x, y, c))
            for a in range(len(outs)) for t, chip in enumerate(chips)]


def _w_forward(outs, layer, send_sems, recv_sems, base):
    x, y, c, chips = _place()
    n = len(outs)
    sends = [_wcopy(outs[a], layer, 2 * chip[0] + chip[1], c, send_sems, recv_sems, base + 3 * a + t, (x, y, 1 - c))
             for a in range(n) for t, chip in enumerate(chips)]
    recvs = [_wcopy(outs[a], layer, 2 * chip[0] + chip[1], 1 - c, send_sems, recv_sems, base + 3 * a + t, (x, y, c))
             for a in range(n) for t, chip in enumerate(chips)]
    return sends, recvs


def allgather_layer(name, bufs, layer):
    n = len(bufs)

    def body(*refs):
        outs = refs[n:2 * n]
        send_sems, recv_sems = refs[2 * n:]
        sent = _w_ici_sends(outs, layer, send_sems, recv_sems)
        for cp in sent:
            cp.start()
        fwd, from_sib = _w_forward(outs, layer, send_sems, recv_sems, 3 * n)
        for landed, fw in zip(_w_ici_landed(outs, layer, send_sems, recv_sems), fwd):
            landed.wait_recv()
            fw.start()
        for cp in from_sib:
            cp.wait_recv()
        for cp in sent + fwd:
            cp.wait_send()

    return _pc(body, name=name, out_shape=[jax.ShapeDtypeStruct(b.shape, b.dtype) for b in bufs],
               in_specs=[_ANY] * n, out_specs=[_ANY] * n, input_output_aliases={a: a for a in range(n)},
               scratch_shapes=[pltpu.SemaphoreType.DMA((6 * n,)), pltpu.SemaphoreType.DMA((6 * n,))])(*bufs)


_HBM = pl.BlockSpec(memory_space=pltpu.HBM)
_SEM = pl.BlockSpec(memory_space=pltpu.SEMAPHORE)
_EFFECT = pltpu.SideEffectType.DATAFLOW_SIDE_EFFECTING


def allgather_layer_start(name, bufs, layer, after):
    n = len(bufs)

    def body(*refs):
        send_sems, recv_sems = refs[n + 1:n + 3]
        outs = refs[n + 3:2 * n + 3]
        token = refs[2 * n + 3]
        for cp in _w_ici_sends(outs, layer, send_sems, recv_sems):
            cp.start()
        token[...] = jnp.zeros_like(token)

    res = _pc(body, name=name,
              out_shape=(pltpu.SemaphoreType.DMA((3 * n,)), pltpu.SemaphoreType.DMA((3 * n,)),
                         *[pltpu.HBM(b.shape, b.dtype) for b in bufs], jax.ShapeDtypeStruct((8, LANE), F32)),
              in_specs=[_HBM] * n + [_ANY],
              out_specs=(_SEM, _SEM, *([_HBM] * n), pl.BlockSpec(memory_space=pltpu.VMEM)),
              input_output_aliases={a: a + 2 for a in range(n)},
              compiler_params=pltpu.CompilerParams(has_side_effects=_EFFECT))(
                  *[pltpu.with_memory_space_constraint(b, pltpu.HBM) for b in bufs], after)
    return res[0], res[1], list(res[2:2 + n]), res[2 + n]


def allgather_layer_wait(name, bufs, send_sems, recv_sems, after, layer):
    n = len(bufs)

    def body(*refs):
        ins = refs[:n]
        send_sems, recv_sems = refs[n:n + 2]
        for cp in _w_ici_sends(ins, layer, send_sems, recv_sems):
            cp.wait_send()
        for cp in _w_ici_landed(ins, layer, send_sems, recv_sems):
            cp.wait_recv()

    return _pc(body, name=name, out_shape=tuple(pltpu.HBM(b.shape, b.dtype) for b in bufs),
               in_specs=[_HBM] * n + [_SEM, _SEM, _ANY], out_specs=tuple([_HBM] * n),
               input_output_aliases={a: a for a in range(n)},
               compiler_params=pltpu.CompilerParams(has_side_effects=_EFFECT))(*bufs, send_sems, recv_sems, after)


def forward_halves(name, bufs, layer):
    n = len(bufs)

    def body(*refs):
        outs = refs[n:2 * n]
        send_sems, recv_sems = refs[2 * n:]
        fwd, from_sib = _w_forward(outs, layer, send_sems, recv_sems, 0)
        for cp in fwd:
            cp.start()
        for cp in from_sib:
            cp.wait_recv()
        for cp in fwd:
            cp.wait_send()

    return _pc(body, name=name, out_shape=[jax.ShapeDtypeStruct(b.shape, b.dtype) for b in bufs],
               in_specs=[_ANY] * n, out_specs=[_ANY] * n, input_output_aliases={a: a for a in range(n)},
               scratch_shapes=[pltpu.SemaphoreType.DMA((3 * n,)), pltpu.SemaphoreType.DMA((3 * n,))])(*bufs)


def exchange_rows(name, grads, layer):
    n = len(grads)

    def body(*refs):
        ins, outs = refs[:n], refs[n:2 * n]
        send_sems, recv_sems = refs[2 * n:]
        x, y, c, _ = _place()
        cps = []
        for a in range(n):
            rh = ins[a].shape[2] // 2
            cps.append(pltpu.make_async_remote_copy(
                src_ref=ins[a].at[layer, pl.ds(0, N_CHIP), pl.ds((1 - c) * rh, rh)], dst_ref=outs[a],
                send_sem=send_sems.at[a], recv_sem=recv_sems.at[a], device_id=(x, y, 1 - c), device_id_type=MESH))
        for cp in cps:
            cp.start()
        for cp in cps:
            cp.wait()

    return _pc(body, name=name,
               out_shape=[jax.ShapeDtypeStruct((N_CHIP, g.shape[2] // 2, g.shape[3]), g.dtype) for g in grads],
               in_specs=[_ANY] * n, out_specs=[_ANY] * n,
               scratch_shapes=[pltpu.SemaphoreType.DMA((n,)), pltpu.SemaphoreType.DMA((n,))])(*grads)


def _scatter_sends(parts, lands, send_sems, recv_sems):
    x, y, c, chips = _place()
    return [pltpu.make_async_remote_copy(
        src_ref=parts[a].at[2 * chip[0] + chip[1]], dst_ref=lands[a].at[2 * x + y], send_sem=send_sems.at[3 * a + t],
        recv_sem=recv_sems.at[3 * a + t], device_id=(*chip, c), device_id_type=MESH)
        for a in range(len(parts)) for t, chip in enumerate(chips)]


def _scatter_landed(lands, send_sems, recv_sems):
    x, y, c, chips = _place()
    return [pltpu.make_async_remote_copy(
        src_ref=lands[a].at[2 * chip[0] + chip[1]], dst_ref=lands[a].at[2 * chip[0] + chip[1]],
        send_sem=send_sems.at[3 * a + t], recv_sem=recv_sems.at[3 * a + t], device_id=(x, y, c), device_id_type=MESH)
        for a in range(len(lands)) for t, chip in enumerate(chips)]


def scatter_slices(name, parts, lands):
    n = len(parts)

    def body(*refs):
        ins, outs = refs[:n], refs[2 * n:3 * n]
        send_sems, recv_sems = refs[3 * n:]
        cps = _scatter_sends(ins, outs, send_sems, recv_sems)
        for cp in cps:
            cp.start()
        for cp in _scatter_landed(outs, send_sems, recv_sems):
            cp.wait_recv()
        for cp in cps:
            cp.wait_send()

    return _pc(body, name=name, out_shape=[jax.ShapeDtypeStruct(p.shape, p.dtype) for p in lands],
               in_specs=[_ANY] * (2 * n), out_specs=[_ANY] * n,
               input_output_aliases={n + a: a for a in range(n)},
               scratch_shapes=[pltpu.SemaphoreType.DMA((3 * n,)), pltpu.SemaphoreType.DMA((3 * n,))])(*parts, *lands)


def scatter_slices_start(name, parts, lands):
    n = len(parts)

    def body(*refs):
        send_sems, recv_sems = refs[2 * n:2 * n + 2]
        p_out, l_out = refs[2 * n + 2:3 * n + 2], refs[3 * n + 2:4 * n + 2]
        token = refs[4 * n + 2]
        for cp in _scatter_sends(p_out, l_out, send_sems, recv_sems):
            cp.start()
        token[...] = jnp.zeros_like(token)

    both = list(parts) + list(lands)
    res = _pc(body, name=name,
              out_shape=(pltpu.SemaphoreType.DMA((3 * n,)), pltpu.SemaphoreType.DMA((3 * n,)),
                         *[pltpu.HBM(b.shape, b.dtype) for b in both], jax.ShapeDtypeStruct((8, LANE), F32)),
              in_specs=[_HBM] * (2 * n),
              out_specs=(_SEM, _SEM, *([_HBM] * (2 * n)), pl.BlockSpec(memory_space=pltpu.VMEM)),
              input_output_aliases={a: a + 2 for a in range(2 * n)},
              compiler_params=pltpu.CompilerParams(has_side_effects=_EFFECT))(
                  *[pltpu.with_memory_space_constraint(b, pltpu.HBM) for b in both])
    return res[0], res[1], list(res[2:2 + n]), list(res[2 + n:2 + 2 * n]), res[2 + 2 * n]


def scatter_slices_wait(name, parts, lands, send_sems, recv_sems, after):
    n = len(parts)

    def body(*refs):
        p_in, l_in = refs[:n], refs[n:2 * n]
        send_sems, recv_sems = refs[2 * n:2 * n + 2]
        for cp in _scatter_sends(p_in, l_in, send_sems, recv_sems):
            cp.wait_send()
        for cp in _scatter_landed(l_in, send_sems, recv_sems):
            cp.wait_recv()

    both = list(parts) + list(lands)
    res = _pc(body, name=name, out_shape=tuple(pltpu.HBM(b.shape, b.dtype) for b in both),
              in_specs=[_HBM] * (2 * n) + [_SEM, _SEM, _ANY], out_specs=tuple([_HBM] * (2 * n)),
              input_output_aliases={a: a for a in range(2 * n)},
              compiler_params=pltpu.CompilerParams(has_side_effects=_EFFECT))(*both, send_sems, recv_sems, after)
    return list(res[n:])


def share_rows(name, bufs):
    n = len(bufs)

    def body(*refs):
        outs = refs[n:2 * n]
        send_sems, recv_sems = refs[2 * n:]
        x, y, c, _ = _place()

        def half(a, h):
            return outs[a].at[pl.ds(0, 2), h]

        cps = [pltpu.make_async_remote_copy(
            src_ref=half(a, c), dst_ref=half(a, c), send_sem=send_sems.at[a], recv_sem=recv_sems.at[a],
            device_id=(x, y, 1 - c), device_id_type=MESH) for a in range(n)]
        for cp in cps:
            cp.start()
        for a in range(n):
            pltpu.make_async_remote_copy(
                src_ref=half(a, 1 - c), dst_ref=half(a, 1 - c), send_sem=send_sems.at[a],
                recv_sem=recv_sems.at[a], device_id=(x, y, c), device_id_type=MESH).wait_recv()
        for cp in cps:
            cp.wait_send()

    return _pc(body, name=name, out_shape=[jax.ShapeDtypeStruct(b.shape, b.dtype) for b in bufs],
               in_specs=[_ANY] * n, out_specs=[_ANY] * n, input_output_aliases={a: a for a in range(n)},
               scratch_shapes=[pltpu.SemaphoreType.DMA((n,)), pltpu.SemaphoreType.DMA((n,))])(*bufs)


def _row_tile(R, C, nbytes=1 << 20):
    t = 8
    while t * 2 <= R and R % (t * 2) == 0 and t * 2 * C * 4 <= nbytes:
        t *= 2
    assert R % t == 0
    return t


def to_bf16_block(name, w, chip_arr, layer):
    _, R, C = w.shape
    tr = _row_tile(R, C)

    def body(j_ref, w_ref, o_ref):
        o_ref[...] = w_ref[...].astype(o_ref.dtype)

    gs = pltpu.PrefetchScalarGridSpec(
        num_scalar_prefetch=1, grid=(R // tr,),
        in_specs=[pl.BlockSpec((None, tr, C), lambda i, j_ref: (layer, i, 0))],
        out_specs=pl.BlockSpec((None, None, tr, C), lambda i, j_ref: (j_ref[0], layer, i, 0)))
    return _pc(body, name=name, grid_spec=gs, out_shape=jax.ShapeDtypeStruct((N_CHIP,) + w.shape, BF16),
               compiler_params=_cp(("parallel",)))(chip_arr, w)


def add_rows(name, g, ra, c_arr, layer):
    _, _, R, C = g.shape
    rh = R // 2
    tr = _row_tile(rh, C)
    nb = rh // tr

    def body(c_ref, g_ref, r_ref, o_ref):
        o_ref[...] = (g_ref[...].astype(F32) + r_ref[...].astype(F32)).astype(o_ref.dtype)

    gs = pltpu.PrefetchScalarGridSpec(
        num_scalar_prefetch=1, grid=(N_CHIP, nb),
        in_specs=[pl.BlockSpec((None, None, tr, C), lambda j, i, c_ref: (layer, j, c_ref[0] * nb + i, 0)),
                  pl.BlockSpec((None, tr, C), lambda j, i, c_ref: (j, i, 0))],
        out_specs=pl.BlockSpec((None, tr, C), lambda j, i, c_ref: (j, i, 0)))
    return _pc(body, name=name, grid_spec=gs, out_shape=jax.ShapeDtypeStruct(ra.shape, BF16),
               compiler_params=_cp(("parallel", "parallel")))(c_arr, g, ra)


def sum_rows_into(name, landed, c_arr, layer, into):
    n, rh, C = landed.shape
    tr = _row_tile(rh, C, nbytes=1 << 19)

    def body(*refs):
        g_ref, o_ref = refs[1], refs[-1]
        acc = g_ref[0].astype(F32)
        for j in range(1, n):
            acc = acc + g_ref[j].astype(F32)
        o_ref[...] = acc

    in_specs, args, alias = [pl.BlockSpec((n, tr, C), lambda i, c_ref: (0, i, 0))], (c_arr, landed), {}
    if into is not None:
        in_specs, args, alias = in_specs + [_ANY], args + (into,), {2: 0}
    gs = pltpu.PrefetchScalarGridSpec(
        num_scalar_prefetch=1, grid=(rh // tr,), in_specs=in_specs,
        out_specs=pl.BlockSpec((None, None, tr, C), lambda i, c_ref: (layer, c_ref[0], i, 0)))
    return _pc(body, name=name, grid_spec=gs, out_shape=jax.ShapeDtypeStruct((2, 2, rh, C), F32),
               input_output_aliases=alias, compiler_params=_cp(("parallel",)))(*args)


def own_row(name, part, chip_arr):
    _, R, C = part.shape
    tr = _row_tile(R, C)

    def body(j_ref, p_ref, o_ref):
        o_ref[...] = p_ref[...]

    gs = pltpu.PrefetchScalarGridSpec(
        num_scalar_prefetch=1, grid=(R // tr,),
        in_specs=[pl.BlockSpec((None, tr, C), lambda i, j_ref: (j_ref[0], i, 0))],
        out_specs=pl.BlockSpec((None, tr, C), lambda i, j_ref: (j_ref[0], i, 0)))
    return _pc(body, name=name, grid_spec=gs, out_shape=jax.ShapeDtypeStruct(part.shape, part.dtype),
               compiler_params=_cp(("parallel",)))(chip_arr, part)


def sum_leading(name, g, plane=None):
    n, R, C = g.shape
    tr = _row_tile(R, C, nbytes=(1 << 21) // n)

    def body(*refs):
        g_ref, o_ref = refs[-2:]
        acc = g_ref[0].astype(F32)
        for j in range(1, n):
            acc = acc + g_ref[j].astype(F32)
        o_ref[...] = acc

    if plane is None:
        return _pc(body, name=name, grid=(R // tr,), in_specs=[pl.BlockSpec((n, tr, C), lambda i: (0, i, 0))],
                   out_specs=pl.BlockSpec((tr, C), lambda i: (i, 0)), out_shape=jax.ShapeDtypeStruct((R, C), F32),
                   compiler_params=_cp(("parallel",)))(g)
    count, idx = plane
    gs = pltpu.PrefetchScalarGridSpec(
        num_scalar_prefetch=1, grid=(R // tr,),
        in_specs=[pl.BlockSpec((n, tr, C), lambda i, p_ref: (0, i, 0))],
        out_specs=pl.BlockSpec((None, tr, C), lambda i, p_ref: (p_ref[0], i, 0)))
    return _pc(body, name=name, grid_spec=gs, out_shape=jax.ShapeDtypeStruct((count, R, C), F32),
               compiler_params=_cp(("parallel",)))(idx, g)


def adamw(name, w, g, m, v):
    R, C = w.shape
    tr = _row_tile(R, C)

    def body(w_ref, g_ref, m_ref, v_ref, d_ref, mo_ref, vo_ref):
        gv = g_ref[...]
        mn = ADAM_B1 * m_ref[...] + (1.0 - ADAM_B1) * gv
        vn = ADAM_B2 * v_ref[...] + (1.0 - ADAM_B2) * (gv * gv)
        m_hat = mn / (1.0 - ADAM_B1 ** ADAM_STEP)
        v_hat = vn / (1.0 - ADAM_B2 ** ADAM_STEP)
        d_ref[...] = -ADAM_LR * (m_hat / (jnp.sqrt(v_hat) + ADAM_EPS) + ADAM_WD * w_ref[...])
        mo_ref[...] = mn
        vo_ref[...] = vn

    spec = pl.BlockSpec((tr, C), lambda i: (i, 0))
    shp = jax.ShapeDtypeStruct((R, C), F32)
    return _pc(body, name=name, grid=(R // tr,), in_specs=[spec] * 4, out_specs=[spec] * 3,
               out_shape=[shp] * 3, compiler_params=_cp(("parallel",)))(w, g, m, v)


_ADA_TN = 512


def adaln_fwd(name, cond, w, b):
    _, D, N = w.shape
    tn = min(_ADA_TN, N)

    def body(c_ref, w_ref, b_ref, o_ref):
        s = _silu(c_ref[...]).astype(BF16)
        o_ref[...] = dot_nn(s, w_ref[...].astype(BF16)) + b_ref[...]

    return _pc(body, name=name, grid=(2, N // tn),
               in_specs=[pl.BlockSpec((16, D), lambda l, n: (0, 0)),
                         pl.BlockSpec((None, D, tn), lambda l, n: (l, 0, n)),
                         pl.BlockSpec((None, 1, tn), lambda l, n: (l, 0, n))],
               out_specs=pl.BlockSpec((None, 16, tn), lambda l, n: (l, 0, n)),
               out_shape=jax.ShapeDtypeStruct((2, 16, N), F32),
               compiler_params=_cp(("parallel", "parallel")))(cond, w, b)


def adaln_bwd(name, cond, w, dm):
    _, D, N = w.shape
    tn = min(_ADA_TN, N)

    def body(c_ref, w_ref, dm_ref, gw_ref, ds_ref):
        first = jnp.logical_and(pl.program_id(0) == 0, pl.program_id(1) == 0)
        s = _silu(c_ref[...]).astype(BF16)
        dmb = dm_ref[...].astype(BF16)
        gw_ref[...] = dot_tn(s, dmb)
        p = dot_nt(dmb, w_ref[...].astype(BF16))

        @pl.when(first)
        def _():
            ds_ref[...] = p

        @pl.when(jnp.logical_not(first))
        def _():
            ds_ref[...] += p

    return _pc(body, name=name, grid=(2, N // tn),
               in_specs=[pl.BlockSpec((16, D), lambda l, n: (0, 0)),
                         pl.BlockSpec((None, D, tn), lambda l, n: (l, 0, n)),
                         pl.BlockSpec((None, 16, tn), lambda l, n: (l, 0, n))],
               out_specs=[pl.BlockSpec((None, D, tn), lambda l, n: (l, 0, n)),
                          pl.BlockSpec((16, D), lambda l, n: (0, 0))],
               out_shape=[jax.ShapeDtypeStruct((2, D, N), F32), jax.ShapeDtypeStruct((16, D), F32)],
               compiler_params=_cp(("arbitrary", "arbitrary")))(cond, w, dm)


def cctx_grad(name, parts, c_ctx):
    def body(p_ref, c_ref, o_ref):
        acc = p_ref[0]
        for j in range(1, N_CHIP):
            acc = acc + p_ref[j]
        o_ref[...] = acc * _dsilu(c_ref[...])

    return _pc(body, name=name, out_shape=jax.ShapeDtypeStruct(c_ctx.shape, F32))(parts, c_ctx)


def _pack(arrs, rows_mult=8):
    flat = jnp.concatenate([a.reshape(-1) for a in arrs])
    n = flat.shape[0]
    unit = rows_mult * LANE
    tot = -(-n // unit) * unit
    return jnp.concatenate([flat, jnp.zeros((tot - n,), F32)]).reshape(tot // LANE, LANE)


def _unpack(flat, shapes):
    out, o = [], 0
    for s in shapes:
        n = int(np.prod(s))
        out.append(flat[o:o + n].reshape(s))
        o += n
    return out


MOD_NAMES = ("sh1", "sc1", "g1", "sh2", "sc2", "g2")


def kernel(x, c, ctx, c_ctx, w_ada, b_ada, norm1_g, w_in, ret_decay, ret_gn_g, conv_dw_w, conv_dw_b, conv_ln_g, conv_ln_b, conv_pw, na_rpb, w_out, norm2_g, ffn_up, ffn_dw_w, ffn_dw_b, ffn_down, final_g, loss_target, m_c_ctx, m_w_ada, m_b_ada, m_norm1_g, m_w_in, m_ret_decay, m_ret_gn_g, m_conv_dw_w, m_conv_dw_b, m_conv_ln_g, m_conv_ln_b, m_conv_pw, m_na_rpb, m_w_out, m_norm2_g, m_ffn_up, m_ffn_dw_w, m_ffn_dw_b, m_ffn_down, m_final_g, v_c_ctx, v_w_ada, v_b_ada, v_norm1_g, v_w_in, v_ret_decay, v_ret_gn_g, v_conv_dw_w, v_conv_dw_b, v_conv_ln_g, v_conv_ln_b, v_conv_pw, v_na_rpb, v_w_out, v_norm2_g, v_ffn_up, v_ffn_dw_w, v_ffn_dw_b, v_ffn_down, v_final_g):
    cfg = make_cfg(D=x.shape[2], T=x.shape[1], TC=ctx.shape[1], RH=ret_decay.shape[2], CW=conv_dw_b.shape[1],
                   NH=na_rpb.shape[1], DFF=ffn_dw_b.shape[1] // 2)
    D, T = cfg.D, cfg.T
    W = dict(c_ctx=c_ctx, w_ada=w_ada, b_ada=b_ada, norm1_g=norm1_g, w_in=w_in, ret_decay=ret_decay, ret_gn_g=ret_gn_g,
             conv_dw_w=conv_dw_w, conv_dw_b=conv_dw_b, conv_ln_g=conv_ln_g, conv_ln_b=conv_ln_b, conv_pw=conv_pw,
             na_rpb=na_rpb, w_out=w_out, norm2_g=norm2_g, ffn_up=ffn_up, ffn_dw_w=ffn_dw_w, ffn_dw_b=ffn_dw_b,
             ffn_down=ffn_down, final_g=final_g)
    Mo = dict(c_ctx=m_c_ctx, w_ada=m_w_ada, b_ada=m_b_ada, norm1_g=m_norm1_g, w_in=m_w_in, ret_decay=m_ret_decay,
              ret_gn_g=m_ret_gn_g, conv_dw_w=m_conv_dw_w, conv_dw_b=m_conv_dw_b, conv_ln_g=m_conv_ln_g,
              conv_ln_b=m_conv_ln_b, conv_pw=m_conv_pw, na_rpb=m_na_rpb, w_out=m_w_out, norm2_g=m_norm2_g,
              ffn_up=m_ffn_up, ffn_dw_w=m_ffn_dw_w, ffn_dw_b=m_ffn_dw_b, ffn_down=m_ffn_down, final_g=m_final_g)
    Vo = dict(c_ctx=v_c_ctx, w_ada=v_w_ada, b_ada=v_b_ada, norm1_g=v_norm1_g, w_in=v_w_in, ret_decay=v_ret_decay,
              ret_gn_g=v_ret_gn_g, conv_dw_w=v_conv_dw_w, conv_dw_b=v_conv_dw_b, conv_ln_g=v_conv_ln_g,
              conv_ln_b=v_conv_ln_b, conv_pw=v_conv_pw, na_rpb=v_na_rpb, w_out=v_w_out, norm2_g=v_norm2_g,
              ffn_up=v_ffn_up, ffn_dw_w=v_ffn_dw_w, ffn_dw_b=v_ffn_dw_b, ffn_down=v_ffn_down, final_g=v_final_g)
    order = list(W)
    xi, yi, ci = lax.axis_index("x"), lax.axis_index("y"), lax.axis_index("c")
    chip = 2 * xi + yi
    dev = 4 * xi + 2 * yi + ci
    NA = w_ada.shape[2]
    ncw, nfw = conv_dw_w.shape[2], ffn_dw_w.shape[2]

    g_in = allgather8("ag_small_in", _pack([c[0], conv_dw_w, ffn_dw_w])).reshape(N_DEV, -1)
    c8 = g_in[:, :D]
    by_chip = g_in[0::2, D:]
    cw_parts, fw_parts = [], []
    for j in range(N_CHIP):
        a, b = _unpack(by_chip[j], [conv_dw_w.shape, ffn_dw_w.shape])
        cw_parts.append(a)
        fw_parts.append(b)
    conv_dw_w_full = jnp.concatenate(cw_parts, axis=2)
    ffn_dw_w_full = jnp.concatenate(fw_parts, axis=2)
    cond = jnp.concatenate([c8, c_ctx[None], jnp.zeros((16 - N_DEV - 1, D), F32)], axis=0)

    b_sh = lax.dynamic_slice(b_ada, (0, chip * NA), (2, NA)).reshape(2, 1, NA)
    m_sh = adaln_fwd("adaln_fwd", cond, w_ada, b_sh)
    m_all = allgather8("ag_mod", m_sh.reshape(2 * 16, NA)).reshape(N_DEV, 2, 16, NA)[0::2]
    m_all = m_all.transpose(1, 2, 0, 3).reshape(2, 16, N_CHIP * NA)
    mods = []
    for l in range(2):
        lat = lax.dynamic_slice(m_all[l], (dev, 0), (1, N_CHIP * NA))[0]
        cx = m_all[l, N_DEV]
        mods.append({nm: jnp.stack([lat[k * D:(k + 1) * D], cx[k * D:(k + 1) * D]], 0)[:, None, :]
                     for k, nm in enumerate(MOD_NAMES)})

    c_arr = jnp.reshape(ci, (1,)).astype(jnp.int32)
    chip_arr = jnp.reshape(chip, (1,)).astype(jnp.int32)
    first, rest = ("w_in", "conv_pw"), ("w_out", "ffn_up", "ffn_down")
    wb = [{nm: to_bf16_block(f"to_bf16_{nm}_{l}", W[nm], chip_arr, l) for nm in BIG} for l in range(2)]
    have = dict(zip([(0, nm) for nm in first], allgather_layer("ag_w0a", [wb[0][nm] for nm in first], 0)))
    flying_w = {}

    def start_gather(tag, l, names, after):
        s_sem, r_sem, bufs, tok = allgather_layer_start(f"ag_{tag}_start", [wb[l][nm] for nm in names], l, after)
        flying_w[(l, names[0])] = (tag, l, names, bufs, s_sem, r_sem)
        return tok[0, 0]

    def land_gather(key, after):
        tag, l, names, bufs, s_sem, r_sem = flying_w.pop(key)
        landed = allgather_layer_wait(f"ag_{tag}_wait", bufs, s_sem, r_sem, after, l)
        have.update(zip([(l, nm) for nm in names], forward_halves(f"ag_{tag}_fwd", list(landed), l)))

    mods[0] = {**mods[0], "sc1": mods[0]["sc1"] + start_gather("w0b", 0, rest, have[(0, first[0])])}

    def wts(l, name, after):
        tok = None
        if (l, name) not in have:
            if l == 0:
                land_gather((0, rest[0]), after)
                tok = start_gather("w1", 1, BIG, have[(0, rest[0])])
            else:
                land_gather((1, BIG[0]), after)
        return have[(l, name)], tok

    sp = dict(norm1_g=norm1_g, norm2_g=norm2_g, ret_decay=ret_decay, ret_gn_g=ret_gn_g, conv_dw_w=conv_dw_w_full,
              conv_dw_b=conv_dw_b, conv_ln_g=conv_ln_g, conv_ln_b=conv_ln_b, na_rpb=na_rpb, ffn_dw_w=ffn_dw_w_full,
              ffn_dw_b=ffn_dw_b, final_g=final_g)
    flights = []

    def grads_ready(l, names, gb):
        tag = f"{l}_{names[0]}"
        from_sib = exchange_rows(f"rs_exchange_{tag}", [gb[nm] for nm in names], l)
        part = [add_rows(f"rs_add_{nm}_{l}", gb[nm], r, c_arr, l) for nm, r in zip(names, from_sib)]
        lands = [own_row(f"rs_own_{nm}_{l}", p, chip_arr) for nm, p in zip(names, part)]
        if l == 0 and names == GRAD_GROUPS[-1]:
            flights.append((l, names, tag, part, lands, None))
            return None
        s_sem, r_sem, part, lands, tok = scatter_slices_start(f"rs_scatter_{tag}_start", part, lands)
        flights.append((l, names, tag, part, lands, (s_sem, r_sem)))
        return tok[0, 0]

    loss_l, gx, gb, gss, dms, dfg = local_step(cfg, x[0], ctx[0], loss_target[0], mods, wts, sp, grads_ready)
    loss = lax.psum(loss_l, ("x", "y", "c"))

    dmseg = jnp.stack([jnp.stack([jnp.concatenate([dms[l][nm][r, 0] for nm in MOD_NAMES]) for r in range(2)])
                       for l in range(2)])
    gsm = dict(
        norm1_g=jnp.stack([gss[l]["norm1_g"][0] for l in range(2)]),
        ret_decay=jnp.stack([gss[l]["lam"] * jax.nn.sigmoid(-ret_decay[l]) for l in range(2)]),
        ret_gn_g=jnp.stack([gss[l]["ret_gn_g"][0] for l in range(2)]),
        conv_dw_w=jnp.stack([gss[l]["conv_dw_w"][:cfg.CK] for l in range(2)]),
        conv_dw_b=jnp.stack([gss[l]["conv_dw_b"][0] for l in range(2)]),
        conv_ln_g=jnp.stack([gss[l]["conv_ln_g"][0] for l in range(2)]),
        conv_ln_b=jnp.stack([gss[l]["conv_ln_b"][0] for l in range(2)]),
        na_rpb=jnp.stack([gss[l]["na_rpb"] for l in range(2)]),
        norm2_g=jnp.stack([gss[l]["norm2_g"][0] for l in range(2)]),
        ffn_dw_w=jnp.stack([gss[l]["ffn_dw_w"][:, :3].transpose(1, 0, 2).reshape(3, 2 * cfg.DFF) for l in range(2)]),
        ffn_dw_b=jnp.stack([gss[l]["ffn_dw_b"].reshape(-1) for l in range(2)]),
        final_g=dfg)
    snames = list(gsm)
    sshapes = [dmseg.shape] + [gsm[nm].shape for nm in snames]
    packed = _pack([dmseg] + [gsm[nm] for nm in snames])
    g_all = allgather8("ag_small_grads", packed).reshape(N_DEV, packed.shape[0], LANE)
    summed = sum_leading("sum_small_grads", g_all).reshape(-1)
    dm_sum, *gsum = _unpack(summed, sshapes)
    gfull = dict(zip(snames, gsum))
    ndm = int(np.prod(dmseg.shape))
    dm_all = g_all.reshape(N_DEV, -1)[:, :ndm].reshape(N_DEV, 2, 2, 6 * D)
    gfull["b_ada"] = sum_leading("sum_b_ada", dm_all.transpose(0, 2, 1, 3).reshape(2 * N_DEV, 2 * 6 * D // LANE, LANE)
                                 ).reshape(2, 6 * D)

    dm16 = jnp.concatenate([dm_all[:, :, 0].transpose(1, 0, 2), dm_sum[:, 1][:, None],
                            jnp.zeros((2, 16 - N_DEV - 1, 6 * D), F32)], axis=1)
    dm16 = lax.dynamic_slice(dm16, (0, 0, chip * NA), (2, 16, NA))
    gfull["w_ada"], ds16 = adaln_bwd("adaln_bwd", cond, w_ada, dm16)
    ds_all = allgather8("ag_dsilu", ds16[8:16]).reshape(N_DEV, 8, D)[0::2, 0:1]
    gfull["c_ctx"] = cctx_grad("cctx_grad", ds_all, c_ctx[None])[0]
    gfull["conv_dw_w"] = lax.dynamic_slice(gfull["conv_dw_w"], (0, 0, chip * ncw), (2, cfg.CK, ncw))
    gfull["ffn_dw_w"] = lax.dynamic_slice(gfull["ffn_dw_w"], (0, 0, chip * nfw), (2, 3, nfw))

    fin = {}
    for l, names, tag, part, lands, sems in flights:
        if sems is None:
            landed = scatter_slices(f"rs_scatter_{tag}", part, lands)
        else:
            landed = scatter_slices_wait(f"rs_scatter_{tag}_wait", part, lands, *sems, gx)
        for nm, p in zip(names, landed):
            fin[nm] = sum_rows_into(f"rs_sum_{nm}_{l}", p, c_arr, l, fin.get(nm))
    for nm, gfin in zip(BIG, share_rows("rs_share", [fin[nm] for nm in BIG])):
        gfull[nm] = gfin.reshape(W[nm].shape)

    delta, new_m, new_v = {}, {}, {}
    bigs = ("w_ada",) + BIG
    for nm in bigs:
        shp = W[nm].shape
        v2 = lambda a: a.reshape(-1, shp[-1])
        d_, m_, v_ = adamw(f"adamw_{nm}", v2(W[nm]), v2(gfull[nm]), v2(Mo[nm]), v2(Vo[nm]))
        delta[nm], new_m[nm], new_v[nm] = d_.reshape(shp), m_.reshape(shp), v_.reshape(shp)
    smalls = [nm for nm in order if nm not in bigs]
    shapes = [W[nm].shape for nm in smalls]
    d_, m_, v_ = adamw("adamw_small", _pack([W[nm] for nm in smalls]), _pack([gfull[nm] for nm in smalls]),
                       _pack([Mo[nm] for nm in smalls]), _pack([Vo[nm] for nm in smalls]))
    for nm, a, b, e in zip(smalls, _unpack(d_.reshape(-1), shapes), _unpack(m_.reshape(-1), shapes),
                           _unpack(v_.reshape(-1), shapes)):
        delta[nm], new_m[nm], new_v[nm] = a, b, e
    return (loss, gx[None], *[gfull[nm] for nm in order], *[delta[nm] for nm in order],
            *[new_m[nm] for nm in order], *[new_v[nm] for nm in order])
```

```python
import collections
import functools

import numpy as np
import jax
import jax.numpy as jnp
from jax import lax
from jax.experimental import pallas as pl
from jax.experimental.pallas import tpu as pltpu

F32 = jnp.float32
BF16 = jnp.bfloat16
EPS = 1e-6
ROPE_BASE = 10000.0
NEG = -1e30
LANE = 128
VMEM_LIMIT = 56 * 1024 * 1024

ADAM_LR, ADAM_B1, ADAM_B2, ADAM_EPS, ADAM_WD, ADAM_STEP = 0.001, 0.9, 0.999, 1e-08, 0.01, 10

Cfg = collections.namedtuple(
    "Cfg", "D T TC GW RH RDK RDV CW CK NH NDH NAR NAC DFF TB")


def make_cfg(D=2048, T=4096, TC=256, RH=4, CW=512, NH=4, DFF=5632):
    return Cfg(D=D, T=T, TC=TC, GW=64, RH=RH, RDK=128, RDV=256, CW=CW, CK=31, NH=NH, NDH=128,
               NAR=8, NAC=16, DFF=DFF, TB=256)


def _offsets(cfg):
    sizes = [cfg.RH * cfg.RDK, cfg.RH * cfg.RDK, cfg.RH * cfg.RDV, cfg.RH * cfg.RDV, cfg.CW, cfg.CW,
             cfg.NH * cfg.NDH, cfg.NH * cfg.NDH, cfg.NH * cfg.NDH]
    offs = [0]
    for s in sizes:
        offs.append(offs[-1] + s)
    return dict(zip(["lq", "lk", "lv", "lg", "la", "lb", "nq", "nk", "nv", "end"], offs))


def _pc(body, **kw):
    return pl.pallas_call(body, **kw)


def _cp(sem=None):
    return pltpu.CompilerParams(dimension_semantics=sem, vmem_limit_bytes=VMEM_LIMIT)


def _dot(a, b, ca, cb):
    return lax.dot_general(a, b, (((ca,), (cb,)), ((), ())), preferred_element_type=F32)


def dot_nn(a, b):
    return _dot(a, b, 1, 0)


def dot_nt(a, b):
    return _dot(a, b, 1, 1)


def dot_tn(a, b):
    return _dot(a, b, 0, 0)


def _sigmoid(x):
    return 1.0 / (1.0 + jnp.exp(-x))


def _silu(x):
    return x * _sigmoid(x)


def _dsilu(x):
    s = _sigmoid(x)
    return s * (1.0 + x * (1.0 - s))


def matmul(name, a, b, *, contract, grid, a_spec, b_spec, out_shape, out_spec, nk, into=None):
    dot = {"nn": dot_nn, "nt": dot_nt, "tn": dot_tn}[contract]
    direct = nk > 1 and out_shape.dtype == F32
    kax = len(grid) - 1

    def body(a_ref, b_ref, *rest):
        o_ref, *scr = rest[1:] if into is not None else rest
        p = dot(a_ref[...].astype(BF16), b_ref[...].astype(BF16))
        if nk == 1:
            o_ref[...] = p.astype(o_ref.dtype)
            return
        acc = o_ref if direct else scr[0]
        k = pl.program_id(kax)

        @pl.when(k == 0)
        def _():
            acc[...] = p

        @pl.when(k > 0)
        def _():
            acc[...] += p

        if not direct:
            @pl.when(k == nk - 1)
            def _():
                o_ref[...] = acc[...].astype(o_ref.dtype)

    scratch = []
    if nk > 1 and not direct:
        blk = [s for s in out_spec.block_shape if s is not None]
        scratch = [pltpu.VMEM(tuple(blk), F32)]
    sem = ("parallel",) * kax + (("arbitrary",) if nk > 1 else ("parallel",))
    in_specs, args, alias = [a_spec, b_spec], (a, b), {}
    if into is not None:
        in_specs, args, alias = in_specs + [pl.BlockSpec(memory_space=pl.ANY)], (a, b, into), {2: 0}
    return _pc(body, name=name, grid=grid, in_specs=in_specs, out_specs=out_spec, out_shape=out_shape,
               scratch_shapes=scratch, input_output_aliases=alias, compiler_params=_cp(sem))(*args)


_WG_ROWS = 1024


def wgrad(cfg, name, a, dc, a_spec, dc_spec, out_shape, out_spec, ntiles, into):
    T, TC = cfg.T, cfg.TC
    tml = min(_WG_ROWS, T)
    nl = T // tml

    def body(al_ref, ac_ref, dl_ref, dcx_ref, *rest):
        o_ref, acc = rest[-2:]
        m = pl.program_id(1)

        @pl.when(m == 0)
        def _():
            acc[...] = dot_tn(al_ref[...], dl_ref[...])

        @pl.when(jnp.logical_and(m > 0, m < nl))
        def _():
            acc[...] += dot_tn(al_ref[...], dl_ref[...])

        @pl.when(m == nl)
        def _():
            o_ref[...] = (acc[...] + dot_tn(ac_ref[...], dcx_ref[...])).astype(o_ref.dtype)

    lat = lambda m: jnp.minimum(m, nl - 1)
    ctx = lambda m: T // TC
    in_specs = [a_spec(tml, lat), a_spec(TC, ctx), dc_spec(tml, lat), dc_spec(TC, ctx)]
    args, alias = (a, a, dc, dc), {}
    if into is not None:
        in_specs, args, alias = in_specs + [pl.BlockSpec(memory_space=pl.ANY)], args + (into,), {4: 0}
    blk = tuple(s for s in out_spec.block_shape if s is not None)
    return _pc(body, name=name, grid=(ntiles, nl + 1), in_specs=in_specs, out_specs=out_spec, out_shape=out_shape,
               scratch_shapes=[pltpu.VMEM(blk, F32)], input_output_aliases=alias,
               compiler_params=_cp(("parallel", "arbitrary")))(*args)


def mm_rowsharded(name, a, w4, l, out_dtype, tn):
    L, K = a.shape
    nch, _, Kb, N = w4.shape
    tm = 256

    def body(a_ref, w_ref, o_ref):
        acc = dot_nn(a_ref[:, 0:Kb], w_ref[0])
        for j in range(1, nch):
            acc += dot_nn(a_ref[:, j * Kb:(j + 1) * Kb], w_ref[j])
        o_ref[...] = acc.astype(o_ref.dtype)

    return _pc(body, name=name, grid=(N // tn, L // tm),
               in_specs=[pl.BlockSpec((tm, K), lambda n, m: (m, 0)),
                         pl.BlockSpec((nch, None, Kb, tn), lambda n, m: (0, l, 0, n))],
               out_specs=pl.BlockSpec((tm, tn), lambda n, m: (m, n)),
               out_shape=jax.ShapeDtypeStruct((L, N), out_dtype),
               compiler_params=_cp(("parallel", "parallel")))(a, w4)


def _region(cfg):
    nlat = cfg.T // cfg.TB
    return lambda i: jnp.minimum(i // nlat, 1)


def norm_mod_fwd(cfg, name, x, ng, sc, sh):
    L, D = x.shape
    TB = cfg.TB
    reg = _region(cfg)

    def body(x_ref, ng_ref, sc_ref, sh_ref, h_ref):
        xv = x_ref[...]
        r = lax.rsqrt(jnp.mean(xv * xv, axis=-1, keepdims=True) + EPS)
        n = xv * r * ng_ref[...]
        h_ref[...] = (n * (1.0 + sc_ref[...]) + sh_ref[...]).astype(h_ref.dtype)

    row = pl.BlockSpec((TB, D), lambda i: (i, 0))
    vec = pl.BlockSpec((1, D), lambda i: (0, 0))
    rvec = pl.BlockSpec((None, 1, D), lambda i: (reg(i), 0, 0))
    return _pc(body, name=name, grid=(L // TB,), in_specs=[row, vec, rvec, rvec], out_specs=row,
               out_shape=jax.ShapeDtypeStruct((L, D), BF16), compiler_params=_cp(("parallel",)))(x, ng, sc, sh)


def norm_mod_bwd(cfg, name, dh, x, ng, sc, dx_in, latent_only=False):
    L, D = x.shape
    TB = cfg.TB
    nlat = cfg.T // TB
    reg = _region(cfg)

    def body(dh_ref, x_ref, ng_ref, sc_ref, dxi_ref, dx_ref, dsc_ref, dsh_ref, dng_ref):
        i = pl.program_id(0)
        xv = x_ref[...]
        r = lax.rsqrt(jnp.mean(xv * xv, axis=-1, keepdims=True) + EPS)
        xh = xv * r
        g = ng_ref[...]
        n = xh * g
        dh = dh_ref[...]
        dn = dh * (1.0 + sc_ref[...])
        dxh = dn * g
        dx = r * (dxh - xh * jnp.mean(dxh * xh, axis=-1, keepdims=True))
        if latent_only:
            @pl.when(i < nlat)
            def _():
                dx_ref[...] = dxi_ref[...] + dx
        else:
            dx_ref[...] = dxi_ref[...] + dx
        s_sh = jnp.sum(dh, axis=0, keepdims=True)
        s_sc = jnp.sum(dh * n, axis=0, keepdims=True)
        s_ng = jnp.sum(dn * xh, axis=0, keepdims=True)
        first = jnp.logical_or(i == 0, i == nlat)

        @pl.when(first)
        def _():
            dsh_ref[...] = s_sh
            dsc_ref[...] = s_sc

        @pl.when(jnp.logical_not(first))
        def _():
            dsh_ref[...] += s_sh
            dsc_ref[...] += s_sc

        @pl.when(i == 0)
        def _():
            dng_ref[...] = s_ng

        @pl.when(i > 0)
        def _():
            dng_ref[...] += s_ng

    row = pl.BlockSpec((TB, D), lambda i: (i, 0))
    vec = pl.BlockSpec((1, D), lambda i: (0, 0))
    rvec = pl.BlockSpec((None, 1, D), lambda i: (reg(i), 0, 0))
    dxs = pl.BlockSpec((TB, D), lambda i: (jnp.minimum(i, nlat - 1), 0)) if latent_only else row
    return _pc(body, name=name, grid=(L // TB,), in_specs=[row, row, vec, rvec, row],
               out_specs=[dxs, rvec, rvec, vec],
               out_shape=[jax.ShapeDtypeStruct((cfg.T if latent_only else L, D), F32),
                          jax.ShapeDtypeStruct((2, 1, D), F32),
                          jax.ShapeDtypeStruct((2, 1, D), F32), jax.ShapeDtypeStruct((1, D), F32)],
               compiler_params=_cp(("arbitrary",)))(dh, x, ng, sc, dx_in)


def resid_fwd(cfg, name, x, y, g):
    L, D = x.shape
    TB = cfg.TB
    reg = _region(cfg)

    def body(x_ref, y_ref, g_ref, o_ref):
        o_ref[...] = x_ref[...] + g_ref[...] * y_ref[...]

    row = pl.BlockSpec((TB, D), lambda i: (i, 0))
    rvec = pl.BlockSpec((None, 1, D), lambda i: (reg(i), 0, 0))
    return _pc(body, name=name, grid=(L // TB,), in_specs=[row, row, rvec], out_specs=row,
               out_shape=jax.ShapeDtypeStruct((L, D), F32), compiler_params=_cp(("parallel",)))(x, y, g)


def resid_bwd(cfg, name, dxo, y, g):
    L, D = y.shape
    TB = cfg.TB
    nlat = cfg.T // TB
    reg = _region(cfg)

    def body(d_ref, y_ref, g_ref, dy_ref, dg_ref):
        i = pl.program_id(0)
        d = d_ref[...]
        dy_ref[...] = (d * g_ref[...]).astype(dy_ref.dtype)
        s = jnp.sum(d * y_ref[...], axis=0, keepdims=True)
        first = jnp.logical_or(i == 0, i == nlat)

        @pl.when(first)
        def _():
            dg_ref[...] = s

        @pl.when(jnp.logical_not(first))
        def _():
            dg_ref[...] += s

    row = pl.BlockSpec((TB, D), lambda i: (i, 0))
    rvec = pl.BlockSpec((None, 1, D), lambda i: (reg(i), 0, 0))
    return _pc(body, name=name, grid=(L // TB,), in_specs=[row, row, rvec], out_specs=[row, rvec],
               out_shape=[jax.ShapeDtypeStruct((L, D), BF16), jax.ShapeDtypeStruct((2, 1, D), F32)],
               compiler_params=_cp(("arbitrary",)))(dxo, y, g)


def final_loss(cfg, name, x, fg, tgt):
    L, D = x.shape
    TB = cfg.TB
    nlat = cfg.T // TB

    def body(x_ref, fg_ref, t_ref, ls_ref, dx_ref, dg_ref):
        i = pl.program_id(0)

        @pl.when(i == 0)
        def _():
            ls_ref[...] = jnp.zeros_like(ls_ref)
            dg_ref[...] = jnp.zeros_like(dg_ref)

        @pl.when(i < nlat)
        def _():
            xv = x_ref[...]
            r = lax.rsqrt(jnp.mean(xv * xv, axis=-1, keepdims=True) + EPS)
            xh = xv * r
            g = fg_ref[...]
            e = xh * g - t_ref[...]
            ls_ref[...] += 0.5 * jnp.sum(e * e) / D
            dy = e / D
            dg_ref[...] += jnp.sum(dy * xh, axis=0, keepdims=True)
            dxh = dy * g
            dx_ref[...] = r * (dxh - xh * jnp.mean(dxh * xh, axis=-1, keepdims=True))

        @pl.when(i >= nlat)
        def _():
            dx_ref[...] = jnp.zeros_like(dx_ref)

    row = pl.BlockSpec((TB, D), lambda i: (i, 0))
    trow = pl.BlockSpec((TB, D), lambda i: (jnp.minimum(i, nlat - 1), 0))
    vec = pl.BlockSpec((1, D), lambda i: (0, 0))
    return _pc(body, name=name, grid=(L // TB,), in_specs=[row, vec, trow],
               out_specs=[pl.BlockSpec((1, LANE), lambda i: (0, 0)), row, vec],
               out_shape=[jax.ShapeDtypeStruct((1, LANE), F32), jax.ShapeDtypeStruct((L, D), F32),
                          jax.ShapeDtypeStruct((1, D), F32)],
               compiler_params=_cp(("arbitrary",)))(x, fg, tgt)


def rope_tables(cfg):
    half = cfg.RDK // 2
    nf = half // 2
    pos = np.arange(cfg.T)
    row = (pos // cfg.GW).astype(np.float32)
    col = (pos % cfg.GW).astype(np.float32)
    inv = jnp.asarray(ROPE_BASE, F32) ** (-jnp.arange(nf, dtype=F32) / nf)
    ar = jnp.asarray(row)[:, None] * inv[None, :]
    ac = jnp.asarray(col)[:, None] * inv[None, :]
    cos = jnp.concatenate([jnp.cos(ar), jnp.cos(ar), jnp.cos(ac), jnp.cos(ac)], axis=1)
    sin = jnp.concatenate([-jnp.sin(ar), jnp.sin(ar), -jnp.sin(ac), jnp.sin(ac)], axis=1)
    cos = jnp.concatenate([cos, jnp.ones((cfg.TC, cfg.RDK), F32)], axis=0)
    sin = jnp.concatenate([sin, jnp.zeros((cfg.TC, cfg.RDK), F32)], axis=0)
    return cos, sin


def _rb(cfg):
    rb = (cfg.T + cfg.TC) // 4
    assert rb % 16 == 0
    return rb


def _swap32(t):
    lane = lax.broadcasted_iota(jnp.int32, t.shape, 1)
    return jnp.where((lane % 64) < 32, pltpu.roll(t, 96, 1), pltpu.roll(t, 32, 1))


def rope_fwd(cfg, name, P, cos, sin):
    L = P.shape[0]
    TB = _rb(cfg)
    off = _offsets(cfg)
    cq, ck = off["lq"] // LANE, off["lk"] // LANE
    scale = cfg.RDK ** -0.5

    def body(q_ref, k_ref, c_ref, s_ref, qo_ref, ko_ref):
        c = c_ref[...]
        s = s_ref[...]
        q = q_ref[...]
        k = k_ref[...]
        qo_ref[...] = (q * c + _swap32(q) * s) * scale
        ko_ref[...] = k * c + _swap32(k) * s

    tab = pl.BlockSpec((TB, LANE), lambda i, h: (i, 0))
    out = pl.BlockSpec((TB, LANE), lambda i, h: (i, h))
    shp = jax.ShapeDtypeStruct((L, cfg.RH * cfg.RDK), F32)
    return _pc(body, name=name, grid=(L // TB, cfg.RH),
               in_specs=[pl.BlockSpec((TB, LANE), lambda i, h: (i, cq + h)),
                         pl.BlockSpec((TB, LANE), lambda i, h: (i, ck + h)), tab, tab],
               out_specs=[out, out], out_shape=[shp, shp],
               compiler_params=_cp(("parallel", "parallel")))(P, P, cos, sin)


def rope_bwd(cfg, name, dq2, dk2, cos, sin):
    L, W = dq2[0].shape
    TB = _rb(cfg)
    scale = cfg.RDK ** -0.5

    def body(dqf_ref, dqb_ref, dkf_ref, dkb_ref, c_ref, s_ref, qo_ref, ko_ref):
        c = c_ref[...]
        s = s_ref[...]
        dq = dqf_ref[...] + dqb_ref[...]
        dk = dkf_ref[...] + dkb_ref[...]
        qo_ref[...] = ((dq * c - _swap32(dq) * s) * scale).astype(qo_ref.dtype)
        ko_ref[...] = (dk * c - _swap32(dk) * s).astype(ko_ref.dtype)

    tab = pl.BlockSpec((TB, LANE), lambda i, h: (i, 0))
    blk = pl.BlockSpec((TB, LANE), lambda i, h: (i, h))
    shp = jax.ShapeDtypeStruct((L, W), BF16)
    return _pc(body, name=name, grid=(L // TB, cfg.RH), in_specs=[blk, blk, blk, blk, tab, tab],
               out_specs=[blk, blk], out_shape=[shp, shp],
               compiler_params=_cp(("parallel", "parallel")))(*dq2, *dk2, cos, sin)


def _ret_chunk_map(cfg):
    C = cfg.RDK
    n = (cfg.T + cfg.TC) // C
    nlat, nctx = cfg.T // C, cfg.TC // C

    def chunk(d, s):
        if d == 0:
            return jnp.where(s < nctx, nlat + s, s - nctx)
        return n - 1 - s

    return n, chunk


def _ret_decay_terms(d, lam, C):
    ii = lax.broadcasted_iota(jnp.int32, (C, C), 0)
    jj = lax.broadcasted_iota(jnp.int32, (C, C), 1)
    diff = (ii - jj if d == 0 else jj - ii).astype(F32)
    dpos = jnp.maximum(diff, 0.0)
    Dm = jnp.where(diff >= 0, jnp.exp(dpos * lam), 0.0)
    ic = lax.broadcasted_iota(jnp.int32, (C, 1), 0).astype(F32)
    cxi = ic + 1.0 if d == 0 else C - ic
    cze = C - 1.0 - ic if d == 0 else ic
    xi = jnp.exp(cxi * lam)
    ze = jnp.exp(cze * lam)
    g = jnp.exp(jnp.full((1, 1), C, F32) * lam)
    return dpos, Dm, cxi, cze, xi, ze, g


def retention_fwd(cfg, name, qr, kr, P, lam):
    L = P.shape[0]
    C, DV, RH = cfg.RDK, cfg.RDV, cfg.RH
    n, chunk = _ret_chunk_map(cfg)

    def body(lam_ref, qf_ref, qb_ref, kf_ref, kb_ref, vf_ref, vb_ref, of_ref, ob_ref, st_ref, S):
        s = pl.program_id(0)

        @pl.when(s == 0)
        def _():
            S[...] = jnp.zeros_like(S)

        for d, (q_ref, k_ref, v_ref, o_ref) in enumerate(((qf_ref, kf_ref, vf_ref, of_ref),
                                                          (qb_ref, kb_ref, vb_ref, ob_ref))):
            for h in range(RH):
                _, Dm, _, _, xi, ze, g = _ret_decay_terms(d, lam_ref[d, h], C)
                k = k_ref[:, h * C:(h + 1) * C]
                qb = q_ref[:, h * C:(h + 1) * C].astype(BF16)
                kb = k.astype(BF16)
                vb = v_ref[:, h * DV:(h + 1) * DV].astype(BF16)
                Sv = S[d, h]
                st_ref[d, h] = Sv
                A = dot_nt(qb, kb) * Dm
                o_ref[:, h * DV:(h + 1) * DV] = dot_nn(A.astype(BF16), vb) + dot_nn(qb, Sv.astype(BF16)) * xi
                S[d, h] = Sv * g + dot_tn((k * ze).astype(BF16), vb)

    def spec(w, col, d):
        return pl.BlockSpec((C, w), lambda s: (chunk(d, s), col))

    W, WV = RH * C, RH * DV
    return _pc(body, name=name, grid=(n,),
               in_specs=[pl.BlockSpec(memory_space=pltpu.SMEM), spec(W, 0, 0), spec(W, 0, 1), spec(W, 0, 0),
                         spec(W, 0, 1), spec(WV, 1, 0), spec(WV, 1, 1)],
               out_specs=[spec(WV, 0, 0), spec(WV, 0, 1),
                          pl.BlockSpec((2, RH, None, C, DV), lambda s: (0, 0, s, 0, 0))],
               out_shape=[jax.ShapeDtypeStruct((L, WV), F32), jax.ShapeDtypeStruct((L, WV), F32),
                          jax.ShapeDtypeStruct((2, RH, n, C, DV), F32)],
               scratch_shapes=[pltpu.VMEM((2, RH, C, DV), F32)],
               compiler_params=_cp(("arbitrary",)))(lam, qr, qr, kr, kr, P, P)


def retention_bwd(cfg, name, qr, kr, P, lam, st, do):
    L = P.shape[0]
    C, DV, RH = cfg.RDK, cfg.RDV, cfg.RH
    n, chunk = _ret_chunk_map(cfg)

    def body(lam_ref, qf_ref, qb_ref, kf_ref, kb_ref, vf_ref, vb_ref, st_ref, dof_ref, dob_ref,
             dqf_ref, dqb_ref, dkf_ref, dkb_ref, dvf_ref, dvb_ref, dl_ref, dS):
        si = pl.program_id(0)

        @pl.when(si == 0)
        def _():
            dS[...] = jnp.zeros_like(dS)
            dl_ref[...] = jnp.zeros_like(dl_ref)

        dirs = ((qf_ref, kf_ref, vf_ref, dof_ref, dqf_ref, dkf_ref, dvf_ref),
                (qb_ref, kb_ref, vb_ref, dob_ref, dqb_ref, dkb_ref, dvb_ref))
        for d, (q_ref, k_ref, v_ref, do_ref, dq_ref, dk_ref, dv_ref) in enumerate(dirs):
            for h in range(RH):
                dpos, Dm, cxi, cze, xi, ze, g = _ret_decay_terms(d, lam_ref[d, h], C)
                hk = slice(h * C, (h + 1) * C)
                hv = slice(h * DV, (h + 1) * DV)
                k = k_ref[:, hk]
                do = do_ref[:, hv]
                qb = q_ref[:, hk].astype(BF16)
                kb = k.astype(BF16)
                vb = v_ref[:, hv].astype(BF16)
                dob = do.astype(BF16)
                Sn = st_ref[d, h]
                Snb = Sn.astype(BF16)
                dSn = dS[d, h]
                dSb = dSn.astype(BF16)
                A = dot_nt(qb, kb) * Dm
                dA = dot_nt(dob, vb)
                dQK = (dA * Dm).astype(BF16)
                kzb = (k * ze).astype(BF16)
                dv_ref[:, hv] = dot_tn(A.astype(BF16), dob) + dot_nn(kzb, dSb)
                dkz = dot_nt(vb, dSb)
                doxb = (do * xi).astype(BF16)
                dq_ref[:, hk] = dot_nn(dQK, kb) + dot_nt(doxb, Snb)
                dk_ref[:, hk] = dot_tn(dQK, qb) + dkz * ze
                QS = dot_nn(qb, Snb)
                t = (jnp.sum(dA * A * dpos) + jnp.sum(do * QS * (cxi * xi)) + jnp.sum(k * dkz * (cze * ze)))
                t4 = jnp.sum(dSn * Sn, axis=0, keepdims=True)
                t4 = jnp.sum(t4 * (g * C), axis=1, keepdims=True)
                dl_ref[d, h] += t + t4
                dS[d, h] = g * dSn + dot_tn(qb, doxb)

    def spec(w, col, d):
        return pl.BlockSpec((C, w), lambda si: (chunk(d, n - 1 - si), col))

    W, WV = RH * C, RH * DV
    return _pc(body, name=name, grid=(n,),
               in_specs=[pl.BlockSpec(memory_space=pltpu.SMEM), spec(W, 0, 0), spec(W, 0, 1), spec(W, 0, 0),
                         spec(W, 0, 1), spec(WV, 1, 0), spec(WV, 1, 1),
                         pl.BlockSpec((2, RH, None, C, DV), lambda si: (0, 0, n - 1 - si, 0, 0)),
                         spec(WV, 0, 0), spec(WV, 0, 1)],
               out_specs=[spec(W, 0, 0), spec(W, 0, 1), spec(W, 0, 0), spec(W, 0, 1), spec(WV, 0, 0), spec(WV, 0, 1),
                          pl.BlockSpec((2, RH, 8, LANE), lambda si: (0, 0, 0, 0))],
               out_shape=[jax.ShapeDtypeStruct((L, W), F32)] * 4 + [jax.ShapeDtypeStruct((L, WV), F32)] * 2
               + [jax.ShapeDtypeStruct((2, RH, 8, LANE), F32)],
               scratch_shapes=[pltpu.VMEM((2, RH, C, DV), F32)],
               compiler_params=_cp(("arbitrary",)))(lam, qr, qr, kr, kr, P, P, st, do, do)


def add_cast(cfg, name, a, b):
    L, W = a.shape
    TB = _rb(cfg)

    def body(a_ref, b_ref, o_ref):
        o_ref[...] = (a_ref[...] + b_ref[...]).astype(o_ref.dtype)

    spec = pl.BlockSpec((TB, W), lambda i: (i, 0))
    return _pc(body, name=name, grid=(L // TB,), in_specs=[spec, spec], out_specs=spec,
               out_shape=jax.ShapeDtypeStruct((L, W), BF16), compiler_params=_cp(("parallel",)))(a, b)


def ggn_fwd(cfg, name, o2, P, gn_g):
    L = P.shape[0]
    TB, DV, RH = _rb(cfg), cfg.RDV, cfg.RH
    gc0 = _offsets(cfg)["lg"] // DV

    def body(of_ref, ob_ref, gate_ref, g_ref, out_ref):
        o = of_ref[...] + ob_ref[...]
        mu = jnp.mean(o, axis=-1, keepdims=True)
        xc = o - mu
        var = jnp.mean(xc * xc, axis=-1, keepdims=True)
        y = xc * lax.rsqrt(var + EPS) * g_ref[...]
        out_ref[...] = (y * _silu(gate_ref[...])).astype(out_ref.dtype)

    blk = pl.BlockSpec((TB, DV), lambda i, h: (i, h))
    return _pc(body, name=name, grid=(L // TB, RH),
               in_specs=[blk, blk, pl.BlockSpec((TB, DV), lambda i, h: (i, gc0 + h)),
                         pl.BlockSpec((1, DV), lambda i, h: (0, h))],
               out_specs=blk, out_shape=jax.ShapeDtypeStruct((L, RH * DV), BF16),
               compiler_params=_cp(("parallel", "parallel")))(*o2, P, gn_g)


def ggn_bwd(cfg, name, dout, o2, P, gn_g, col0):
    L = P.shape[0]
    TB, DV, RH = _rb(cfg), cfg.RDV, cfg.RH
    gc0 = _offsets(cfg)["lg"] // DV

    def body(d_ref, of_ref, ob_ref, gate_ref, g_ref, do_ref, dgate_ref, dg_ref):
        i = pl.program_id(1)
        o = of_ref[...] + ob_ref[...]
        mu = jnp.mean(o, axis=-1, keepdims=True)
        xc = o - mu
        var = jnp.mean(xc * xc, axis=-1, keepdims=True)
        r = lax.rsqrt(var + EPS)
        y = xc * r
        g = g_ref[...]
        gate = gate_ref[...]
        d = d_ref[...]
        dgate_ref[...] = (d * (y * g) * _dsilu(gate)).astype(dgate_ref.dtype)
        dyg = d * _silu(gate)
        s = jnp.sum(dyg * y, axis=0, keepdims=True)

        @pl.when(i == 0)
        def _():
            dg_ref[...] = s

        @pl.when(i > 0)
        def _():
            dg_ref[...] += s

        dy = dyg * g
        do_ref[...] = r * (dy - jnp.mean(dy, axis=-1, keepdims=True)
                           - y * jnp.mean(dy * y, axis=-1, keepdims=True))

    blk = pl.BlockSpec((TB, DV), lambda h, i: (i, h))
    return _pc(body, name=name, grid=(RH, L // TB),
               in_specs=[pl.BlockSpec((TB, DV), lambda h, i: (i, col0 + h)), blk, blk,
                         pl.BlockSpec((TB, DV), lambda h, i: (i, gc0 + h)),
                         pl.BlockSpec((1, DV), lambda h, i: (0, h))],
               out_specs=[blk, blk, pl.BlockSpec((1, DV), lambda h, i: (0, h))],
               out_shape=[jax.ShapeDtypeStruct((L, RH * DV), F32), jax.ShapeDtypeStruct((L, RH * DV), BF16),
                          jax.ShapeDtypeStruct((1, RH * DV), F32)],
               compiler_params=_cp(("parallel", "arbitrary")))(dout, *o2, P, gn_g)


def cast_cols(cfg, name, src, col0, ncols, width):
    L = src.shape[0]
    TB = _rb(cfg)

    def body(s_ref, o_ref):
        o_ref[...] = s_ref[...].astype(o_ref.dtype)

    spec = pl.BlockSpec((TB, width), lambda i, j: (i, col0 + j))
    return _pc(body, name=name, grid=(L // TB, ncols), in_specs=[spec],
               out_specs=pl.BlockSpec((TB, width), lambda i, j: (i, j)),
               out_shape=jax.ShapeDtypeStruct((L, ncols * width), BF16),
               compiler_params=_cp(("parallel", "parallel")))(src)


_CPAD = 16


def _conv_windows(cfg):
    T, TC, TB = cfg.T, cfg.TC, cfg.TB
    assert TC % TB == 0 and T % TB == 0 and cfg.CK // 2 < _CPAD
    return T // TB, [(T + j * TB, T + _CPAD + j * TB) for j in range(TC // TB)]


def _fill_padded(cfg, pb, get):
    T, TC, TB = cfg.T, cfg.TC, cfg.TB
    z = jnp.zeros((_CPAD, LANE), F32)
    pb[0:_CPAD, :] = z
    pb[_CPAD + T:2 * _CPAD + T, :] = z
    pb[2 * _CPAD + T + TC:3 * _CPAD + T + TC, :] = z

    def fill(i, c):
        r0 = pl.multiple_of(i * TB, TB)
        pb[pl.ds(r0 + _CPAD, TB), :] = get(r0)
        return c

    lax.fori_loop(0, T // TB, fill, 0)
    for j in range(TC // TB):
        pb[2 * _CPAD + T + j * TB:2 * _CPAD + T + (j + 1) * TB, :] = get(T + j * TB)


def _taps(win, TB):
    W = TB + 2 * _CPAD
    rot = {0: win}

    def tap(k):
        a, b = divmod(k + 1, 8)
        if b not in rot:
            rot[b] = pltpu.roll(win, W - b, 0)
        return rot[b][8 * a:8 * a + TB, :]

    return tap


def glu_dwconv_fwd(cfg, name, P, w, b):
    L = P.shape[0]
    T, TC, TB, K = cfg.T, cfg.TC, cfg.TB, cfg.CK
    off = _offsets(cfg)
    ca, cb = off["la"] // LANE, off["lb"] // LANE
    nlat, ctx_tiles = _conv_windows(cfg)
    PBL = 3 * _CPAD + T + TC

    def body(a_ref, b_ref, w_ref, bias_ref, y_ref, pb):
        _fill_padded(cfg, pb, lambda r0: a_ref[pl.ds(r0, TB), :] * _sigmoid(b_ref[pl.ds(r0, TB), :]))
        wv = w_ref[...]
        bias = bias_ref[...]

        def tile(win):
            tap = _taps(win, TB)
            acc = jnp.zeros((TB, LANE), F32) + bias
            for k in range(K):
                acc = acc + wv[k:k + 1, :] * tap(k)
            return acc

        def lat(i, c):
            r0 = pl.multiple_of(i * TB, TB)
            y_ref[pl.ds(r0, TB), :] = tile(pb[pl.ds(r0, TB + 2 * _CPAD), :])
            return c

        lax.fori_loop(0, nlat, lat, 0)
        for r0, w0 in ctx_tiles:
            y_ref[r0:r0 + TB, :] = tile(pb[w0:w0 + TB + 2 * _CPAD, :])

    return _pc(body, name=name, grid=(cfg.CW // LANE,),
               in_specs=[pl.BlockSpec((L, LANE), lambda j: (0, ca + j)),
                         pl.BlockSpec((L, LANE), lambda j: (0, cb + j)),
                         pl.BlockSpec((32, LANE), lambda j: (0, j)),
                         pl.BlockSpec((1, LANE), lambda j: (0, j))],
               out_specs=pl.BlockSpec((L, LANE), lambda j: (0, j)),
               out_shape=jax.ShapeDtypeStruct((L, cfg.CW), F32),
               scratch_shapes=[pltpu.VMEM((PBL, LANE), F32)],
               compiler_params=_cp(("parallel",)))(P, P, w, b)


def glu_dwconv_bwd(cfg, name, P, w, dy):
    L = P.shape[0]
    T, TC, TB, K = cfg.T, cfg.TC, cfg.TB, cfg.CK
    off = _offsets(cfg)
    ca, cb = off["la"] // LANE, off["lb"] // LANE
    nlat, ctx_tiles = _conv_windows(cfg)
    PBL = 3 * _CPAD + T + TC

    def body(a_ref, b_ref, w_ref, dy_ref, da_ref, db_ref, dw_ref, dbias_ref, pbu, pbd):
        _fill_padded(cfg, pbu, lambda r0: a_ref[pl.ds(r0, TB), :] * _sigmoid(b_ref[pl.ds(r0, TB), :]))
        _fill_padded(cfg, pbd, lambda r0: dy_ref[pl.ds(r0, TB), :])
        wv = w_ref[...]
        dw_ref[...] = jnp.zeros_like(dw_ref)
        dbias_ref[...] = jnp.zeros_like(dbias_ref)

        def tile(r0, winu, wind):
            tapu = _taps(winu, TB)
            tapd = _taps(wind, TB)
            dyt = dy_ref[pl.ds(r0, TB), :]
            du = jnp.zeros((TB, LANE), F32)
            for k in range(K):
                du = du + wv[k:k + 1, :] * tapd(K - 1 - k)
                dw_ref[k:k + 1, :] += jnp.sum(dyt * tapu(k), axis=0, keepdims=True)
            dbias_ref[...] += jnp.sum(dyt, axis=0, keepdims=True)
            a = a_ref[pl.ds(r0, TB), :]
            sg = _sigmoid(b_ref[pl.ds(r0, TB), :])
            da_ref[pl.ds(r0, TB), :] = (du * sg).astype(da_ref.dtype)
            db_ref[pl.ds(r0, TB), :] = (du * a * sg * (1.0 - sg)).astype(db_ref.dtype)

        def lat(i, c):
            r0 = pl.multiple_of(i * TB, TB)
            tile(r0, pbu[pl.ds(r0, TB + 2 * _CPAD), :], pbd[pl.ds(r0, TB + 2 * _CPAD), :])
            return c

        lax.fori_loop(0, nlat, lat, 0)
        for r0, w0 in ctx_tiles:
            tile(r0, pbu[w0:w0 + TB + 2 * _CPAD, :], pbd[w0:w0 + TB + 2 * _CPAD, :])

    col = pl.BlockSpec((L, LANE), lambda j: (0, j))
    return _pc(body, name=name, grid=(cfg.CW // LANE,),
               in_specs=[pl.BlockSpec((L, LANE), lambda j: (0, ca + j)),
                         pl.BlockSpec((L, LANE), lambda j: (0, cb + j)),
                         pl.BlockSpec((32, LANE), lambda j: (0, j)), col],
               out_specs=[col, col, pl.BlockSpec((32, LANE), lambda j: (0, j)),
                          pl.BlockSpec((1, LANE), lambda j: (0, j))],
               out_shape=[jax.ShapeDtypeStruct((L, cfg.CW), BF16), jax.ShapeDtypeStruct((L, cfg.CW), BF16),
                          jax.ShapeDtypeStruct((32, cfg.CW), F32), jax.ShapeDtypeStruct((1, cfg.CW), F32)],
               scratch_shapes=[pltpu.VMEM((PBL, LANE), F32), pltpu.VMEM((PBL, LANE), F32)],
               compiler_params=_cp(("parallel",)))(P, P, w, dy)


def ln_silu_fwd(cfg, name, y, g, b):
    L, W = y.shape
    TB = cfg.TB

    def body(y_ref, g_ref, b_ref, o_ref):
        yv = y_ref[...]
        mu = jnp.mean(yv, axis=-1, keepdims=True)
        xc = yv - mu
        var = jnp.mean(xc * xc, axis=-1, keepdims=True)
        z = xc * lax.rsqrt(var + EPS) * g_ref[...] + b_ref[...]
        o_ref[...] = _silu(z).astype(o_ref.dtype)

    row = pl.BlockSpec((TB, W), lambda i: (i, 0))
    vec = pl.BlockSpec((1, W), lambda i: (0, 0))
    return _pc(body, name=name, grid=(L // TB,), in_specs=[row, vec, vec], out_specs=row,
               out_shape=jax.ShapeDtypeStruct((L, W), BF16), compiler_params=_cp(("parallel",)))(y, g, b)


def ln_silu_bwd(cfg, name, dact, y, g, b):
    L, W = y.shape
    TB = cfg.TB

    def body(d_ref, y_ref, g_ref, b_ref, dy_ref, dg_ref, db_ref):
        i = pl.program_id(0)
        yv = y_ref[...]
        mu = jnp.mean(yv, axis=-1, keepdims=True)
        xc = yv - mu
        var = jnp.mean(xc * xc, axis=-1, keepdims=True)
        r = lax.rsqrt(var + EPS)
        yh = xc * r
        g = g_ref[...]
        z = yh * g + b_ref[...]
        dz = d_ref[...] * _dsilu(z)
        sg = jnp.sum(dz * yh, axis=0, keepdims=True)
        sb = jnp.sum(dz, axis=0, keepdims=True)

        @pl.when(i == 0)
        def _():
            dg_ref[...] = sg
            db_ref[...] = sb

        @pl.when(i > 0)
        def _():
            dg_ref[...] += sg
            db_ref[...] += sb

        dh = dz * g
        dy_ref[...] = r * (dh - jnp.mean(dh, axis=-1, keepdims=True)
                           - yh * jnp.mean(dh * yh, axis=-1, keepdims=True))

    row = pl.BlockSpec((TB, W), lambda i: (i, 0))
    vec = pl.BlockSpec((1, W), lambda i: (0, 0))
    return _pc(body, name=name, grid=(L // TB,), in_specs=[row, row, vec, vec], out_specs=[row, vec, vec],
               out_shape=[jax.ShapeDtypeStruct((L, W), F32), jax.ShapeDtypeStruct((1, W), F32),
                          jax.ShapeDtypeStruct((1, W), F32)],
               compiler_params=_cp(("arbitrary",)))(dact, y, g, b)


def _na_geometry(cfg):
    R = cfg.T // cfg.GW
    nb = R // cfg.NAR
    assert nb >= 3 and cfg.GW == 64 and cfg.NAR == 8
    ks = [int(np.clip(8 * b - 4, 0, R - 16)) for b in range(nb)]
    return R, nb, ks


_NTAB = 18


def _split3(x):
    hi = x.astype(BF16)
    r = x - hi.astype(F32)
    mid = r.astype(BF16)
    lo = (r - mid.astype(F32)).astype(BF16)
    return hi, mid, lo


def _na_col_onehot(cfg):
    GW, NAC = cfg.GW, cfg.NAC
    qc = np.arange(GW)[:, None]
    kc = np.arange(GW)[None, :]
    cs = np.clip(qc - NAC // 2, 0, GW - NAC)
    vcol = (kc >= cs) & (kc < cs + NAC)
    dd = np.clip(kc - qc + NAC - 1, 0, 2 * NAC - 2)
    oh = (np.arange(LANE)[:, None, None] == dd[None]).astype(np.float32)
    z = np.zeros_like(oh)
    oda = np.concatenate([oh, z], axis=2).reshape(LANE, GW * LANE)
    odb = np.concatenate([z, oh], axis=2).reshape(LANE, GW * LANE)
    cm = np.where(np.concatenate([vcol, vcol], axis=1), 0.0, NEG).astype(np.float32).reshape(1, GW * LANE)
    return oda, odb, cm


def na_tables(cfg, name, rpb):
    NH, GW = cfg.NH, cfg.GW
    na = rpb.shape[1]
    oda, odb, cm = _na_col_onehot(cfg)
    rp = jnp.zeros((NH, _NTAB + 1, LANE), F32).at[:, 1:1 + na, :rpb.shape[2]].set(rpb.astype(F32))
    r0 = rp[:, :_NTAB].reshape(NH * _NTAB, LANE)
    r1 = rp[:, 1:].reshape(NH * _NTAB, LANE)
    a = np.arange(_NTAB) - 1
    rm0 = np.where((a >= 0) & (a < na), 0.0, NEG).astype(np.float32)
    rm1 = np.where((a + 1 >= 0) & (a + 1 < na), 0.0, NEG).astype(np.float32)
    half = (np.arange(GW * LANE) % LANE >= GW)[None, :]
    rmask = np.where(half, np.tile(rm1, NH)[:, None], np.tile(rm0, NH)[:, None]).astype(np.float32)
    tn = 2048
    rows = NH * _NTAB

    def body(r0_ref, r1_ref, a_ref, b_ref, cm_ref, rm_ref, o_ref):
        acc = cm_ref[...] + rm_ref[...]
        for t in _split3(r0_ref[...]):
            acc = acc + dot_nn(t, a_ref[...])
        for t in _split3(r1_ref[...]):
            acc = acc + dot_nn(t, b_ref[...])
        o_ref[...] = acc

    rs = pl.BlockSpec((rows, LANE), lambda n: (0, 0))
    out = _pc(body, name=name, grid=(GW * LANE // tn,),
              in_specs=[rs, rs, pl.BlockSpec((LANE, tn), lambda n: (0, n)), pl.BlockSpec((LANE, tn), lambda n: (0, n)),
                        pl.BlockSpec((1, tn), lambda n: (0, n)), pl.BlockSpec((rows, tn), lambda n: (0, n))],
              out_specs=pl.BlockSpec((rows, tn), lambda n: (0, n)),
              out_shape=jax.ShapeDtypeStruct((rows, GW * LANE), F32),
              compiler_params=_cp(("parallel",)))(r0, r1, jnp.asarray(oda, BF16), jnp.asarray(odb, BF16),
                                                  jnp.asarray(cm), jnp.asarray(rmask))
    return out.reshape(NH, _NTAB, GW, LANE)


def _na_tiles(cfg, b):
    R, nb, _ = _na_geometry(cfg)
    NAR = cfg.NAR
    ksb = jnp.clip(8 * b - 4, 0, R - 16)
    for i in range(8):
        qr = 8 * b + i
        ws = jnp.clip(qr - NAR // 2, 0, R - NAR)
        for J in range(8):
            kr0 = ksb + 2 * J
            row = jnp.clip(kr0 - qr + NAR - 1, -1, _NTAB - 2) + 1
            v0 = jnp.logical_and(kr0 >= ws, kr0 < ws + NAR)
            v1 = jnp.logical_and(kr0 + 1 >= ws, kr0 + 1 < ws + NAR)
            yield i, J, row, v0, v1


def _na_fill_bias(cfg, tab_ref, bias, b):
    GW = cfg.GW
    first = lax.broadcasted_iota(jnp.int32, (GW, LANE), 1) < GW
    for i, J, row, v0, v1 in _na_tiles(cfg, b):
        ok = jnp.where(first, v0.astype(jnp.int32), v1.astype(jnp.int32))
        bias[i * GW:(i + 1) * GW, J * LANE:(J + 1) * LANE] = jnp.where(ok > 0, tab_ref[row], NEG)


def _na_specs(cfg):
    R, nb, ks = _na_geometry(cfg)
    off = _offsets(cfg)
    TQ = 8 * cfg.GW
    KP = 4 * cfg.GW
    ks4 = [k // 4 for k in ks]
    lat_blocks = cfg.T // KP

    def ks4_of(b):
        return jnp.clip(2 * b - 1, 0, R // 4 - 4)

    assert all(int(np.clip(2 * b - 1, 0, R // 4 - 4)) == ks4[b] for b in range(nb))
    assert cfg.TC == KP

    def col(nm):
        c0 = off[nm] // LANE
        q = pl.BlockSpec((TQ, LANE), lambda h, b: (b, c0 + h))
        parts = [pl.BlockSpec((KP, LANE), functools.partial(lambda h, b, t: (ks4_of(b) + t, c0 + h), t=t))
                 for t in range(4)]
        ctx = pl.BlockSpec((KP, LANE), lambda h, b: (lat_blocks, c0 + h))
        return q, parts, ctx

    return nb, TQ, KP, ks4_of, col


def na_fwd(cfg, name, P, tab):
    nb, TQ, KP, ks4_of, col = _na_specs(cfg)
    NH = cfg.NH
    scale = cfg.NDH ** -0.5
    qs, _, _ = col("nq")
    _, kparts, kctx = col("nk")
    _, vparts, vctx = col("nv")

    def body(q_ref, k0, k1, k2, k3, kc_ref, v0, v1, v2, v3, vc_ref, tab_ref, o_ref, lse_ref, bias_ref):
        _na_fill_bias(cfg, tab_ref, bias_ref, pl.program_id(1))
        q = (q_ref[...] * scale).astype(BF16)
        kl = jnp.concatenate([k0[...], k1[...], k2[...], k3[...]], axis=0).astype(BF16)
        vl = jnp.concatenate([v0[...], v1[...], v2[...], v3[...]], axis=0).astype(BF16)
        kc = kc_ref[...].astype(BF16)
        vc = vc_ref[...].astype(BF16)
        sl = dot_nt(q, kl) + bias_ref[...]
        sc = dot_nt(q, kc)
        m = jnp.maximum(jnp.max(sl, axis=-1, keepdims=True), jnp.max(sc, axis=-1, keepdims=True))
        pl_ = jnp.exp(sl - m)
        pc = jnp.exp(sc - m)
        den = jnp.sum(pl_, axis=-1, keepdims=True) + jnp.sum(pc, axis=-1, keepdims=True)
        o = dot_nn(pl_.astype(BF16), vl) + dot_nn(pc.astype(BF16), vc)
        o_ref[...] = o / den
        lse_ref[...] = m + jnp.log(den)

    return _pc(body, name=name, grid=(NH, nb),
               in_specs=[qs, *kparts, kctx, *vparts, vctx,
                         pl.BlockSpec((None, _NTAB, cfg.GW, LANE), lambda h, b: (h, 0, 0, 0))],
               out_specs=[pl.BlockSpec((TQ, LANE), lambda h, b: (b, h)),
                          pl.BlockSpec((None, TQ, 1), lambda h, b: (h, b, 0))],
               out_shape=[jax.ShapeDtypeStruct((cfg.T, NH * LANE), F32),
                          jax.ShapeDtypeStruct((NH, cfg.T, 1), F32)],
               scratch_shapes=[pltpu.VMEM((TQ, 4 * KP), F32)],
               compiler_params=_cp(("parallel", "parallel")))(P, *([P] * 5), *([P] * 5), tab)


def na_bwd(cfg, name, P, tab, o, lse, dmix, dcol0):
    nb, TQ, KP, ks4_of, col = _na_specs(cfg)
    NH, GW = cfg.NH, cfg.GW
    L = P.shape[0]
    scale = cfg.NDH ** -0.5
    qs, _, _ = col("nq")
    _, kparts, kctx = col("nk")
    _, vparts, vctx = col("nv")

    def body(q_ref, k0, k1, k2, k3, kc_ref, v0, v1, v2, v3, vc_ref, tab_ref, o_ref, lse_ref, do_ref,
             dq_ref, dk_ref, dv_ref, dtab_ref, bias_ref):
        b = pl.program_id(1)

        @pl.when(b == 0)
        def _():
            dk_ref[...] = jnp.zeros_like(dk_ref)
            dv_ref[...] = jnp.zeros_like(dv_ref)
            dtab_ref[...] = jnp.zeros_like(dtab_ref)

        _na_fill_bias(cfg, tab_ref, bias_ref, b)

        q = (q_ref[...] * scale).astype(BF16)
        kl = jnp.concatenate([k0[...], k1[...], k2[...], k3[...]], axis=0).astype(BF16)
        vl = jnp.concatenate([v0[...], v1[...], v2[...], v3[...]], axis=0).astype(BF16)
        kc = kc_ref[...].astype(BF16)
        vc = vc_ref[...].astype(BF16)
        lse = lse_ref[...]
        do = do_ref[...]
        dob = do.astype(BF16)
        p_l = jnp.exp(dot_nt(q, kl) + bias_ref[...] - lse)
        p_c = jnp.exp(dot_nt(q, kc) - lse)
        delta = jnp.sum(do * o_ref[...], axis=-1, keepdims=True)
        ds_l = p_l * (dot_nt(dob, vl) - delta)
        ds_c = p_c * (dot_nt(dob, vc) - delta)
        dslb = ds_l.astype(BF16)
        dscb = ds_c.astype(BF16)
        dq_ref[...] = ((dot_nn(dslb, kl) + dot_nn(dscb, kc)) * scale).astype(dq_ref.dtype)
        r0 = pl.multiple_of(ks4_of(b) * KP, KP)
        dk_ref[pl.ds(r0, 4 * KP), :] += dot_tn(dslb, q)
        dv_ref[pl.ds(r0, 4 * KP), :] += dot_tn(p_l.astype(BF16), dob)
        dk_ref[cfg.T:cfg.T + KP, :] += dot_tn(dscb, q)
        dv_ref[cfg.T:cfg.T + KP, :] += dot_tn(p_c.astype(BF16), dob)
        bias_ref[...] = ds_l
        for i, J, row, _, _ in _na_tiles(cfg, b):
            dtab_ref[row] += bias_ref[i * GW:(i + 1) * GW, J * LANE:(J + 1) * LANE]

    full = pl.BlockSpec((L, LANE), lambda h, b: (0, h))
    tabs = pl.BlockSpec((None, _NTAB, GW, LANE), lambda h, b: (h, 0, 0, 0))
    return _pc(body, name=name, grid=(NH, nb),
               in_specs=[qs, *kparts, kctx, *vparts, vctx, tabs,
                         pl.BlockSpec((TQ, LANE), lambda h, b: (b, h)),
                         pl.BlockSpec((None, TQ, 1), lambda h, b: (h, b, 0)),
                         pl.BlockSpec((TQ, LANE), lambda h, b: (b, dcol0 + h))],
               out_specs=[pl.BlockSpec((TQ, LANE), lambda h, b: (b, h)), full, full, tabs],
               out_shape=[jax.ShapeDtypeStruct((cfg.T, NH * LANE), BF16),
                          jax.ShapeDtypeStruct((L, NH * LANE), F32), jax.ShapeDtypeStruct((L, NH * LANE), F32),
                          jax.ShapeDtypeStruct((NH, _NTAB, GW, LANE), F32)],
               scratch_shapes=[pltpu.VMEM((TQ, 4 * KP), F32)],
               compiler_params=_cp(("parallel", "arbitrary")))(
                   P, *([P] * 5), *([P] * 5), tab, o, lse, dmix)


def na_ctx_fwd(cfg, name, P):
    off = _offsets(cfg)
    TC, NH = cfg.TC, cfg.NH
    rb = cfg.T // TC
    scale = cfg.NDH ** -0.5

    def body(q_ref, k_ref, v_ref, o_ref, lse_ref):
        q = (q_ref[...] * scale).astype(BF16)
        s = dot_nt(q, k_ref[...].astype(BF16))
        m = jnp.max(s, axis=-1, keepdims=True)
        p = jnp.exp(s - m)
        den = jnp.sum(p, axis=-1, keepdims=True)
        o_ref[...] = dot_nn(p.astype(BF16), v_ref[...].astype(BF16)) / den
        lse_ref[...] = m + jnp.log(den)

    spec = lambda nm: pl.BlockSpec((TC, LANE), functools.partial(lambda h, c0: (rb, c0 + h), c0=off[nm] // LANE))
    return _pc(body, name=name, grid=(NH,), in_specs=[spec("nq"), spec("nk"), spec("nv")],
               out_specs=[pl.BlockSpec((TC, LANE), lambda h: (0, h)), pl.BlockSpec((None, TC, 1), lambda h: (h, 0, 0))],
               out_shape=[jax.ShapeDtypeStruct((TC, NH * LANE), F32), jax.ShapeDtypeStruct((NH, TC, 1), F32)],
               compiler_params=_cp(("parallel",)))(P, P, P)


def na_ctx_bwd(cfg, name, P, o, lse, dmix, dcol0, dk_in, dv_in):
    off = _offsets(cfg)
    TC, NH = cfg.TC, cfg.NH
    rb = cfg.T // TC
    scale = cfg.NDH ** -0.5

    def body(q_ref, k_ref, v_ref, o_ref, lse_ref, do_ref, dki_ref, dvi_ref, dq_ref, dk_ref, dv_ref):
        q = (q_ref[...] * scale).astype(BF16)
        kb = k_ref[...].astype(BF16)
        vb = v_ref[...].astype(BF16)
        do = do_ref[...]
        dob = do.astype(BF16)
        p = jnp.exp(dot_nt(q, kb) - lse_ref[...])
        delta = jnp.sum(do * o_ref[...], axis=-1, keepdims=True)
        ds = (p * (dot_nt(dob, vb) - delta)).astype(BF16)
        dq_ref[...] = (dot_nn(ds, kb) * scale).astype(dq_ref.dtype)
        dk_ref[...] = (dki_ref[...] + dot_tn(ds, q)).astype(dk_ref.dtype)
        dv_ref[...] = (dvi_ref[...] + dot_tn(p.astype(BF16), dob)).astype(dv_ref.dtype)

    spec = lambda nm: pl.BlockSpec((TC, LANE), functools.partial(lambda h, c0: (rb, c0 + h), c0=off[nm] // LANE))
    hb = pl.BlockSpec((TC, LANE), lambda h: (0, h))
    ctxrow = pl.BlockSpec((TC, LANE), lambda h: (rb, h))
    shp = jax.ShapeDtypeStruct((TC, NH * LANE), BF16)
    return _pc(body, name=name, grid=(NH,),
               in_specs=[spec("nq"), spec("nk"), spec("nv"), hb, pl.BlockSpec((None, TC, 1), lambda h: (h, 0, 0)),
                         pl.BlockSpec((TC, LANE), lambda h: (rb, dcol0 + h)), ctxrow, ctxrow],
               out_specs=[hb, hb, hb], out_shape=[shp, shp, shp],
               compiler_params=_cp(("parallel",)))(P, P, P, o, lse, dmix, dk_in, dv_in)


def na_rpb_grad(cfg, name, dtab):
    NH, GW = cfg.NH, cfg.GW
    na, nd = 2 * cfg.NAR - 1, 2 * cfg.NAC - 1
    oda, odb, _ = _na_col_onehot(cfg)
    E = np.concatenate([oda.T, odb.T], axis=1)
    rows = NH * _NTAB

    def body(z_ref, e_ref, o_ref):
        zv = z_ref[...]
        hi = zv.astype(BF16)
        lo = (zv - hi.astype(F32)).astype(BF16)
        e = e_ref[...]
        o_ref[...] = dot_nn(hi, e) + dot_nn(lo, e)

    g = _pc(body, name=name, out_shape=jax.ShapeDtypeStruct((rows, 2 * LANE), F32),
            compiler_params=_cp())(dtab.reshape(rows, GW * LANE), jnp.asarray(E, BF16))
    g = g.reshape(NH, _NTAB, 2, LANE)
    return g[:, 1:1 + na, 0, :nd] + g[:, 0:na, 1, :nd]


def _seq_tiles(cfg):
    T, TC, TB = cfg.T, cfg.TC, cfg.TB
    tiles = []
    for i in range((T + TC) // TB):
        r0 = i * TB
        tiles.append((r0, r0 == 0 or r0 == T, r0 + TB == T or r0 + TB == T + TC))
    return tiles


def _shift3(ref_get, r0, TB, start, end, width):
    cur = ref_get(r0, TB)
    if start or end:
        rowi = lax.broadcasted_iota(jnp.int32, (TB, width), 0)
    up = jnp.where(rowi == 0, 0.0, pltpu.roll(cur, 1, 0)) if start else ref_get(r0 - 1, TB)
    dn = jnp.where(rowi == TB - 1, 0.0, pltpu.roll(cur, TB - 1, 0)) if end else ref_get(r0 + 1, TB)
    return up, cur, dn


def ffn_act_fwd(cfg, name, U2, w, b):
    _, L, DFF = U2.shape
    TB = cfg.TB
    tiles = _seq_tiles(cfg)

    def body(u_ref, w_ref, b_ref, a_ref):
        def plane(p, r0, st, en):
            up, cur, dn = _shift3(lambda r, n: u_ref[p, r:r + n, :], r0, TB, st, en, LANE)
            wv = w_ref[p]
            return wv[0:1, :] * up + wv[1:2, :] * cur + wv[2:3, :] * dn + b_ref[p]

        for r0, st, en in tiles:
            val = plane(0, r0, st, en)
            gate = plane(1, r0, st, en)
            a_ref[r0:r0 + TB, :] = (_silu(gate) * val).astype(a_ref.dtype)

    return _pc(body, name=name, grid=(DFF // LANE,),
               in_specs=[pl.BlockSpec((2, L, LANE), lambda j: (0, 0, j)),
                         pl.BlockSpec((2, 8, LANE), lambda j: (0, 0, j)),
                         pl.BlockSpec((2, 1, LANE), lambda j: (0, 0, j))],
               out_specs=pl.BlockSpec((L, LANE), lambda j: (0, j)),
               out_shape=jax.ShapeDtypeStruct((L, DFF), BF16),
               compiler_params=_cp(("parallel",)))(U2, w, b)


def ffn_act_bwd(cfg, name, U2, w, b, dA):
    _, L, DFF = U2.shape
    TB = cfg.TB
    tiles = _seq_tiles(cfg)

    def body(u_ref, w_ref, b_ref, da_ref, du_ref, dw_ref, db_ref, dbuf):
        dw_ref[...] = jnp.zeros_like(dw_ref)
        db_ref[...] = jnp.zeros_like(db_ref)
        for r0, st, en in tiles:
            shifted = []
            pre = []
            for p in range(2):
                up, cur, dn = _shift3(lambda r, n: u_ref[p, r:r + n, :], r0, TB, st, en, LANE)
                wv = w_ref[p]
                shifted.append((up, cur, dn))
                pre.append(wv[0:1, :] * up + wv[1:2, :] * cur + wv[2:3, :] * dn + b_ref[p])
            val, gate = pre
            da = da_ref[r0:r0 + TB, :]
            dpre = (da * _silu(gate), da * val * _dsilu(gate))
            for p in range(2):
                dbuf[p, r0:r0 + TB, :] = dpre[p]
                for k in range(3):
                    dw_ref[p, k:k + 1, :] += jnp.sum(dpre[p] * shifted[p][k], axis=0, keepdims=True)
                db_ref[p] += jnp.sum(dpre[p], axis=0, keepdims=True)
        for r0, st, en in tiles:
            for p in range(2):
                up, cur, dn = _shift3(lambda r, n: dbuf[p, r:r + n, :], r0, TB, st, en, LANE)
                wv = w_ref[p]
                du_ref[p, r0:r0 + TB, :] = (wv[0:1, :] * dn + wv[1:2, :] * cur + wv[2:3, :] * up).astype(du_ref.dtype)

    blk = pl.BlockSpec((2, L, LANE), lambda j: (0, 0, j))
    wspec = pl.BlockSpec((2, 8, LANE), lambda j: (0, 0, j))
    bspec = pl.BlockSpec((2, 1, LANE), lambda j: (0, 0, j))
    return _pc(body, name=name, grid=(DFF // LANE,),
               in_specs=[blk, wspec, bspec, pl.BlockSpec((L, LANE), lambda j: (0, j))],
               out_specs=[blk, wspec, bspec],
               out_shape=[jax.ShapeDtypeStruct((2, L, DFF), BF16), jax.ShapeDtypeStruct((2, 8, DFF), F32),
                          jax.ShapeDtypeStruct((2, 1, DFF), F32)],
               scratch_shapes=[pltpu.VMEM((2, L, LANE), F32)],
               compiler_params=_cp(("parallel",)))(U2, w, b, dA)


def _tm(L, parts):
    assert L % parts == 0
    return L // parts


def layer_fwd(cfg, l, XS, mod, wts, small, tabs):
    L, D = XS.shape
    off = _offsets(cfg)
    DIN = off["end"]
    tmA = _tm(L, 4)
    sv = {"XS": XS, "W": {}}

    def weight(name, after):
        sv["W"][name], tok = wts(name, after)
        return sv["W"][name], tok

    Win4, _ = weight("w_in", XS)
    nbi = Win4.shape[3]
    h1 = norm_mod_fwd(cfg, f"norm1_fwd_{l}", XS, small["norm1_g"], mod["sc1"], mod["sh1"])
    P = matmul(f"mm_in_{l}", h1, Win4, contract="nn", grid=(4, L // tmA),
               a_spec=pl.BlockSpec((tmA, D), lambda n, m: (m, 0)),
               b_spec=pl.BlockSpec((None, None, D, nbi), lambda n, m: (n, l, 0, 0)),
               out_shape=jax.ShapeDtypeStruct((L, DIN), F32),
               out_spec=pl.BlockSpec((tmA, nbi), lambda n, m: (m, n)), nk=1)
    qr, kr = rope_fwd(cfg, f"rope_fwd_{l}", P, tabs["cos"], tabs["sin"])
    o_f, o_b, st = retention_fwd(cfg, f"ret_fwd_{l}", qr, kr, P, small["lam"])
    o2 = (o_f, o_b)
    ret = ggn_fwd(cfg, f"ggn_fwd_{l}", o2, P, small["ret_gn_g"])
    ycv = glu_dwconv_fwd(cfg, f"dwconv_fwd_{l}", P, small["conv_dw_w"], small["conv_dw_b"])
    act = ln_silu_fwd(cfg, f"ln_silu_fwd_{l}", ycv, small["conv_ln_g"], small["conv_ln_b"])
    Wpw4, _ = weight("conv_pw", act)
    cv = mm_rowsharded(f"mm_pw_{l}", act, Wpw4, l, BF16, cfg.CW)
    bias = na_tables(cfg, f"na_tables_{l}", small["na_rpb"])
    na_l, lse = na_fwd(cfg, f"na_fwd_{l}", P, bias)
    na_c, lse_c = na_ctx_fwd(cfg, f"na_ctx_fwd_{l}", P)
    mix = jnp.concatenate([ret, cv, jnp.concatenate([na_l, na_c], axis=0).astype(BF16)], axis=1)
    Wout4, tok = weight("w_out", mix)
    Y1 = mm_rowsharded(f"mm_out_{l}", mix, Wout4, l, F32, D)
    XM = resid_fwd(cfg, f"resid1_fwd_{l}", XS, Y1, mod["g1"] if tok is None else mod["g1"] + tok)
    h2 = norm_mod_fwd(cfg, f"norm2_fwd_{l}", XM, small["norm2_g"], mod["sc2"], mod["sh2"])
    Wup4, _ = weight("ffn_up", h2)
    nbu = Wup4.shape[3]
    tnu = nbu // 2
    U2 = matmul(f"mm_up_{l}", h2, Wup4, contract="nn", grid=(8, L // tmA),
                a_spec=pl.BlockSpec((tmA, D), lambda n, m: (m, 0)),
                b_spec=pl.BlockSpec((None, None, D, tnu), lambda n, m: (n // 2, l, 0, n % 2)),
                out_shape=jax.ShapeDtypeStruct((2, L, cfg.DFF), F32),
                out_spec=pl.BlockSpec((None, tmA, tnu), lambda n, m: (n // 4, m, n % 4)), nk=1)
    A = ffn_act_fwd(cfg, f"ffn_act_fwd_{l}", U2, small["ffn_dw_w"], small["ffn_dw_b"])
    Wdn4, _ = weight("ffn_down", A)
    Y2 = mm_rowsharded(f"mm_down_{l}", A, Wdn4, l, F32, D // 2)
    XO = resid_fwd(cfg, f"resid2_fwd_{l}", XM, Y2, mod["g2"])
    sv.update(h1=h1, P=P, qr=qr, kr=kr, o2=o2, st=st, ycv=ycv, act=act, bias=bias, na_l=na_l, lse=lse,
              na_c=na_c, lse_c=lse_c, mix=mix, Y1=Y1, XM=XM, h2=h2, U2=U2, A=A, Y2=Y2)
    return XO, sv


GRAD_GROUPS = (("ffn_down", "ffn_up"), ("w_out", "conv_pw", "w_in"))


def layer_bwd(cfg, l, dXO, sv, mod, wts, small, tabs, gbuf, ready):
    L, D = dXO.shape
    off = _offsets(cfg)
    DIN = off["end"]
    Win4, Wout4, Wup4, Wdn4, Wpw4 = wts["w_in"], wts["w_out"], wts["ffn_up"], wts["ffn_down"], wts["conv_pw"]
    tmA, tmB = _tm(L, 4), _tm(L, 8)
    depth = Win4.shape[1]
    gb, gs, dm = {}, {}, {}
    P = sv["P"]
    dY2, dm["g2"] = resid_bwd(cfg, f"resid2_bwd_{l}", dXO, sv["Y2"], mod["g2"])
    nbd = Wdn4.shape[2]
    dA = matmul(f"mm_down_da_{l}", dY2, Wdn4, contract="nt", grid=(4, L // tmA),
                a_spec=pl.BlockSpec((tmA, D), lambda j, m: (m, 0)),
                b_spec=pl.BlockSpec((None, None, nbd, D), lambda j, m: (j, l, 0, 0)),
                out_shape=jax.ShapeDtypeStruct((L, cfg.DFF), F32),
                out_spec=pl.BlockSpec((tmA, nbd), lambda j, m: (m, j)), nk=1)
    gb["ffn_down"] = wgrad(cfg, f"mm_down_dw_{l}", sv["A"], dY2,
                           lambda rb, ri: pl.BlockSpec((rb, nbd), lambda j, m: (ri(m), j)),
                           lambda rb, ri: pl.BlockSpec((rb, D), lambda j, m: (ri(m), 0)),
                           jax.ShapeDtypeStruct((depth, 4, nbd, D), BF16),
                           pl.BlockSpec((None, None, nbd, D), lambda j, m: (l, j, 0, 0)), 4, gbuf.get("ffn_down"))
    dU2, dfw, dfb = ffn_act_bwd(cfg, f"ffn_act_bwd_{l}", sv["U2"], small["ffn_dw_w"], small["ffn_dw_b"], dA)
    gs["ffn_dw_w"], gs["ffn_dw_b"] = dfw, dfb
    nbu = Wup4.shape[3]
    tnu = nbu // 2
    dH2 = matmul(f"mm_up_dh_{l}", dU2, Wup4, contract="nt", grid=(L // tmA, 8),
                 a_spec=pl.BlockSpec((None, tmA, tnu), lambda m, n: (n // 4, m, n % 4)),
                 b_spec=pl.BlockSpec((None, None, D, tnu), lambda m, n: (n // 2, l, 0, n % 2)),
                 out_shape=jax.ShapeDtypeStruct((L, D), F32),
                 out_spec=pl.BlockSpec((tmA, D), lambda m, n: (m, 0)), nk=8)
    gb["ffn_up"] = wgrad(cfg, f"mm_up_dw_{l}", sv["h2"], dU2,
                         lambda rb, ri: pl.BlockSpec((rb, D), lambda n, m: (ri(m), 0)),
                         lambda rb, ri: pl.BlockSpec((None, rb, tnu), lambda n, m: (n // 4, ri(m), n % 4)),
                         jax.ShapeDtypeStruct((depth, 4, D, nbu), BF16),
                         pl.BlockSpec((None, None, D, tnu), lambda n, m: (l, n // 2, 0, n % 2)), 8, gbuf.get("ffn_up"))
    dXM, dm["sc2"], dm["sh2"], gs["norm2_g"] = norm_mod_bwd(
        cfg, f"norm2_bwd_{l}", dH2, sv["XM"], small["norm2_g"], mod["sc2"], dXO)
    tok = ready(GRAD_GROUPS[0], gb)
    dY1, dm["g1"] = resid_bwd(cfg, f"resid1_bwd_{l}", dXM, sv["Y1"], mod["g1"] if tok is None else mod["g1"] + tok)
    nbo = Wout4.shape[2]
    dmix = matmul(f"mm_out_dmix_{l}", dY1, Wout4, contract="nt", grid=(4, L // tmA),
                  a_spec=pl.BlockSpec((tmA, D), lambda j, m: (m, 0)),
                  b_spec=pl.BlockSpec((None, None, nbo, D), lambda j, m: (j, l, 0, 0)),
                  out_shape=jax.ShapeDtypeStruct((L, D), F32),
                  out_spec=pl.BlockSpec((tmA, nbo), lambda j, m: (m, j)), nk=1)
    gb["w_out"] = wgrad(cfg, f"mm_out_dw_{l}", sv["mix"], dY1,
                        lambda rb, ri: pl.BlockSpec((rb, nbo), lambda j, m: (ri(m), j)),
                        lambda rb, ri: pl.BlockSpec((rb, D), lambda j, m: (ri(m), 0)),
                        jax.ShapeDtypeStruct((depth, 4, nbo, D), BF16),
                        pl.BlockSpec((None, None, nbo, D), lambda j, m: (l, j, 0, 0)), 4, gbuf.get("w_out"))
    RW = cfg.RH * cfg.RDV
    do, dlg, gs["ret_gn_g"] = ggn_bwd(cfg, f"ggn_bwd_{l}", dmix, sv["o2"], P, small["ret_gn_g"], 0)
    dqf, dqb, dkf, dkb, dvf, dvb, dlam = retention_bwd(
        cfg, f"ret_bwd_{l}", sv["qr"], sv["kr"], P, small["lam"], sv["st"], do)
    gs["lam"] = dlam[:, :, 0, 0]
    dlq, dlk = rope_bwd(cfg, f"rope_bwd_{l}", (dqf, dqb), (dkf, dkb), tabs["cos"], tabs["sin"])
    dlv = add_cast(cfg, f"ret_dv_{l}", dvf, dvb)
    dcv = cast_cols(cfg, f"conv_dcv_{l}", dmix, RW // LANE, cfg.CW // LANE, LANE)
    nbp = Wpw4.shape[2]
    dact = matmul(f"mm_pw_dact_{l}", dcv, Wpw4, contract="nt", grid=(4, L // tmA),
                  a_spec=pl.BlockSpec((tmA, cfg.CW), lambda j, m: (m, 0)),
                  b_spec=pl.BlockSpec((None, None, nbp, cfg.CW), lambda j, m: (j, l, 0, 0)),
                  out_shape=jax.ShapeDtypeStruct((L, cfg.CW), F32),
                  out_spec=pl.BlockSpec((tmA, nbp), lambda j, m: (m, j)), nk=1)
    gb["conv_pw"] = wgrad(cfg, f"mm_pw_dw_{l}", sv["act"], dcv,
                          lambda rb, ri: pl.BlockSpec((rb, nbp), lambda j, m: (ri(m), j)),
                          lambda rb, ri: pl.BlockSpec((rb, cfg.CW), lambda j, m: (ri(m), 0)),
                          jax.ShapeDtypeStruct((depth, 4, nbp, cfg.CW), BF16),
                          pl.BlockSpec((None, None, nbp, cfg.CW), lambda j, m: (l, j, 0, 0)), 4, gbuf.get("conv_pw"))
    dycv, gs["conv_ln_g"], gs["conv_ln_b"] = ln_silu_bwd(
        cfg, f"ln_silu_bwd_{l}", dact, sv["ycv"], small["conv_ln_g"], small["conv_ln_b"])
    dla, dlb, gs["conv_dw_w"], gs["conv_dw_b"] = glu_dwconv_bwd(cfg, f"dwconv_bwd_{l}", P, small["conv_dw_w"], dycv)
    nac0 = (RW + cfg.CW) // LANE
    dnq_l, dnk, dnv, dsb = na_bwd(cfg, f"na_bwd_{l}", P, sv["bias"], sv["na_l"], sv["lse"], dmix, nac0)
    dnq_c, dnk_c, dnv_c = na_ctx_bwd(cfg, f"na_ctx_bwd_{l}", P, sv["na_c"], sv["lse_c"], dmix, nac0, dnk, dnv)
    gs["na_rpb"] = na_rpb_grad(cfg, f"na_rpb_{l}", dsb)
    dnq = jnp.concatenate([dnq_l, dnq_c], axis=0)
    dnk = jnp.concatenate([dnk[:cfg.T].astype(BF16), dnk_c], axis=0)
    dnv = jnp.concatenate([dnv[:cfg.T].astype(BF16), dnv_c], axis=0)
    dP = jnp.concatenate([dlq, dlk, dlv, dlg, dla, dlb, dnq, dnk, dnv], axis=1)
    nbi = Win4.shape[3]
    dH1 = matmul(f"mm_in_dh_{l}", dP, Win4, contract="nt", grid=(L // tmA, 4),
                 a_spec=pl.BlockSpec((tmA, nbi), lambda m, n: (m, n)),
                 b_spec=pl.BlockSpec((None, None, D, nbi), lambda m, n: (n, l, 0, 0)),
                 out_shape=jax.ShapeDtypeStruct((L, D), F32),
                 out_spec=pl.BlockSpec((tmA, D), lambda m, n: (m, 0)), nk=4)
    gb["w_in"] = wgrad(cfg, f"mm_in_dw_{l}", sv["h1"], dP,
                       lambda rb, ri: pl.BlockSpec((rb, D), lambda n, m: (ri(m), 0)),
                       lambda rb, ri: pl.BlockSpec((rb, nbi), lambda n, m: (ri(m), n)),
                       jax.ShapeDtypeStruct((depth, 4, D, nbi), BF16),
                       pl.BlockSpec((None, None, D, nbi), lambda n, m: (l, n, 0, 0)), 4, gbuf.get("w_in"))
    dXS, dm["sc1"], dm["sh1"], gs["norm1_g"] = norm_mod_bwd(
        cfg, f"norm1_bwd_{l}", dH1, sv["XS"], small["norm1_g"], mod["sc1"], dXM, latent_only=(l == 0))
    return dXS, gb, gs, dm, ready(GRAD_GROUPS[1], gb)


def _layer_small(cfg, l, sp):
    DFF = cfg.DFF
    fw = sp["ffn_dw_w"][l].reshape(3, 2, DFF).transpose(1, 0, 2)
    fw = jnp.concatenate([fw, jnp.zeros((2, 5, DFF), F32)], axis=1)
    cw = jnp.concatenate([sp["conv_dw_w"][l], jnp.zeros((32 - cfg.CK, cfg.CW), F32)], axis=0)
    return dict(
        norm1_g=sp["norm1_g"][l][None], norm2_g=sp["norm2_g"][l][None],
        lam=jax.nn.log_sigmoid(sp["ret_decay"][l]), ret_gn_g=sp["ret_gn_g"][l][None],
        conv_dw_w=cw, conv_dw_b=sp["conv_dw_b"][l][None], conv_ln_g=sp["conv_ln_g"][l][None],
        conv_ln_b=sp["conv_ln_b"][l][None], na_rpb=sp["na_rpb"][l],
        ffn_dw_w=fw, ffn_dw_b=sp["ffn_dw_b"][l].reshape(2, 1, DFF))


def local_step(cfg, x, ctx, tgt, mods, wts, sp, grads_ready=lambda l, names, gb: None):
    depth = sp["norm1_g"].shape[0]
    cos, sin = rope_tables(cfg)
    tabs = dict(cos=cos, sin=sin)
    XS = jnp.concatenate([x, ctx], axis=0)
    smalls = [_layer_small(cfg, l, sp) for l in range(depth)]
    saves = []
    for l in range(depth):
        XS, sv = layer_fwd(cfg, l, XS, mods[l], functools.partial(wts, l), smalls[l], tabs)
        saves.append(sv)
    ls, dX, dfg = final_loss(cfg, "final_loss", XS, sp["final_g"][None], tgt)
    gb, gss, dms = {}, [None] * depth, [None] * depth
    token = None
    for l in reversed(range(depth)):
        mod = mods[l] if token is None else {**mods[l], "g2": mods[l]["g2"] + token}
        dX, gb, gss[l], dms[l], token = layer_bwd(cfg, l, dX, saves[l], mod, saves[l]["W"], smalls[l], tabs, gb,
                                                  functools.partial(grads_ready, l))
    return ls[0, 0], dX[:cfg.T], gb, gss, dms, dfg[0]


MESH = pl.DeviceIdType.MESH
N_DEV = 8
N_CHIP = 4
BIG = ("w_in", "w_out", "ffn_up", "ffn_down", "conv_pw")
_ANY = pl.BlockSpec(memory_space=pl.ANY)


def _place():
    x, y, c = lax.axis_index("x"), lax.axis_index("y"), lax.axis_index("c")
    chips = [(1 - x, y), (x, 1 - y), (1 - x, 1 - y)]
    return x, y, c, chips


def allgather8(name, xs):
    m_per, n = xs.shape

    def body(x_ref, out_ref, send_sems, recv_sems, local_sem):
        x, y, c, chips = _place()
        me, sibling = (x, y, c), (x, y, 1 - c)

        def rows(px, py, pc):
            return out_ref.at[pl.ds((4 * px + 2 * py + pc) * m_per, m_per), :]

        def copy(k, block, to, src=None):
            return pltpu.make_async_remote_copy(
                src_ref=rows(*block) if src is None else src, dst_ref=rows(*block),
                send_sem=send_sems.at[k], recv_sem=recv_sems.at[k], device_id=to, device_id_type=MESH)

        mine = pltpu.make_async_copy(x_ref, rows(*me), local_sem)
        mine.start()
        first = [copy(0, me, sibling, src=x_ref)]
        first += [copy(1 + j, me, (*chip, c), src=x_ref) for j, chip in enumerate(chips)]
        for cp in first:
            cp.start()
        passed = [copy(4 + j, (*chip, c), sibling) for j, chip in enumerate(chips)]
        for j, chip in enumerate(chips):
            copy(1 + j, (*chip, c), me).wait_recv()
            passed[j].start()
        copy(0, sibling, me).wait_recv()
        for j, chip in enumerate(chips):
            copy(4 + j, (*chip, 1 - c), me).wait_recv()
        for cp in first + passed:
            cp.wait_send()
        mine.wait()

    return _pc(body, name=name, out_shape=jax.ShapeDtypeStruct((N_DEV * m_per, n), xs.dtype),
               in_specs=[pl.BlockSpec(memory_space=pltpu.VMEM)], out_specs=pl.BlockSpec(memory_space=pltpu.VMEM),
               scratch_shapes=[pltpu.SemaphoreType.DMA((7,)), pltpu.SemaphoreType.DMA((7,)), pltpu.SemaphoreType.DMA],
               compiler_params=pltpu.CompilerParams(vmem_limit_bytes=VMEM_LIMIT))(xs)


def _wpiece(ref, layer, chip_idx, half):
    rh = ref.shape[2] // 2
    return ref.at[chip_idx, layer, pl.ds(half * rh, rh)]


def _wcopy(ref, layer, chip_idx, half, send_sems, recv_sems, k, to):
    piece = _wpiece(ref, layer, chip_idx, half)
    return pltpu.make_async_remote_copy(src_ref=piece, dst_ref=piece, send_sem=send_sems.at[k],
                                        recv_sem=recv_sems.at[k], device_id=to, device_id_type=MESH)


def _w_ici_sends(outs, layer, send_sems, recv_sems):
    x, y, c, chips = _place()
    return [_wcopy(outs[a], layer, 2 * x + y, c, send_sems, recv_sems, 3 * a + t, (*chip, c))
            for a in range(len(outs)) for t, chip in enumerate(chips)]


def _w_ici_landed(outs, layer, send_sems, recv_sems):
    x, y, c, chips = _place()
    return [_wcopy(outs[a], layer, 2 * chip[0] + chip[1], c, send_sems, recv_sems, 3 * a + t, (x, y, c))
            for a in range(len(outs)) for t, chip in enumerate(chips)]


def _w_forward(outs, layer, send_sems, recv_sems, base):
    x, y, c, chips = _place()
    n = len(outs)
    sends = [_wcopy(outs[a], layer, 2 * chip[0] + chip[1], c, send_sems, recv_sems, base + 3 * a + t, (x, y, 1 - c))
             for a in range(n) for t, chip in enumerate(chips)]
    recvs = [_wcopy(outs[a], layer, 2 * chip[0] + chip[1], 1 - c, send_sems, recv_sems, base + 3 * a + t, (x, y, c))
             for a in range(n) for t, chip in enumerate(chips)]
    return sends, recvs


def allgather_layer(name, bufs, layer):
    n = len(bufs)

    def body(*refs):
        outs = refs[n:2 * n]
        send_sems, recv_sems = refs[2 * n:]
        sent = _w_ici_sends(outs, layer, send_sems, recv_sems)
        for cp in sent:
            cp.start()
        fwd, from_sib = _w_forward(outs, layer, send_sems, recv_sems, 3 * n)
        for landed, fw in zip(_w_ici_landed(outs, layer, send_sems, recv_sems), fwd):
            landed.wait_recv()
            fw.start()
        for cp in from_sib:
            cp.wait_recv()
        for cp in sent + fwd:
            cp.wait_send()

    return _pc(body, name=name, out_shape=[jax.ShapeDtypeStruct(b.shape, b.dtype) for b in bufs],
               in_specs=[_ANY] * n, out_specs=[_ANY] * n, input_output_aliases={a: a for a in range(n)},
               scratch_shapes=[pltpu.SemaphoreType.DMA((6 * n,)), pltpu.SemaphoreType.DMA((6 * n,))])(*bufs)


_HBM = pl.BlockSpec(memory_space=pltpu.HBM)
_SEM = pl.BlockSpec(memory_space=pltpu.SEMAPHORE)
_EFFECT = pltpu.SideEffectType.DATAFLOW_SIDE_EFFECTING


def allgather_layer_start(name, bufs, layer, after):
    n = len(bufs)

    def body(*refs):
        send_sems, recv_sems = refs[n + 1:n + 3]
        outs = refs[n + 3:2 * n + 3]
        token = refs[2 * n + 3]
        for cp in _w_ici_sends(outs, layer, send_sems, recv_sems):
            cp.start()
        token[...] = jnp.zeros_like(token)

    res = _pc(body, name=name,
              out_shape=(pltpu.SemaphoreType.DMA((3 * n,)), pltpu.SemaphoreType.DMA((3 * n,)),
                         *[pltpu.HBM(b.shape, b.dtype) for b in bufs], jax.ShapeDtypeStruct((8, LANE), F32)),
              in_specs=[_HBM] * n + [_ANY],
              out_specs=(_SEM, _SEM, *([_HBM] * n), pl.BlockSpec(memory_space=pltpu.VMEM)),
              input_output_aliases={a: a + 2 for a in range(n)},
              compiler_params=pltpu.CompilerParams(has_side_effects=_EFFECT))(
                  *[pltpu.with_memory_space_constraint(b, pltpu.HBM) for b in bufs], after)
    return res[0], res[1], list(res[2:2 + n]), res[2 + n]


def allgather_layer_wait(name, bufs, send_sems, recv_sems, after, layer):
    n = len(bufs)

    def body(*refs):
        ins = refs[:n]
        send_sems, recv_sems = refs[n:n + 2]
        for cp in _w_ici_sends(ins, layer, send_sems, recv_sems):
            cp.wait_send()
        for cp in _w_ici_landed(ins, layer, send_sems, recv_sems):
            cp.wait_recv()

    return _pc(body, name=name, out_shape=tuple(pltpu.HBM(b.shape, b.dtype) for b in bufs),
               in_specs=[_HBM] * n + [_SEM, _SEM, _ANY], out_specs=tuple([_HBM] * n),
               input_output_aliases={a: a for a in range(n)},
               compiler_params=pltpu.CompilerParams(has_side_effects=_EFFECT))(*bufs, send_sems, recv_sems, after)


def forward_halves(name, bufs, layer):
    n = len(bufs)

    def body(*refs):
        outs = refs[n:2 * n]
        send_sems, recv_sems = refs[2 * n:]
        fwd, from_sib = _w_forward(outs, layer, send_sems, recv_sems, 0)
        for cp in fwd:
            cp.start()
        for cp in from_sib:
            cp.wait_recv()
        for cp in fwd:
            cp.wait_send()

    return _pc(body, name=name, out_shape=[jax.ShapeDtypeStruct(b.shape, b.dtype) for b in bufs],
               in_specs=[_ANY] * n, out_specs=[_ANY] * n, input_output_aliases={a: a for a in range(n)},
               scratch_shapes=[pltpu.SemaphoreType.DMA((3 * n,)), pltpu.SemaphoreType.DMA((3 * n,))])(*bufs)


def exchange_rows(name, grads, layer):
    n = len(grads)

    def body(*refs):
        ins, outs = refs[:n], refs[n:2 * n]
        send_sems, recv_sems = refs[2 * n:]
        x, y, c, _ = _place()
        cps = []
        for a in range(n):
            rh = ins[a].shape[2] // 2
            cps.append(pltpu.make_async_remote_copy(
                src_ref=ins[a].at[layer, pl.ds(0, N_CHIP), pl.ds((1 - c) * rh, rh)], dst_ref=outs[a],
                send_sem=send_sems.at[a], recv_sem=recv_sems.at[a], device_id=(x, y, 1 - c), device_id_type=MESH))
        for cp in cps:
            cp.start()
        for cp in cps:
            cp.wait()

    return _pc(body, name=name,
               out_shape=[jax.ShapeDtypeStruct((N_CHIP, g.shape[2] // 2, g.shape[3]), g.dtype) for g in grads],
               in_specs=[_ANY] * n, out_specs=[_ANY] * n,
               scratch_shapes=[pltpu.SemaphoreType.DMA((n,)), pltpu.SemaphoreType.DMA((n,))])(*grads)


def _scatter_sends(parts, lands, send_sems, recv_sems):
    x, y, c, chips = _place()
    return [pltpu.make_async_remote_copy(
        src_ref=parts[a].at[2 * chip[0] + chip[1]], dst_ref=lands[a].at[2 * x + y], send_sem=send_sems.at[3 * a + t],
        recv_sem=recv_sems.at[3 * a + t], device_id=(*chip, c), device_id_type=MESH)
        for a in range(len(parts)) for t, chip in enumerate(chips)]


def _scatter_landed(lands, send_sems, recv_sems):
    x, y, c, chips = _place()
    return [pltpu.make_async_remote_copy(
        src_ref=lands[a].at[2 * chip[0] + chip[1]], dst_ref=lands[a].at[2 * chip[0] + chip[1]],
        send_sem=send_sems.at[3 * a + t], recv_sem=recv_sems.at[3 * a + t], device_id=(x, y, c), device_id_type=MESH)
        for a in range(len(lands)) for t, chip in enumerate(chips)]


def scatter_slices(name, parts, lands):
    n = len(parts)

    def body(*refs):
        ins, outs = refs[:n], refs[2 * n:3 * n]
        send_sems, recv_sems = refs[3 * n:]
        cps = _scatter_sends(ins, outs, send_sems, recv_sems)
        for cp in cps:
            cp.start()
        for cp in _scatter_landed(outs, send_sems, recv_sems):
            cp.wait_recv()
        for cp in cps:
            cp.wait_send()

    return _pc(body, name=name, out_shape=[jax.ShapeDtypeStruct(p.shape, p.dtype) for p in lands],
               in_specs=[_ANY] * (2 * n), out_specs=[_ANY] * n,
               input_output_aliases={n + a: a for a in range(n)},
               scratch_shapes=[pltpu.SemaphoreType.DMA((3 * n,)), pltpu.SemaphoreType.DMA((3 * n,))])(*parts, *lands)


def scatter_slices_start(name, parts, lands):
    n = len(parts)

    def body(*refs):
        send_sems, recv_sems = refs[2 * n:2 * n + 2]
        p_out, l_out = refs[2 * n + 2:3 * n + 2], refs[3 * n + 2:4 * n + 2]
        token = refs[4 * n + 2]
        for cp in _scatter_sends(p_out, l_out, send_sems, recv_sems):
            cp.start()
        token[...] = jnp.zeros_like(token)

    both = list(parts) + list(lands)
    res = _pc(body, name=name,
              out_shape=(pltpu.SemaphoreType.DMA((3 * n,)), pltpu.SemaphoreType.DMA((3 * n,)),
                         *[pltpu.HBM(b.shape, b.dtype) for b in both], jax.ShapeDtypeStruct((8, LANE), F32)),
              in_specs=[_HBM] * (2 * n),
              out_specs=(_SEM, _SEM, *([_HBM] * (2 * n)), pl.BlockSpec(memory_space=pltpu.VMEM)),
              input_output_aliases={a: a + 2 for a in range(2 * n)},
              compiler_params=pltpu.CompilerParams(has_side_effects=_EFFECT))(
                  *[pltpu.with_memory_space_constraint(b, pltpu.HBM) for b in both])
    return res[0], res[1], list(res[2:2 + n]), list(res[2 + n:2 + 2 * n]), res[2 + 2 * n]


def scatter_slices_wait(name, parts, lands, send_sems, recv_sems, after):
    n = len(parts)

    def body(*refs):
        p_in, l_in = refs[:n], refs[n:2 * n]
        send_sems, recv_sems = refs[2 * n:2 * n + 2]
        for cp in _scatter_sends(p_in, l_in, send_sems, recv_sems):
            cp.wait_send()
        for cp in _scatter_landed(l_in, send_sems, recv_sems):
            cp.wait_recv()

    both = list(parts) + list(lands)
    res = _pc(body, name=name, out_shape=tuple(pltpu.HBM(b.shape, b.dtype) for b in both),
              in_specs=[_HBM] * (2 * n) + [_SEM, _SEM, _ANY], out_specs=tuple([_HBM] * (2 * n)),
              input_output_aliases={a: a for a in range(2 * n)},
              compiler_params=pltpu.CompilerParams(has_side_effects=_EFFECT))(*both, send_sems, recv_sems, after)
    return list(res[n:])


def share_rows(name, bufs):
    n = len(bufs)

    def body(*refs):
        outs = refs[n:2 * n]
        send_sems, recv_sems = refs[2 * n:]
        x, y, c, _ = _place()

        def half(a, h):
            return outs[a].at[pl.ds(0, 2), h]

        cps = [pltpu.make_async_remote_copy(
            src_ref=half(a, c), dst_ref=half(a, c), send_sem=send_sems.at[a], recv_sem=recv_sems.at[a],
            device_id=(x, y, 1 - c), device_id_type=MESH) for a in range(n)]
        for cp in cps:
            cp.start()
        for a in range(n):
            pltpu.make_async_remote_copy(
                src_ref=half(a, 1 - c), dst_ref=half(a, 1 - c), send_sem=send_sems.at[a],
                recv_sem=recv_sems.at[a], device_id=(x, y, c), device_id_type=MESH).wait_recv()
        for cp in cps:
            cp.wait_send()

    return _pc(body, name=name, out_shape=[jax.ShapeDtypeStruct(b.shape, b.dtype) for b in bufs],
               in_specs=[_ANY] * n, out_specs=[_ANY] * n, input_output_aliases={a: a for a in range(n)},
               scratch_shapes=[pltpu.SemaphoreType.DMA((n,)), pltpu.SemaphoreType.DMA((n,))])(*bufs)


def _row_tile(R, C, nbytes=1 << 20):
    t = 8
    while t * 2 <= R and R % (t * 2) == 0 and t * 2 * C * 4 <= nbytes:
        t *= 2
    assert R % t == 0
    return t


def to_bf16_block(name, w, chip_arr, layer):
    _, R, C = w.shape
    tr = _row_tile(R, C)

    def body(j_ref, w_ref, o_ref):
        o_ref[...] = w_ref[...].astype(o_ref.dtype)

    gs = pltpu.PrefetchScalarGridSpec(
        num_scalar_prefetch=1, grid=(R // tr,),
        in_specs=[pl.BlockSpec((None, tr, C), lambda i, j_ref: (layer, i, 0))],
        out_specs=pl.BlockSpec((None, None, tr, C), lambda i, j_ref: (j_ref[0], layer, i, 0)))
    return _pc(body, name=name, grid_spec=gs, out_shape=jax.ShapeDtypeStruct((N_CHIP,) + w.shape, BF16),
               compiler_params=_cp(("parallel",)))(chip_arr, w)


def add_rows(name, g, ra, c_arr, layer):
    _, _, R, C = g.shape
    rh = R // 2
    tr = _row_tile(rh, C)
    nb = rh // tr

    def body(c_ref, g_ref, r_ref, o_ref):
        o_ref[...] = (g_ref[...].astype(F32) + r_ref[...].astype(F32)).astype(o_ref.dtype)

    gs = pltpu.PrefetchScalarGridSpec(
        num_scalar_prefetch=1, grid=(N_CHIP, nb),
        in_specs=[pl.BlockSpec((None, None, tr, C), lambda j, i, c_ref: (layer, j, c_ref[0] * nb + i, 0)),
                  pl.BlockSpec((None, tr, C), lambda j, i, c_ref: (j, i, 0))],
        out_specs=pl.BlockSpec((None, tr, C), lambda j, i, c_ref: (j, i, 0)))
    return _pc(body, name=name, grid_spec=gs, out_shape=jax.ShapeDtypeStruct(ra.shape, BF16),
               compiler_params=_cp(("parallel", "parallel")))(c_arr, g, ra)


def sum_rows_into(name, landed, c_arr, layer, into):
    n, rh, C = landed.shape
    tr = _row_tile(rh, C, nbytes=1 << 19)

    def body(*refs):
        g_ref, o_ref = refs[1], refs[-1]
        acc = g_ref[0].astype(F32)
        for j in range(1, n):
            acc = acc + g_ref[j].astype(F32)
        o_ref[...] = acc

    in_specs, args, alias = [pl.BlockSpec((n, tr, C), lambda i, c_ref: (0, i, 0))], (c_arr, landed), {}
    if into is not None:
        in_specs, args, alias = in_specs + [_ANY], args + (into,), {2: 0}
    gs = pltpu.PrefetchScalarGridSpec(
        num_scalar_prefetch=1, grid=(rh // tr,), in_specs=in_specs,
        out_specs=pl.BlockSpec((None, None, tr, C), lambda i, c_ref: (layer, c_ref[0], i, 0)))
    return _pc(body, name=name, grid_spec=gs, out_shape=jax.ShapeDtypeStruct((2, 2, rh, C), F32),
               input_output_aliases=alias, compiler_params=_cp(("parallel",)))(*args)


def own_row(name, part, chip_arr):
    _, R, C = part.shape
    tr = _row_tile(R, C)

    def body(j_ref, p_ref, o_ref):
        o_ref[...] = p_ref[...]

    gs = pltpu.PrefetchScalarGridSpec(
        num_scalar_prefetch=1, grid=(R // tr,),
        in_specs=[pl.BlockSpec((None, tr, C), lambda i, j_ref: (j_ref[0], i, 0))],
        out_specs=pl.BlockSpec((None, tr, C), lambda i, j_ref: (j_ref[0], i, 0)))
    return _pc(body, name=name, grid_spec=gs, out_shape=jax.ShapeDtypeStruct(part.shape, part.dtype),
               compiler_params=_cp(("parallel",)))(chip_arr, part)


def sum_leading(name, g, plane=None):
    n, R, C = g.shape
    tr = _row_tile(R, C, nbytes=(1 << 21) // n)

    def body(*refs):
        g_ref, o_ref = refs[-2:]
        acc = g_ref[0].astype(F32)
        for j in range(1, n):
            acc = acc + g_ref[j].astype(F32)
        o_ref[...] = acc

    if plane is None:
        return _pc(body, name=name, grid=(R // tr,), in_specs=[pl.BlockSpec((n, tr, C), lambda i: (0, i, 0))],
                   out_specs=pl.BlockSpec((tr, C), lambda i: (i, 0)), out_shape=jax.ShapeDtypeStruct((R, C), F32),
                   compiler_params=_cp(("parallel",)))(g)
    count, idx = plane
    gs = pltpu.PrefetchScalarGridSpec(
        num_scalar_prefetch=1, grid=(R // tr,),
        in_specs=[pl.BlockSpec((n, tr, C), lambda i, p_ref: (0, i, 0))],
        out_specs=pl.BlockSpec((None, tr, C), lambda i, p_ref: (p_ref[0], i, 0)))
    return _pc(body, name=name, grid_spec=gs, out_shape=jax.ShapeDtypeStruct((count, R, C), F32),
               compiler_params=_cp(("parallel",)))(idx, g)


def adamw(name, w, g, m, v):
    R, C = w.shape
    tr = _row_tile(R, C)

    def body(w_ref, g_ref, m_ref, v_ref, d_ref, mo_ref, vo_ref):
        gv = g_ref[...]
        mn = ADAM_B1 * m_ref[...] + (1.0 - ADAM_B1) * gv
        vn = ADAM_B2 * v_ref[...] + (1.0 - ADAM_B2) * (gv * gv)
        m_hat = mn / (1.0 - ADAM_B1 ** ADAM_STEP)
        v_hat = vn / (1.0 - ADAM_B2 ** ADAM_STEP)
        d_ref[...] = -ADAM_LR * (m_hat / (jnp.sqrt(v_hat) + ADAM_EPS) + ADAM_WD * w_ref[...])
        mo_ref[...] = mn
        vo_ref[...] = vn

    spec = pl.BlockSpec((tr, C), lambda i: (i, 0))
    shp = jax.ShapeDtypeStruct((R, C), F32)
    return _pc(body, name=name, grid=(R // tr,), in_specs=[spec] * 4, out_specs=[spec] * 3,
               out_shape=[shp] * 3, compiler_params=_cp(("parallel",)))(w, g, m, v)


_ADA_TN = 512


def adaln_fwd(name, cond, w, b):
    _, D, N = w.shape
    tn = min(_ADA_TN, N)

    def body(c_ref, w_ref, b_ref, o_ref):
        s = _silu(c_ref[...]).astype(BF16)
        o_ref[...] = dot_nn(s, w_ref[...].astype(BF16)) + b_ref[...]

    return _pc(body, name=name, grid=(2, N // tn),
               in_specs=[pl.BlockSpec((16, D), lambda l, n: (0, 0)),
                         pl.BlockSpec((None, D, tn), lambda l, n: (l, 0, n)),
                         pl.BlockSpec((None, 1, tn), lambda l, n: (l, 0, n))],
               out_specs=pl.BlockSpec((None, 16, tn), lambda l, n: (l, 0, n)),
               out_shape=jax.ShapeDtypeStruct((2, 16, N), F32),
               compiler_params=_cp(("parallel", "parallel")))(cond, w, b)


def adaln_bwd(name, cond, w, dm):
    _, D, N = w.shape
    tn = min(_ADA_TN, N)

    def body(c_ref, w_ref, dm_ref, gw_ref, ds_ref):
        first = jnp.logical_and(pl.program_id(0) == 0, pl.program_id(1) == 0)
        s = _silu(c_ref[...]).astype(BF16)
        dmb = dm_ref[...].astype(BF16)
        gw_ref[...] = dot_tn(s, dmb)
        p = dot_nt(dmb, w_ref[...].astype(BF16))

        @pl.when(first)
        def _():
            ds_ref[...] = p

        @pl.when(jnp.logical_not(first))
        def _():
            ds_ref[...] += p

    return _pc(body, name=name, grid=(2, N // tn),
               in_specs=[pl.BlockSpec((16, D), lambda l, n: (0, 0)),
                         pl.BlockSpec((None, D, tn), lambda l, n: (l, 0, n)),
                         pl.BlockSpec((None, 16, tn), lambda l, n: (l, 0, n))],
               out_specs=[pl.BlockSpec((None, D, tn), lambda l, n: (l, 0, n)),
                          pl.BlockSpec((16, D), lambda l, n: (0, 0))],
               out_shape=[jax.ShapeDtypeStruct((2, D, N), F32), jax.ShapeDtypeStruct((16, D), F32)],
               compiler_params=_cp(("arbitrary", "arbitrary")))(cond, w, dm)


def cctx_grad(name, parts, c_ctx):
    def body(p_ref, c_ref, o_ref):
        acc = p_ref[0]
        for j in range(1, N_CHIP):
            acc = acc + p_ref[j]
        o_ref[...] = acc * _dsilu(c_ref[...])

    return _pc(body, name=name, out_shape=jax.ShapeDtypeStruct(c_ctx.shape, F32))(parts, c_ctx)


def _pack(arrs, rows_mult=8):
    flat = jnp.concatenate([a.reshape(-1) for a in arrs])
    n = flat.shape[0]
    unit = rows_mult * LANE
    tot = -(-n // unit) * unit
    return jnp.concatenate([flat, jnp.zeros((tot - n,), F32)]).reshape(tot // LANE, LANE)


def _unpack(flat, shapes):
    out, o = [], 0
    for s in shapes:
        n = int(np.prod(s))
        out.append(flat[o:o + n].reshape(s))
        o += n
    return out


MOD_NAMES = ("sh1", "sc1", "g1", "sh2", "sc2", "g2")


def kernel(x, c, ctx, c_ctx, w_ada, b_ada, norm1_g, w_in, ret_decay, ret_gn_g, conv_dw_w, conv_dw_b, conv_ln_g, conv_ln_b, conv_pw, na_rpb, w_out, norm2_g, ffn_up, ffn_dw_w, ffn_dw_b, ffn_down, final_g, loss_target, m_c_ctx, m_w_ada, m_b_ada, m_norm1_g, m_w_in, m_ret_decay, m_ret_gn_g, m_conv_dw_w, m_conv_dw_b, m_conv_ln_g, m_conv_ln_b, m_conv_pw, m_na_rpb, m_w_out, m_norm2_g, m_ffn_up, m_ffn_dw_w, m_ffn_dw_b, m_ffn_down, m_final_g, v_c_ctx, v_w_ada, v_b_ada, v_norm1_g, v_w_in, v_ret_decay, v_ret_gn_g, v_conv_dw_w, v_conv_dw_b, v_conv_ln_g, v_conv_ln_b, v_conv_pw, v_na_rpb, v_w_out, v_norm2_g, v_ffn_up, v_ffn_dw_w, v_ffn_dw_b, v_ffn_down, v_final_g):
    cfg = make_cfg(D=x.shape[2], T=x.shape[1], TC=ctx.shape[1], RH=ret_decay.shape[2], CW=conv_dw_b.shape[1],
                   NH=na_rpb.shape[1], DFF=ffn_dw_b.shape[1] // 2)
    D, T = cfg.D, cfg.T
    W = dict(c_ctx=c_ctx, w_ada=w_ada, b_ada=b_ada, norm1_g=norm1_g, w_in=w_in, ret_decay=ret_decay, ret_gn_g=ret_gn_g,
             conv_dw_w=conv_dw_w, conv_dw_b=conv_dw_b, conv_ln_g=conv_ln_g, conv_ln_b=conv_ln_b, conv_pw=conv_pw,
             na_rpb=na_rpb, w_out=w_out, norm2_g=norm2_g, ffn_up=ffn_up, ffn_dw_w=ffn_dw_w, ffn_dw_b=ffn_dw_b,
             ffn_down=ffn_down, final_g=final_g)
    Mo = dict(c_ctx=m_c_ctx, w_ada=m_w_ada, b_ada=m_b_ada, norm1_g=m_norm1_g, w_in=m_w_in, ret_decay=m_ret_decay,
              ret_gn_g=m_ret_gn_g, conv_dw_w=m_conv_dw_w, conv_dw_b=m_conv_dw_b, conv_ln_g=m_conv_ln_g,
              conv_ln_b=m_conv_ln_b, conv_pw=m_conv_pw, na_rpb=m_na_rpb, w_out=m_w_out, norm2_g=m_norm2_g,
              ffn_up=m_ffn_up, ffn_dw_w=m_ffn_dw_w, ffn_dw_b=m_ffn_dw_b, ffn_down=m_ffn_down, final_g=m_final_g)
    Vo = dict(c_ctx=v_c_ctx, w_ada=v_w_ada, b_ada=v_b_ada, norm1_g=v_norm1_g, w_in=v_w_in, ret_decay=v_ret_decay,
              ret_gn_g=v_ret_gn_g, conv_dw_w=v_conv_dw_w, conv_dw_b=v_conv_dw_b, conv_ln_g=v_conv_ln_g,
              conv_ln_b=v_conv_ln_b, conv_pw=v_conv_pw, na_rpb=v_na_rpb, w_out=v_w_out, norm2_g=v_norm2_g,
              ffn_up=v_ffn_up, ffn_dw_w=v_ffn_dw_w, ffn_dw_b=v_ffn_dw_b, ffn_down=v_ffn_down, final_g=v_final_g)
    order = list(W)
    xi, yi, ci = lax.axis_index("x"), lax.axis_index("y"), lax.axis_index("c")
    chip = 2 * xi + yi
    dev = 4 * xi + 2 * yi + ci
    NA = w_ada.shape[2]
    ncw, nfw = conv_dw_w.shape[2], ffn_dw_w.shape[2]

    g_in = allgather8("ag_small_in", _pack([c[0], conv_dw_w, ffn_dw_w])).reshape(N_DEV, -1)
    c8 = g_in[:, :D]
    by_chip = g_in[0::2, D:]
    cw_parts, fw_parts = [], []
    for j in range(N_CHIP):
        a, b = _unpack(by_chip[j], [conv_dw_w.shape, ffn_dw_w.shape])
        cw_parts.append(a)
        fw_parts.append(b)
    conv_dw_w_full = jnp.concatenate(cw_parts, axis=2)
    ffn_dw_w_full = jnp.concatenate(fw_parts, axis=2)
    cond = jnp.concatenate([c8, c_ctx[None], jnp.zeros((16 - N_DEV - 1, D), F32)], axis=0)

    b_sh = lax.dynamic_slice(b_ada, (0, chip * NA), (2, NA)).reshape(2, 1, NA)
    m_sh = adaln_fwd("adaln_fwd", cond, w_ada, b_sh)
    m_all = allgather8("ag_mod", m_sh.reshape(2 * 16, NA)).reshape(N_DEV, 2, 16, NA)[0::2]
    m_all = m_all.transpose(1, 2, 0, 3).reshape(2, 16, N_CHIP * NA)
    mods = []
    for l in range(2):
        lat = lax.dynamic_slice(m_all[l], (dev, 0), (1, N_CHIP * NA))[0]
        cx = m_all[l, N_DEV]
        mods.append({nm: jnp.stack([lat[k * D:(k + 1) * D], cx[k * D:(k + 1) * D]], 0)[:, None, :]
                     for k, nm in enumerate(MOD_NAMES)})

    c_arr = jnp.reshape(ci, (1,)).astype(jnp.int32)
    chip_arr = jnp.reshape(chip, (1,)).astype(jnp.int32)
    first, rest = ("w_in", "conv_pw"), ("w_out", "ffn_up", "ffn_down")
    wb = [{nm: to_bf16_block(f"to_bf16_{nm}_{l}", W[nm], chip_arr, l) for nm in BIG} for l in range(2)]
    have = dict(zip([(0, nm) for nm in first], allgather_layer("ag_w0a", [wb[0][nm] for nm in first], 0)))
    flying_w = {}

    def start_gather(tag, l, names, after):
        s_sem, r_sem, bufs, tok = allgather_layer_start(f"ag_{tag}_start", [wb[l][nm] for nm in names], l, after)
        flying_w[(l, names[0])] = (tag, l, names, bufs, s_sem, r_sem)
        return tok[0, 0]

    def land_gather(key, after):
        tag, l, names, bufs, s_sem, r_sem = flying_w.pop(key)
        landed = allgather_layer_wait(f"ag_{tag}_wait", bufs, s_sem, r_sem, after, l)
        have.update(zip([(l, nm) for nm in names], forward_halves(f"ag_{tag}_fwd", list(landed), l)))

    mods[0] = {**mods[0], "sc1": mods[0]["sc1"] + start_gather("w0b", 0, rest, have[(0, first[0])])}

    def wts(l, name, after):
        tok = None
        if (l, name) not in have:
            if l == 0:
                land_gather((0, rest[0]), after)
                tok = start_gather("w1", 1, BIG, have[(0, rest[0])])
            else:
                land_gather((1, BIG[0]), after)
        return have[(l, name)], tok

    sp = dict(norm1_g=norm1_g, norm2_g=norm2_g, ret_decay=ret_decay, ret_gn_g=ret_gn_g, conv_dw_w=conv_dw_w_full,
              conv_dw_b=conv_dw_b, conv_ln_g=conv_ln_g, conv_ln_b=conv_ln_b, na_rpb=na_rpb, ffn_dw_w=ffn_dw_w_full,
              ffn_dw_b=ffn_dw_b, final_g=final_g)
    flights = []

    def grads_ready(l, names, gb):
        tag = f"{l}_{names[0]}"
        from_sib = exchange_rows(f"rs_exchange_{tag}", [gb[nm] for nm in names], l)
        part = [add_rows(f"rs_add_{nm}_{l}", gb[nm], r, c_arr, l) for nm, r in zip(names, from_sib)]
        lands = [own_row(f"rs_own_{nm}_{l}", p, chip_arr) for nm, p in zip(names, part)]
        if l == 0 and names == GRAD_GROUPS[-1]:
            flights.append((l, names, tag, part, lands, None))
            return None
        s_sem, r_sem, part, lands, tok = scatter_slices_start(f"rs_scatter_{tag}_start", part, lands)
        flights.append((l, names, tag, part, lands, (s_sem, r_sem)))
        return tok[0, 0]

    loss_l, gx, gb, gss, dms, dfg = local_step(cfg, x[0], ctx[0], loss_target[0], mods, wts, sp, grads_ready)
    loss = lax.psum(loss_l, ("x", "y", "c"))

    dmseg = jnp.stack([jnp.stack([jnp.concatenate([dms[l][nm][r, 0] for nm in MOD_NAMES]) for r in range(2)])
                       for l in range(2)])
    gsm = dict(
        norm1_g=jnp.stack([gss[l]["norm1_g"][0] for l in range(2)]),
        ret_decay=jnp.stack([gss[l]["lam"] * jax.nn.sigmoid(-ret_decay[l]) for l in range(2)]),
        ret_gn_g=jnp.stack([gss[l]["ret_gn_g"][0] for l in range(2)]),
        conv_dw_w=jnp.stack([gss[l]["conv_dw_w"][:cfg.CK] for l in range(2)]),
        conv_dw_b=jnp.stack([gss[l]["conv_dw_b"][0] for l in range(2)]),
        conv_ln_g=jnp.stack([gss[l]["conv_ln_g"][0] for l in range(2)]),
        conv_ln_b=jnp.stack([gss[l]["conv_ln_b"][0] for l in range(2)]),
        na_rpb=jnp.stack([gss[l]["na_rpb"] for l in range(2)]),
        norm2_g=jnp.stack([gss[l]["norm2_g"][0] for l in range(2)]),
        ffn_dw_w=jnp.stack([gss[l]["ffn_dw_w"][:, :3].transpose(1, 0, 2).reshape(3, 2 * cfg.DFF) for l in range(2)]),
        ffn_dw_b=jnp.stack([gss[l]["ffn_dw_b"].reshape(-1) for l in range(2)]),
        final_g=dfg)
    snames = list(gsm)
    sshapes = [dmseg.shape] + [gsm[nm].shape for nm in snames]
    packed = _pack([dmseg] + [gsm[nm] for nm in snames])
    g_all = allgather8("ag_small_grads", packed).reshape(N_DEV, packed.shape[0], LANE)
    summed = sum_leading("sum_small_grads", g_all).reshape(-1)
    dm_sum, *gsum = _unpack(summed, sshapes)
    gfull = dict(zip(snames, gsum))
    ndm = int(np.prod(dmseg.shape))
    dm_all = g_all.reshape(N_DEV, -1)[:, :ndm].reshape(N_DEV, 2, 2, 6 * D)
    gfull["b_ada"] = sum_leading("sum_b_ada", dm_all.transpose(0, 2, 1, 3).reshape(2 * N_DEV, 2 * 6 * D // LANE, LANE)
                                 ).reshape(2, 6 * D)

    dm16 = jnp.concatenate([dm_all[:, :, 0].transpose(1, 0, 2), dm_sum[:, 1][:, None],
                            jnp.zeros((2, 16 - N_DEV - 1, 6 * D), F32)], axis=1)
    dm16 = lax.dynamic_slice(dm16, (0, 0, chip * NA), (2, 16, NA))
    gfull["w_ada"], ds16 = adaln_bwd("adaln_bwd", cond, w_ada, dm16)
    ds_all = allgather8("ag_dsilu", ds16[8:16]).reshape(N_DEV, 8, D)[0::2, 0:1]
    gfull["c_ctx"] = cctx_grad("cctx_grad", ds_all, c_ctx[None])[0]
    gfull["conv_dw_w"] = lax.dynamic_slice(gfull["conv_dw_w"], (0, 0, chip * ncw), (2, cfg.CK, ncw))
    gfull["ffn_dw_w"] = lax.dynamic_slice(gfull["ffn_dw_w"], (0, 0, chip * nfw), (2, 3, nfw))

    fin = {}
    for l, names, tag, part, lands, sems in flights:
        if sems is None:
            landed = scatter_slices(f"rs_scatter_{tag}", part, lands)
        else:
            landed = scatter_slices_wait(f"rs_scatter_{tag}_wait", part, lands, *sems, gx)
        for nm, p in zip(names, landed):
            fin[nm] = sum_rows_into(f"rs_sum_{nm}_{l}", p, c_arr, l, fin.get(nm))
    for nm, gfin in zip(BIG, share_rows("rs_share", [fin[nm] for nm in BIG])):
        gfull[nm] = gfin.reshape(W[nm].shape)

    delta, new_m, new_v = {}, {}, {}
    bigs = ("w_ada",) + BIG
    for nm in bigs:
        shp = W[nm].shape
        v2 = lambda a: a.reshape(-1, shp[-1])
        d_, m_, v_ = adamw(f"adamw_{nm}", v2(W[nm]), v2(gfull[nm]), v2(Mo[nm]), v2(Vo[nm]))
        delta[nm], new_m[nm], new_v[nm] = d_.reshape(shp), m_.reshape(shp), v_.reshape(shp)
    smalls = [nm for nm in order if nm not in bigs]
    shapes = [W[nm].shape for nm in smalls]
    d_, m_, v_ = adamw("adamw_small", _pack([W[nm] for nm in smalls]), _pack([gfull[nm] for nm in smalls]),
                       _pack([Mo[nm] for nm in smalls]), _pack([Vo[nm] for nm in smalls]))
    for nm, a, b, e in zip(smalls, _unpack(d_.reshape(-1), shapes), _unpack(m_.reshape(-1), shapes),
                           _unpack(v_.reshape(-1), shapes)):
        delta[nm], new_m[nm], new_v[nm] = a, b, e
    return (loss, gx[None], *[gfull[nm] for nm in order], *[delta[nm] for nm in order],
            *[new_m[nm] for nm in order], *[new_v[nm] for nm in order])
```

```python
import collections
import functools

import numpy as np
import jax
import jax.numpy as jnp
from jax import lax
from jax.experimental import pallas as pl
from jax.experimental.pallas import tpu as pltpu

F32 = jnp.float32
BF16 = jnp.bfloat16
EPS = 1e-6
ROPE_BASE = 10000.0
NEG = -1e30
LANE = 128
VMEM_LIMIT = 56 * 1024 * 1024

ADAM_LR, ADAM_B1, ADAM_B2, ADAM_EPS, ADAM_WD, ADAM_STEP = 0.001, 0.9, 0.999, 1e-08, 0.01, 10

Cfg = collections.namedtuple(
    "Cfg", "D T TC GW RH RDK RDV CW CK NH NDH NAR NAC DFF TB")


def make_cfg(D=2048, T=4096, TC=256, RH=4, CW=512, NH=4, DFF=5632):
    return Cfg(D=D, T=T, TC=TC, GW=64, RH=RH, RDK=128, RDV=256, CW=CW, CK=31, NH=NH, NDH=128,
               NAR=8, NAC=16, DFF=DFF, TB=256)


def _offsets(cfg):
    sizes = [cfg.RH * cfg.RDK, cfg.RH * cfg.RDK, cfg.RH * cfg.RDV, cfg.RH * cfg.RDV, cfg.CW, cfg.CW,
             cfg.NH * cfg.NDH, cfg.NH * cfg.NDH, cfg.NH * cfg.NDH]
    offs = [0]
    for s in sizes:
        offs.append(offs[-1] + s)
    return dict(zip(["lq", "lk", "lv", "lg", "la", "lb", "nq", "nk", "nv", "end"], offs))


def _pc(body, **kw):
    return pl.pallas_call(body, **kw)


def _cp(sem=None):
    return pltpu.CompilerParams(dimension_semantics=sem, vmem_limit_bytes=VMEM_LIMIT)


def _dot(a, b, ca, cb):
    return lax.dot_general(a, b, (((ca,), (cb,)), ((), ())), preferred_element_type=F32)


def dot_nn(a, b):
    return _dot(a, b, 1, 0)


def dot_nt(a, b):
    return _dot(a, b, 1, 1)


def dot_tn(a, b):
    return _dot(a, b, 0, 0)


def _sigmoid(x):
    return 1.0 / (1.0 + jnp.exp(-x))


def _silu(x):
    return x * _sigmoid(x)


def _dsilu(x):
    s = _sigmoid(x)
    return s * (1.0 + x * (1.0 - s))


def matmul(name, a, b, *, contract, grid, a_spec, b_spec, out_shape, out_spec, nk, into=None):
    dot = {"nn": dot_nn, "nt": dot_nt, "tn": dot_tn}[contract]
    direct = nk > 1 and out_shape.dtype == F32
    kax = len(grid) - 1

    def body(a_ref, b_ref, *rest):
        o_ref, *scr = rest[1:] if into is not None else rest
        p = dot(a_ref[...].astype(BF16), b_ref[...].astype(BF16))
        if nk == 1:
            o_ref[...] = p.astype(o_ref.dtype)
            return
        acc = o_ref if direct else scr[0]
        k = pl.program_id(kax)

        @pl.when(k == 0)
        def _():
            acc[...] = p

        @pl.when(k > 0)
        def _():
            acc[...] += p

        if not direct:
            @pl.when(k == nk - 1)
            def _():
                o_ref[...] = acc[...].astype(o_ref.dtype)

    scratch = []
    if nk > 1 and not direct:
        blk = [s for s in out_spec.block_shape if s is not None]
        scratch = [pltpu.VMEM(tuple(blk), F32)]
    sem = ("parallel",) * kax + (("arbitrary",) if nk > 1 else ("parallel",))
    in_specs, args, alias = [a_spec, b_spec], (a, b), {}
    if into is not None:
        in_specs, args, alias = in_specs + [pl.BlockSpec(memory_space=pl.ANY)], (a, b, into), {2: 0}
    return _pc(body, name=name, grid=grid, in_specs=in_specs, out_specs=out_spec, out_shape=out_shape,
               scratch_shapes=scratch, input_output_aliases=alias, compiler_params=_cp(sem))(*args)


_WG_ROWS = 1024


def wgrad(cfg, name, a, dc, a_spec, dc_spec, out_shape, out_spec, ntiles, into):
    T, TC = cfg.T, cfg.TC
    tml = min(_WG_ROWS, T)
    nl = T // tml

    def body(al_ref, ac_ref, dl_ref, dcx_ref, *rest):
        o_ref, acc = rest[-2:]
        m = pl.program_id(1)

        @pl.when(m == 0)
        def _():
            acc[...] = dot_tn(al_ref[...], dl_ref[...])

        @pl.when(jnp.logical_and(m > 0, m < nl))
        def _():
            acc[...] += dot_tn(al_ref[...], dl_ref[...])

        @pl.when(m == nl)
        def _():
            o_ref[...] = (acc[...] + dot_tn(ac_ref[...], dcx_ref[...])).astype(o_ref.dtype)

    lat = lambda m: jnp.minimum(m, nl - 1)
    ctx = lambda m: T // TC
    in_specs = [a_spec(tml, lat), a_spec(TC, ctx), dc_spec(tml, lat), dc_spec(TC, ctx)]
    args, alias = (a, a, dc, dc), {}
    if into is not None:
        in_specs, args, alias = in_specs + [pl.BlockSpec(memory_space=pl.ANY)], args + (into,), {4: 0}
    blk = tuple(s for s in out_spec.block_shape if s is not None)
    return _pc(body, name=name, grid=(ntiles, nl + 1), in_specs=in_specs, out_specs=out_spec, out_shape=out_shape,
               scratch_shapes=[pltpu.VMEM(blk, F32)], input_output_aliases=alias,
               compiler_params=_cp(("parallel", "arbitrary")))(*args)


def mm_rowsharded(name, a, w4, l, out_dtype, tn):
    L, K = a.shape
    nch, _, Kb, N = w4.shape
    tm = 256

    def body(a_ref, w_ref, o_ref):
        acc = dot_nn(a_ref[:, 0:Kb], w_ref[0])
        for j in range(1, nch):
            acc += dot_nn(a_ref[:, j * Kb:(j + 1) * Kb], w_ref[j])
        o_ref[...] = acc.astype(o_ref.dtype)

    return _pc(body, name=name, grid=(N // tn, L // tm),
               in_specs=[pl.BlockSpec((tm, K), lambda n, m: (m, 0)),
                         pl.BlockSpec((nch, None, Kb, tn), lambda n, m: (0, l, 0, n))],
               out_specs=pl.BlockSpec((tm, tn), lambda n, m: (m, n)),
               out_shape=jax.ShapeDtypeStruct((L, N), out_dtype),
               compiler_params=_cp(("parallel", "parallel")))(a, w4)


def _region(cfg):
    nlat = cfg.T // cfg.TB
    return lambda i: jnp.minimum(i // nlat, 1)


def norm_mod_fwd(cfg, name, x, ng, sc, sh):
    L, D = x.shape
    TB = cfg.TB
    reg = _region(cfg)

    def body(x_ref, ng_ref, sc_ref, sh_ref, h_ref):
        xv = x_ref[...]
        r = lax.rsqrt(jnp.mean(xv * xv, axis=-1, keepdims=True) + EPS)
        n = xv * r * ng_ref[...]
        h_ref[...] = (n * (1.0 + sc_ref[...]) + sh_ref[...]).astype(h_ref.dtype)

    row = pl.BlockSpec((TB, D), lambda i: (i, 0))
    vec = pl.BlockSpec((1, D), lambda i: (0, 0))
    rvec = pl.BlockSpec((None, 1, D), lambda i: (reg(i), 0, 0))
    return _pc(body, name=name, grid=(L // TB,), in_specs=[row, vec, rvec, rvec], out_specs=row,
               out_shape=jax.ShapeDtypeStruct((L, D), BF16), compiler_params=_cp(("parallel",)))(x, ng, sc, sh)


def norm_mod_bwd(cfg, name, dh, x, ng, sc, dx_in, latent_only=False):
    L, D = x.shape
    TB = cfg.TB
    nlat = cfg.T // TB
    reg = _region(cfg)

    def body(dh_ref, x_ref, ng_ref, sc_ref, dxi_ref, dx_ref, dsc_ref, dsh_ref, dng_ref):
        i = pl.program_id(0)
        xv = x_ref[...]
        r = lax.rsqrt(jnp.mean(xv * xv, axis=-1, keepdims=True) + EPS)
        xh = xv * r
        g = ng_ref[...]
        n = xh * g
        dh = dh_ref[...]
        dn = dh * (1.0 + sc_ref[...])
        dxh = dn * g
        dx = r * (dxh - xh * jnp.mean(dxh * xh, axis=-1, keepdims=True))
        if latent_only:
            @pl.when(i < nlat)
            def _():
                dx_ref[...] = dxi_ref[...] + dx
        else:
            dx_ref[...] = dxi_ref[...] + dx
        s_sh = jnp.sum(dh, axis=0, keepdims=True)
        s_sc = jnp.sum(dh * n, axis=0, keepdims=True)
        s_ng = jnp.sum(dn * xh, axis=0, keepdims=True)
        first = jnp.logical_or(i == 0, i == nlat)

        @pl.when(first)
        def _():
            dsh_ref[...] = s_sh
            dsc_ref[...] = s_sc

        @pl.when(jnp.logical_not(first))
        def _():
            dsh_ref[...] += s_sh
            dsc_ref[...] += s_sc

        @pl.when(i == 0)
        def _():
            dng_ref[...] = s_ng

        @pl.when(i > 0)
        def _():
            dng_ref[...] += s_ng

    row = pl.BlockSpec((TB, D), lambda i: (i, 0))
    vec = pl.BlockSpec((1, D), lambda i: (0, 0))
    rvec = pl.BlockSpec((None, 1, D), lambda i: (reg(i), 0, 0))
    dxs = pl.BlockSpec((TB, D), lambda i: (jnp.minimum(i, nlat - 1), 0)) if latent_only else row
    return _pc(body, name=name, grid=(L // TB,), in_specs=[row, row, vec, rvec, row],
               out_specs=[dxs, rvec, rvec, vec],
               out_shape=[jax.ShapeDtypeStruct((cfg.T if latent_only else L, D), F32),
                          jax.ShapeDtypeStruct((2, 1, D), F32),
                          jax.ShapeDtypeStruct((2, 1, D), F32), jax.ShapeDtypeStruct((1, D), F32)],
               compiler_params=_cp(("arbitrary",)))(dh, x, ng, sc, dx_in)


def resid_fwd(cfg, name, x, y, g):
    L, D = x.shape
    TB = cfg.TB
    reg = _region(cfg)

    def body(x_ref, y_ref, g_ref, o_ref):
        o_ref[...] = x_ref[...] + g_ref[...] * y_ref[...]

    row = pl.BlockSpec((TB, D), lambda i: (i, 0))
    rvec = pl.BlockSpec((None, 1, D), lambda i: (reg(i), 0, 0))
    return _pc(body, name=name, grid=(L // TB,), in_specs=[row, row, rvec], out_specs=row,
               out_shape=jax.ShapeDtypeStruct((L, D), F32), compiler_params=_cp(("parallel",)))(x, y, g)


def resid_bwd(cfg, name, dxo, y, g):
    L, D = y.shape
    TB = cfg.TB
    nlat = cfg.T // TB
    reg = _region(cfg)

    def body(d_ref, y_ref, g_ref, dy_ref, dg_ref):
        i = pl.program_id(0)
        d = d_ref[...]
        dy_ref[...] = (d * g_ref[...]).astype(dy_ref.dtype)
        s = jnp.sum(d * y_ref[...], axis=0, keepdims=True)
        first = jnp.logical_or(i == 0, i == nlat)

        @pl.when(first)
        def _():
            dg_ref[...] = s

        @pl.when(jnp.logical_not(first))
        def _():
            dg_ref[...] += s

    row = pl.BlockSpec((TB, D), lambda i: (i, 0))
    rvec = pl.BlockSpec((None, 1, D), lambda i: (reg(i), 0, 0))
    return _pc(body, name=name, grid=(L // TB,), in_specs=[row, row, rvec], out_specs=[row, rvec],
               out_shape=[jax.ShapeDtypeStruct((L, D), BF16), jax.ShapeDtypeStruct((2, 1, D), F32)],
               compiler_params=_cp(("arbitrary",)))(dxo, y, g)


def final_loss(cfg, name, x, fg, tgt):
    L, D = x.shape
    TB = cfg.TB
    nlat = cfg.T // TB

    def body(x_ref, fg_ref, t_ref, ls_ref, dx_ref, dg_ref):
        i = pl.program_id(0)

        @pl.when(i == 0)
        def _():
            ls_ref[...] = jnp.zeros_like(ls_ref)
            dg_ref[...] = jnp.zeros_like(dg_ref)

        @pl.when(i < nlat)
        def _():
            xv = x_ref[...]
            r = lax.rsqrt(jnp.mean(xv * xv, axis=-1, keepdims=True) + EPS)
            xh = xv * r
            g = fg_ref[...]
            e = xh * g - t_ref[...]
            ls_ref[...] += 0.5 * jnp.sum(e * e) / D
            dy = e / D
            dg_ref[...] += jnp.sum(dy * xh, axis=0, keepdims=True)
            dxh = dy * g
            dx_ref[...] = r * (dxh - xh * jnp.mean(dxh * xh, axis=-1, keepdims=True))

        @pl.when(i >= nlat)
        def _():
            dx_ref[...] = jnp.zeros_like(dx_ref)

    row = pl.BlockSpec((TB, D), lambda i: (i, 0))
    trow = pl.BlockSpec((TB, D), lambda i: (jnp.minimum(i, nlat - 1), 0))
    vec = pl.BlockSpec((1, D), lambda i: (0, 0))
    return _pc(body, name=name, grid=(L // TB,), in_specs=[row, vec, trow],
               out_specs=[pl.BlockSpec((1, LANE), lambda i: (0, 0)), row, vec],
               out_shape=[jax.ShapeDtypeStruct((1, LANE), F32), jax.ShapeDtypeStruct((L, D), F32),
                          jax.ShapeDtypeStruct((1, D), F32)],
               compiler_params=_cp(("arbitrary",)))(x, fg, tgt)


def rope_tables(cfg):
    half = cfg.RDK // 2
    nf = half // 2
    pos = np.arange(cfg.T)
    row = (pos // cfg.GW).astype(np.float32)
    col = (pos % cfg.GW).astype(np.float32)
    inv = jnp.asarray(ROPE_BASE, F32) ** (-jnp.arange(nf, dtype=F32) / nf)
    ar = jnp.asarray(row)[:, None] * inv[None, :]
    ac = jnp.asarray(col)[:, None] * inv[None, :]
    cos = jnp.concatenate([jnp.cos(ar), jnp.cos(ar), jnp.cos(ac), jnp.cos(ac)], axis=1)
    sin = jnp.concatenate([-jnp.sin(ar), jnp.sin(ar), -jnp.sin(ac), jnp.sin(ac)], axis=1)
    cos = jnp.concatenate([cos, jnp.ones((cfg.TC, cfg.RDK), F32)], axis=0)
    sin = jnp.concatenate([sin, jnp.zeros((cfg.TC, cfg.RDK), F32)], axis=0)
    return cos, sin


def _rb(cfg):
    rb = (cfg.T + cfg.TC) // 4
    assert rb % 16 == 0
    return rb


def _swap32(t):
    lane = lax.broadcasted_iota(jnp.int32, t.shape, 1)
    return jnp.where((lane % 64) < 32, pltpu.roll(t, 96, 1), pltpu.roll(t, 32, 1))


def rope_fwd(cfg, name, P, cos, sin):
    L = P.shape[0]
    TB = _rb(cfg)
    off = _offsets(cfg)
    cq, ck = off["lq"] // LANE, off["lk"] // LANE
    scale = cfg.RDK ** -0.5

    def body(q_ref, k_ref, c_ref, s_ref, qo_ref, ko_ref):
        c = c_ref[...]
        s = s_ref[...]
        q = q_ref[...]
        k = k_ref[...]
        qo_ref[...] = (q * c + _swap32(q) * s) * scale
        ko_ref[...] = k * c + _swap32(k) * s

    tab = pl.BlockSpec((TB, LANE), lambda i, h: (i, 0))
    out = pl.BlockSpec((TB, LANE), lambda i, h: (i, h))
    shp = jax.ShapeDtypeStruct((L, cfg.RH * cfg.RDK), F32)
    return _pc(body, name=name, grid=(L // TB, cfg.RH),
               in_specs=[pl.BlockSpec((TB, LANE), lambda i, h: (i, cq + h)),
                         pl.BlockSpec((TB, LANE), lambda i, h: (i, ck + h)), tab, tab],
               out_specs=[out, out], out_shape=[shp, shp],
               compiler_params=_cp(("parallel", "parallel")))(P, P, cos, sin)


def rope_bwd(cfg, name, dq2, dk2, cos, sin):
    L, W = dq2[0].shape
    TB = _rb(cfg)
    scale = cfg.RDK ** -0.5

    def body(dqf_ref, dqb_ref, dkf_ref, dkb_ref, c_ref, s_ref, qo_ref, ko_ref):
        c = c_ref[...]
        s = s_ref[...]
        dq = dqf_ref[...] + dqb_ref[...]
        dk = dkf_ref[...] + dkb_ref[...]
        qo_ref[...] = ((dq * c - _swap32(dq) * s) * scale).astype(qo_ref.dtype)
        ko_ref[...] = (dk * c - _swap32(dk) * s).astype(ko_ref.dtype)

    tab = pl.BlockSpec((TB, LANE), lambda i, h: (i, 0))
    blk = pl.BlockSpec((TB, LANE), lambda i, h: (i, h))
    shp = jax.ShapeDtypeStruct((L, W), BF16)
    return _pc(body, name=name, grid=(L // TB, cfg.RH), in_specs=[blk, blk, blk, blk, tab, tab],
               out_specs=[blk, blk], out_shape=[shp, shp],
               compiler_params=_cp(("parallel", "parallel")))(*dq2, *dk2, cos, sin)


def _ret_chunk_map(cfg):
    C = cfg.RDK
    n = (cfg.T + cfg.TC) // C
    nlat, nctx = cfg.T // C, cfg.TC // C

    def chunk(d, s):
        if d == 0:
            return jnp.where(s < nctx, nlat + s, s - nctx)
        return n - 1 - s

    return n, chunk


def _ret_decay_terms(d, lam, C):
    ii = lax.broadcasted_iota(jnp.int32, (C, C), 0)
    jj = lax.broadcasted_iota(jnp.int32, (C, C), 1)
    diff = (ii - jj if d == 0 else jj - ii).astype(F32)
    dpos = jnp.maximum(diff, 0.0)
    Dm = jnp.where(diff >= 0, jnp.exp(dpos * lam), 0.0)
    ic = lax.broadcasted_iota(jnp.int32, (C, 1), 0).astype(F32)
    cxi = ic + 1.0 if d == 0 else C - ic
    cze = C - 1.0 - ic if d == 0 else ic
    xi = jnp.exp(cxi * lam)
    ze = jnp.exp(cze * lam)
    g = jnp.exp(jnp.full((1, 1), C, F32) * lam)
    return dpos, Dm, cxi, cze, xi, ze, g


def retention_fwd(cfg, name, qr, kr, P, lam):
    L = P.shape[0]
    C, DV, RH = cfg.RDK, cfg.RDV, cfg.RH
    n, chunk = _ret_chunk_map(cfg)

    def body(lam_ref, qf_ref, qb_ref, kf_ref, kb_ref, vf_ref, vb_ref, of_ref, ob_ref, st_ref, S):
        s = pl.program_id(0)

        @pl.when(s == 0)
        def _():
            S[...] = jnp.zeros_like(S)

        for d, (q_ref, k_ref, v_ref, o_ref) in enumerate(((qf_ref, kf_ref, vf_ref, of_ref),
                                                          (qb_ref, kb_ref, vb_ref, ob_ref))):
            for h in range(RH):
                _, Dm, _, _, xi, ze, g = _ret_decay_terms(d, lam_ref[d, h], C)
                k = k_ref[:, h * C:(h + 1) * C]
                qb = q_ref[:, h * C:(h + 1) * C].astype(BF16)
                kb = k.astype(BF16)
                vb = v_ref[:, h * DV:(h + 1) * DV].astype(BF16)
                Sv = S[d, h]
                st_ref[d, h] = Sv
                A = dot_nt(qb, kb) * Dm
                o_ref[:, h * DV:(h + 1) * DV] = dot_nn(A.astype(BF16), vb) + dot_nn(qb, Sv.astype(BF16)) * xi
                S[d, h] = Sv * g + dot_tn((k * ze).astype(BF16), vb)

    def spec(w, col, d):
        return pl.BlockSpec((C, w), lambda s: (chunk(d, s), col))

    W, WV = RH * C, RH * DV
    return _pc(body, name=name, grid=(n,),
               in_specs=[pl.BlockSpec(memory_space=pltpu.SMEM), spec(W, 0, 0), spec(W, 0, 1), spec(W, 0, 0),
                         spec(W, 0, 1), spec(WV, 1, 0), spec(WV, 1, 1)],
               out_specs=[spec(WV, 0, 0), spec(WV, 0, 1),
                          pl.BlockSpec((2, RH, None, C, DV), lambda s: (0, 0, s, 0, 0))],
               out_shape=[jax.ShapeDtypeStruct((L, WV), F32), jax.ShapeDtypeStruct((L, WV), F32),
                          jax.ShapeDtypeStruct((2, RH, n, C, DV), F32)],
               scratch_shapes=[pltpu.VMEM((2, RH, C, DV), F32)],
               compiler_params=_cp(("arbitrary",)))(lam, qr, qr, kr, kr, P, P)


def retention_bwd(cfg, name, qr, kr, P, lam, st, do):
    L = P.shape[0]
    C, DV, RH = cfg.RDK, cfg.RDV, cfg.RH
    n, chunk = _ret_chunk_map(cfg)

    def body(lam_ref, qf_ref, qb_ref, kf_ref, kb_ref, vf_ref, vb_ref, st_ref, dof_ref, dob_ref,
             dqf_ref, dqb_ref, dkf_ref, dkb_ref, dvf_ref, dvb_ref, dl_ref, dS):
        si = pl.program_id(0)

        @pl.when(si == 0)
        def _():
            dS[...] = jnp.zeros_like(dS)
            dl_ref[...] = jnp.zeros_like(dl_ref)

        dirs = ((qf_ref, kf_ref, vf_ref, dof_ref, dqf_ref, dkf_ref, dvf_ref),
                (qb_ref, kb_ref, vb_ref, dob_ref, dqb_ref, dkb_ref, dvb_ref))
        for d, (q_ref, k_ref, v_ref, do_ref, dq_ref, dk_ref, dv_ref) in enumerate(dirs):
            for h in range(RH):
                dpos, Dm, cxi, cze, xi, ze, g = _ret_decay_terms(d, lam_ref[d, h], C)
                hk = slice(h * C, (h + 1) * C)
                hv = slice(h * DV, (h + 1) * DV)
                k = k_ref[:, hk]
                do = do_ref[:, hv]
                qb = q_ref[:, hk].astype(BF16)
                kb = k.astype(BF16)
                vb = v_ref[:, hv].astype(BF16)
                dob = do.astype(BF16)
                Sn = st_ref[d, h]
                Snb = Sn.astype(BF16)
                dSn = dS[d, h]
                dSb = dSn.astype(BF16)
                A = dot_nt(qb, kb) * Dm
                dA = dot_nt(dob, vb)
                dQK = (dA * Dm).astype(BF16)
                kzb = (k * ze).astype(BF16)
                dv_ref[:, hv] = dot_tn(A.astype(BF16), dob) + dot_nn(kzb, dSb)
                dkz = dot_nt(vb, dSb)
                doxb = (do * xi).astype(BF16)
                dq_ref[:, hk] = dot_nn(dQK, kb) + dot_nt(doxb, Snb)
                dk_ref[:, hk] = dot_tn(dQK, qb) + dkz * ze
                QS = dot_nn(qb, Snb)
                t = (jnp.sum(dA * A * dpos) + jnp.sum(do * QS * (cxi * xi)) + jnp.sum(k * dkz * (cze * ze)))
                t4 = jnp.sum(dSn * Sn, axis=0, keepdims=True)
                t4 = jnp.sum(t4 * (g * C), axis=1, keepdims=True)
                dl_ref[d, h] += t + t4
                dS[d, h] = g * dSn + dot_tn(qb, doxb)

    def spec(w, col, d):
        return pl.BlockSpec((C, w), lambda si: (chunk(d, n - 1 - si), col))

    W, WV = RH * C, RH * DV
    return _pc(body, name=name, grid=(n,),
               in_specs=[pl.BlockSpec(memory_space=pltpu.SMEM), spec(W, 0, 0), spec(W, 0, 1), spec(W, 0, 0),
                         spec(W, 0, 1), spec(WV, 1, 0), spec(WV, 1, 1),
                         pl.BlockSpec((2, RH, None, C, DV), lambda si: (0, 0, n - 1 - si, 0, 0)),
                         spec(WV, 0, 0), spec(WV, 0, 1)],
               out_specs=[spec(W, 0, 0), spec(W, 0, 1), spec(W, 0, 0), spec(W, 0, 1), spec(WV, 0, 0), spec(WV, 0, 1),
                          pl.BlockSpec((2, RH, 8, LANE), lambda si: (0, 0, 0, 0))],
               out_shape=[jax.ShapeDtypeStruct((L, W), F32)] * 4 + [jax.ShapeDtypeStruct((L, WV), F32)] * 2
               + [jax.ShapeDtypeStruct((2, RH, 8, LANE), F32)],
               scratch_shapes=[pltpu.VMEM((2, RH, C, DV), F32)],
               compiler_params=_cp(("arbitrary",)))(lam, qr, qr, kr, kr, P, P, st, do, do)


def add_cast(cfg, name, a, b):
    L, W = a.shape
    TB = _rb(cfg)

    def body(a_ref, b_ref, o_ref):
        o_ref[...] = (a_ref[...] + b_ref[...]).astype(o_ref.dtype)

    spec = pl.BlockSpec((TB, W), lambda i: (i, 0))
    return _pc(body, name=name, grid=(L // TB,), in_specs=[spec, spec], out_specs=spec,
               out_shape=jax.ShapeDtypeStruct((L, W), BF16), compiler_params=_cp(("parallel",)))(a, b)


def ggn_fwd(cfg, name, o2, P, gn_g):
    L = P.shape[0]
    TB, DV, RH = _rb(cfg), cfg.RDV, cfg.RH
    gc0 = _offsets(cfg)["lg"] // DV

    def body(of_ref, ob_ref, gate_ref, g_ref, out_ref):
        o = of_ref[...] + ob_ref[...]
        mu = jnp.mean(o, axis=-1, keepdims=True)
        xc = o - mu
        var = jnp.mean(xc * xc, axis=-1, keepdims=True)
        y = xc * lax.rsqrt(var + EPS) * g_ref[...]
        out_ref[...] = (y * _silu(gate_ref[...])).astype(out_ref.dtype)

    blk = pl.BlockSpec((TB, DV), lambda i, h: (i, h))
    return _pc(body, name=name, grid=(L // TB, RH),
               in_specs=[blk, blk, pl.BlockSpec((TB, DV), lambda i, h: (i, gc0 + h)),
                         pl.BlockSpec((1, DV), lambda i, h: (0, h))],
               out_specs=blk, out_shape=jax.ShapeDtypeStruct((L, RH * DV), BF16),
               compiler_params=_cp(("parallel", "parallel")))(*o2, P, gn_g)


def ggn_bwd(cfg, name, dout, o2, P, gn_g, col0):
    L = P.shape[0]
    TB, DV, RH = _rb(cfg), cfg.RDV, cfg.RH
    gc0 = _offsets(cfg)["lg"] // DV

    def body(d_ref, of_ref, ob_ref, gate_ref, g_ref, do_ref, dgate_ref, dg_ref):
        i = pl.program_id(1)
        o = of_ref[...] + ob_ref[...]
        mu = jnp.mean(o, axis=-1, keepdims=True)
        xc = o - mu
        var = jnp.mean(xc * xc, axis=-1, keepdims=True)
        r = lax.rsqrt(var + EPS)
        y = xc * r
        g = g_ref[...]
        gate = gate_ref[...]
        d = d_ref[...]
        dgate_ref[...] = (d * (y * g) * _dsilu(gate)).astype(dgate_ref.dtype)
        dyg = d * _silu(gate)
        s = jnp.sum(dyg * y, axis=0, keepdims=True)

        @pl.when(i == 0)
        def _():
            dg_ref[...] = s

        @pl.when(i > 0)
        def _():
            dg_ref[...] += s

        dy = dyg * g
        do_ref[...] = r * (dy - jnp.mean(dy, axis=-1, keepdims=True)
                           - y * jnp.mean(dy * y, axis=-1, keepdims=True))

    blk = pl.BlockSpec((TB, DV), lambda h, i: (i, h))
    return _pc(body, name=name, grid=(RH, L // TB),
               in_specs=[pl.BlockSpec((TB, DV), lambda h, i: (i, col0 + h)), blk, blk,
                         pl.BlockSpec((TB, DV), lambda h, i: (i, gc0 + h)),
                         pl.BlockSpec((1, DV), lambda h, i: (0, h))],
               out_specs=[blk, blk, pl.BlockSpec((1, DV), lambda h, i: (0, h))],
               out_shape=[jax.ShapeDtypeStruct((L, RH * DV), F32), jax.ShapeDtypeStruct((L, RH * DV), BF16),
                          jax.ShapeDtypeStruct((1, RH * DV), F32)],
               compiler_params=_cp(("parallel", "arbitrary")))(dout, *o2, P, gn_g)


def cast_cols(cfg, name, src, col0, ncols, width):
    L = src.shape[0]
    TB = _rb(cfg)

    def body(s_ref, o_ref):
        o_ref[...] = s_ref[...].astype(o_ref.dtype)

    spec = pl.BlockSpec((TB, width), lambda i, j: (i, col0 + j))
    return _pc(body, name=name, grid=(L // TB, ncols), in_specs=[spec],
               out_specs=pl.BlockSpec((TB, width), lambda i, j: (i, j)),
               out_shape=jax.ShapeDtypeStruct((L, ncols * width), BF16),
               compiler_params=_cp(("parallel", "parallel")))(src)


_CPAD = 16


def _conv_windows(cfg):
    T, TC, TB = cfg.T, cfg.TC, cfg.TB
    assert TC % TB == 0 and T % TB == 0 and cfg.CK // 2 < _CPAD
    return T // TB, [(T + j * TB, T + _CPAD + j * TB) for j in range(TC // TB)]


def _fill_padded(cfg, pb, get):
    T, TC, TB = cfg.T, cfg.TC, cfg.TB
    z = jnp.zeros((_CPAD, LANE), F32)
    pb[0:_CPAD, :] = z
    pb[_CPAD + T:2 * _CPAD + T, :] = z
    pb[2 * _CPAD + T + TC:3 * _CPAD + T + TC, :] = z

    def fill(i, c):
        r0 = pl.multiple_of(i * TB, TB)
        pb[pl.ds(r0 + _CPAD, TB), :] = get(r0)
        return c

    lax.fori_loop(0, T // TB, fill, 0)
    for j in range(TC // TB):
        pb[2 * _CPAD + T + j * TB:2 * _CPAD + T + (j + 1) * TB, :] = get(T + j * TB)


def _taps(win, TB):
    W = TB + 2 * _CPAD
    rot = {0: win}

    def tap(k):
        a, b = divmod(k + 1, 8)
        if b not in rot:
            rot[b] = pltpu.roll(win, W - b, 0)
        return rot[b][8 * a:8 * a + TB, :]

    return tap


def glu_dwconv_fwd(cfg, name, P, w, b):
    L = P.shape[0]
    T, TC, TB, K = cfg.T, cfg.TC, cfg.TB, cfg.CK
    off = _offsets(cfg)
    ca, cb = off["la"] // LANE, off["lb"] // LANE
    nlat, ctx_tiles = _conv_windows(cfg)
    PBL = 3 * _CPAD + T + TC

    def body(a_ref, b_ref, w_ref, bias_ref, y_ref, pb):
        _fill_padded(cfg, pb, lambda r0: a_ref[pl.ds(r0, TB), :] * _sigmoid(b_ref[pl.ds(r0, TB), :]))
        wv = w_ref[...]
        bias = bias_ref[...]

        def tile(win):
            tap = _taps(win, TB)
            acc = jnp.zeros((TB, LANE), F32) + bias
            for k in range(K):
                acc = acc + wv[k:k + 1, :] * tap(k)
            return acc

        def lat(i, c):
            r0 = pl.multiple_of(i * TB, TB)
            y_ref[pl.ds(r0, TB), :] = tile(pb[pl.ds(r0, TB + 2 * _CPAD), :])
            return c

        lax.fori_loop(0, nlat, lat, 0)
        for r0, w0 in ctx_tiles:
            y_ref[r0:r0 + TB, :] = tile(pb[w0:w0 + TB + 2 * _CPAD, :])

    return _pc(body, name=name, grid=(cfg.CW // LANE,),
               in_specs=[pl.BlockSpec((L, LANE), lambda j: (0, ca + j)),
                         pl.BlockSpec((L, LANE), lambda j: (0, cb + j)),
                         pl.BlockSpec((32, LANE), lambda j: (0, j)),
                         pl.BlockSpec((1, LANE), lambda j: (0, j))],
               out_specs=pl.BlockSpec((L, LANE), lambda j: (0, j)),
               out_shape=jax.ShapeDtypeStruct((L, cfg.CW), F32),
               scratch_shapes=[pltpu.VMEM((PBL, LANE), F32)],
               compiler_params=_cp(("parallel",)))(P, P, w, b)


def glu_dwconv_bwd(cfg, name, P, w, dy):
    L = P.shape[0]
    T, TC, TB, K = cfg.T, cfg.TC, cfg.TB, cfg.CK
    off = _offsets(cfg)
    ca, cb = off["la"] // LANE, off["lb"] // LANE
    nlat, ctx_tiles = _conv_windows(cfg)
    PBL = 3 * _CPAD + T + TC

    def body(a_ref, b_ref, w_ref, dy_ref, da_ref, db_ref, dw_ref, dbias_ref, pbu, pbd):
        _fill_padded(cfg, pbu, lambda r0: a_ref[pl.ds(r0, TB), :] * _sigmoid(b_ref[pl.ds(r0, TB), :]))
        _fill_padded(cfg, pbd, lambda r0: dy_ref[pl.ds(r0, TB), :])
        wv = w_ref[...]
        dw_ref[...] = jnp.zeros_like(dw_ref)
        dbias_ref[...] = jnp.zeros_like(dbias_ref)

        def tile(r0, winu, wind):
            tapu = _taps(winu, TB)
            tapd = _taps(wind, TB)
            dyt = dy_ref[pl.ds(r0, TB), :]
            du = jnp.zeros((TB, LANE), F32)
            for k in range(K):
                du = du + wv[k:k + 1, :] * tapd(K - 1 - k)
                dw_ref[k:k + 1, :] += jnp.sum(dyt * tapu(k), axis=0, keepdims=True)
            dbias_ref[...] += jnp.sum(dyt, axis=0, keepdims=True)
            a = a_ref[pl.ds(r0, TB), :]
            sg = _sigmoid(b_ref[pl.ds(r0, TB), :])
            da_ref[pl.ds(r0, TB), :] = (du * sg).astype(da_ref.dtype)
            db_ref[pl.ds(r0, TB), :] = (du * a * sg * (1.0 - sg)).astype(db_ref.dtype)

        def lat(i, c):
            r0 = pl.multiple_of(i * TB, TB)
            tile(r0, pbu[pl.ds(r0, TB + 2 * _CPAD), :], pbd[pl.ds(r0, TB + 2 * _CPAD), :])
            return c

        lax.fori_loop(0, nlat, lat, 0)
        for r0, w0 in ctx_tiles:
            tile(r0, pbu[w0:w0 + TB + 2 * _CPAD, :], pbd[w0:w0 + TB + 2 * _CPAD, :])

    col = pl.BlockSpec((L, LANE), lambda j: (0, j))
    return _pc(body, name=name, grid=(cfg.CW // LANE,),
               in_specs=[pl.BlockSpec((L, LANE), lambda j: (0, ca + j)),
                         pl.BlockSpec((L, LANE), lambda j: (0, cb + j)),
                         pl.BlockSpec((32, LANE), lambda j: (0, j)), col],
               out_specs=[col, col, pl.BlockSpec((32, LANE), lambda j: (0, j)),
                          pl.BlockSpec((1, LANE), lambda j: (0, j))],
               out_shape=[jax.ShapeDtypeStruct((L, cfg.CW), BF16), jax.ShapeDtypeStruct((L, cfg.CW), BF16),
                          jax.ShapeDtypeStruct((32, cfg.CW), F32), jax.ShapeDtypeStruct((1, cfg.CW), F32)],
               scratch_shapes=[pltpu.VMEM((PBL, LANE), F32), pltpu.VMEM((PBL, LANE), F32)],
               compiler_params=_cp(("parallel",)))(P, P, w, dy)


def ln_silu_fwd(cfg, name, y, g, b):
    L, W = y.shape
    TB = cfg.TB

    def body(y_ref, g_ref, b_ref, o_ref):
        yv = y_ref[...]
        mu = jnp.mean(yv, axis=-1, keepdims=True)
        xc = yv - mu
        var = jnp.mean(xc * xc, axis=-1, keepdims=True)
        z = xc * lax.rsqrt(var + EPS) * g_ref[...] + b_ref[...]
        o_ref[...] = _silu(z).astype(o_ref.dtype)

    row = pl.BlockSpec((TB, W), lambda i: (i, 0))
    vec = pl.BlockSpec((1, W), lambda i: (0, 0))
    return _pc(body, name=name, grid=(L // TB,), in_specs=[row, vec, vec], out_specs=row,
               out_shape=jax.ShapeDtypeStruct((L, W), BF16), compiler_params=_cp(("parallel",)))(y, g, b)


def ln_silu_bwd(cfg, name, dact, y, g, b):
    L, W = y.shape
    TB = cfg.TB

    def body(d_ref, y_ref, g_ref, b_ref, dy_ref, dg_ref, db_ref):
        i = pl.program_id(0)
        yv = y_ref[...]
        mu = jnp.mean(yv, axis=-1, keepdims=True)
        xc = yv - mu
        var = jnp.mean(xc * xc, axis=-1, keepdims=True)
        r = lax.rsqrt(var + EPS)
        yh = xc * r
        g = g_ref[...]
        z = yh * g + b_ref[...]
        dz = d_ref[...] * _dsilu(z)
        sg = jnp.sum(dz * yh, axis=0, keepdims=True)
        sb = jnp.sum(dz, axis=0, keepdims=True)

        @pl.when(i == 0)
        def _():
            dg_ref[...] = sg
            db_ref[...] = sb

        @pl.when(i > 0)
        def _():
            dg_ref[...] += sg
            db_ref[...] += sb

        dh = dz * g
        dy_ref[...] = r * (dh - jnp.mean(dh, axis=-1, keepdims=True)
                           - yh * jnp.mean(dh * yh, axis=-1, keepdims=True))

    row = pl.BlockSpec((TB, W), lambda i: (i, 0))
    vec = pl.BlockSpec((1, W), lambda i: (0, 0))
    return _pc(body, name=name, grid=(L // TB,), in_specs=[row, row, vec, vec], out_specs=[row, vec, vec],
               out_shape=[jax.ShapeDtypeStruct((L, W), F32), jax.ShapeDtypeStruct((1, W), F32),
                          jax.ShapeDtypeStruct((1, W), F32)],
               compiler_params=_cp(("arbitrary",)))(dact, y, g, b)


def _na_geometry(cfg):
    R = cfg.T // cfg.GW
    nb = R // cfg.NAR
    assert nb >= 3 and cfg.GW == 64 and cfg.NAR == 8
    ks = [int(np.clip(8 * b - 4, 0, R - 16)) for b in range(nb)]
    return R, nb, ks


_NTAB = 18


def _split3(x):
    hi = x.astype(BF16)
    r = x - hi.astype(F32)
    mid = r.astype(BF16)
    lo = (r - mid.astype(F32)).astype(BF16)
    return hi, mid, lo


def _na_col_onehot(cfg):
    GW, NAC = cfg.GW, cfg.NAC
    qc = np.arange(GW)[:, None]
    kc = np.arange(GW)[None, :]
    cs = np.clip(qc - NAC // 2, 0, GW - NAC)
    vcol = (kc >= cs) & (kc < cs + NAC)
    dd = np.clip(kc - qc + NAC - 1, 0, 2 * NAC - 2)
    oh = (np.arange(LANE)[:, None, None] == dd[None]).astype(np.float32)
    z = np.zeros_like(oh)
    oda = np.concatenate([oh, z], axis=2).reshape(LANE, GW * LANE)
    odb = np.concatenate([z, oh], axis=2).reshape(LANE, GW * LANE)
    cm = np.where(np.concatenate([vcol, vcol], axis=1), 0.0, NEG).astype(np.float32).reshape(1, GW * LANE)
    return oda, odb, cm


def na_tables(cfg, name, rpb):
    NH, GW = cfg.NH, cfg.GW
    na = rpb.shape[1]
    oda, odb, cm = _na_col_onehot(cfg)
    rp = jnp.zeros((NH, _NTAB + 1, LANE), F32).at[:, 1:1 + na, :rpb.shape[2]].set(rpb.astype(F32))
    r0 = rp[:, :_NTAB].reshape(NH * _NTAB, LANE)
    r1 = rp[:, 1:].reshape(NH * _NTAB, LANE)
    a = np.arange(_NTAB) - 1
    rm0 = np.where((a >= 0) & (a < na), 0.0, NEG).astype(np.float32)
    rm1 = np.where((a + 1 >= 0) & (a + 1 < na), 0.0, NEG).astype(np.float32)
    half = (np.arange(GW * LANE) % LANE >= GW)[None, :]
    rmask = np.where(half, np.tile(rm1, NH)[:, None], np.tile(rm0, NH)[:, None]).astype(np.float32)
    tn = 2048
    rows = NH * _NTAB

    def body(r0_ref, r1_ref, a_ref, b_ref, cm_ref, rm_ref, o_ref):
        acc = cm_ref[...] + rm_ref[...]
        for t in _split3(r0_ref[...]):
            acc = acc + dot_nn(t, a_ref[...])
        for t in _split3(r1_ref[...]):
            acc = acc + dot_nn(t, b_ref[...])
        o_ref[...] = acc

    rs = pl.BlockSpec((rows, LANE), lambda n: (0, 0))
    out = _pc(body, name=name, grid=(GW * LANE // tn,),
              in_specs=[rs, rs, pl.BlockSpec((LANE, tn), lambda n: (0, n)), pl.BlockSpec((LANE, tn), lambda n: (0, n)),
                        pl.BlockSpec((1, tn), lambda n: (0, n)), pl.BlockSpec((rows, tn), lambda n: (0, n))],
              out_specs=pl.BlockSpec((rows, tn), lambda n: (0, n)),
              out_shape=jax.ShapeDtypeStruct((rows, GW * LANE), F32),
              compiler_params=_cp(("parallel",)))(r0, r1, jnp.asarray(oda, BF16), jnp.asarray(odb, BF16),
                                                  jnp.asarray(cm), jnp.asarray(rmask))
    return out.reshape(NH, _NTAB, GW, LANE)


def _na_tiles(cfg, b):
    R, nb, _ = _na_geometry(cfg)
    NAR = cfg.NAR
    ksb = jnp.clip(8 * b - 4, 0, R - 16)
    for i in range(8):
        qr = 8 * b + i
        ws = jnp.clip(qr - NAR // 2, 0, R - NAR)
        for J in range(8):
            kr0 = ksb + 2 * J
            row = jnp.clip(kr0 - qr + NAR - 1, -1, _NTAB - 2) + 1
            v0 = jnp.logical_and(kr0 >= ws, kr0 < ws + NAR)
            v1 = jnp.logical_and(kr0 + 1 >= ws, kr0 + 1 < ws + NAR)
            yield i, J, row, v0, v1


def _na_fill_bias(cfg, tab_ref, bias, b):
    GW = cfg.GW
    first = lax.broadcasted_iota(jnp.int32, (GW, LANE), 1) < GW
    for i, J, row, v0, v1 in _na_tiles(cfg, b):
        ok = jnp.where(first, v0.astype(jnp.int32), v1.astype(jnp.int32))
        bias[i * GW:(i + 1) * GW, J * LANE:(J + 1) * LANE] = jnp.where(ok > 0, tab_ref[row], NEG)


def _na_specs(cfg):
    R, nb, ks = _na_geometry(cfg)
    off = _offsets(cfg)
    TQ = 8 * cfg.GW
    KP = 4 * cfg.GW
    ks4 = [k // 4 for k in ks]
    lat_blocks = cfg.T // KP

    def ks4_of(b):
        return jnp.clip(2 * b - 1, 0, R // 4 - 4)

    assert all(int(np.clip(2 * b - 1, 0, R // 4 - 4)) == ks4[b] for b in range(nb))
    assert cfg.TC == KP

    def col(nm):
        c0 = off[nm] // LANE
        q = pl.BlockSpec((TQ, LANE), lambda h, b: (b, c0 + h))
        parts = [pl.BlockSpec((KP, LANE), functools.partial(lambda h, b, t: (ks4_of(b) + t, c0 + h), t=t))
                 for t in range(4)]
        ctx = pl.BlockSpec((KP, LANE), lambda h, b: (lat_blocks, c0 + h))
        return q, parts, ctx

    return nb, TQ, KP, ks4_of, col


def na_fwd(cfg, name, P, tab):
    nb, TQ, KP, ks4_of, col = _na_specs(cfg)
    NH = cfg.NH
    scale = cfg.NDH ** -0.5
    qs, _, _ = col("nq")
    _, kparts, kctx = col("nk")
    _, vparts, vctx = col("nv")

    def body(q_ref, k0, k1, k2, k3, kc_ref, v0, v1, v2, v3, vc_ref, tab_ref, o_ref, lse_ref, bias_ref):
        _na_fill_bias(cfg, tab_ref, bias_ref, pl.program_id(1))
        q = (q_ref[...] * scale).astype(BF16)
        kl = jnp.concatenate([k0[...], k1[...], k2[...], k3[...]], axis=0).astype(BF16)
        vl = jnp.concatenate([v0[...], v1[...], v2[...], v3[...]], axis=0).astype(BF16)
        kc = kc_ref[...].astype(BF16)
        vc = vc_ref[...].astype(BF16)
        sl = dot_nt(q, kl) + bias_ref[...]
        sc = dot_nt(q, kc)
        m = jnp.maximum(jnp.max(sl, axis=-1, keepdims=True), jnp.max(sc, axis=-1, keepdims=True))
        pl_ = jnp.exp(sl - m)
        pc = jnp.exp(sc - m)
        den = jnp.sum(pl_, axis=-1, keepdims=True) + jnp.sum(pc, axis=-1, keepdims=True)
        o = dot_nn(pl_.astype(BF16), vl) + dot_nn(pc.astype(BF16), vc)
        o_ref[...] = o / den
        lse_ref[...] = m + jnp.log(den)

    return _pc(body, name=name, grid=(NH, nb),
               in_specs=[qs, *kparts, kctx, *vparts, vctx,
                         pl.BlockSpec((None, _NTAB, cfg.GW, LANE), lambda h, b: (h, 0, 0, 0))],
               out_specs=[pl.BlockSpec((TQ, LANE), lambda h, b: (b, h)),
                          pl.BlockSpec((None, TQ, 1), lambda h, b: (h, b, 0))],
               out_shape=[jax.ShapeDtypeStruct((cfg.T, NH * LANE), F32),
                          jax.ShapeDtypeStruct((NH, cfg.T, 1), F32)],
               scratch_shapes=[pltpu.VMEM((TQ, 4 * KP), F32)],
               compiler_params=_cp(("parallel", "parallel")))(P, *([P] * 5), *([P] * 5), tab)


def na_bwd(cfg, name, P, tab, o, lse, dmix, dcol0):
    nb, TQ, KP, ks4_of, col = _na_specs(cfg)
    NH, GW = cfg.NH, cfg.GW
    L = P.shape[0]
    scale = cfg.NDH ** -0.5
    qs, _, _ = col("nq")
    _, kparts, kctx = col("nk")
    _, vparts, vctx = col("nv")

    def body(q_ref, k0, k1, k2, k3, kc_ref, v0, v1, v2, v3, vc_ref, tab_ref, o_ref, lse_ref, do_ref,
             dq_ref, dk_ref, dv_ref, dtab_ref, bias_ref):
        b = pl.program_id(1)

        @pl.when(b == 0)
        def _():
            dk_ref[...] = jnp.zeros_like(dk_ref)
            dv_ref[...] = jnp.zeros_like(dv_ref)
            dtab_ref[...] = jnp.zeros_like(dtab_ref)

        _na_fill_bias(cfg, tab_ref, bias_ref, b)

        q = (q_ref[...] * scale).astype(BF16)
        kl = jnp.concatenate([k0[...], k1[...], k2[...], k3[...]], axis=0).astype(BF16)
        vl = jnp.concatenate([v0[...], v1[...], v2[...], v3[...]], axis=0).astype(BF16)
        kc = kc_ref[...].astype(BF16)
        vc = vc_ref[...].astype(BF16)
        lse = lse_ref[...]
        do = do_ref[...]
        dob = do.astype(BF16)
        p_l = jnp.exp(dot_nt(q, kl) + bias_ref[...] - lse)
        p_c = jnp.exp(dot_nt(q, kc) - lse)
        delta = jnp.sum(do * o_ref[...], axis=-1, keepdims=True)
        ds_l = p_l * (dot_nt(dob, vl) - delta)
        ds_c = p_c * (dot_nt(dob, vc) - delta)
        dslb = ds_l.astype(BF16)
        dscb = ds_c.astype(BF16)
        dq_ref[...] = ((dot_nn(dslb, kl) + dot_nn(dscb, kc)) * scale).astype(dq_ref.dtype)
        r0 = pl.multiple_of(ks4_of(b) * KP, KP)
        dk_ref[pl.ds(r0, 4 * KP), :] += dot_tn(dslb, q)
        dv_ref[pl.ds(r0, 4 * KP), :] += dot_tn(p_l.astype(BF16), dob)
        dk_ref[cfg.T:cfg.T + KP, :] += dot_tn(dscb, q)
        dv_ref[cfg.T:cfg.T + KP, :] += dot_tn(p_c.astype(BF16), dob)
        bias_ref[...] = ds_l
        for i, J, row, _, _ in _na_tiles(cfg, b):
            dtab_ref[row] += bias_ref[i * GW:(i + 1) * GW, J * LANE:(J + 1) * LANE]

    full = pl.BlockSpec((L, LANE), lambda h, b: (0, h))
    tabs = pl.BlockSpec((None, _NTAB, GW, LANE), lambda h, b: (h, 0, 0, 0))
    return _pc(body, name=name, grid=(NH, nb),
               in_specs=[qs, *kparts, kctx, *vparts, vctx, tabs,
                         pl.BlockSpec((TQ, LANE), lambda h, b: (b, h)),
                         pl.BlockSpec((None, TQ, 1), lambda h, b: (h, b, 0)),
                         pl.BlockSpec((TQ, LANE), lambda h, b: (b, dcol0 + h))],
               out_specs=[pl.BlockSpec((TQ, LANE), lambda h, b: (b, h)), full, full, tabs],
               out_shape=[jax.ShapeDtypeStruct((cfg.T, NH * LANE), BF16),
                          jax.ShapeDtypeStruct((L, NH * LANE), F32), jax.ShapeDtypeStruct((L, NH * LANE), F32),
                          jax.ShapeDtypeStruct((NH, _NTAB, GW, LANE), F32)],
               scratch_shapes=[pltpu.VMEM((TQ, 4 * KP), F32)],
               compiler_params=_cp(("parallel", "arbitrary")))(
                   P, *([P] * 5), *([P] * 5), tab, o, lse, dmix)


def na_ctx_fwd(cfg, name, P):
    off = _offsets(cfg)
    TC, NH = cfg.TC, cfg.NH
    rb = cfg.T // TC
    scale = cfg.NDH ** -0.5

    def body(q_ref, k_ref, v_ref, o_ref, lse_ref):
        q = (q_ref[...] * scale).astype(BF16)
        s = dot_nt(q, k_ref[...].astype(BF16))
        m = jnp.max(s, axis=-1, keepdims=True)
        p = jnp.exp(s - m)
        den = jnp.sum(p, axis=-1, keepdims=True)
        o_ref[...] = dot_nn(p.astype(BF16), v_ref[...].astype(BF16)) / den
        lse_ref[...] = m + jnp.log(den)

    spec = lambda nm: pl.BlockSpec((TC, LANE), functools.partial(lambda h, c0: (rb, c0 + h), c0=off[nm] // LANE))
    return _pc(body, name=name, grid=(NH,), in_specs=[spec("nq"), spec("nk"), spec("nv")],
               out_specs=[pl.BlockSpec((TC, LANE), lambda h: (0, h)), pl.BlockSpec((None, TC, 1), lambda h: (h, 0, 0))],
               out_shape=[jax.ShapeDtypeStruct((TC, NH * LANE), F32), jax.ShapeDtypeStruct((NH, TC, 1), F32)],
               compiler_params=_cp(("parallel",)))(P, P, P)


def na_ctx_bwd(cfg, name, P, o, lse, dmix, dcol0, dk_in, dv_in):
    off = _offsets(cfg)
    TC, NH = cfg.TC, cfg.NH
    rb = cfg.T // TC
    scale = cfg.NDH ** -0.5

    def body(q_ref, k_ref, v_ref, o_ref, lse_ref, do_ref, dki_ref, dvi_ref, dq_ref, dk_ref, dv_ref):
        q = (q_ref[...] * scale).astype(BF16)
        kb = k_ref[...].astype(BF16)
        vb = v_ref[...].astype(BF16)
        do = do_ref[...]
        dob = do.astype(BF16)
        p = jnp.exp(dot_nt(q, kb) - lse_ref[...])
        delta = jnp.sum(do * o_ref[...], axis=-1, keepdims=True)
        ds = (p * (dot_nt(dob, vb) - delta)).astype(BF16)
        dq_ref[...] = (dot_nn(ds, kb) * scale).astype(dq_ref.dtype)
        dk_ref[...] = (dki_ref[...] + dot_tn(ds, q)).astype(dk_ref.dtype)
        dv_ref[...] = (dvi_ref[...] + dot_tn(p.astype(BF16), dob)).astype(dv_ref.dtype)

    spec = lambda nm: pl.BlockSpec((TC, LANE), functools.partial(lambda h, c0: (rb, c0 + h), c0=off[nm] // LANE))
    hb = pl.BlockSpec((TC, LANE), lambda h: (0, h))
    ctxrow = pl.BlockSpec((TC, LANE), lambda h: (rb, h))
    shp = jax.ShapeDtypeStruct((TC, NH * LANE), BF16)
    return _pc(body, name=name, grid=(NH,),
               in_specs=[spec("nq"), spec("nk"), spec("nv"), hb, pl.BlockSpec((None, TC, 1), lambda h: (h, 0, 0)),
                         pl.BlockSpec((TC, LANE), lambda h: (rb, dcol0 + h)), ctxrow, ctxrow],
               out_specs=[hb, hb, hb], out_shape=[shp, shp, shp],
               compiler_params=_cp(("parallel",)))(P, P, P, o, lse, dmix, dk_in, dv_in)


def na_rpb_grad(cfg, name, dtab):
    NH, GW = cfg.NH, cfg.GW
    na, nd = 2 * cfg.NAR - 1, 2 * cfg.NAC - 1
    oda, odb, _ = _na_col_onehot(cfg)
    E = np.concatenate([oda.T, odb.T], axis=1)
    rows = NH * _NTAB

    def body(z_ref, e_ref, o_ref):
        zv = z_ref[...]
        hi = zv.astype(BF16)
        lo = (zv - hi.astype(F32)).astype(BF16)
        e = e_ref[...]
        o_ref[...] = dot_nn(hi, e) + dot_nn(lo, e)

    g = _pc(body, name=name, out_shape=jax.ShapeDtypeStruct((rows, 2 * LANE), F32),
            compiler_params=_cp())(dtab.reshape(rows, GW * LANE), jnp.asarray(E, BF16))
    g = g.reshape(NH, _NTAB, 2, LANE)
    return g[:, 1:1 + na, 0, :nd] + g[:, 0:na, 1, :nd]


def _seq_tiles(cfg):
    T, TC, TB = cfg.T, cfg.TC, cfg.TB
    tiles = []
    for i in range((T + TC) // TB):
        r0 = i * TB
        tiles.append((r0, r0 == 0 or r0 == T, r0 + TB == T or r0 + TB == T + TC))
    return tiles


def _shift3(ref_get, r0, TB, start, end, width):
    cur = ref_get(r0, TB)
    if start or end:
        rowi = lax.broadcasted_iota(jnp.int32, (TB, width), 0)
    up = jnp.where(rowi == 0, 0.0, pltpu.roll(cur, 1, 0)) if start else ref_get(r0 - 1, TB)
    dn = jnp.where(rowi == TB - 1, 0.0, pltpu.roll(cur, TB - 1, 0)) if end else ref_get(r0 + 1, TB)
    return up, cur, dn


def ffn_act_fwd(cfg, name, U2, w, b):
    _, L, DFF = U2.shape
    TB = cfg.TB
    tiles = _seq_tiles(cfg)

    def body(u_ref, w_ref, b_ref, a_ref):
        def plane(p, r0, st, en):
            up, cur, dn = _shift3(lambda r, n: u_ref[p, r:r + n, :], r0, TB, st, en, LANE)
            wv = w_ref[p]
            return wv[0:1, :] * up + wv[1:2, :] * cur + wv[2:3, :] * dn + b_ref[p]

        for r0, st, en in tiles:
            val = plane(0, r0, st, en)
            gate = plane(1, r0, st, en)
            a_ref[r0:r0 + TB, :] = (_silu(gate) * val).astype(a_ref.dtype)

    return _pc(body, name=name, grid=(DFF // LANE,),
               in_specs=[pl.BlockSpec((2, L, LANE), lambda j: (0, 0, j)),
                         pl.BlockSpec((2, 8, LANE), lambda j: (0, 0, j)),
                         pl.BlockSpec((2, 1, LANE), lambda j: (0, 0, j))],
               out_specs=pl.BlockSpec((L, LANE), lambda j: (0, j)),
               out_shape=jax.ShapeDtypeStruct((L, DFF), BF16),
               compiler_params=_cp(("parallel",)))(U2, w, b)


def ffn_act_bwd(cfg, name, U2, w, b, dA):
    _, L, DFF = U2.shape
    TB = cfg.TB
    tiles = _seq_tiles(cfg)

    def body(u_ref, w_ref, b_ref, da_ref, du_ref, dw_ref, db_ref, dbuf):
        dw_ref[...] = jnp.zeros_like(dw_ref)
        db_ref[...] = jnp.zeros_like(db_ref)
        for r0, st, en in tiles:
            shifted = []
            pre = []
            for p in range(2):
                up, cur, dn = _shift3(lambda r, n: u_ref[p, r:r + n, :], r0, TB, st, en, LANE)
                wv = w_ref[p]
                shifted.append((up, cur, dn))
                pre.append(wv[0:1, :] * up + wv[1:2, :] * cur + wv[2:3, :] * dn + b_ref[p])
            val, gate = pre
            da = da_ref[r0:r0 + TB, :]
            dpre = (da * _silu(gate), da * val * _dsilu(gate))
            for p in range(2):
                dbuf[p, r0:r0 + TB, :] = dpre[p]
                for k in range(3):
                    dw_ref[p, k:k + 1, :] += jnp.sum(dpre[p] * shifted[p][k], axis=0, keepdims=True)
                db_ref[p] += jnp.sum(dpre[p], axis=0, keepdims=True)
        for r0, st, en in tiles:
            for p in range(2):
                up, cur, dn = _shift3(lambda r, n: dbuf[p, r:r + n, :], r0, TB, st, en, LANE)
                wv = w_ref[p]
                du_ref[p, r0:r0 + TB, :] = (wv[0:1, :] * dn + wv[1:2, :] * cur + wv[2:3, :] * up).astype(du_ref.dtype)

    blk = pl.BlockSpec((2, L, LANE), lambda j: (0, 0, j))
    wspec = pl.BlockSpec((2, 8, LANE), lambda j: (0, 0, j))
    bspec = pl.BlockSpec((2, 1, LANE), lambda j: (0, 0, j))
    return _pc(body, name=name, grid=(DFF // LANE,),
               in_specs=[blk, wspec, bspec, pl.BlockSpec((L, LANE), lambda j: (0, j))],
               out_specs=[blk, wspec, bspec],
               out_shape=[jax.ShapeDtypeStruct((2, L, DFF), BF16), jax.ShapeDtypeStruct((2, 8, DFF), F32),
                          jax.ShapeDtypeStruct((2, 1, DFF), F32)],
               scratch_shapes=[pltpu.VMEM((2, L, LANE), F32)],
               compiler_params=_cp(("parallel",)))(U2, w, b, dA)


def _tm(L, parts):
    assert L % parts == 0
    return L // parts


def layer_fwd(cfg, l, XS, mod, wts, small, tabs):
    L, D = XS.shape
    off = _offsets(cfg)
    DIN = off["end"]
    tmA = _tm(L, 4)
    sv = {"XS": XS, "W": {}}

    def weight(name, after):
        sv["W"][name], tok = wts(name, after)
        return sv["W"][name], tok

    Win4, _ = weight("w_in", XS)
    nbi = Win4.shape[3]
    h1 = norm_mod_fwd(cfg, f"norm1_fwd_{l}", XS, small["norm1_g"], mod["sc1"], mod["sh1"])
    P = matmul(f"mm_in_{l}", h1, Win4, contract="nn", grid=(4, L // tmA),
               a_spec=pl.BlockSpec((tmA, D), lambda n, m: (m, 0)),
               b_spec=pl.BlockSpec((None, None, D, nbi), lambda n, m: (n, l, 0, 0)),
               out_shape=jax.ShapeDtypeStruct((L, DIN), F32),
               out_spec=pl.BlockSpec((tmA, nbi), lambda n, m: (m, n)), nk=1)
    qr, kr = rope_fwd(cfg, f"rope_fwd_{l}", P, tabs["cos"], tabs["sin"])
    o_f, o_b, st = retention_fwd(cfg, f"ret_fwd_{l}", qr, kr, P, small["lam"])
    o2 = (o_f, o_b)
    ret = ggn_fwd(cfg, f"ggn_fwd_{l}", o2, P, small["ret_gn_g"])
    ycv = glu_dwconv_fwd(cfg, f"dwconv_fwd_{l}", P, small["conv_dw_w"], small["conv_dw_b"])
    act = ln_silu_fwd(cfg, f"ln_silu_fwd_{l}", ycv, small["conv_ln_g"], small["conv_ln_b"])
    Wpw4, _ = weight("conv_pw", act)
    cv = mm_rowsharded(f"mm_pw_{l}", act, Wpw4, l, BF16, cfg.CW)
    bias = na_tables(cfg, f"na_tables_{l}", small["na_rpb"])
    na_l, lse = na_fwd(cfg, f"na_fwd_{l}", P, bias)
    na_c, lse_c = na_ctx_fwd(cfg, f"na_ctx_fwd_{l}", P)
    mix = jnp.concatenate([ret, cv, jnp.concatenate([na_l, na_c], axis=0).astype(BF16)], axis=1)
    Wout4, tok = weight("w_out", mix)
    Y1 = mm_rowsharded(f"mm_out_{l}", mix, Wout4, l, F32, D)
    XM = resid_fwd(cfg, f"resid1_fwd_{l}", XS, Y1, mod["g1"] if tok is None else mod["g1"] + tok)
    h2 = norm_mod_fwd(cfg, f"norm2_fwd_{l}", XM, small["norm2_g"], mod["sc2"], mod["sh2"])
    Wup4, _ = weight("ffn_up", h2)
    nbu = Wup4.shape[3]
    tnu = nbu // 2
    U2 = matmul(f"mm_up_{l}", h2, Wup4, contract="nn", grid=(8, L // tmA),
                a_spec=pl.BlockSpec((tmA, D), lambda n, m: (m, 0)),
                b_spec=pl.BlockSpec((None, None, D, tnu), lambda n, m: (n // 2, l, 0, n % 2)),
                out_shape=jax.ShapeDtypeStruct((2, L, cfg.DFF), F32),
                out_spec=pl.BlockSpec((None, tmA, tnu), lambda n, m: (n // 4, m, n % 4)), nk=1)
    A = ffn_act_fwd(cfg, f"ffn_act_fwd_{l}", U2, small["ffn_dw_w"], small["ffn_dw_b"])
    Wdn4, _ = weight("ffn_down", A)
    Y2 = mm_rowsharded(f"mm_down_{l}", A, Wdn4, l, F32, D // 2)
    XO = resid_fwd(cfg, f"resid2_fwd_{l}", XM, Y2, mod["g2"])
    sv.update(h1=h1, P=P, qr=qr, kr=kr, o2=o2, st=st, ycv=ycv, act=act, bias=bias, na_l=na_l, lse=lse,
              na_c=na_c, lse_c=lse_c, mix=mix, Y1=Y1, XM=XM, h2=h2, U2=U2, A=A, Y2=Y2)
    return XO, sv


GRAD_GROUPS = (("ffn_down", "ffn_up"), ("w_out", "conv_pw", "w_in"))


def layer_bwd(cfg, l, dXO, sv, mod, wts, small, tabs, gbuf, ready):
    L, D = dXO.shape
    off = _offsets(cfg)
    DIN = off["end"]
    Win4, Wout4, Wup4, Wdn4, Wpw4 = wts["w_in"], wts["w_out"], wts["ffn_up"], wts["ffn_down"], wts["conv_pw"]
    tmA, tmB = _tm(L, 4), _tm(L, 8)
    depth = Win4.shape[1]
    gb, gs, dm = {}, {}, {}
    P = sv["P"]
    dY2, dm["g2"] = resid_bwd(cfg, f"resid2_bwd_{l}", dXO, sv["Y2"], mod["g2"])
    nbd = Wdn4.shape[2]
    dA = matmul(f"mm_down_da_{l}", dY2, Wdn4, contract="nt", grid=(4, L // tmA),
                a_spec=pl.BlockSpec((tmA, D), lambda j, m: (m, 0)),
                b_spec=pl.BlockSpec((None, None, nbd, D), lambda j, m: (j, l, 0, 0)),
                out_shape=jax.ShapeDtypeStruct((L, cfg.DFF), F32),
                out_spec=pl.BlockSpec((tmA, nbd), lambda j, m: (m, j)), nk=1)
    gb["ffn_down"] = wgrad(cfg, f"mm_down_dw_{l}", sv["A"], dY2,
                           lambda rb, ri: pl.BlockSpec((rb, nbd), lambda j, m: (ri(m), j)),
                           lambda rb, ri: pl.BlockSpec((rb, D), lambda j, m: (ri(m), 0)),
                           jax.ShapeDtypeStruct((depth, 4, nbd, D), BF16),
                           pl.BlockSpec((None, None, nbd, D), lambda j, m: (l, j, 0, 0)), 4, gbuf.get("ffn_down"))
    dU2, dfw, dfb = ffn_act_bwd(cfg, f"ffn_act_bwd_{l}", sv["U2"], small["ffn_dw_w"], small["ffn_dw_b"], dA)
    gs["ffn_dw_w"], gs["ffn_dw_b"] = dfw, dfb
    nbu = Wup4.shape[3]
    tnu = nbu // 2
    dH2 = matmul(f"mm_up_dh_{l}", dU2, Wup4, contract="nt", grid=(L // tmA, 8),
                 a_spec=pl.BlockSpec((None, tmA, tnu), lambda m, n: (n // 4, m, n % 4)),
                 b_spec=pl.BlockSpec((None, None, D, tnu), lambda m, n: (n // 2, l, 0, n % 2)),
                 out_shape=jax.ShapeDtypeStruct((L, D), F32),
                 out_spec=pl.BlockSpec((tmA, D), lambda m, n: (m, 0)), nk=8)
    gb["ffn_up"] = wgrad(cfg, f"mm_up_dw_{l}", sv["h2"], dU2,
                         lambda rb, ri: pl.BlockSpec((rb, D), lambda n, m: (ri(m), 0)),
                         lambda rb, ri: pl.BlockSpec((None, rb, tnu), lambda n, m: (n // 4, ri(m), n % 4)),
                         jax.ShapeDtypeStruct((depth, 4, D, nbu), BF16),
                         pl.BlockSpec((None, None, D, tnu), lambda n, m: (l, n // 2, 0, n % 2)), 8, gbuf.get("ffn_up"))
    dXM, dm["sc2"], dm["sh2"], gs["norm2_g"] = norm_mod_bwd(
        cfg, f"norm2_bwd_{l}", dH2, sv["XM"], small["norm2_g"], mod["sc2"], dXO)
    tok = ready(GRAD_GROUPS[0], gb)
    dY1, dm["g1"] = resid_bwd(cfg, f"resid1_bwd_{l}", dXM, sv["Y1"], mod["g1"] if tok is None else mod["g1"] + tok)
    nbo = Wout4.shape[2]
    dmix = matmul(f"mm_out_dmix_{l}", dY1, Wout4, contract="nt", grid=(4, L // tmA),
                  a_spec=pl.BlockSpec((tmA, D), lambda j, m: (m, 0)),
                  b_spec=pl.BlockSpec((None, None, nbo, D), lambda j, m: (j, l, 0, 0)),
                  out_shape=jax.ShapeDtypeStruct((L, D), F32),
                  out_spec=pl.BlockSpec((tmA, nbo), lambda j, m: (m, j)), nk=1)
    gb["w_out"] = wgrad(cfg, f"mm_out_dw_{l}", sv["mix"], dY1,
                        lambda rb, ri: pl.BlockSpec((rb, nbo), lambda j, m: (ri(m), j)),
                        lambda rb, ri: pl.BlockSpec((rb, D), lambda j, m: (ri(m), 0)),
                        jax.ShapeDtypeStruct((depth, 4, nbo, D), BF16),
                        pl.BlockSpec((None, None, nbo, D), lambda j, m: (l, j, 0, 0)), 4, gbuf.get("w_out"))
    RW = cfg.RH * cfg.RDV
    do, dlg, gs["ret_gn_g"] = ggn_bwd(cfg, f"ggn_bwd_{l}", dmix, sv["o2"], P, small["ret_gn_g"], 0)
    dqf, dqb, dkf, dkb, dvf, dvb, dlam = retention_bwd(
        cfg, f"ret_bwd_{l}", sv["qr"], sv["kr"], P, small["lam"], sv["st"], do)
    gs["lam"] = dlam[:, :, 0, 0]
    dlq, dlk = rope_bwd(cfg, f"rope_bwd_{l}", (dqf, dqb), (dkf, dkb), tabs["cos"], tabs["sin"])
    dlv = add_cast(cfg, f"ret_dv_{l}", dvf, dvb)
    dcv = cast_cols(cfg, f"conv_dcv_{l}", dmix, RW // LANE, cfg.CW // LANE, LANE)
    nbp = Wpw4.shape[2]
    dact = matmul(f"mm_pw_dact_{l}", dcv, Wpw4, contract="nt", grid=(4, L // tmA),
                  a_spec=pl.BlockSpec((tmA, cfg.CW), lambda j, m: (m, 0)),
                  b_spec=pl.BlockSpec((None, None, nbp, cfg.CW), lambda j, m: (j, l, 0, 0)),
                  out_shape=jax.ShapeDtypeStruct((L, cfg.CW), F32),
                  out_spec=pl.BlockSpec((tmA, nbp), lambda j, m: (m, j)), nk=1)
    gb["conv_pw"] = wgrad(cfg, f"mm_pw_dw_{l}", sv["act"], dcv,
                          lambda rb, ri: pl.BlockSpec((rb, nbp), lambda j, m: (ri(m), j)),
                          lambda rb, ri: pl.BlockSpec((rb, cfg.CW), lambda j, m: (ri(m), 0)),
                          jax.ShapeDtypeStruct((depth, 4, nbp, cfg.CW), BF16),
                          pl.BlockSpec((None, None, nbp, cfg.CW), lambda j, m: (l, j, 0, 0)), 4, gbuf.get("conv_pw"))
    dycv, gs["conv_ln_g"], gs["conv_ln_b"] = ln_silu_bwd(
        cfg, f"ln_silu_bwd_{l}", dact, sv["ycv"], small["conv_ln_g"], small["conv_ln_b"])
    dla, dlb, gs["conv_dw_w"], gs["conv_dw_b"] = glu_dwconv_bwd(cfg, f"dwconv_bwd_{l}", P, small["conv_dw_w"], dycv)
    nac0 = (RW + cfg.CW) // LANE
    dnq_l, dnk, dnv, dsb = na_bwd(cfg, f"na_bwd_{l}", P, sv["bias"], sv["na_l"], sv["lse"], dmix, nac0)
    dnq_c, dnk_c, dnv_c = na_ctx_bwd(cfg, f"na_ctx_bwd_{l}", P, sv["na_c"], sv["lse_c"], dmix, nac0, dnk, dnv)
    gs["na_rpb"] = na_rpb_grad(cfg, f"na_rpb_{l}", dsb)
    dnq = jnp.concatenate([dnq_l, dnq_c], axis=0)
    dnk = jnp.concatenate([dnk[:cfg.T].astype(BF16), dnk_c], axis=0)
    dnv = jnp.concatenate([dnv[:cfg.T].astype(BF16), dnv_c], axis=0)
    dP = jnp.concatenate([dlq, dlk, dlv, dlg, dla, dlb, dnq, dnk, dnv], axis=1)
    nbi = Win4.shape[3]
    dH1 = matmul(f"mm_in_dh_{l}", dP, Win4, contract="nt", grid=(L // tmA, 4),
                 a_spec=pl.BlockSpec((tmA, nbi), lambda m, n: (m, n)),
                 b_spec=pl.BlockSpec((None, None, D, nbi), lambda m, n: (n, l, 0, 0)),
                 out_shape=jax.ShapeDtypeStruct((L, D), F32),
                 out_spec=pl.BlockSpec((tmA, D), lambda m, n: (m, 0)), nk=4)
    gb["w_in"] = wgrad(cfg, f"mm_in_dw_{l}", sv["h1"], dP,
                       lambda rb, ri: pl.BlockSpec((rb, D), lambda n, m: (ri(m), 0)),
                       lambda rb, ri: pl.BlockSpec((rb, nbi), lambda n, m: (ri(m), n)),
                       jax.ShapeDtypeStruct((depth, 4, D, nbi), BF16),
                       pl.BlockSpec((None, None, D, nbi), lambda n, m: (l, n, 0, 0)), 4, gbuf.get("w_in"))
    dXS, dm["sc1"], dm["sh1"], gs["norm1_g"] = norm_mod_bwd(
        cfg, f"norm1_bwd_{l}", dH1, sv["XS"], small["norm1_g"], mod["sc1"], dXM, latent_only=(l == 0))
    return dXS, gb, gs, dm, ready(GRAD_GROUPS[1], gb)


def _layer_small(cfg, l, sp):
    DFF = cfg.DFF
    fw = sp["ffn_dw_w"][l].reshape(3, 2, DFF).transpose(1, 0, 2)
    fw = jnp.concatenate([fw, jnp.zeros((2, 5, DFF), F32)], axis=1)
    cw = jnp.concatenate([sp["conv_dw_w"][l], jnp.zeros((32 - cfg.CK, cfg.CW), F32)], axis=0)
    return dict(
        norm1_g=sp["norm1_g"][l][None], norm2_g=sp["norm2_g"][l][None],
        lam=jax.nn.log_sigmoid(sp["ret_decay"][l]), ret_gn_g=sp["ret_gn_g"][l][None],
        conv_dw_w=cw, conv_dw_b=sp["conv_dw_b"][l][None], conv_ln_g=sp["conv_ln_g"][l][None],
        conv_ln_b=sp["conv_ln_b"][l][None], na_rpb=sp["na_rpb"][l],
        ffn_dw_w=fw, ffn_dw_b=sp["ffn_dw_b"][l].reshape(2, 1, DFF))


def local_step(cfg, x, ctx, tgt, mods, wts, sp, grads_ready=lambda l, names, gb: None):
    depth = sp["norm1_g"].shape[0]
    cos, sin = rope_tables(cfg)
    tabs = dict(cos=cos, sin=sin)
    XS = jnp.concatenate([x, ctx], axis=0)
    smalls = [_layer_small(cfg, l, sp) for l in range(depth)]
    saves = []
    for l in range(depth):
        XS, sv = layer_fwd(cfg, l, XS, mods[l], functools.partial(wts, l), smalls[l], tabs)
        saves.append(sv)
    ls, dX, dfg = final_loss(cfg, "final_loss", XS, sp["final_g"][None], tgt)
    gb, gss, dms = {}, [None] * depth, [None] * depth
    token = None
    for l in reversed(range(depth)):
        mod = mods[l] if token is None else {**mods[l], "g2": mods[l]["g2"] + token}
        dX, gb, gss[l], dms[l], token = layer_bwd(cfg, l, dX, saves[l], mod, saves[l]["W"], smalls[l], tabs, gb,
                                                  functools.partial(grads_ready, l))
    return ls[0, 0], dX[:cfg.T], gb, gss, dms, dfg[0], token


MESH = pl.DeviceIdType.MESH
N_DEV = 8
N_CHIP = 4
BIG = ("w_in", "w_out", "ffn_up", "ffn_down", "conv_pw")
_ANY = pl.BlockSpec(memory_space=pl.ANY)


def _place():
    x, y, c = lax.axis_index("x"), lax.axis_index("y"), lax.axis_index("c")
    chips = [(1 - x, y), (x, 1 - y), (1 - x, 1 - y)]
    return x, y, c, chips


def allgather8(name, xs):
    m_per, n = xs.shape

    def body(x_ref, out_ref, send_sems, recv_sems, local_sem):
        x, y, c, chips = _place()
        me, sibling = (x, y, c), (x, y, 1 - c)

        def rows(px, py, pc):
            return out_ref.at[pl.ds((4 * px + 2 * py + pc) * m_per, m_per), :]

        def copy(k, block, to, src=None):
            return pltpu.make_async_remote_copy(
                src_ref=rows(*block) if src is None else src, dst_ref=rows(*block),
                send_sem=send_sems.at[k], recv_sem=recv_sems.at[k], device_id=to, device_id_type=MESH)

        mine = pltpu.make_async_copy(x_ref, rows(*me), local_sem)
        mine.start()
        first = [copy(0, me, sibling, src=x_ref)]
        first += [copy(1 + j, me, (*chip, c), src=x_ref) for j, chip in enumerate(chips)]
        for cp in first:
            cp.start()
        passed = [copy(4 + j, (*chip, c), sibling) for j, chip in enumerate(chips)]
        for j, chip in enumerate(chips):
            copy(1 + j, (*chip, c), me).wait_recv()
            passed[j].start()
        copy(0, sibling, me).wait_recv()
        for j, chip in enumerate(chips):
            copy(4 + j, (*chip, 1 - c), me).wait_recv()
        for cp in first + passed:
            cp.wait_send()
        mine.wait()

    return _pc(body, name=name, out_shape=jax.ShapeDtypeStruct((N_DEV * m_per, n), xs.dtype),
               in_specs=[pl.BlockSpec(memory_space=pltpu.VMEM)], out_specs=pl.BlockSpec(memory_space=pltpu.VMEM),
               scratch_shapes=[pltpu.SemaphoreType.DMA((7,)), pltpu.SemaphoreType.DMA((7,)), pltpu.SemaphoreType.DMA],
               compiler_params=pltpu.CompilerParams(vmem_limit_bytes=VMEM_LIMIT))(xs)


def _wpiece(ref, layer, chip_idx, half):
    rh = ref.shape[2] // 2
    return ref.at[chip_idx, layer, pl.ds(half * rh, rh)]


def _wcopy(ref, layer, chip_idx, half, send_sems, recv_sems, k, to):
    piece = _wpiece(ref, layer, chip_idx, half)
    return pltpu.make_async_remote_copy(src_ref=piece, dst_ref=piece, send_sem=send_sems.at[k],
                                        recv_sem=recv_sems.at[k], device_id=to, device_id_type=MESH)


def _w_ici_sends(outs, layer, send_sems, recv_sems):
    x, y, c, chips = _place()
    return [_wcopy(outs[a], layer, 2 * x + y, c, send_sems, recv_sems, 3 * a + t, (*chip, c))
            for a in range(len(outs)) for t, chip in enumerate(chips)]


def _w_ici_landed(outs, layer, send_sems, recv_sems):
    x, y, c, chips = _place()
    return [_wcopy(outs[a], layer, 2 * chip[0] + chip[1], c, send_sems, recv_sems, 3 * a + t, (x, y, c))
            for a in range(len(outs)) for t, chip in enumerate(chips)]


def _w_forward(outs, layer, send_sems, recv_sems, base):
    x, y, c, chips = _place()
    n = len(outs)
    sends = [_wcopy(outs[a], layer, 2 * chip[0] + chip[1], c, send_sems, recv_sems, base + 3 * a + t, (x, y, 1 - c))
             for a in range(n) for t, chip in enumerate(chips)]
    recvs = [_wcopy(outs[a], layer, 2 * chip[0] + chip[1], 1 - c, send_sems, recv_sems, base + 3 * a + t, (x, y, c))
             for a in range(n) for t, chip in enumerate(chips)]
    return sends, recvs


def allgather_layer(name, bufs, layer):
    n = len(bufs)

    def body(*refs):
        outs = refs[n:2 * n]
        send_sems, recv_sems = refs[2 * n:]
        sent = _w_ici_sends(outs, layer, send_sems, recv_sems)
        for cp in sent:
            cp.start()
        fwd, from_sib = _w_forward(outs, layer, send_sems, recv_sems, 3 * n)
        for landed, fw in zip(_w_ici_landed(outs, layer, send_sems, recv_sems), fwd):
            landed.wait_recv()
            fw.start()
        for cp in from_sib:
            cp.wait_recv()
        for cp in sent + fwd:
            cp.wait_send()

    return _pc(body, name=name, out_shape=[jax.ShapeDtypeStruct(b.shape, b.dtype) for b in bufs],
               in_specs=[_ANY] * n, out_specs=[_ANY] * n, input_output_aliases={a: a for a in range(n)},
               scratch_shapes=[pltpu.SemaphoreType.DMA((6 * n,)), pltpu.SemaphoreType.DMA((6 * n,))])(*bufs)


_HBM = pl.BlockSpec(memory_space=pltpu.HBM)
_SEM = pl.BlockSpec(memory_space=pltpu.SEMAPHORE)
_EFFECT = pltpu.SideEffectType.DATAFLOW_SIDE_EFFECTING


def allgather_layer_start(name, bufs, layer, after):
    n = len(bufs)

    def body(*refs):
        send_sems, recv_sems = refs[n + 1:n + 3]
        outs = refs[n + 3:2 * n + 3]
        token = refs[2 * n + 3]
        for cp in _w_ici_sends(outs, layer, send_sems, recv_sems):
            cp.start()
        token[...] = jnp.zeros_like(token)

    res = _pc(body, name=name,
              out_shape=(pltpu.SemaphoreType.DMA((3 * n,)), pltpu.SemaphoreType.DMA((3 * n,)),
                         *[pltpu.HBM(b.shape, b.dtype) for b in bufs], jax.ShapeDtypeStruct((8, LANE), F32)),
              in_specs=[_HBM] * n + [_ANY],
              out_specs=(_SEM, _SEM, *([_HBM] * n), pl.BlockSpec(memory_space=pltpu.VMEM)),
              input_output_aliases={a: a + 2 for a in range(n)},
              compiler_params=pltpu.CompilerParams(has_side_effects=_EFFECT))(
                  *[pltpu.with_memory_space_constraint(b, pltpu.HBM) for b in bufs], after)
    return res[0], res[1], list(res[2:2 + n]), res[2 + n]


def allgather_layer_wait(name, bufs, send_sems, recv_sems, after, layer):
    n = len(bufs)

    def body(*refs):
        ins = refs[:n]
        send_sems, recv_sems = refs[n:n + 2]
        for cp in _w_ici_sends(ins, layer, send_sems, recv_sems):
            cp.wait_send()
        for cp in _w_ici_landed(ins, layer, send_sems, recv_sems):
            cp.wait_recv()

    return _pc(body, name=name, out_shape=tuple(pltpu.HBM(b.shape, b.dtype) for b in bufs),
               in_specs=[_HBM] * n + [_SEM, _SEM, _ANY], out_specs=tuple([_HBM] * n),
               input_output_aliases={a: a for a in range(n)},
               compiler_params=pltpu.CompilerParams(has_side_effects=_EFFECT))(*bufs, send_sems, recv_sems, after)


def forward_halves(name, bufs, layer):
    n = len(bufs)

    def body(*refs):
        outs = refs[n:2 * n]
        send_sems, recv_sems = refs[2 * n:]
        fwd, from_sib = _w_forward(outs, layer, send_sems, recv_sems, 0)
        for cp in fwd:
            cp.start()
        for cp in from_sib:
            cp.wait_recv()
        for cp in fwd:
            cp.wait_send()

    return _pc(body, name=name, out_shape=[jax.ShapeDtypeStruct(b.shape, b.dtype) for b in bufs],
               in_specs=[_ANY] * n, out_specs=[_ANY] * n, input_output_aliases={a: a for a in range(n)},
               scratch_shapes=[pltpu.SemaphoreType.DMA((3 * n,)), pltpu.SemaphoreType.DMA((3 * n,))])(*bufs)


def exchange_rows(name, grads, layer):
    n = len(grads)

    def body(*refs):
        ins, outs = refs[:n], refs[n:2 * n]
        send_sems, recv_sems = refs[2 * n:]
        x, y, c, _ = _place()
        cps = []
        for a in range(n):
            rh = ins[a].shape[2] // 2
            cps.append(pltpu.make_async_remote_copy(
                src_ref=ins[a].at[layer, pl.ds(0, N_CHIP), pl.ds((1 - c) * rh, rh)], dst_ref=outs[a],
                send_sem=send_sems.at[a], recv_sem=recv_sems.at[a], device_id=(x, y, 1 - c), device_id_type=MESH))
        for cp in cps:
            cp.start()
        for cp in cps:
            cp.wait()

    return _pc(body, name=name,
               out_shape=[jax.ShapeDtypeStruct((N_CHIP, g.shape[2] // 2, g.shape[3]), g.dtype) for g in grads],
               in_specs=[_ANY] * n, out_specs=[_ANY] * n,
               scratch_shapes=[pltpu.SemaphoreType.DMA((n,)), pltpu.SemaphoreType.DMA((n,))])(*grads)


def _scatter_sends(parts, lands, send_sems, recv_sems):
    x, y, c, chips = _place()
    return [pltpu.make_async_remote_copy(
        src_ref=parts[a].at[2 * chip[0] + chip[1]], dst_ref=lands[a].at[2 * x + y], send_sem=send_sems.at[3 * a + t],
        recv_sem=recv_sems.at[3 * a + t], device_id=(*chip, c), device_id_type=MESH)
        for a in range(len(parts)) for t, chip in enumerate(chips)]


def _scatter_landed(lands, send_sems, recv_sems):
    x, y, c, chips = _place()
    return [pltpu.make_async_remote_copy(
        src_ref=lands[a].at[2 * chip[0] + chip[1]], dst_ref=lands[a].at[2 * chip[0] + chip[1]],
        send_sem=send_sems.at[3 * a + t], recv_sem=recv_sems.at[3 * a + t], device_id=(x, y, c), device_id_type=MESH)
        for a in range(len(lands)) for t, chip in enumerate(chips)]


def scatter_slices(name, parts, lands):
    n = len(parts)

    def body(*refs):
        ins, outs = refs[:n], refs[2 * n:3 * n]
        send_sems, recv_sems = refs[3 * n:]
        cps = _scatter_sends(ins, outs, send_sems, recv_sems)
        for cp in cps:
            cp.start()
        for cp in _scatter_landed(outs, send_sems, recv_sems):
            cp.wait_recv()
        for cp in cps:
            cp.wait_send()

    return _pc(body, name=name, out_shape=[jax.ShapeDtypeStruct(p.shape, p.dtype) for p in lands],
               in_specs=[_ANY] * (2 * n), out_specs=[_ANY] * n,
               input_output_aliases={n + a: a for a in range(n)},
               scratch_shapes=[pltpu.SemaphoreType.DMA((3 * n,)), pltpu.SemaphoreType.DMA((3 * n,))])(*parts, *lands)


def scatter_slices_start(name, parts, lands):
    n = len(parts)

    def body(*refs):
        send_sems, recv_sems = refs[2 * n:2 * n + 2]
        p_out, l_out = refs[2 * n + 2:3 * n + 2], refs[3 * n + 2:4 * n + 2]
        token = refs[4 * n + 2]
        for cp in _scatter_sends(p_out, l_out, send_sems, recv_sems):
            cp.start()
        token[...] = jnp.zeros_like(token)

    both = list(parts) + list(lands)
    res = _pc(body, name=name,
              out_shape=(pltpu.SemaphoreType.DMA((3 * n,)), pltpu.SemaphoreType.DMA((3 * n,)),
                         *[pltpu.HBM(b.shape, b.dtype) for b in both], jax.ShapeDtypeStruct((8, LANE), F32)),
              in_specs=[_HBM] * (2 * n),
              out_specs=(_SEM, _SEM, *([_HBM] * (2 * n)), pl.BlockSpec(memory_space=pltpu.VMEM)),
              input_output_aliases={a: a + 2 for a in range(2 * n)},
              compiler_params=pltpu.CompilerParams(has_side_effects=_EFFECT))(
                  *[pltpu.with_memory_space_constraint(b, pltpu.HBM) for b in both])
    return res[0], res[1], list(res[2:2 + n]), list(res[2 + n:2 + 2 * n]), res[2 + 2 * n]


def scatter_slices_wait(name, parts, lands, send_sems, recv_sems, after):
    n = len(parts)

    def body(*refs):
        p_in, l_in = refs[:n], refs[n:2 * n]
        send_sems, recv_sems = refs[2 * n:2 * n + 2]
        for cp in _scatter_sends(p_in, l_in, send_sems, recv_sems):
            cp.wait_send()
        for cp in _scatter_landed(l_in, send_sems, recv_sems):
            cp.wait_recv()

    both = list(parts) + list(lands)
    res = _pc(body, name=name, out_shape=tuple(pltpu.HBM(b.shape, b.dtype) for b in both),
              in_specs=[_HBM] * (2 * n) + [_SEM, _SEM, _ANY], out_specs=tuple([_HBM] * (2 * n)),
              input_output_aliases={a: a for a in range(2 * n)},
              compiler_params=pltpu.CompilerParams(has_side_effects=_EFFECT))(*both, send_sems, recv_sems, after)
    return list(res[n:])


def share_rows(name, bufs):
    n = len(bufs)

    def body(*refs):
        outs = refs[n:2 * n]
        send_sems, recv_sems = refs[2 * n:]
        x, y, c, _ = _place()

        def half(a, h):
            return outs[a].at[pl.ds(0, 2), h]

        cps = [pltpu.make_async_remote_copy(
            src_ref=half(a, c), dst_ref=half(a, c), send_sem=send_sems.at[a], recv_sem=recv_sems.at[a],
            device_id=(x, y, 1 - c), device_id_type=MESH) for a in range(n)]
        for cp in cps:
            cp.start()
        for a in range(n):
            pltpu.make_async_remote_copy(
                src_ref=half(a, 1 - c), dst_ref=half(a, 1 - c), send_sem=send_sems.at[a],
                recv_sem=recv_sems.at[a], device_id=(x, y, c), device_id_type=MESH).wait_recv()
        for cp in cps:
            cp.wait_send()

    return _pc(body, name=name, out_shape=[jax.ShapeDtypeStruct(b.shape, b.dtype) for b in bufs],
               in_specs=[_ANY] * n, out_specs=[_ANY] * n, input_output_aliases={a: a for a in range(n)},
               scratch_shapes=[pltpu.SemaphoreType.DMA((n,)), pltpu.SemaphoreType.DMA((n,))])(*bufs)


def _row_tile(R, C, nbytes=1 << 20):
    t = 8
    while t * 2 <= R and R % (t * 2) == 0 and t * 2 * C * 4 <= nbytes:
        t *= 2
    assert R % t == 0
    return t


def to_bf16_block(name, w, chip_arr, layer):
    _, R, C = w.shape
    tr = _row_tile(R, C)

    def body(j_ref, w_ref, o_ref):
        o_ref[...] = w_ref[...].astype(o_ref.dtype)

    gs = pltpu.PrefetchScalarGridSpec(
        num_scalar_prefetch=1, grid=(R // tr,),
        in_specs=[pl.BlockSpec((None, tr, C), lambda i, j_ref: (layer, i, 0))],
        out_specs=pl.BlockSpec((None, None, tr, C), lambda i, j_ref: (j_ref[0], layer, i, 0)))
    return _pc(body, name=name, grid_spec=gs, out_shape=jax.ShapeDtypeStruct((N_CHIP,) + w.shape, BF16),
               compiler_params=_cp(("parallel",)))(chip_arr, w)


def add_rows(name, g, ra, c_arr, layer):
    _, _, R, C = g.shape
    rh = R // 2
    tr = _row_tile(rh, C)
    nb = rh // tr

    def body(c_ref, g_ref, r_ref, o_ref):
        o_ref[...] = (g_ref[...].astype(F32) + r_ref[...].astype(F32)).astype(o_ref.dtype)

    gs = pltpu.PrefetchScalarGridSpec(
        num_scalar_prefetch=1, grid=(N_CHIP, nb),
        in_specs=[pl.BlockSpec((None, None, tr, C), lambda j, i, c_ref: (layer, j, c_ref[0] * nb + i, 0)),
                  pl.BlockSpec((None, tr, C), lambda j, i, c_ref: (j, i, 0))],
        out_specs=pl.BlockSpec((None, tr, C), lambda j, i, c_ref: (j, i, 0)))
    return _pc(body, name=name, grid_spec=gs, out_shape=jax.ShapeDtypeStruct(ra.shape, BF16),
               compiler_params=_cp(("parallel", "parallel")))(c_arr, g, ra)


def sum_rows_into(name, landed, c_arr, layer, into):
    n, rh, C = landed.shape
    tr = _row_tile(rh, C, nbytes=1 << 19)

    def body(*refs):
        g_ref, o_ref = refs[1], refs[-1]
        acc = g_ref[0].astype(F32)
        for j in range(1, n):
            acc = acc + g_ref[j].astype(F32)
        o_ref[...] = acc

    in_specs, args, alias = [pl.BlockSpec((n, tr, C), lambda i, c_ref: (0, i, 0))], (c_arr, landed), {}
    if into is not None:
        in_specs, args, alias = in_specs + [_ANY], args + (into,), {2: 0}
    gs = pltpu.PrefetchScalarGridSpec(
        num_scalar_prefetch=1, grid=(rh // tr,), in_specs=in_specs,
        out_specs=pl.BlockSpec((None, None, tr, C), lambda i, c_ref: (layer, c_ref[0], i, 0)))
    return _pc(body, name=name, grid_spec=gs, out_shape=jax.ShapeDtypeStruct((2, 2, rh, C), F32),
               input_output_aliases=alias, compiler_params=_cp(("parallel",)))(*args)


def own_row(name, part, chip_arr):
    _, R, C = part.shape
    tr = _row_tile(R, C)

    def body(j_ref, p_ref, o_ref):
        o_ref[...] = p_ref[...]

    gs = pltpu.PrefetchScalarGridSpec(
        num_scalar_prefetch=1, grid=(R // tr,),
        in_specs=[pl.BlockSpec((None, tr, C), lambda i, j_ref: (j_ref[0], i, 0))],
        out_specs=pl.BlockSpec((None, tr, C), lambda i, j_ref: (j_ref[0], i, 0)))
    return _pc(body, name=name, grid_spec=gs, out_shape=jax.ShapeDtypeStruct(part.shape, part.dtype),
               compiler_params=_cp(("parallel",)))(chip_arr, part)


def sum_leading(name, g, plane=None):
    n, R, C = g.shape
    tr = _row_tile(R, C, nbytes=(1 << 21) // n)

    def body(*refs):
        g_ref, o_ref = refs[-2:]
        acc = g_ref[0].astype(F32)
        for j in range(1, n):
            acc = acc + g_ref[j].astype(F32)
        o_ref[...] = acc

    if plane is None:
        return _pc(body, name=name, grid=(R // tr,), in_specs=[pl.BlockSpec((n, tr, C), lambda i: (0, i, 0))],
                   out_specs=pl.BlockSpec((tr, C), lambda i: (i, 0)), out_shape=jax.ShapeDtypeStruct((R, C), F32),
                   compiler_params=_cp(("parallel",)))(g)
    count, idx = plane
    gs = pltpu.PrefetchScalarGridSpec(
        num_scalar_prefetch=1, grid=(R // tr,),
        in_specs=[pl.BlockSpec((n, tr, C), lambda i, p_ref: (0, i, 0))],
        out_specs=pl.BlockSpec((None, tr, C), lambda i, p_ref: (p_ref[0], i, 0)))
    return _pc(body, name=name, grid_spec=gs, out_shape=jax.ShapeDtypeStruct((count, R, C), F32),
               compiler_params=_cp(("parallel",)))(idx, g)


def adamw(name, w, g, m, v, emit_g=False):
    R, C = w.shape
    tr = _row_tile(R, C)

    def body(w_ref, g_ref, m_ref, v_ref, d_ref, mo_ref, vo_ref, *go_ref):
        gv = g_ref[...]
        if emit_g:
            go_ref[0][...] = gv
        mn = ADAM_B1 * m_ref[...] + (1.0 - ADAM_B1) * gv
        vn = ADAM_B2 * v_ref[...] + (1.0 - ADAM_B2) * (gv * gv)
        m_hat = mn / (1.0 - ADAM_B1 ** ADAM_STEP)
        v_hat = vn / (1.0 - ADAM_B2 ** ADAM_STEP)
        d_ref[...] = -ADAM_LR * (m_hat / (jnp.sqrt(v_hat) + ADAM_EPS) + ADAM_WD * w_ref[...])
        mo_ref[...] = mn
        vo_ref[...] = vn

    spec = pl.BlockSpec((tr, C), lambda i: (i, 0))
    shp = jax.ShapeDtypeStruct((R, C), F32)
    nout = 4 if emit_g else 3
    return _pc(body, name=name, grid=(R // tr,), in_specs=[spec] * 4, out_specs=[spec] * nout,
               out_shape=[shp] * nout, compiler_params=_cp(("parallel",)))(w, g, m, v)


_ADA_TN = 512


def adaln_fwd(name, cond, w, b):
    _, D, N = w.shape
    tn = min(_ADA_TN, N)

    def body(c_ref, w_ref, b_ref, o_ref):
        s = _silu(c_ref[...]).astype(BF16)
        o_ref[...] = dot_nn(s, w_ref[...].astype(BF16)) + b_ref[...]

    return _pc(body, name=name, grid=(2, N // tn),
               in_specs=[pl.BlockSpec((16, D), lambda l, n: (0, 0)),
                         pl.BlockSpec((None, D, tn), lambda l, n: (l, 0, n)),
                         pl.BlockSpec((None, 1, tn), lambda l, n: (l, 0, n))],
               out_specs=pl.BlockSpec((None, 16, tn), lambda l, n: (l, 0, n)),
               out_shape=jax.ShapeDtypeStruct((2, 16, N), F32),
               compiler_params=_cp(("parallel", "parallel")))(cond, w, b)


def adaln_bwd(name, cond, w, dm):
    _, D, N = w.shape
    tn = min(_ADA_TN, N)

    def body(c_ref, w_ref, dm_ref, gw_ref, ds_ref):
        first = jnp.logical_and(pl.program_id(0) == 0, pl.program_id(1) == 0)
        s = _silu(c_ref[...]).astype(BF16)
        dmb = dm_ref[...].astype(BF16)
        gw_ref[...] = dot_tn(s, dmb)
        p = dot_nt(dmb, w_ref[...].astype(BF16))

        @pl.when(first)
        def _():
            ds_ref[...] = p

        @pl.when(jnp.logical_not(first))
        def _():
            ds_ref[...] += p

    return _pc(body, name=name, grid=(2, N // tn),
               in_specs=[pl.BlockSpec((16, D), lambda l, n: (0, 0)),
                         pl.BlockSpec((None, D, tn), lambda l, n: (l, 0, n)),
                         pl.BlockSpec((None, 16, tn), lambda l, n: (l, 0, n))],
               out_specs=[pl.BlockSpec((None, D, tn), lambda l, n: (l, 0, n)),
                          pl.BlockSpec((16, D), lambda l, n: (0, 0))],
               out_shape=[jax.ShapeDtypeStruct((2, D, N), F32), jax.ShapeDtypeStruct((16, D), F32)],
               compiler_params=_cp(("arbitrary", "arbitrary")))(cond, w, dm)


def cctx_grad(name, parts, c_ctx):
    def body(p_ref, c_ref, o_ref):
        acc = p_ref[0]
        for j in range(1, N_CHIP):
            acc = acc + p_ref[j]
        o_ref[...] = acc * _dsilu(c_ref[...])

    return _pc(body, name=name, out_shape=jax.ShapeDtypeStruct(c_ctx.shape, F32))(parts, c_ctx)


def _pack(arrs):
    rows = []
    for a in arrs:
        f = a.reshape(-1)
        pad = (-f.shape[0]) % LANE
        rows.append(jnp.pad(f, (0, pad)).reshape(-1, LANE))
    out = jnp.concatenate(rows, axis=0)
    pad = (-out.shape[0]) % 8
    return jnp.pad(out, ((0, pad), (0, 0))) if pad else out


def _unpack(rows, shapes):
    out, r = [], 0
    for s in shapes:
        n = int(np.prod(s))
        nr = -(-n // LANE)
        out.append(rows[r:r + nr].reshape(-1)[:n].reshape(s))
        r += nr
    return out


MOD_NAMES = ("sh1", "sc1", "g1", "sh2", "sc2", "g2")


def kernel(x, c, ctx, c_ctx, w_ada, b_ada, norm1_g, w_in, ret_decay, ret_gn_g, conv_dw_w, conv_dw_b, conv_ln_g, conv_ln_b, conv_pw, na_rpb, w_out, norm2_g, ffn_up, ffn_dw_w, ffn_dw_b, ffn_down, final_g, loss_target, m_c_ctx, m_w_ada, m_b_ada, m_norm1_g, m_w_in, m_ret_decay, m_ret_gn_g, m_conv_dw_w, m_conv_dw_b, m_conv_ln_g, m_conv_ln_b, m_conv_pw, m_na_rpb, m_w_out, m_norm2_g, m_ffn_up, m_ffn_dw_w, m_ffn_dw_b, m_ffn_down, m_final_g, v_c_ctx, v_w_ada, v_b_ada, v_norm1_g, v_w_in, v_ret_decay, v_ret_gn_g, v_conv_dw_w, v_conv_dw_b, v_conv_ln_g, v_conv_ln_b, v_conv_pw, v_na_rpb, v_w_out, v_norm2_g, v_ffn_up, v_ffn_dw_w, v_ffn_dw_b, v_ffn_down, v_final_g):
    cfg = make_cfg(D=x.shape[2], T=x.shape[1], TC=ctx.shape[1], RH=ret_decay.shape[2], CW=conv_dw_b.shape[1],
                   NH=na_rpb.shape[1], DFF=ffn_dw_b.shape[1] // 2)
    D, T = cfg.D, cfg.T
    W = dict(c_ctx=c_ctx, w_ada=w_ada, b_ada=b_ada, norm1_g=norm1_g, w_in=w_in, ret_decay=ret_decay, ret_gn_g=ret_gn_g,
             conv_dw_w=conv_dw_w, conv_dw_b=conv_dw_b, conv_ln_g=conv_ln_g, conv_ln_b=conv_ln_b, conv_pw=conv_pw,
             na_rpb=na_rpb, w_out=w_out, norm2_g=norm2_g, ffn_up=ffn_up, ffn_dw_w=ffn_dw_w, ffn_dw_b=ffn_dw_b,
             ffn_down=ffn_down, final_g=final_g)
    Mo = dict(c_ctx=m_c_ctx, w_ada=m_w_ada, b_ada=m_b_ada, norm1_g=m_norm1_g, w_in=m_w_in, ret_decay=m_ret_decay,
              ret_gn_g=m_ret_gn_g, conv_dw_w=m_conv_dw_w, conv_dw_b=m_conv_dw_b, conv_ln_g=m_conv_ln_g,
              conv_ln_b=m_conv_ln_b, conv_pw=m_conv_pw, na_rpb=m_na_rpb, w_out=m_w_out, norm2_g=m_norm2_g,
              ffn_up=m_ffn_up, ffn_dw_w=m_ffn_dw_w, ffn_dw_b=m_ffn_dw_b, ffn_down=m_ffn_down, final_g=m_final_g)
    Vo = dict(c_ctx=v_c_ctx, w_ada=v_w_ada, b_ada=v_b_ada, norm1_g=v_norm1_g, w_in=v_w_in, ret_decay=v_ret_decay,
              ret_gn_g=v_ret_gn_g, conv_dw_w=v_conv_dw_w, conv_dw_b=v_conv_dw_b, conv_ln_g=v_conv_ln_g,
              conv_ln_b=v_conv_ln_b, conv_pw=v_conv_pw, na_rpb=v_na_rpb, w_out=v_w_out, norm2_g=v_norm2_g,
              ffn_up=v_ffn_up, ffn_dw_w=v_ffn_dw_w, ffn_dw_b=v_ffn_dw_b, ffn_down=v_ffn_down, final_g=v_final_g)
    order = list(W)
    xi, yi, ci = lax.axis_index("x"), lax.axis_index("y"), lax.axis_index("c")
    chip = 2 * xi + yi
    dev = 4 * xi + 2 * yi + ci
    NA = w_ada.shape[2]
    ncw, nfw = conv_dw_w.shape[2], ffn_dw_w.shape[2]

    c_arr = jnp.reshape(ci, (1,)).astype(jnp.int32)
    chip_arr = jnp.reshape(chip, (1,)).astype(jnp.int32)
    first, rest = ("w_in", "conv_pw"), ("w_out", "ffn_up", "ffn_down")
    wb = [{nm: to_bf16_block(f"to_bf16_{nm}_{l}", W[nm], chip_arr, l) for nm in BIG} for l in range(2)]
    have, flying_w = {}, {}

    def start_gather(tag, l, names, after):
        s_sem, r_sem, bufs, tok = allgather_layer_start(f"ag_{tag}_start", [wb[l][nm] for nm in names], l, after)
        flying_w[(l, names[0])] = (tag, l, names, bufs, s_sem, r_sem)
        return tok[0, 0]

    def land_gather(key, after):
        tag, l, names, bufs, s_sem, r_sem = flying_w.pop(key)
        landed = allgather_layer_wait(f"ag_{tag}_wait", bufs, s_sem, r_sem, after, l)
        have.update(zip([(l, nm) for nm in names], forward_halves(f"ag_{tag}_fwd", list(landed), l)))

    tok_first = start_gather("w0a", 0, first, c)

    g_in = allgather8("ag_small_in", _pack([c[0] + tok_first, conv_dw_w, ffn_dw_w])).reshape(N_DEV, -1, LANE)
    c8 = g_in[:, :D // LANE].reshape(N_DEV, D)
    cw_parts, fw_parts = [], []
    for j in range(N_CHIP):
        _, a, b = _unpack(g_in[2 * j], [(D,), conv_dw_w.shape, ffn_dw_w.shape])
        cw_parts.append(a)
        fw_parts.append(b)
    conv_dw_w_full = jnp.concatenate(cw_parts, axis=2)
    ffn_dw_w_full = jnp.concatenate(fw_parts, axis=2)
    cond = jnp.concatenate([c8, c_ctx[None], jnp.zeros((16 - N_DEV - 1, D), F32)], axis=0)

    b_sh = lax.dynamic_slice(b_ada, (0, chip * NA), (2, NA)).reshape(2, 1, NA)
    m_sh = adaln_fwd("adaln_fwd", cond, w_ada, b_sh)
    m_dev = allgather8("ag_mod", m_sh.reshape(2 * 16, NA)).reshape(N_DEV, 2, 16, NA)
    m_all = jnp.concatenate([m_dev[2 * j] for j in range(N_CHIP)], axis=-1)
    mods = []
    for l in range(2):
        lat = lax.dynamic_slice(m_all[l], (dev, 0), (1, N_CHIP * NA))[0]
        cx = m_all[l, N_DEV]
        mods.append({nm: jnp.stack([lat[k * D:(k + 1) * D], cx[k * D:(k + 1) * D]], 0)[:, None, :]
                     for k, nm in enumerate(MOD_NAMES)})

    land_gather((0, first[0]), m_all)
    mods[0] = {**mods[0], "sc1": mods[0]["sc1"] + start_gather("w0b", 0, rest, have[(0, first[0])])}

    def wts(l, name, after):
        tok = None
        if (l, name) not in have:
            if l == 0:
                land_gather((0, rest[0]), after)
                tok = start_gather("w1", 1, BIG, have[(0, rest[0])])
            else:
                land_gather((1, BIG[0]), after)
        return have[(l, name)], tok

    sp = dict(norm1_g=norm1_g, norm2_g=norm2_g, ret_decay=ret_decay, ret_gn_g=ret_gn_g, conv_dw_w=conv_dw_w_full,
              conv_dw_b=conv_dw_b, conv_ln_g=conv_ln_g, conv_ln_b=conv_ln_b, na_rpb=na_rpb, ffn_dw_w=ffn_dw_w_full,
              ffn_dw_b=ffn_dw_b, final_g=final_g)
    flights = []

    def grads_ready(l, names, gb):
        tag = f"{l}_{names[0]}"
        from_sib = exchange_rows(f"rs_exchange_{tag}", [gb[nm] for nm in names], l)
        part = [add_rows(f"rs_add_{nm}_{l}", gb[nm], r, c_arr, l) for nm, r in zip(names, from_sib)]
        lands = [own_row(f"rs_own_{nm}_{l}", p, chip_arr) for nm, p in zip(names, part)]
        s_sem, r_sem, part, lands, tok = scatter_slices_start(f"rs_scatter_{tag}_start", part, lands)
        flights.append((l, names, tag, part, lands, (s_sem, r_sem)))
        return tok[0, 0]

    loss_l, gx, gb, gss, dms, dfg, tok_last = local_step(
        cfg, x[0], ctx[0], loss_target[0], mods, wts, sp, grads_ready)
    loss = lax.psum(loss_l, ("x", "y", "c"))

    dmseg = jnp.stack([jnp.stack([jnp.concatenate([dms[l][nm][r, 0] for nm in MOD_NAMES]) for r in range(2)])
                       for l in range(2)]) + tok_last
    gsm = dict(
        norm1_g=jnp.stack([gss[l]["norm1_g"][0] for l in range(2)]),
        ret_decay=jnp.stack([gss[l]["lam"] * jax.nn.sigmoid(-ret_decay[l]) for l in range(2)]),
        ret_gn_g=jnp.stack([gss[l]["ret_gn_g"][0] for l in range(2)]),
        conv_dw_w=jnp.stack([gss[l]["conv_dw_w"][:cfg.CK] for l in range(2)]),
        conv_dw_b=jnp.stack([gss[l]["conv_dw_b"][0] for l in range(2)]),
        conv_ln_g=jnp.stack([gss[l]["conv_ln_g"][0] for l in range(2)]),
        conv_ln_b=jnp.stack([gss[l]["conv_ln_b"][0] for l in range(2)]),
        na_rpb=jnp.stack([gss[l]["na_rpb"] for l in range(2)]),
        norm2_g=jnp.stack([gss[l]["norm2_g"][0] for l in range(2)]),
        ffn_dw_w=jnp.stack([gss[l]["ffn_dw_w"][:, :3].transpose(1, 0, 2).reshape(3, 2 * cfg.DFF) for l in range(2)]),
        ffn_dw_b=jnp.stack([gss[l]["ffn_dw_b"].reshape(-1) for l in range(2)]),
        final_g=dfg)
    snames = list(gsm)
    sshapes = [dmseg.shape] + [gsm[nm].shape for nm in snames]
    packed = _pack([dmseg] + [gsm[nm] for nm in snames])
    g_all = allgather8("ag_small_grads", packed).reshape(N_DEV, packed.shape[0], LANE)
    summed = sum_leading("sum_small_grads", g_all)
    dm_sum, *gsum = _unpack(summed, sshapes)
    gfull = dict(zip(snames, gsum))
    ndm = int(np.prod(dmseg.shape))
    dm_all = g_all[:, :ndm // LANE].reshape(N_DEV, 2, 2, 6 * D)
    gfull["b_ada"] = sum_leading("sum_b_ada", dm_all.transpose(0, 2, 1, 3).reshape(2 * N_DEV, 2 * 6 * D // LANE, LANE)
                                 ).reshape(2, 6 * D)

    dm16 = jnp.concatenate([dm_all[:, :, 0].transpose(1, 0, 2), dm_sum[:, 1][:, None],
                            jnp.zeros((2, 16 - N_DEV - 1, 6 * D), F32)], axis=1)
    dm16 = lax.dynamic_slice(dm16, (0, 0, chip * NA), (2, 16, NA))
    gfull["w_ada"], ds16 = adaln_bwd("adaln_bwd", cond, w_ada, dm16)
    ds_all = allgather8("ag_dsilu", ds16[8:16]).reshape(N_DEV, 8, D)[0::2, 0:1]
    gfull["c_ctx"] = cctx_grad("cctx_grad", ds_all, c_ctx[None])[0]
    gfull["conv_dw_w"] = lax.dynamic_slice(gfull["conv_dw_w"], (0, 0, chip * ncw), (2, cfg.CK, ncw))
    gfull["ffn_dw_w"] = lax.dynamic_slice(gfull["ffn_dw_w"], (0, 0, chip * nfw), (2, 3, nfw))

    delta, new_m, new_v = {}, {}, {}
    bigs = ("w_ada",) + BIG

    def adamw_big(nm):
        shp = W[nm].shape
        v2 = lambda a: a.reshape(-1, shp[-1])
        d_, m_, v_, *g_ = adamw(f"adamw_{nm}", v2(W[nm]), v2(gfull[nm]), v2(Mo[nm]), v2(Vo[nm]), emit_g=nm in BIG)
        delta[nm], new_m[nm], new_v[nm] = d_.reshape(shp), m_.reshape(shp), v_.reshape(shp)
        if g_:
            gfull[nm] = g_[0].reshape(shp)

    adamw_big("w_ada")
    smalls = [nm for nm in order if nm not in bigs]
    shapes = [W[nm].shape for nm in smalls]
    d_, m_, v_ = adamw("adamw_small", _pack([W[nm] for nm in smalls]), _pack([gfull[nm] for nm in smalls]),
                       _pack([Mo[nm] for nm in smalls]), _pack([Vo[nm] for nm in smalls]))
    for nm, a, b, e in zip(smalls, _unpack(d_, shapes), _unpack(m_, shapes), _unpack(v_, shapes)):
        delta[nm], new_m[nm], new_v[nm] = a, b, e

    fin = {}
    for l, names, tag, part, lands, sems in flights:
        landed = scatter_slices_wait(f"rs_scatter_{tag}_wait", part, lands, *sems, delta["w_ada"])
        for nm, p in zip(names, landed):
            fin[nm] = sum_rows_into(f"rs_sum_{nm}_{l}", p, c_arr, l, fin.get(nm))
    for nm, gfin in zip(BIG, share_rows("rs_share", [fin[nm] for nm in BIG])):
        gfull[nm] = gfin.reshape(W[nm].shape)
    for nm in BIG:
        adamw_big(nm)
    return (loss, gx[None], *[gfull[nm] for nm in order], *[delta[nm] for nm in order],
            *[new_m[nm] for nm in order], *[new_v[nm] for nm in order])
```

```python
import collections
import functools

import numpy as np
import jax
import jax.numpy as jnp
from jax import lax
from jax.experimental import pallas as pl
from jax.experimental.pallas import tpu as pltpu

F32 = jnp.float32
BF16 = jnp.bfloat16
EPS = 1e-6
ROPE_BASE = 10000.0
NEG = -1e30
LANE = 128
VMEM_LIMIT = 56 * 1024 * 1024

ADAM_LR, ADAM_B1, ADAM_B2, ADAM_EPS, ADAM_WD, ADAM_STEP = 0.001, 0.9, 0.999, 1e-08, 0.01, 10

Cfg = collections.namedtuple(
    "Cfg", "D T TC GW RH RDK RDV CW CK NH NDH NAR NAC DFF TB")


def make_cfg(D=2048, T=4096, TC=256, RH=4, CW=512, NH=4, DFF=5632):
    return Cfg(D=D, T=T, TC=TC, GW=64, RH=RH, RDK=128, RDV=256, CW=CW, CK=31, NH=NH, NDH=128,
               NAR=8, NAC=16, DFF=DFF, TB=256)


def _offsets(cfg):
    sizes = [cfg.RH * cfg.RDK, cfg.RH * cfg.RDK, cfg.RH * cfg.RDV, cfg.RH * cfg.RDV, cfg.CW, cfg.CW,
             cfg.NH * cfg.NDH, cfg.NH * cfg.NDH, cfg.NH * cfg.NDH]
    offs = [0]
    for s in sizes:
        offs.append(offs[-1] + s)
    return dict(zip(["lq", "lk", "lv", "lg", "la", "lb", "nq", "nk", "nv", "end"], offs))


def _pc(body, **kw):
    return pl.pallas_call(body, **kw)


def _cp(sem=None):
    return pltpu.CompilerParams(dimension_semantics=sem, vmem_limit_bytes=VMEM_LIMIT)


def _dot(a, b, ca, cb):
    return lax.dot_general(a, b, (((ca,), (cb,)), ((), ())), preferred_element_type=F32)


def dot_nn(a, b):
    return _dot(a, b, 1, 0)


def dot_nt(a, b):
    return _dot(a, b, 1, 1)


def dot_tn(a, b):
    return _dot(a, b, 0, 0)


def _sigmoid(x):
    return 1.0 / (1.0 + jnp.exp(-x))


def _silu(x):
    return x * _sigmoid(x)


def _dsilu(x):
    s = _sigmoid(x)
    return s * (1.0 + x * (1.0 - s))


def matmul(name, a, b, *, contract, grid, a_spec, b_spec, out_shape, out_spec, nk, into=None):
    dot = {"nn": dot_nn, "nt": dot_nt, "tn": dot_tn}[contract]
    direct = nk > 1 and out_shape.dtype == F32
    kax = len(grid) - 1

    def body(a_ref, b_ref, *rest):
        o_ref, *scr = rest[1:] if into is not None else rest
        p = dot(a_ref[...].astype(BF16), b_ref[...].astype(BF16))
        if nk == 1:
            o_ref[...] = p.astype(o_ref.dtype)
            return
        acc = o_ref if direct else scr[0]
        k = pl.program_id(kax)

        @pl.when(k == 0)
        def _():
            acc[...] = p

        @pl.when(k > 0)
        def _():
            acc[...] += p

        if not direct:
            @pl.when(k == nk - 1)
            def _():
                o_ref[...] = acc[...].astype(o_ref.dtype)

    scratch = []
    if nk > 1 and not direct:
        blk = [s for s in out_spec.block_shape if s is not None]
        scratch = [pltpu.VMEM(tuple(blk), F32)]
    sem = ("parallel",) * kax + (("arbitrary",) if nk > 1 else ("parallel",))
    in_specs, args, alias = [a_spec, b_spec], (a, b), {}
    if into is not None:
        in_specs, args, alias = in_specs + [pl.BlockSpec(memory_space=pl.ANY)], (a, b, into), {2: 0}
    return _pc(body, name=name, grid=grid, in_specs=in_specs, out_specs=out_spec, out_shape=out_shape,
               scratch_shapes=scratch, input_output_aliases=alias, compiler_params=_cp(sem))(*args)


_WG_ROWS = 1024


def wgrad(cfg, name, a, dc, a_spec, dc_spec, out_shape, out_spec, ntiles, into):
    T, TC = cfg.T, cfg.TC
    tml = min(_WG_ROWS, T)
    nl = T // tml

    def body(al_ref, ac_ref, dl_ref, dcx_ref, *rest):
        o_ref, acc = rest[-2:]
        m = pl.program_id(1)

        @pl.when(m == 0)
        def _():
            acc[...] = dot_tn(al_ref[...], dl_ref[...])

        @pl.when(jnp.logical_and(m > 0, m < nl))
        def _():
            acc[...] += dot_tn(al_ref[...], dl_ref[...])

        @pl.when(m == nl)
        def _():
            o_ref[...] = (acc[...] + dot_tn(ac_ref[...], dcx_ref[...])).astype(o_ref.dtype)

    lat = lambda m: jnp.minimum(m, nl - 1)
    ctx = lambda m: T // TC
    in_specs = [a_spec(tml, lat), a_spec(TC, ctx), dc_spec(tml, lat), dc_spec(TC, ctx)]
    args, alias = (a, a, dc, dc), {}
    if into is not None:
        in_specs, args, alias = in_specs + [pl.BlockSpec(memory_space=pl.ANY)], args + (into,), {4: 0}
    blk = tuple(s for s in out_spec.block_shape if s is not None)
    return _pc(body, name=name, grid=(ntiles, nl + 1), in_specs=in_specs, out_specs=out_spec, out_shape=out_shape,
               scratch_shapes=[pltpu.VMEM(blk, F32)], input_output_aliases=alias,
               compiler_params=_cp(("parallel", "arbitrary")))(*args)


def mm_rowsharded(name, a, w4, l, out_dtype, tn):
    L, K = a.shape
    nch, _, Kb, N = w4.shape
    tm = 256

    def body(a_ref, w_ref, o_ref):
        acc = dot_nn(a_ref[:, 0:Kb], w_ref[0])
        for j in range(1, nch):
            acc += dot_nn(a_ref[:, j * Kb:(j + 1) * Kb], w_ref[j])
        o_ref[...] = acc.astype(o_ref.dtype)

    return _pc(body, name=name, grid=(N // tn, L // tm),
               in_specs=[pl.BlockSpec((tm, K), lambda n, m: (m, 0)),
                         pl.BlockSpec((nch, None, Kb, tn), lambda n, m: (0, l, 0, n))],
               out_specs=pl.BlockSpec((tm, tn), lambda n, m: (m, n)),
               out_shape=jax.ShapeDtypeStruct((L, N), out_dtype),
               compiler_params=_cp(("parallel", "parallel")))(a, w4)


def _region(cfg):
    nlat = cfg.T // cfg.TB
    return lambda i: jnp.minimum(i // nlat, 1)


def norm_mod_fwd(cfg, name, x, ng, sc, sh):
    L, D = x.shape
    TB = cfg.TB
    reg = _region(cfg)

    def body(x_ref, ng_ref, sc_ref, sh_ref, h_ref):
        xv = x_ref[...]
        r = lax.rsqrt(jnp.mean(xv * xv, axis=-1, keepdims=True) + EPS)
        n = xv * r * ng_ref[...]
        h_ref[...] = (n * (1.0 + sc_ref[...]) + sh_ref[...]).astype(h_ref.dtype)

    row = pl.BlockSpec((TB, D), lambda i: (i, 0))
    vec = pl.BlockSpec((1, D), lambda i: (0, 0))
    rvec = pl.BlockSpec((None, 1, D), lambda i: (reg(i), 0, 0))
    return _pc(body, name=name, grid=(L // TB,), in_specs=[row, vec, rvec, rvec], out_specs=row,
               out_shape=jax.ShapeDtypeStruct((L, D), BF16), compiler_params=_cp(("parallel",)))(x, ng, sc, sh)


def norm_mod_bwd(cfg, name, dh, x, ng, sc, dx_in, latent_only=False):
    L, D = x.shape
    TB = cfg.TB
    nlat = cfg.T // TB
    reg = _region(cfg)

    def body(dh_ref, x_ref, ng_ref, sc_ref, dxi_ref, dx_ref, dsc_ref, dsh_ref, dng_ref):
        i = pl.program_id(0)
        xv = x_ref[...]
        r = lax.rsqrt(jnp.mean(xv * xv, axis=-1, keepdims=True) + EPS)
        xh = xv * r
        g = ng_ref[...]
        n = xh * g
        dh = dh_ref[...]
        dn = dh * (1.0 + sc_ref[...])
        dxh = dn * g
        dx = r * (dxh - xh * jnp.mean(dxh * xh, axis=-1, keepdims=True))
        if latent_only:
            @pl.when(i < nlat)
            def _():
                dx_ref[...] = dxi_ref[...] + dx
        else:
            dx_ref[...] = dxi_ref[...] + dx
        s_sh = jnp.sum(dh, axis=0, keepdims=True)
        s_sc = jnp.sum(dh * n, axis=0, keepdims=True)
        s_ng = jnp.sum(dn * xh, axis=0, keepdims=True)
        first = jnp.logical_or(i == 0, i == nlat)

        @pl.when(first)
        def _():
            dsh_ref[...] = s_sh
            dsc_ref[...] = s_sc

        @pl.when(jnp.logical_not(first))
        def _():
            dsh_ref[...] += s_sh
            dsc_ref[...] += s_sc

        @pl.when(i == 0)
        def _():
            dng_ref[...] = s_ng

        @pl.when(i > 0)
        def _():
            dng_ref[...] += s_ng

    row = pl.BlockSpec((TB, D), lambda i: (i, 0))
    vec = pl.BlockSpec((1, D), lambda i: (0, 0))
    rvec = pl.BlockSpec((None, 1, D), lambda i: (reg(i), 0, 0))
    dxs = pl.BlockSpec((TB, D), lambda i: (jnp.minimum(i, nlat - 1), 0)) if latent_only else row
    return _pc(body, name=name, grid=(L // TB,), in_specs=[row, row, vec, rvec, row],
               out_specs=[dxs, rvec, rvec, vec],
               out_shape=[jax.ShapeDtypeStruct((cfg.T if latent_only else L, D), F32),
                          jax.ShapeDtypeStruct((2, 1, D), F32),
                          jax.ShapeDtypeStruct((2, 1, D), F32), jax.ShapeDtypeStruct((1, D), F32)],
               compiler_params=_cp(("arbitrary",)))(dh, x, ng, sc, dx_in)


def resid_fwd(cfg, name, x, y, g):
    L, D = x.shape
    TB = cfg.TB
    reg = _region(cfg)

    def body(x_ref, y_ref, g_ref, o_ref):
        o_ref[...] = x_ref[...] + g_ref[...] * y_ref[...]

    row = pl.BlockSpec((TB, D), lambda i: (i, 0))
    rvec = pl.BlockSpec((None, 1, D), lambda i: (reg(i), 0, 0))
    return _pc(body, name=name, grid=(L // TB,), in_specs=[row, row, rvec], out_specs=row,
               out_shape=jax.ShapeDtypeStruct((L, D), F32), compiler_params=_cp(("parallel",)))(x, y, g)


def resid_bwd(cfg, name, dxo, y, g):
    L, D = y.shape
    TB = cfg.TB
    nlat = cfg.T // TB
    reg = _region(cfg)

    def body(d_ref, y_ref, g_ref, dy_ref, dg_ref):
        i = pl.program_id(0)
        d = d_ref[...]
        dy_ref[...] = (d * g_ref[...]).astype(dy_ref.dtype)
        s = jnp.sum(d * y_ref[...], axis=0, keepdims=True)
        first = jnp.logical_or(i == 0, i == nlat)

        @pl.when(first)
        def _():
            dg_ref[...] = s

        @pl.when(jnp.logical_not(first))
        def _():
            dg_ref[...] += s

    row = pl.BlockSpec((TB, D), lambda i: (i, 0))
    rvec = pl.BlockSpec((None, 1, D), lambda i: (reg(i), 0, 0))
    return _pc(body, name=name, grid=(L // TB,), in_specs=[row, row, rvec], out_specs=[row, rvec],
               out_shape=[jax.ShapeDtypeStruct((L, D), BF16), jax.ShapeDtypeStruct((2, 1, D), F32)],
               compiler_params=_cp(("arbitrary",)))(dxo, y, g)


def final_loss(cfg, name, x, fg, tgt):
    L, D = x.shape
    TB = cfg.TB
    nlat = cfg.T // TB

    def body(x_ref, fg_ref, t_ref, ls_ref, dx_ref, dg_ref):
        i = pl.program_id(0)

        @pl.when(i == 0)
        def _():
            ls_ref[...] = jnp.zeros_like(ls_ref)
            dg_ref[...] = jnp.zeros_like(dg_ref)

        @pl.when(i < nlat)
        def _():
            xv = x_ref[...]
            r = lax.rsqrt(jnp.mean(xv * xv, axis=-1, keepdims=True) + EPS)
            xh = xv * r
            g = fg_ref[...]
            e = xh * g - t_ref[...]
            ls_ref[...] += 0.5 * jnp.sum(e * e) / D
            dy = e / D
            dg_ref[...] += jnp.sum(dy * xh, axis=0, keepdims=True)
            dxh = dy * g
            dx_ref[...] = r * (dxh - xh * jnp.mean(dxh * xh, axis=-1, keepdims=True))

        @pl.when(i >= nlat)
        def _():
            dx_ref[...] = jnp.zeros_like(dx_ref)

    row = pl.BlockSpec((TB, D), lambda i: (i, 0))
    trow = pl.BlockSpec((TB, D), lambda i: (jnp.minimum(i, nlat - 1), 0))
    vec = pl.BlockSpec((1, D), lambda i: (0, 0))
    return _pc(body, name=name, grid=(L // TB,), in_specs=[row, vec, trow],
               out_specs=[pl.BlockSpec((1, LANE), lambda i: (0, 0)), row, vec],
               out_shape=[jax.ShapeDtypeStruct((1, LANE), F32), jax.ShapeDtypeStruct((L, D), F32),
                          jax.ShapeDtypeStruct((1, D), F32)],
               compiler_params=_cp(("arbitrary",)))(x, fg, tgt)


def rope_tables(cfg):
    half = cfg.RDK // 2
    nf = half // 2
    pos = np.arange(cfg.T)
    row = (pos // cfg.GW).astype(np.float32)
    col = (pos % cfg.GW).astype(np.float32)
    inv = jnp.asarray(ROPE_BASE, F32) ** (-jnp.arange(nf, dtype=F32) / nf)
    ar = jnp.asarray(row)[:, None] * inv[None, :]
    ac = jnp.asarray(col)[:, None] * inv[None, :]
    cos = jnp.concatenate([jnp.cos(ar), jnp.cos(ar), jnp.cos(ac), jnp.cos(ac)], axis=1)
    sin = jnp.concatenate([-jnp.sin(ar), jnp.sin(ar), -jnp.sin(ac), jnp.sin(ac)], axis=1)
    cos = jnp.concatenate([cos, jnp.ones((cfg.TC, cfg.RDK), F32)], axis=0)
    sin = jnp.concatenate([sin, jnp.zeros((cfg.TC, cfg.RDK), F32)], axis=0)
    return cos, sin


def _rb(cfg):
    rb = (cfg.T + cfg.TC) // 4
    assert rb % 16 == 0
    return rb


def _swap32(t):
    lane = lax.broadcasted_iota(jnp.int32, t.shape, 1)
    return jnp.where((lane % 64) < 32, pltpu.roll(t, 96, 1), pltpu.roll(t, 32, 1))


def rope_fwd(cfg, name, P, cos, sin):
    L = P.shape[0]
    TB = _rb(cfg)
    off = _offsets(cfg)
    cq, ck = off["lq"] // LANE, off["lk"] // LANE
    scale = cfg.RDK ** -0.5

    def body(q_ref, k_ref, c_ref, s_ref, qo_ref, ko_ref):
        c = c_ref[...]
        s = s_ref[...]
        q = q_ref[...]
        k = k_ref[...]
        qo_ref[...] = (q * c + _swap32(q) * s) * scale
        ko_ref[...] = k * c + _swap32(k) * s

    tab = pl.BlockSpec((TB, LANE), lambda i, h: (i, 0))
    out = pl.BlockSpec((TB, LANE), lambda i, h: (i, h))
    shp = jax.ShapeDtypeStruct((L, cfg.RH * cfg.RDK), F32)
    return _pc(body, name=name, grid=(L // TB, cfg.RH),
               in_specs=[pl.BlockSpec((TB, LANE), lambda i, h: (i, cq + h)),
                         pl.BlockSpec((TB, LANE), lambda i, h: (i, ck + h)), tab, tab],
               out_specs=[out, out], out_shape=[shp, shp],
               compiler_params=_cp(("parallel", "parallel")))(P, P, cos, sin)


def rope_bwd(cfg, name, dq2, dk2, cos, sin):
    L, W = dq2[0].shape
    TB = _rb(cfg)
    scale = cfg.RDK ** -0.5

    def body(dqf_ref, dqb_ref, dkf_ref, dkb_ref, c_ref, s_ref, qo_ref, ko_ref):
        c = c_ref[...]
        s = s_ref[...]
        dq = dqf_ref[...] + dqb_ref[...]
        dk = dkf_ref[...] + dkb_ref[...]
        qo_ref[...] = ((dq * c - _swap32(dq) * s) * scale).astype(qo_ref.dtype)
        ko_ref[...] = (dk * c - _swap32(dk) * s).astype(ko_ref.dtype)

    tab = pl.BlockSpec((TB, LANE), lambda i, h: (i, 0))
    blk = pl.BlockSpec((TB, LANE), lambda i, h: (i, h))
    shp = jax.ShapeDtypeStruct((L, W), BF16)
    return _pc(body, name=name, grid=(L // TB, cfg.RH), in_specs=[blk, blk, blk, blk, tab, tab],
               out_specs=[blk, blk], out_shape=[shp, shp],
               compiler_params=_cp(("parallel", "parallel")))(*dq2, *dk2, cos, sin)


def _ret_chunk_map(cfg):
    C = cfg.RDK
    n = (cfg.T + cfg.TC) // C
    nlat, nctx = cfg.T // C, cfg.TC // C

    def chunk(d, s):
        if d == 0:
            return jnp.where(s < nctx, nlat + s, s - nctx)
        return n - 1 - s

    return n, chunk


def _ret_decay_terms(d, lam, C):
    ii = lax.broadcasted_iota(jnp.int32, (C, C), 0)
    jj = lax.broadcasted_iota(jnp.int32, (C, C), 1)
    diff = (ii - jj if d == 0 else jj - ii).astype(F32)
    dpos = jnp.maximum(diff, 0.0)
    Dm = jnp.where(diff >= 0, jnp.exp(dpos * lam), 0.0)
    ic = lax.broadcasted_iota(jnp.int32, (C, 1), 0).astype(F32)
    cxi = ic + 1.0 if d == 0 else C - ic
    cze = C - 1.0 - ic if d == 0 else ic
    xi = jnp.exp(cxi * lam)
    ze = jnp.exp(cze * lam)
    g = jnp.exp(jnp.full((1, 1), C, F32) * lam)
    return dpos, Dm, cxi, cze, xi, ze, g


def retention_fwd(cfg, name, qr, kr, P, lam):
    L = P.shape[0]
    C, DV, RH = cfg.RDK, cfg.RDV, cfg.RH
    n, chunk = _ret_chunk_map(cfg)

    def body(lam_ref, qf_ref, qb_ref, kf_ref, kb_ref, vf_ref, vb_ref, of_ref, ob_ref, st_ref, S):
        s = pl.program_id(0)

        @pl.when(s == 0)
        def _():
            S[...] = jnp.zeros_like(S)

        for d, (q_ref, k_ref, v_ref, o_ref) in enumerate(((qf_ref, kf_ref, vf_ref, of_ref),
                                                          (qb_ref, kb_ref, vb_ref, ob_ref))):
            for h in range(RH):
                _, Dm, _, _, xi, ze, g = _ret_decay_terms(d, lam_ref[d, h], C)
                k = k_ref[:, h * C:(h + 1) * C]
                qb = q_ref[:, h * C:(h + 1) * C].astype(BF16)
                kb = k.astype(BF16)
                vb = v_ref[:, h * DV:(h + 1) * DV].astype(BF16)
                Sv = S[d, h]
                st_ref[d, h] = Sv
                A = dot_nt(qb, kb) * Dm
                o_ref[:, h * DV:(h + 1) * DV] = dot_nn(A.astype(BF16), vb) + dot_nn(qb, Sv.astype(BF16)) * xi
                S[d, h] = Sv * g + dot_tn((k * ze).astype(BF16), vb)

    def spec(w, col, d):
        return pl.BlockSpec((C, w), lambda s: (chunk(d, s), col))

    W, WV = RH * C, RH * DV
    return _pc(body, name=name, grid=(n,),
               in_specs=[pl.BlockSpec(memory_space=pltpu.SMEM), spec(W, 0, 0), spec(W, 0, 1), spec(W, 0, 0),
                         spec(W, 0, 1), spec(WV, 1, 0), spec(WV, 1, 1)],
               out_specs=[spec(WV, 0, 0), spec(WV, 0, 1),
                          pl.BlockSpec((2, RH, None, C, DV), lambda s: (0, 0, s, 0, 0))],
               out_shape=[jax.ShapeDtypeStruct((L, WV), F32), jax.ShapeDtypeStruct((L, WV), F32),
                          jax.ShapeDtypeStruct((2, RH, n, C, DV), F32)],
               scratch_shapes=[pltpu.VMEM((2, RH, C, DV), F32)],
               compiler_params=_cp(("arbitrary",)))(lam, qr, qr, kr, kr, P, P)


def retention_bwd(cfg, name, qr, kr, P, lam, st, do):
    L = P.shape[0]
    C, DV, RH = cfg.RDK, cfg.RDV, cfg.RH
    n, chunk = _ret_chunk_map(cfg)

    def body(lam_ref, qf_ref, qb_ref, kf_ref, kb_ref, vf_ref, vb_ref, st_ref, dof_ref, dob_ref,
             dqf_ref, dqb_ref, dkf_ref, dkb_ref, dvf_ref, dvb_ref, dl_ref, dS):
        si = pl.program_id(0)

        @pl.when(si == 0)
        def _():
            dS[...] = jnp.zeros_like(dS)
            dl_ref[...] = jnp.zeros_like(dl_ref)

        dirs = ((qf_ref, kf_ref, vf_ref, dof_ref, dqf_ref, dkf_ref, dvf_ref),
                (qb_ref, kb_ref, vb_ref, dob_ref, dqb_ref, dkb_ref, dvb_ref))
        for d, (q_ref, k_ref, v_ref, do_ref, dq_ref, dk_ref, dv_ref) in enumerate(dirs):
            for h in range(RH):
                dpos, Dm, cxi, cze, xi, ze, g = _ret_decay_terms(d, lam_ref[d, h], C)
                hk = slice(h * C, (h + 1) * C)
                hv = slice(h * DV, (h + 1) * DV)
                k = k_ref[:, hk]
                do = do_ref[:, hv]
                qb = q_ref[:, hk].astype(BF16)
                kb = k.astype(BF16)
                vb = v_ref[:, hv].astype(BF16)
                dob = do.astype(BF16)
                Sn = st_ref[d, h]
                Snb = Sn.astype(BF16)
                dSn = dS[d, h]
                dSb = dSn.astype(BF16)
                A = dot_nt(qb, kb) * Dm
                dA = dot_nt(dob, vb)
                dQK = (dA * Dm).astype(BF16)
                kzb = (k * ze).astype(BF16)
                dv_ref[:, hv] = dot_tn(A.astype(BF16), dob) + dot_nn(kzb, dSb)
                dkz = dot_nt(vb, dSb)
                doxb = (do * xi).astype(BF16)
                dq_ref[:, hk] = dot_nn(dQK, kb) + dot_nt(doxb, Snb)
                dk_ref[:, hk] = dot_tn(dQK, qb) + dkz * ze
                QS = dot_nn(qb, Snb)
                t = (jnp.sum(dA * A * dpos) + jnp.sum(do * QS * (cxi * xi)) + jnp.sum(k * dkz * (cze * ze)))
                t4 = jnp.sum(dSn * Sn, axis=0, keepdims=True)
                t4 = jnp.sum(t4 * (g * C), axis=1, keepdims=True)
                dl_ref[d, h] += t + t4
                dS[d, h] = g * dSn + dot_tn(qb, doxb)

    def spec(w, col, d):
        return pl.BlockSpec((C, w), lambda si: (chunk(d, n - 1 - si), col))

    W, WV = RH * C, RH * DV
    return _pc(body, name=name, grid=(n,),
               in_specs=[pl.BlockSpec(memory_space=pltpu.SMEM), spec(W, 0, 0), spec(W, 0, 1), spec(W, 0, 0),
                         spec(W, 0, 1), spec(WV, 1, 0), spec(WV, 1, 1),
                         pl.BlockSpec((2, RH, None, C, DV), lambda si: (0, 0, n - 1 - si, 0, 0)),
                         spec(WV, 0, 0), spec(WV, 0, 1)],
               out_specs=[spec(W, 0, 0), spec(W, 0, 1), spec(W, 0, 0), spec(W, 0, 1), spec(WV, 0, 0), spec(WV, 0, 1),
                          pl.BlockSpec((2, RH, 8, LANE), lambda si: (0, 0, 0, 0))],
               out_shape=[jax.ShapeDtypeStruct((L, W), F32)] * 4 + [jax.ShapeDtypeStruct((L, WV), F32)] * 2
               + [jax.ShapeDtypeStruct((2, RH, 8, LANE), F32)],
               scratch_shapes=[pltpu.VMEM((2, RH, C, DV), F32)],
               compiler_params=_cp(("arbitrary",)))(lam, qr, qr, kr, kr, P, P, st, do, do)


def add_cast(cfg, name, a, b):
    L, W = a.shape
    TB = _rb(cfg)

    def body(a_ref, b_ref, o_ref):
        o_ref[...] = (a_ref[...] + b_ref[...]).astype(o_ref.dtype)

    spec = pl.BlockSpec((TB, W), lambda i: (i, 0))
    return _pc(body, name=name, grid=(L // TB,), in_specs=[spec, spec], out_specs=spec,
               out_shape=jax.ShapeDtypeStruct((L, W), BF16), compiler_params=_cp(("parallel",)))(a, b)


def ggn_fwd(cfg, name, o2, P, gn_g):
    L = P.shape[0]
    TB, DV, RH = _rb(cfg), cfg.RDV, cfg.RH
    gc0 = _offsets(cfg)["lg"] // DV

    def body(of_ref, ob_ref, gate_ref, g_ref, out_ref):
        o = of_ref[...] + ob_ref[...]
        mu = jnp.mean(o, axis=-1, keepdims=True)
        xc = o - mu
        var = jnp.mean(xc * xc, axis=-1, keepdims=True)
        y = xc * lax.rsqrt(var + EPS) * g_ref[...]
        out_ref[...] = (y * _silu(gate_ref[...])).astype(out_ref.dtype)

    blk = pl.BlockSpec((TB, DV), lambda i, h: (i, h))
    return _pc(body, name=name, grid=(L // TB, RH),
               in_specs=[blk, blk, pl.BlockSpec((TB, DV), lambda i, h: (i, gc0 + h)),
                         pl.BlockSpec((1, DV), lambda i, h: (0, h))],
               out_specs=blk, out_shape=jax.ShapeDtypeStruct((L, RH * DV), BF16),
               compiler_params=_cp(("parallel", "parallel")))(*o2, P, gn_g)


def ggn_bwd(cfg, name, dout, o2, P, gn_g, col0):
    L = P.shape[0]
    TB, DV, RH = _rb(cfg), cfg.RDV, cfg.RH
    gc0 = _offsets(cfg)["lg"] // DV

    def body(d_ref, of_ref, ob_ref, gate_ref, g_ref, do_ref, dgate_ref, dg_ref):
        i = pl.program_id(1)
        o = of_ref[...] + ob_ref[...]
        mu = jnp.mean(o, axis=-1, keepdims=True)
        xc = o - mu
        var = jnp.mean(xc * xc, axis=-1, keepdims=True)
        r = lax.rsqrt(var + EPS)
        y = xc * r
        g = g_ref[...]
        gate = gate_ref[...]
        d = d_ref[...]
        dgate_ref[...] = (d * (y * g) * _dsilu(gate)).astype(dgate_ref.dtype)
        dyg = d * _silu(gate)
        s = jnp.sum(dyg * y, axis=0, keepdims=True)

        @pl.when(i == 0)
        def _():
            dg_ref[...] = s

        @pl.when(i > 0)
        def _():
            dg_ref[...] += s

        dy = dyg * g
        do_ref[...] = r * (dy - jnp.mean(dy, axis=-1, keepdims=True)
                           - y * jnp.mean(dy * y, axis=-1, keepdims=True))

    blk = pl.BlockSpec((TB, DV), lambda h, i: (i, h))
    return _pc(body, name=name, grid=(RH, L // TB),
               in_specs=[pl.BlockSpec((TB, DV), lambda h, i: (i, col0 + h)), blk, blk,
                         pl.BlockSpec((TB, DV), lambda h, i: (i, gc0 + h)),
                         pl.BlockSpec((1, DV), lambda h, i: (0, h))],
               out_specs=[blk, blk, pl.BlockSpec((1, DV), lambda h, i: (0, h))],
               out_shape=[jax.ShapeDtypeStruct((L, RH * DV), F32), jax.ShapeDtypeStruct((L, RH * DV), BF16),
                          jax.ShapeDtypeStruct((1, RH * DV), F32)],
               compiler_params=_cp(("parallel", "arbitrary")))(dout, *o2, P, gn_g)


def cast_cols(cfg, name, src, col0, ncols, width):
    L = src.shape[0]
    TB = _rb(cfg)

    def body(s_ref, o_ref):
        o_ref[...] = s_ref[...].astype(o_ref.dtype)

    spec = pl.BlockSpec((TB, width), lambda i, j: (i, col0 + j))
    return _pc(body, name=name, grid=(L // TB, ncols), in_specs=[spec],
               out_specs=pl.BlockSpec((TB, width), lambda i, j: (i, j)),
               out_shape=jax.ShapeDtypeStruct((L, ncols * width), BF16),
               compiler_params=_cp(("parallel", "parallel")))(src)


_CPAD = 16


def _conv_windows(cfg):
    T, TC, TB = cfg.T, cfg.TC, cfg.TB
    assert TC % TB == 0 and T % TB == 0 and cfg.CK // 2 < _CPAD
    return T // TB, [(T + j * TB, T + _CPAD + j * TB) for j in range(TC // TB)]


def _fill_padded(cfg, pb, get):
    T, TC, TB = cfg.T, cfg.TC, cfg.TB
    z = jnp.zeros((_CPAD, LANE), F32)
    pb[0:_CPAD, :] = z
    pb[_CPAD + T:2 * _CPAD + T, :] = z
    pb[2 * _CPAD + T + TC:3 * _CPAD + T + TC, :] = z

    def fill(i, c):
        r0 = pl.multiple_of(i * TB, TB)
        pb[pl.ds(r0 + _CPAD, TB), :] = get(r0)
        return c

    lax.fori_loop(0, T // TB, fill, 0)
    for j in range(TC // TB):
        pb[2 * _CPAD + T + j * TB:2 * _CPAD + T + (j + 1) * TB, :] = get(T + j * TB)


def _taps(win, TB):
    W = TB + 2 * _CPAD
    rot = {0: win}

    def tap(k):
        a, b = divmod(k + 1, 8)
        if b not in rot:
            rot[b] = pltpu.roll(win, W - b, 0)
        return rot[b][8 * a:8 * a + TB, :]

    return tap


def glu_dwconv_fwd(cfg, name, P, w, b):
    L = P.shape[0]
    T, TC, TB, K = cfg.T, cfg.TC, cfg.TB, cfg.CK
    off = _offsets(cfg)
    ca, cb = off["la"] // LANE, off["lb"] // LANE
    nlat, ctx_tiles = _conv_windows(cfg)
    PBL = 3 * _CPAD + T + TC

    def body(a_ref, b_ref, w_ref, bias_ref, y_ref, pb):
        _fill_padded(cfg, pb, lambda r0: a_ref[pl.ds(r0, TB), :] * _sigmoid(b_ref[pl.ds(r0, TB), :]))
        wv = w_ref[...]
        bias = bias_ref[...]

        def tile(win):
            tap = _taps(win, TB)
            acc = jnp.zeros((TB, LANE), F32) + bias
            for k in range(K):
                acc = acc + wv[k:k + 1, :] * tap(k)
            return acc

        def lat(i, c):
            r0 = pl.multiple_of(i * TB, TB)
            y_ref[pl.ds(r0, TB), :] = tile(pb[pl.ds(r0, TB + 2 * _CPAD), :])
            return c

        lax.fori_loop(0, nlat, lat, 0)
        for r0, w0 in ctx_tiles:
            y_ref[r0:r0 + TB, :] = tile(pb[w0:w0 + TB + 2 * _CPAD, :])

    return _pc(body, name=name, grid=(cfg.CW // LANE,),
               in_specs=[pl.BlockSpec((L, LANE), lambda j: (0, ca + j)),
                         pl.BlockSpec((L, LANE), lambda j: (0, cb + j)),
                         pl.BlockSpec((32, LANE), lambda j: (0, j)),
                         pl.BlockSpec((1, LANE), lambda j: (0, j))],
               out_specs=pl.BlockSpec((L, LANE), lambda j: (0, j)),
               out_shape=jax.ShapeDtypeStruct((L, cfg.CW), F32),
               scratch_shapes=[pltpu.VMEM((PBL, LANE), F32)],
               compiler_params=_cp(("parallel",)))(P, P, w, b)


def glu_dwconv_bwd(cfg, name, P, w, dy):
    L = P.shape[0]
    T, TC, TB, K = cfg.T, cfg.TC, cfg.TB, cfg.CK
    off = _offsets(cfg)
    ca, cb = off["la"] // LANE, off["lb"] // LANE
    nlat, ctx_tiles = _conv_windows(cfg)
    PBL = 3 * _CPAD + T + TC

    def body(a_ref, b_ref, w_ref, dy_ref, da_ref, db_ref, dw_ref, dbias_ref, pbu, pbd):
        _fill_padded(cfg, pbu, lambda r0: a_ref[pl.ds(r0, TB), :] * _sigmoid(b_ref[pl.ds(r0, TB), :]))
        _fill_padded(cfg, pbd, lambda r0: dy_ref[pl.ds(r0, TB), :])
        wv = w_ref[...]
        dw_ref[...] = jnp.zeros_like(dw_ref)
        dbias_ref[...] = jnp.zeros_like(dbias_ref)

        def tile(r0, winu, wind):
            tapu = _taps(winu, TB)
            tapd = _taps(wind, TB)
            dyt = dy_ref[pl.ds(r0, TB), :]
            du = jnp.zeros((TB, LANE), F32)
            for k in range(K):
                du = du + wv[k:k + 1, :] * tapd(K - 1 - k)
                dw_ref[k:k + 1, :] += jnp.sum(dyt * tapu(k), axis=0, keepdims=True)
            dbias_ref[...] += jnp.sum(dyt, axis=0, keepdims=True)
            a = a_ref[pl.ds(r0, TB), :]
            sg = _sigmoid(b_ref[pl.ds(r0, TB), :])
            da_ref[pl.ds(r0, TB), :] = (du * sg).astype(da_ref.dtype)
            db_ref[pl.ds(r0, TB), :] = (du * a * sg * (1.0 - sg)).astype(db_ref.dtype)

        def lat(i, c):
            r0 = pl.multiple_of(i * TB, TB)
            tile(r0, pbu[pl.ds(r0, TB + 2 * _CPAD), :], pbd[pl.ds(r0, TB + 2 * _CPAD), :])
            return c

        lax.fori_loop(0, nlat, lat, 0)
        for r0, w0 in ctx_tiles:
            tile(r0, pbu[w0:w0 + TB + 2 * _CPAD, :], pbd[w0:w0 + TB + 2 * _CPAD, :])

    col = pl.BlockSpec((L, LANE), lambda j: (0, j))
    return _pc(body, name=name, grid=(cfg.CW // LANE,),
               in_specs=[pl.BlockSpec((L, LANE), lambda j: (0, ca + j)),
                         pl.BlockSpec((L, LANE), lambda j: (0, cb + j)),
                         pl.BlockSpec((32, LANE), lambda j: (0, j)), col],
               out_specs=[col, col, pl.BlockSpec((32, LANE), lambda j: (0, j)),
                          pl.BlockSpec((1, LANE), lambda j: (0, j))],
               out_shape=[jax.ShapeDtypeStruct((L, cfg.CW), BF16), jax.ShapeDtypeStruct((L, cfg.CW), BF16),
                          jax.ShapeDtypeStruct((32, cfg.CW), F32), jax.ShapeDtypeStruct((1, cfg.CW), F32)],
               scratch_shapes=[pltpu.VMEM((PBL, LANE), F32), pltpu.VMEM((PBL, LANE), F32)],
               compiler_params=_cp(("parallel",)))(P, P, w, dy)


def ln_silu_fwd(cfg, name, y, g, b):
    L, W = y.shape
    TB = cfg.TB

    def body(y_ref, g_ref, b_ref, o_ref):
        yv = y_ref[...]
        mu = jnp.mean(yv, axis=-1, keepdims=True)
        xc = yv - mu
        var = jnp.mean(xc * xc, axis=-1, keepdims=True)
        z = xc * lax.rsqrt(var + EPS) * g_ref[...] + b_ref[...]
        o_ref[...] = _silu(z).astype(o_ref.dtype)

    row = pl.BlockSpec((TB, W), lambda i: (i, 0))
    vec = pl.BlockSpec((1, W), lambda i: (0, 0))
    return _pc(body, name=name, grid=(L // TB,), in_specs=[row, vec, vec], out_specs=row,
               out_shape=jax.ShapeDtypeStruct((L, W), BF16), compiler_params=_cp(("parallel",)))(y, g, b)


def ln_silu_bwd(cfg, name, dact, y, g, b):
    L, W = y.shape
    TB = cfg.TB

    def body(d_ref, y_ref, g_ref, b_ref, dy_ref, dg_ref, db_ref):
        i = pl.program_id(0)
        yv = y_ref[...]
        mu = jnp.mean(yv, axis=-1, keepdims=True)
        xc = yv - mu
        var = jnp.mean(xc * xc, axis=-1, keepdims=True)
        r = lax.rsqrt(var + EPS)
        yh = xc * r
        g = g_ref[...]
        z = yh * g + b_ref[...]
        dz = d_ref[...] * _dsilu(z)
        sg = jnp.sum(dz * yh, axis=0, keepdims=True)
        sb = jnp.sum(dz, axis=0, keepdims=True)

        @pl.when(i == 0)
        def _():
            dg_ref[...] = sg
            db_ref[...] = sb

        @pl.when(i > 0)
        def _():
            dg_ref[...] += sg
            db_ref[...] += sb

        dh = dz * g
        dy_ref[...] = r * (dh - jnp.mean(dh, axis=-1, keepdims=True)
                           - yh * jnp.mean(dh * yh, axis=-1, keepdims=True))

    row = pl.BlockSpec((TB, W), lambda i: (i, 0))
    vec = pl.BlockSpec((1, W), lambda i: (0, 0))
    return _pc(body, name=name, grid=(L // TB,), in_specs=[row, row, vec, vec], out_specs=[row, vec, vec],
               out_shape=[jax.ShapeDtypeStruct((L, W), F32), jax.ShapeDtypeStruct((1, W), F32),
                          jax.ShapeDtypeStruct((1, W), F32)],
               compiler_params=_cp(("arbitrary",)))(dact, y, g, b)


def _na_geometry(cfg):
    R = cfg.T // cfg.GW
    nb = R // cfg.NAR
    assert nb >= 3 and cfg.GW == 64 and cfg.NAR == 8
    ks = [int(np.clip(8 * b - 4, 0, R - 16)) for b in range(nb)]
    return R, nb, ks


_NTAB = 18


def _split3(x):
    hi = x.astype(BF16)
    r = x - hi.astype(F32)
    mid = r.astype(BF16)
    lo = (r - mid.astype(F32)).astype(BF16)
    return hi, mid, lo


def _na_col_onehot(cfg):
    GW, NAC = cfg.GW, cfg.NAC
    qc = np.arange(GW)[:, None]
    kc = np.arange(GW)[None, :]
    cs = np.clip(qc - NAC // 2, 0, GW - NAC)
    vcol = (kc >= cs) & (kc < cs + NAC)
    dd = np.clip(kc - qc + NAC - 1, 0, 2 * NAC - 2)
    oh = (np.arange(LANE)[:, None, None] == dd[None]).astype(np.float32)
    z = np.zeros_like(oh)
    oda = np.concatenate([oh, z], axis=2).reshape(LANE, GW * LANE)
    odb = np.concatenate([z, oh], axis=2).reshape(LANE, GW * LANE)
    cm = np.where(np.concatenate([vcol, vcol], axis=1), 0.0, NEG).astype(np.float32).reshape(1, GW * LANE)
    return oda, odb, cm


def na_tables(cfg, name, rpb):
    NH, GW = cfg.NH, cfg.GW
    na = rpb.shape[1]
    oda, odb, cm = _na_col_onehot(cfg)
    rp = jnp.zeros((NH, _NTAB + 1, LANE), F32).at[:, 1:1 + na, :rpb.shape[2]].set(rpb.astype(F32))
    r0 = rp[:, :_NTAB].reshape(NH * _NTAB, LANE)
    r1 = rp[:, 1:].reshape(NH * _NTAB, LANE)
    a = np.arange(_NTAB) - 1
    rm0 = np.where((a >= 0) & (a < na), 0.0, NEG).astype(np.float32)
    rm1 = np.where((a + 1 >= 0) & (a + 1 < na), 0.0, NEG).astype(np.float32)
    half = (np.arange(GW * LANE) % LANE >= GW)[None, :]
    rmask = np.where(half, np.tile(rm1, NH)[:, None], np.tile(rm0, NH)[:, None]).astype(np.float32)
    tn = 2048
    rows = NH * _NTAB

    def body(r0_ref, r1_ref, a_ref, b_ref, cm_ref, rm_ref, o_ref):
        acc = cm_ref[...] + rm_ref[...]
        for t in _split3(r0_ref[...]):
            acc = acc + dot_nn(t, a_ref[...])
        for t in _split3(r1_ref[...]):
            acc = acc + dot_nn(t, b_ref[...])
        o_ref[...] = acc

    rs = pl.BlockSpec((rows, LANE), lambda n: (0, 0))
    out = _pc(body, name=name, grid=(GW * LANE // tn,),
              in_specs=[rs, rs, pl.BlockSpec((LANE, tn), lambda n: (0, n)), pl.BlockSpec((LANE, tn), lambda n: (0, n)),
                        pl.BlockSpec((1, tn), lambda n: (0, n)), pl.BlockSpec((rows, tn), lambda n: (0, n))],
              out_specs=pl.BlockSpec((rows, tn), lambda n: (0, n)),
              out_shape=jax.ShapeDtypeStruct((rows, GW * LANE), F32),
              compiler_params=_cp(("parallel",)))(r0, r1, jnp.asarray(oda, BF16), jnp.asarray(odb, BF16),
                                                  jnp.asarray(cm), jnp.asarray(rmask))
    return out.reshape(NH, _NTAB, GW, LANE)


def _na_tiles(cfg, b):
    R, nb, _ = _na_geometry(cfg)
    NAR = cfg.NAR
    ksb = jnp.clip(8 * b - 4, 0, R - 16)
    for i in range(8):
        qr = 8 * b + i
        ws = jnp.clip(qr - NAR // 2, 0, R - NAR)
        for J in range(8):
            kr0 = ksb + 2 * J
            row = jnp.clip(kr0 - qr + NAR - 1, -1, _NTAB - 2) + 1
            v0 = jnp.logical_and(kr0 >= ws, kr0 < ws + NAR)
            v1 = jnp.logical_and(kr0 + 1 >= ws, kr0 + 1 < ws + NAR)
            yield i, J, row, v0, v1


def _na_fill_bias(cfg, tab_ref, bias, b):
    GW = cfg.GW
    first = lax.broadcasted_iota(jnp.int32, (GW, LANE), 1) < GW
    for i, J, row, v0, v1 in _na_tiles(cfg, b):
        ok = jnp.where(first, v0.astype(jnp.int32), v1.astype(jnp.int32))
        bias[i * GW:(i + 1) * GW, J * LANE:(J + 1) * LANE] = jnp.where(ok > 0, tab_ref[row], NEG)


def _na_specs(cfg):
    R, nb, ks = _na_geometry(cfg)
    off = _offsets(cfg)
    TQ = 8 * cfg.GW
    KP = 4 * cfg.GW
    ks4 = [k // 4 for k in ks]
    lat_blocks = cfg.T // KP

    def ks4_of(b):
        return jnp.clip(2 * b - 1, 0, R // 4 - 4)

    assert all(int(np.clip(2 * b - 1, 0, R // 4 - 4)) == ks4[b] for b in range(nb))
    assert cfg.TC == KP

    def col(nm):
        c0 = off[nm] // LANE
        q = pl.BlockSpec((TQ, LANE), lambda h, b: (b, c0 + h))
        parts = [pl.BlockSpec((KP, LANE), functools.partial(lambda h, b, t: (ks4_of(b) + t, c0 + h), t=t))
                 for t in range(4)]
        ctx = pl.BlockSpec((KP, LANE), lambda h, b: (lat_blocks, c0 + h))
        return q, parts, ctx

    return nb, TQ, KP, ks4_of, col


def na_fwd(cfg, name, P, tab):
    nb, TQ, KP, ks4_of, col = _na_specs(cfg)
    NH = cfg.NH
    scale = cfg.NDH ** -0.5
    qs, _, _ = col("nq")
    _, kparts, kctx = col("nk")
    _, vparts, vctx = col("nv")

    def body(q_ref, k0, k1, k2, k3, kc_ref, v0, v1, v2, v3, vc_ref, tab_ref, o_ref, lse_ref, bias_ref):
        _na_fill_bias(cfg, tab_ref, bias_ref, pl.program_id(1))
        q = (q_ref[...] * scale).astype(BF16)
        kl = jnp.concatenate([k0[...], k1[...], k2[...], k3[...]], axis=0).astype(BF16)
        vl = jnp.concatenate([v0[...], v1[...], v2[...], v3[...]], axis=0).astype(BF16)
        kc = kc_ref[...].astype(BF16)
        vc = vc_ref[...].astype(BF16)
        sl = dot_nt(q, kl) + bias_ref[...]
        sc = dot_nt(q, kc)
        m = jnp.maximum(jnp.max(sl, axis=-1, keepdims=True), jnp.max(sc, axis=-1, keepdims=True))
        pl_ = jnp.exp(sl - m)
        pc = jnp.exp(sc - m)
        den = jnp.sum(pl_, axis=-1, keepdims=True) + jnp.sum(pc, axis=-1, keepdims=True)
        o = dot_nn(pl_.astype(BF16), vl) + dot_nn(pc.astype(BF16), vc)
        o_ref[...] = o / den
        lse_ref[...] = m + jnp.log(den)

    return _pc(body, name=name, grid=(NH, nb),
               in_specs=[qs, *kparts, kctx, *vparts, vctx,
                         pl.BlockSpec((None, _NTAB, cfg.GW, LANE), lambda h, b: (h, 0, 0, 0))],
               out_specs=[pl.BlockSpec((TQ, LANE), lambda h, b: (b, h)),
                          pl.BlockSpec((None, TQ, 1), lambda h, b: (h, b, 0))],
               out_shape=[jax.ShapeDtypeStruct((cfg.T, NH * LANE), F32),
                          jax.ShapeDtypeStruct((NH, cfg.T, 1), F32)],
               scratch_shapes=[pltpu.VMEM((TQ, 4 * KP), F32)],
               compiler_params=_cp(("parallel", "parallel")))(P, *([P] * 5), *([P] * 5), tab)


def na_bwd(cfg, name, P, tab, o, lse, dmix, dcol0):
    nb, TQ, KP, ks4_of, col = _na_specs(cfg)
    NH, GW = cfg.NH, cfg.GW
    L = P.shape[0]
    scale = cfg.NDH ** -0.5
    qs, _, _ = col("nq")
    _, kparts, kctx = col("nk")
    _, vparts, vctx = col("nv")

    def body(q_ref, k0, k1, k2, k3, kc_ref, v0, v1, v2, v3, vc_ref, tab_ref, o_ref, lse_ref, do_ref,
             dq_ref, dk_ref, dv_ref, dtab_ref, bias_ref):
        b = pl.program_id(1)

        @pl.when(b == 0)
        def _():
            dk_ref[...] = jnp.zeros_like(dk_ref)
            dv_ref[...] = jnp.zeros_like(dv_ref)
            dtab_ref[...] = jnp.zeros_like(dtab_ref)

        _na_fill_bias(cfg, tab_ref, bias_ref, b)

        q = (q_ref[...] * scale).astype(BF16)
        kl = jnp.concatenate([k0[...], k1[...], k2[...], k3[...]], axis=0).astype(BF16)
        vl = jnp.concatenate([v0[...], v1[...], v2[...], v3[...]], axis=0).astype(BF16)
        kc = kc_ref[...].astype(BF16)
        vc = vc_ref[...].astype(BF16)
        lse = lse_ref[...]
        do = do_ref[...]
        dob = do.astype(BF16)
        p_l = jnp.exp(dot_nt(q, kl) + bias_ref[...] - lse)
        p_c = jnp.exp(dot_nt(q, kc) - lse)
        delta = jnp.sum(do * o_ref[...], axis=-1, keepdims=True)
        ds_l = p_l * (dot_nt(dob, vl) - delta)
        ds_c = p_c * (dot_nt(dob, vc) - delta)
        dslb = ds_l.astype(BF16)
        dscb = ds_c.astype(BF16)
        dq_ref[...] = ((dot_nn(dslb, kl) + dot_nn(dscb, kc)) * scale).astype(dq_ref.dtype)
        r0 = pl.multiple_of(ks4_of(b) * KP, KP)
        dk_ref[pl.ds(r0, 4 * KP), :] += dot_tn(dslb, q)
        dv_ref[pl.ds(r0, 4 * KP), :] += dot_tn(p_l.astype(BF16), dob)
        dk_ref[cfg.T:cfg.T + KP, :] += dot_tn(dscb, q)
        dv_ref[cfg.T:cfg.T + KP, :] += dot_tn(p_c.astype(BF16), dob)
        bias_ref[...] = ds_l
        for i, J, row, _, _ in _na_tiles(cfg, b):
            dtab_ref[row] += bias_ref[i * GW:(i + 1) * GW, J * LANE:(J + 1) * LANE]

    full = pl.BlockSpec((L, LANE), lambda h, b: (0, h))
    tabs = pl.BlockSpec((None, _NTAB, GW, LANE), lambda h, b: (h, 0, 0, 0))
    return _pc(body, name=name, grid=(NH, nb),
               in_specs=[qs, *kparts, kctx, *vparts, vctx, tabs,
                         pl.BlockSpec((TQ, LANE), lambda h, b: (b, h)),
                         pl.BlockSpec((None, TQ, 1), lambda h, b: (h, b, 0)),
                         pl.BlockSpec((TQ, LANE), lambda h, b: (b, dcol0 + h))],
               out_specs=[pl.BlockSpec((TQ, LANE), lambda h, b: (b, h)), full, full, tabs],
               out_shape=[jax.ShapeDtypeStruct((cfg.T, NH * LANE), BF16),
                          jax.ShapeDtypeStruct((L, NH * LANE), F32), jax.ShapeDtypeStruct((L, NH * LANE), F32),
                          jax.ShapeDtypeStruct((NH, _NTAB, GW, LANE), F32)],
               scratch_shapes=[pltpu.VMEM((TQ, 4 * KP), F32)],
               compiler_params=_cp(("parallel", "arbitrary")))(
                   P, *([P] * 5), *([P] * 5), tab, o, lse, dmix)


def na_ctx_fwd(cfg, name, P):
    off = _offsets(cfg)
    TC, NH = cfg.TC, cfg.NH
    rb = cfg.T // TC
    scale = cfg.NDH ** -0.5

    def body(q_ref, k_ref, v_ref, o_ref, lse_ref):
        q = (q_ref[...] * scale).astype(BF16)
        s = dot_nt(q, k_ref[...].astype(BF16))
        m = jnp.max(s, axis=-1, keepdims=True)
        p = jnp.exp(s - m)
        den = jnp.sum(p, axis=-1, keepdims=True)
        o_ref[...] = dot_nn(p.astype(BF16), v_ref[...].astype(BF16)) / den
        lse_ref[...] = m + jnp.log(den)

    spec = lambda nm: pl.BlockSpec((TC, LANE), functools.partial(lambda h, c0: (rb, c0 + h), c0=off[nm] // LANE))
    return _pc(body, name=name, grid=(NH,), in_specs=[spec("nq"), spec("nk"), spec("nv")],
               out_specs=[pl.BlockSpec((TC, LANE), lambda h: (0, h)), pl.BlockSpec((None, TC, 1), lambda h: (h, 0, 0))],
               out_shape=[jax.ShapeDtypeStruct((TC, NH * LANE), F32), jax.ShapeDtypeStruct((NH, TC, 1), F32)],
               compiler_params=_cp(("parallel",)))(P, P, P)


def na_ctx_bwd(cfg, name, P, o, lse, dmix, dcol0, dk_in, dv_in):
    off = _offsets(cfg)
    TC, NH = cfg.TC, cfg.NH
    rb = cfg.T // TC
    scale = cfg.NDH ** -0.5

    def body(q_ref, k_ref, v_ref, o_ref, lse_ref, do_ref, dki_ref, dvi_ref, dq_ref, dk_ref, dv_ref):
        q = (q_ref[...] * scale).astype(BF16)
        kb = k_ref[...].astype(BF16)
        vb = v_ref[...].astype(BF16)
        do = do_ref[...]
        dob = do.astype(BF16)
        p = jnp.exp(dot_nt(q, kb) - lse_ref[...])
        delta = jnp.sum(do * o_ref[...], axis=-1, keepdims=True)
        ds = (p * (dot_nt(dob, vb) - delta)).astype(BF16)
        dq_ref[...] = (dot_nn(ds, kb) * scale).astype(dq_ref.dtype)
        dk_ref[...] = (dki_ref[...] + dot_tn(ds, q)).astype(dk_ref.dtype)
        dv_ref[...] = (dvi_ref[...] + dot_tn(p.astype(BF16), dob)).astype(dv_ref.dtype)

    spec = lambda nm: pl.BlockSpec((TC, LANE), functools.partial(lambda h, c0: (rb, c0 + h), c0=off[nm] // LANE))
    hb = pl.BlockSpec((TC, LANE), lambda h: (0, h))
    ctxrow = pl.BlockSpec((TC, LANE), lambda h: (rb, h))
    shp = jax.ShapeDtypeStruct((TC, NH * LANE), BF16)
    return _pc(body, name=name, grid=(NH,),
               in_specs=[spec("nq"), spec("nk"), spec("nv"), hb, pl.BlockSpec((None, TC, 1), lambda h: (h, 0, 0)),
                         pl.BlockSpec((TC, LANE), lambda h: (rb, dcol0 + h)), ctxrow, ctxrow],
               out_specs=[hb, hb, hb], out_shape=[shp, shp, shp],
               compiler_params=_cp(("parallel",)))(P, P, P, o, lse, dmix, dk_in, dv_in)


def na_rpb_grad(cfg, name, dtab):
    NH, GW = cfg.NH, cfg.GW
    na, nd = 2 * cfg.NAR - 1, 2 * cfg.NAC - 1
    oda, odb, _ = _na_col_onehot(cfg)
    E = np.concatenate([oda.T, odb.T], axis=1)
    rows = NH * _NTAB

    def body(z_ref, e_ref, o_ref):
        zv = z_ref[...]
        hi = zv.astype(BF16)
        lo = (zv - hi.astype(F32)).astype(BF16)
        e = e_ref[...]
        o_ref[...] = dot_nn(hi, e) + dot_nn(lo, e)

    g = _pc(body, name=name, out_shape=jax.ShapeDtypeStruct((rows, 2 * LANE), F32),
            compiler_params=_cp())(dtab.reshape(rows, GW * LANE), jnp.asarray(E, BF16))
    g = g.reshape(NH, _NTAB, 2, LANE)
    return g[:, 1:1 + na, 0, :nd] + g[:, 0:na, 1, :nd]


def _seq_tiles(cfg):
    T, TC, TB = cfg.T, cfg.TC, cfg.TB
    tiles = []
    for i in range((T + TC) // TB):
        r0 = i * TB
        tiles.append((r0, r0 == 0 or r0 == T, r0 + TB == T or r0 + TB == T + TC))
    return tiles


def _shift3(ref_get, r0, TB, start, end, width):
    cur = ref_get(r0, TB)
    if start or end:
        rowi = lax.broadcasted_iota(jnp.int32, (TB, width), 0)
    up = jnp.where(rowi == 0, 0.0, pltpu.roll(cur, 1, 0)) if start else ref_get(r0 - 1, TB)
    dn = jnp.where(rowi == TB - 1, 0.0, pltpu.roll(cur, TB - 1, 0)) if end else ref_get(r0 + 1, TB)
    return up, cur, dn


def ffn_act_fwd(cfg, name, U2, w, b):
    _, L, DFF = U2.shape
    TB = cfg.TB
    tiles = _seq_tiles(cfg)

    def body(u_ref, w_ref, b_ref, a_ref):
        def plane(p, r0, st, en):
            up, cur, dn = _shift3(lambda r, n: u_ref[p, r:r + n, :], r0, TB, st, en, LANE)
            wv = w_ref[p]
            return wv[0:1, :] * up + wv[1:2, :] * cur + wv[2:3, :] * dn + b_ref[p]

        for r0, st, en in tiles:
            val = plane(0, r0, st, en)
            gate = plane(1, r0, st, en)
            a_ref[r0:r0 + TB, :] = (_silu(gate) * val).astype(a_ref.dtype)

    return _pc(body, name=name, grid=(DFF // LANE,),
               in_specs=[pl.BlockSpec((2, L, LANE), lambda j: (0, 0, j)),
                         pl.BlockSpec((2, 8, LANE), lambda j: (0, 0, j)),
                         pl.BlockSpec((2, 1, LANE), lambda j: (0, 0, j))],
               out_specs=pl.BlockSpec((L, LANE), lambda j: (0, j)),
               out_shape=jax.ShapeDtypeStruct((L, DFF), BF16),
               compiler_params=_cp(("parallel",)))(U2, w, b)


def ffn_act_bwd(cfg, name, U2, w, b, dA):
    _, L, DFF = U2.shape
    TB = cfg.TB
    tiles = _seq_tiles(cfg)

    def body(u_ref, w_ref, b_ref, da_ref, du_ref, dw_ref, db_ref, dbuf):
        dw_ref[...] = jnp.zeros_like(dw_ref)
        db_ref[...] = jnp.zeros_like(db_ref)
        for r0, st, en in tiles:
            shifted = []
            pre = []
            for p in range(2):
                up, cur, dn = _shift3(lambda r, n: u_ref[p, r:r + n, :], r0, TB, st, en, LANE)
                wv = w_ref[p]
                shifted.append((up, cur, dn))
                pre.append(wv[0:1, :] * up + wv[1:2, :] * cur + wv[2:3, :] * dn + b_ref[p])
            val, gate = pre
            da = da_ref[r0:r0 + TB, :]
            dpre = (da * _silu(gate), da * val * _dsilu(gate))
            for p in range(2):
                dbuf[p, r0:r0 + TB, :] = dpre[p]
                for k in range(3):
                    dw_ref[p, k:k + 1, :] += jnp.sum(dpre[p] * shifted[p][k], axis=0, keepdims=True)
                db_ref[p] += jnp.sum(dpre[p], axis=0, keepdims=True)
        for r0, st, en in tiles:
            for p in range(2):
                up, cur, dn = _shift3(lambda r, n: dbuf[p, r:r + n, :], r0, TB, st, en, LANE)
                wv = w_ref[p]
                du_ref[p, r0:r0 + TB, :] = (wv[0:1, :] * dn + wv[1:2, :] * cur + wv[2:3, :] * up).astype(du_ref.dtype)

    blk = pl.BlockSpec((2, L, LANE), lambda j: (0, 0, j))
    wspec = pl.BlockSpec((2, 8, LANE), lambda j: (0, 0, j))
    bspec = pl.BlockSpec((2, 1, LANE), lambda j: (0, 0, j))
    return _pc(body, name=name, grid=(DFF // LANE,),
               in_specs=[blk, wspec, bspec, pl.BlockSpec((L, LANE), lambda j: (0, j))],
               out_specs=[blk, wspec, bspec],
               out_shape=[jax.ShapeDtypeStruct((2, L, DFF), BF16), jax.ShapeDtypeStruct((2, 8, DFF), F32),
                          jax.ShapeDtypeStruct((2, 1, DFF), F32)],
               scratch_shapes=[pltpu.VMEM((2, L, LANE), F32)],
               compiler_params=_cp(("parallel",)))(U2, w, b, dA)


def _tm(L, parts):
    assert L % parts == 0
    return L // parts


def layer_fwd(cfg, l, XS, mod, wts, small, tabs):
    L, D = XS.shape
    off = _offsets(cfg)
    DIN = off["end"]
    tmA = _tm(L, 4)
    sv = {"XS": XS, "W": {}}

    def weight(name, after):
        sv["W"][name], tok = wts(name, after)
        return sv["W"][name], tok

    Win4, _ = weight("w_in", XS)
    nbi = Win4.shape[3]
    h1 = norm_mod_fwd(cfg, f"norm1_fwd_{l}", XS, small["norm1_g"], mod["sc1"], mod["sh1"])
    P = matmul(f"mm_in_{l}", h1, Win4, contract="nn", grid=(4, L // tmA),
               a_spec=pl.BlockSpec((tmA, D), lambda n, m: (m, 0)),
               b_spec=pl.BlockSpec((None, None, D, nbi), lambda n, m: (n, l, 0, 0)),
               out_shape=jax.ShapeDtypeStruct((L, DIN), F32),
               out_spec=pl.BlockSpec((tmA, nbi), lambda n, m: (m, n)), nk=1)
    qr, kr = rope_fwd(cfg, f"rope_fwd_{l}", P, tabs["cos"], tabs["sin"])
    o_f, o_b, st = retention_fwd(cfg, f"ret_fwd_{l}", qr, kr, P, small["lam"])
    o2 = (o_f, o_b)
    ret = ggn_fwd(cfg, f"ggn_fwd_{l}", o2, P, small["ret_gn_g"])
    ycv = glu_dwconv_fwd(cfg, f"dwconv_fwd_{l}", P, small["conv_dw_w"], small["conv_dw_b"])
    act = ln_silu_fwd(cfg, f"ln_silu_fwd_{l}", ycv, small["conv_ln_g"], small["conv_ln_b"])
    Wpw4, _ = weight("conv_pw", act)
    cv = mm_rowsharded(f"mm_pw_{l}", act, Wpw4, l, BF16, cfg.CW)
    bias = na_tables(cfg, f"na_tables_{l}", small["na_rpb"])
    na_l, lse = na_fwd(cfg, f"na_fwd_{l}", P, bias)
    na_c, lse_c = na_ctx_fwd(cfg, f"na_ctx_fwd_{l}", P)
    mix = jnp.concatenate([ret, cv, jnp.concatenate([na_l, na_c], axis=0).astype(BF16)], axis=1)
    Wout4, tok = weight("w_out", mix)
    Y1 = mm_rowsharded(f"mm_out_{l}", mix, Wout4, l, F32, D)
    XM = resid_fwd(cfg, f"resid1_fwd_{l}", XS, Y1, mod["g1"] if tok is None else mod["g1"] + tok)
    h2 = norm_mod_fwd(cfg, f"norm2_fwd_{l}", XM, small["norm2_g"], mod["sc2"], mod["sh2"])
    Wup4, _ = weight("ffn_up", h2)
    nbu = Wup4.shape[3]
    tnu = nbu // 2
    U2 = matmul(f"mm_up_{l}", h2, Wup4, contract="nn", grid=(8, L // tmA),
                a_spec=pl.BlockSpec((tmA, D), lambda n, m: (m, 0)),
                b_spec=pl.BlockSpec((None, None, D, tnu), lambda n, m: (n // 2, l, 0, n % 2)),
                out_shape=jax.ShapeDtypeStruct((2, L, cfg.DFF), F32),
                out_spec=pl.BlockSpec((None, tmA, tnu), lambda n, m: (n // 4, m, n % 4)), nk=1)
    A = ffn_act_fwd(cfg, f"ffn_act_fwd_{l}", U2, small["ffn_dw_w"], small["ffn_dw_b"])
    Wdn4, _ = weight("ffn_down", A)
    Y2 = mm_rowsharded(f"mm_down_{l}", A, Wdn4, l, F32, D // 2)
    XO = resid_fwd(cfg, f"resid2_fwd_{l}", XM, Y2, mod["g2"])
    sv.update(h1=h1, P=P, qr=qr, kr=kr, o2=o2, st=st, ycv=ycv, act=act, bias=bias, na_l=na_l, lse=lse,
              na_c=na_c, lse_c=lse_c, mix=mix, Y1=Y1, XM=XM, h2=h2, U2=U2, A=A, Y2=Y2)
    return XO, sv


GRAD_GROUPS = (("ffn_down", "ffn_up"), ("w_out", "conv_pw", "w_in"))


def layer_bwd(cfg, l, dXO, sv, mod, wts, small, tabs, gbuf, ready):
    L, D = dXO.shape
    off = _offsets(cfg)
    DIN = off["end"]
    Win4, Wout4, Wup4, Wdn4, Wpw4 = wts["w_in"], wts["w_out"], wts["ffn_up"], wts["ffn_down"], wts["conv_pw"]
    tmA, tmB = _tm(L, 4), _tm(L, 8)
    depth = Win4.shape[1]
    gb, gs, dm = {}, {}, {}
    P = sv["P"]
    dY2, dm["g2"] = resid_bwd(cfg, f"resid2_bwd_{l}", dXO, sv["Y2"], mod["g2"])
    nbd = Wdn4.shape[2]
    dA = matmul(f"mm_down_da_{l}", dY2, Wdn4, contract="nt", grid=(4, L // tmA),
                a_spec=pl.BlockSpec((tmA, D), lambda j, m: (m, 0)),
                b_spec=pl.BlockSpec((None, None, nbd, D), lambda j, m: (j, l, 0, 0)),
                out_shape=jax.ShapeDtypeStruct((L, cfg.DFF), F32),
                out_spec=pl.BlockSpec((tmA, nbd), lambda j, m: (m, j)), nk=1)
    gb["ffn_down"] = wgrad(cfg, f"mm_down_dw_{l}", sv["A"], dY2,
                           lambda rb, ri: pl.BlockSpec((rb, nbd), lambda j, m: (ri(m), j)),
                           lambda rb, ri: pl.BlockSpec((rb, D), lambda j, m: (ri(m), 0)),
                           jax.ShapeDtypeStruct((depth, 4, nbd, D), BF16),
                           pl.BlockSpec((None, None, nbd, D), lambda j, m: (l, j, 0, 0)), 4, gbuf.get("ffn_down"))
    dU2, dfw, dfb = ffn_act_bwd(cfg, f"ffn_act_bwd_{l}", sv["U2"], small["ffn_dw_w"], small["ffn_dw_b"], dA)
    gs["ffn_dw_w"], gs["ffn_dw_b"] = dfw, dfb
    nbu = Wup4.shape[3]
    tnu = nbu // 2
    dH2 = matmul(f"mm_up_dh_{l}", dU2, Wup4, contract="nt", grid=(L // tmA, 8),
                 a_spec=pl.BlockSpec((None, tmA, tnu), lambda m, n: (n // 4, m, n % 4)),
                 b_spec=pl.BlockSpec((None, None, D, tnu), lambda m, n: (n // 2, l, 0, n % 2)),
                 out_shape=jax.ShapeDtypeStruct((L, D), F32),
                 out_spec=pl.BlockSpec((tmA, D), lambda m, n: (m, 0)), nk=8)
    gb["ffn_up"] = wgrad(cfg, f"mm_up_dw_{l}", sv["h2"], dU2,
                         lambda rb, ri: pl.BlockSpec((rb, D), lambda n, m: (ri(m), 0)),
                         lambda rb, ri: pl.BlockSpec((None, rb, tnu), lambda n, m: (n // 4, ri(m), n % 4)),
                         jax.ShapeDtypeStruct((depth, 4, D, nbu), BF16),
                         pl.BlockSpec((None, None, D, tnu), lambda n, m: (l, n // 2, 0, n % 2)), 8, gbuf.get("ffn_up"))
    dXM, dm["sc2"], dm["sh2"], gs["norm2_g"] = norm_mod_bwd(
        cfg, f"norm2_bwd_{l}", dH2, sv["XM"], small["norm2_g"], mod["sc2"], dXO)
    tok = ready(GRAD_GROUPS[0], gb)
    dY1, dm["g1"] = resid_bwd(cfg, f"resid1_bwd_{l}", dXM, sv["Y1"], mod["g1"] if tok is None else mod["g1"] + tok)
    nbo = Wout4.shape[2]
    dmix = matmul(f"mm_out_dmix_{l}", dY1, Wout4, contract="nt", grid=(4, L // tmA),
                  a_spec=pl.BlockSpec((tmA, D), lambda j, m: (m, 0)),
                  b_spec=pl.BlockSpec((None, None, nbo, D), lambda j, m: (j, l, 0, 0)),
                  out_shape=jax.ShapeDtypeStruct((L, D), F32),
                  out_spec=pl.BlockSpec((tmA, nbo), lambda j, m: (m, j)), nk=1)
    gb["w_out"] = wgrad(cfg, f"mm_out_dw_{l}", sv["mix"], dY1,
                        lambda rb, ri: pl.BlockSpec((rb, nbo), lambda j, m: (ri(m), j)),
                        lambda rb, ri: pl.BlockSpec((rb, D), lambda j, m: (ri(m), 0)),
                        jax.ShapeDtypeStruct((depth, 4, nbo, D), BF16),
                        pl.BlockSpec((None, None, nbo, D), lambda j, m: (l, j, 0, 0)), 4, gbuf.get("w_out"))
    RW = cfg.RH * cfg.RDV
    do, dlg, gs["ret_gn_g"] = ggn_bwd(cfg, f"ggn_bwd_{l}", dmix, sv["o2"], P, small["ret_gn_g"], 0)
    dqf, dqb, dkf, dkb, dvf, dvb, dlam = retention_bwd(
        cfg, f"ret_bwd_{l}", sv["qr"], sv["kr"], P, small["lam"], sv["st"], do)
    gs["lam"] = dlam[:, :, 0, 0]
    dlq, dlk = rope_bwd(cfg, f"rope_bwd_{l}", (dqf, dqb), (dkf, dkb), tabs["cos"], tabs["sin"])
    dlv = add_cast(cfg, f"ret_dv_{l}", dvf, dvb)
    dcv = cast_cols(cfg, f"conv_dcv_{l}", dmix, RW // LANE, cfg.CW // LANE, LANE)
    nbp = Wpw4.shape[2]
    dact = matmul(f"mm_pw_dact_{l}", dcv, Wpw4, contract="nt", grid=(4, L // tmA),
                  a_spec=pl.BlockSpec((tmA, cfg.CW), lambda j, m: (m, 0)),
                  b_spec=pl.BlockSpec((None, None, nbp, cfg.CW), lambda j, m: (j, l, 0, 0)),
                  out_shape=jax.ShapeDtypeStruct((L, cfg.CW), F32),
                  out_spec=pl.BlockSpec((tmA, nbp), lambda j, m: (m, j)), nk=1)
    gb["conv_pw"] = wgrad(cfg, f"mm_pw_dw_{l}", sv["act"], dcv,
                          lambda rb, ri: pl.BlockSpec((rb, nbp), lambda j, m: (ri(m), j)),
                          lambda rb, ri: pl.BlockSpec((rb, cfg.CW), lambda j, m: (ri(m), 0)),
                          jax.ShapeDtypeStruct((depth, 4, nbp, cfg.CW), BF16),
                          pl.BlockSpec((None, None, nbp, cfg.CW), lambda j, m: (l, j, 0, 0)), 4, gbuf.get("conv_pw"))
    dycv, gs["conv_ln_g"], gs["conv_ln_b"] = ln_silu_bwd(
        cfg, f"ln_silu_bwd_{l}", dact, sv["ycv"], small["conv_ln_g"], small["conv_ln_b"])
    dla, dlb, gs["conv_dw_w"], gs["conv_dw_b"] = glu_dwconv_bwd(cfg, f"dwconv_bwd_{l}", P, small["conv_dw_w"], dycv)
    nac0 = (RW + cfg.CW) // LANE
    dnq_l, dnk, dnv, dsb = na_bwd(cfg, f"na_bwd_{l}", P, sv["bias"], sv["na_l"], sv["lse"], dmix, nac0)
    dnq_c, dnk_c, dnv_c = na_ctx_bwd(cfg, f"na_ctx_bwd_{l}", P, sv["na_c"], sv["lse_c"], dmix, nac0, dnk, dnv)
    gs["na_rpb"] = na_rpb_grad(cfg, f"na_rpb_{l}", dsb)
    dnq = jnp.concatenate([dnq_l, dnq_c], axis=0)
    dnk = jnp.concatenate([dnk[:cfg.T].astype(BF16), dnk_c], axis=0)
    dnv = jnp.concatenate([dnv[:cfg.T].astype(BF16), dnv_c], axis=0)
    dP = jnp.concatenate([dlq, dlk, dlv, dlg, dla, dlb, dnq, dnk, dnv], axis=1)
    nbi = Win4.shape[3]
    dH1 = matmul(f"mm_in_dh_{l}", dP, Win4, contract="nt", grid=(L // tmA, 4),
                 a_spec=pl.BlockSpec((tmA, nbi), lambda m, n: (m, n)),
                 b_spec=pl.BlockSpec((None, None, D, nbi), lambda m, n: (n, l, 0, 0)),
                 out_shape=jax.ShapeDtypeStruct((L, D), F32),
                 out_spec=pl.BlockSpec((tmA, D), lambda m, n: (m, 0)), nk=4)
    gb["w_in"] = wgrad(cfg, f"mm_in_dw_{l}", sv["h1"], dP,
                       lambda rb, ri: pl.BlockSpec((rb, D), lambda n, m: (ri(m), 0)),
                       lambda rb, ri: pl.BlockSpec((rb, nbi), lambda n, m: (ri(m), n)),
                       jax.ShapeDtypeStruct((depth, 4, D, nbi), BF16),
                       pl.BlockSpec((None, None, D, nbi), lambda n, m: (l, n, 0, 0)), 4, gbuf.get("w_in"))
    dXS, dm["sc1"], dm["sh1"], gs["norm1_g"] = norm_mod_bwd(
        cfg, f"norm1_bwd_{l}", dH1, sv["XS"], small["norm1_g"], mod["sc1"], dXM, latent_only=(l == 0))
    return dXS, gb, gs, dm, ready(GRAD_GROUPS[1], gb)


def _layer_small(cfg, l, sp):
    DFF = cfg.DFF
    fw = sp["ffn_dw_w"][l].reshape(3, 2, DFF).transpose(1, 0, 2)
    fw = jnp.concatenate([fw, jnp.zeros((2, 5, DFF), F32)], axis=1)
    cw = jnp.concatenate([sp["conv_dw_w"][l], jnp.zeros((32 - cfg.CK, cfg.CW), F32)], axis=0)
    return dict(
        norm1_g=sp["norm1_g"][l][None], norm2_g=sp["norm2_g"][l][None],
        lam=jax.nn.log_sigmoid(sp["ret_decay"][l]), ret_gn_g=sp["ret_gn_g"][l][None],
        conv_dw_w=cw, conv_dw_b=sp["conv_dw_b"][l][None], conv_ln_g=sp["conv_ln_g"][l][None],
        conv_ln_b=sp["conv_ln_b"][l][None], na_rpb=sp["na_rpb"][l],
        ffn_dw_w=fw, ffn_dw_b=sp["ffn_dw_b"][l].reshape(2, 1, DFF))


def local_step(cfg, x, ctx, tgt, mods, wts, sp, grads_ready=lambda l, names, gb: None):
    depth = sp["norm1_g"].shape[0]
    cos, sin = rope_tables(cfg)
    tabs = dict(cos=cos, sin=sin)
    XS = jnp.concatenate([x, ctx], axis=0)
    smalls = [_layer_small(cfg, l, sp) for l in range(depth)]
    saves = []
    for l in range(depth):
        XS, sv = layer_fwd(cfg, l, XS, mods[l], functools.partial(wts, l), smalls[l], tabs)
        saves.append(sv)
    ls, dX, dfg = final_loss(cfg, "final_loss", XS, sp["final_g"][None], tgt)
    gb, gss, dms = {}, [None] * depth, [None] * depth
    token = None
    for l in reversed(range(depth)):
        mod = mods[l] if token is None else {**mods[l], "g2": mods[l]["g2"] + token}
        dX, gb, gss[l], dms[l], token = layer_bwd(cfg, l, dX, saves[l], mod, saves[l]["W"], smalls[l], tabs, gb,
                                                  functools.partial(grads_ready, l))
    return ls[0, 0], dX[:cfg.T], gb, gss, dms, dfg[0], token


MESH = pl.DeviceIdType.MESH
N_DEV = 8
N_CHIP = 4
BIG = ("w_in", "w_out", "ffn_up", "ffn_down", "conv_pw")
_ANY = pl.BlockSpec(memory_space=pl.ANY)


def _place():
    x, y, c = lax.axis_index("x"), lax.axis_index("y"), lax.axis_index("c")
    chips = [(1 - x, y), (x, 1 - y), (1 - x, 1 - y)]
    return x, y, c, chips


def allgather8(name, xs):
    m_per, n = xs.shape

    def body(x_ref, out_ref, send_sems, recv_sems, local_sem):
        x, y, c, chips = _place()
        me, sibling = (x, y, c), (x, y, 1 - c)

        def rows(px, py, pc):
            return out_ref.at[pl.ds((4 * px + 2 * py + pc) * m_per, m_per), :]

        def copy(k, block, to, src=None):
            return pltpu.make_async_remote_copy(
                src_ref=rows(*block) if src is None else src, dst_ref=rows(*block),
                send_sem=send_sems.at[k], recv_sem=recv_sems.at[k], device_id=to, device_id_type=MESH)

        mine = pltpu.make_async_copy(x_ref, rows(*me), local_sem)
        mine.start()
        first = [copy(0, me, sibling, src=x_ref)]
        first += [copy(1 + j, me, (*chip, c), src=x_ref) for j, chip in enumerate(chips)]
        for cp in first:
            cp.start()
        passed = [copy(4 + j, (*chip, c), sibling) for j, chip in enumerate(chips)]
        for j, chip in enumerate(chips):
            copy(1 + j, (*chip, c), me).wait_recv()
            passed[j].start()
        copy(0, sibling, me).wait_recv()
        for j, chip in enumerate(chips):
            copy(4 + j, (*chip, 1 - c), me).wait_recv()
        for cp in first + passed:
            cp.wait_send()
        mine.wait()

    return _pc(body, name=name, out_shape=jax.ShapeDtypeStruct((N_DEV * m_per, n), xs.dtype),
               in_specs=[pl.BlockSpec(memory_space=pltpu.VMEM)], out_specs=pl.BlockSpec(memory_space=pltpu.VMEM),
               scratch_shapes=[pltpu.SemaphoreType.DMA((7,)), pltpu.SemaphoreType.DMA((7,)), pltpu.SemaphoreType.DMA],
               compiler_params=pltpu.CompilerParams(vmem_limit_bytes=VMEM_LIMIT))(xs)


def _wpiece(ref, layer, chip_idx, half):
    rh = ref.shape[2] // 2
    return ref.at[chip_idx, layer, pl.ds(half * rh, rh)]


def _wcopy(ref, layer, chip_idx, half, send_sems, recv_sems, k, to):
    piece = _wpiece(ref, layer, chip_idx, half)
    return pltpu.make_async_remote_copy(src_ref=piece, dst_ref=piece, send_sem=send_sems.at[k],
                                        recv_sem=recv_sems.at[k], device_id=to, device_id_type=MESH)


def _w_ici_sends(outs, layer, send_sems, recv_sems):
    x, y, c, chips = _place()
    return [_wcopy(outs[a], layer, 2 * x + y, c, send_sems, recv_sems, 3 * a + t, (*chip, c))
            for a in range(len(outs)) for t, chip in enumerate(chips)]


def _w_ici_landed(outs, layer, send_sems, recv_sems):
    x, y, c, chips = _place()
    return [_wcopy(outs[a], layer, 2 * chip[0] + chip[1], c, send_sems, recv_sems, 3 * a + t, (x, y, c))
            for a in range(len(outs)) for t, chip in enumerate(chips)]


def _w_forward(outs, layer, send_sems, recv_sems, base):
    x, y, c, chips = _place()
    n = len(outs)
    sends = [_wcopy(outs[a], layer, 2 * chip[0] + chip[1], c, send_sems, recv_sems, base + 3 * a + t, (x, y, 1 - c))
             for a in range(n) for t, chip in enumerate(chips)]
    recvs = [_wcopy(outs[a], layer, 2 * chip[0] + chip[1], 1 - c, send_sems, recv_sems, base + 3 * a + t, (x, y, c))
             for a in range(n) for t, chip in enumerate(chips)]
    return sends, recvs


def allgather_layer(name, bufs, layer):
    n = len(bufs)

    def body(*refs):
        outs = refs[n:2 * n]
        send_sems, recv_sems = refs[2 * n:]
        sent = _w_ici_sends(outs, layer, send_sems, recv_sems)
        for cp in sent:
            cp.start()
        fwd, from_sib = _w_forward(outs, layer, send_sems, recv_sems, 3 * n)
        for landed, fw in zip(_w_ici_landed(outs, layer, send_sems, recv_sems), fwd):
            landed.wait_recv()
            fw.start()
        for cp in from_sib:
            cp.wait_recv()
        for cp in sent + fwd:
            cp.wait_send()

    return _pc(body, name=name, out_shape=[jax.ShapeDtypeStruct(b.shape, b.dtype) for b in bufs],
               in_specs=[_ANY] * n, out_specs=[_ANY] * n, input_output_aliases={a: a for a in range(n)},
               scratch_shapes=[pltpu.SemaphoreType.DMA((6 * n,)), pltpu.SemaphoreType.DMA((6 * n,))])(*bufs)


_HBM = pl.BlockSpec(memory_space=pltpu.HBM)
_SEM = pl.BlockSpec(memory_space=pltpu.SEMAPHORE)
_EFFECT = pltpu.SideEffectType.DATAFLOW_SIDE_EFFECTING


def allgather_layer_start(name, bufs, layer, after):
    n = len(bufs)

    def body(*refs):
        send_sems, recv_sems = refs[n + 1:n + 3]
        outs = refs[n + 3:2 * n + 3]
        token = refs[2 * n + 3]
        for cp in _w_ici_sends(outs, layer, send_sems, recv_sems):
            cp.start()
        token[...] = jnp.zeros_like(token)

    res = _pc(body, name=name,
              out_shape=(pltpu.SemaphoreType.DMA((3 * n,)), pltpu.SemaphoreType.DMA((3 * n,)),
                         *[pltpu.HBM(b.shape, b.dtype) for b in bufs], jax.ShapeDtypeStruct((8, LANE), F32)),
              in_specs=[_HBM] * n + [_ANY],
              out_specs=(_SEM, _SEM, *([_HBM] * n), pl.BlockSpec(memory_space=pltpu.VMEM)),
              input_output_aliases={a: a + 2 for a in range(n)},
              compiler_params=pltpu.CompilerParams(has_side_effects=_EFFECT))(
                  *[pltpu.with_memory_space_constraint(b, pltpu.HBM) for b in bufs], after)
    return res[0], res[1], list(res[2:2 + n]), res[2 + n]


def allgather_layer_wait(name, bufs, send_sems, recv_sems, after, layer):
    n = len(bufs)

    def body(*refs):
        ins = refs[:n]
        send_sems, recv_sems = refs[n:n + 2]
        for cp in _w_ici_sends(ins, layer, send_sems, recv_sems):
            cp.wait_send()
        for cp in _w_ici_landed(ins, layer, send_sems, recv_sems):
            cp.wait_recv()

    return _pc(body, name=name, out_shape=tuple(pltpu.HBM(b.shape, b.dtype) for b in bufs),
               in_specs=[_HBM] * n + [_SEM, _SEM, _ANY], out_specs=tuple([_HBM] * n),
               input_output_aliases={a: a for a in range(n)},
               compiler_params=pltpu.CompilerParams(has_side_effects=_EFFECT))(*bufs, send_sems, recv_sems, after)


def forward_halves(name, bufs, layer):
    n = len(bufs)

    def body(*refs):
        outs = refs[n:2 * n]
        send_sems, recv_sems = refs[2 * n:]
        fwd, from_sib = _w_forward(outs, layer, send_sems, recv_sems, 0)
        for cp in fwd:
            cp.start()
        for cp in from_sib:
            cp.wait_recv()
        for cp in fwd:
            cp.wait_send()

    return _pc(body, name=name, out_shape=[jax.ShapeDtypeStruct(b.shape, b.dtype) for b in bufs],
               in_specs=[_ANY] * n, out_specs=[_ANY] * n, input_output_aliases={a: a for a in range(n)},
               scratch_shapes=[pltpu.SemaphoreType.DMA((3 * n,)), pltpu.SemaphoreType.DMA((3 * n,))])(*bufs)


def exchange_rows(name, grads, layer):
    n = len(grads)

    def body(*refs):
        ins, outs = refs[:n], refs[n:2 * n]
        send_sems, recv_sems = refs[2 * n:]
        x, y, c, _ = _place()
        cps = []
        for a in range(n):
            rh = ins[a].shape[2] // 2
            cps.append(pltpu.make_async_remote_copy(
                src_ref=ins[a].at[layer, pl.ds(0, N_CHIP), pl.ds((1 - c) * rh, rh)], dst_ref=outs[a],
                send_sem=send_sems.at[a], recv_sem=recv_sems.at[a], device_id=(x, y, 1 - c), device_id_type=MESH))
        for cp in cps:
            cp.start()
        for cp in cps:
            cp.wait()

    return _pc(body, name=name,
               out_shape=[jax.ShapeDtypeStruct((N_CHIP, g.shape[2] // 2, g.shape[3]), g.dtype) for g in grads],
               in_specs=[_ANY] * n, out_specs=[_ANY] * n,
               scratch_shapes=[pltpu.SemaphoreType.DMA((n,)), pltpu.SemaphoreType.DMA((n,))])(*grads)


def _scatter_sends(parts, lands, send_sems, recv_sems):
    x, y, c, chips = _place()
    return [pltpu.make_async_remote_copy(
        src_ref=parts[a].at[2 * chip[0] + chip[1]], dst_ref=lands[a].at[2 * x + y], send_sem=send_sems.at[3 * a + t],
        recv_sem=recv_sems.at[3 * a + t], device_id=(*chip, c), device_id_type=MESH)
        for a in range(len(parts)) for t, chip in enumerate(chips)]


def _scatter_landed(lands, send_sems, recv_sems):
    x, y, c, chips = _place()
    return [pltpu.make_async_remote_copy(
        src_ref=lands[a].at[2 * chip[0] + chip[1]], dst_ref=lands[a].at[2 * chip[0] + chip[1]],
        send_sem=send_sems.at[3 * a + t], recv_sem=recv_sems.at[3 * a + t], device_id=(x, y, c), device_id_type=MESH)
        for a in range(len(lands)) for t, chip in enumerate(chips)]


def scatter_slices(name, parts, lands):
    n = len(parts)

    def body(*refs):
        ins, outs = refs[:n], refs[2 * n:3 * n]
        send_sems, recv_sems = refs[3 * n:]
        cps = _scatter_sends(ins, outs, send_sems, recv_sems)
        for cp in cps:
            cp.start()
        for cp in _scatter_landed(outs, send_sems, recv_sems):
            cp.wait_recv()
        for cp in cps:
            cp.wait_send()

    return _pc(body, name=name, out_shape=[jax.ShapeDtypeStruct(p.shape, p.dtype) for p in lands],
               in_specs=[_ANY] * (2 * n), out_specs=[_ANY] * n,
               input_output_aliases={n + a: a for a in range(n)},
               scratch_shapes=[pltpu.SemaphoreType.DMA((3 * n,)), pltpu.SemaphoreType.DMA((3 * n,))])(*parts, *lands)


def scatter_slices_start(name, parts, lands):
    n = len(parts)

    def body(*refs):
        send_sems, recv_sems = refs[2 * n:2 * n + 2]
        p_out, l_out = refs[2 * n + 2:3 * n + 2], refs[3 * n + 2:4 * n + 2]
        token = refs[4 * n + 2]
        for cp in _scatter_sends(p_out, l_out, send_sems, recv_sems):
            cp.start()
        token[...] = jnp.zeros_like(token)

    both = list(parts) + list(lands)
    res = _pc(body, name=name,
              out_shape=(pltpu.SemaphoreType.DMA((3 * n,)), pltpu.SemaphoreType.DMA((3 * n,)),
                         *[pltpu.HBM(b.shape, b.dtype) for b in both], jax.ShapeDtypeStruct((8, LANE), F32)),
              in_specs=[_HBM] * (2 * n),
              out_specs=(_SEM, _SEM, *([_HBM] * (2 * n)), pl.BlockSpec(memory_space=pltpu.VMEM)),
              input_output_aliases={a: a + 2 for a in range(2 * n)},
              compiler_params=pltpu.CompilerParams(has_side_effects=_EFFECT))(
                  *[pltpu.with_memory_space_constraint(b, pltpu.HBM) for b in both])
    return res[0], res[1], list(res[2:2 + n]), list(res[2 + n:2 + 2 * n]), res[2 + 2 * n]


def scatter_slices_wait(name, parts, lands, send_sems, recv_sems, after):
    n = len(parts)

    def body(*refs):
        p_in, l_in = refs[:n], refs[n:2 * n]
        send_sems, recv_sems = refs[2 * n:2 * n + 2]
        for cp in _scatter_sends(p_in, l_in, send_sems, recv_sems):
            cp.wait_send()
        for cp in _scatter_landed(l_in, send_sems, recv_sems):
            cp.wait_recv()

    both = list(parts) + list(lands)
    res = _pc(body, name=name, out_shape=tuple(pltpu.HBM(b.shape, b.dtype) for b in both),
              in_specs=[_HBM] * (2 * n) + [_SEM, _SEM, _ANY], out_specs=tuple([_HBM] * (2 * n)),
              input_output_aliases={a: a for a in range(2 * n)},
              compiler_params=pltpu.CompilerParams(has_side_effects=_EFFECT))(*both, send_sems, recv_sems, after)
    return list(res[n:])


def share_rows(name, bufs):
    n = len(bufs)

    def body(*refs):
        outs = refs[n:2 * n]
        send_sems, recv_sems = refs[2 * n:]
        x, y, c, _ = _place()

        def half(a, h):
            return outs[a].at[pl.ds(0, 2), h]

        cps = [pltpu.make_async_remote_copy(
            src_ref=half(a, c), dst_ref=half(a, c), send_sem=send_sems.at[a], recv_sem=recv_sems.at[a],
            device_id=(x, y, 1 - c), device_id_type=MESH) for a in range(n)]
        for cp in cps:
            cp.start()
        for a in range(n):
            pltpu.make_async_remote_copy(
                src_ref=half(a, 1 - c), dst_ref=half(a, 1 - c), send_sem=send_sems.at[a],
                recv_sem=recv_sems.at[a], device_id=(x, y, c), device_id_type=MESH).wait_recv()
        for cp in cps:
            cp.wait_send()

    return _pc(body, name=name, out_shape=[jax.ShapeDtypeStruct(b.shape, b.dtype) for b in bufs],
               in_specs=[_ANY] * n, out_specs=[_ANY] * n, input_output_aliases={a: a for a in range(n)},
               scratch_shapes=[pltpu.SemaphoreType.DMA((n,)), pltpu.SemaphoreType.DMA((n,))])(*bufs)


def _row_tile(R, C, nbytes=1 << 20):
    t = 8
    while t * 2 <= R and R % (t * 2) == 0 and t * 2 * C * 4 <= nbytes:
        t *= 2
    assert R % t == 0
    return t


def to_bf16_block(name, w, chip_arr, layer, after=None):
    _, R, C = w.shape
    tr = _row_tile(R, C)

    def body(j_ref, w_ref, *rest):
        o_ref = rest[-1]
        o_ref[...] = w_ref[...].astype(o_ref.dtype)

    in_specs, args = [pl.BlockSpec((None, tr, C), lambda i, j_ref: (layer, i, 0))], (chip_arr, w)
    if after is not None:
        in_specs, args = in_specs + [_ANY], args + (after,)
    gs = pltpu.PrefetchScalarGridSpec(
        num_scalar_prefetch=1, grid=(R // tr,), in_specs=in_specs,
        out_specs=pl.BlockSpec((None, None, tr, C), lambda i, j_ref: (j_ref[0], layer, i, 0)))
    return _pc(body, name=name, grid_spec=gs, out_shape=jax.ShapeDtypeStruct((N_CHIP,) + w.shape, BF16),
               compiler_params=_cp(("parallel",)))(*args)


def add_rows(name, g, ra, c_arr, layer):
    _, _, R, C = g.shape
    rh = R // 2
    tr = _row_tile(rh, C)
    nb = rh // tr

    def body(c_ref, g_ref, r_ref, o_ref):
        o_ref[...] = (g_ref[...].astype(F32) + r_ref[...].astype(F32)).astype(o_ref.dtype)

    gs = pltpu.PrefetchScalarGridSpec(
        num_scalar_prefetch=1, grid=(N_CHIP, nb),
        in_specs=[pl.BlockSpec((None, None, tr, C), lambda j, i, c_ref: (layer, j, c_ref[0] * nb + i, 0)),
                  pl.BlockSpec((None, tr, C), lambda j, i, c_ref: (j, i, 0))],
        out_specs=pl.BlockSpec((None, tr, C), lambda j, i, c_ref: (j, i, 0)))
    return _pc(body, name=name, grid_spec=gs, out_shape=jax.ShapeDtypeStruct(ra.shape, BF16),
               compiler_params=_cp(("parallel", "parallel")))(c_arr, g, ra)


def sum_rows_into(name, landed, c_arr, layer, into):
    n, rh, C = landed.shape
    tr = _row_tile(rh, C, nbytes=1 << 19)

    def body(*refs):
        g_ref, o_ref = refs[1], refs[-1]
        acc = g_ref[0].astype(F32)
        for j in range(1, n):
            acc = acc + g_ref[j].astype(F32)
        o_ref[...] = acc

    in_specs, args, alias = [pl.BlockSpec((n, tr, C), lambda i, c_ref: (0, i, 0))], (c_arr, landed), {}
    if into is not None:
        in_specs, args, alias = in_specs + [_ANY], args + (into,), {2: 0}
    gs = pltpu.PrefetchScalarGridSpec(
        num_scalar_prefetch=1, grid=(rh // tr,), in_specs=in_specs,
        out_specs=pl.BlockSpec((None, None, tr, C), lambda i, c_ref: (layer, c_ref[0], i, 0)))
    return _pc(body, name=name, grid_spec=gs, out_shape=jax.ShapeDtypeStruct((2, 2, rh, C), F32),
               input_output_aliases=alias, compiler_params=_cp(("parallel",)))(*args)


def own_row(name, part, chip_arr):
    _, R, C = part.shape
    tr = _row_tile(R, C)

    def body(j_ref, p_ref, o_ref):
        o_ref[...] = p_ref[...]

    gs = pltpu.PrefetchScalarGridSpec(
        num_scalar_prefetch=1, grid=(R // tr,),
        in_specs=[pl.BlockSpec((None, tr, C), lambda i, j_ref: (j_ref[0], i, 0))],
        out_specs=pl.BlockSpec((None, tr, C), lambda i, j_ref: (j_ref[0], i, 0)))
    return _pc(body, name=name, grid_spec=gs, out_shape=jax.ShapeDtypeStruct(part.shape, part.dtype),
               compiler_params=_cp(("parallel",)))(chip_arr, part)


def sum_leading(name, g, plane=None):
    n, R, C = g.shape
    tr = _row_tile(R, C, nbytes=(1 << 21) // n)

    def body(*refs):
        g_ref, o_ref = refs[-2:]
        acc = g_ref[0].astype(F32)
        for j in range(1, n):
            acc = acc + g_ref[j].astype(F32)
        o_ref[...] = acc

    if plane is None:
        return _pc(body, name=name, grid=(R // tr,), in_specs=[pl.BlockSpec((n, tr, C), lambda i: (0, i, 0))],
                   out_specs=pl.BlockSpec((tr, C), lambda i: (i, 0)), out_shape=jax.ShapeDtypeStruct((R, C), F32),
                   compiler_params=_cp(("parallel",)))(g)
    count, idx = plane
    gs = pltpu.PrefetchScalarGridSpec(
        num_scalar_prefetch=1, grid=(R // tr,),
        in_specs=[pl.BlockSpec((n, tr, C), lambda i, p_ref: (0, i, 0))],
        out_specs=pl.BlockSpec((None, tr, C), lambda i, p_ref: (p_ref[0], i, 0)))
    return _pc(body, name=name, grid_spec=gs, out_shape=jax.ShapeDtypeStruct((count, R, C), F32),
               compiler_params=_cp(("parallel",)))(idx, g)


def adamw(name, w, g, m, v, emit_g=False):
    R, C = w.shape
    tr = _row_tile(R, C)

    def body(w_ref, g_ref, m_ref, v_ref, d_ref, mo_ref, vo_ref, *go_ref):
        gv = g_ref[...]
        if emit_g:
            go_ref[0][...] = gv
        mn = ADAM_B1 * m_ref[...] + (1.0 - ADAM_B1) * gv
        vn = ADAM_B2 * v_ref[...] + (1.0 - ADAM_B2) * (gv * gv)
        m_hat = mn / (1.0 - ADAM_B1 ** ADAM_STEP)
        v_hat = vn / (1.0 - ADAM_B2 ** ADAM_STEP)
        d_ref[...] = -ADAM_LR * (m_hat / (jnp.sqrt(v_hat) + ADAM_EPS) + ADAM_WD * w_ref[...])
        mo_ref[...] = mn
        vo_ref[...] = vn

    spec = pl.BlockSpec((tr, C), lambda i: (i, 0))
    shp = jax.ShapeDtypeStruct((R, C), F32)
    nout = 4 if emit_g else 3
    return _pc(body, name=name, grid=(R // tr,), in_specs=[spec] * 4, out_specs=[spec] * nout,
               out_shape=[shp] * nout, compiler_params=_cp(("parallel",)))(w, g, m, v)


_ADA_TN = 512


def adaln_fwd(name, cond, w, b):
    _, D, N = w.shape
    tn = min(_ADA_TN, N)

    def body(c_ref, w_ref, b_ref, o_ref):
        s = _silu(c_ref[...]).astype(BF16)
        o_ref[...] = dot_nn(s, w_ref[...].astype(BF16)) + b_ref[...]

    return _pc(body, name=name, grid=(2, N // tn),
               in_specs=[pl.BlockSpec((16, D), lambda l, n: (0, 0)),
                         pl.BlockSpec((None, D, tn), lambda l, n: (l, 0, n)),
                         pl.BlockSpec((None, 1, tn), lambda l, n: (l, 0, n))],
               out_specs=pl.BlockSpec((None, 16, tn), lambda l, n: (l, 0, n)),
               out_shape=jax.ShapeDtypeStruct((2, 16, N), F32),
               compiler_params=_cp(("parallel", "parallel")))(cond, w, b)


def adaln_bwd(name, cond, w, dm):
    _, D, N = w.shape
    tn = min(_ADA_TN, N)

    def body(c_ref, w_ref, dm_ref, gw_ref, ds_ref):
        first = jnp.logical_and(pl.program_id(0) == 0, pl.program_id(1) == 0)
        s = _silu(c_ref[...]).astype(BF16)
        dmb = dm_ref[...].astype(BF16)
        gw_ref[...] = dot_tn(s, dmb)
        p = dot_nt(dmb, w_ref[...].astype(BF16))

        @pl.when(first)
        def _():
            ds_ref[...] = p

        @pl.when(jnp.logical_not(first))
        def _():
            ds_ref[...] += p

    return _pc(body, name=name, grid=(2, N // tn),
               in_specs=[pl.BlockSpec((16, D), lambda l, n: (0, 0)),
                         pl.BlockSpec((None, D, tn), lambda l, n: (l, 0, n)),
                         pl.BlockSpec((None, 16, tn), lambda l, n: (l, 0, n))],
               out_specs=[pl.BlockSpec((None, D, tn), lambda l, n: (l, 0, n)),
                          pl.BlockSpec((16, D), lambda l, n: (0, 0))],
               out_shape=[jax.ShapeDtypeStruct((2, D, N), F32), jax.ShapeDtypeStruct((16, D), F32)],
               compiler_params=_cp(("arbitrary", "arbitrary")))(cond, w, dm)


def cctx_grad(name, parts, c_ctx):
    def body(p_ref, c_ref, o_ref):
        acc = p_ref[0]
        for j in range(1, N_CHIP):
            acc = acc + p_ref[j]
        o_ref[...] = acc * _dsilu(c_ref[...])

    return _pc(body, name=name, out_shape=jax.ShapeDtypeStruct(c_ctx.shape, F32))(parts, c_ctx)


def _pack(arrs):
    rows = []
    for a in arrs:
        f = a.reshape(-1)
        pad = (-f.shape[0]) % LANE
        rows.append(jnp.pad(f, (0, pad)).reshape(-1, LANE))
    out = jnp.concatenate(rows, axis=0)
    pad = (-out.shape[0]) % 8
    return jnp.pad(out, ((0, pad), (0, 0))) if pad else out


def _unpack(rows, shapes):
    out, r = [], 0
    for s in shapes:
        n = int(np.prod(s))
        nr = -(-n // LANE)
        out.append(rows[r:r + nr].reshape(-1)[:n].reshape(s))
        r += nr
    return out


MOD_NAMES = ("sh1", "sc1", "g1", "sh2", "sc2", "g2")


def kernel(x, c, ctx, c_ctx, w_ada, b_ada, norm1_g, w_in, ret_decay, ret_gn_g, conv_dw_w, conv_dw_b, conv_ln_g, conv_ln_b, conv_pw, na_rpb, w_out, norm2_g, ffn_up, ffn_dw_w, ffn_dw_b, ffn_down, final_g, loss_target, m_c_ctx, m_w_ada, m_b_ada, m_norm1_g, m_w_in, m_ret_decay, m_ret_gn_g, m_conv_dw_w, m_conv_dw_b, m_conv_ln_g, m_conv_ln_b, m_conv_pw, m_na_rpb, m_w_out, m_norm2_g, m_ffn_up, m_ffn_dw_w, m_ffn_dw_b, m_ffn_down, m_final_g, v_c_ctx, v_w_ada, v_b_ada, v_norm1_g, v_w_in, v_ret_decay, v_ret_gn_g, v_conv_dw_w, v_conv_dw_b, v_conv_ln_g, v_conv_ln_b, v_conv_pw, v_na_rpb, v_w_out, v_norm2_g, v_ffn_up, v_ffn_dw_w, v_ffn_dw_b, v_ffn_down, v_final_g):
    cfg = make_cfg(D=x.shape[2], T=x.shape[1], TC=ctx.shape[1], RH=ret_decay.shape[2], CW=conv_dw_b.shape[1],
                   NH=na_rpb.shape[1], DFF=ffn_dw_b.shape[1] // 2)
    D, T = cfg.D, cfg.T
    W = dict(c_ctx=c_ctx, w_ada=w_ada, b_ada=b_ada, norm1_g=norm1_g, w_in=w_in, ret_decay=ret_decay, ret_gn_g=ret_gn_g,
             conv_dw_w=conv_dw_w, conv_dw_b=conv_dw_b, conv_ln_g=conv_ln_g, conv_ln_b=conv_ln_b, conv_pw=conv_pw,
             na_rpb=na_rpb, w_out=w_out, norm2_g=norm2_g, ffn_up=ffn_up, ffn_dw_w=ffn_dw_w, ffn_dw_b=ffn_dw_b,
             ffn_down=ffn_down, final_g=final_g)
    Mo = dict(c_ctx=m_c_ctx, w_ada=m_w_ada, b_ada=m_b_ada, norm1_g=m_norm1_g, w_in=m_w_in, ret_decay=m_ret_decay,
              ret_gn_g=m_ret_gn_g, conv_dw_w=m_conv_dw_w, conv_dw_b=m_conv_dw_b, conv_ln_g=m_conv_ln_g,
              conv_ln_b=m_conv_ln_b, conv_pw=m_conv_pw, na_rpb=m_na_rpb, w_out=m_w_out, norm2_g=m_norm2_g,
              ffn_up=m_ffn_up, ffn_dw_w=m_ffn_dw_w, ffn_dw_b=m_ffn_dw_b, ffn_down=m_ffn_down, final_g=m_final_g)
    Vo = dict(c_ctx=v_c_ctx, w_ada=v_w_ada, b_ada=v_b_ada, norm1_g=v_norm1_g, w_in=v_w_in, ret_decay=v_ret_decay,
              ret_gn_g=v_ret_gn_g, conv_dw_w=v_conv_dw_w, conv_dw_b=v_conv_dw_b, conv_ln_g=v_conv_ln_g,
              conv_ln_b=v_conv_ln_b, conv_pw=v_conv_pw, na_rpb=v_na_rpb, w_out=v_w_out, norm2_g=v_norm2_g,
              ffn_up=v_ffn_up, ffn_dw_w=v_ffn_dw_w, ffn_dw_b=v_ffn_dw_b, ffn_down=v_ffn_down, final_g=v_final_g)
    order = list(W)
    xi, yi, ci = lax.axis_index("x"), lax.axis_index("y"), lax.axis_index("c")
    chip = 2 * xi + yi
    dev = 4 * xi + 2 * yi + ci
    NA = w_ada.shape[2]
    ncw, nfw = conv_dw_w.shape[2], ffn_dw_w.shape[2]

    c_arr = jnp.reshape(ci, (1,)).astype(jnp.int32)
    chip_arr = jnp.reshape(chip, (1,)).astype(jnp.int32)

    g_in = allgather8("ag_small_in", _pack([c[0], conv_dw_w, ffn_dw_w])).reshape(N_DEV, -1, LANE)
    c8 = g_in[:, :D // LANE].reshape(N_DEV, D)
    cw_parts, fw_parts = [], []
    for j in range(N_CHIP):
        _, a, b = _unpack(g_in[2 * j], [(D,), conv_dw_w.shape, ffn_dw_w.shape])
        cw_parts.append(a)
        fw_parts.append(b)
    conv_dw_w_full = jnp.concatenate(cw_parts, axis=2)
    ffn_dw_w_full = jnp.concatenate(fw_parts, axis=2)
    cond = jnp.concatenate([c8, c_ctx[None], jnp.zeros((16 - N_DEV - 1, D), F32)], axis=0)

    b_sh = lax.dynamic_slice(b_ada, (0, chip * NA), (2, NA)).reshape(2, 1, NA)
    m_sh = adaln_fwd("adaln_fwd", cond, w_ada, b_sh)
    m_dev = allgather8("ag_mod", m_sh.reshape(2 * 16, NA)).reshape(N_DEV, 2, 16, NA)
    m_all = jnp.concatenate([m_dev[2 * j] for j in range(N_CHIP)], axis=-1)
    mods = []
    for l in range(2):
        lat = lax.dynamic_slice(m_all[l], (dev, 0), (1, N_CHIP * NA))[0]
        cx = m_all[l, N_DEV]
        mods.append({nm: jnp.stack([lat[k * D:(k + 1) * D], cx[k * D:(k + 1) * D]], 0)[:, None, :]
                     for k, nm in enumerate(MOD_NAMES)})

    first, rest = ("w_in", "conv_pw"), ("w_out", "ffn_up", "ffn_down")
    wb = [{}, {}]
    have, flying_w = {}, {}

    def start_gather(tag, l, names, after):
        s_sem, r_sem, bufs, tok = allgather_layer_start(f"ag_{tag}_start", [wb[l][nm] for nm in names], l, after)
        flying_w[(l, names[0])] = (tag, l, names, bufs, s_sem, r_sem)
        return tok

    def land_gather(key, after):
        tag, l, names, bufs, s_sem, r_sem = flying_w.pop(key)
        landed = allgather_layer_wait(f"ag_{tag}_wait", bufs, s_sem, r_sem, after, l)
        have.update(zip([(l, nm) for nm in names], forward_halves(f"ag_{tag}_fwd", list(landed), l)))

    for nm in first:
        wb[0][nm] = to_bf16_block(f"to_bf16_{nm}_0", W[nm], chip_arr, 0)
    tok_first = start_gather("w0a", 0, first, m_all)
    for l in range(2):
        for nm in BIG:
            if nm not in wb[l]:
                wb[l][nm] = to_bf16_block(f"to_bf16_{nm}_{l}", W[nm], chip_arr, l, after=tok_first)
    land_gather((0, first[0]), wb[1][BIG[-1]])
    mods[0] = {**mods[0], "sc1": mods[0]["sc1"] + start_gather("w0b", 0, rest, have[(0, first[0])])[0, 0]}

    def wts(l, name, after):
        tok = None
        if (l, name) not in have:
            if l == 0:
                land_gather((0, rest[0]), after)
                tok = start_gather("w1", 1, BIG, have[(0, rest[0])])[0, 0]
            else:
                land_gather((1, BIG[0]), after)
        return have[(l, name)], tok

    sp = dict(norm1_g=norm1_g, norm2_g=norm2_g, ret_decay=ret_decay, ret_gn_g=ret_gn_g, conv_dw_w=conv_dw_w_full,
              conv_dw_b=conv_dw_b, conv_ln_g=conv_ln_g, conv_ln_b=conv_ln_b, na_rpb=na_rpb, ffn_dw_w=ffn_dw_w_full,
              ffn_dw_b=ffn_dw_b, final_g=final_g)
    flights = []

    def grads_ready(l, names, gb):
        tag = f"{l}_{names[0]}"
        from_sib = exchange_rows(f"rs_exchange_{tag}", [gb[nm] for nm in names], l)
        part = [add_rows(f"rs_add_{nm}_{l}", gb[nm], r, c_arr, l) for nm, r in zip(names, from_sib)]
        lands = [own_row(f"rs_own_{nm}_{l}", p, chip_arr) for nm, p in zip(names, part)]
        s_sem, r_sem, part, lands, tok = scatter_slices_start(f"rs_scatter_{tag}_start", part, lands)
        flights.append((l, names, tag, part, lands, (s_sem, r_sem)))
        return tok[0, 0]

    loss_l, gx, gb, gss, dms, dfg, tok_last = local_step(
        cfg, x[0], ctx[0], loss_target[0], mods, wts, sp, grads_ready)
    loss = lax.psum(loss_l, ("x", "y", "c"))

    delta, new_m, new_v = {}, {}, {}
    bigs = ("w_ada",) + BIG

    def adamw_big(nm):
        shp = W[nm].shape
        v2 = lambda a: a.reshape(-1, shp[-1])
        d_, m_, v_, *g_ = adamw(f"adamw_{nm}", v2(W[nm]), v2(gfull[nm]), v2(Mo[nm]), v2(Vo[nm]), emit_g=nm in BIG)
        delta[nm], new_m[nm], new_v[nm] = d_.reshape(shp), m_.reshape(shp), v_.reshape(shp)
        if g_:
            gfull[nm] = g_[0].reshape(shp)

    gfull, fin, after = {}, {}, gx
    for names in GRAD_GROUPS:
        for l, _, tag, part, lands, sems in sorted([f for f in flights if f[1] == names], key=lambda f: -f[0]):
            landed = scatter_slices_wait(f"rs_scatter_{tag}_wait", part, lands, *sems, after)
            for nm, p in zip(names, landed):
                fin[nm] = sum_rows_into(f"rs_sum_{nm}_{l}", p, c_arr, l, fin.get(nm))
        for nm, gfin in zip(names, share_rows(f"rs_share_{names[0]}", [fin[nm] for nm in names])):
            gfull[nm] = gfin.reshape(W[nm].shape)
            adamw_big(nm)
        after = delta[names[-1]]

    dmseg = jnp.stack([jnp.stack([jnp.concatenate([dms[l][nm][r, 0] for nm in MOD_NAMES]) for r in range(2)])
                       for l in range(2)])
    dmseg, _ = lax.optimization_barrier((dmseg, fin[GRAD_GROUPS[-1][-1]]))
    gsm = dict(
        norm1_g=jnp.stack([gss[l]["norm1_g"][0] for l in range(2)]),
        ret_decay=jnp.stack([gss[l]["lam"] * jax.nn.sigmoid(-ret_decay[l]) for l in range(2)]),
        ret_gn_g=jnp.stack([gss[l]["ret_gn_g"][0] for l in range(2)]),
        conv_dw_w=jnp.stack([gss[l]["conv_dw_w"][:cfg.CK] for l in range(2)]),
        conv_dw_b=jnp.stack([gss[l]["conv_dw_b"][0] for l in range(2)]),
        conv_ln_g=jnp.stack([gss[l]["conv_ln_g"][0] for l in range(2)]),
        conv_ln_b=jnp.stack([gss[l]["conv_ln_b"][0] for l in range(2)]),
        na_rpb=jnp.stack([gss[l]["na_rpb"] for l in range(2)]),
        norm2_g=jnp.stack([gss[l]["norm2_g"][0] for l in range(2)]),
        ffn_dw_w=jnp.stack([gss[l]["ffn_dw_w"][:, :3].transpose(1, 0, 2).reshape(3, 2 * cfg.DFF) for l in range(2)]),
        ffn_dw_b=jnp.stack([gss[l]["ffn_dw_b"].reshape(-1) for l in range(2)]),
        final_g=dfg)
    snames = list(gsm)
    sshapes = [dmseg.shape] + [gsm[nm].shape for nm in snames]
    packed = _pack([dmseg] + [gsm[nm] for nm in snames])
    g_all = allgather8("ag_small_grads", packed).reshape(N_DEV, packed.shape[0], LANE)
    summed = sum_leading("sum_small_grads", g_all)
    dm_sum, *gsum = _unpack(summed, sshapes)
    gfull.update(zip(snames, gsum))
    ndm = int(np.prod(dmseg.shape))
    dm_all = g_all[:, :ndm // LANE].reshape(N_DEV, 2, 2, 6 * D)
    gfull["b_ada"] = sum_leading("sum_b_ada", dm_all.transpose(0, 2, 1, 3).reshape(2 * N_DEV, 2 * 6 * D // LANE, LANE)
                                 ).reshape(2, 6 * D)

    dm16 = jnp.concatenate([dm_all[:, :, 0].transpose(1, 0, 2), dm_sum[:, 1][:, None],
                            jnp.zeros((2, 16 - N_DEV - 1, 6 * D), F32)], axis=1)
    dm16 = lax.dynamic_slice(dm16, (0, 0, chip * NA), (2, 16, NA))
    gfull["w_ada"], ds16 = adaln_bwd("adaln_bwd", cond, w_ada, dm16)
    ds_all = allgather8("ag_dsilu", ds16[8:16]).reshape(N_DEV, 8, D)[0::2, 0:1]
    gfull["c_ctx"] = cctx_grad("cctx_grad", ds_all, c_ctx[None])[0]
    gfull["conv_dw_w"] = lax.dynamic_slice(gfull["conv_dw_w"], (0, 0, chip * ncw), (2, cfg.CK, ncw))
    gfull["ffn_dw_w"] = lax.dynamic_slice(gfull["ffn_dw_w"], (0, 0, chip * nfw), (2, 3, nfw))

    adamw_big("w_ada")
    smalls = [nm for nm in order if nm not in bigs]
    shapes = [W[nm].shape for nm in smalls]
    d_, m_, v_ = adamw("adamw_small", _pack([W[nm] for nm in smalls]), _pack([gfull[nm] for nm in smalls]),
                       _pack([Mo[nm] for nm in smalls]), _pack([Vo[nm] for nm in smalls]))
    for nm, a, b, e in zip(smalls, _unpack(d_, shapes), _unpack(m_, shapes), _unpack(v_, shapes)):
        delta[nm], new_m[nm], new_v[nm] = a, b, e
    return (loss, gx[None], *[gfull[nm] for nm in order], *[delta[nm] for nm in order],
            *[new_m[nm] for nm in order], *[new_v[nm] for nm in order])
```

```python
import collections
import functools

import numpy as np
import jax
import jax.numpy as jnp
from jax import lax
from jax.experimental import pallas as pl
from jax.experimental.pallas import tpu as pltpu

F32 = jnp.float32
BF16 = jnp.bfloat16
EPS = 1e-6
ROPE_BASE = 10000.0
NEG = -1e30
LANE = 128
VMEM_LIMIT = 56 * 1024 * 1024

ADAM_LR, ADAM_B1, ADAM_B2, ADAM_EPS, ADAM_WD, ADAM_STEP = 0.001, 0.9, 0.999, 1e-08, 0.01, 10

Cfg = collections.namedtuple(
    "Cfg", "D T TC GW RH RDK RDV CW CK NH NDH NAR NAC DFF TB")


def make_cfg(D=2048, T=4096, TC=256, RH=4, CW=512, NH=4, DFF=5632):
    return Cfg(D=D, T=T, TC=TC, GW=64, RH=RH, RDK=128, RDV=256, CW=CW, CK=31, NH=NH, NDH=128,
               NAR=8, NAC=16, DFF=DFF, TB=256)


def _offsets(cfg):
    sizes = [cfg.RH * cfg.RDK, cfg.RH * cfg.RDK, cfg.RH * cfg.RDV, cfg.RH * cfg.RDV, cfg.CW, cfg.CW,
             cfg.NH * cfg.NDH, cfg.NH * cfg.NDH, cfg.NH * cfg.NDH]
    offs = [0]
    for s in sizes:
        offs.append(offs[-1] + s)
    return dict(zip(["lq", "lk", "lv", "lg", "la", "lb", "nq", "nk", "nv", "end"], offs))


def _pc(body, **kw):
    return pl.pallas_call(body, **kw)


def _cp(sem=None):
    return pltpu.CompilerParams(dimension_semantics=sem, vmem_limit_bytes=VMEM_LIMIT)


def _dot(a, b, ca, cb):
    return lax.dot_general(a, b, (((ca,), (cb,)), ((), ())), preferred_element_type=F32)


def dot_nn(a, b):
    return _dot(a, b, 1, 0)


def dot_nt(a, b):
    return _dot(a, b, 1, 1)


def dot_tn(a, b):
    return _dot(a, b, 0, 0)


def _sigmoid(x):
    return 1.0 / (1.0 + jnp.exp(-x))


def _silu(x):
    return x * _sigmoid(x)


def _dsilu(x):
    s = _sigmoid(x)
    return s * (1.0 + x * (1.0 - s))


def matmul(name, a, b, *, contract, grid, a_spec, b_spec, out_shape, out_spec, nk, into=None):
    dot = {"nn": dot_nn, "nt": dot_nt, "tn": dot_tn}[contract]
    direct = nk > 1 and out_shape.dtype == F32
    kax = len(grid) - 1

    def body(a_ref, b_ref, *rest):
        o_ref, *scr = rest[1:] if into is not None else rest
        p = dot(a_ref[...].astype(BF16), b_ref[...].astype(BF16))
        if nk == 1:
            o_ref[...] = p.astype(o_ref.dtype)
            return
        acc = o_ref if direct else scr[0]
        k = pl.program_id(kax)

        @pl.when(k == 0)
        def _():
            acc[...] = p

        @pl.when(k > 0)
        def _():
            acc[...] += p

        if not direct:
            @pl.when(k == nk - 1)
            def _():
                o_ref[...] = acc[...].astype(o_ref.dtype)

    scratch = []
    if nk > 1 and not direct:
        blk = [s for s in out_spec.block_shape if s is not None]
        scratch = [pltpu.VMEM(tuple(blk), F32)]
    sem = ("parallel",) * kax + (("arbitrary",) if nk > 1 else ("parallel",))
    in_specs, args, alias = [a_spec, b_spec], (a, b), {}
    if into is not None:
        in_specs, args, alias = in_specs + [pl.BlockSpec(memory_space=pl.ANY)], (a, b, into), {2: 0}
    return _pc(body, name=name, grid=grid, in_specs=in_specs, out_specs=out_spec, out_shape=out_shape,
               scratch_shapes=scratch, input_output_aliases=alias, compiler_params=_cp(sem))(*args)


_WG_ROWS = 1024


def wgrad(cfg, name, a, dc, a_spec, dc_spec, out_shape, out_spec, ntiles, into):
    T, TC = cfg.T, cfg.TC
    tml = min(_WG_ROWS, T)
    nl = T // tml

    def body(al_ref, ac_ref, dl_ref, dcx_ref, *rest):
        o_ref, acc = rest[-2:]
        m = pl.program_id(1)

        @pl.when(m == 0)
        def _():
            acc[...] = dot_tn(al_ref[...], dl_ref[...])

        @pl.when(jnp.logical_and(m > 0, m < nl))
        def _():
            acc[...] += dot_tn(al_ref[...], dl_ref[...])

        @pl.when(m == nl)
        def _():
            o_ref[...] = (acc[...] + dot_tn(ac_ref[...], dcx_ref[...])).astype(o_ref.dtype)

    lat = lambda m: jnp.minimum(m, nl - 1)
    ctx = lambda m: T // TC
    in_specs = [a_spec(tml, lat), a_spec(TC, ctx), dc_spec(tml, lat), dc_spec(TC, ctx)]
    args, alias = (a, a, dc, dc), {}
    if into is not None:
        in_specs, args, alias = in_specs + [pl.BlockSpec(memory_space=pl.ANY)], args + (into,), {4: 0}
    blk = tuple(s for s in out_spec.block_shape if s is not None)
    return _pc(body, name=name, grid=(ntiles, nl + 1), in_specs=in_specs, out_specs=out_spec, out_shape=out_shape,
               scratch_shapes=[pltpu.VMEM(blk, F32)], input_output_aliases=alias,
               compiler_params=_cp(("parallel", "arbitrary")))(*args)


def mm_rowsharded(name, a, w4, l, out_dtype, tn):
    L, K = a.shape
    nch, _, Kb, N = w4.shape
    tm = _tm(L, 8)

    def body(a_ref, w_ref, o_ref):
        acc = dot_nn(a_ref[:, 0:Kb], w_ref[0])
        for j in range(1, nch):
            acc += dot_nn(a_ref[:, j * Kb:(j + 1) * Kb], w_ref[j])
        o_ref[...] = acc.astype(o_ref.dtype)

    return _pc(body, name=name, grid=(N // tn, L // tm),
               in_specs=[pl.BlockSpec((tm, K), lambda n, m: (m, 0)),
                         pl.BlockSpec((nch, None, Kb, tn), lambda n, m: (0, l, 0, n))],
               out_specs=pl.BlockSpec((tm, tn), lambda n, m: (m, n)),
               out_shape=jax.ShapeDtypeStruct((L, N), out_dtype),
               compiler_params=_cp(("parallel", "parallel")))(a, w4)


def _region(cfg):
    nlat = cfg.T // cfg.TB
    return lambda i: jnp.minimum(i // nlat, 1)


def norm_mod_fwd(cfg, name, x, ng, sc, sh):
    L, D = x.shape
    TB = cfg.TB
    reg = _region(cfg)

    def body(x_ref, ng_ref, sc_ref, sh_ref, h_ref):
        xv = x_ref[...]
        r = lax.rsqrt(jnp.mean(xv * xv, axis=-1, keepdims=True) + EPS)
        n = xv * r * ng_ref[...]
        h_ref[...] = (n * (1.0 + sc_ref[...]) + sh_ref[...]).astype(h_ref.dtype)

    row = pl.BlockSpec((TB, D), lambda i: (i, 0))
    vec = pl.BlockSpec((1, D), lambda i: (0, 0))
    rvec = pl.BlockSpec((None, 1, D), lambda i: (reg(i), 0, 0))
    return _pc(body, name=name, grid=(L // TB,), in_specs=[row, vec, rvec, rvec], out_specs=row,
               out_shape=jax.ShapeDtypeStruct((L, D), BF16), compiler_params=_cp(("parallel",)))(x, ng, sc, sh)


def norm_mod_bwd(cfg, name, dh, x, ng, sc, dx_in, latent_only=False):
    L, D = x.shape
    TB = cfg.TB
    nlat = cfg.T // TB
    reg = _region(cfg)

    def body(dh_ref, x_ref, ng_ref, sc_ref, dxi_ref, dx_ref, dsc_ref, dsh_ref, dng_ref):
        i = pl.program_id(0)
        xv = x_ref[...]
        r = lax.rsqrt(jnp.mean(xv * xv, axis=-1, keepdims=True) + EPS)
        xh = xv * r
        g = ng_ref[...]
        n = xh * g
        dh = dh_ref[...]
        dn = dh * (1.0 + sc_ref[...])
        dxh = dn * g
        dx = r * (dxh - xh * jnp.mean(dxh * xh, axis=-1, keepdims=True))
        if latent_only:
            @pl.when(i < nlat)
            def _():
                dx_ref[...] = dxi_ref[...] + dx
        else:
            dx_ref[...] = dxi_ref[...] + dx
        s_sh = jnp.sum(dh, axis=0, keepdims=True)
        s_sc = jnp.sum(dh * n, axis=0, keepdims=True)
        s_ng = jnp.sum(dn * xh, axis=0, keepdims=True)
        first = jnp.logical_or(i == 0, i == nlat)

        @pl.when(first)
        def _():
            dsh_ref[...] = s_sh
            dsc_ref[...] = s_sc

        @pl.when(jnp.logical_not(first))
        def _():
            dsh_ref[...] += s_sh
            dsc_ref[...] += s_sc

        @pl.when(i == 0)
        def _():
            dng_ref[...] = s_ng

        @pl.when(i > 0)
        def _():
            dng_ref[...] += s_ng

    row = pl.BlockSpec((TB, D), lambda i: (i, 0))
    vec = pl.BlockSpec((1, D), lambda i: (0, 0))
    rvec = pl.BlockSpec((None, 1, D), lambda i: (reg(i), 0, 0))
    dxs = pl.BlockSpec((TB, D), lambda i: (jnp.minimum(i, nlat - 1), 0)) if latent_only else row
    return _pc(body, name=name, grid=(L // TB,), in_specs=[row, row, vec, rvec, row],
               out_specs=[dxs, rvec, rvec, vec],
               out_shape=[jax.ShapeDtypeStruct((cfg.T if latent_only else L, D), F32),
                          jax.ShapeDtypeStruct((2, 1, D), F32),
                          jax.ShapeDtypeStruct((2, 1, D), F32), jax.ShapeDtypeStruct((1, D), F32)],
               compiler_params=_cp(("arbitrary",)))(dh, x, ng, sc, dx_in)


def resid_fwd(cfg, name, x, y, g):
    L, D = x.shape
    TB = cfg.TB
    reg = _region(cfg)

    def body(x_ref, y_ref, g_ref, o_ref):
        o_ref[...] = x_ref[...] + g_ref[...] * y_ref[...]

    row = pl.BlockSpec((TB, D), lambda i: (i, 0))
    rvec = pl.BlockSpec((None, 1, D), lambda i: (reg(i), 0, 0))
    return _pc(body, name=name, grid=(L // TB,), in_specs=[row, row, rvec], out_specs=row,
               out_shape=jax.ShapeDtypeStruct((L, D), F32), compiler_params=_cp(("parallel",)))(x, y, g)


def resid_norm_fwd(cfg, name, x, y, g, ng, sc, sh):
    L, D = x.shape
    TB = cfg.TB
    reg = _region(cfg)

    def body(x_ref, y_ref, g_ref, ng_ref, sc_ref, sh_ref, xo_ref, h_ref):
        xv = x_ref[...] + g_ref[...] * y_ref[...]
        xo_ref[...] = xv
        r = lax.rsqrt(jnp.mean(xv * xv, axis=-1, keepdims=True) + EPS)
        n = xv * r * ng_ref[...]
        h_ref[...] = (n * (1.0 + sc_ref[...]) + sh_ref[...]).astype(h_ref.dtype)

    row = pl.BlockSpec((TB, D), lambda i: (i, 0))
    vec = pl.BlockSpec((1, D), lambda i: (0, 0))
    rvec = pl.BlockSpec((None, 1, D), lambda i: (reg(i), 0, 0))
    return _pc(body, name=name, grid=(L // TB,), in_specs=[row, row, rvec, vec, rvec, rvec], out_specs=[row, row],
               out_shape=[jax.ShapeDtypeStruct((L, D), F32), jax.ShapeDtypeStruct((L, D), BF16)],
               compiler_params=_cp(("parallel",)))(x, y, g, ng, sc, sh)


def resid_bwd(cfg, name, dxo, y, g):
    L, D = y.shape
    TB = cfg.TB
    nlat = cfg.T // TB
    reg = _region(cfg)

    def body(d_ref, y_ref, g_ref, dy_ref, dg_ref):
        i = pl.program_id(0)
        d = d_ref[...]
        dy_ref[...] = (d * g_ref[...]).astype(dy_ref.dtype)
        s = jnp.sum(d * y_ref[...], axis=0, keepdims=True)
        first = jnp.logical_or(i == 0, i == nlat)

        @pl.when(first)
        def _():
            dg_ref[...] = s

        @pl.when(jnp.logical_not(first))
        def _():
            dg_ref[...] += s

    row = pl.BlockSpec((TB, D), lambda i: (i, 0))
    rvec = pl.BlockSpec((None, 1, D), lambda i: (reg(i), 0, 0))
    return _pc(body, name=name, grid=(L // TB,), in_specs=[row, row, rvec], out_specs=[row, rvec],
               out_shape=[jax.ShapeDtypeStruct((L, D), BF16), jax.ShapeDtypeStruct((2, 1, D), F32)],
               compiler_params=_cp(("arbitrary",)))(dxo, y, g)


def final_loss(cfg, name, x, fg, tgt):
    L, D = x.shape
    TB = cfg.TB
    nlat = cfg.T // TB

    def body(x_ref, fg_ref, t_ref, ls_ref, dx_ref, dg_ref):
        i = pl.program_id(0)

        @pl.when(i == 0)
        def _():
            ls_ref[...] = jnp.zeros_like(ls_ref)
            dg_ref[...] = jnp.zeros_like(dg_ref)

        @pl.when(i < nlat)
        def _():
            xv = x_ref[...]
            r = lax.rsqrt(jnp.mean(xv * xv, axis=-1, keepdims=True) + EPS)
            xh = xv * r
            g = fg_ref[...]
            e = xh * g - t_ref[...]
            ls_ref[...] += 0.5 * jnp.sum(e * e) / D
            dy = e / D
            dg_ref[...] += jnp.sum(dy * xh, axis=0, keepdims=True)
            dxh = dy * g
            dx_ref[...] = r * (dxh - xh * jnp.mean(dxh * xh, axis=-1, keepdims=True))

        @pl.when(i >= nlat)
        def _():
            dx_ref[...] = jnp.zeros_like(dx_ref)

    row = pl.BlockSpec((TB, D), lambda i: (i, 0))
    trow = pl.BlockSpec((TB, D), lambda i: (jnp.minimum(i, nlat - 1), 0))
    vec = pl.BlockSpec((1, D), lambda i: (0, 0))
    return _pc(body, name=name, grid=(L // TB,), in_specs=[row, vec, trow],
               out_specs=[pl.BlockSpec((1, LANE), lambda i: (0, 0)), row, vec],
               out_shape=[jax.ShapeDtypeStruct((1, LANE), F32), jax.ShapeDtypeStruct((L, D), F32),
                          jax.ShapeDtypeStruct((1, D), F32)],
               compiler_params=_cp(("arbitrary",)))(x, fg, tgt)


def rope_tables(cfg):
    half = cfg.RDK // 2
    nf = half // 2
    pos = np.arange(cfg.T)
    row = (pos // cfg.GW).astype(np.float32)
    col = (pos % cfg.GW).astype(np.float32)
    inv = jnp.asarray(ROPE_BASE, F32) ** (-jnp.arange(nf, dtype=F32) / nf)
    ar = jnp.asarray(row)[:, None] * inv[None, :]
    ac = jnp.asarray(col)[:, None] * inv[None, :]
    cos = jnp.concatenate([jnp.cos(ar), jnp.cos(ar), jnp.cos(ac), jnp.cos(ac)], axis=1)
    sin = jnp.concatenate([-jnp.sin(ar), jnp.sin(ar), -jnp.sin(ac), jnp.sin(ac)], axis=1)
    cos = jnp.concatenate([cos, jnp.ones((cfg.TC, cfg.RDK), F32)], axis=0)
    sin = jnp.concatenate([sin, jnp.zeros((cfg.TC, cfg.RDK), F32)], axis=0)
    return cos, sin


def _rb(cfg):
    rb = (cfg.T + cfg.TC) // 4
    assert rb % 16 == 0
    return rb


def _swap32(t):
    lane = lax.broadcasted_iota(jnp.int32, t.shape, 1)
    return jnp.where((lane % 64) < 32, pltpu.roll(t, 96, 1), pltpu.roll(t, 32, 1))


def rope_fwd(cfg, name, P, cos, sin):
    L = P.shape[0]
    TB = _rb(cfg)
    off = _offsets(cfg)
    cq, ck = off["lq"] // LANE, off["lk"] // LANE
    scale = cfg.RDK ** -0.5

    def body(q_ref, k_ref, c_ref, s_ref, qo_ref, ko_ref):
        c = c_ref[...]
        s = s_ref[...]
        q = q_ref[...]
        k = k_ref[...]
        qo_ref[...] = (q * c + _swap32(q) * s) * scale
        ko_ref[...] = k * c + _swap32(k) * s

    tab = pl.BlockSpec((TB, LANE), lambda i, h: (i, 0))
    out = pl.BlockSpec((TB, LANE), lambda i, h: (i, h))
    shp = jax.ShapeDtypeStruct((L, cfg.RH * cfg.RDK), F32)
    return _pc(body, name=name, grid=(L // TB, cfg.RH),
               in_specs=[pl.BlockSpec((TB, LANE), lambda i, h: (i, cq + h)),
                         pl.BlockSpec((TB, LANE), lambda i, h: (i, ck + h)), tab, tab],
               out_specs=[out, out], out_shape=[shp, shp],
               compiler_params=_cp(("parallel", "parallel")))(P, P, cos, sin)


def rope_bwd(cfg, name, dq2, dk2, cos, sin):
    L, W = dq2[0].shape
    TB = _rb(cfg)
    scale = cfg.RDK ** -0.5

    def body(dqf_ref, dqb_ref, dkf_ref, dkb_ref, c_ref, s_ref, qo_ref, ko_ref):
        c = c_ref[...]
        s = s_ref[...]
        dq = dqf_ref[...] + dqb_ref[...]
        dk = dkf_ref[...] + dkb_ref[...]
        qo_ref[...] = ((dq * c - _swap32(dq) * s) * scale).astype(qo_ref.dtype)
        ko_ref[...] = (dk * c - _swap32(dk) * s).astype(ko_ref.dtype)

    tab = pl.BlockSpec((TB, LANE), lambda i, h: (i, 0))
    blk = pl.BlockSpec((TB, LANE), lambda i, h: (i, h))
    shp = jax.ShapeDtypeStruct((L, W), BF16)
    return _pc(body, name=name, grid=(L // TB, cfg.RH), in_specs=[blk, blk, blk, blk, tab, tab],
               out_specs=[blk, blk], out_shape=[shp, shp],
               compiler_params=_cp(("parallel", "parallel")))(*dq2, *dk2, cos, sin)


def _ret_chunk_map(cfg):
    C = cfg.RDK
    n = (cfg.T + cfg.TC) // C
    nlat, nctx = cfg.T // C, cfg.TC // C

    def chunk(d, s):
        if d == 0:
            return jnp.where(s < nctx, nlat + s, s - nctx)
        return n - 1 - s

    return n, chunk


def _ret_decay_terms(d, lam, C):
    ii = lax.broadcasted_iota(jnp.int32, (C, C), 0)
    jj = lax.broadcasted_iota(jnp.int32, (C, C), 1)
    diff = (ii - jj if d == 0 else jj - ii).astype(F32)
    dpos = jnp.maximum(diff, 0.0)
    Dm = jnp.where(diff >= 0, jnp.exp(dpos * lam), 0.0)
    ic = lax.broadcasted_iota(jnp.int32, (C, 1), 0).astype(F32)
    cxi = ic + 1.0 if d == 0 else C - ic
    cze = C - 1.0 - ic if d == 0 else ic
    xi = jnp.exp(cxi * lam)
    ze = jnp.exp(cze * lam)
    g = jnp.exp(jnp.full((1, 1), C, F32) * lam)
    return dpos, Dm, cxi, cze, xi, ze, g


def retention_fwd(cfg, name, qr, kr, P, lam):
    L = P.shape[0]
    C, DV, RH = cfg.RDK, cfg.RDV, cfg.RH
    n, chunk = _ret_chunk_map(cfg)

    def body(lam_ref, qf_ref, qb_ref, kf_ref, kb_ref, vf_ref, vb_ref, of_ref, ob_ref, st_ref, S):
        s = pl.program_id(0)

        @pl.when(s == 0)
        def _():
            S[...] = jnp.zeros_like(S)

        for d, (q_ref, k_ref, v_ref, o_ref) in enumerate(((qf_ref, kf_ref, vf_ref, of_ref),
                                                          (qb_ref, kb_ref, vb_ref, ob_ref))):
            for h in range(RH):
                _, Dm, _, _, xi, ze, g = _ret_decay_terms(d, lam_ref[d, h], C)
                k = k_ref[:, h * C:(h + 1) * C]
                qb = q_ref[:, h * C:(h + 1) * C].astype(BF16)
                kb = k.astype(BF16)
                vb = v_ref[:, h * DV:(h + 1) * DV].astype(BF16)
                Sv = S[d, h]
                st_ref[d, h] = Sv
                A = dot_nt(qb, kb) * Dm
                o_ref[:, h * DV:(h + 1) * DV] = dot_nn(A.astype(BF16), vb) + dot_nn(qb, Sv.astype(BF16)) * xi
                S[d, h] = Sv * g + dot_tn((k * ze).astype(BF16), vb)

    def spec(w, col, d):
        return pl.BlockSpec((C, w), lambda s: (chunk(d, s), col))

    W, WV = RH * C, RH * DV
    return _pc(body, name=name, grid=(n,),
               in_specs=[pl.BlockSpec(memory_space=pltpu.SMEM), spec(W, 0, 0), spec(W, 0, 1), spec(W, 0, 0),
                         spec(W, 0, 1), spec(WV, 1, 0), spec(WV, 1, 1)],
               out_specs=[spec(WV, 0, 0), spec(WV, 0, 1),
                          pl.BlockSpec((2, RH, None, C, DV), lambda s: (0, 0, s, 0, 0))],
               out_shape=[jax.ShapeDtypeStruct((L, WV), F32), jax.ShapeDtypeStruct((L, WV), F32),
                          jax.ShapeDtypeStruct((2, RH, n, C, DV), F32)],
               scratch_shapes=[pltpu.VMEM((2, RH, C, DV), F32)],
               compiler_params=_cp(("arbitrary",)))(lam, qr, qr, kr, kr, P, P)


def retention_bwd(cfg, name, qr, kr, P, lam, st, do):
    L = P.shape[0]
    C, DV, RH = cfg.RDK, cfg.RDV, cfg.RH
    n, chunk = _ret_chunk_map(cfg)

    def body(lam_ref, qf_ref, qb_ref, kf_ref, kb_ref, vf_ref, vb_ref, st_ref, dof_ref, dob_ref,
             dqf_ref, dqb_ref, dkf_ref, dkb_ref, dvf_ref, dvb_ref, dl_ref, dS):
        si = pl.program_id(0)

        @pl.when(si == 0)
        def _():
            dS[...] = jnp.zeros_like(dS)
            dl_ref[...] = jnp.zeros_like(dl_ref)

        dirs = ((qf_ref, kf_ref, vf_ref, dof_ref, dqf_ref, dkf_ref, dvf_ref),
                (qb_ref, kb_ref, vb_ref, dob_ref, dqb_ref, dkb_ref, dvb_ref))
        for d, (q_ref, k_ref, v_ref, do_ref, dq_ref, dk_ref, dv_ref) in enumerate(dirs):
            for h in range(RH):
                dpos, Dm, cxi, cze, xi, ze, g = _ret_decay_terms(d, lam_ref[d, h], C)
                hk = slice(h * C, (h + 1) * C)
                hv = slice(h * DV, (h + 1) * DV)
                k = k_ref[:, hk]
                do = do_ref[:, hv]
                qb = q_ref[:, hk].astype(BF16)
                kb = k.astype(BF16)
                vb = v_ref[:, hv].astype(BF16)
                dob = do.astype(BF16)
                Sn = st_ref[d, h]
                Snb = Sn.astype(BF16)
                dSn = dS[d, h]
                dSb = dSn.astype(BF16)
                A = dot_nt(qb, kb) * Dm
                dA = dot_nt(dob, vb)
                dQK = (dA * Dm).astype(BF16)
                kzb = (k * ze).astype(BF16)
                dv_ref[:, hv] = dot_tn(A.astype(BF16), dob) + dot_nn(kzb, dSb)
                dkz = dot_nt(vb, dSb)
                doxb = (do * xi).astype(BF16)
                dq_ref[:, hk] = dot_nn(dQK, kb) + dot_nt(doxb, Snb)
                dk_ref[:, hk] = dot_tn(dQK, qb) + dkz * ze
                QS = dot_nn(qb, Snb)
                t = (jnp.sum(dA * A * dpos) + jnp.sum(do * QS * (cxi * xi)) + jnp.sum(k * dkz * (cze * ze)))
                t4 = jnp.sum(dSn * Sn, axis=0, keepdims=True)
                t4 = jnp.sum(t4 * (g * C), axis=1, keepdims=True)
                dl_ref[d, h] += t + t4
                dS[d, h] = g * dSn + dot_tn(qb, doxb)

    def spec(w, col, d):
        return pl.BlockSpec((C, w), lambda si: (chunk(d, n - 1 - si), col))

    W, WV = RH * C, RH * DV
    return _pc(body, name=name, grid=(n,),
               in_specs=[pl.BlockSpec(memory_space=pltpu.SMEM), spec(W, 0, 0), spec(W, 0, 1), spec(W, 0, 0),
                         spec(W, 0, 1), spec(WV, 1, 0), spec(WV, 1, 1),
                         pl.BlockSpec((2, RH, None, C, DV), lambda si: (0, 0, n - 1 - si, 0, 0)),
                         spec(WV, 0, 0), spec(WV, 0, 1)],
               out_specs=[spec(W, 0, 0), spec(W, 0, 1), spec(W, 0, 0), spec(W, 0, 1), spec(WV, 0, 0), spec(WV, 0, 1),
                          pl.BlockSpec((2, RH, 8, LANE), lambda si: (0, 0, 0, 0))],
               out_shape=[jax.ShapeDtypeStruct((L, W), F32)] * 4 + [jax.ShapeDtypeStruct((L, WV), F32)] * 2
               + [jax.ShapeDtypeStruct((2, RH, 8, LANE), F32)],
               scratch_shapes=[pltpu.VMEM((2, RH, C, DV), F32)],
               compiler_params=_cp(("arbitrary",)))(lam, qr, qr, kr, kr, P, P, st, do, do)


def add_cast(cfg, name, a, b):
    L, W = a.shape
    TB = _rb(cfg)

    def body(a_ref, b_ref, o_ref):
        o_ref[...] = (a_ref[...] + b_ref[...]).astype(o_ref.dtype)

    spec = pl.BlockSpec((TB, W), lambda i: (i, 0))
    return _pc(body, name=name, grid=(L // TB,), in_specs=[spec, spec], out_specs=spec,
               out_shape=jax.ShapeDtypeStruct((L, W), BF16), compiler_params=_cp(("parallel",)))(a, b)


def ggn_fwd(cfg, name, o2, P, gn_g):
    L = P.shape[0]
    TB, DV, RH = _rb(cfg), cfg.RDV, cfg.RH
    gc0 = _offsets(cfg)["lg"] // DV

    def body(of_ref, ob_ref, gate_ref, g_ref, out_ref):
        o = of_ref[...] + ob_ref[...]
        mu = jnp.mean(o, axis=-1, keepdims=True)
        xc = o - mu
        var = jnp.mean(xc * xc, axis=-1, keepdims=True)
        y = xc * lax.rsqrt(var + EPS) * g_ref[...]
        out_ref[...] = (y * _silu(gate_ref[...])).astype(out_ref.dtype)

    blk = pl.BlockSpec((TB, DV), lambda i, h: (i, h))
    return _pc(body, name=name, grid=(L // TB, RH),
               in_specs=[blk, blk, pl.BlockSpec((TB, DV), lambda i, h: (i, gc0 + h)),
                         pl.BlockSpec((1, DV), lambda i, h: (0, h))],
               out_specs=blk, out_shape=jax.ShapeDtypeStruct((L, RH * DV), BF16),
               compiler_params=_cp(("parallel", "parallel")))(*o2, P, gn_g)


def ggn_bwd(cfg, name, dout, o2, P, gn_g, col0):
    L = P.shape[0]
    TB, DV, RH = _rb(cfg), cfg.RDV, cfg.RH
    gc0 = _offsets(cfg)["lg"] // DV

    def body(d_ref, of_ref, ob_ref, gate_ref, g_ref, do_ref, dgate_ref, dg_ref):
        i = pl.program_id(1)
        o = of_ref[...] + ob_ref[...]
        mu = jnp.mean(o, axis=-1, keepdims=True)
        xc = o - mu
        var = jnp.mean(xc * xc, axis=-1, keepdims=True)
        r = lax.rsqrt(var + EPS)
        y = xc * r
        g = g_ref[...]
        gate = gate_ref[...]
        d = d_ref[...]
        dgate_ref[...] = (d * (y * g) * _dsilu(gate)).astype(dgate_ref.dtype)
        dyg = d * _silu(gate)
        s = jnp.sum(dyg * y, axis=0, keepdims=True)

        @pl.when(i == 0)
        def _():
            dg_ref[...] = s

        @pl.when(i > 0)
        def _():
            dg_ref[...] += s

        dy = dyg * g
        do_ref[...] = r * (dy - jnp.mean(dy, axis=-1, keepdims=True)
                           - y * jnp.mean(dy * y, axis=-1, keepdims=True))

    blk = pl.BlockSpec((TB, DV), lambda h, i: (i, h))
    return _pc(body, name=name, grid=(RH, L // TB),
               in_specs=[pl.BlockSpec((TB, DV), lambda h, i: (i, col0 + h)), blk, blk,
                         pl.BlockSpec((TB, DV), lambda h, i: (i, gc0 + h)),
                         pl.BlockSpec((1, DV), lambda h, i: (0, h))],
               out_specs=[blk, blk, pl.BlockSpec((1, DV), lambda h, i: (0, h))],
               out_shape=[jax.ShapeDtypeStruct((L, RH * DV), F32), jax.ShapeDtypeStruct((L, RH * DV), BF16),
                          jax.ShapeDtypeStruct((1, RH * DV), F32)],
               compiler_params=_cp(("parallel", "arbitrary")))(dout, *o2, P, gn_g)


def cast_cols(cfg, name, src, col0, ncols, width):
    L = src.shape[0]
    TB = _rb(cfg)

    def body(s_ref, o_ref):
        o_ref[...] = s_ref[...].astype(o_ref.dtype)

    spec = pl.BlockSpec((TB, width), lambda i, j: (i, col0 + j))
    return _pc(body, name=name, grid=(L // TB, ncols), in_specs=[spec],
               out_specs=pl.BlockSpec((TB, width), lambda i, j: (i, j)),
               out_shape=jax.ShapeDtypeStruct((L, ncols * width), BF16),
               compiler_params=_cp(("parallel", "parallel")))(src)


_CPAD = 16


def _conv_windows(cfg):
    T, TC, TB = cfg.T, cfg.TC, cfg.TB
    assert TC % TB == 0 and T % TB == 0 and cfg.CK // 2 < _CPAD
    return T // TB, [(T + j * TB, T + _CPAD + j * TB) for j in range(TC // TB)]


def _fill_padded(cfg, pb, get):
    T, TC, TB = cfg.T, cfg.TC, cfg.TB
    z = jnp.zeros((_CPAD, LANE), F32)
    pb[0:_CPAD, :] = z
    pb[_CPAD + T:2 * _CPAD + T, :] = z
    pb[2 * _CPAD + T + TC:3 * _CPAD + T + TC, :] = z

    def fill(i, c):
        r0 = pl.multiple_of(i * TB, TB)
        pb[pl.ds(r0 + _CPAD, TB), :] = get(r0)
        return c

    lax.fori_loop(0, T // TB, fill, 0)
    for j in range(TC // TB):
        pb[2 * _CPAD + T + j * TB:2 * _CPAD + T + (j + 1) * TB, :] = get(T + j * TB)


def _taps(win, TB):
    W = TB + 2 * _CPAD
    rot = {0: win}

    def tap(k):
        a, b = divmod(k + 1, 8)
        if b not in rot:
            rot[b] = pltpu.roll(win, W - b, 0)
        return rot[b][8 * a:8 * a + TB, :]

    return tap


def glu_dwconv_fwd(cfg, name, P, w, b):
    L = P.shape[0]
    T, TC, TB, K = cfg.T, cfg.TC, cfg.TB, cfg.CK
    off = _offsets(cfg)
    ca, cb = off["la"] // LANE, off["lb"] // LANE
    nlat, ctx_tiles = _conv_windows(cfg)
    PBL = 3 * _CPAD + T + TC

    def body(a_ref, b_ref, w_ref, bias_ref, y_ref, pb):
        _fill_padded(cfg, pb, lambda r0: a_ref[pl.ds(r0, TB), :] * _sigmoid(b_ref[pl.ds(r0, TB), :]))
        wv = w_ref[...]
        bias = bias_ref[...]

        def tile(win):
            tap = _taps(win, TB)
            acc = jnp.zeros((TB, LANE), F32) + bias
            for k in range(K):
                acc = acc + wv[k:k + 1, :] * tap(k)
            return acc

        def lat(i, c):
            r0 = pl.multiple_of(i * TB, TB)
            y_ref[pl.ds(r0, TB), :] = tile(pb[pl.ds(r0, TB + 2 * _CPAD), :])
            return c

        lax.fori_loop(0, nlat, lat, 0)
        for r0, w0 in ctx_tiles:
            y_ref[r0:r0 + TB, :] = tile(pb[w0:w0 + TB + 2 * _CPAD, :])

    return _pc(body, name=name, grid=(cfg.CW // LANE,),
               in_specs=[pl.BlockSpec((L, LANE), lambda j: (0, ca + j)),
                         pl.BlockSpec((L, LANE), lambda j: (0, cb + j)),
                         pl.BlockSpec((32, LANE), lambda j: (0, j)),
                         pl.BlockSpec((1, LANE), lambda j: (0, j))],
               out_specs=pl.BlockSpec((L, LANE), lambda j: (0, j)),
               out_shape=jax.ShapeDtypeStruct((L, cfg.CW), F32),
               scratch_shapes=[pltpu.VMEM((PBL, LANE), F32)],
               compiler_params=_cp(("parallel",)))(P, P, w, b)


def glu_dwconv_bwd(cfg, name, P, w, dy):
    L = P.shape[0]
    T, TC, TB, K = cfg.T, cfg.TC, cfg.TB, cfg.CK
    off = _offsets(cfg)
    ca, cb = off["la"] // LANE, off["lb"] // LANE
    nlat, ctx_tiles = _conv_windows(cfg)
    PBL = 3 * _CPAD + T + TC

    def body(a_ref, b_ref, w_ref, dy_ref, da_ref, db_ref, dw_ref, dbias_ref, pbu, pbd):
        _fill_padded(cfg, pbu, lambda r0: a_ref[pl.ds(r0, TB), :] * _sigmoid(b_ref[pl.ds(r0, TB), :]))
        _fill_padded(cfg, pbd, lambda r0: dy_ref[pl.ds(r0, TB), :])
        wv = w_ref[...]
        dw_ref[...] = jnp.zeros_like(dw_ref)
        dbias_ref[...] = jnp.zeros_like(dbias_ref)

        def tile(r0, winu, wind):
            tapu = _taps(winu, TB)
            tapd = _taps(wind, TB)
            dyt = dy_ref[pl.ds(r0, TB), :]
            du = jnp.zeros((TB, LANE), F32)
            for k in range(K):
                du = du + wv[k:k + 1, :] * tapd(K - 1 - k)
                dw_ref[k:k + 1, :] += jnp.sum(dyt * tapu(k), axis=0, keepdims=True)
            dbias_ref[...] += jnp.sum(dyt, axis=0, keepdims=True)
            a = a_ref[pl.ds(r0, TB), :]
            sg = _sigmoid(b_ref[pl.ds(r0, TB), :])
            da_ref[pl.ds(r0, TB), :] = (du * sg).astype(da_ref.dtype)
            db_ref[pl.ds(r0, TB), :] = (du * a * sg * (1.0 - sg)).astype(db_ref.dtype)

        def lat(i, c):
            r0 = pl.multiple_of(i * TB, TB)
            tile(r0, pbu[pl.ds(r0, TB + 2 * _CPAD), :], pbd[pl.ds(r0, TB + 2 * _CPAD), :])
            return c

        lax.fori_loop(0, nlat, lat, 0)
        for r0, w0 in ctx_tiles:
            tile(r0, pbu[w0:w0 + TB + 2 * _CPAD, :], pbd[w0:w0 + TB + 2 * _CPAD, :])

    col = pl.BlockSpec((L, LANE), lambda j: (0, j))
    return _pc(body, name=name, grid=(cfg.CW // LANE,),
               in_specs=[pl.BlockSpec((L, LANE), lambda j: (0, ca + j)),
                         pl.BlockSpec((L, LANE), lambda j: (0, cb + j)),
                         pl.BlockSpec((32, LANE), lambda j: (0, j)), col],
               out_specs=[col, col, pl.BlockSpec((32, LANE), lambda j: (0, j)),
                          pl.BlockSpec((1, LANE), lambda j: (0, j))],
               out_shape=[jax.ShapeDtypeStruct((L, cfg.CW), BF16), jax.ShapeDtypeStruct((L, cfg.CW), BF16),
                          jax.ShapeDtypeStruct((32, cfg.CW), F32), jax.ShapeDtypeStruct((1, cfg.CW), F32)],
               scratch_shapes=[pltpu.VMEM((PBL, LANE), F32), pltpu.VMEM((PBL, LANE), F32)],
               compiler_params=_cp(("parallel",)))(P, P, w, dy)


def ln_silu_fwd(cfg, name, y, g, b):
    L, W = y.shape
    TB = cfg.TB

    def body(y_ref, g_ref, b_ref, o_ref):
        yv = y_ref[...]
        mu = jnp.mean(yv, axis=-1, keepdims=True)
        xc = yv - mu
        var = jnp.mean(xc * xc, axis=-1, keepdims=True)
        z = xc * lax.rsqrt(var + EPS) * g_ref[...] + b_ref[...]
        o_ref[...] = _silu(z).astype(o_ref.dtype)

    row = pl.BlockSpec((TB, W), lambda i: (i, 0))
    vec = pl.BlockSpec((1, W), lambda i: (0, 0))
    return _pc(body, name=name, grid=(L // TB,), in_specs=[row, vec, vec], out_specs=row,
               out_shape=jax.ShapeDtypeStruct((L, W), BF16), compiler_params=_cp(("parallel",)))(y, g, b)


def ln_silu_bwd(cfg, name, dact, y, g, b):
    L, W = y.shape
    TB = cfg.TB

    def body(d_ref, y_ref, g_ref, b_ref, dy_ref, dg_ref, db_ref):
        i = pl.program_id(0)
        yv = y_ref[...]
        mu = jnp.mean(yv, axis=-1, keepdims=True)
        xc = yv - mu
        var = jnp.mean(xc * xc, axis=-1, keepdims=True)
        r = lax.rsqrt(var + EPS)
        yh = xc * r
        g = g_ref[...]
        z = yh * g + b_ref[...]
        dz = d_ref[...] * _dsilu(z)
        sg = jnp.sum(dz * yh, axis=0, keepdims=True)
        sb = jnp.sum(dz, axis=0, keepdims=True)

        @pl.when(i == 0)
        def _():
            dg_ref[...] = sg
            db_ref[...] = sb

        @pl.when(i > 0)
        def _():
            dg_ref[...] += sg
            db_ref[...] += sb

        dh = dz * g
        dy_ref[...] = r * (dh - jnp.mean(dh, axis=-1, keepdims=True)
                           - yh * jnp.mean(dh * yh, axis=-1, keepdims=True))

    row = pl.BlockSpec((TB, W), lambda i: (i, 0))
    vec = pl.BlockSpec((1, W), lambda i: (0, 0))
    return _pc(body, name=name, grid=(L // TB,), in_specs=[row, row, vec, vec], out_specs=[row, vec, vec],
               out_shape=[jax.ShapeDtypeStruct((L, W), F32), jax.ShapeDtypeStruct((1, W), F32),
                          jax.ShapeDtypeStruct((1, W), F32)],
               compiler_params=_cp(("arbitrary",)))(dact, y, g, b)


def _na_geometry(cfg):
    R = cfg.T // cfg.GW
    nb = R // cfg.NAR
    assert nb >= 3 and cfg.GW == 64 and cfg.NAR == 8
    ks = [int(np.clip(8 * b - 4, 0, R - 16)) for b in range(nb)]
    return R, nb, ks


_NTAB = 18


def _split3(x):
    hi = x.astype(BF16)
    r = x - hi.astype(F32)
    mid = r.astype(BF16)
    lo = (r - mid.astype(F32)).astype(BF16)
    return hi, mid, lo


def _na_col_onehot(cfg):
    GW, NAC = cfg.GW, cfg.NAC
    qc = np.arange(GW)[:, None]
    kc = np.arange(GW)[None, :]
    cs = np.clip(qc - NAC // 2, 0, GW - NAC)
    vcol = (kc >= cs) & (kc < cs + NAC)
    dd = np.clip(kc - qc + NAC - 1, 0, 2 * NAC - 2)
    oh = (np.arange(LANE)[:, None, None] == dd[None]).astype(np.float32)
    z = np.zeros_like(oh)
    oda = np.concatenate([oh, z], axis=2).reshape(LANE, GW * LANE)
    odb = np.concatenate([z, oh], axis=2).reshape(LANE, GW * LANE)
    cm = np.where(np.concatenate([vcol, vcol], axis=1), 0.0, NEG).astype(np.float32).reshape(1, GW * LANE)
    return oda, odb, cm


def na_tables(cfg, name, rpb):
    NH, GW = cfg.NH, cfg.GW
    na = rpb.shape[1]
    oda, odb, cm = _na_col_onehot(cfg)
    rp = jnp.zeros((NH, _NTAB + 1, LANE), F32).at[:, 1:1 + na, :rpb.shape[2]].set(rpb.astype(F32))
    r0 = rp[:, :_NTAB].reshape(NH * _NTAB, LANE)
    r1 = rp[:, 1:].reshape(NH * _NTAB, LANE)
    a = np.arange(_NTAB) - 1
    rm0 = np.where((a >= 0) & (a < na), 0.0, NEG).astype(np.float32)
    rm1 = np.where((a + 1 >= 0) & (a + 1 < na), 0.0, NEG).astype(np.float32)
    half = (np.arange(GW * LANE) % LANE >= GW)[None, :]
    rmask = np.where(half, np.tile(rm1, NH)[:, None], np.tile(rm0, NH)[:, None]).astype(np.float32)
    tn = 2048
    rows = NH * _NTAB

    def body(r0_ref, r1_ref, a_ref, b_ref, cm_ref, rm_ref, o_ref):
        acc = cm_ref[...] + rm_ref[...]
        for t in _split3(r0_ref[...]):
            acc = acc + dot_nn(t, a_ref[...])
        for t in _split3(r1_ref[...]):
            acc = acc + dot_nn(t, b_ref[...])
        o_ref[...] = acc

    rs = pl.BlockSpec((rows, LANE), lambda n: (0, 0))
    out = _pc(body, name=name, grid=(GW * LANE // tn,),
              in_specs=[rs, rs, pl.BlockSpec((LANE, tn), lambda n: (0, n)), pl.BlockSpec((LANE, tn), lambda n: (0, n)),
                        pl.BlockSpec((1, tn), lambda n: (0, n)), pl.BlockSpec((rows, tn), lambda n: (0, n))],
              out_specs=pl.BlockSpec((rows, tn), lambda n: (0, n)),
              out_shape=jax.ShapeDtypeStruct((rows, GW * LANE), F32),
              compiler_params=_cp(("parallel",)))(r0, r1, jnp.asarray(oda, BF16), jnp.asarray(odb, BF16),
                                                  jnp.asarray(cm), jnp.asarray(rmask))
    return out.reshape(NH, _NTAB, GW, LANE)


def _na_tiles(cfg, b):
    R, nb, _ = _na_geometry(cfg)
    NAR = cfg.NAR
    ksb = jnp.clip(8 * b - 4, 0, R - 16)
    for i in range(8):
        qr = 8 * b + i
        ws = jnp.clip(qr - NAR // 2, 0, R - NAR)
        for J in range(8):
            kr0 = ksb + 2 * J
            row = jnp.clip(kr0 - qr + NAR - 1, -1, _NTAB - 2) + 1
            v0 = jnp.logical_and(kr0 >= ws, kr0 < ws + NAR)
            v1 = jnp.logical_and(kr0 + 1 >= ws, kr0 + 1 < ws + NAR)
            yield i, J, row, v0, v1


def _na_fill_bias(cfg, tab_ref, bias, b):
    GW = cfg.GW
    first = lax.broadcasted_iota(jnp.int32, (GW, LANE), 1) < GW
    for i, J, row, v0, v1 in _na_tiles(cfg, b):
        ok = jnp.where(first, v0.astype(jnp.int32), v1.astype(jnp.int32))
        bias[i * GW:(i + 1) * GW, J * LANE:(J + 1) * LANE] = jnp.where(ok > 0, tab_ref[row], NEG)


def _na_specs(cfg):
    R, nb, ks = _na_geometry(cfg)
    off = _offsets(cfg)
    TQ = 8 * cfg.GW
    KP = 4 * cfg.GW
    ks4 = [k // 4 for k in ks]
    lat_blocks = cfg.T // KP

    def ks4_of(b):
        return jnp.clip(2 * b - 1, 0, R // 4 - 4)

    assert all(int(np.clip(2 * b - 1, 0, R // 4 - 4)) == ks4[b] for b in range(nb))
    assert cfg.TC == KP

    def col(nm):
        c0 = off[nm] // LANE
        q = pl.BlockSpec((TQ, LANE), lambda h, b: (b, c0 + h))
        parts = [pl.BlockSpec((KP, LANE), functools.partial(lambda h, b, t: (ks4_of(b) + t, c0 + h), t=t))
                 for t in range(4)]
        ctx = pl.BlockSpec((KP, LANE), lambda h, b: (lat_blocks, c0 + h))
        return q, parts, ctx

    return nb, TQ, KP, ks4_of, col


def na_fwd(cfg, name, P, tab):
    nb, TQ, KP, ks4_of, col = _na_specs(cfg)
    NH = cfg.NH
    scale = cfg.NDH ** -0.5
    qs, _, _ = col("nq")
    _, kparts, kctx = col("nk")
    _, vparts, vctx = col("nv")

    def body(q_ref, k0, k1, k2, k3, kc_ref, v0, v1, v2, v3, vc_ref, tab_ref, o_ref, lse_ref, bias_ref):
        _na_fill_bias(cfg, tab_ref, bias_ref, pl.program_id(1))
        q = (q_ref[...] * scale).astype(BF16)
        kl = jnp.concatenate([k0[...], k1[...], k2[...], k3[...]], axis=0).astype(BF16)
        vl = jnp.concatenate([v0[...], v1[...], v2[...], v3[...]], axis=0).astype(BF16)
        kc = kc_ref[...].astype(BF16)
        vc = vc_ref[...].astype(BF16)
        sl = dot_nt(q, kl) + bias_ref[...]
        sc = dot_nt(q, kc)
        m = jnp.maximum(jnp.max(sl, axis=-1, keepdims=True), jnp.max(sc, axis=-1, keepdims=True))
        pl_ = jnp.exp(sl - m)
        pc = jnp.exp(sc - m)
        den = jnp.sum(pl_, axis=-1, keepdims=True) + jnp.sum(pc, axis=-1, keepdims=True)
        o = dot_nn(pl_.astype(BF16), vl) + dot_nn(pc.astype(BF16), vc)
        o_ref[...] = o / den
        lse_ref[...] = m + jnp.log(den)

    return _pc(body, name=name, grid=(NH, nb),
               in_specs=[qs, *kparts, kctx, *vparts, vctx,
                         pl.BlockSpec((None, _NTAB, cfg.GW, LANE), lambda h, b: (h, 0, 0, 0))],
               out_specs=[pl.BlockSpec((TQ, LANE), lambda h, b: (b, h)),
                          pl.BlockSpec((None, TQ, 1), lambda h, b: (h, b, 0))],
               out_shape=[jax.ShapeDtypeStruct((cfg.T, NH * LANE), F32),
                          jax.ShapeDtypeStruct((NH, cfg.T, 1), F32)],
               scratch_shapes=[pltpu.VMEM((TQ, 4 * KP), F32)],
               compiler_params=_cp(("parallel", "parallel")))(P, *([P] * 5), *([P] * 5), tab)


def na_bwd(cfg, name, P, tab, o, lse, dmix, dcol0):
    nb, TQ, KP, ks4_of, col = _na_specs(cfg)
    NH, GW = cfg.NH, cfg.GW
    L = P.shape[0]
    scale = cfg.NDH ** -0.5
    qs, _, _ = col("nq")
    _, kparts, kctx = col("nk")
    _, vparts, vctx = col("nv")

    def body(q_ref, k0, k1, k2, k3, kc_ref, v0, v1, v2, v3, vc_ref, tab_ref, o_ref, lse_ref, do_ref,
             dq_ref, dk_ref, dv_ref, dtab_ref, bias_ref):
        b = pl.program_id(1)

        @pl.when(b == 0)
        def _():
            dk_ref[...] = jnp.zeros_like(dk_ref)
            dv_ref[...] = jnp.zeros_like(dv_ref)
            dtab_ref[...] = jnp.zeros_like(dtab_ref)

        _na_fill_bias(cfg, tab_ref, bias_ref, b)

        q = (q_ref[...] * scale).astype(BF16)
        kl = jnp.concatenate([k0[...], k1[...], k2[...], k3[...]], axis=0).astype(BF16)
        vl = jnp.concatenate([v0[...], v1[...], v2[...], v3[...]], axis=0).astype(BF16)
        kc = kc_ref[...].astype(BF16)
        vc = vc_ref[...].astype(BF16)
        lse = lse_ref[...]
        do = do_ref[...]
        dob = do.astype(BF16)
        p_l = jnp.exp(dot_nt(q, kl) + bias_ref[...] - lse)
        p_c = jnp.exp(dot_nt(q, kc) - lse)
        delta = jnp.sum(do * o_ref[...], axis=-1, keepdims=True)
        ds_l = p_l * (dot_nt(dob, vl) - delta)
        ds_c = p_c * (dot_nt(dob, vc) - delta)
        dslb = ds_l.astype(BF16)
        dscb = ds_c.astype(BF16)
        dq_ref[...] = ((dot_nn(dslb, kl) + dot_nn(dscb, kc)) * scale).astype(dq_ref.dtype)
        r0 = pl.multiple_of(ks4_of(b) * KP, KP)
        dk_ref[pl.ds(r0, 4 * KP), :] += dot_tn(dslb, q)
        dv_ref[pl.ds(r0, 4 * KP), :] += dot_tn(p_l.astype(BF16), dob)
        dk_ref[cfg.T:cfg.T + KP, :] += dot_tn(dscb, q)
        dv_ref[cfg.T:cfg.T + KP, :] += dot_tn(p_c.astype(BF16), dob)
        bias_ref[...] = ds_l
        for i, J, row, _, _ in _na_tiles(cfg, b):
            dtab_ref[row] += bias_ref[i * GW:(i + 1) * GW, J * LANE:(J + 1) * LANE]

    full = pl.BlockSpec((L, LANE), lambda h, b: (0, h))
    tabs = pl.BlockSpec((None, _NTAB, GW, LANE), lambda h, b: (h, 0, 0, 0))
    return _pc(body, name=name, grid=(NH, nb),
               in_specs=[qs, *kparts, kctx, *vparts, vctx, tabs,
                         pl.BlockSpec((TQ, LANE), lambda h, b: (b, h)),
                         pl.BlockSpec((None, TQ, 1), lambda h, b: (h, b, 0)),
                         pl.BlockSpec((TQ, LANE), lambda h, b: (b, dcol0 + h))],
               out_specs=[pl.BlockSpec((TQ, LANE), lambda h, b: (b, h)), full, full, tabs],
               out_shape=[jax.ShapeDtypeStruct((cfg.T, NH * LANE), BF16),
                          jax.ShapeDtypeStruct((L, NH * LANE), F32), jax.ShapeDtypeStruct((L, NH * LANE), F32),
                          jax.ShapeDtypeStruct((NH, _NTAB, GW, LANE), F32)],
               scratch_shapes=[pltpu.VMEM((TQ, 4 * KP), F32)],
               compiler_params=_cp(("parallel", "arbitrary")))(
                   P, *([P] * 5), *([P] * 5), tab, o, lse, dmix)


def na_ctx_fwd(cfg, name, P):
    off = _offsets(cfg)
    TC, NH = cfg.TC, cfg.NH
    rb = cfg.T // TC
    scale = cfg.NDH ** -0.5

    def body(q_ref, k_ref, v_ref, o_ref, lse_ref):
        q = (q_ref[...] * scale).astype(BF16)
        s = dot_nt(q, k_ref[...].astype(BF16))
        m = jnp.max(s, axis=-1, keepdims=True)
        p = jnp.exp(s - m)
        den = jnp.sum(p, axis=-1, keepdims=True)
        o_ref[...] = dot_nn(p.astype(BF16), v_ref[...].astype(BF16)) / den
        lse_ref[...] = m + jnp.log(den)

    spec = lambda nm: pl.BlockSpec((TC, LANE), functools.partial(lambda h, c0: (rb, c0 + h), c0=off[nm] // LANE))
    return _pc(body, name=name, grid=(NH,), in_specs=[spec("nq"), spec("nk"), spec("nv")],
               out_specs=[pl.BlockSpec((TC, LANE), lambda h: (0, h)), pl.BlockSpec((None, TC, 1), lambda h: (h, 0, 0))],
               out_shape=[jax.ShapeDtypeStruct((TC, NH * LANE), F32), jax.ShapeDtypeStruct((NH, TC, 1), F32)],
               compiler_params=_cp(("parallel",)))(P, P, P)


def na_ctx_bwd(cfg, name, P, o, lse, dmix, dcol0, dk_in, dv_in):
    off = _offsets(cfg)
    TC, NH = cfg.TC, cfg.NH
    rb = cfg.T // TC
    scale = cfg.NDH ** -0.5

    def body(q_ref, k_ref, v_ref, o_ref, lse_ref, do_ref, dki_ref, dvi_ref, dq_ref, dk_ref, dv_ref):
        q = (q_ref[...] * scale).astype(BF16)
        kb = k_ref[...].astype(BF16)
        vb = v_ref[...].astype(BF16)
        do = do_ref[...]
        dob = do.astype(BF16)
        p = jnp.exp(dot_nt(q, kb) - lse_ref[...])
        delta = jnp.sum(do * o_ref[...], axis=-1, keepdims=True)
        ds = (p * (dot_nt(dob, vb) - delta)).astype(BF16)
        dq_ref[...] = (dot_nn(ds, kb) * scale).astype(dq_ref.dtype)
        dk_ref[...] = (dki_ref[...] + dot_tn(ds, q)).astype(dk_ref.dtype)
        dv_ref[...] = (dvi_ref[...] + dot_tn(p.astype(BF16), dob)).astype(dv_ref.dtype)

    spec = lambda nm: pl.BlockSpec((TC, LANE), functools.partial(lambda h, c0: (rb, c0 + h), c0=off[nm] // LANE))
    hb = pl.BlockSpec((TC, LANE), lambda h: (0, h))
    ctxrow = pl.BlockSpec((TC, LANE), lambda h: (rb, h))
    shp = jax.ShapeDtypeStruct((TC, NH * LANE), BF16)
    return _pc(body, name=name, grid=(NH,),
               in_specs=[spec("nq"), spec("nk"), spec("nv"), hb, pl.BlockSpec((None, TC, 1), lambda h: (h, 0, 0)),
                         pl.BlockSpec((TC, LANE), lambda h: (rb, dcol0 + h)), ctxrow, ctxrow],
               out_specs=[hb, hb, hb], out_shape=[shp, shp, shp],
               compiler_params=_cp(("parallel",)))(P, P, P, o, lse, dmix, dk_in, dv_in)


def na_rpb_grad(cfg, name, dtab):
    NH, GW = cfg.NH, cfg.GW
    na, nd = 2 * cfg.NAR - 1, 2 * cfg.NAC - 1
    oda, odb, _ = _na_col_onehot(cfg)
    E = np.concatenate([oda.T, odb.T], axis=1)
    rows = NH * _NTAB

    def body(z_ref, e_ref, o_ref):
        zv = z_ref[...]
        hi = zv.astype(BF16)
        lo = (zv - hi.astype(F32)).astype(BF16)
        e = e_ref[...]
        o_ref[...] = dot_nn(hi, e) + dot_nn(lo, e)

    g = _pc(body, name=name, out_shape=jax.ShapeDtypeStruct((rows, 2 * LANE), F32),
            compiler_params=_cp())(dtab.reshape(rows, GW * LANE), jnp.asarray(E, BF16))
    g = g.reshape(NH, _NTAB, 2, LANE)
    return g[:, 1:1 + na, 0, :nd] + g[:, 0:na, 1, :nd]


def _seq_tiles(cfg):
    T, TC, TB = cfg.T, cfg.TC, cfg.TB
    tiles = []
    for i in range((T + TC) // TB):
        r0 = i * TB
        tiles.append((r0, r0 == 0 or r0 == T, r0 + TB == T or r0 + TB == T + TC))
    return tiles


def _shift3(ref_get, r0, TB, start, end, width):
    cur = ref_get(r0, TB)
    if start or end:
        rowi = lax.broadcasted_iota(jnp.int32, (TB, width), 0)
    up = jnp.where(rowi == 0, 0.0, pltpu.roll(cur, 1, 0)) if start else ref_get(r0 - 1, TB)
    dn = jnp.where(rowi == TB - 1, 0.0, pltpu.roll(cur, TB - 1, 0)) if end else ref_get(r0 + 1, TB)
    return up, cur, dn


def ffn_act_fwd(cfg, name, U2, w, b):
    _, L, DFF = U2.shape
    TB = cfg.TB
    tiles = _seq_tiles(cfg)

    def body(u_ref, w_ref, b_ref, a_ref):
        def plane(p, r0, st, en):
            up, cur, dn = _shift3(lambda r, n: u_ref[p, r:r + n, :], r0, TB, st, en, LANE)
            wv = w_ref[p]
            return wv[0:1, :] * up + wv[1:2, :] * cur + wv[2:3, :] * dn + b_ref[p]

        for r0, st, en in tiles:
            val = plane(0, r0, st, en)
            gate = plane(1, r0, st, en)
            a_ref[r0:r0 + TB, :] = (_silu(gate) * val).astype(a_ref.dtype)

    return _pc(body, name=name, grid=(DFF // LANE,),
               in_specs=[pl.BlockSpec((2, L, LANE), lambda j: (0, 0, j)),
                         pl.BlockSpec((2, 8, LANE), lambda j: (0, 0, j)),
                         pl.BlockSpec((2, 1, LANE), lambda j: (0, 0, j))],
               out_specs=pl.BlockSpec((L, LANE), lambda j: (0, j)),
               out_shape=jax.ShapeDtypeStruct((L, DFF), BF16),
               compiler_params=_cp(("parallel",)))(U2, w, b)


def ffn_act_bwd(cfg, name, U2, w, b, dA):
    _, L, DFF = U2.shape
    TB = cfg.TB
    tiles = _seq_tiles(cfg)

    def body(u_ref, w_ref, b_ref, da_ref, du_ref, dw_ref, db_ref, dbuf):
        dw_ref[...] = jnp.zeros_like(dw_ref)
        db_ref[...] = jnp.zeros_like(db_ref)
        for r0, st, en in tiles:
            shifted = []
            pre = []
            for p in range(2):
                up, cur, dn = _shift3(lambda r, n: u_ref[p, r:r + n, :], r0, TB, st, en, LANE)
                wv = w_ref[p]
                shifted.append((up, cur, dn))
                pre.append(wv[0:1, :] * up + wv[1:2, :] * cur + wv[2:3, :] * dn + b_ref[p])
            val, gate = pre
            da = da_ref[r0:r0 + TB, :]
            dpre = (da * _silu(gate), da * val * _dsilu(gate))
            for p in range(2):
                dbuf[p, r0:r0 + TB, :] = dpre[p]
                for k in range(3):
                    dw_ref[p, k:k + 1, :] += jnp.sum(dpre[p] * shifted[p][k], axis=0, keepdims=True)
                db_ref[p] += jnp.sum(dpre[p], axis=0, keepdims=True)
        for r0, st, en in tiles:
            for p in range(2):
                up, cur, dn = _shift3(lambda r, n: dbuf[p, r:r + n, :], r0, TB, st, en, LANE)
                wv = w_ref[p]
                du_ref[p, r0:r0 + TB, :] = (wv[0:1, :] * dn + wv[1:2, :] * cur + wv[2:3, :] * up).astype(du_ref.dtype)

    blk = pl.BlockSpec((2, L, LANE), lambda j: (0, 0, j))
    wspec = pl.BlockSpec((2, 8, LANE), lambda j: (0, 0, j))
    bspec = pl.BlockSpec((2, 1, LANE), lambda j: (0, 0, j))
    return _pc(body, name=name, grid=(DFF // LANE,),
               in_specs=[blk, wspec, bspec, pl.BlockSpec((L, LANE), lambda j: (0, j))],
               out_specs=[blk, wspec, bspec],
               out_shape=[jax.ShapeDtypeStruct((2, L, DFF), BF16), jax.ShapeDtypeStruct((2, 8, DFF), F32),
                          jax.ShapeDtypeStruct((2, 1, DFF), F32)],
               scratch_shapes=[pltpu.VMEM((2, L, LANE), F32)],
               compiler_params=_cp(("parallel",)))(U2, w, b, dA)


def _tm(L, parts):
    assert L % parts == 0
    return L // parts


def layer_fwd(cfg, l, XS, mod, wts, small, tabs):
    L, D = XS.shape
    off = _offsets(cfg)
    DIN = off["end"]
    tmA = _tm(L, 4)
    sv = {"XS": XS, "W": {}}

    def weight(name, after):
        sv["W"][name], tok = wts(name, after)
        return sv["W"][name], tok

    Win4, _ = weight("w_in", XS)
    nbi = Win4.shape[3]
    h1 = norm_mod_fwd(cfg, f"norm1_fwd_{l}", XS, small["norm1_g"], mod["sc1"], mod["sh1"])
    P = matmul(f"mm_in_{l}", h1, Win4, contract="nn", grid=(4, L // tmA),
               a_spec=pl.BlockSpec((tmA, D), lambda n, m: (m, 0)),
               b_spec=pl.BlockSpec((None, None, D, nbi), lambda n, m: (n, l, 0, 0)),
               out_shape=jax.ShapeDtypeStruct((L, DIN), F32),
               out_spec=pl.BlockSpec((tmA, nbi), lambda n, m: (m, n)), nk=1)
    qr, kr = rope_fwd(cfg, f"rope_fwd_{l}", P, tabs["cos"], tabs["sin"])
    o_f, o_b, st = retention_fwd(cfg, f"ret_fwd_{l}", qr, kr, P, small["lam"])
    o2 = (o_f, o_b)
    ret = ggn_fwd(cfg, f"ggn_fwd_{l}", o2, P, small["ret_gn_g"])
    ycv = glu_dwconv_fwd(cfg, f"dwconv_fwd_{l}", P, small["conv_dw_w"], small["conv_dw_b"])
    act = ln_silu_fwd(cfg, f"ln_silu_fwd_{l}", ycv, small["conv_ln_g"], small["conv_ln_b"])
    Wpw4, _ = weight("conv_pw", act)
    cv = mm_rowsharded(f"mm_pw_{l}", act, Wpw4, l, BF16, cfg.CW)
    bias = na_tables(cfg, f"na_tables_{l}", small["na_rpb"])
    na_l, lse = na_fwd(cfg, f"na_fwd_{l}", P, bias)
    na_c, lse_c = na_ctx_fwd(cfg, f"na_ctx_fwd_{l}", P)
    mix = jnp.concatenate([ret, cv, jnp.concatenate([na_l, na_c], axis=0).astype(BF16)], axis=1)
    Wout4, tok = weight("w_out", mix)
    Y1 = mm_rowsharded(f"mm_out_{l}", mix, Wout4, l, F32, D)
    XM, h2 = resid_norm_fwd(cfg, f"resid1_norm2_fwd_{l}", XS, Y1, mod["g1"] if tok is None else mod["g1"] + tok,
                            small["norm2_g"], mod["sc2"], mod["sh2"])
    Wup4, _ = weight("ffn_up", h2)
    nbu = Wup4.shape[3]
    tnu = nbu // 2
    U2 = matmul(f"mm_up_{l}", h2, Wup4, contract="nn", grid=(8, L // tmA),
                a_spec=pl.BlockSpec((tmA, D), lambda n, m: (m, 0)),
                b_spec=pl.BlockSpec((None, None, D, tnu), lambda n, m: (n // 2, l, 0, n % 2)),
                out_shape=jax.ShapeDtypeStruct((2, L, cfg.DFF), F32),
                out_spec=pl.BlockSpec((None, tmA, tnu), lambda n, m: (n // 4, m, n % 4)), nk=1)
    A = ffn_act_fwd(cfg, f"ffn_act_fwd_{l}", U2, small["ffn_dw_w"], small["ffn_dw_b"])
    Wdn4, _ = weight("ffn_down", A)
    Y2 = mm_rowsharded(f"mm_down_{l}", A, Wdn4, l, F32, D // 2)
    XO = resid_fwd(cfg, f"resid2_fwd_{l}", XM, Y2, mod["g2"])
    sv.update(h1=h1, P=P, qr=qr, kr=kr, o2=o2, st=st, ycv=ycv, act=act, bias=bias, na_l=na_l, lse=lse,
              na_c=na_c, lse_c=lse_c, mix=mix, Y1=Y1, XM=XM, h2=h2, U2=U2, A=A, Y2=Y2)
    return XO, sv


GRAD_GROUPS = (("ffn_down", "ffn_up"), ("w_out", "conv_pw", "w_in"))


def layer_bwd(cfg, l, dXO, sv, mod, wts, small, tabs, gbuf, ready):
    L, D = dXO.shape
    off = _offsets(cfg)
    DIN = off["end"]
    Win4, Wout4, Wup4, Wdn4, Wpw4 = wts["w_in"], wts["w_out"], wts["ffn_up"], wts["ffn_down"], wts["conv_pw"]
    tmA, tmB = _tm(L, 4), _tm(L, 8)
    depth = Win4.shape[1]
    gb, gs, dm = {}, {}, {}
    P = sv["P"]
    dY2, dm["g2"] = resid_bwd(cfg, f"resid2_bwd_{l}", dXO, sv["Y2"], mod["g2"])
    nbd = Wdn4.shape[2]
    dA = matmul(f"mm_down_da_{l}", dY2, Wdn4, contract="nt", grid=(4, L // tmA),
                a_spec=pl.BlockSpec((tmA, D), lambda j, m: (m, 0)),
                b_spec=pl.BlockSpec((None, None, nbd, D), lambda j, m: (j, l, 0, 0)),
                out_shape=jax.ShapeDtypeStruct((L, cfg.DFF), F32),
                out_spec=pl.BlockSpec((tmA, nbd), lambda j, m: (m, j)), nk=1)
    gb["ffn_down"] = wgrad(cfg, f"mm_down_dw_{l}", sv["A"], dY2,
                           lambda rb, ri: pl.BlockSpec((rb, nbd), lambda j, m: (ri(m), j)),
                           lambda rb, ri: pl.BlockSpec((rb, D), lambda j, m: (ri(m), 0)),
                           jax.ShapeDtypeStruct((depth, 4, nbd, D), BF16),
                           pl.BlockSpec((None, None, nbd, D), lambda j, m: (l, j, 0, 0)), 4, gbuf.get("ffn_down"))
    dU2, dfw, dfb = ffn_act_bwd(cfg, f"ffn_act_bwd_{l}", sv["U2"], small["ffn_dw_w"], small["ffn_dw_b"], dA)
    gs["ffn_dw_w"], gs["ffn_dw_b"] = dfw, dfb
    nbu = Wup4.shape[3]
    tnu = nbu // 2
    dH2 = matmul(f"mm_up_dh_{l}", dU2, Wup4, contract="nt", grid=(L // tmA, 8),
                 a_spec=pl.BlockSpec((None, tmA, tnu), lambda m, n: (n // 4, m, n % 4)),
                 b_spec=pl.BlockSpec((None, None, D, tnu), lambda m, n: (n // 2, l, 0, n % 2)),
                 out_shape=jax.ShapeDtypeStruct((L, D), F32),
                 out_spec=pl.BlockSpec((tmA, D), lambda m, n: (m, 0)), nk=8)
    gb["ffn_up"] = wgrad(cfg, f"mm_up_dw_{l}", sv["h2"], dU2,
                         lambda rb, ri: pl.BlockSpec((rb, D), lambda n, m: (ri(m), 0)),
                         lambda rb, ri: pl.BlockSpec((None, rb, tnu), lambda n, m: (n // 4, ri(m), n % 4)),
                         jax.ShapeDtypeStruct((depth, 4, D, nbu), BF16),
                         pl.BlockSpec((None, None, D, tnu), lambda n, m: (l, n // 2, 0, n % 2)), 8, gbuf.get("ffn_up"))
    dXM, dm["sc2"], dm["sh2"], gs["norm2_g"] = norm_mod_bwd(
        cfg, f"norm2_bwd_{l}", dH2, sv["XM"], small["norm2_g"], mod["sc2"], dXO)
    tok = ready(GRAD_GROUPS[0], gb)
    dY1, dm["g1"] = resid_bwd(cfg, f"resid1_bwd_{l}", dXM, sv["Y1"], mod["g1"] if tok is None else mod["g1"] + tok)
    nbo = Wout4.shape[2]
    dmix = matmul(f"mm_out_dmix_{l}", dY1, Wout4, contract="nt", grid=(4, L // tmA),
                  a_spec=pl.BlockSpec((tmA, D), lambda j, m: (m, 0)),
                  b_spec=pl.BlockSpec((None, None, nbo, D), lambda j, m: (j, l, 0, 0)),
                  out_shape=jax.ShapeDtypeStruct((L, D), F32),
                  out_spec=pl.BlockSpec((tmA, nbo), lambda j, m: (m, j)), nk=1)
    gb["w_out"] = wgrad(cfg, f"mm_out_dw_{l}", sv["mix"], dY1,
                        lambda rb, ri: pl.BlockSpec((rb, nbo), lambda j, m: (ri(m), j)),
                        lambda rb, ri: pl.BlockSpec((rb, D), lambda j, m: (ri(m), 0)),
                        jax.ShapeDtypeStruct((depth, 4, nbo, D), BF16),
                        pl.BlockSpec((None, None, nbo, D), lambda j, m: (l, j, 0, 0)), 4, gbuf.get("w_out"))
    RW = cfg.RH * cfg.RDV
    do, dlg, gs["ret_gn_g"] = ggn_bwd(cfg, f"ggn_bwd_{l}", dmix, sv["o2"], P, small["ret_gn_g"], 0)
    dqf, dqb, dkf, dkb, dvf, dvb, dlam = retention_bwd(
        cfg, f"ret_bwd_{l}", sv["qr"], sv["kr"], P, small["lam"], sv["st"], do)
    gs["lam"] = dlam[:, :, 0, 0]
    dlq, dlk = rope_bwd(cfg, f"rope_bwd_{l}", (dqf, dqb), (dkf, dkb), tabs["cos"], tabs["sin"])
    dlv = add_cast(cfg, f"ret_dv_{l}", dvf, dvb)
    dcv = cast_cols(cfg, f"conv_dcv_{l}", dmix, RW // LANE, cfg.CW // LANE, LANE)
    nbp = Wpw4.shape[2]
    dact = matmul(f"mm_pw_dact_{l}", dcv, Wpw4, contract="nt", grid=(4, L // tmA),
                  a_spec=pl.BlockSpec((tmA, cfg.CW), lambda j, m: (m, 0)),
                  b_spec=pl.BlockSpec((None, None, nbp, cfg.CW), lambda j, m: (j, l, 0, 0)),
                  out_shape=jax.ShapeDtypeStruct((L, cfg.CW), F32),
                  out_spec=pl.BlockSpec((tmA, nbp), lambda j, m: (m, j)), nk=1)
    gb["conv_pw"] = wgrad(cfg, f"mm_pw_dw_{l}", sv["act"], dcv,
                          lambda rb, ri: pl.BlockSpec((rb, nbp), lambda j, m: (ri(m), j)),
                          lambda rb, ri: pl.BlockSpec((rb, cfg.CW), lambda j, m: (ri(m), 0)),
                          jax.ShapeDtypeStruct((depth, 4, nbp, cfg.CW), BF16),
                          pl.BlockSpec((None, None, nbp, cfg.CW), lambda j, m: (l, j, 0, 0)), 4, gbuf.get("conv_pw"))
    dycv, gs["conv_ln_g"], gs["conv_ln_b"] = ln_silu_bwd(
        cfg, f"ln_silu_bwd_{l}", dact, sv["ycv"], small["conv_ln_g"], small["conv_ln_b"])
    dla, dlb, gs["conv_dw_w"], gs["conv_dw_b"] = glu_dwconv_bwd(cfg, f"dwconv_bwd_{l}", P, small["conv_dw_w"], dycv)
    nac0 = (RW + cfg.CW) // LANE
    dnq_l, dnk, dnv, dsb = na_bwd(cfg, f"na_bwd_{l}", P, sv["bias"], sv["na_l"], sv["lse"], dmix, nac0)
    dnq_c, dnk_c, dnv_c = na_ctx_bwd(cfg, f"na_ctx_bwd_{l}", P, sv["na_c"], sv["lse_c"], dmix, nac0, dnk, dnv)
    gs["na_rpb"] = na_rpb_grad(cfg, f"na_rpb_{l}", dsb)
    dnq = jnp.concatenate([dnq_l, dnq_c], axis=0)
    dnk = jnp.concatenate([dnk[:cfg.T].astype(BF16), dnk_c], axis=0)
    dnv = jnp.concatenate([dnv[:cfg.T].astype(BF16), dnv_c], axis=0)
    dP = jnp.concatenate([dlq, dlk, dlv, dlg, dla, dlb, dnq, dnk, dnv], axis=1)
    nbi = Win4.shape[3]
    dH1 = matmul(f"mm_in_dh_{l}", dP, Win4, contract="nt", grid=(L // tmA, 4),
                 a_spec=pl.BlockSpec((tmA, nbi), lambda m, n: (m, n)),
                 b_spec=pl.BlockSpec((None, None, D, nbi), lambda m, n: (n, l, 0, 0)),
                 out_shape=jax.ShapeDtypeStruct((L, D), F32),
                 out_spec=pl.BlockSpec((tmA, D), lambda m, n: (m, 0)), nk=4)
    gb["w_in"] = wgrad(cfg, f"mm_in_dw_{l}", sv["h1"], dP,
                       lambda rb, ri: pl.BlockSpec((rb, D), lambda n, m: (ri(m), 0)),
                       lambda rb, ri: pl.BlockSpec((rb, nbi), lambda n, m: (ri(m), n)),
                       jax.ShapeDtypeStruct((depth, 4, D, nbi), BF16),
                       pl.BlockSpec((None, None, D, nbi), lambda n, m: (l, n, 0, 0)), 4, gbuf.get("w_in"))
    dXS, dm["sc1"], dm["sh1"], gs["norm1_g"] = norm_mod_bwd(
        cfg, f"norm1_bwd_{l}", dH1, sv["XS"], small["norm1_g"], mod["sc1"], dXM, latent_only=(l == 0))
    return dXS, gb, gs, dm, ready(GRAD_GROUPS[1], gb)


def _layer_small(cfg, l, sp):
    DFF = cfg.DFF
    fw = sp["ffn_dw_w"][l].reshape(3, 2, DFF).transpose(1, 0, 2)
    fw = jnp.concatenate([fw, jnp.zeros((2, 5, DFF), F32)], axis=1)
    cw = jnp.concatenate([sp["conv_dw_w"][l], jnp.zeros((32 - cfg.CK, cfg.CW), F32)], axis=0)
    return dict(
        norm1_g=sp["norm1_g"][l][None], norm2_g=sp["norm2_g"][l][None],
        lam=jax.nn.log_sigmoid(sp["ret_decay"][l]), ret_gn_g=sp["ret_gn_g"][l][None],
        conv_dw_w=cw, conv_dw_b=sp["conv_dw_b"][l][None], conv_ln_g=sp["conv_ln_g"][l][None],
        conv_ln_b=sp["conv_ln_b"][l][None], na_rpb=sp["na_rpb"][l],
        ffn_dw_w=fw, ffn_dw_b=sp["ffn_dw_b"][l].reshape(2, 1, DFF))


def local_step(cfg, x, ctx, tgt, mods, wts, sp, grads_ready=lambda l, names, gb: None):
    depth = sp["norm1_g"].shape[0]
    cos, sin = rope_tables(cfg)
    tabs = dict(cos=cos, sin=sin)
    XS = jnp.concatenate([x, ctx], axis=0)
    smalls = [_layer_small(cfg, l, sp) for l in range(depth)]
    saves = []
    for l in range(depth):
        XS, sv = layer_fwd(cfg, l, XS, mods[l], functools.partial(wts, l), smalls[l], tabs)
        saves.append(sv)
    ls, dX, dfg = final_loss(cfg, "final_loss", XS, sp["final_g"][None], tgt)
    gb, gss, dms = {}, [None] * depth, [None] * depth
    token = None
    for l in reversed(range(depth)):
        mod = mods[l] if token is None else {**mods[l], "g2": mods[l]["g2"] + token}
        dX, gb, gss[l], dms[l], token = layer_bwd(cfg, l, dX, saves[l], mod, saves[l]["W"], smalls[l], tabs, gb,
                                                  functools.partial(grads_ready, l))
    return ls[0, 0], dX[:cfg.T], gb, gss, dms, dfg[0], token


MESH = pl.DeviceIdType.MESH
N_DEV = 8
N_CHIP = 4
BIG = ("w_in", "w_out", "ffn_up", "ffn_down", "conv_pw")
_ANY = pl.BlockSpec(memory_space=pl.ANY)


def _place():
    x, y, c = lax.axis_index("x"), lax.axis_index("y"), lax.axis_index("c")
    chips = [(1 - x, y), (x, 1 - y), (1 - x, 1 - y)]
    return x, y, c, chips


def allgather8(name, xs):
    m_per, n = xs.shape

    def body(x_ref, out_ref, send_sems, recv_sems, local_sem):
        x, y, c, chips = _place()
        me, sibling = (x, y, c), (x, y, 1 - c)

        def rows(px, py, pc):
            return out_ref.at[pl.ds((4 * px + 2 * py + pc) * m_per, m_per), :]

        def copy(k, block, to, src=None):
            return pltpu.make_async_remote_copy(
                src_ref=rows(*block) if src is None else src, dst_ref=rows(*block),
                send_sem=send_sems.at[k], recv_sem=recv_sems.at[k], device_id=to, device_id_type=MESH)

        mine = pltpu.make_async_copy(x_ref, rows(*me), local_sem)
        mine.start()
        first = [copy(0, me, sibling, src=x_ref)]
        first += [copy(1 + j, me, (*chip, c), src=x_ref) for j, chip in enumerate(chips)]
        for cp in first:
            cp.start()
        passed = [copy(4 + j, (*chip, c), sibling) for j, chip in enumerate(chips)]
        for j, chip in enumerate(chips):
            copy(1 + j, (*chip, c), me).wait_recv()
            passed[j].start()
        copy(0, sibling, me).wait_recv()
        for j, chip in enumerate(chips):
            copy(4 + j, (*chip, 1 - c), me).wait_recv()
        for cp in first + passed:
            cp.wait_send()
        mine.wait()

    return _pc(body, name=name, out_shape=jax.ShapeDtypeStruct((N_DEV * m_per, n), xs.dtype),
               in_specs=[pl.BlockSpec(memory_space=pltpu.VMEM)], out_specs=pl.BlockSpec(memory_space=pltpu.VMEM),
               scratch_shapes=[pltpu.SemaphoreType.DMA((7,)), pltpu.SemaphoreType.DMA((7,)), pltpu.SemaphoreType.DMA],
               compiler_params=pltpu.CompilerParams(vmem_limit_bytes=VMEM_LIMIT))(xs)


def _wpiece(ref, layer, chip_idx, half):
    rh = ref.shape[2] // 2
    return ref.at[chip_idx, layer, pl.ds(half * rh, rh)]


def _wcopy(ref, layer, chip_idx, half, send_sems, recv_sems, k, to):
    piece = _wpiece(ref, layer, chip_idx, half)
    return pltpu.make_async_remote_copy(src_ref=piece, dst_ref=piece, send_sem=send_sems.at[k],
                                        recv_sem=recv_sems.at[k], device_id=to, device_id_type=MESH)


def _w_ici_sends(outs, layer, send_sems, recv_sems):
    x, y, c, chips = _place()
    return [_wcopy(outs[a], layer, 2 * x + y, c, send_sems, recv_sems, 3 * a + t, (*chip, c))
            for a in range(len(outs)) for t, chip in enumerate(chips)]


def _w_ici_landed(outs, layer, send_sems, recv_sems):
    x, y, c, chips = _place()
    return [_wcopy(outs[a], layer, 2 * chip[0] + chip[1], c, send_sems, recv_sems, 3 * a + t, (x, y, c))
            for a in range(len(outs)) for t, chip in enumerate(chips)]


def _w_forward(outs, layer, send_sems, recv_sems, base):
    x, y, c, chips = _place()
    n = len(outs)
    sends = [_wcopy(outs[a], layer, 2 * chip[0] + chip[1], c, send_sems, recv_sems, base + 3 * a + t, (x, y, 1 - c))
             for a in range(n) for t, chip in enumerate(chips)]
    recvs = [_wcopy(outs[a], layer, 2 * chip[0] + chip[1], 1 - c, send_sems, recv_sems, base + 3 * a + t, (x, y, c))
             for a in range(n) for t, chip in enumerate(chips)]
    return sends, recvs


def allgather_layer(name, bufs, layer):
    n = len(bufs)

    def body(*refs):
        outs = refs[n:2 * n]
        send_sems, recv_sems = refs[2 * n:]
        sent = _w_ici_sends(outs, layer, send_sems, recv_sems)
        for cp in sent:
            cp.start()
        fwd, from_sib = _w_forward(outs, layer, send_sems, recv_sems, 3 * n)
        for landed, fw in zip(_w_ici_landed(outs, layer, send_sems, recv_sems), fwd):
            landed.wait_recv()
            fw.start()
        for cp in from_sib:
            cp.wait_recv()
        for cp in sent + fwd:
            cp.wait_send()

    return _pc(body, name=name, out_shape=[jax.ShapeDtypeStruct(b.shape, b.dtype) for b in bufs],
               in_specs=[_ANY] * n, out_specs=[_ANY] * n, input_output_aliases={a: a for a in range(n)},
               scratch_shapes=[pltpu.SemaphoreType.DMA((6 * n,)), pltpu.SemaphoreType.DMA((6 * n,))])(*bufs)


_HBM = pl.BlockSpec(memory_space=pltpu.HBM)
_SEM = pl.BlockSpec(memory_space=pltpu.SEMAPHORE)
_EFFECT = pltpu.SideEffectType.DATAFLOW_SIDE_EFFECTING


def allgather_layer_start(name, bufs, layer, after):
    n = len(bufs)

    def body(*refs):
        send_sems, recv_sems = refs[n + 1:n + 3]
        outs = refs[n + 3:2 * n + 3]
        token = refs[2 * n + 3]
        for cp in _w_ici_sends(outs, layer, send_sems, recv_sems):
            cp.start()
        token[...] = jnp.zeros_like(token)

    res = _pc(body, name=name,
              out_shape=(pltpu.SemaphoreType.DMA((3 * n,)), pltpu.SemaphoreType.DMA((3 * n,)),
                         *[pltpu.HBM(b.shape, b.dtype) for b in bufs], jax.ShapeDtypeStruct((8, LANE), F32)),
              in_specs=[_HBM] * n + [_ANY],
              out_specs=(_SEM, _SEM, *([_HBM] * n), pl.BlockSpec(memory_space=pltpu.VMEM)),
              input_output_aliases={a: a + 2 for a in range(n)},
              compiler_params=pltpu.CompilerParams(has_side_effects=_EFFECT))(
                  *[pltpu.with_memory_space_constraint(b, pltpu.HBM) for b in bufs], after)
    return res[0], res[1], list(res[2:2 + n]), res[2 + n]


def allgather_layer_wait(name, bufs, send_sems, recv_sems, after, layer):
    n = len(bufs)

    def body(*refs):
        ins = refs[:n]
        send_sems, recv_sems = refs[n:n + 2]
        for cp in _w_ici_sends(ins, layer, send_sems, recv_sems):
            cp.wait_send()
        for cp in _w_ici_landed(ins, layer, send_sems, recv_sems):
            cp.wait_recv()

    return _pc(body, name=name, out_shape=tuple(pltpu.HBM(b.shape, b.dtype) for b in bufs),
               in_specs=[_HBM] * n + [_SEM, _SEM, _ANY], out_specs=tuple([_HBM] * n),
               input_output_aliases={a: a for a in range(n)},
               compiler_params=pltpu.CompilerParams(has_side_effects=_EFFECT))(*bufs, send_sems, recv_sems, after)


def forward_halves(name, bufs, layer):
    n = len(bufs)

    def body(*refs):
        outs = refs[n:2 * n]
        send_sems, recv_sems = refs[2 * n:]
        fwd, from_sib = _w_forward(outs, layer, send_sems, recv_sems, 0)
        for cp in fwd:
            cp.start()
        for cp in from_sib:
            cp.wait_recv()
        for cp in fwd:
            cp.wait_send()

    return _pc(body, name=name, out_shape=[jax.ShapeDtypeStruct(b.shape, b.dtype) for b in bufs],
               in_specs=[_ANY] * n, out_specs=[_ANY] * n, input_output_aliases={a: a for a in range(n)},
               scratch_shapes=[pltpu.SemaphoreType.DMA((3 * n,)), pltpu.SemaphoreType.DMA((3 * n,))])(*bufs)


def exchange_rows(name, grads, layer):
    n = len(grads)

    def body(*refs):
        ins, outs = refs[:n], refs[n:2 * n]
        send_sems, recv_sems = refs[2 * n:]
        x, y, c, _ = _place()
        cps = []
        for a in range(n):
            rh = ins[a].shape[2] // 2
            cps.append(pltpu.make_async_remote_copy(
                src_ref=ins[a].at[layer, pl.ds(0, N_CHIP), pl.ds((1 - c) * rh, rh)], dst_ref=outs[a],
                send_sem=send_sems.at[a], recv_sem=recv_sems.at[a], device_id=(x, y, 1 - c), device_id_type=MESH))
        for cp in cps:
            cp.start()
        for cp in cps:
            cp.wait()

    return _pc(body, name=name,
               out_shape=[jax.ShapeDtypeStruct((N_CHIP, g.shape[2] // 2, g.shape[3]), g.dtype) for g in grads],
               in_specs=[_ANY] * n, out_specs=[_ANY] * n,
               scratch_shapes=[pltpu.SemaphoreType.DMA((n,)), pltpu.SemaphoreType.DMA((n,))])(*grads)


def _scatter_sends(parts, lands, send_sems, recv_sems):
    x, y, c, chips = _place()
    return [pltpu.make_async_remote_copy(
        src_ref=parts[a].at[2 * chip[0] + chip[1]], dst_ref=lands[a].at[2 * x + y], send_sem=send_sems.at[3 * a + t],
        recv_sem=recv_sems.at[3 * a + t], device_id=(*chip, c), device_id_type=MESH)
        for a in range(len(parts)) for t, chip in enumerate(chips)]


def _scatter_landed(lands, send_sems, recv_sems):
    x, y, c, chips = _place()
    return [pltpu.make_async_remote_copy(
        src_ref=lands[a].at[2 * chip[0] + chip[1]], dst_ref=lands[a].at[2 * chip[0] + chip[1]],
        send_sem=send_sems.at[3 * a + t], recv_sem=recv_sems.at[3 * a + t], device_id=(x, y, c), device_id_type=MESH)
        for a in range(len(lands)) for t, chip in enumerate(chips)]


def scatter_slices(name, parts, lands):
    n = len(parts)

    def body(*refs):
        ins, outs = refs[:n], refs[2 * n:3 * n]
        send_sems, recv_sems = refs[3 * n:]
        cps = _scatter_sends(ins, outs, send_sems, recv_sems)
        for cp in cps:
            cp.start()
        for cp in _scatter_landed(outs, send_sems, recv_sems):
            cp.wait_recv()
        for cp in cps:
            cp.wait_send()

    return _pc(body, name=name, out_shape=[jax.ShapeDtypeStruct(p.shape, p.dtype) for p in lands],
               in_specs=[_ANY] * (2 * n), out_specs=[_ANY] * n,
               input_output_aliases={n + a: a for a in range(n)},
               scratch_shapes=[pltpu.SemaphoreType.DMA((3 * n,)), pltpu.SemaphoreType.DMA((3 * n,))])(*parts, *lands)


def scatter_slices_start(name, parts, lands):
    n = len(parts)

    def body(*refs):
        send_sems, recv_sems = refs[2 * n:2 * n + 2]
        p_out, l_out = refs[2 * n + 2:3 * n + 2], refs[3 * n + 2:4 * n + 2]
        token = refs[4 * n + 2]
        for cp in _scatter_sends(p_out, l_out, send_sems, recv_sems):
            cp.start()
        token[...] = jnp.zeros_like(token)

    both = list(parts) + list(lands)
    res = _pc(body, name=name,
              out_shape=(pltpu.SemaphoreType.DMA((3 * n,)), pltpu.SemaphoreType.DMA((3 * n,)),
                         *[pltpu.HBM(b.shape, b.dtype) for b in both], jax.ShapeDtypeStruct((8, LANE), F32)),
              in_specs=[_HBM] * (2 * n),
              out_specs=(_SEM, _SEM, *([_HBM] * (2 * n)), pl.BlockSpec(memory_space=pltpu.VMEM)),
              input_output_aliases={a: a + 2 for a in range(2 * n)},
              compiler_params=pltpu.CompilerParams(has_side_effects=_EFFECT))(
                  *[pltpu.with_memory_space_constraint(b, pltpu.HBM) for b in both])
    return res[0], res[1], list(res[2:2 + n]), list(res[2 + n:2 + 2 * n]), res[2 + 2 * n]


def scatter_slices_wait(name, parts, lands, send_sems, recv_sems, after):
    n = len(parts)

    def body(*refs):
        p_in, l_in = refs[:n], refs[n:2 * n]
        send_sems, recv_sems = refs[2 * n:2 * n + 2]
        for cp in _scatter_sends(p_in, l_in, send_sems, recv_sems):
            cp.wait_send()
        for cp in _scatter_landed(l_in, send_sems, recv_sems):
            cp.wait_recv()

    both = list(parts) + list(lands)
    res = _pc(body, name=name, out_shape=tuple(pltpu.HBM(b.shape, b.dtype) for b in both),
              in_specs=[_HBM] * (2 * n) + [_SEM, _SEM, _ANY], out_specs=tuple([_HBM] * (2 * n)),
              input_output_aliases={a: a for a in range(2 * n)},
              compiler_params=pltpu.CompilerParams(has_side_effects=_EFFECT))(*both, send_sems, recv_sems, after)
    return list(res[n:])


def share_rows(name, bufs):
    n = len(bufs)

    def body(*refs):
        outs = refs[n:2 * n]
        send_sems, recv_sems = refs[2 * n:]
        x, y, c, _ = _place()

        def half(a, h):
            return outs[a].at[pl.ds(0, 2), h]

        cps = [pltpu.make_async_remote_copy(
            src_ref=half(a, c), dst_ref=half(a, c), send_sem=send_sems.at[a], recv_sem=recv_sems.at[a],
            device_id=(x, y, 1 - c), device_id_type=MESH) for a in range(n)]
        for cp in cps:
            cp.start()
        for a in range(n):
            pltpu.make_async_remote_copy(
                src_ref=half(a, 1 - c), dst_ref=half(a, 1 - c), send_sem=send_sems.at[a],
                recv_sem=recv_sems.at[a], device_id=(x, y, c), device_id_type=MESH).wait_recv()
        for cp in cps:
            cp.wait_send()

    return _pc(body, name=name, out_shape=[jax.ShapeDtypeStruct(b.shape, b.dtype) for b in bufs],
               in_specs=[_ANY] * n, out_specs=[_ANY] * n, input_output_aliases={a: a for a in range(n)},
               scratch_shapes=[pltpu.SemaphoreType.DMA((n,)), pltpu.SemaphoreType.DMA((n,))])(*bufs)


def _row_tile(R, C, nbytes=1 << 20):
    t = 8
    while t * 2 <= R and R % (t * 2) == 0 and t * 2 * C * 4 <= nbytes:
        t *= 2
    assert R % t == 0
    return t


def to_bf16_block(name, w, chip_arr, layer, after=None):
    _, R, C = w.shape
    tr = _row_tile(R, C)

    def body(j_ref, w_ref, *rest):
        o_ref = rest[-1]
        o_ref[...] = w_ref[...].astype(o_ref.dtype)

    in_specs, args = [pl.BlockSpec((None, tr, C), lambda i, j_ref: (layer, i, 0))], (chip_arr, w)
    if after is not None:
        in_specs, args = in_specs + [_ANY], args + (after,)
    gs = pltpu.PrefetchScalarGridSpec(
        num_scalar_prefetch=1, grid=(R // tr,), in_specs=in_specs,
        out_specs=pl.BlockSpec((None, None, tr, C), lambda i, j_ref: (j_ref[0], layer, i, 0)))
    return _pc(body, name=name, grid_spec=gs, out_shape=jax.ShapeDtypeStruct((N_CHIP,) + w.shape, BF16),
               compiler_params=_cp(("parallel",)))(*args)


def add_rows(name, g, ra, c_arr, layer):
    _, _, R, C = g.shape
    rh = R // 2
    tr = _row_tile(rh, C)
    nb = rh // tr

    def body(c_ref, g_ref, r_ref, o_ref):
        o_ref[...] = (g_ref[...].astype(F32) + r_ref[...].astype(F32)).astype(o_ref.dtype)

    gs = pltpu.PrefetchScalarGridSpec(
        num_scalar_prefetch=1, grid=(N_CHIP, nb),
        in_specs=[pl.BlockSpec((None, None, tr, C), lambda j, i, c_ref: (layer, j, c_ref[0] * nb + i, 0)),
                  pl.BlockSpec((None, tr, C), lambda j, i, c_ref: (j, i, 0))],
        out_specs=pl.BlockSpec((None, tr, C), lambda j, i, c_ref: (j, i, 0)))
    return _pc(body, name=name, grid_spec=gs, out_shape=jax.ShapeDtypeStruct(ra.shape, BF16),
               compiler_params=_cp(("parallel", "parallel")))(c_arr, g, ra)


def sum_rows_into(name, landed, c_arr, layer, into):
    n, rh, C = landed.shape
    tr = _row_tile(rh, C, nbytes=1 << 19)

    def body(*refs):
        g_ref, o_ref = refs[1], refs[-1]
        acc = g_ref[0].astype(F32)
        for j in range(1, n):
            acc = acc + g_ref[j].astype(F32)
        o_ref[...] = acc

    in_specs, args, alias = [pl.BlockSpec((n, tr, C), lambda i, c_ref: (0, i, 0))], (c_arr, landed), {}
    if into is not None:
        in_specs, args, alias = in_specs + [_ANY], args + (into,), {2: 0}
    gs = pltpu.PrefetchScalarGridSpec(
        num_scalar_prefetch=1, grid=(rh // tr,), in_specs=in_specs,
        out_specs=pl.BlockSpec((None, None, tr, C), lambda i, c_ref: (layer, c_ref[0], i, 0)))
    return _pc(body, name=name, grid_spec=gs, out_shape=jax.ShapeDtypeStruct((2, 2, rh, C), F32),
               input_output_aliases=alias, compiler_params=_cp(("parallel",)))(*args)


def own_row(name, part, chip_arr):
    _, R, C = part.shape
    tr = _row_tile(R, C)

    def body(j_ref, p_ref, o_ref):
        o_ref[...] = p_ref[...]

    gs = pltpu.PrefetchScalarGridSpec(
        num_scalar_prefetch=1, grid=(R // tr,),
        in_specs=[pl.BlockSpec((None, tr, C), lambda i, j_ref: (j_ref[0], i, 0))],
        out_specs=pl.BlockSpec((None, tr, C), lambda i, j_ref: (j_ref[0], i, 0)))
    return _pc(body, name=name, grid_spec=gs, out_shape=jax.ShapeDtypeStruct(part.shape, part.dtype),
               compiler_params=_cp(("parallel",)))(chip_arr, part)


def sum_leading(name, g, plane=None):
    n, R, C = g.shape
    tr = _row_tile(R, C, nbytes=(1 << 21) // n)

    def body(*refs):
        g_ref, o_ref = refs[-2:]
        acc = g_ref[0].astype(F32)
        for j in range(1, n):
            acc = acc + g_ref[j].astype(F32)
        o_ref[...] = acc

    if plane is None:
        return _pc(body, name=name, grid=(R // tr,), in_specs=[pl.BlockSpec((n, tr, C), lambda i: (0, i, 0))],
                   out_specs=pl.BlockSpec((tr, C), lambda i: (i, 0)), out_shape=jax.ShapeDtypeStruct((R, C), F32),
                   compiler_params=_cp(("parallel",)))(g)
    count, idx = plane
    gs = pltpu.PrefetchScalarGridSpec(
        num_scalar_prefetch=1, grid=(R // tr,),
        in_specs=[pl.BlockSpec((n, tr, C), lambda i, p_ref: (0, i, 0))],
        out_specs=pl.BlockSpec((None, tr, C), lambda i, p_ref: (p_ref[0], i, 0)))
    return _pc(body, name=name, grid_spec=gs, out_shape=jax.ShapeDtypeStruct((count, R, C), F32),
               compiler_params=_cp(("parallel",)))(idx, g)


def adamw(name, w, g, m, v, emit_g=False):
    R, C = w.shape
    tr = _row_tile(R, C)

    def body(w_ref, g_ref, m_ref, v_ref, d_ref, mo_ref, vo_ref, *go_ref):
        gv = g_ref[...]
        if emit_g:
            go_ref[0][...] = gv
        mn = ADAM_B1 * m_ref[...] + (1.0 - ADAM_B1) * gv
        vn = ADAM_B2 * v_ref[...] + (1.0 - ADAM_B2) * (gv * gv)
        m_hat = mn / (1.0 - ADAM_B1 ** ADAM_STEP)
        v_hat = vn / (1.0 - ADAM_B2 ** ADAM_STEP)
        d_ref[...] = -ADAM_LR * (m_hat / (jnp.sqrt(v_hat) + ADAM_EPS) + ADAM_WD * w_ref[...])
        mo_ref[...] = mn
        vo_ref[...] = vn

    spec = pl.BlockSpec((tr, C), lambda i: (i, 0))
    shp = jax.ShapeDtypeStruct((R, C), F32)
    nout = 4 if emit_g else 3
    return _pc(body, name=name, grid=(R // tr,), in_specs=[spec] * 4, out_specs=[spec] * nout,
               out_shape=[shp] * nout, compiler_params=_cp(("parallel",)))(w, g, m, v)


_ADA_TN = 512


def adaln_fwd(name, cond, w, b):
    _, D, N = w.shape
    tn = min(_ADA_TN, N)

    def body(c_ref, w_ref, b_ref, o_ref):
        s = _silu(c_ref[...]).astype(BF16)
        o_ref[...] = dot_nn(s, w_ref[...].astype(BF16)) + b_ref[...]

    return _pc(body, name=name, grid=(2, N // tn),
               in_specs=[pl.BlockSpec((16, D), lambda l, n: (0, 0)),
                         pl.BlockSpec((None, D, tn), lambda l, n: (l, 0, n)),
                         pl.BlockSpec((None, 1, tn), lambda l, n: (l, 0, n))],
               out_specs=pl.BlockSpec((None, 16, tn), lambda l, n: (l, 0, n)),
               out_shape=jax.ShapeDtypeStruct((2, 16, N), F32),
               compiler_params=_cp(("parallel", "parallel")))(cond, w, b)


def adaln_bwd(name, cond, w, dm):
    _, D, N = w.shape
    tn = min(_ADA_TN, N)

    def body(c_ref, w_ref, dm_ref, gw_ref, ds_ref):
        first = jnp.logical_and(pl.program_id(0) == 0, pl.program_id(1) == 0)
        s = _silu(c_ref[...]).astype(BF16)
        dmb = dm_ref[...].astype(BF16)
        gw_ref[...] = dot_tn(s, dmb)
        p = dot_nt(dmb, w_ref[...].astype(BF16))

        @pl.when(first)
        def _():
            ds_ref[...] = p

        @pl.when(jnp.logical_not(first))
        def _():
            ds_ref[...] += p

    return _pc(body, name=name, grid=(2, N // tn),
               in_specs=[pl.BlockSpec((16, D), lambda l, n: (0, 0)),
                         pl.BlockSpec((None, D, tn), lambda l, n: (l, 0, n)),
                         pl.BlockSpec((None, 16, tn), lambda l, n: (l, 0, n))],
               out_specs=[pl.BlockSpec((None, D, tn), lambda l, n: (l, 0, n)),
                          pl.BlockSpec((16, D), lambda l, n: (0, 0))],
               out_shape=[jax.ShapeDtypeStruct((2, D, N), F32), jax.ShapeDtypeStruct((16, D), F32)],
               compiler_params=_cp(("arbitrary", "arbitrary")))(cond, w, dm)


def cctx_grad(name, parts, c_ctx):
    def body(p_ref, c_ref, o_ref):
        acc = p_ref[0]
        for j in range(1, N_CHIP):
            acc = acc + p_ref[j]
        o_ref[...] = acc * _dsilu(c_ref[...])

    return _pc(body, name=name, out_shape=jax.ShapeDtypeStruct(c_ctx.shape, F32))(parts, c_ctx)


def _pack(arrs):
    rows = []
    for a in arrs:
        f = a.reshape(-1)
        pad = (-f.shape[0]) % LANE
        rows.append(jnp.pad(f, (0, pad)).reshape(-1, LANE))
    out = jnp.concatenate(rows, axis=0)
    pad = (-out.shape[0]) % 8
    return jnp.pad(out, ((0, pad), (0, 0))) if pad else out


def _unpack(rows, shapes):
    out, r = [], 0
    for s in shapes:
        n = int(np.prod(s))
        nr = -(-n // LANE)
        out.append(rows[r:r + nr].reshape(-1)[:n].reshape(s))
        r += nr
    return out


MOD_NAMES = ("sh1", "sc1", "g1", "sh2", "sc2", "g2")


def kernel(x, c, ctx, c_ctx, w_ada, b_ada, norm1_g, w_in, ret_decay, ret_gn_g, conv_dw_w, conv_dw_b, conv_ln_g, conv_ln_b, conv_pw, na_rpb, w_out, norm2_g, ffn_up, ffn_dw_w, ffn_dw_b, ffn_down, final_g, loss_target, m_c_ctx, m_w_ada, m_b_ada, m_norm1_g, m_w_in, m_ret_decay, m_ret_gn_g, m_conv_dw_w, m_conv_dw_b, m_conv_ln_g, m_conv_ln_b, m_conv_pw, m_na_rpb, m_w_out, m_norm2_g, m_ffn_up, m_ffn_dw_w, m_ffn_dw_b, m_ffn_down, m_final_g, v_c_ctx, v_w_ada, v_b_ada, v_norm1_g, v_w_in, v_ret_decay, v_ret_gn_g, v_conv_dw_w, v_conv_dw_b, v_conv_ln_g, v_conv_ln_b, v_conv_pw, v_na_rpb, v_w_out, v_norm2_g, v_ffn_up, v_ffn_dw_w, v_ffn_dw_b, v_ffn_down, v_final_g):
    cfg = make_cfg(D=x.shape[2], T=x.shape[1], TC=ctx.shape[1], RH=ret_decay.shape[2], CW=conv_dw_b.shape[1],
                   NH=na_rpb.shape[1], DFF=ffn_dw_b.shape[1] // 2)
    D, T = cfg.D, cfg.T
    W = dict(c_ctx=c_ctx, w_ada=w_ada, b_ada=b_ada, norm1_g=norm1_g, w_in=w_in, ret_decay=ret_decay, ret_gn_g=ret_gn_g,
             conv_dw_w=conv_dw_w, conv_dw_b=conv_dw_b, conv_ln_g=conv_ln_g, conv_ln_b=conv_ln_b, conv_pw=conv_pw,
             na_rpb=na_rpb, w_out=w_out, norm2_g=norm2_g, ffn_up=ffn_up, ffn_dw_w=ffn_dw_w, ffn_dw_b=ffn_dw_b,
             ffn_down=ffn_down, final_g=final_g)
    Mo = dict(c_ctx=m_c_ctx, w_ada=m_w_ada, b_ada=m_b_ada, norm1_g=m_norm1_g, w_in=m_w_in, ret_decay=m_ret_decay,
              ret_gn_g=m_ret_gn_g, conv_dw_w=m_conv_dw_w, conv_dw_b=m_conv_dw_b, conv_ln_g=m_conv_ln_g,
              conv_ln_b=m_conv_ln_b, conv_pw=m_conv_pw, na_rpb=m_na_rpb, w_out=m_w_out, norm2_g=m_norm2_g,
              ffn_up=m_ffn_up, ffn_dw_w=m_ffn_dw_w, ffn_dw_b=m_ffn_dw_b, ffn_down=m_ffn_down, final_g=m_final_g)
    Vo = dict(c_ctx=v_c_ctx, w_ada=v_w_ada, b_ada=v_b_ada, norm1_g=v_norm1_g, w_in=v_w_in, ret_decay=v_ret_decay,
              ret_gn_g=v_ret_gn_g, conv_dw_w=v_conv_dw_w, conv_dw_b=v_conv_dw_b, conv_ln_g=v_conv_ln_g,
              conv_ln_b=v_conv_ln_b, conv_pw=v_conv_pw, na_rpb=v_na_rpb, w_out=v_w_out, norm2_g=v_norm2_g,
              ffn_up=v_ffn_up, ffn_dw_w=v_ffn_dw_w, ffn_dw_b=v_ffn_dw_b, ffn_down=v_ffn_down, final_g=v_final_g)
    order = list(W)
    xi, yi, ci = lax.axis_index("x"), lax.axis_index("y"), lax.axis_index("c")
    chip = 2 * xi + yi
    dev = 4 * xi + 2 * yi + ci
    NA = w_ada.shape[2]
    ncw, nfw = conv_dw_w.shape[2], ffn_dw_w.shape[2]

    c_arr = jnp.reshape(ci, (1,)).astype(jnp.int32)
    chip_arr = jnp.reshape(chip, (1,)).astype(jnp.int32)

    g_in = allgather8("ag_small_in", _pack([c[0], conv_dw_w, ffn_dw_w])).reshape(N_DEV, -1, LANE)
    c8 = g_in[:, :D // LANE].reshape(N_DEV, D)
    cw_parts, fw_parts = [], []
    for j in range(N_CHIP):
        _, a, b = _unpack(g_in[2 * j], [(D,), conv_dw_w.shape, ffn_dw_w.shape])
        cw_parts.append(a)
        fw_parts.append(b)
    conv_dw_w_full = jnp.concatenate(cw_parts, axis=2)
    ffn_dw_w_full = jnp.concatenate(fw_parts, axis=2)
    cond = jnp.concatenate([c8, c_ctx[None], jnp.zeros((16 - N_DEV - 1, D), F32)], axis=0)

    b_sh = lax.dynamic_slice(b_ada, (0, chip * NA), (2, NA)).reshape(2, 1, NA)
    m_sh = adaln_fwd("adaln_fwd", cond, w_ada, b_sh)
    m_dev = allgather8("ag_mod", m_sh.reshape(2 * 16, NA)).reshape(N_DEV, 2, 16, NA)
    m_all = jnp.concatenate([m_dev[2 * j] for j in range(N_CHIP)], axis=-1)
    mods = []
    for l in range(2):
        lat = lax.dynamic_slice(m_all[l], (dev, 0), (1, N_CHIP * NA))[0]
        cx = m_all[l, N_DEV]
        mods.append({nm: jnp.stack([lat[k * D:(k + 1) * D], cx[k * D:(k + 1) * D]], 0)[:, None, :]
                     for k, nm in enumerate(MOD_NAMES)})

    first, rest = ("w_in", "conv_pw"), ("w_out", "ffn_up", "ffn_down")
    wb = [{}, {}]
    have, flying_w = {}, {}

    def start_gather(tag, l, names, after):
        s_sem, r_sem, bufs, tok = allgather_layer_start(f"ag_{tag}_start", [wb[l][nm] for nm in names], l, after)
        flying_w[(l, names[0])] = (tag, l, names, bufs, s_sem, r_sem)
        return tok

    def land_gather(key, after):
        tag, l, names, bufs, s_sem, r_sem = flying_w.pop(key)
        landed = allgather_layer_wait(f"ag_{tag}_wait", bufs, s_sem, r_sem, after, l)
        have.update(zip([(l, nm) for nm in names], forward_halves(f"ag_{tag}_fwd", list(landed), l)))

    for nm in first:
        wb[0][nm] = to_bf16_block(f"to_bf16_{nm}_0", W[nm], chip_arr, 0)
    tok_first = start_gather("w0a", 0, first, m_all)
    later = [(l, nm) for l in range(2) for nm in BIG if nm not in wb[l]]
    casts = lax.optimization_barrier(tuple(
        to_bf16_block(f"to_bf16_{nm}_{l}", W[nm], chip_arr, l, after=tok_first) for l, nm in later))
    for (l, nm), cast in zip(later, casts):
        wb[l][nm] = cast
    land_gather((0, first[0]), casts[0])
    mods[0] = {**mods[0], "sc1": mods[0]["sc1"] + start_gather("w0b", 0, rest, have[(0, first[0])])[0, 0]}

    def wts(l, name, after):
        tok = None
        if (l, name) not in have:
            if l == 0:
                land_gather((0, rest[0]), after)
                tok = start_gather("w1", 1, BIG, have[(0, rest[0])])[0, 0]
            else:
                land_gather((1, BIG[0]), after)
        return have[(l, name)], tok

    sp = dict(norm1_g=norm1_g, norm2_g=norm2_g, ret_decay=ret_decay, ret_gn_g=ret_gn_g, conv_dw_w=conv_dw_w_full,
              conv_dw_b=conv_dw_b, conv_ln_g=conv_ln_g, conv_ln_b=conv_ln_b, na_rpb=na_rpb, ffn_dw_w=ffn_dw_w_full,
              ffn_dw_b=ffn_dw_b, final_g=final_g)
    flights = []

    def grads_ready(l, names, gb):
        tag = f"{l}_{names[0]}"
        from_sib = exchange_rows(f"rs_exchange_{tag}", [gb[nm] for nm in names], l)
        part = [add_rows(f"rs_add_{nm}_{l}", gb[nm], r, c_arr, l) for nm, r in zip(names, from_sib)]
        lands = [own_row(f"rs_own_{nm}_{l}", p, chip_arr) for nm, p in zip(names, part)]
        s_sem, r_sem, part, lands, tok = scatter_slices_start(f"rs_scatter_{tag}_start", part, lands)
        flights.append((l, names, tag, part, lands, (s_sem, r_sem)))
        return tok[0, 0]

    loss_l, gx, gb, gss, dms, dfg, tok_last = local_step(
        cfg, x[0], ctx[0], loss_target[0], mods, wts, sp, grads_ready)
    loss = lax.psum(loss_l, ("x", "y", "c"))

    delta, new_m, new_v = {}, {}, {}
    bigs = ("w_ada",) + BIG

    def adamw_big(nm):
        shp = W[nm].shape
        v2 = lambda a: a.reshape(-1, shp[-1])
        d_, m_, v_, *g_ = adamw(f"adamw_{nm}", v2(W[nm]), v2(gfull[nm]), v2(Mo[nm]), v2(Vo[nm]), emit_g=nm in BIG)
        delta[nm], new_m[nm], new_v[nm] = d_.reshape(shp), m_.reshape(shp), v_.reshape(shp)
        if g_:
            gfull[nm] = g_[0].reshape(shp)

    gfull, fin, after = {}, {}, gx
    for names in GRAD_GROUPS:
        for l, _, tag, part, lands, sems in sorted([f for f in flights if f[1] == names], key=lambda f: -f[0]):
            landed = scatter_slices_wait(f"rs_scatter_{tag}_wait", part, lands, *sems, after)
            for nm, p in zip(names, landed):
                fin[nm] = sum_rows_into(f"rs_sum_{nm}_{l}", p, c_arr, l, fin.get(nm))
        for nm, gfin in zip(names, share_rows(f"rs_share_{names[0]}", [fin[nm] for nm in names])):
            gfull[nm] = gfin.reshape(W[nm].shape)
            adamw_big(nm)
        after = delta[names[-1]]

    dmseg = jnp.stack([jnp.stack([jnp.concatenate([dms[l][nm][r, 0] for nm in MOD_NAMES]) for r in range(2)])
                       for l in range(2)])
    dmseg, _ = lax.optimization_barrier((dmseg, fin[GRAD_GROUPS[-1][-1]]))
    gsm = dict(
        norm1_g=jnp.stack([gss[l]["norm1_g"][0] for l in range(2)]),
        ret_decay=jnp.stack([gss[l]["lam"] * jax.nn.sigmoid(-ret_decay[l]) for l in range(2)]),
        ret_gn_g=jnp.stack([gss[l]["ret_gn_g"][0] for l in range(2)]),
        conv_dw_w=jnp.stack([gss[l]["conv_dw_w"][:cfg.CK] for l in range(2)]),
        conv_dw_b=jnp.stack([gss[l]["conv_dw_b"][0] for l in range(2)]),
        conv_ln_g=jnp.stack([gss[l]["conv_ln_g"][0] for l in range(2)]),
        conv_ln_b=jnp.stack([gss[l]["conv_ln_b"][0] for l in range(2)]),
        na_rpb=jnp.stack([gss[l]["na_rpb"] for l in range(2)]),
        norm2_g=jnp.stack([gss[l]["norm2_g"][0] for l in range(2)]),
        ffn_dw_w=jnp.stack([gss[l]["ffn_dw_w"][:, :3].transpose(1, 0, 2).reshape(3, 2 * cfg.DFF) for l in range(2)]),
        ffn_dw_b=jnp.stack([gss[l]["ffn_dw_b"].reshape(-1) for l in range(2)]),
        final_g=dfg)
    snames = list(gsm)
    sshapes = [dmseg.shape] + [gsm[nm].shape for nm in snames]
    packed = _pack([dmseg] + [gsm[nm] for nm in snames])
    g_all = allgather8("ag_small_grads", packed).reshape(N_DEV, packed.shape[0], LANE)
    summed = sum_leading("sum_small_grads", g_all)
    dm_sum, *gsum = _unpack(summed, sshapes)
    gfull.update(zip(snames, gsum))
    ndm = int(np.prod(dmseg.shape))
    dm_all = g_all[:, :ndm // LANE].reshape(N_DEV, 2, 2, 6 * D)
    gfull["b_ada"] = sum_leading("sum_b_ada", dm_all.transpose(0, 2, 1, 3).reshape(2 * N_DEV, 2 * 6 * D // LANE, LANE)
                                 ).reshape(2, 6 * D)

    dm16 = jnp.concatenate([dm_all[:, :, 0].transpose(1, 0, 2), dm_sum[:, 1][:, None],
                            jnp.zeros((2, 16 - N_DEV - 1, 6 * D), F32)], axis=1)
    dm16 = lax.dynamic_slice(dm16, (0, 0, chip * NA), (2, 16, NA))
    gfull["w_ada"], ds16 = adaln_bwd("adaln_bwd", cond, w_ada, dm16)
    ds_all = allgather8("ag_dsilu", ds16[8:16]).reshape(N_DEV, 8, D)[0::2, 0:1]
    gfull["c_ctx"] = cctx_grad("cctx_grad", ds_all, c_ctx[None])[0]
    gfull["conv_dw_w"] = lax.dynamic_slice(gfull["conv_dw_w"], (0, 0, chip * ncw), (2, cfg.CK, ncw))
    gfull["ffn_dw_w"] = lax.dynamic_slice(gfull["ffn_dw_w"], (0, 0, chip * nfw), (2, 3, nfw))

    adamw_big("w_ada")
    smalls = [nm for nm in order if nm not in bigs]
    shapes = [W[nm].shape for nm in smalls]
    d_, m_, v_ = adamw("adamw_small", _pack([W[nm] for nm in smalls]), _pack([gfull[nm] for nm in smalls]),
                       _pack([Mo[nm] for nm in smalls]), _pack([Vo[nm] for nm in smalls]))
    for nm, a, b, e in zip(smalls, _unpack(d_, shapes), _unpack(m_, shapes), _unpack(v_, shapes)):
        delta[nm], new_m[nm], new_v[nm] = a, b, e
    return (loss, gx[None], *[gfull[nm] for nm in order], *[delta[nm] for nm in order],
            *[new_m[nm] for nm in order], *[new_v[nm] for nm in order])
```

```python
import collections
import functools

import numpy as np
import jax
import jax.numpy as jnp
from jax import lax
from jax.experimental import pallas as pl
from jax.experimental.pallas import tpu as pltpu

F32 = jnp.float32
BF16 = jnp.bfloat16
EPS = 1e-6
ROPE_BASE = 10000.0
NEG = -1e30
LANE = 128
VMEM_LIMIT = 56 * 1024 * 1024

ADAM_LR, ADAM_B1, ADAM_B2, ADAM_EPS, ADAM_WD, ADAM_STEP = 0.001, 0.9, 0.999, 1e-08, 0.01, 10

Cfg = collections.namedtuple(
    "Cfg", "D T TC GW RH RDK RDV CW CK NH NDH NAR NAC DFF TB")


def make_cfg(D=2048, T=4096, TC=256, RH=4, CW=512, NH=4, DFF=5632):
    return Cfg(D=D, T=T, TC=TC, GW=64, RH=RH, RDK=128, RDV=256, CW=CW, CK=31, NH=NH, NDH=128,
               NAR=8, NAC=16, DFF=DFF, TB=256)


def _offsets(cfg):
    sizes = [cfg.RH * cfg.RDK, cfg.RH * cfg.RDK, cfg.RH * cfg.RDV, cfg.RH * cfg.RDV, cfg.CW, cfg.CW,
             cfg.NH * cfg.NDH, cfg.NH * cfg.NDH, cfg.NH * cfg.NDH]
    offs = [0]
    for s in sizes:
        offs.append(offs[-1] + s)
    return dict(zip(["lq", "lk", "lv", "lg", "la", "lb", "nq", "nk", "nv", "end"], offs))


def _pc(body, **kw):
    return pl.pallas_call(body, **kw)


def _cp(sem=None):
    return pltpu.CompilerParams(dimension_semantics=sem, vmem_limit_bytes=VMEM_LIMIT)


def _dot(a, b, ca, cb):
    return lax.dot_general(a, b, (((ca,), (cb,)), ((), ())), preferred_element_type=F32)


def dot_nn(a, b):
    return _dot(a, b, 1, 0)


def dot_nt(a, b):
    return _dot(a, b, 1, 1)


def dot_tn(a, b):
    return _dot(a, b, 0, 0)


def _sigmoid(x):
    return 1.0 / (1.0 + jnp.exp(-x))


def _silu(x):
    return x * _sigmoid(x)


def _dsilu(x):
    s = _sigmoid(x)
    return s * (1.0 + x * (1.0 - s))


def matmul(name, a, b, *, contract, grid, a_spec, b_spec, out_shape, out_spec, nk, into=None):
    dot = {"nn": dot_nn, "nt": dot_nt, "tn": dot_tn}[contract]
    direct = nk > 1 and out_shape.dtype == F32
    kax = len(grid) - 1

    def body(a_ref, b_ref, *rest):
        o_ref, *scr = rest[1:] if into is not None else rest
        p = dot(a_ref[...].astype(BF16), b_ref[...].astype(BF16))
        if nk == 1:
            o_ref[...] = p.astype(o_ref.dtype)
            return
        acc = o_ref if direct else scr[0]
        k = pl.program_id(kax)

        @pl.when(k == 0)
        def _():
            acc[...] = p

        @pl.when(k > 0)
        def _():
            acc[...] += p

        if not direct:
            @pl.when(k == nk - 1)
            def _():
                o_ref[...] = acc[...].astype(o_ref.dtype)

    scratch = []
    if nk > 1 and not direct:
        blk = [s for s in out_spec.block_shape if s is not None]
        scratch = [pltpu.VMEM(tuple(blk), F32)]
    sem = ("parallel",) * kax + (("arbitrary",) if nk > 1 else ("parallel",))
    in_specs, args, alias = [a_spec, b_spec], (a, b), {}
    if into is not None:
        in_specs, args, alias = in_specs + [pl.BlockSpec(memory_space=pl.ANY)], (a, b, into), {2: 0}
    return _pc(body, name=name, grid=grid, in_specs=in_specs, out_specs=out_spec, out_shape=out_shape,
               scratch_shapes=scratch, input_output_aliases=alias, compiler_params=_cp(sem))(*args)


_WG_ROWS = 1024


def wgrad(cfg, name, a, dc, a_spec, dc_spec, out_shape, out_spec, ntiles, into):
    T, TC = cfg.T, cfg.TC
    tml = min(_WG_ROWS, T)
    nl = T // tml

    def body(al_ref, ac_ref, dl_ref, dcx_ref, *rest):
        o_ref, acc = rest[-2:]
        m = pl.program_id(1)

        @pl.when(m == 0)
        def _():
            acc[...] = dot_tn(al_ref[...], dl_ref[...])

        @pl.when(jnp.logical_and(m > 0, m < nl))
        def _():
            acc[...] += dot_tn(al_ref[...], dl_ref[...])

        @pl.when(m == nl)
        def _():
            o_ref[...] = (acc[...] + dot_tn(ac_ref[...], dcx_ref[...])).astype(o_ref.dtype)

    lat = lambda m: jnp.minimum(m, nl - 1)
    ctx = lambda m: T // TC
    in_specs = [a_spec(tml, lat), a_spec(TC, ctx), dc_spec(tml, lat), dc_spec(TC, ctx)]
    args, alias = (a, a, dc, dc), {}
    if into is not None:
        in_specs, args, alias = in_specs + [pl.BlockSpec(memory_space=pl.ANY)], args + (into,), {4: 0}
    blk = tuple(s for s in out_spec.block_shape if s is not None)
    return _pc(body, name=name, grid=(ntiles, nl + 1), in_specs=in_specs, out_specs=out_spec, out_shape=out_shape,
               scratch_shapes=[pltpu.VMEM(blk, F32)], input_output_aliases=alias,
               compiler_params=_cp(("parallel", "arbitrary")))(*args)


def mm_rowsharded(name, a, w4, l, out_dtype, tn):
    pieces = a if isinstance(a, tuple) else (a,)
    L = pieces[0].shape[0]
    nch, _, Kb, N = w4.shape
    tm = _tm(L, 8)
    assert all(p.shape[1] % Kb == 0 for p in pieces) and sum(p.shape[1] for p in pieces) == nch * Kb

    def body(*refs):
        w_ref, o_ref = refs[-2:]
        acc, j = None, 0
        for a_ref in refs[:-2]:
            for b in range(a_ref.shape[1] // Kb):
                p = dot_nn(a_ref[:, b * Kb:(b + 1) * Kb], w_ref[j])
                acc = p if acc is None else acc + p
                j += 1
        o_ref[...] = acc.astype(o_ref.dtype)

    return _pc(body, name=name, grid=(N // tn, L // tm),
               in_specs=[pl.BlockSpec((tm, p.shape[1]), lambda n, m: (m, 0)) for p in pieces]
               + [pl.BlockSpec((nch, None, Kb, tn), lambda n, m: (0, l, 0, n))],
               out_specs=pl.BlockSpec((tm, tn), lambda n, m: (m, n)),
               out_shape=jax.ShapeDtypeStruct((L, N), out_dtype),
               compiler_params=_cp(("parallel", "parallel")))(*pieces, w4)


def _region(cfg):
    nlat = cfg.T // cfg.TB
    return lambda i: jnp.minimum(i // nlat, 1)


def norm_mod_fwd(cfg, name, x, ng, sc, sh):
    L, D = x.shape
    TB = cfg.TB
    reg = _region(cfg)

    def body(x_ref, ng_ref, sc_ref, sh_ref, h_ref):
        xv = x_ref[...]
        r = lax.rsqrt(jnp.mean(xv * xv, axis=-1, keepdims=True) + EPS)
        n = xv * r * ng_ref[...]
        h_ref[...] = (n * (1.0 + sc_ref[...]) + sh_ref[...]).astype(h_ref.dtype)

    row = pl.BlockSpec((TB, D), lambda i: (i, 0))
    vec = pl.BlockSpec((1, D), lambda i: (0, 0))
    rvec = pl.BlockSpec((None, 1, D), lambda i: (reg(i), 0, 0))
    return _pc(body, name=name, grid=(L // TB,), in_specs=[row, vec, rvec, rvec], out_specs=row,
               out_shape=jax.ShapeDtypeStruct((L, D), BF16), compiler_params=_cp(("parallel",)))(x, ng, sc, sh)


def norm_mod_bwd(cfg, name, dh, x, ng, sc, dx_in, latent_only=False):
    L, D = x.shape
    TB = cfg.TB
    nlat = cfg.T // TB
    reg = _region(cfg)

    def body(dh_ref, x_ref, ng_ref, sc_ref, dxi_ref, dx_ref, dsc_ref, dsh_ref, dng_ref):
        i = pl.program_id(0)
        xv = x_ref[...]
        r = lax.rsqrt(jnp.mean(xv * xv, axis=-1, keepdims=True) + EPS)
        xh = xv * r
        g = ng_ref[...]
        n = xh * g
        dh = dh_ref[...]
        dn = dh * (1.0 + sc_ref[...])
        dxh = dn * g
        dx = r * (dxh - xh * jnp.mean(dxh * xh, axis=-1, keepdims=True))
        if latent_only:
            @pl.when(i < nlat)
            def _():
                dx_ref[...] = dxi_ref[...] + dx
        else:
            dx_ref[...] = dxi_ref[...] + dx
        s_sh = jnp.sum(dh, axis=0, keepdims=True)
        s_sc = jnp.sum(dh * n, axis=0, keepdims=True)
        s_ng = jnp.sum(dn * xh, axis=0, keepdims=True)
        first = jnp.logical_or(i == 0, i == nlat)

        @pl.when(first)
        def _():
            dsh_ref[...] = s_sh
            dsc_ref[...] = s_sc

        @pl.when(jnp.logical_not(first))
        def _():
            dsh_ref[...] += s_sh
            dsc_ref[...] += s_sc

        @pl.when(i == 0)
        def _():
            dng_ref[...] = s_ng

        @pl.when(i > 0)
        def _():
            dng_ref[...] += s_ng

    row = pl.BlockSpec((TB, D), lambda i: (i, 0))
    vec = pl.BlockSpec((1, D), lambda i: (0, 0))
    rvec = pl.BlockSpec((None, 1, D), lambda i: (reg(i), 0, 0))
    dxs = pl.BlockSpec((TB, D), lambda i: (jnp.minimum(i, nlat - 1), 0)) if latent_only else row
    return _pc(body, name=name, grid=(L // TB,), in_specs=[row, row, vec, rvec, row],
               out_specs=[dxs, rvec, rvec, vec],
               out_shape=[jax.ShapeDtypeStruct((cfg.T if latent_only else L, D), F32),
                          jax.ShapeDtypeStruct((2, 1, D), F32),
                          jax.ShapeDtypeStruct((2, 1, D), F32), jax.ShapeDtypeStruct((1, D), F32)],
               compiler_params=_cp(("arbitrary",)))(dh, x, ng, sc, dx_in)


def resid_fwd(cfg, name, x, y, g):
    L, D = x.shape
    TB = cfg.TB
    reg = _region(cfg)

    def body(x_ref, y_ref, g_ref, o_ref):
        o_ref[...] = x_ref[...] + g_ref[...] * y_ref[...]

    row = pl.BlockSpec((TB, D), lambda i: (i, 0))
    rvec = pl.BlockSpec((None, 1, D), lambda i: (reg(i), 0, 0))
    return _pc(body, name=name, grid=(L // TB,), in_specs=[row, row, rvec], out_specs=row,
               out_shape=jax.ShapeDtypeStruct((L, D), F32), compiler_params=_cp(("parallel",)))(x, y, g)


def resid_norm_fwd(cfg, name, x, y, g, ng, sc, sh):
    L, D = x.shape
    TB = cfg.TB
    reg = _region(cfg)

    def body(x_ref, y_ref, g_ref, ng_ref, sc_ref, sh_ref, xo_ref, h_ref):
        xv = x_ref[...] + g_ref[...] * y_ref[...]
        xo_ref[...] = xv
        r = lax.rsqrt(jnp.mean(xv * xv, axis=-1, keepdims=True) + EPS)
        n = xv * r * ng_ref[...]
        h_ref[...] = (n * (1.0 + sc_ref[...]) + sh_ref[...]).astype(h_ref.dtype)

    row = pl.BlockSpec((TB, D), lambda i: (i, 0))
    vec = pl.BlockSpec((1, D), lambda i: (0, 0))
    rvec = pl.BlockSpec((None, 1, D), lambda i: (reg(i), 0, 0))
    return _pc(body, name=name, grid=(L // TB,), in_specs=[row, row, rvec, vec, rvec, rvec], out_specs=[row, row],
               out_shape=[jax.ShapeDtypeStruct((L, D), F32), jax.ShapeDtypeStruct((L, D), BF16)],
               compiler_params=_cp(("parallel",)))(x, y, g, ng, sc, sh)


def resid_bwd(cfg, name, dxo, y, g):
    L, D = y.shape
    TB = cfg.TB
    nlat = cfg.T // TB
    reg = _region(cfg)

    def body(d_ref, y_ref, g_ref, dy_ref, dg_ref):
        i = pl.program_id(0)
        d = d_ref[...]
        dy_ref[...] = (d * g_ref[...]).astype(dy_ref.dtype)
        s = jnp.sum(d * y_ref[...], axis=0, keepdims=True)
        first = jnp.logical_or(i == 0, i == nlat)

        @pl.when(first)
        def _():
            dg_ref[...] = s

        @pl.when(jnp.logical_not(first))
        def _():
            dg_ref[...] += s

    row = pl.BlockSpec((TB, D), lambda i: (i, 0))
    rvec = pl.BlockSpec((None, 1, D), lambda i: (reg(i), 0, 0))
    return _pc(body, name=name, grid=(L // TB,), in_specs=[row, row, rvec], out_specs=[row, rvec],
               out_shape=[jax.ShapeDtypeStruct((L, D), BF16), jax.ShapeDtypeStruct((2, 1, D), F32)],
               compiler_params=_cp(("arbitrary",)))(dxo, y, g)


def final_loss(cfg, name, x, fg, tgt):
    L, D = x.shape
    TB = cfg.TB
    nlat = cfg.T // TB

    def body(x_ref, fg_ref, t_ref, ls_ref, dx_ref, dg_ref):
        i = pl.program_id(0)

        @pl.when(i == 0)
        def _():
            ls_ref[...] = jnp.zeros_like(ls_ref)
            dg_ref[...] = jnp.zeros_like(dg_ref)

        @pl.when(i < nlat)
        def _():
            xv = x_ref[...]
            r = lax.rsqrt(jnp.mean(xv * xv, axis=-1, keepdims=True) + EPS)
            xh = xv * r
            g = fg_ref[...]
            e = xh * g - t_ref[...]
            ls_ref[...] += 0.5 * jnp.sum(e * e) / D
            dy = e / D
            dg_ref[...] += jnp.sum(dy * xh, axis=0, keepdims=True)
            dxh = dy * g
            dx_ref[...] = r * (dxh - xh * jnp.mean(dxh * xh, axis=-1, keepdims=True))

        @pl.when(i >= nlat)
        def _():
            dx_ref[...] = jnp.zeros_like(dx_ref)

    row = pl.BlockSpec((TB, D), lambda i: (i, 0))
    trow = pl.BlockSpec((TB, D), lambda i: (jnp.minimum(i, nlat - 1), 0))
    vec = pl.BlockSpec((1, D), lambda i: (0, 0))
    return _pc(body, name=name, grid=(L // TB,), in_specs=[row, vec, trow],
               out_specs=[pl.BlockSpec((1, LANE), lambda i: (0, 0)), row, vec],
               out_shape=[jax.ShapeDtypeStruct((1, LANE), F32), jax.ShapeDtypeStruct((L, D), F32),
                          jax.ShapeDtypeStruct((1, D), F32)],
               compiler_params=_cp(("arbitrary",)))(x, fg, tgt)


def rope_tables(cfg):
    half = cfg.RDK // 2
    nf = half // 2
    pos = np.arange(cfg.T)
    row = (pos // cfg.GW).astype(np.float32)
    col = (pos % cfg.GW).astype(np.float32)
    inv = jnp.asarray(ROPE_BASE, F32) ** (-jnp.arange(nf, dtype=F32) / nf)
    ar = jnp.asarray(row)[:, None] * inv[None, :]
    ac = jnp.asarray(col)[:, None] * inv[None, :]
    cos = jnp.concatenate([jnp.cos(ar), jnp.cos(ar), jnp.cos(ac), jnp.cos(ac)], axis=1)
    sin = jnp.concatenate([-jnp.sin(ar), jnp.sin(ar), -jnp.sin(ac), jnp.sin(ac)], axis=1)
    cos = jnp.concatenate([cos, jnp.ones((cfg.TC, cfg.RDK), F32)], axis=0)
    sin = jnp.concatenate([sin, jnp.zeros((cfg.TC, cfg.RDK), F32)], axis=0)
    return cos, sin


def _rb(cfg):
    rb = (cfg.T + cfg.TC) // 4
    assert rb % 16 == 0
    return rb


def _swap32(t):
    lane = lax.broadcasted_iota(jnp.int32, t.shape, 1)
    return jnp.where((lane % 64) < 32, pltpu.roll(t, 96, 1), pltpu.roll(t, 32, 1))


def rope_fwd(cfg, name, P, cos, sin):
    L = P.shape[0]
    TB = _rb(cfg)
    off = _offsets(cfg)
    cq, ck = off["lq"] // LANE, off["lk"] // LANE
    scale = cfg.RDK ** -0.5

    def body(q_ref, k_ref, c_ref, s_ref, qo_ref, ko_ref):
        c = c_ref[...]
        s = s_ref[...]
        q = q_ref[...]
        k = k_ref[...]
        qo_ref[...] = (q * c + _swap32(q) * s) * scale
        ko_ref[...] = k * c + _swap32(k) * s

    tab = pl.BlockSpec((TB, LANE), lambda i, h: (i, 0))
    out = pl.BlockSpec((TB, LANE), lambda i, h: (i, h))
    shp = jax.ShapeDtypeStruct((L, cfg.RH * cfg.RDK), F32)
    return _pc(body, name=name, grid=(L // TB, cfg.RH),
               in_specs=[pl.BlockSpec((TB, LANE), lambda i, h: (i, cq + h)),
                         pl.BlockSpec((TB, LANE), lambda i, h: (i, ck + h)), tab, tab],
               out_specs=[out, out], out_shape=[shp, shp],
               compiler_params=_cp(("parallel", "parallel")))(P, P, cos, sin)


def rope_bwd(cfg, name, dq2, dk2, cos, sin):
    L, W = dq2[0].shape
    TB = _rb(cfg)
    scale = cfg.RDK ** -0.5

    def body(dqf_ref, dqb_ref, dkf_ref, dkb_ref, c_ref, s_ref, qo_ref, ko_ref):
        c = c_ref[...]
        s = s_ref[...]
        dq = dqf_ref[...] + dqb_ref[...]
        dk = dkf_ref[...] + dkb_ref[...]
        qo_ref[...] = ((dq * c - _swap32(dq) * s) * scale).astype(qo_ref.dtype)
        ko_ref[...] = (dk * c - _swap32(dk) * s).astype(ko_ref.dtype)

    tab = pl.BlockSpec((TB, LANE), lambda i, h: (i, 0))
    blk = pl.BlockSpec((TB, LANE), lambda i, h: (i, h))
    shp = jax.ShapeDtypeStruct((L, W), BF16)
    return _pc(body, name=name, grid=(L // TB, cfg.RH), in_specs=[blk, blk, blk, blk, tab, tab],
               out_specs=[blk, blk], out_shape=[shp, shp],
               compiler_params=_cp(("parallel", "parallel")))(*dq2, *dk2, cos, sin)


def _ret_chunk_map(cfg):
    C = cfg.RDK
    n = (cfg.T + cfg.TC) // C
    nlat, nctx = cfg.T // C, cfg.TC // C

    def chunk(d, s):
        if d == 0:
            return jnp.where(s < nctx, nlat + s, s - nctx)
        return n - 1 - s

    return n, chunk


def _ret_decay_terms(d, lam, C):
    ii = lax.broadcasted_iota(jnp.int32, (C, C), 0)
    jj = lax.broadcasted_iota(jnp.int32, (C, C), 1)
    diff = (ii - jj if d == 0 else jj - ii).astype(F32)
    dpos = jnp.maximum(diff, 0.0)
    Dm = jnp.where(diff >= 0, jnp.exp(dpos * lam), 0.0)
    ic = lax.broadcasted_iota(jnp.int32, (C, 1), 0).astype(F32)
    cxi = ic + 1.0 if d == 0 else C - ic
    cze = C - 1.0 - ic if d == 0 else ic
    xi = jnp.exp(cxi * lam)
    ze = jnp.exp(cze * lam)
    g = jnp.exp(jnp.full((1, 1), C, F32) * lam)
    return dpos, Dm, cxi, cze, xi, ze, g


def retention_fwd(cfg, name, qr, kr, P, lam):
    L = P.shape[0]
    C, DV, RH = cfg.RDK, cfg.RDV, cfg.RH
    n, chunk = _ret_chunk_map(cfg)

    def body(lam_ref, qf_ref, qb_ref, kf_ref, kb_ref, vf_ref, vb_ref, of_ref, ob_ref, st_ref, S):
        s = pl.program_id(0)

        @pl.when(s == 0)
        def _():
            S[...] = jnp.zeros_like(S)

        for d, (q_ref, k_ref, v_ref, o_ref) in enumerate(((qf_ref, kf_ref, vf_ref, of_ref),
                                                          (qb_ref, kb_ref, vb_ref, ob_ref))):
            for h in range(RH):
                _, Dm, _, _, xi, ze, g = _ret_decay_terms(d, lam_ref[d, h], C)
                k = k_ref[:, h * C:(h + 1) * C]
                qb = q_ref[:, h * C:(h + 1) * C].astype(BF16)
                kb = k.astype(BF16)
                vb = v_ref[:, h * DV:(h + 1) * DV].astype(BF16)
                Sv = S[d, h]
                st_ref[d, h] = Sv
                A = dot_nt(qb, kb) * Dm
                o_ref[:, h * DV:(h + 1) * DV] = dot_nn(A.astype(BF16), vb) + dot_nn(qb, Sv.astype(BF16)) * xi
                S[d, h] = Sv * g + dot_tn((k * ze).astype(BF16), vb)

    def spec(w, col, d):
        return pl.BlockSpec((C, w), lambda s: (chunk(d, s), col))

    W, WV = RH * C, RH * DV
    return _pc(body, name=name, grid=(n,),
               in_specs=[pl.BlockSpec(memory_space=pltpu.SMEM), spec(W, 0, 0), spec(W, 0, 1), spec(W, 0, 0),
                         spec(W, 0, 1), spec(WV, 1, 0), spec(WV, 1, 1)],
               out_specs=[spec(WV, 0, 0), spec(WV, 0, 1),
                          pl.BlockSpec((2, RH, None, C, DV), lambda s: (0, 0, s, 0, 0))],
               out_shape=[jax.ShapeDtypeStruct((L, WV), F32), jax.ShapeDtypeStruct((L, WV), F32),
                          jax.ShapeDtypeStruct((2, RH, n, C, DV), F32)],
               scratch_shapes=[pltpu.VMEM((2, RH, C, DV), F32)],
               compiler_params=_cp(("arbitrary",)))(lam, qr, qr, kr, kr, P, P)


def retention_bwd(cfg, name, qr, kr, P, lam, st, do):
    L = P.shape[0]
    C, DV, RH = cfg.RDK, cfg.RDV, cfg.RH
    n, chunk = _ret_chunk_map(cfg)

    def body(lam_ref, qf_ref, qb_ref, kf_ref, kb_ref, vf_ref, vb_ref, st_ref, dof_ref, dob_ref,
             dqf_ref, dqb_ref, dkf_ref, dkb_ref, dvf_ref, dvb_ref, dl_ref, dS):
        si = pl.program_id(0)

        @pl.when(si == 0)
        def _():
            dS[...] = jnp.zeros_like(dS)
            dl_ref[...] = jnp.zeros_like(dl_ref)

        dirs = ((qf_ref, kf_ref, vf_ref, dof_ref, dqf_ref, dkf_ref, dvf_ref),
                (qb_ref, kb_ref, vb_ref, dob_ref, dqb_ref, dkb_ref, dvb_ref))
        for d, (q_ref, k_ref, v_ref, do_ref, dq_ref, dk_ref, dv_ref) in enumerate(dirs):
            for h in range(RH):
                dpos, Dm, cxi, cze, xi, ze, g = _ret_decay_terms(d, lam_ref[d, h], C)
                hk = slice(h * C, (h + 1) * C)
                hv = slice(h * DV, (h + 1) * DV)
                k = k_ref[:, hk]
                do = do_ref[:, hv]
                qb = q_ref[:, hk].astype(BF16)
                kb = k.astype(BF16)
                vb = v_ref[:, hv].astype(BF16)
                dob = do.astype(BF16)
                Sn = st_ref[d, h]
                Snb = Sn.astype(BF16)
                dSn = dS[d, h]
                dSb = dSn.astype(BF16)
                A = dot_nt(qb, kb) * Dm
                dA = dot_nt(dob, vb)
                dQK = (dA * Dm).astype(BF16)
                kzb = (k * ze).astype(BF16)
                dv_ref[:, hv] = dot_tn(A.astype(BF16), dob) + dot_nn(kzb, dSb)
                dkz = dot_nt(vb, dSb)
                doxb = (do * xi).astype(BF16)
                dq_ref[:, hk] = dot_nn(dQK, kb) + dot_nt(doxb, Snb)
                dk_ref[:, hk] = dot_tn(dQK, qb) + dkz * ze
                QS = dot_nn(qb, Snb)
                t = (jnp.sum(dA * A * dpos) + jnp.sum(do * QS * (cxi * xi)) + jnp.sum(k * dkz * (cze * ze)))
                t4 = jnp.sum(dSn * Sn, axis=0, keepdims=True)
                t4 = jnp.sum(t4 * (g * C), axis=1, keepdims=True)
                dl_ref[d, h] += t + t4
                dS[d, h] = g * dSn + dot_tn(qb, doxb)

    def spec(w, col, d):
        return pl.BlockSpec((C, w), lambda si: (chunk(d, n - 1 - si), col))

    W, WV = RH * C, RH * DV
    return _pc(body, name=name, grid=(n,),
               in_specs=[pl.BlockSpec(memory_space=pltpu.SMEM), spec(W, 0, 0), spec(W, 0, 1), spec(W, 0, 0),
                         spec(W, 0, 1), spec(WV, 1, 0), spec(WV, 1, 1),
                         pl.BlockSpec((2, RH, None, C, DV), lambda si: (0, 0, n - 1 - si, 0, 0)),
                         spec(WV, 0, 0), spec(WV, 0, 1)],
               out_specs=[spec(W, 0, 0), spec(W, 0, 1), spec(W, 0, 0), spec(W, 0, 1), spec(WV, 0, 0), spec(WV, 0, 1),
                          pl.BlockSpec((2, RH, 8, LANE), lambda si: (0, 0, 0, 0))],
               out_shape=[jax.ShapeDtypeStruct((L, W), F32)] * 4 + [jax.ShapeDtypeStruct((L, WV), F32)] * 2
               + [jax.ShapeDtypeStruct((2, RH, 8, LANE), F32)],
               scratch_shapes=[pltpu.VMEM((2, RH, C, DV), F32)],
               compiler_params=_cp(("arbitrary",)))(lam, qr, qr, kr, kr, P, P, st, do, do)


def add_cast(cfg, name, a, b):
    L, W = a.shape
    TB = _rb(cfg)

    def body(a_ref, b_ref, o_ref):
        o_ref[...] = (a_ref[...] + b_ref[...]).astype(o_ref.dtype)

    spec = pl.BlockSpec((TB, W), lambda i: (i, 0))
    return _pc(body, name=name, grid=(L // TB,), in_specs=[spec, spec], out_specs=spec,
               out_shape=jax.ShapeDtypeStruct((L, W), BF16), compiler_params=_cp(("parallel",)))(a, b)


def ggn_fwd(cfg, name, o2, P, gn_g):
    L = P.shape[0]
    TB, DV, RH = _rb(cfg), cfg.RDV, cfg.RH
    gc0 = _offsets(cfg)["lg"] // DV

    def body(of_ref, ob_ref, gate_ref, g_ref, out_ref):
        o = of_ref[...] + ob_ref[...]
        mu = jnp.mean(o, axis=-1, keepdims=True)
        xc = o - mu
        var = jnp.mean(xc * xc, axis=-1, keepdims=True)
        y = xc * lax.rsqrt(var + EPS) * g_ref[...]
        out_ref[...] = (y * _silu(gate_ref[...])).astype(out_ref.dtype)

    blk = pl.BlockSpec((TB, DV), lambda i, h: (i, h))
    return _pc(body, name=name, grid=(L // TB, RH),
               in_specs=[blk, blk, pl.BlockSpec((TB, DV), lambda i, h: (i, gc0 + h)),
                         pl.BlockSpec((1, DV), lambda i, h: (0, h))],
               out_specs=blk, out_shape=jax.ShapeDtypeStruct((L, RH * DV), BF16),
               compiler_params=_cp(("parallel", "parallel")))(*o2, P, gn_g)


def ggn_bwd(cfg, name, dout, o2, P, gn_g, col0):
    L = P.shape[0]
    TB, DV, RH = _rb(cfg), cfg.RDV, cfg.RH
    gc0 = _offsets(cfg)["lg"] // DV

    def body(d_ref, of_ref, ob_ref, gate_ref, g_ref, do_ref, dgate_ref, dg_ref):
        i = pl.program_id(1)
        o = of_ref[...] + ob_ref[...]
        mu = jnp.mean(o, axis=-1, keepdims=True)
        xc = o - mu
        var = jnp.mean(xc * xc, axis=-1, keepdims=True)
        r = lax.rsqrt(var + EPS)
        y = xc * r
        g = g_ref[...]
        gate = gate_ref[...]
        d = d_ref[...]
        dgate_ref[...] = (d * (y * g) * _dsilu(gate)).astype(dgate_ref.dtype)
        dyg = d * _silu(gate)
        s = jnp.sum(dyg * y, axis=0, keepdims=True)

        @pl.when(i == 0)
        def _():
            dg_ref[...] = s

        @pl.when(i > 0)
        def _():
            dg_ref[...] += s

        dy = dyg * g
        do_ref[...] = r * (dy - jnp.mean(dy, axis=-1, keepdims=True)
                           - y * jnp.mean(dy * y, axis=-1, keepdims=True))

    blk = pl.BlockSpec((TB, DV), lambda h, i: (i, h))
    return _pc(body, name=name, grid=(RH, L // TB),
               in_specs=[pl.BlockSpec((TB, DV), lambda h, i: (i, col0 + h)), blk, blk,
                         pl.BlockSpec((TB, DV), lambda h, i: (i, gc0 + h)),
                         pl.BlockSpec((1, DV), lambda h, i: (0, h))],
               out_specs=[blk, blk, pl.BlockSpec((1, DV), lambda h, i: (0, h))],
               out_shape=[jax.ShapeDtypeStruct((L, RH * DV), F32), jax.ShapeDtypeStruct((L, RH * DV), BF16),
                          jax.ShapeDtypeStruct((1, RH * DV), F32)],
               compiler_params=_cp(("parallel", "arbitrary")))(dout, *o2, P, gn_g)


def cast_cols(cfg, name, src, col0, ncols, width):
    L = src.shape[0]
    TB = _rb(cfg)

    def body(s_ref, o_ref):
        o_ref[...] = s_ref[...].astype(o_ref.dtype)

    spec = pl.BlockSpec((TB, width), lambda i, j: (i, col0 + j))
    return _pc(body, name=name, grid=(L // TB, ncols), in_specs=[spec],
               out_specs=pl.BlockSpec((TB, width), lambda i, j: (i, j)),
               out_shape=jax.ShapeDtypeStruct((L, ncols * width), BF16),
               compiler_params=_cp(("parallel", "parallel")))(src)


_CPAD = 16


def _conv_windows(cfg):
    T, TC, TB = cfg.T, cfg.TC, cfg.TB
    assert TC % TB == 0 and T % TB == 0 and cfg.CK // 2 < _CPAD
    return T // TB, [(T + j * TB, T + _CPAD + j * TB) for j in range(TC // TB)]


def _fill_padded(cfg, pb, get):
    T, TC, TB = cfg.T, cfg.TC, cfg.TB
    z = jnp.zeros((_CPAD, LANE), F32)
    pb[0:_CPAD, :] = z
    pb[_CPAD + T:2 * _CPAD + T, :] = z
    pb[2 * _CPAD + T + TC:3 * _CPAD + T + TC, :] = z

    def fill(i, c):
        r0 = pl.multiple_of(i * TB, TB)
        pb[pl.ds(r0 + _CPAD, TB), :] = get(r0)
        return c

    lax.fori_loop(0, T // TB, fill, 0)
    for j in range(TC // TB):
        pb[2 * _CPAD + T + j * TB:2 * _CPAD + T + (j + 1) * TB, :] = get(T + j * TB)


def _taps(win, TB):
    W = TB + 2 * _CPAD
    rot = {0: win}

    def tap(k):
        a, b = divmod(k + 1, 8)
        if b not in rot:
            rot[b] = pltpu.roll(win, W - b, 0)
        return rot[b][8 * a:8 * a + TB, :]

    return tap


def glu_dwconv_fwd(cfg, name, P, w, b):
    L = P.shape[0]
    T, TC, TB, K = cfg.T, cfg.TC, cfg.TB, cfg.CK
    off = _offsets(cfg)
    ca, cb = off["la"] // LANE, off["lb"] // LANE
    nlat, ctx_tiles = _conv_windows(cfg)
    PBL = 3 * _CPAD + T + TC

    def body(a_ref, b_ref, w_ref, bias_ref, y_ref, pb):
        _fill_padded(cfg, pb, lambda r0: a_ref[pl.ds(r0, TB), :] * _sigmoid(b_ref[pl.ds(r0, TB), :]))
        wv = w_ref[...]
        bias = bias_ref[...]

        def tile(win):
            tap = _taps(win, TB)
            acc = jnp.zeros((TB, LANE), F32) + bias
            for k in range(K):
                acc = acc + wv[k:k + 1, :] * tap(k)
            return acc

        def lat(i, c):
            r0 = pl.multiple_of(i * TB, TB)
            y_ref[pl.ds(r0, TB), :] = tile(pb[pl.ds(r0, TB + 2 * _CPAD), :])
            return c

        lax.fori_loop(0, nlat, lat, 0)
        for r0, w0 in ctx_tiles:
            y_ref[r0:r0 + TB, :] = tile(pb[w0:w0 + TB + 2 * _CPAD, :])

    return _pc(body, name=name, grid=(cfg.CW // LANE,),
               in_specs=[pl.BlockSpec((L, LANE), lambda j: (0, ca + j)),
                         pl.BlockSpec((L, LANE), lambda j: (0, cb + j)),
                         pl.BlockSpec((32, LANE), lambda j: (0, j)),
                         pl.BlockSpec((1, LANE), lambda j: (0, j))],
               out_specs=pl.BlockSpec((L, LANE), lambda j: (0, j)),
               out_shape=jax.ShapeDtypeStruct((L, cfg.CW), F32),
               scratch_shapes=[pltpu.VMEM((PBL, LANE), F32)],
               compiler_params=_cp(("parallel",)))(P, P, w, b)


def glu_dwconv_bwd(cfg, name, P, w, dy):
    L = P.shape[0]
    T, TC, TB, K = cfg.T, cfg.TC, cfg.TB, cfg.CK
    off = _offsets(cfg)
    ca, cb = off["la"] // LANE, off["lb"] // LANE
    nlat, ctx_tiles = _conv_windows(cfg)
    PBL = 3 * _CPAD + T + TC

    def body(a_ref, b_ref, w_ref, dy_ref, da_ref, db_ref, dw_ref, dbias_ref, pbu, pbd):
        _fill_padded(cfg, pbu, lambda r0: a_ref[pl.ds(r0, TB), :] * _sigmoid(b_ref[pl.ds(r0, TB), :]))
        _fill_padded(cfg, pbd, lambda r0: dy_ref[pl.ds(r0, TB), :])
        wv = w_ref[...]
        dw_ref[...] = jnp.zeros_like(dw_ref)
        dbias_ref[...] = jnp.zeros_like(dbias_ref)

        def tile(r0, winu, wind):
            tapu = _taps(winu, TB)
            tapd = _taps(wind, TB)
            dyt = dy_ref[pl.ds(r0, TB), :]
            du = jnp.zeros((TB, LANE), F32)
            for k in range(K):
                du = du + wv[k:k + 1, :] * tapd(K - 1 - k)
                dw_ref[k:k + 1, :] += jnp.sum(dyt * tapu(k), axis=0, keepdims=True)
            dbias_ref[...] += jnp.sum(dyt, axis=0, keepdims=True)
            a = a_ref[pl.ds(r0, TB), :]
            sg = _sigmoid(b_ref[pl.ds(r0, TB), :])
            da_ref[pl.ds(r0, TB), :] = (du * sg).astype(da_ref.dtype)
            db_ref[pl.ds(r0, TB), :] = (du * a * sg * (1.0 - sg)).astype(db_ref.dtype)

        def lat(i, c):
            r0 = pl.multiple_of(i * TB, TB)
            tile(r0, pbu[pl.ds(r0, TB + 2 * _CPAD), :], pbd[pl.ds(r0, TB + 2 * _CPAD), :])
            return c

        lax.fori_loop(0, nlat, lat, 0)
        for r0, w0 in ctx_tiles:
            tile(r0, pbu[w0:w0 + TB + 2 * _CPAD, :], pbd[w0:w0 + TB + 2 * _CPAD, :])

    col = pl.BlockSpec((L, LANE), lambda j: (0, j))
    return _pc(body, name=name, grid=(cfg.CW // LANE,),
               in_specs=[pl.BlockSpec((L, LANE), lambda j: (0, ca + j)),
                         pl.BlockSpec((L, LANE), lambda j: (0, cb + j)),
                         pl.BlockSpec((32, LANE), lambda j: (0, j)), col],
               out_specs=[col, col, pl.BlockSpec((32, LANE), lambda j: (0, j)),
                          pl.BlockSpec((1, LANE), lambda j: (0, j))],
               out_shape=[jax.ShapeDtypeStruct((L, cfg.CW), BF16), jax.ShapeDtypeStruct((L, cfg.CW), BF16),
                          jax.ShapeDtypeStruct((32, cfg.CW), F32), jax.ShapeDtypeStruct((1, cfg.CW), F32)],
               scratch_shapes=[pltpu.VMEM((PBL, LANE), F32), pltpu.VMEM((PBL, LANE), F32)],
               compiler_params=_cp(("parallel",)))(P, P, w, dy)


def ln_silu_fwd(cfg, name, y, g, b):
    L, W = y.shape
    TB = cfg.TB

    def body(y_ref, g_ref, b_ref, o_ref):
        yv = y_ref[...]
        mu = jnp.mean(yv, axis=-1, keepdims=True)
        xc = yv - mu
        var = jnp.mean(xc * xc, axis=-1, keepdims=True)
        z = xc * lax.rsqrt(var + EPS) * g_ref[...] + b_ref[...]
        o_ref[...] = _silu(z).astype(o_ref.dtype)

    row = pl.BlockSpec((TB, W), lambda i: (i, 0))
    vec = pl.BlockSpec((1, W), lambda i: (0, 0))
    return _pc(body, name=name, grid=(L // TB,), in_specs=[row, vec, vec], out_specs=row,
               out_shape=jax.ShapeDtypeStruct((L, W), BF16), compiler_params=_cp(("parallel",)))(y, g, b)


def ln_silu_bwd(cfg, name, dact, y, g, b):
    L, W = y.shape
    TB = cfg.TB

    def body(d_ref, y_ref, g_ref, b_ref, dy_ref, dg_ref, db_ref):
        i = pl.program_id(0)
        yv = y_ref[...]
        mu = jnp.mean(yv, axis=-1, keepdims=True)
        xc = yv - mu
        var = jnp.mean(xc * xc, axis=-1, keepdims=True)
        r = lax.rsqrt(var + EPS)
        yh = xc * r
        g = g_ref[...]
        z = yh * g + b_ref[...]
        dz = d_ref[...] * _dsilu(z)
        sg = jnp.sum(dz * yh, axis=0, keepdims=True)
        sb = jnp.sum(dz, axis=0, keepdims=True)

        @pl.when(i == 0)
        def _():
            dg_ref[...] = sg
            db_ref[...] = sb

        @pl.when(i > 0)
        def _():
            dg_ref[...] += sg
            db_ref[...] += sb

        dh = dz * g
        dy_ref[...] = r * (dh - jnp.mean(dh, axis=-1, keepdims=True)
                           - yh * jnp.mean(dh * yh, axis=-1, keepdims=True))

    row = pl.BlockSpec((TB, W), lambda i: (i, 0))
    vec = pl.BlockSpec((1, W), lambda i: (0, 0))
    return _pc(body, name=name, grid=(L // TB,), in_specs=[row, row, vec, vec], out_specs=[row, vec, vec],
               out_shape=[jax.ShapeDtypeStruct((L, W), F32), jax.ShapeDtypeStruct((1, W), F32),
                          jax.ShapeDtypeStruct((1, W), F32)],
               compiler_params=_cp(("arbitrary",)))(dact, y, g, b)


def _na_geometry(cfg):
    R = cfg.T // cfg.GW
    nb = R // cfg.NAR
    assert nb >= 3 and cfg.GW == 64 and cfg.NAR == 8
    ks = [int(np.clip(8 * b - 4, 0, R - 16)) for b in range(nb)]
    return R, nb, ks


_NTAB = 18


def _split3(x):
    hi = x.astype(BF16)
    r = x - hi.astype(F32)
    mid = r.astype(BF16)
    lo = (r - mid.astype(F32)).astype(BF16)
    return hi, mid, lo


def _na_col_onehot(cfg):
    GW, NAC = cfg.GW, cfg.NAC
    qc = np.arange(GW)[:, None]
    kc = np.arange(GW)[None, :]
    cs = np.clip(qc - NAC // 2, 0, GW - NAC)
    vcol = (kc >= cs) & (kc < cs + NAC)
    dd = np.clip(kc - qc + NAC - 1, 0, 2 * NAC - 2)
    oh = (np.arange(LANE)[:, None, None] == dd[None]).astype(np.float32)
    z = np.zeros_like(oh)
    oda = np.concatenate([oh, z], axis=2).reshape(LANE, GW * LANE)
    odb = np.concatenate([z, oh], axis=2).reshape(LANE, GW * LANE)
    cm = np.where(np.concatenate([vcol, vcol], axis=1), 0.0, NEG).astype(np.float32).reshape(1, GW * LANE)
    return oda, odb, cm


def na_tables(cfg, name, rpb):
    NH, GW = cfg.NH, cfg.GW
    na = rpb.shape[1]
    oda, odb, cm = _na_col_onehot(cfg)
    rp = jnp.zeros((NH, _NTAB + 1, LANE), F32).at[:, 1:1 + na, :rpb.shape[2]].set(rpb.astype(F32))
    r0 = rp[:, :_NTAB].reshape(NH * _NTAB, LANE)
    r1 = rp[:, 1:].reshape(NH * _NTAB, LANE)
    a = np.arange(_NTAB) - 1
    rm0 = np.where((a >= 0) & (a < na), 0.0, NEG).astype(np.float32)
    rm1 = np.where((a + 1 >= 0) & (a + 1 < na), 0.0, NEG).astype(np.float32)
    half = (np.arange(GW * LANE) % LANE >= GW)[None, :]
    rmask = np.where(half, np.tile(rm1, NH)[:, None], np.tile(rm0, NH)[:, None]).astype(np.float32)
    tn = 2048
    rows = NH * _NTAB

    def body(r0_ref, r1_ref, a_ref, b_ref, cm_ref, rm_ref, o_ref):
        acc = cm_ref[...] + rm_ref[...]
        for t in _split3(r0_ref[...]):
            acc = acc + dot_nn(t, a_ref[...])
        for t in _split3(r1_ref[...]):
            acc = acc + dot_nn(t, b_ref[...])
        o_ref[...] = acc

    rs = pl.BlockSpec((rows, LANE), lambda n: (0, 0))
    out = _pc(body, name=name, grid=(GW * LANE // tn,),
              in_specs=[rs, rs, pl.BlockSpec((LANE, tn), lambda n: (0, n)), pl.BlockSpec((LANE, tn), lambda n: (0, n)),
                        pl.BlockSpec((1, tn), lambda n: (0, n)), pl.BlockSpec((rows, tn), lambda n: (0, n))],
              out_specs=pl.BlockSpec((rows, tn), lambda n: (0, n)),
              out_shape=jax.ShapeDtypeStruct((rows, GW * LANE), F32),
              compiler_params=_cp(("parallel",)))(r0, r1, jnp.asarray(oda, BF16), jnp.asarray(odb, BF16),
                                                  jnp.asarray(cm), jnp.asarray(rmask))
    return out.reshape(NH, _NTAB, GW, LANE)


def _na_tiles(cfg, b):
    R, nb, _ = _na_geometry(cfg)
    NAR = cfg.NAR
    ksb = jnp.clip(8 * b - 4, 0, R - 16)
    for i in range(8):
        qr = 8 * b + i
        ws = jnp.clip(qr - NAR // 2, 0, R - NAR)
        for J in range(8):
            kr0 = ksb + 2 * J
            row = jnp.clip(kr0 - qr + NAR - 1, -1, _NTAB - 2) + 1
            v0 = jnp.logical_and(kr0 >= ws, kr0 < ws + NAR)
            v1 = jnp.logical_and(kr0 + 1 >= ws, kr0 + 1 < ws + NAR)
            yield i, J, row, v0, v1


def _na_fill_bias(cfg, tab_ref, bias, b):
    GW = cfg.GW
    first = lax.broadcasted_iota(jnp.int32, (GW, LANE), 1) < GW
    for i, J, row, v0, v1 in _na_tiles(cfg, b):
        ok = jnp.where(first, v0.astype(jnp.int32), v1.astype(jnp.int32))
        bias[i * GW:(i + 1) * GW, J * LANE:(J + 1) * LANE] = jnp.where(ok > 0, tab_ref[row], NEG)


def _na_specs(cfg):
    R, nb, ks = _na_geometry(cfg)
    off = _offsets(cfg)
    TQ = 8 * cfg.GW
    KP = 4 * cfg.GW
    ks4 = [k // 4 for k in ks]
    lat_blocks = cfg.T // KP

    def ks4_of(b):
        return jnp.clip(2 * b - 1, 0, R // 4 - 4)

    assert all(int(np.clip(2 * b - 1, 0, R // 4 - 4)) == ks4[b] for b in range(nb))
    assert cfg.TC == KP

    def col(nm):
        c0 = off[nm] // LANE
        q = pl.BlockSpec((TQ, LANE), lambda h, b: (b, c0 + h))
        parts = [pl.BlockSpec((KP, LANE), functools.partial(lambda h, b, t: (ks4_of(b) + t, c0 + h), t=t))
                 for t in range(4)]
        ctx = pl.BlockSpec((KP, LANE), lambda h, b: (lat_blocks, c0 + h))
        return q, parts, ctx

    return nb, TQ, KP, ks4_of, col


def na_fwd(cfg, name, P, tab):
    nb, TQ, KP, ks4_of, col = _na_specs(cfg)
    NH = cfg.NH
    scale = cfg.NDH ** -0.5
    qs, _, _ = col("nq")
    _, kparts, kctx = col("nk")
    _, vparts, vctx = col("nv")

    def body(q_ref, k0, k1, k2, k3, kc_ref, v0, v1, v2, v3, vc_ref, tab_ref, o_ref, lse_ref, bias_ref):
        _na_fill_bias(cfg, tab_ref, bias_ref, pl.program_id(1))
        q = (q_ref[...] * scale).astype(BF16)
        kl = jnp.concatenate([k0[...], k1[...], k2[...], k3[...]], axis=0).astype(BF16)
        vl = jnp.concatenate([v0[...], v1[...], v2[...], v3[...]], axis=0).astype(BF16)
        kc = kc_ref[...].astype(BF16)
        vc = vc_ref[...].astype(BF16)
        sl = dot_nt(q, kl) + bias_ref[...]
        sc = dot_nt(q, kc)
        m = jnp.maximum(jnp.max(sl, axis=-1, keepdims=True), jnp.max(sc, axis=-1, keepdims=True))
        pl_ = jnp.exp(sl - m)
        pc = jnp.exp(sc - m)
        den = jnp.sum(pl_, axis=-1, keepdims=True) + jnp.sum(pc, axis=-1, keepdims=True)
        o = dot_nn(pl_.astype(BF16), vl) + dot_nn(pc.astype(BF16), vc)
        o_ref[...] = o / den
        lse_ref[...] = m + jnp.log(den)

    return _pc(body, name=name, grid=(NH, nb),
               in_specs=[qs, *kparts, kctx, *vparts, vctx,
                         pl.BlockSpec((None, _NTAB, cfg.GW, LANE), lambda h, b: (h, 0, 0, 0))],
               out_specs=[pl.BlockSpec((TQ, LANE), lambda h, b: (b, h)),
                          pl.BlockSpec((None, TQ, 1), lambda h, b: (h, b, 0))],
               out_shape=[jax.ShapeDtypeStruct((cfg.T, NH * LANE), F32),
                          jax.ShapeDtypeStruct((NH, cfg.T, 1), F32)],
               scratch_shapes=[pltpu.VMEM((TQ, 4 * KP), F32)],
               compiler_params=_cp(("parallel", "parallel")))(P, *([P] * 5), *([P] * 5), tab)


def na_bwd(cfg, name, P, tab, o, lse, dmix, dcol0):
    nb, TQ, KP, ks4_of, col = _na_specs(cfg)
    NH, GW = cfg.NH, cfg.GW
    L = P.shape[0]
    scale = cfg.NDH ** -0.5
    qs, _, _ = col("nq")
    _, kparts, kctx = col("nk")
    _, vparts, vctx = col("nv")

    def body(q_ref, k0, k1, k2, k3, kc_ref, v0, v1, v2, v3, vc_ref, tab_ref, o_ref, lse_ref, do_ref,
             dq_ref, dk_ref, dv_ref, dtab_ref, bias_ref):
        b = pl.program_id(1)

        @pl.when(b == 0)
        def _():
            dk_ref[...] = jnp.zeros_like(dk_ref)
            dv_ref[...] = jnp.zeros_like(dv_ref)
            dtab_ref[...] = jnp.zeros_like(dtab_ref)

        _na_fill_bias(cfg, tab_ref, bias_ref, b)

        q = (q_ref[...] * scale).astype(BF16)
        kl = jnp.concatenate([k0[...], k1[...], k2[...], k3[...]], axis=0).astype(BF16)
        vl = jnp.concatenate([v0[...], v1[...], v2[...], v3[...]], axis=0).astype(BF16)
        kc = kc_ref[...].astype(BF16)
        vc = vc_ref[...].astype(BF16)
        lse = lse_ref[...]
        do = do_ref[...]
        dob = do.astype(BF16)
        p_l = jnp.exp(dot_nt(q, kl) + bias_ref[...] - lse)
        p_c = jnp.exp(dot_nt(q, kc) - lse)
        delta = jnp.sum(do * o_ref[...], axis=-1, keepdims=True)
        ds_l = p_l * (dot_nt(dob, vl) - delta)
        ds_c = p_c * (dot_nt(dob, vc) - delta)
        dslb = ds_l.astype(BF16)
        dscb = ds_c.astype(BF16)
        dq_ref[...] = ((dot_nn(dslb, kl) + dot_nn(dscb, kc)) * scale).astype(dq_ref.dtype)
        r0 = pl.multiple_of(ks4_of(b) * KP, KP)
        dk_ref[pl.ds(r0, 4 * KP), :] += dot_tn(dslb, q)
        dv_ref[pl.ds(r0, 4 * KP), :] += dot_tn(p_l.astype(BF16), dob)
        dk_ref[cfg.T:cfg.T + KP, :] += dot_tn(dscb, q)
        dv_ref[cfg.T:cfg.T + KP, :] += dot_tn(p_c.astype(BF16), dob)
        bias_ref[...] = ds_l
        for i, J, row, _, _ in _na_tiles(cfg, b):
            dtab_ref[row] += bias_ref[i * GW:(i + 1) * GW, J * LANE:(J + 1) * LANE]

    full = pl.BlockSpec((L, LANE), lambda h, b: (0, h))
    tabs = pl.BlockSpec((None, _NTAB, GW, LANE), lambda h, b: (h, 0, 0, 0))
    return _pc(body, name=name, grid=(NH, nb),
               in_specs=[qs, *kparts, kctx, *vparts, vctx, tabs,
                         pl.BlockSpec((TQ, LANE), lambda h, b: (b, h)),
                         pl.BlockSpec((None, TQ, 1), lambda h, b: (h, b, 0)),
                         pl.BlockSpec((TQ, LANE), lambda h, b: (b, dcol0 + h))],
               out_specs=[pl.BlockSpec((TQ, LANE), lambda h, b: (b, h)), full, full, tabs],
               out_shape=[jax.ShapeDtypeStruct((cfg.T, NH * LANE), BF16),
                          jax.ShapeDtypeStruct((L, NH * LANE), F32), jax.ShapeDtypeStruct((L, NH * LANE), F32),
                          jax.ShapeDtypeStruct((NH, _NTAB, GW, LANE), F32)],
               scratch_shapes=[pltpu.VMEM((TQ, 4 * KP), F32)],
               compiler_params=_cp(("parallel", "arbitrary")))(
                   P, *([P] * 5), *([P] * 5), tab, o, lse, dmix)


def na_ctx_fwd(cfg, name, P):
    off = _offsets(cfg)
    TC, NH = cfg.TC, cfg.NH
    rb = cfg.T // TC
    scale = cfg.NDH ** -0.5

    def body(q_ref, k_ref, v_ref, o_ref, lse_ref):
        q = (q_ref[...] * scale).astype(BF16)
        s = dot_nt(q, k_ref[...].astype(BF16))
        m = jnp.max(s, axis=-1, keepdims=True)
        p = jnp.exp(s - m)
        den = jnp.sum(p, axis=-1, keepdims=True)
        o_ref[...] = dot_nn(p.astype(BF16), v_ref[...].astype(BF16)) / den
        lse_ref[...] = m + jnp.log(den)

    spec = lambda nm: pl.BlockSpec((TC, LANE), functools.partial(lambda h, c0: (rb, c0 + h), c0=off[nm] // LANE))
    return _pc(body, name=name, grid=(NH,), in_specs=[spec("nq"), spec("nk"), spec("nv")],
               out_specs=[pl.BlockSpec((TC, LANE), lambda h: (0, h)), pl.BlockSpec((None, TC, 1), lambda h: (h, 0, 0))],
               out_shape=[jax.ShapeDtypeStruct((TC, NH * LANE), F32), jax.ShapeDtypeStruct((NH, TC, 1), F32)],
               compiler_params=_cp(("parallel",)))(P, P, P)


def na_ctx_bwd(cfg, name, P, o, lse, dmix, dcol0, dk_in, dv_in):
    off = _offsets(cfg)
    TC, NH = cfg.TC, cfg.NH
    rb = cfg.T // TC
    scale = cfg.NDH ** -0.5

    def body(q_ref, k_ref, v_ref, o_ref, lse_ref, do_ref, dki_ref, dvi_ref, dq_ref, dk_ref, dv_ref):
        q = (q_ref[...] * scale).astype(BF16)
        kb = k_ref[...].astype(BF16)
        vb = v_ref[...].astype(BF16)
        do = do_ref[...]
        dob = do.astype(BF16)
        p = jnp.exp(dot_nt(q, kb) - lse_ref[...])
        delta = jnp.sum(do * o_ref[...], axis=-1, keepdims=True)
        ds = (p * (dot_nt(dob, vb) - delta)).astype(BF16)
        dq_ref[...] = (dot_nn(ds, kb) * scale).astype(dq_ref.dtype)
        dk_ref[...] = (dki_ref[...] + dot_tn(ds, q)).astype(dk_ref.dtype)
        dv_ref[...] = (dvi_ref[...] + dot_tn(p.astype(BF16), dob)).astype(dv_ref.dtype)

    spec = lambda nm: pl.BlockSpec((TC, LANE), functools.partial(lambda h, c0: (rb, c0 + h), c0=off[nm] // LANE))
    hb = pl.BlockSpec((TC, LANE), lambda h: (0, h))
    ctxrow = pl.BlockSpec((TC, LANE), lambda h: (rb, h))
    shp = jax.ShapeDtypeStruct((TC, NH * LANE), BF16)
    return _pc(body, name=name, grid=(NH,),
               in_specs=[spec("nq"), spec("nk"), spec("nv"), hb, pl.BlockSpec((None, TC, 1), lambda h: (h, 0, 0)),
                         pl.BlockSpec((TC, LANE), lambda h: (rb, dcol0 + h)), ctxrow, ctxrow],
               out_specs=[hb, hb, hb], out_shape=[shp, shp, shp],
               compiler_params=_cp(("parallel",)))(P, P, P, o, lse, dmix, dk_in, dv_in)


def na_rpb_grad(cfg, name, dtab):
    NH, GW = cfg.NH, cfg.GW
    na, nd = 2 * cfg.NAR - 1, 2 * cfg.NAC - 1
    oda, odb, _ = _na_col_onehot(cfg)
    E = np.concatenate([oda.T, odb.T], axis=1)
    rows = NH * _NTAB

    def body(z_ref, e_ref, o_ref):
        zv = z_ref[...]
        hi = zv.astype(BF16)
        lo = (zv - hi.astype(F32)).astype(BF16)
        e = e_ref[...]
        o_ref[...] = dot_nn(hi, e) + dot_nn(lo, e)

    g = _pc(body, name=name, out_shape=jax.ShapeDtypeStruct((rows, 2 * LANE), F32),
            compiler_params=_cp())(dtab.reshape(rows, GW * LANE), jnp.asarray(E, BF16))
    g = g.reshape(NH, _NTAB, 2, LANE)
    return g[:, 1:1 + na, 0, :nd] + g[:, 0:na, 1, :nd]


def _seq_tiles(cfg):
    T, TC, TB = cfg.T, cfg.TC, cfg.TB
    tiles = []
    for i in range((T + TC) // TB):
        r0 = i * TB
        tiles.append((r0, r0 == 0 or r0 == T, r0 + TB == T or r0 + TB == T + TC))
    return tiles


def _shift3(ref_get, r0, TB, start, end, width):
    cur = ref_get(r0, TB)
    if start or end:
        rowi = lax.broadcasted_iota(jnp.int32, (TB, width), 0)
    up = jnp.where(rowi == 0, 0.0, pltpu.roll(cur, 1, 0)) if start else ref_get(r0 - 1, TB)
    dn = jnp.where(rowi == TB - 1, 0.0, pltpu.roll(cur, TB - 1, 0)) if end else ref_get(r0 + 1, TB)
    return up, cur, dn


def ffn_act_fwd(cfg, name, U2, w, b):
    _, L, DFF = U2.shape
    TB = cfg.TB
    tiles = _seq_tiles(cfg)

    def body(u_ref, w_ref, b_ref, a_ref):
        def plane(p, r0, st, en):
            up, cur, dn = _shift3(lambda r, n: u_ref[p, r:r + n, :], r0, TB, st, en, LANE)
            wv = w_ref[p]
            return wv[0:1, :] * up + wv[1:2, :] * cur + wv[2:3, :] * dn + b_ref[p]

        for r0, st, en in tiles:
            val = plane(0, r0, st, en)
            gate = plane(1, r0, st, en)
            a_ref[r0:r0 + TB, :] = (_silu(gate) * val).astype(a_ref.dtype)

    return _pc(body, name=name, grid=(DFF // LANE,),
               in_specs=[pl.BlockSpec((2, L, LANE), lambda j: (0, 0, j)),
                         pl.BlockSpec((2, 8, LANE), lambda j: (0, 0, j)),
                         pl.BlockSpec((2, 1, LANE), lambda j: (0, 0, j))],
               out_specs=pl.BlockSpec((L, LANE), lambda j: (0, j)),
               out_shape=jax.ShapeDtypeStruct((L, DFF), BF16),
               compiler_params=_cp(("parallel",)))(U2, w, b)


def ffn_act_bwd(cfg, name, U2, w, b, dA):
    _, L, DFF = U2.shape
    TB = cfg.TB
    tiles = _seq_tiles(cfg)

    def body(u_ref, w_ref, b_ref, da_ref, du_ref, dw_ref, db_ref, dbuf):
        dw_ref[...] = jnp.zeros_like(dw_ref)
        db_ref[...] = jnp.zeros_like(db_ref)
        for r0, st, en in tiles:
            shifted = []
            pre = []
            for p in range(2):
                up, cur, dn = _shift3(lambda r, n: u_ref[p, r:r + n, :], r0, TB, st, en, LANE)
                wv = w_ref[p]
                shifted.append((up, cur, dn))
                pre.append(wv[0:1, :] * up + wv[1:2, :] * cur + wv[2:3, :] * dn + b_ref[p])
            val, gate = pre
            da = da_ref[r0:r0 + TB, :]
            dpre = (da * _silu(gate), da * val * _dsilu(gate))
            for p in range(2):
                dbuf[p, r0:r0 + TB, :] = dpre[p]
                for k in range(3):
                    dw_ref[p, k:k + 1, :] += jnp.sum(dpre[p] * shifted[p][k], axis=0, keepdims=True)
                db_ref[p] += jnp.sum(dpre[p], axis=0, keepdims=True)
        for r0, st, en in tiles:
            for p in range(2):
                up, cur, dn = _shift3(lambda r, n: dbuf[p, r:r + n, :], r0, TB, st, en, LANE)
                wv = w_ref[p]
                du_ref[p, r0:r0 + TB, :] = (wv[0:1, :] * dn + wv[1:2, :] * cur + wv[2:3, :] * up).astype(du_ref.dtype)

    blk = pl.BlockSpec((2, L, LANE), lambda j: (0, 0, j))
    wspec = pl.BlockSpec((2, 8, LANE), lambda j: (0, 0, j))
    bspec = pl.BlockSpec((2, 1, LANE), lambda j: (0, 0, j))
    return _pc(body, name=name, grid=(DFF // LANE,),
               in_specs=[blk, wspec, bspec, pl.BlockSpec((L, LANE), lambda j: (0, j))],
               out_specs=[blk, wspec, bspec],
               out_shape=[jax.ShapeDtypeStruct((2, L, DFF), BF16), jax.ShapeDtypeStruct((2, 8, DFF), F32),
                          jax.ShapeDtypeStruct((2, 1, DFF), F32)],
               scratch_shapes=[pltpu.VMEM((2, L, LANE), F32)],
               compiler_params=_cp(("parallel",)))(U2, w, b, dA)


def _tm(L, parts):
    assert L % parts == 0
    return L // parts


def layer_fwd(cfg, l, XS, mod, wts, small, tabs):
    L, D = XS.shape
    off = _offsets(cfg)
    DIN = off["end"]
    tmA = _tm(L, 4)
    sv = {"XS": XS, "W": {}}

    def weight(name, after):
        sv["W"][name], tok = wts(name, after)
        return sv["W"][name], tok

    Win4, _ = weight("w_in", XS)
    nbi = Win4.shape[3]
    h1 = norm_mod_fwd(cfg, f"norm1_fwd_{l}", XS, small["norm1_g"], mod["sc1"], mod["sh1"])
    P = matmul(f"mm_in_{l}", h1, Win4, contract="nn", grid=(4, L // tmA),
               a_spec=pl.BlockSpec((tmA, D), lambda n, m: (m, 0)),
               b_spec=pl.BlockSpec((None, None, D, nbi), lambda n, m: (n, l, 0, 0)),
               out_shape=jax.ShapeDtypeStruct((L, DIN), F32),
               out_spec=pl.BlockSpec((tmA, nbi), lambda n, m: (m, n)), nk=1)
    qr, kr = rope_fwd(cfg, f"rope_fwd_{l}", P, tabs["cos"], tabs["sin"])
    o_f, o_b, st = retention_fwd(cfg, f"ret_fwd_{l}", qr, kr, P, small["lam"])
    o2 = (o_f, o_b)
    ret = ggn_fwd(cfg, f"ggn_fwd_{l}", o2, P, small["ret_gn_g"])
    ycv = glu_dwconv_fwd(cfg, f"dwconv_fwd_{l}", P, small["conv_dw_w"], small["conv_dw_b"])
    act = ln_silu_fwd(cfg, f"ln_silu_fwd_{l}", ycv, small["conv_ln_g"], small["conv_ln_b"])
    Wpw4, _ = weight("conv_pw", act)
    cv = mm_rowsharded(f"mm_pw_{l}", act, Wpw4, l, BF16, cfg.CW)
    bias = na_tables(cfg, f"na_tables_{l}", small["na_rpb"])
    na_l, lse = na_fwd(cfg, f"na_fwd_{l}", P, bias)
    na_c, lse_c = na_ctx_fwd(cfg, f"na_ctx_fwd_{l}", P)
    mix = (ret, cv, jnp.concatenate([na_l, na_c], axis=0).astype(BF16))
    Wout4, tok = weight("w_out", ret)
    Y1 = mm_rowsharded(f"mm_out_{l}", mix, Wout4, l, F32, D)
    XM, h2 = resid_norm_fwd(cfg, f"resid1_norm2_fwd_{l}", XS, Y1, mod["g1"] if tok is None else mod["g1"] + tok,
                            small["norm2_g"], mod["sc2"], mod["sh2"])
    Wup4, _ = weight("ffn_up", h2)
    nbu = Wup4.shape[3]
    tnu = nbu // 2
    U2 = matmul(f"mm_up_{l}", h2, Wup4, contract="nn", grid=(8, L // tmA),
                a_spec=pl.BlockSpec((tmA, D), lambda n, m: (m, 0)),
                b_spec=pl.BlockSpec((None, None, D, tnu), lambda n, m: (n // 2, l, 0, n % 2)),
                out_shape=jax.ShapeDtypeStruct((2, L, cfg.DFF), F32),
                out_spec=pl.BlockSpec((None, tmA, tnu), lambda n, m: (n // 4, m, n % 4)), nk=1)
    A = ffn_act_fwd(cfg, f"ffn_act_fwd_{l}", U2, small["ffn_dw_w"], small["ffn_dw_b"])
    Wdn4, _ = weight("ffn_down", A)
    Y2 = mm_rowsharded(f"mm_down_{l}", A, Wdn4, l, F32, D // 2)
    XO = resid_fwd(cfg, f"resid2_fwd_{l}", XM, Y2, mod["g2"])
    sv.update(h1=h1, P=P, qr=qr, kr=kr, o2=o2, st=st, ycv=ycv, act=act, bias=bias, na_l=na_l, lse=lse,
              na_c=na_c, lse_c=lse_c, mix=mix, Y1=Y1, XM=XM, h2=h2, U2=U2, A=A, Y2=Y2)
    return XO, sv


GRAD_GROUPS = (("ffn_down", "ffn_up"), ("w_out", "conv_pw", "w_in"))


def layer_bwd(cfg, l, dXO, sv, mod, wts, small, tabs, gbuf, ready):
    L, D = dXO.shape
    off = _offsets(cfg)
    DIN = off["end"]
    Win4, Wout4, Wup4, Wdn4, Wpw4 = wts["w_in"], wts["w_out"], wts["ffn_up"], wts["ffn_down"], wts["conv_pw"]
    tmA, tmB = _tm(L, 4), _tm(L, 8)
    depth = Win4.shape[1]
    gb, gs, dm = {}, {}, {}
    P = sv["P"]
    dY2, dm["g2"] = resid_bwd(cfg, f"resid2_bwd_{l}", dXO, sv["Y2"], mod["g2"])
    nbd = Wdn4.shape[2]
    dA = matmul(f"mm_down_da_{l}", dY2, Wdn4, contract="nt", grid=(4, L // tmA),
                a_spec=pl.BlockSpec((tmA, D), lambda j, m: (m, 0)),
                b_spec=pl.BlockSpec((None, None, nbd, D), lambda j, m: (j, l, 0, 0)),
                out_shape=jax.ShapeDtypeStruct((L, cfg.DFF), F32),
                out_spec=pl.BlockSpec((tmA, nbd), lambda j, m: (m, j)), nk=1)
    gb["ffn_down"] = wgrad(cfg, f"mm_down_dw_{l}", sv["A"], dY2,
                           lambda rb, ri: pl.BlockSpec((rb, nbd), lambda j, m: (ri(m), j)),
                           lambda rb, ri: pl.BlockSpec((rb, D), lambda j, m: (ri(m), 0)),
                           jax.ShapeDtypeStruct((depth, 4, nbd, D), BF16),
                           pl.BlockSpec((None, None, nbd, D), lambda j, m: (l, j, 0, 0)), 4, gbuf.get("ffn_down"))
    dU2, dfw, dfb = ffn_act_bwd(cfg, f"ffn_act_bwd_{l}", sv["U2"], small["ffn_dw_w"], small["ffn_dw_b"], dA)
    gs["ffn_dw_w"], gs["ffn_dw_b"] = dfw, dfb
    nbu = Wup4.shape[3]
    tnu = nbu // 2
    dH2 = matmul(f"mm_up_dh_{l}", dU2, Wup4, contract="nt", grid=(L // tmA, 8),
                 a_spec=pl.BlockSpec((None, tmA, tnu), lambda m, n: (n // 4, m, n % 4)),
                 b_spec=pl.BlockSpec((None, None, D, tnu), lambda m, n: (n // 2, l, 0, n % 2)),
                 out_shape=jax.ShapeDtypeStruct((L, D), F32),
                 out_spec=pl.BlockSpec((tmA, D), lambda m, n: (m, 0)), nk=8)
    gb["ffn_up"] = wgrad(cfg, f"mm_up_dw_{l}", sv["h2"], dU2,
                         lambda rb, ri: pl.BlockSpec((rb, D), lambda n, m: (ri(m), 0)),
                         lambda rb, ri: pl.BlockSpec((None, rb, tnu), lambda n, m: (n // 4, ri(m), n % 4)),
                         jax.ShapeDtypeStruct((depth, 4, D, nbu), BF16),
                         pl.BlockSpec((None, None, D, tnu), lambda n, m: (l, n // 2, 0, n % 2)), 8, gbuf.get("ffn_up"))
    dXM, dm["sc2"], dm["sh2"], gs["norm2_g"] = norm_mod_bwd(
        cfg, f"norm2_bwd_{l}", dH2, sv["XM"], small["norm2_g"], mod["sc2"], dXO)
    tok = ready(GRAD_GROUPS[0], gb)
    dY1, dm["g1"] = resid_bwd(cfg, f"resid1_bwd_{l}", dXM, sv["Y1"], mod["g1"] if tok is None else mod["g1"] + tok)
    nbo = Wout4.shape[2]
    dmix = matmul(f"mm_out_dmix_{l}", dY1, Wout4, contract="nt", grid=(4, L // tmA),
                  a_spec=pl.BlockSpec((tmA, D), lambda j, m: (m, 0)),
                  b_spec=pl.BlockSpec((None, None, nbo, D), lambda j, m: (j, l, 0, 0)),
                  out_shape=jax.ShapeDtypeStruct((L, D), F32),
                  out_spec=pl.BlockSpec((tmA, nbo), lambda j, m: (m, j)), nk=1)
    gw, j0 = gbuf.get("w_out"), 0
    for pi, piece in enumerate(sv["mix"]):
        nblk = piece.shape[1] // nbo
        gw = wgrad(cfg, f"mm_out_dw_{l}_{pi}", piece, dY1,
                   lambda rb, ri: pl.BlockSpec((rb, nbo), lambda j, m: (ri(m), j)),
                   lambda rb, ri: pl.BlockSpec((rb, D), lambda j, m: (ri(m), 0)),
                   jax.ShapeDtypeStruct((depth, 4, nbo, D), BF16),
                   pl.BlockSpec((None, None, nbo, D), functools.partial(lambda j, m, j0: (l, j + j0, 0, 0), j0=j0)),
                   nblk, gw)
        j0 += nblk
    gb["w_out"] = gw
    RW = cfg.RH * cfg.RDV
    do, dlg, gs["ret_gn_g"] = ggn_bwd(cfg, f"ggn_bwd_{l}", dmix, sv["o2"], P, small["ret_gn_g"], 0)
    dqf, dqb, dkf, dkb, dvf, dvb, dlam = retention_bwd(
        cfg, f"ret_bwd_{l}", sv["qr"], sv["kr"], P, small["lam"], sv["st"], do)
    gs["lam"] = dlam[:, :, 0, 0]
    dlq, dlk = rope_bwd(cfg, f"rope_bwd_{l}", (dqf, dqb), (dkf, dkb), tabs["cos"], tabs["sin"])
    dlv = add_cast(cfg, f"ret_dv_{l}", dvf, dvb)
    dcv = cast_cols(cfg, f"conv_dcv_{l}", dmix, RW // LANE, cfg.CW // LANE, LANE)
    nbp = Wpw4.shape[2]
    dact = matmul(f"mm_pw_dact_{l}", dcv, Wpw4, contract="nt", grid=(4, L // tmA),
                  a_spec=pl.BlockSpec((tmA, cfg.CW), lambda j, m: (m, 0)),
                  b_spec=pl.BlockSpec((None, None, nbp, cfg.CW), lambda j, m: (j, l, 0, 0)),
                  out_shape=jax.ShapeDtypeStruct((L, cfg.CW), F32),
                  out_spec=pl.BlockSpec((tmA, nbp), lambda j, m: (m, j)), nk=1)
    gb["conv_pw"] = wgrad(cfg, f"mm_pw_dw_{l}", sv["act"], dcv,
                          lambda rb, ri: pl.BlockSpec((rb, nbp), lambda j, m: (ri(m), j)),
                          lambda rb, ri: pl.BlockSpec((rb, cfg.CW), lambda j, m: (ri(m), 0)),
                          jax.ShapeDtypeStruct((depth, 4, nbp, cfg.CW), BF16),
                          pl.BlockSpec((None, None, nbp, cfg.CW), lambda j, m: (l, j, 0, 0)), 4, gbuf.get("conv_pw"))
    dycv, gs["conv_ln_g"], gs["conv_ln_b"] = ln_silu_bwd(
        cfg, f"ln_silu_bwd_{l}", dact, sv["ycv"], small["conv_ln_g"], small["conv_ln_b"])
    dla, dlb, gs["conv_dw_w"], gs["conv_dw_b"] = glu_dwconv_bwd(cfg, f"dwconv_bwd_{l}", P, small["conv_dw_w"], dycv)
    nac0 = (RW + cfg.CW) // LANE
    dnq_l, dnk, dnv, dsb = na_bwd(cfg, f"na_bwd_{l}", P, sv["bias"], sv["na_l"], sv["lse"], dmix, nac0)
    dnq_c, dnk_c, dnv_c = na_ctx_bwd(cfg, f"na_ctx_bwd_{l}", P, sv["na_c"], sv["lse_c"], dmix, nac0, dnk, dnv)
    gs["na_rpb"] = na_rpb_grad(cfg, f"na_rpb_{l}", dsb)
    dnq = jnp.concatenate([dnq_l, dnq_c], axis=0)
    dnk = jnp.concatenate([dnk[:cfg.T].astype(BF16), dnk_c], axis=0)
    dnv = jnp.concatenate([dnv[:cfg.T].astype(BF16), dnv_c], axis=0)
    dP = jnp.concatenate([dlq, dlk, dlv, dlg, dla, dlb, dnq, dnk, dnv], axis=1)
    nbi = Win4.shape[3]
    dH1 = matmul(f"mm_in_dh_{l}", dP, Win4, contract="nt", grid=(L // tmA, 4),
                 a_spec=pl.BlockSpec((tmA, nbi), lambda m, n: (m, n)),
                 b_spec=pl.BlockSpec((None, None, D, nbi), lambda m, n: (n, l, 0, 0)),
                 out_shape=jax.ShapeDtypeStruct((L, D), F32),
                 out_spec=pl.BlockSpec((tmA, D), lambda m, n: (m, 0)), nk=4)
    gb["w_in"] = wgrad(cfg, f"mm_in_dw_{l}", sv["h1"], dP,
                       lambda rb, ri: pl.BlockSpec((rb, D), lambda n, m: (ri(m), 0)),
                       lambda rb, ri: pl.BlockSpec((rb, nbi), lambda n, m: (ri(m), n)),
                       jax.ShapeDtypeStruct((depth, 4, D, nbi), BF16),
                       pl.BlockSpec((None, None, D, nbi), lambda n, m: (l, n, 0, 0)), 4, gbuf.get("w_in"))
    dXS, dm["sc1"], dm["sh1"], gs["norm1_g"] = norm_mod_bwd(
        cfg, f"norm1_bwd_{l}", dH1, sv["XS"], small["norm1_g"], mod["sc1"], dXM, latent_only=(l == 0))
    return dXS, gb, gs, dm, ready(GRAD_GROUPS[1], gb)


def _layer_small(cfg, l, sp):
    DFF = cfg.DFF
    fw = sp["ffn_dw_w"][l].reshape(3, 2, DFF).transpose(1, 0, 2)
    fw = jnp.concatenate([fw, jnp.zeros((2, 5, DFF), F32)], axis=1)
    cw = jnp.concatenate([sp["conv_dw_w"][l], jnp.zeros((32 - cfg.CK, cfg.CW), F32)], axis=0)
    return dict(
        norm1_g=sp["norm1_g"][l][None], norm2_g=sp["norm2_g"][l][None],
        lam=jax.nn.log_sigmoid(sp["ret_decay"][l]), ret_gn_g=sp["ret_gn_g"][l][None],
        conv_dw_w=cw, conv_dw_b=sp["conv_dw_b"][l][None], conv_ln_g=sp["conv_ln_g"][l][None],
        conv_ln_b=sp["conv_ln_b"][l][None], na_rpb=sp["na_rpb"][l],
        ffn_dw_w=fw, ffn_dw_b=sp["ffn_dw_b"][l].reshape(2, 1, DFF))


def local_step(cfg, x, ctx, tgt, mods, wts, sp, grads_ready=lambda l, names, gb: None):
    depth = sp["norm1_g"].shape[0]
    cos, sin = rope_tables(cfg)
    tabs = dict(cos=cos, sin=sin)
    XS = jnp.concatenate([x, ctx], axis=0)
    smalls = [_layer_small(cfg, l, sp) for l in range(depth)]
    saves = []
    for l in range(depth):
        XS, sv = layer_fwd(cfg, l, XS, mods[l], functools.partial(wts, l), smalls[l], tabs)
        saves.append(sv)
    ls, dX, dfg = final_loss(cfg, "final_loss", XS, sp["final_g"][None], tgt)
    gb, gss, dms = {}, [None] * depth, [None] * depth
    token = None
    for l in reversed(range(depth)):
        mod = mods[l] if token is None else {**mods[l], "g2": mods[l]["g2"] + token}
        dX, gb, gss[l], dms[l], token = layer_bwd(cfg, l, dX, saves[l], mod, saves[l]["W"], smalls[l], tabs, gb,
                                                  functools.partial(grads_ready, l))
    return ls[0, 0], dX[:cfg.T], gb, gss, dms, dfg[0], token


MESH = pl.DeviceIdType.MESH
N_DEV = 8
N_CHIP = 4
BIG = ("w_in", "w_out", "ffn_up", "ffn_down", "conv_pw")
_ANY = pl.BlockSpec(memory_space=pl.ANY)


def _place():
    x, y, c = lax.axis_index("x"), lax.axis_index("y"), lax.axis_index("c")
    chips = [(1 - x, y), (x, 1 - y), (1 - x, 1 - y)]
    return x, y, c, chips


def allgather8(name, xs):
    m_per, n = xs.shape

    def body(x_ref, out_ref, send_sems, recv_sems, local_sem):
        x, y, c, chips = _place()
        me, sibling = (x, y, c), (x, y, 1 - c)

        def rows(px, py, pc):
            return out_ref.at[pl.ds((4 * px + 2 * py + pc) * m_per, m_per), :]

        def copy(k, block, to, src=None):
            return pltpu.make_async_remote_copy(
                src_ref=rows(*block) if src is None else src, dst_ref=rows(*block),
                send_sem=send_sems.at[k], recv_sem=recv_sems.at[k], device_id=to, device_id_type=MESH)

        mine = pltpu.make_async_copy(x_ref, rows(*me), local_sem)
        mine.start()
        first = [copy(0, me, sibling, src=x_ref)]
        first += [copy(1 + j, me, (*chip, c), src=x_ref) for j, chip in enumerate(chips)]
        for cp in first:
            cp.start()
        passed = [copy(4 + j, (*chip, c), sibling) for j, chip in enumerate(chips)]
        for j, chip in enumerate(chips):
            copy(1 + j, (*chip, c), me).wait_recv()
            passed[j].start()
        copy(0, sibling, me).wait_recv()
        for j, chip in enumerate(chips):
            copy(4 + j, (*chip, 1 - c), me).wait_recv()
        for cp in first + passed:
            cp.wait_send()
        mine.wait()

    return _pc(body, name=name, out_shape=jax.ShapeDtypeStruct((N_DEV * m_per, n), xs.dtype),
               in_specs=[pl.BlockSpec(memory_space=pltpu.VMEM)], out_specs=pl.BlockSpec(memory_space=pltpu.VMEM),
               scratch_shapes=[pltpu.SemaphoreType.DMA((7,)), pltpu.SemaphoreType.DMA((7,)), pltpu.SemaphoreType.DMA],
               compiler_params=pltpu.CompilerParams(vmem_limit_bytes=VMEM_LIMIT))(xs)


def _wpiece(ref, layer, chip_idx, half):
    rh = ref.shape[2] // 2
    return ref.at[chip_idx, layer, pl.ds(half * rh, rh)]


def _wcopy(ref, layer, chip_idx, half, send_sems, recv_sems, k, to):
    piece = _wpiece(ref, layer, chip_idx, half)
    return pltpu.make_async_remote_copy(src_ref=piece, dst_ref=piece, send_sem=send_sems.at[k],
                                        recv_sem=recv_sems.at[k], device_id=to, device_id_type=MESH)


def _w_ici_sends(outs, layer, send_sems, recv_sems):
    x, y, c, chips = _place()
    return [_wcopy(outs[a], layer, 2 * x + y, c, send_sems, recv_sems, 3 * a + t, (*chip, c))
            for a in range(len(outs)) for t, chip in enumerate(chips)]


def _w_ici_landed(outs, layer, send_sems, recv_sems):
    x, y, c, chips = _place()
    return [_wcopy(outs[a], layer, 2 * chip[0] + chip[1], c, send_sems, recv_sems, 3 * a + t, (x, y, c))
            for a in range(len(outs)) for t, chip in enumerate(chips)]


def _w_forward(outs, layer, send_sems, recv_sems, base):
    x, y, c, chips = _place()
    n = len(outs)
    sends = [_wcopy(outs[a], layer, 2 * chip[0] + chip[1], c, send_sems, recv_sems, base + 3 * a + t, (x, y, 1 - c))
             for a in range(n) for t, chip in enumerate(chips)]
    recvs = [_wcopy(outs[a], layer, 2 * chip[0] + chip[1], 1 - c, send_sems, recv_sems, base + 3 * a + t, (x, y, c))
             for a in range(n) for t, chip in enumerate(chips)]
    return sends, recvs


def allgather_layer(name, bufs, layer):
    n = len(bufs)

    def body(*refs):
        outs = refs[n:2 * n]
        send_sems, recv_sems = refs[2 * n:]
        sent = _w_ici_sends(outs, layer, send_sems, recv_sems)
        for cp in sent:
            cp.start()
        fwd, from_sib = _w_forward(outs, layer, send_sems, recv_sems, 3 * n)
        for landed, fw in zip(_w_ici_landed(outs, layer, send_sems, recv_sems), fwd):
            landed.wait_recv()
            fw.start()
        for cp in from_sib:
            cp.wait_recv()
        for cp in sent + fwd:
            cp.wait_send()

    return _pc(body, name=name, out_shape=[jax.ShapeDtypeStruct(b.shape, b.dtype) for b in bufs],
               in_specs=[_ANY] * n, out_specs=[_ANY] * n, input_output_aliases={a: a for a in range(n)},
               scratch_shapes=[pltpu.SemaphoreType.DMA((6 * n,)), pltpu.SemaphoreType.DMA((6 * n,))])(*bufs)


_HBM = pl.BlockSpec(memory_space=pltpu.HBM)
_SEM = pl.BlockSpec(memory_space=pltpu.SEMAPHORE)
_EFFECT = pltpu.SideEffectType.DATAFLOW_SIDE_EFFECTING


def allgather_layer_start(name, bufs, layer, after):
    n = len(bufs)

    def body(*refs):
        send_sems, recv_sems = refs[n + 1:n + 3]
        outs = refs[n + 3:2 * n + 3]
        token = refs[2 * n + 3]
        for cp in _w_ici_sends(outs, layer, send_sems, recv_sems):
            cp.start()
        token[...] = jnp.zeros_like(token)

    res = _pc(body, name=name,
              out_shape=(pltpu.SemaphoreType.DMA((3 * n,)), pltpu.SemaphoreType.DMA((3 * n,)),
                         *[pltpu.HBM(b.shape, b.dtype) for b in bufs], jax.ShapeDtypeStruct((8, LANE), F32)),
              in_specs=[_HBM] * n + [_ANY],
              out_specs=(_SEM, _SEM, *([_HBM] * n), pl.BlockSpec(memory_space=pltpu.VMEM)),
              input_output_aliases={a: a + 2 for a in range(n)},
              compiler_params=pltpu.CompilerParams(has_side_effects=_EFFECT))(
                  *[pltpu.with_memory_space_constraint(b, pltpu.HBM) for b in bufs], after)
    return res[0], res[1], list(res[2:2 + n]), res[2 + n]


def allgather_layer_wait(name, bufs, send_sems, recv_sems, after, layer):
    n = len(bufs)

    def body(*refs):
        ins = refs[:n]
        send_sems, recv_sems = refs[n:n + 2]
        for cp in _w_ici_sends(ins, layer, send_sems, recv_sems):
            cp.wait_send()
        for cp in _w_ici_landed(ins, layer, send_sems, recv_sems):
            cp.wait_recv()

    return _pc(body, name=name, out_shape=tuple(pltpu.HBM(b.shape, b.dtype) for b in bufs),
               in_specs=[_HBM] * n + [_SEM, _SEM, _ANY], out_specs=tuple([_HBM] * n),
               input_output_aliases={a: a for a in range(n)},
               compiler_params=pltpu.CompilerParams(has_side_effects=_EFFECT))(*bufs, send_sems, recv_sems, after)


def forward_halves(name, bufs, layer):
    n = len(bufs)

    def body(*refs):
        outs = refs[n:2 * n]
        send_sems, recv_sems = refs[2 * n:]
        fwd, from_sib = _w_forward(outs, layer, send_sems, recv_sems, 0)
        for cp in fwd:
            cp.start()
        for cp in from_sib:
            cp.wait_recv()
        for cp in fwd:
            cp.wait_send()

    return _pc(body, name=name, out_shape=[jax.ShapeDtypeStruct(b.shape, b.dtype) for b in bufs],
               in_specs=[_ANY] * n, out_specs=[_ANY] * n, input_output_aliases={a: a for a in range(n)},
               scratch_shapes=[pltpu.SemaphoreType.DMA((3 * n,)), pltpu.SemaphoreType.DMA((3 * n,))])(*bufs)


def exchange_rows(name, grads, layer):
    n = len(grads)

    def body(*refs):
        ins, outs = refs[:n], refs[n:2 * n]
        send_sems, recv_sems = refs[2 * n:]
        x, y, c, _ = _place()
        cps = []
        for a in range(n):
            rh = ins[a].shape[2] // 2
            cps.append(pltpu.make_async_remote_copy(
                src_ref=ins[a].at[layer, pl.ds(0, N_CHIP), pl.ds((1 - c) * rh, rh)], dst_ref=outs[a],
                send_sem=send_sems.at[a], recv_sem=recv_sems.at[a], device_id=(x, y, 1 - c), device_id_type=MESH))
        for cp in cps:
            cp.start()
        for cp in cps:
            cp.wait()

    return _pc(body, name=name,
               out_shape=[jax.ShapeDtypeStruct((N_CHIP, g.shape[2] // 2, g.shape[3]), g.dtype) for g in grads],
               in_specs=[_ANY] * n, out_specs=[_ANY] * n,
               scratch_shapes=[pltpu.SemaphoreType.DMA((n,)), pltpu.SemaphoreType.DMA((n,))])(*grads)


def _scatter_sends(parts, lands, send_sems, recv_sems):
    x, y, c, chips = _place()
    return [pltpu.make_async_remote_copy(
        src_ref=parts[a].at[2 * chip[0] + chip[1]], dst_ref=lands[a].at[2 * x + y], send_sem=send_sems.at[3 * a + t],
        recv_sem=recv_sems.at[3 * a + t], device_id=(*chip, c), device_id_type=MESH)
        for a in range(len(parts)) for t, chip in enumerate(chips)]


def _scatter_landed(lands, send_sems, recv_sems):
    x, y, c, chips = _place()
    return [pltpu.make_async_remote_copy(
        src_ref=lands[a].at[2 * chip[0] + chip[1]], dst_ref=lands[a].at[2 * chip[0] + chip[1]],
        send_sem=send_sems.at[3 * a + t], recv_sem=recv_sems.at[3 * a + t], device_id=(x, y, c), device_id_type=MESH)
        for a in range(len(lands)) for t, chip in enumerate(chips)]


def scatter_slices(name, parts, lands):
    n = len(parts)

    def body(*refs):
        ins, outs = refs[:n], refs[2 * n:3 * n]
        send_sems, recv_sems = refs[3 * n:]
        cps = _scatter_sends(ins, outs, send_sems, recv_sems)
        for cp in cps:
            cp.start()
        for cp in _scatter_landed(outs, send_sems, recv_sems):
            cp.wait_recv()
        for cp in cps:
            cp.wait_send()

    return _pc(body, name=name, out_shape=[jax.ShapeDtypeStruct(p.shape, p.dtype) for p in lands],
               in_specs=[_ANY] * (2 * n), out_specs=[_ANY] * n,
               input_output_aliases={n + a: a for a in range(n)},
               scratch_shapes=[pltpu.SemaphoreType.DMA((3 * n,)), pltpu.SemaphoreType.DMA((3 * n,))])(*parts, *lands)


def scatter_slices_start(name, parts, lands):
    n = len(parts)

    def body(*refs):
        send_sems, recv_sems = refs[2 * n:2 * n + 2]
        p_out, l_out = refs[2 * n + 2:3 * n + 2], refs[3 * n + 2:4 * n + 2]
        token = refs[4 * n + 2]
        for cp in _scatter_sends(p_out, l_out, send_sems, recv_sems):
            cp.start()
        token[...] = jnp.zeros_like(token)

    both = list(parts) + list(lands)
    res = _pc(body, name=name,
              out_shape=(pltpu.SemaphoreType.DMA((3 * n,)), pltpu.SemaphoreType.DMA((3 * n,)),
                         *[pltpu.HBM(b.shape, b.dtype) for b in both], jax.ShapeDtypeStruct((8, LANE), F32)),
              in_specs=[_HBM] * (2 * n),
              out_specs=(_SEM, _SEM, *([_HBM] * (2 * n)), pl.BlockSpec(memory_space=pltpu.VMEM)),
              input_output_aliases={a: a + 2 for a in range(2 * n)},
              compiler_params=pltpu.CompilerParams(has_side_effects=_EFFECT))(
                  *[pltpu.with_memory_space_constraint(b, pltpu.HBM) for b in both])
    return res[0], res[1], list(res[2:2 + n]), list(res[2 + n:2 + 2 * n]), res[2 + 2 * n]


def scatter_slices_wait(name, parts, lands, send_sems, recv_sems, after):
    n = len(parts)

    def body(*refs):
        p_in, l_in = refs[:n], refs[n:2 * n]
        send_sems, recv_sems = refs[2 * n:2 * n + 2]
        for cp in _scatter_sends(p_in, l_in, send_sems, recv_sems):
            cp.wait_send()
        for cp in _scatter_landed(l_in, send_sems, recv_sems):
            cp.wait_recv()

    both = list(parts) + list(lands)
    res = _pc(body, name=name, out_shape=tuple(pltpu.HBM(b.shape, b.dtype) for b in both),
              in_specs=[_HBM] * (2 * n) + [_SEM, _SEM, _ANY], out_specs=tuple([_HBM] * (2 * n)),
              input_output_aliases={a: a for a in range(2 * n)},
              compiler_params=pltpu.CompilerParams(has_side_effects=_EFFECT))(*both, send_sems, recv_sems, after)
    return list(res[n:])


def share_rows(name, bufs):
    n = len(bufs)

    def body(*refs):
        outs = refs[n:2 * n]
        send_sems, recv_sems = refs[2 * n:]
        x, y, c, _ = _place()

        def half(a, h):
            return outs[a].at[pl.ds(0, 2), h]

        cps = [pltpu.make_async_remote_copy(
            src_ref=half(a, c), dst_ref=half(a, c), send_sem=send_sems.at[a], recv_sem=recv_sems.at[a],
            device_id=(x, y, 1 - c), device_id_type=MESH) for a in range(n)]
        for cp in cps:
            cp.start()
        for a in range(n):
            pltpu.make_async_remote_copy(
                src_ref=half(a, 1 - c), dst_ref=half(a, 1 - c), send_sem=send_sems.at[a],
                recv_sem=recv_sems.at[a], device_id=(x, y, c), device_id_type=MESH).wait_recv()
        for cp in cps:
            cp.wait_send()

    return _pc(body, name=name, out_shape=[jax.ShapeDtypeStruct(b.shape, b.dtype) for b in bufs],
               in_specs=[_ANY] * n, out_specs=[_ANY] * n, input_output_aliases={a: a for a in range(n)},
               scratch_shapes=[pltpu.SemaphoreType.DMA((n,)), pltpu.SemaphoreType.DMA((n,))])(*bufs)


def _row_tile(R, C, nbytes=1 << 20):
    t = 8
    while t * 2 <= R and R % (t * 2) == 0 and t * 2 * C * 4 <= nbytes:
        t *= 2
    assert R % t == 0
    return t


def to_bf16_block(name, w, chip_arr, layer, after=None):
    _, R, C = w.shape
    tr = _row_tile(R, C)

    def body(j_ref, w_ref, *rest):
        o_ref = rest[-1]
        o_ref[...] = w_ref[...].astype(o_ref.dtype)

    in_specs, args = [pl.BlockSpec((None, tr, C), lambda i, j_ref: (layer, i, 0))], (chip_arr, w)
    if after is not None:
        in_specs, args = in_specs + [_ANY], args + (after,)
    gs = pltpu.PrefetchScalarGridSpec(
        num_scalar_prefetch=1, grid=(R // tr,), in_specs=in_specs,
        out_specs=pl.BlockSpec((None, None, tr, C), lambda i, j_ref: (j_ref[0], layer, i, 0)))
    return _pc(body, name=name, grid_spec=gs, out_shape=jax.ShapeDtypeStruct((N_CHIP,) + w.shape, BF16),
               compiler_params=_cp(("parallel",)))(*args)


def add_rows(name, g, ra, c_arr, chip_arr, layer):
    _, _, R, C = g.shape
    rh = R // 2
    tr = _row_tile(rh, C)
    nb = rh // tr

    def body(c_ref, j_ref, g_ref, r_ref, o_ref, own_ref):
        s = (g_ref[...].astype(F32) + r_ref[...].astype(F32)).astype(o_ref.dtype)
        o_ref[...] = s

        @pl.when(pl.program_id(1) == j_ref[0])
        def _():
            own_ref[...] = s

    gs = pltpu.PrefetchScalarGridSpec(
        num_scalar_prefetch=2, grid=(nb, N_CHIP),
        in_specs=[pl.BlockSpec((None, None, tr, C), lambda i, j, c_ref, j_ref: (layer, j, c_ref[0] * nb + i, 0)),
                  pl.BlockSpec((None, tr, C), lambda i, j, c_ref, j_ref: (j, i, 0))],
        out_specs=[pl.BlockSpec((None, tr, C), lambda i, j, c_ref, j_ref: (j, i, 0)),
                   pl.BlockSpec((None, tr, C), lambda i, j, c_ref, j_ref: (j_ref[0], i, 0))])
    shp = jax.ShapeDtypeStruct(ra.shape, BF16)
    return _pc(body, name=name, grid_spec=gs, out_shape=[shp, shp],
               compiler_params=_cp(("arbitrary", "arbitrary")))(c_arr, chip_arr, g, ra)


def sum_rows_into(name, landed, c_arr, layer, into):
    n, rh, C = landed.shape
    tr = _row_tile(rh, C, nbytes=1 << 19)

    def body(*refs):
        g_ref, o_ref = refs[1], refs[-1]
        acc = g_ref[0].astype(F32)
        for j in range(1, n):
            acc = acc + g_ref[j].astype(F32)
        o_ref[...] = acc

    in_specs, args, alias = [pl.BlockSpec((n, tr, C), lambda i, c_ref: (0, i, 0))], (c_arr, landed), {}
    if into is not None:
        in_specs, args, alias = in_specs + [_ANY], args + (into,), {2: 0}
    gs = pltpu.PrefetchScalarGridSpec(
        num_scalar_prefetch=1, grid=(rh // tr,), in_specs=in_specs,
        out_specs=pl.BlockSpec((None, None, tr, C), lambda i, c_ref: (layer, c_ref[0], i, 0)))
    return _pc(body, name=name, grid_spec=gs, out_shape=jax.ShapeDtypeStruct((2, 2, rh, C), F32),
               input_output_aliases=alias, compiler_params=_cp(("parallel",)))(*args)


def sum_leading(name, g, plane=None):
    n, R, C = g.shape
    tr = _row_tile(R, C, nbytes=(1 << 21) // n)

    def body(*refs):
        g_ref, o_ref = refs[-2:]
        acc = g_ref[0].astype(F32)
        for j in range(1, n):
            acc = acc + g_ref[j].astype(F32)
        o_ref[...] = acc

    if plane is None:
        return _pc(body, name=name, grid=(R // tr,), in_specs=[pl.BlockSpec((n, tr, C), lambda i: (0, i, 0))],
                   out_specs=pl.BlockSpec((tr, C), lambda i: (i, 0)), out_shape=jax.ShapeDtypeStruct((R, C), F32),
                   compiler_params=_cp(("parallel",)))(g)
    count, idx = plane
    gs = pltpu.PrefetchScalarGridSpec(
        num_scalar_prefetch=1, grid=(R // tr,),
        in_specs=[pl.BlockSpec((n, tr, C), lambda i, p_ref: (0, i, 0))],
        out_specs=pl.BlockSpec((None, tr, C), lambda i, p_ref: (p_ref[0], i, 0)))
    return _pc(body, name=name, grid_spec=gs, out_shape=jax.ShapeDtypeStruct((count, R, C), F32),
               compiler_params=_cp(("parallel",)))(idx, g)


def adamw(name, w, g, m, v, emit_g=False):
    R, C = w.shape
    tr = _row_tile(R, C)

    def body(w_ref, g_ref, m_ref, v_ref, d_ref, mo_ref, vo_ref, *go_ref):
        gv = g_ref[...]
        if emit_g:
            go_ref[0][...] = gv
        mn = ADAM_B1 * m_ref[...] + (1.0 - ADAM_B1) * gv
        vn = ADAM_B2 * v_ref[...] + (1.0 - ADAM_B2) * (gv * gv)
        m_hat = mn / (1.0 - ADAM_B1 ** ADAM_STEP)
        v_hat = vn / (1.0 - ADAM_B2 ** ADAM_STEP)
        d_ref[...] = -ADAM_LR * (m_hat / (jnp.sqrt(v_hat) + ADAM_EPS) + ADAM_WD * w_ref[...])
        mo_ref[...] = mn
        vo_ref[...] = vn

    spec = pl.BlockSpec((tr, C), lambda i: (i, 0))
    shp = jax.ShapeDtypeStruct((R, C), F32)
    nout = 4 if emit_g else 3
    return _pc(body, name=name, grid=(R // tr,), in_specs=[spec] * 4, out_specs=[spec] * nout,
               out_shape=[shp] * nout, compiler_params=_cp(("parallel",)))(w, g, m, v)


_ADA_TN = 512


def adaln_fwd(name, cond, w, b):
    _, D, N = w.shape
    tn = min(_ADA_TN, N)

    def body(c_ref, w_ref, b_ref, o_ref):
        s = _silu(c_ref[...]).astype(BF16)
        o_ref[...] = dot_nn(s, w_ref[...].astype(BF16)) + b_ref[...]

    return _pc(body, name=name, grid=(2, N // tn),
               in_specs=[pl.BlockSpec((16, D), lambda l, n: (0, 0)),
                         pl.BlockSpec((None, D, tn), lambda l, n: (l, 0, n)),
                         pl.BlockSpec((None, 1, tn), lambda l, n: (l, 0, n))],
               out_specs=pl.BlockSpec((None, 16, tn), lambda l, n: (l, 0, n)),
               out_shape=jax.ShapeDtypeStruct((2, 16, N), F32),
               compiler_params=_cp(("parallel", "parallel")))(cond, w, b)


def adaln_bwd(name, cond, w, dm):
    _, D, N = w.shape
    tn = min(_ADA_TN, N)

    def body(c_ref, w_ref, dm_ref, gw_ref, ds_ref):
        first = jnp.logical_and(pl.program_id(0) == 0, pl.program_id(1) == 0)
        s = _silu(c_ref[...]).astype(BF16)
        dmb = dm_ref[...].astype(BF16)
        gw_ref[...] = dot_tn(s, dmb)
        p = dot_nt(dmb, w_ref[...].astype(BF16))

        @pl.when(first)
        def _():
            ds_ref[...] = p

        @pl.when(jnp.logical_not(first))
        def _():
            ds_ref[...] += p

    return _pc(body, name=name, grid=(2, N // tn),
               in_specs=[pl.BlockSpec((16, D), lambda l, n: (0, 0)),
                         pl.BlockSpec((None, D, tn), lambda l, n: (l, 0, n)),
                         pl.BlockSpec((None, 16, tn), lambda l, n: (l, 0, n))],
               out_specs=[pl.BlockSpec((None, D, tn), lambda l, n: (l, 0, n)),
                          pl.BlockSpec((16, D), lambda l, n: (0, 0))],
               out_shape=[jax.ShapeDtypeStruct((2, D, N), F32), jax.ShapeDtypeStruct((16, D), F32)],
               compiler_params=_cp(("arbitrary", "arbitrary")))(cond, w, dm)


def cctx_grad(name, parts, c_ctx):
    def body(p_ref, c_ref, o_ref):
        acc = p_ref[0]
        for j in range(1, N_CHIP):
            acc = acc + p_ref[j]
        o_ref[...] = acc * _dsilu(c_ref[...])

    return _pc(body, name=name, out_shape=jax.ShapeDtypeStruct(c_ctx.shape, F32))(parts, c_ctx)


def _pack(arrs):
    rows = []
    for a in arrs:
        f = a.reshape(-1)
        pad = (-f.shape[0]) % LANE
        rows.append(jnp.pad(f, (0, pad)).reshape(-1, LANE))
    out = jnp.concatenate(rows, axis=0)
    pad = (-out.shape[0]) % 8
    return jnp.pad(out, ((0, pad), (0, 0))) if pad else out


def _unpack(rows, shapes):
    out, r = [], 0
    for s in shapes:
        n = int(np.prod(s))
        nr = -(-n // LANE)
        out.append(rows[r:r + nr].reshape(-1)[:n].reshape(s))
        r += nr
    return out


MOD_NAMES = ("sh1", "sc1", "g1", "sh2", "sc2", "g2")


def kernel(x, c, ctx, c_ctx, w_ada, b_ada, norm1_g, w_in, ret_decay, ret_gn_g, conv_dw_w, conv_dw_b, conv_ln_g, conv_ln_b, conv_pw, na_rpb, w_out, norm2_g, ffn_up, ffn_dw_w, ffn_dw_b, ffn_down, final_g, loss_target, m_c_ctx, m_w_ada, m_b_ada, m_norm1_g, m_w_in, m_ret_decay, m_ret_gn_g, m_conv_dw_w, m_conv_dw_b, m_conv_ln_g, m_conv_ln_b, m_conv_pw, m_na_rpb, m_w_out, m_norm2_g, m_ffn_up, m_ffn_dw_w, m_ffn_dw_b, m_ffn_down, m_final_g, v_c_ctx, v_w_ada, v_b_ada, v_norm1_g, v_w_in, v_ret_decay, v_ret_gn_g, v_conv_dw_w, v_conv_dw_b, v_conv_ln_g, v_conv_ln_b, v_conv_pw, v_na_rpb, v_w_out, v_norm2_g, v_ffn_up, v_ffn_dw_w, v_ffn_dw_b, v_ffn_down, v_final_g):
    cfg = make_cfg(D=x.shape[2], T=x.shape[1], TC=ctx.shape[1], RH=ret_decay.shape[2], CW=conv_dw_b.shape[1],
                   NH=na_rpb.shape[1], DFF=ffn_dw_b.shape[1] // 2)
    D, T = cfg.D, cfg.T
    W = dict(c_ctx=c_ctx, w_ada=w_ada, b_ada=b_ada, norm1_g=norm1_g, w_in=w_in, ret_decay=ret_decay, ret_gn_g=ret_gn_g,
             conv_dw_w=conv_dw_w, conv_dw_b=conv_dw_b, conv_ln_g=conv_ln_g, conv_ln_b=conv_ln_b, conv_pw=conv_pw,
             na_rpb=na_rpb, w_out=w_out, norm2_g=norm2_g, ffn_up=ffn_up, ffn_dw_w=ffn_dw_w, ffn_dw_b=ffn_dw_b,
             ffn_down=ffn_down, final_g=final_g)
    Mo = dict(c_ctx=m_c_ctx, w_ada=m_w_ada, b_ada=m_b_ada, norm1_g=m_norm1_g, w_in=m_w_in, ret_decay=m_ret_decay,
              ret_gn_g=m_ret_gn_g, conv_dw_w=m_conv_dw_w, conv_dw_b=m_conv_dw_b, conv_ln_g=m_conv_ln_g,
              conv_ln_b=m_conv_ln_b, conv_pw=m_conv_pw, na_rpb=m_na_rpb, w_out=m_w_out, norm2_g=m_norm2_g,
              ffn_up=m_ffn_up, ffn_dw_w=m_ffn_dw_w, ffn_dw_b=m_ffn_dw_b, ffn_down=m_ffn_down, final_g=m_final_g)
    Vo = dict(c_ctx=v_c_ctx, w_ada=v_w_ada, b_ada=v_b_ada, norm1_g=v_norm1_g, w_in=v_w_in, ret_decay=v_ret_decay,
              ret_gn_g=v_ret_gn_g, conv_dw_w=v_conv_dw_w, conv_dw_b=v_conv_dw_b, conv_ln_g=v_conv_ln_g,
              conv_ln_b=v_conv_ln_b, conv_pw=v_conv_pw, na_rpb=v_na_rpb, w_out=v_w_out, norm2_g=v_norm2_g,
              ffn_up=v_ffn_up, ffn_dw_w=v_ffn_dw_w, ffn_dw_b=v_ffn_dw_b, ffn_down=v_ffn_down, final_g=v_final_g)
    order = list(W)
    xi, yi, ci = lax.axis_index("x"), lax.axis_index("y"), lax.axis_index("c")
    chip = 2 * xi + yi
    dev = 4 * xi + 2 * yi + ci
    NA = w_ada.shape[2]
    ncw, nfw = conv_dw_w.shape[2], ffn_dw_w.shape[2]

    c_arr = jnp.reshape(ci, (1,)).astype(jnp.int32)
    chip_arr = jnp.reshape(chip, (1,)).astype(jnp.int32)

    g_in = allgather8("ag_small_in", _pack([c[0], conv_dw_w, ffn_dw_w])).reshape(N_DEV, -1, LANE)
    c8 = g_in[:, :D // LANE].reshape(N_DEV, D)
    cw_parts, fw_parts = [], []
    for j in range(N_CHIP):
        _, a, b = _unpack(g_in[2 * j], [(D,), conv_dw_w.shape, ffn_dw_w.shape])
        cw_parts.append(a)
        fw_parts.append(b)
    conv_dw_w_full = jnp.concatenate(cw_parts, axis=2)
    ffn_dw_w_full = jnp.concatenate(fw_parts, axis=2)
    cond = jnp.concatenate([c8, c_ctx[None], jnp.zeros((16 - N_DEV - 1, D), F32)], axis=0)

    b_sh = lax.dynamic_slice(b_ada, (0, chip * NA), (2, NA)).reshape(2, 1, NA)
    m_sh = adaln_fwd("adaln_fwd", cond, w_ada, b_sh)
    m_dev = allgather8("ag_mod", m_sh.reshape(2 * 16, NA)).reshape(N_DEV, 2, 16, NA)
    m_all = jnp.concatenate([m_dev[2 * j] for j in range(N_CHIP)], axis=-1)
    mods = []
    for l in range(2):
        lat = lax.dynamic_slice(m_all[l], (dev, 0), (1, N_CHIP * NA))[0]
        cx = m_all[l, N_DEV]
        mods.append({nm: jnp.stack([lat[k * D:(k + 1) * D], cx[k * D:(k + 1) * D]], 0)[:, None, :]
                     for k, nm in enumerate(MOD_NAMES)})

    first, rest = ("w_in", "conv_pw"), ("w_out", "ffn_up", "ffn_down")
    wb = [{}, {}]
    have, flying_w = {}, {}

    def start_gather(tag, l, names, after):
        s_sem, r_sem, bufs, tok = allgather_layer_start(f"ag_{tag}_start", [wb[l][nm] for nm in names], l, after)
        flying_w[(l, names[0])] = (tag, l, names, bufs, s_sem, r_sem)
        return tok

    def land_gather(key, after):
        tag, l, names, bufs, s_sem, r_sem = flying_w.pop(key)
        landed = allgather_layer_wait(f"ag_{tag}_wait", bufs, s_sem, r_sem, after, l)
        have.update(zip([(l, nm) for nm in names], forward_halves(f"ag_{tag}_fwd", list(landed), l)))

    for nm in first:
        wb[0][nm] = to_bf16_block(f"to_bf16_{nm}_0", W[nm], chip_arr, 0)
    tok_first = start_gather("w0a", 0, first, m_all)
    later = [(l, nm) for l in range(2) for nm in BIG if nm not in wb[l]]
    casts = lax.optimization_barrier(tuple(
        to_bf16_block(f"to_bf16_{nm}_{l}", W[nm], chip_arr, l, after=tok_first) for l, nm in later))
    for (l, nm), cast in zip(later, casts):
        wb[l][nm] = cast
    land_gather((0, first[0]), casts[0])
    mods[0] = {**mods[0], "sc1": mods[0]["sc1"] + start_gather("w0b", 0, rest, have[(0, first[0])])[0, 0]}

    def wts(l, name, after):
        tok = None
        if (l, name) not in have:
            if l == 0:
                land_gather((0, rest[0]), after)
                tok = start_gather("w1", 1, BIG, have[(0, rest[0])])[0, 0]
            else:
                land_gather((1, BIG[0]), after)
        return have[(l, name)], tok

    sp = dict(norm1_g=norm1_g, norm2_g=norm2_g, ret_decay=ret_decay, ret_gn_g=ret_gn_g, conv_dw_w=conv_dw_w_full,
              conv_dw_b=conv_dw_b, conv_ln_g=conv_ln_g, conv_ln_b=conv_ln_b, na_rpb=na_rpb, ffn_dw_w=ffn_dw_w_full,
              ffn_dw_b=ffn_dw_b, final_g=final_g)
    flights = []

    def grads_ready(l, names, gb):
        tag = f"{l}_{names[0]}"
        from_sib = exchange_rows(f"rs_exchange_{tag}", [gb[nm] for nm in names], l)
        both = [add_rows(f"rs_add_{nm}_{l}", gb[nm], r, c_arr, chip_arr, l) for nm, r in zip(names, from_sib)]
        part, lands = [b[0] for b in both], [b[1] for b in both]
        s_sem, r_sem, part, lands, tok = scatter_slices_start(f"rs_scatter_{tag}_start", part, lands)
        flights.append((l, names, tag, part, lands, (s_sem, r_sem)))
        return tok[0, 0]

    loss_l, gx, gb, gss, dms, dfg, tok_last = local_step(
        cfg, x[0], ctx[0], loss_target[0], mods, wts, sp, grads_ready)
    loss = lax.psum(loss_l, ("x", "y", "c"))

    delta, new_m, new_v = {}, {}, {}
    bigs = ("w_ada",) + BIG

    def adamw_big(nm):
        shp = W[nm].shape
        v2 = lambda a: a.reshape(-1, shp[-1])
        d_, m_, v_, *g_ = adamw(f"adamw_{nm}", v2(W[nm]), v2(gfull[nm]), v2(Mo[nm]), v2(Vo[nm]), emit_g=nm in BIG)
        delta[nm], new_m[nm], new_v[nm] = d_.reshape(shp), m_.reshape(shp), v_.reshape(shp)
        if g_:
            gfull[nm] = g_[0].reshape(shp)

    gfull, fin, after = {}, {}, gx
    for names in GRAD_GROUPS:
        for l, _, tag, part, lands, sems in sorted([f for f in flights if f[1] == names], key=lambda f: -f[0]):
            landed = scatter_slices_wait(f"rs_scatter_{tag}_wait", part, lands, *sems, after)
            for nm, p in zip(names, landed):
                fin[nm] = sum_rows_into(f"rs_sum_{nm}_{l}", p, c_arr, l, fin.get(nm))
        for nm, gfin in zip(names, share_rows(f"rs_share_{names[0]}", [fin[nm] for nm in names])):
            gfull[nm] = gfin.reshape(W[nm].shape)
            adamw_big(nm)
        after = delta[names[-1]]

    dmseg = jnp.stack([jnp.stack([jnp.concatenate([dms[l][nm][r, 0] for nm in MOD_NAMES]) for r in range(2)])
                       for l in range(2)])
    dmseg, _ = lax.optimization_barrier((dmseg, fin[GRAD_GROUPS[-1][-1]]))
    gsm = dict(
        norm1_g=jnp.stack([gss[l]["norm1_g"][0] for l in range(2)]),
        ret_decay=jnp.stack([gss[l]["lam"] * jax.nn.sigmoid(-ret_decay[l]) for l in range(2)]),
        ret_gn_g=jnp.stack([gss[l]["ret_gn_g"][0] for l in range(2)]),
        conv_dw_w=jnp.stack([gss[l]["conv_dw_w"][:cfg.CK] for l in range(2)]),
        conv_dw_b=jnp.stack([gss[l]["conv_dw_b"][0] for l in range(2)]),
        conv_ln_g=jnp.stack([gss[l]["conv_ln_g"][0] for l in range(2)]),
        conv_ln_b=jnp.stack([gss[l]["conv_ln_b"][0] for l in range(2)]),
        na_rpb=jnp.stack([gss[l]["na_rpb"] for l in range(2)]),
        norm2_g=jnp.stack([gss[l]["norm2_g"][0] for l in range(2)]),
        ffn_dw_w=jnp.stack([gss[l]["ffn_dw_w"][:, :3].transpose(1, 0, 2).reshape(3, 2 * cfg.DFF) for l in range(2)]),
        ffn_dw_b=jnp.stack([gss[l]["ffn_dw_b"].reshape(-1) for l in range(2)]),
        final_g=dfg)
    snames = list(gsm)
    sshapes = [dmseg.shape] + [gsm[nm].shape for nm in snames]
    packed = _pack([dmseg] + [gsm[nm] for nm in snames])
    g_all = allgather8("ag_small_grads", packed).reshape(N_DEV, packed.shape[0], LANE)
    summed = sum_leading("sum_small_grads", g_all)
    dm_sum, *gsum = _unpack(summed, sshapes)
    gfull.update(zip(snames, gsum))
    ndm = int(np.prod(dmseg.shape))
    dm_all = g_all[:, :ndm // LANE].reshape(N_DEV, 2, 2, 6 * D)
    gfull["b_ada"] = sum_leading("sum_b_ada", dm_all.transpose(0, 2, 1, 3).reshape(2 * N_DEV, 2 * 6 * D // LANE, LANE)
                                 ).reshape(2, 6 * D)

    dm16 = jnp.concatenate([dm_all[:, :, 0].transpose(1, 0, 2), dm_sum[:, 1][:, None],
                            jnp.zeros((2, 16 - N_DEV - 1, 6 * D), F32)], axis=1)
    dm16 = lax.dynamic_slice(dm16, (0, 0, chip * NA), (2, 16, NA))
    gfull["w_ada"], ds16 = adaln_bwd("adaln_bwd", cond, w_ada, dm16)
    ds_all = allgather8("ag_dsilu", ds16[8:16]).reshape(N_DEV, 8, D)[0::2, 0:1]
    gfull["c_ctx"] = cctx_grad("cctx_grad", ds_all, c_ctx[None])[0]
    gfull["conv_dw_w"] = lax.dynamic_slice(gfull["conv_dw_w"], (0, 0, chip * ncw), (2, cfg.CK, ncw))
    gfull["ffn_dw_w"] = lax.dynamic_slice(gfull["ffn_dw_w"], (0, 0, chip * nfw), (2, 3, nfw))

    adamw_big("w_ada")
    smalls = [nm for nm in order if nm not in bigs]
    shapes = [W[nm].shape for nm in smalls]
    d_, m_, v_ = adamw("adamw_small", _pack([W[nm] for nm in smalls]), _pack([gfull[nm] for nm in smalls]),
                       _pack([Mo[nm] for nm in smalls]), _pack([Vo[nm] for nm in smalls]))
    for nm, a, b, e in zip(smalls, _unpack(d_, shapes), _unpack(m_, shapes), _unpack(v_, shapes)):
        delta[nm], new_m[nm], new_v[nm] = a, b, e
    return (loss, gx[None], *[gfull[nm] for nm in order], *[delta[nm] for nm in order],
            *[new_m[nm] for nm in order], *[new_v[nm] for nm in order])
```

```python
import collections
import functools

import numpy as np
import jax
import jax.numpy as jnp
from jax import lax
from jax.experimental import pallas as pl
from jax.experimental.pallas import tpu as pltpu

F32 = jnp.float32
BF16 = jnp.bfloat16
EPS = 1e-6
ROPE_BASE = 10000.0
NEG = -1e30
LANE = 128
VMEM_LIMIT = 56 * 1024 * 1024

ADAM_LR, ADAM_B1, ADAM_B2, ADAM_EPS, ADAM_WD, ADAM_STEP = 0.001, 0.9, 0.999, 1e-08, 0.01, 10

Cfg = collections.namedtuple(
    "Cfg", "D T TC GW RH RDK RDV CW CK NH NDH NAR NAC DFF TB")


def make_cfg(D=2048, T=4096, TC=256, RH=4, CW=512, NH=4, DFF=5632):
    return Cfg(D=D, T=T, TC=TC, GW=64, RH=RH, RDK=128, RDV=256, CW=CW, CK=31, NH=NH, NDH=128,
               NAR=8, NAC=16, DFF=DFF, TB=256)


def _offsets(cfg):
    sizes = [cfg.RH * cfg.RDK, cfg.RH * cfg.RDK, cfg.RH * cfg.RDV, cfg.RH * cfg.RDV, cfg.CW, cfg.CW,
             cfg.NH * cfg.NDH, cfg.NH * cfg.NDH, cfg.NH * cfg.NDH]
    offs = [0]
    for s in sizes:
        offs.append(offs[-1] + s)
    return dict(zip(["lq", "lk", "lv", "lg", "la", "lb", "nq", "nk", "nv", "end"], offs))


def _pc(body, **kw):
    return pl.pallas_call(body, **kw)


def _cp(sem=None):
    return pltpu.CompilerParams(dimension_semantics=sem, vmem_limit_bytes=VMEM_LIMIT)


def _dot(a, b, ca, cb):
    return lax.dot_general(a, b, (((ca,), (cb,)), ((), ())), preferred_element_type=F32)


def dot_nn(a, b):
    return _dot(a, b, 1, 0)


def dot_nt(a, b):
    return _dot(a, b, 1, 1)


def dot_tn(a, b):
    return _dot(a, b, 0, 0)


def _sigmoid(x):
    return 1.0 / (1.0 + jnp.exp(-x))


def _silu(x):
    return x * _sigmoid(x)


def _dsilu(x):
    s = _sigmoid(x)
    return s * (1.0 + x * (1.0 - s))


def matmul(name, a, b, *, contract, grid, a_spec, b_spec, out_shape, out_spec, nk, into=None):
    dot = {"nn": dot_nn, "nt": dot_nt, "tn": dot_tn}[contract]
    direct = nk > 1 and out_shape.dtype == F32
    kax = len(grid) - 1

    def body(a_ref, b_ref, *rest):
        o_ref, *scr = rest[1:] if into is not None else rest
        p = dot(a_ref[...].astype(BF16), b_ref[...].astype(BF16))
        if nk == 1:
            o_ref[...] = p.astype(o_ref.dtype)
            return
        acc = o_ref if direct else scr[0]
        k = pl.program_id(kax)

        @pl.when(k == 0)
        def _():
            acc[...] = p

        @pl.when(k > 0)
        def _():
            acc[...] += p

        if not direct:
            @pl.when(k == nk - 1)
            def _():
                o_ref[...] = acc[...].astype(o_ref.dtype)

    scratch = []
    if nk > 1 and not direct:
        blk = [s for s in out_spec.block_shape if s is not None]
        scratch = [pltpu.VMEM(tuple(blk), F32)]
    sem = ("parallel",) * kax + (("arbitrary",) if nk > 1 else ("parallel",))
    in_specs, args, alias = [a_spec, b_spec], (a, b), {}
    if into is not None:
        in_specs, args, alias = in_specs + [pl.BlockSpec(memory_space=pl.ANY)], (a, b, into), {2: 0}
    return _pc(body, name=name, grid=grid, in_specs=in_specs, out_specs=out_spec, out_shape=out_shape,
               scratch_shapes=scratch, input_output_aliases=alias, compiler_params=_cp(sem))(*args)


_WG_ROWS = 1024


def wgrad(cfg, name, a, dc, a_spec, dc_spec, out_shape, out_spec, ntiles, into):
    T, TC = cfg.T, cfg.TC
    tml = min(_WG_ROWS, T)
    nl = T // tml

    def body(al_ref, ac_ref, dl_ref, dcx_ref, *rest):
        o_ref, acc = rest[-2:]
        m = pl.program_id(1)

        @pl.when(m == 0)
        def _():
            acc[...] = dot_tn(al_ref[...], dl_ref[...])

        @pl.when(jnp.logical_and(m > 0, m < nl))
        def _():
            acc[...] += dot_tn(al_ref[...], dl_ref[...])

        @pl.when(m == nl)
        def _():
            o_ref[...] = (acc[...] + dot_tn(ac_ref[...], dcx_ref[...])).astype(o_ref.dtype)

    lat = lambda m: jnp.minimum(m, nl - 1)
    ctx = lambda m: T // TC
    in_specs = [a_spec(tml, lat), a_spec(TC, ctx), dc_spec(tml, lat), dc_spec(TC, ctx)]
    args, alias = (a, a, dc, dc), {}
    if into is not None:
        in_specs, args, alias = in_specs + [pl.BlockSpec(memory_space=pl.ANY)], args + (into,), {4: 0}
    blk = tuple(s for s in out_spec.block_shape if s is not None)
    return _pc(body, name=name, grid=(ntiles, nl + 1), in_specs=in_specs, out_specs=out_spec, out_shape=out_shape,
               scratch_shapes=[pltpu.VMEM(blk, F32)], input_output_aliases=alias,
               compiler_params=_cp(("parallel", "arbitrary")))(*args)


def mm_rowsharded(name, a, w4, l, out_dtype, tn):
    pieces = a if isinstance(a, tuple) else (a,)
    L = pieces[0].shape[0]
    nch, _, Kb, N = w4.shape
    tm = _tm(L, 8)
    assert all(p.shape[1] % Kb == 0 for p in pieces) and sum(p.shape[1] for p in pieces) == nch * Kb

    def body(*refs):
        w_ref, o_ref = refs[-2:]
        acc, j = None, 0
        for a_ref in refs[:-2]:
            for b in range(a_ref.shape[1] // Kb):
                p = dot_nn(a_ref[:, b * Kb:(b + 1) * Kb], w_ref[j])
                acc = p if acc is None else acc + p
                j += 1
        o_ref[...] = acc.astype(o_ref.dtype)

    return _pc(body, name=name, grid=(N // tn, L // tm),
               in_specs=[pl.BlockSpec((tm, p.shape[1]), lambda n, m: (m, 0)) for p in pieces]
               + [pl.BlockSpec((nch, None, Kb, tn), lambda n, m: (0, l, 0, n))],
               out_specs=pl.BlockSpec((tm, tn), lambda n, m: (m, n)),
               out_shape=jax.ShapeDtypeStruct((L, N), out_dtype),
               compiler_params=_cp(("parallel", "parallel")))(*pieces, w4)


def _region(cfg):
    nlat = cfg.T // cfg.TB
    return lambda i: jnp.minimum(i // nlat, 1)


def norm_mod_fwd(cfg, name, x, ng, sc, sh):
    L, D = x.shape
    TB = cfg.TB
    reg = _region(cfg)

    def body(x_ref, ng_ref, sc_ref, sh_ref, h_ref):
        xv = x_ref[...]
        r = lax.rsqrt(jnp.mean(xv * xv, axis=-1, keepdims=True) + EPS)
        n = xv * r * ng_ref[...]
        h_ref[...] = (n * (1.0 + sc_ref[...]) + sh_ref[...]).astype(h_ref.dtype)

    row = pl.BlockSpec((TB, D), lambda i: (i, 0))
    vec = pl.BlockSpec((1, D), lambda i: (0, 0))
    rvec = pl.BlockSpec((None, 1, D), lambda i: (reg(i), 0, 0))
    return _pc(body, name=name, grid=(L // TB,), in_specs=[row, vec, rvec, rvec], out_specs=row,
               out_shape=jax.ShapeDtypeStruct((L, D), BF16), compiler_params=_cp(("parallel",)))(x, ng, sc, sh)


def norm_mod_bwd(cfg, name, dh, x, ng, sc, dx_in, latent_only=False):
    L, D = x.shape
    TB = cfg.TB
    nlat = cfg.T // TB
    reg = _region(cfg)

    def body(dh_ref, x_ref, ng_ref, sc_ref, dxi_ref, dx_ref, dsc_ref, dsh_ref, dng_ref):
        i = pl.program_id(0)
        xv = x_ref[...]
        r = lax.rsqrt(jnp.mean(xv * xv, axis=-1, keepdims=True) + EPS)
        xh = xv * r
        g = ng_ref[...]
        n = xh * g
        dh = dh_ref[...]
        dn = dh * (1.0 + sc_ref[...])
        dxh = dn * g
        dx = r * (dxh - xh * jnp.mean(dxh * xh, axis=-1, keepdims=True))
        if latent_only:
            @pl.when(i < nlat)
            def _():
                dx_ref[...] = dxi_ref[...] + dx
        else:
            dx_ref[...] = dxi_ref[...] + dx
        s_sh = jnp.sum(dh, axis=0, keepdims=True)
        s_sc = jnp.sum(dh * n, axis=0, keepdims=True)
        s_ng = jnp.sum(dn * xh, axis=0, keepdims=True)
        first = jnp.logical_or(i == 0, i == nlat)

        @pl.when(first)
        def _():
            dsh_ref[...] = s_sh
            dsc_ref[...] = s_sc

        @pl.when(jnp.logical_not(first))
        def _():
            dsh_ref[...] += s_sh
            dsc_ref[...] += s_sc

        @pl.when(i == 0)
        def _():
            dng_ref[...] = s_ng

        @pl.when(i > 0)
        def _():
            dng_ref[...] += s_ng

    row = pl.BlockSpec((TB, D), lambda i: (i, 0))
    vec = pl.BlockSpec((1, D), lambda i: (0, 0))
    rvec = pl.BlockSpec((None, 1, D), lambda i: (reg(i), 0, 0))
    dxs = pl.BlockSpec((TB, D), lambda i: (jnp.minimum(i, nlat - 1), 0)) if latent_only else row
    return _pc(body, name=name, grid=(L // TB,), in_specs=[row, row, vec, rvec, row],
               out_specs=[dxs, rvec, rvec, vec],
               out_shape=[jax.ShapeDtypeStruct((cfg.T if latent_only else L, D), F32),
                          jax.ShapeDtypeStruct((2, 1, D), F32),
                          jax.ShapeDtypeStruct((2, 1, D), F32), jax.ShapeDtypeStruct((1, D), F32)],
               compiler_params=_cp(("arbitrary",)))(dh, x, ng, sc, dx_in)


def resid_norm_fwd(cfg, name, x, y, g, ng, sc, sh):
    L, D = x.shape
    TB = cfg.TB
    reg = _region(cfg)

    def body(x_ref, y_ref, g_ref, ng_ref, sc_ref, sh_ref, xo_ref, h_ref):
        xv = x_ref[...] + g_ref[...] * y_ref[...]
        xo_ref[...] = xv
        r = lax.rsqrt(jnp.mean(xv * xv, axis=-1, keepdims=True) + EPS)
        n = xv * r * ng_ref[...]
        h_ref[...] = (n * (1.0 + sc_ref[...]) + sh_ref[...]).astype(h_ref.dtype)

    row = pl.BlockSpec((TB, D), lambda i: (i, 0))
    vec = pl.BlockSpec((1, D), lambda i: (0, 0))
    rvec = pl.BlockSpec((None, 1, D), lambda i: (reg(i), 0, 0))
    return _pc(body, name=name, grid=(L // TB,), in_specs=[row, row, rvec, vec, rvec, rvec], out_specs=[row, row],
               out_shape=[jax.ShapeDtypeStruct((L, D), F32), jax.ShapeDtypeStruct((L, D), BF16)],
               compiler_params=_cp(("parallel",)))(x, y, g, ng, sc, sh)


def resid_bwd(cfg, name, dxo, y, g):
    L, D = y.shape
    TB = cfg.TB
    nlat = cfg.T // TB
    reg = _region(cfg)

    def body(d_ref, y_ref, g_ref, dy_ref, dg_ref):
        i = pl.program_id(0)
        d = d_ref[...]
        dy_ref[...] = (d * g_ref[...]).astype(dy_ref.dtype)
        s = jnp.sum(d * y_ref[...], axis=0, keepdims=True)
        first = jnp.logical_or(i == 0, i == nlat)

        @pl.when(first)
        def _():
            dg_ref[...] = s

        @pl.when(jnp.logical_not(first))
        def _():
            dg_ref[...] += s

    row = pl.BlockSpec((TB, D), lambda i: (i, 0))
    rvec = pl.BlockSpec((None, 1, D), lambda i: (reg(i), 0, 0))
    return _pc(body, name=name, grid=(L // TB,), in_specs=[row, row, rvec], out_specs=[row, rvec],
               out_shape=[jax.ShapeDtypeStruct((L, D), BF16), jax.ShapeDtypeStruct((2, 1, D), F32)],
               compiler_params=_cp(("arbitrary",)))(dxo, y, g)


def final_loss(cfg, name, xm, y2, g2, fg, tgt):
    L, D = xm.shape
    TB = cfg.TB
    nlat = cfg.T // TB

    def body(x_ref, y_ref, g2_ref, fg_ref, t_ref, ls_ref, dx_ref, dg_ref):
        i = pl.program_id(0)

        @pl.when(i == 0)
        def _():
            ls_ref[...] = jnp.zeros_like(ls_ref)
            dg_ref[...] = jnp.zeros_like(dg_ref)

        @pl.when(i < nlat)
        def _():
            xv = x_ref[...] + g2_ref[...] * y_ref[...]
            r = lax.rsqrt(jnp.mean(xv * xv, axis=-1, keepdims=True) + EPS)
            xh = xv * r
            g = fg_ref[...]
            e = xh * g - t_ref[...]
            ls_ref[...] += 0.5 * jnp.sum(e * e) / D
            dy = e / D
            dg_ref[...] += jnp.sum(dy * xh, axis=0, keepdims=True)
            dxh = dy * g
            dx_ref[...] = r * (dxh - xh * jnp.mean(dxh * xh, axis=-1, keepdims=True))

        @pl.when(i >= nlat)
        def _():
            dx_ref[...] = jnp.zeros_like(dx_ref)

    row = pl.BlockSpec((TB, D), lambda i: (i, 0))
    trow = pl.BlockSpec((TB, D), lambda i: (jnp.minimum(i, nlat - 1), 0))
    vec = pl.BlockSpec((1, D), lambda i: (0, 0))
    return _pc(body, name=name, grid=(L // TB,), in_specs=[row, row, vec, vec, trow],
               out_specs=[pl.BlockSpec((1, LANE), lambda i: (0, 0)), row, vec],
               out_shape=[jax.ShapeDtypeStruct((1, LANE), F32), jax.ShapeDtypeStruct((L, D), F32),
                          jax.ShapeDtypeStruct((1, D), F32)],
               compiler_params=_cp(("arbitrary",)))(xm, y2, g2, fg, tgt)


def rope_tables(cfg):
    half = cfg.RDK // 2
    nf = half // 2
    pos = np.arange(cfg.T)
    row = (pos // cfg.GW).astype(np.float32)
    col = (pos % cfg.GW).astype(np.float32)
    inv = jnp.asarray(ROPE_BASE, F32) ** (-jnp.arange(nf, dtype=F32) / nf)
    ar = jnp.asarray(row)[:, None] * inv[None, :]
    ac = jnp.asarray(col)[:, None] * inv[None, :]
    cos = jnp.concatenate([jnp.cos(ar), jnp.cos(ar), jnp.cos(ac), jnp.cos(ac)], axis=1)
    sin = jnp.concatenate([-jnp.sin(ar), jnp.sin(ar), -jnp.sin(ac), jnp.sin(ac)], axis=1)
    cos = jnp.concatenate([cos, jnp.ones((cfg.TC, cfg.RDK), F32)], axis=0)
    sin = jnp.concatenate([sin, jnp.zeros((cfg.TC, cfg.RDK), F32)], axis=0)
    return cos, sin


def _rb(cfg):
    rb = (cfg.T + cfg.TC) // 4
    assert rb % 16 == 0
    return rb


def _swap32(t):
    lane = lax.broadcasted_iota(jnp.int32, t.shape, 1)
    return jnp.where((lane % 64) < 32, pltpu.roll(t, 96, 1), pltpu.roll(t, 32, 1))


def rope_fwd(cfg, name, P, cos, sin):
    L = P.shape[0]
    TB = _rb(cfg)
    off = _offsets(cfg)
    cq, ck = off["lq"] // LANE, off["lk"] // LANE
    scale = cfg.RDK ** -0.5

    def body(q_ref, k_ref, c_ref, s_ref, qo_ref, ko_ref):
        c = c_ref[...]
        s = s_ref[...]
        q = q_ref[...]
        k = k_ref[...]
        qo_ref[...] = (q * c + _swap32(q) * s) * scale
        ko_ref[...] = k * c + _swap32(k) * s

    tab = pl.BlockSpec((TB, LANE), lambda i, h: (i, 0))
    out = pl.BlockSpec((TB, LANE), lambda i, h: (i, h))
    shp = jax.ShapeDtypeStruct((L, cfg.RH * cfg.RDK), F32)
    return _pc(body, name=name, grid=(L // TB, cfg.RH),
               in_specs=[pl.BlockSpec((TB, LANE), lambda i, h: (i, cq + h)),
                         pl.BlockSpec((TB, LANE), lambda i, h: (i, ck + h)), tab, tab],
               out_specs=[out, out], out_shape=[shp, shp],
               compiler_params=_cp(("parallel", "parallel")))(P, P, cos, sin)


def rope_bwd(cfg, name, dq2, dk2, cos, sin):
    L, W = dq2[0].shape
    TB = _rb(cfg)
    scale = cfg.RDK ** -0.5

    def body(dqf_ref, dqb_ref, dkf_ref, dkb_ref, c_ref, s_ref, qo_ref, ko_ref):
        c = c_ref[...]
        s = s_ref[...]
        dq = dqf_ref[...] + dqb_ref[...]
        dk = dkf_ref[...] + dkb_ref[...]
        qo_ref[...] = ((dq * c - _swap32(dq) * s) * scale).astype(qo_ref.dtype)
        ko_ref[...] = (dk * c - _swap32(dk) * s).astype(ko_ref.dtype)

    tab = pl.BlockSpec((TB, LANE), lambda i, h: (i, 0))
    blk = pl.BlockSpec((TB, LANE), lambda i, h: (i, h))
    shp = jax.ShapeDtypeStruct((L, W), BF16)
    return _pc(body, name=name, grid=(L // TB, cfg.RH), in_specs=[blk, blk, blk, blk, tab, tab],
               out_specs=[blk, blk], out_shape=[shp, shp],
               compiler_params=_cp(("parallel", "parallel")))(*dq2, *dk2, cos, sin)


def _ret_chunk_map(cfg):
    C = cfg.RDK
    n = (cfg.T + cfg.TC) // C
    nlat, nctx = cfg.T // C, cfg.TC // C

    def chunk(d, s):
        if d == 0:
            return jnp.where(s < nctx, nlat + s, s - nctx)
        return n - 1 - s

    return n, chunk


def _ret_decay_terms(d, lam, C):
    ii = lax.broadcasted_iota(jnp.int32, (C, C), 0)
    jj = lax.broadcasted_iota(jnp.int32, (C, C), 1)
    diff = (ii - jj if d == 0 else jj - ii).astype(F32)
    dpos = jnp.maximum(diff, 0.0)
    Dm = jnp.where(diff >= 0, jnp.exp(dpos * lam), 0.0)
    ic = lax.broadcasted_iota(jnp.int32, (C, 1), 0).astype(F32)
    cxi = ic + 1.0 if d == 0 else C - ic
    cze = C - 1.0 - ic if d == 0 else ic
    xi = jnp.exp(cxi * lam)
    ze = jnp.exp(cze * lam)
    g = jnp.exp(jnp.full((1, 1), C, F32) * lam)
    return dpos, Dm, cxi, cze, xi, ze, g


def retention_fwd(cfg, name, qr, kr, P, lam):
    L = P.shape[0]
    C, DV, RH = cfg.RDK, cfg.RDV, cfg.RH
    n, chunk = _ret_chunk_map(cfg)

    def body(lam_ref, qf_ref, qb_ref, kf_ref, kb_ref, vf_ref, vb_ref, of_ref, ob_ref, st_ref, S):
        s = pl.program_id(0)

        @pl.when(s == 0)
        def _():
            S[...] = jnp.zeros_like(S)

        for d, (q_ref, k_ref, v_ref, o_ref) in enumerate(((qf_ref, kf_ref, vf_ref, of_ref),
                                                          (qb_ref, kb_ref, vb_ref, ob_ref))):
            for h in range(RH):
                _, Dm, _, _, xi, ze, g = _ret_decay_terms(d, lam_ref[d, h], C)
                k = k_ref[:, h * C:(h + 1) * C]
                qb = q_ref[:, h * C:(h + 1) * C].astype(BF16)
                kb = k.astype(BF16)
                vb = v_ref[:, h * DV:(h + 1) * DV].astype(BF16)
                Sv = S[d, h]
                st_ref[d, h] = Sv
                A = dot_nt(qb, kb) * Dm
                o_ref[:, h * DV:(h + 1) * DV] = dot_nn(A.astype(BF16), vb) + dot_nn(qb, Sv.astype(BF16)) * xi
                S[d, h] = Sv * g + dot_tn((k * ze).astype(BF16), vb)

    def spec(w, col, d):
        return pl.BlockSpec((C, w), lambda s: (chunk(d, s), col))

    W, WV = RH * C, RH * DV
    return _pc(body, name=name, grid=(n,),
               in_specs=[pl.BlockSpec(memory_space=pltpu.SMEM), spec(W, 0, 0), spec(W, 0, 1), spec(W, 0, 0),
                         spec(W, 0, 1), spec(WV, 1, 0), spec(WV, 1, 1)],
               out_specs=[spec(WV, 0, 0), spec(WV, 0, 1),
                          pl.BlockSpec((2, RH, None, C, DV), lambda s: (0, 0, s, 0, 0))],
               out_shape=[jax.ShapeDtypeStruct((L, WV), F32), jax.ShapeDtypeStruct((L, WV), F32),
                          jax.ShapeDtypeStruct((2, RH, n, C, DV), F32)],
               scratch_shapes=[pltpu.VMEM((2, RH, C, DV), F32)],
               compiler_params=_cp(("arbitrary",)))(lam, qr, qr, kr, kr, P, P)


def retention_bwd(cfg, name, qr, kr, P, lam, st, do):
    L = P.shape[0]
    C, DV, RH = cfg.RDK, cfg.RDV, cfg.RH
    n, chunk = _ret_chunk_map(cfg)

    def body(lam_ref, qf_ref, qb_ref, kf_ref, kb_ref, vf_ref, vb_ref, st_ref, dof_ref, dob_ref,
             dqf_ref, dqb_ref, dkf_ref, dkb_ref, dvf_ref, dvb_ref, dl_ref, dS):
        si = pl.program_id(0)

        @pl.when(si == 0)
        def _():
            dS[...] = jnp.zeros_like(dS)
            dl_ref[...] = jnp.zeros_like(dl_ref)

        dirs = ((qf_ref, kf_ref, vf_ref, dof_ref, dqf_ref, dkf_ref, dvf_ref),
                (qb_ref, kb_ref, vb_ref, dob_ref, dqb_ref, dkb_ref, dvb_ref))
        for d, (q_ref, k_ref, v_ref, do_ref, dq_ref, dk_ref, dv_ref) in enumerate(dirs):
            for h in range(RH):
                dpos, Dm, cxi, cze, xi, ze, g = _ret_decay_terms(d, lam_ref[d, h], C)
                hk = slice(h * C, (h + 1) * C)
                hv = slice(h * DV, (h + 1) * DV)
                k = k_ref[:, hk]
                do = do_ref[:, hv]
                qb = q_ref[:, hk].astype(BF16)
                kb = k.astype(BF16)
                vb = v_ref[:, hv].astype(BF16)
                dob = do.astype(BF16)
                Sn = st_ref[d, h]
                Snb = Sn.astype(BF16)
                dSn = dS[d, h]
                dSb = dSn.astype(BF16)
                A = dot_nt(qb, kb) * Dm
                dA = dot_nt(dob, vb)
                dQK = (dA * Dm).astype(BF16)
                kzb = (k * ze).astype(BF16)
                dv_ref[:, hv] = dot_tn(A.astype(BF16), dob) + dot_nn(kzb, dSb)
                dkz = dot_nt(vb, dSb)
                doxb = (do * xi).astype(BF16)
                dq_ref[:, hk] = dot_nn(dQK, kb) + dot_nt(doxb, Snb)
                dk_ref[:, hk] = dot_tn(dQK, qb) + dkz * ze
                QS = dot_nn(qb, Snb)
                t = (jnp.sum(dA * A * dpos) + jnp.sum(do * QS * (cxi * xi)) + jnp.sum(k * dkz * (cze * ze)))
                t4 = jnp.sum(dSn * Sn, axis=0, keepdims=True)
                t4 = jnp.sum(t4 * (g * C), axis=1, keepdims=True)
                dl_ref[d, h] += t + t4
                dS[d, h] = g * dSn + dot_tn(qb, doxb)

    def spec(w, col, d):
        return pl.BlockSpec((C, w), lambda si: (chunk(d, n - 1 - si), col))

    W, WV = RH * C, RH * DV
    return _pc(body, name=name, grid=(n,),
               in_specs=[pl.BlockSpec(memory_space=pltpu.SMEM), spec(W, 0, 0), spec(W, 0, 1), spec(W, 0, 0),
                         spec(W, 0, 1), spec(WV, 1, 0), spec(WV, 1, 1),
                         pl.BlockSpec((2, RH, None, C, DV), lambda si: (0, 0, n - 1 - si, 0, 0)),
                         spec(WV, 0, 0), spec(WV, 0, 1)],
               out_specs=[spec(W, 0, 0), spec(W, 0, 1), spec(W, 0, 0), spec(W, 0, 1), spec(WV, 0, 0), spec(WV, 0, 1),
                          pl.BlockSpec((2, RH, 8, LANE), lambda si: (0, 0, 0, 0))],
               out_shape=[jax.ShapeDtypeStruct((L, W), F32)] * 4 + [jax.ShapeDtypeStruct((L, WV), F32)] * 2
               + [jax.ShapeDtypeStruct((2, RH, 8, LANE), F32)],
               scratch_shapes=[pltpu.VMEM((2, RH, C, DV), F32)],
               compiler_params=_cp(("arbitrary",)))(lam, qr, qr, kr, kr, P, P, st, do, do)


def add_cast(cfg, name, a, b):
    L, W = a.shape
    TB = _rb(cfg)

    def body(a_ref, b_ref, o_ref):
        o_ref[...] = (a_ref[...] + b_ref[...]).astype(o_ref.dtype)

    spec = pl.BlockSpec((TB, W), lambda i: (i, 0))
    return _pc(body, name=name, grid=(L // TB,), in_specs=[spec, spec], out_specs=spec,
               out_shape=jax.ShapeDtypeStruct((L, W), BF16), compiler_params=_cp(("parallel",)))(a, b)


def ggn_fwd(cfg, name, o2, P, gn_g):
    L = P.shape[0]
    TB, DV, RH = _rb(cfg), cfg.RDV, cfg.RH
    gc0 = _offsets(cfg)["lg"] // DV

    def body(of_ref, ob_ref, gate_ref, g_ref, out_ref):
        o = of_ref[...] + ob_ref[...]
        mu = jnp.mean(o, axis=-1, keepdims=True)
        xc = o - mu
        var = jnp.mean(xc * xc, axis=-1, keepdims=True)
        y = xc * lax.rsqrt(var + EPS) * g_ref[...]
        out_ref[...] = (y * _silu(gate_ref[...])).astype(out_ref.dtype)

    blk = pl.BlockSpec((TB, DV), lambda i, h: (i, h))
    return _pc(body, name=name, grid=(L // TB, RH),
               in_specs=[blk, blk, pl.BlockSpec((TB, DV), lambda i, h: (i, gc0 + h)),
                         pl.BlockSpec((1, DV), lambda i, h: (0, h))],
               out_specs=blk, out_shape=jax.ShapeDtypeStruct((L, RH * DV), BF16),
               compiler_params=_cp(("parallel", "parallel")))(*o2, P, gn_g)


def ggn_bwd(cfg, name, dout, o2, P, gn_g, col0):
    L = P.shape[0]
    TB, DV, RH = _rb(cfg), cfg.RDV, cfg.RH
    gc0 = _offsets(cfg)["lg"] // DV

    def body(d_ref, of_ref, ob_ref, gate_ref, g_ref, do_ref, dgate_ref, dg_ref):
        i = pl.program_id(1)
        o = of_ref[...] + ob_ref[...]
        mu = jnp.mean(o, axis=-1, keepdims=True)
        xc = o - mu
        var = jnp.mean(xc * xc, axis=-1, keepdims=True)
        r = lax.rsqrt(var + EPS)
        y = xc * r
        g = g_ref[...]
        gate = gate_ref[...]
        d = d_ref[...]
        dgate_ref[...] = (d * (y * g) * _dsilu(gate)).astype(dgate_ref.dtype)
        dyg = d * _silu(gate)
        s = jnp.sum(dyg * y, axis=0, keepdims=True)

        @pl.when(i == 0)
        def _():
            dg_ref[...] = s

        @pl.when(i > 0)
        def _():
            dg_ref[...] += s

        dy = dyg * g
        do_ref[...] = r * (dy - jnp.mean(dy, axis=-1, keepdims=True)
                           - y * jnp.mean(dy * y, axis=-1, keepdims=True))

    blk = pl.BlockSpec((TB, DV), lambda h, i: (i, h))
    return _pc(body, name=name, grid=(RH, L // TB),
               in_specs=[pl.BlockSpec((TB, DV), lambda h, i: (i, col0 + h)), blk, blk,
                         pl.BlockSpec((TB, DV), lambda h, i: (i, gc0 + h)),
                         pl.BlockSpec((1, DV), lambda h, i: (0, h))],
               out_specs=[blk, blk, pl.BlockSpec((1, DV), lambda h, i: (0, h))],
               out_shape=[jax.ShapeDtypeStruct((L, RH * DV), F32), jax.ShapeDtypeStruct((L, RH * DV), BF16),
                          jax.ShapeDtypeStruct((1, RH * DV), F32)],
               compiler_params=_cp(("parallel", "arbitrary")))(dout, *o2, P, gn_g)


def cast_cols(cfg, name, src, col0, ncols, width):
    L = src.shape[0]
    TB = _rb(cfg)

    def body(s_ref, o_ref):
        o_ref[...] = s_ref[...].astype(o_ref.dtype)

    spec = pl.BlockSpec((TB, width), lambda i, j: (i, col0 + j))
    return _pc(body, name=name, grid=(L // TB, ncols), in_specs=[spec],
               out_specs=pl.BlockSpec((TB, width), lambda i, j: (i, j)),
               out_shape=jax.ShapeDtypeStruct((L, ncols * width), BF16),
               compiler_params=_cp(("parallel", "parallel")))(src)


_CPAD = 16


def _conv_windows(cfg):
    T, TC, TB = cfg.T, cfg.TC, cfg.TB
    assert TC % TB == 0 and T % TB == 0 and cfg.CK // 2 < _CPAD
    return T // TB, [(T + j * TB, T + _CPAD + j * TB) for j in range(TC // TB)]


def _fill_padded(cfg, pb, get):
    T, TC, TB = cfg.T, cfg.TC, cfg.TB
    z = jnp.zeros((_CPAD, LANE), F32)
    pb[0:_CPAD, :] = z
    pb[_CPAD + T:2 * _CPAD + T, :] = z
    pb[2 * _CPAD + T + TC:3 * _CPAD + T + TC, :] = z

    def fill(i, c):
        r0 = pl.multiple_of(i * TB, TB)
        pb[pl.ds(r0 + _CPAD, TB), :] = get(r0)
        return c

    lax.fori_loop(0, T // TB, fill, 0)
    for j in range(TC // TB):
        pb[2 * _CPAD + T + j * TB:2 * _CPAD + T + (j + 1) * TB, :] = get(T + j * TB)


def _taps(win, TB):
    W = TB + 2 * _CPAD
    rot = {0: win}

    def tap(k):
        a, b = divmod(k + 1, 8)
        if b not in rot:
            rot[b] = pltpu.roll(win, W - b, 0)
        return rot[b][8 * a:8 * a + TB, :]

    return tap


def glu_dwconv_fwd(cfg, name, P, w, b):
    L = P.shape[0]
    T, TC, TB, K = cfg.T, cfg.TC, cfg.TB, cfg.CK
    off = _offsets(cfg)
    ca, cb = off["la"] // LANE, off["lb"] // LANE
    nlat, ctx_tiles = _conv_windows(cfg)
    PBL = 3 * _CPAD + T + TC

    def body(a_ref, b_ref, w_ref, bias_ref, y_ref, pb):
        _fill_padded(cfg, pb, lambda r0: a_ref[pl.ds(r0, TB), :] * _sigmoid(b_ref[pl.ds(r0, TB), :]))
        wv = w_ref[...]
        bias = bias_ref[...]

        def tile(win):
            tap = _taps(win, TB)
            acc = jnp.zeros((TB, LANE), F32) + bias
            for k in range(K):
                acc = acc + wv[k:k + 1, :] * tap(k)
            return acc

        def lat(i, c):
            r0 = pl.multiple_of(i * TB, TB)
            y_ref[pl.ds(r0, TB), :] = tile(pb[pl.ds(r0, TB + 2 * _CPAD), :])
            return c

        lax.fori_loop(0, nlat, lat, 0)
        for r0, w0 in ctx_tiles:
            y_ref[r0:r0 + TB, :] = tile(pb[w0:w0 + TB + 2 * _CPAD, :])

    return _pc(body, name=name, grid=(cfg.CW // LANE,),
               in_specs=[pl.BlockSpec((L, LANE), lambda j: (0, ca + j)),
                         pl.BlockSpec((L, LANE), lambda j: (0, cb + j)),
                         pl.BlockSpec((32, LANE), lambda j: (0, j)),
                         pl.BlockSpec((1, LANE), lambda j: (0, j))],
               out_specs=pl.BlockSpec((L, LANE), lambda j: (0, j)),
               out_shape=jax.ShapeDtypeStruct((L, cfg.CW), F32),
               scratch_shapes=[pltpu.VMEM((PBL, LANE), F32)],
               compiler_params=_cp(("parallel",)))(P, P, w, b)


def glu_dwconv_bwd(cfg, name, P, w, dy):
    L = P.shape[0]
    T, TC, TB, K = cfg.T, cfg.TC, cfg.TB, cfg.CK
    off = _offsets(cfg)
    ca, cb = off["la"] // LANE, off["lb"] // LANE
    nlat, ctx_tiles = _conv_windows(cfg)
    PBL = 3 * _CPAD + T + TC

    def body(a_ref, b_ref, w_ref, dy_ref, da_ref, db_ref, dw_ref, dbias_ref, pbu, pbd):
        _fill_padded(cfg, pbu, lambda r0: a_ref[pl.ds(r0, TB), :] * _sigmoid(b_ref[pl.ds(r0, TB), :]))
        _fill_padded(cfg, pbd, lambda r0: dy_ref[pl.ds(r0, TB), :])
        wv = w_ref[...]
        dw_ref[...] = jnp.zeros_like(dw_ref)
        dbias_ref[...] = jnp.zeros_like(dbias_ref)

        def tile(r0, winu, wind):
            tapu = _taps(winu, TB)
            tapd = _taps(wind, TB)
            dyt = dy_ref[pl.ds(r0, TB), :]
            du = jnp.zeros((TB, LANE), F32)
            for k in range(K):
                du = du + wv[k:k + 1, :] * tapd(K - 1 - k)
                dw_ref[k:k + 1, :] += jnp.sum(dyt * tapu(k), axis=0, keepdims=True)
            dbias_ref[...] += jnp.sum(dyt, axis=0, keepdims=True)
            a = a_ref[pl.ds(r0, TB), :]
            sg = _sigmoid(b_ref[pl.ds(r0, TB), :])
            da_ref[pl.ds(r0, TB), :] = (du * sg).astype(da_ref.dtype)
            db_ref[pl.ds(r0, TB), :] = (du * a * sg * (1.0 - sg)).astype(db_ref.dtype)

        def lat(i, c):
            r0 = pl.multiple_of(i * TB, TB)
            tile(r0, pbu[pl.ds(r0, TB + 2 * _CPAD), :], pbd[pl.ds(r0, TB + 2 * _CPAD), :])
            return c

        lax.fori_loop(0, nlat, lat, 0)
        for r0, w0 in ctx_tiles:
            tile(r0, pbu[w0:w0 + TB + 2 * _CPAD, :], pbd[w0:w0 + TB + 2 * _CPAD, :])

    col = pl.BlockSpec((L, LANE), lambda j: (0, j))
    return _pc(body, name=name, grid=(cfg.CW // LANE,),
               in_specs=[pl.BlockSpec((L, LANE), lambda j: (0, ca + j)),
                         pl.BlockSpec((L, LANE), lambda j: (0, cb + j)),
                         pl.BlockSpec((32, LANE), lambda j: (0, j)), col],
               out_specs=[col, col, pl.BlockSpec((32, LANE), lambda j: (0, j)),
                          pl.BlockSpec((1, LANE), lambda j: (0, j))],
               out_shape=[jax.ShapeDtypeStruct((L, cfg.CW), BF16), jax.ShapeDtypeStruct((L, cfg.CW), BF16),
                          jax.ShapeDtypeStruct((32, cfg.CW), F32), jax.ShapeDtypeStruct((1, cfg.CW), F32)],
               scratch_shapes=[pltpu.VMEM((PBL, LANE), F32), pltpu.VMEM((PBL, LANE), F32)],
               compiler_params=_cp(("parallel",)))(P, P, w, dy)


def ln_silu_fwd(cfg, name, y, g, b):
    L, W = y.shape
    TB = cfg.TB

    def body(y_ref, g_ref, b_ref, o_ref):
        yv = y_ref[...]
        mu = jnp.mean(yv, axis=-1, keepdims=True)
        xc = yv - mu
        var = jnp.mean(xc * xc, axis=-1, keepdims=True)
        z = xc * lax.rsqrt(var + EPS) * g_ref[...] + b_ref[...]
        o_ref[...] = _silu(z).astype(o_ref.dtype)

    row = pl.BlockSpec((TB, W), lambda i: (i, 0))
    vec = pl.BlockSpec((1, W), lambda i: (0, 0))
    return _pc(body, name=name, grid=(L // TB,), in_specs=[row, vec, vec], out_specs=row,
               out_shape=jax.ShapeDtypeStruct((L, W), BF16), compiler_params=_cp(("parallel",)))(y, g, b)


def ln_silu_bwd(cfg, name, dact, y, g, b):
    L, W = y.shape
    TB = cfg.TB

    def body(d_ref, y_ref, g_ref, b_ref, dy_ref, dg_ref, db_ref):
        i = pl.program_id(0)
        yv = y_ref[...]
        mu = jnp.mean(yv, axis=-1, keepdims=True)
        xc = yv - mu
        var = jnp.mean(xc * xc, axis=-1, keepdims=True)
        r = lax.rsqrt(var + EPS)
        yh = xc * r
        g = g_ref[...]
        z = yh * g + b_ref[...]
        dz = d_ref[...] * _dsilu(z)
        sg = jnp.sum(dz * yh, axis=0, keepdims=True)
        sb = jnp.sum(dz, axis=0, keepdims=True)

        @pl.when(i == 0)
        def _():
            dg_ref[...] = sg
            db_ref[...] = sb

        @pl.when(i > 0)
        def _():
            dg_ref[...] += sg
            db_ref[...] += sb

        dh = dz * g
        dy_ref[...] = r * (dh - jnp.mean(dh, axis=-1, keepdims=True)
                           - yh * jnp.mean(dh * yh, axis=-1, keepdims=True))

    row = pl.BlockSpec((TB, W), lambda i: (i, 0))
    vec = pl.BlockSpec((1, W), lambda i: (0, 0))
    return _pc(body, name=name, grid=(L // TB,), in_specs=[row, row, vec, vec], out_specs=[row, vec, vec],
               out_shape=[jax.ShapeDtypeStruct((L, W), F32), jax.ShapeDtypeStruct((1, W), F32),
                          jax.ShapeDtypeStruct((1, W), F32)],
               compiler_params=_cp(("arbitrary",)))(dact, y, g, b)


def _na_geometry(cfg):
    R = cfg.T // cfg.GW
    nb = R // cfg.NAR
    assert nb >= 3 and cfg.GW == 64 and cfg.NAR == 8
    ks = [int(np.clip(8 * b - 4, 0, R - 16)) for b in range(nb)]
    return R, nb, ks


_NTAB = 18


def _split3(x):
    hi = x.astype(BF16)
    r = x - hi.astype(F32)
    mid = r.astype(BF16)
    lo = (r - mid.astype(F32)).astype(BF16)
    return hi, mid, lo


def _na_col_onehot(cfg):
    GW, NAC = cfg.GW, cfg.NAC
    qc = np.arange(GW)[:, None]
    kc = np.arange(GW)[None, :]
    cs = np.clip(qc - NAC // 2, 0, GW - NAC)
    vcol = (kc >= cs) & (kc < cs + NAC)
    dd = np.clip(kc - qc + NAC - 1, 0, 2 * NAC - 2)
    oh = (np.arange(LANE)[:, None, None] == dd[None]).astype(np.float32)
    z = np.zeros_like(oh)
    oda = np.concatenate([oh, z], axis=2).reshape(LANE, GW * LANE)
    odb = np.concatenate([z, oh], axis=2).reshape(LANE, GW * LANE)
    cm = np.where(np.concatenate([vcol, vcol], axis=1), 0.0, NEG).astype(np.float32).reshape(1, GW * LANE)
    return oda, odb, cm


def na_tables(cfg, name, rpb):
    NH, GW = cfg.NH, cfg.GW
    na = rpb.shape[1]
    oda, odb, cm = _na_col_onehot(cfg)
    rp = jnp.zeros((NH, _NTAB + 1, LANE), F32).at[:, 1:1 + na, :rpb.shape[2]].set(rpb.astype(F32))
    r0 = rp[:, :_NTAB].reshape(NH * _NTAB, LANE)
    r1 = rp[:, 1:].reshape(NH * _NTAB, LANE)
    a = np.arange(_NTAB) - 1
    rm0 = np.where((a >= 0) & (a < na), 0.0, NEG).astype(np.float32)
    rm1 = np.where((a + 1 >= 0) & (a + 1 < na), 0.0, NEG).astype(np.float32)
    half = (np.arange(GW * LANE) % LANE >= GW)[None, :]
    rmask = np.where(half, np.tile(rm1, NH)[:, None], np.tile(rm0, NH)[:, None]).astype(np.float32)
    tn = 2048
    rows = NH * _NTAB

    def body(r0_ref, r1_ref, a_ref, b_ref, cm_ref, rm_ref, o_ref):
        acc = cm_ref[...] + rm_ref[...]
        for t in _split3(r0_ref[...]):
            acc = acc + dot_nn(t, a_ref[...])
        for t in _split3(r1_ref[...]):
            acc = acc + dot_nn(t, b_ref[...])
        o_ref[...] = acc

    rs = pl.BlockSpec((rows, LANE), lambda n: (0, 0))
    out = _pc(body, name=name, grid=(GW * LANE // tn,),
              in_specs=[rs, rs, pl.BlockSpec((LANE, tn), lambda n: (0, n)), pl.BlockSpec((LANE, tn), lambda n: (0, n)),
                        pl.BlockSpec((1, tn), lambda n: (0, n)), pl.BlockSpec((rows, tn), lambda n: (0, n))],
              out_specs=pl.BlockSpec((rows, tn), lambda n: (0, n)),
              out_shape=jax.ShapeDtypeStruct((rows, GW * LANE), F32),
              compiler_params=_cp(("parallel",)))(r0, r1, jnp.asarray(oda, BF16), jnp.asarray(odb, BF16),
                                                  jnp.asarray(cm), jnp.asarray(rmask))
    return out.reshape(NH, _NTAB, GW, LANE)


def _na_tiles(cfg, b):
    R, nb, _ = _na_geometry(cfg)
    NAR = cfg.NAR
    ksb = jnp.clip(8 * b - 4, 0, R - 16)
    for i in range(8):
        qr = 8 * b + i
        ws = jnp.clip(qr - NAR // 2, 0, R - NAR)
        for J in range(8):
            kr0 = ksb + 2 * J
            row = jnp.clip(kr0 - qr + NAR - 1, -1, _NTAB - 2) + 1
            v0 = jnp.logical_and(kr0 >= ws, kr0 < ws + NAR)
            v1 = jnp.logical_and(kr0 + 1 >= ws, kr0 + 1 < ws + NAR)
            yield i, J, row, v0, v1


def _na_fill_bias(cfg, tab_ref, bias, b):
    GW = cfg.GW
    first = lax.broadcasted_iota(jnp.int32, (GW, LANE), 1) < GW
    for i, J, row, v0, v1 in _na_tiles(cfg, b):
        ok = jnp.where(first, v0.astype(jnp.int32), v1.astype(jnp.int32))
        bias[i * GW:(i + 1) * GW, J * LANE:(J + 1) * LANE] = jnp.where(ok > 0, tab_ref[row], NEG)


def _na_specs(cfg):
    R, nb, ks = _na_geometry(cfg)
    off = _offsets(cfg)
    TQ = 8 * cfg.GW
    KP = 4 * cfg.GW
    ks4 = [k // 4 for k in ks]
    lat_blocks = cfg.T // KP

    def ks4_of(b):
        return jnp.clip(2 * b - 1, 0, R // 4 - 4)

    assert all(int(np.clip(2 * b - 1, 0, R // 4 - 4)) == ks4[b] for b in range(nb))
    assert cfg.TC == KP

    def col(nm):
        c0 = off[nm] // LANE
        q = pl.BlockSpec((TQ, LANE), lambda h, b: (b, c0 + h))
        parts = [pl.BlockSpec((KP, LANE), functools.partial(lambda h, b, t: (ks4_of(b) + t, c0 + h), t=t))
                 for t in range(4)]
        ctx = pl.BlockSpec((KP, LANE), lambda h, b: (lat_blocks, c0 + h))
        return q, parts, ctx

    return nb, TQ, KP, ks4_of, col


def na_fwd(cfg, name, P, tab):
    nb, TQ, KP, ks4_of, col = _na_specs(cfg)
    NH = cfg.NH
    scale = cfg.NDH ** -0.5
    qs, _, _ = col("nq")
    _, kparts, kctx = col("nk")
    _, vparts, vctx = col("nv")

    def body(q_ref, k0, k1, k2, k3, kc_ref, v0, v1, v2, v3, vc_ref, tab_ref, o_ref, lse_ref, bias_ref):
        _na_fill_bias(cfg, tab_ref, bias_ref, pl.program_id(1))
        q = (q_ref[...] * scale).astype(BF16)
        kl = jnp.concatenate([k0[...], k1[...], k2[...], k3[...]], axis=0).astype(BF16)
        vl = jnp.concatenate([v0[...], v1[...], v2[...], v3[...]], axis=0).astype(BF16)
        kc = kc_ref[...].astype(BF16)
        vc = vc_ref[...].astype(BF16)
        sl = dot_nt(q, kl) + bias_ref[...]
        sc = dot_nt(q, kc)
        m = jnp.maximum(jnp.max(sl, axis=-1, keepdims=True), jnp.max(sc, axis=-1, keepdims=True))
        pl_ = jnp.exp(sl - m)
        pc = jnp.exp(sc - m)
        den = jnp.sum(pl_, axis=-1, keepdims=True) + jnp.sum(pc, axis=-1, keepdims=True)
        o = dot_nn(pl_.astype(BF16), vl) + dot_nn(pc.astype(BF16), vc)
        o_ref[...] = o / den
        lse_ref[...] = m + jnp.log(den)

    return _pc(body, name=name, grid=(NH, nb),
               in_specs=[qs, *kparts, kctx, *vparts, vctx,
                         pl.BlockSpec((None, _NTAB, cfg.GW, LANE), lambda h, b: (h, 0, 0, 0))],
               out_specs=[pl.BlockSpec((TQ, LANE), lambda h, b: (b, h)),
                          pl.BlockSpec((None, TQ, 1), lambda h, b: (h, b, 0))],
               out_shape=[jax.ShapeDtypeStruct((cfg.T, NH * LANE), F32),
                          jax.ShapeDtypeStruct((NH, cfg.T, 1), F32)],
               scratch_shapes=[pltpu.VMEM((TQ, 4 * KP), F32)],
               compiler_params=_cp(("parallel", "parallel")))(P, *([P] * 5), *([P] * 5), tab)


def na_bwd(cfg, name, P, tab, o, lse, dmix, dcol0):
    nb, TQ, KP, ks4_of, col = _na_specs(cfg)
    NH, GW = cfg.NH, cfg.GW
    L = P.shape[0]
    scale = cfg.NDH ** -0.5
    qs, _, _ = col("nq")
    _, kparts, kctx = col("nk")
    _, vparts, vctx = col("nv")

    def body(q_ref, k0, k1, k2, k3, kc_ref, v0, v1, v2, v3, vc_ref, tab_ref, o_ref, lse_ref, do_ref,
             dq_ref, dk_ref, dv_ref, dtab_ref, bias_ref):
        b = pl.program_id(1)

        @pl.when(b == 0)
        def _():
            dk_ref[...] = jnp.zeros_like(dk_ref)
            dv_ref[...] = jnp.zeros_like(dv_ref)
            dtab_ref[...] = jnp.zeros_like(dtab_ref)

        _na_fill_bias(cfg, tab_ref, bias_ref, b)

        q = (q_ref[...] * scale).astype(BF16)
        kl = jnp.concatenate([k0[...], k1[...], k2[...], k3[...]], axis=0).astype(BF16)
        vl = jnp.concatenate([v0[...], v1[...], v2[...], v3[...]], axis=0).astype(BF16)
        kc = kc_ref[...].astype(BF16)
        vc = vc_ref[...].astype(BF16)
        lse = lse_ref[...]
        do = do_ref[...]
        dob = do.astype(BF16)
        p_l = jnp.exp(dot_nt(q, kl) + bias_ref[...] - lse)
        p_c = jnp.exp(dot_nt(q, kc) - lse)
        delta = jnp.sum(do * o_ref[...], axis=-1, keepdims=True)
        ds_l = p_l * (dot_nt(dob, vl) - delta)
        ds_c = p_c * (dot_nt(dob, vc) - delta)
        dslb = ds_l.astype(BF16)
        dscb = ds_c.astype(BF16)
        dq_ref[...] = ((dot_nn(dslb, kl) + dot_nn(dscb, kc)) * scale).astype(dq_ref.dtype)
        r0 = pl.multiple_of(ks4_of(b) * KP, KP)
        dk_ref[pl.ds(r0, 4 * KP), :] += dot_tn(dslb, q)
        dv_ref[pl.ds(r0, 4 * KP), :] += dot_tn(p_l.astype(BF16), dob)
        dk_ref[cfg.T:cfg.T + KP, :] += dot_tn(dscb, q)
        dv_ref[cfg.T:cfg.T + KP, :] += dot_tn(p_c.astype(BF16), dob)
        bias_ref[...] = ds_l
        for i, J, row, _, _ in _na_tiles(cfg, b):
            dtab_ref[row] += bias_ref[i * GW:(i + 1) * GW, J * LANE:(J + 1) * LANE]

    full = pl.BlockSpec((L, LANE), lambda h, b: (0, h))
    tabs = pl.BlockSpec((None, _NTAB, GW, LANE), lambda h, b: (h, 0, 0, 0))
    return _pc(body, name=name, grid=(NH, nb),
               in_specs=[qs, *kparts, kctx, *vparts, vctx, tabs,
                         pl.BlockSpec((TQ, LANE), lambda h, b: (b, h)),
                         pl.BlockSpec((None, TQ, 1), lambda h, b: (h, b, 0)),
                         pl.BlockSpec((TQ, LANE), lambda h, b: (b, dcol0 + h))],
               out_specs=[pl.BlockSpec((TQ, LANE), lambda h, b: (b, h)), full, full, tabs],
               out_shape=[jax.ShapeDtypeStruct((cfg.T, NH * LANE), BF16),
                          jax.ShapeDtypeStruct((L, NH * LANE), F32), jax.ShapeDtypeStruct((L, NH * LANE), F32),
                          jax.ShapeDtypeStruct((NH, _NTAB, GW, LANE), F32)],
               scratch_shapes=[pltpu.VMEM((TQ, 4 * KP), F32)],
               compiler_params=_cp(("parallel", "arbitrary")))(
                   P, *([P] * 5), *([P] * 5), tab, o, lse, dmix)


def na_ctx_fwd(cfg, name, P):
    off = _offsets(cfg)
    TC, NH = cfg.TC, cfg.NH
    rb = cfg.T // TC
    scale = cfg.NDH ** -0.5

    def body(q_ref, k_ref, v_ref, o_ref, lse_ref):
        q = (q_ref[...] * scale).astype(BF16)
        s = dot_nt(q, k_ref[...].astype(BF16))
        m = jnp.max(s, axis=-1, keepdims=True)
        p = jnp.exp(s - m)
        den = jnp.sum(p, axis=-1, keepdims=True)
        o_ref[...] = dot_nn(p.astype(BF16), v_ref[...].astype(BF16)) / den
        lse_ref[...] = m + jnp.log(den)

    spec = lambda nm: pl.BlockSpec((TC, LANE), functools.partial(lambda h, c0: (rb, c0 + h), c0=off[nm] // LANE))
    return _pc(body, name=name, grid=(NH,), in_specs=[spec("nq"), spec("nk"), spec("nv")],
               out_specs=[pl.BlockSpec((TC, LANE), lambda h: (0, h)), pl.BlockSpec((None, TC, 1), lambda h: (h, 0, 0))],
               out_shape=[jax.ShapeDtypeStruct((TC, NH * LANE), F32), jax.ShapeDtypeStruct((NH, TC, 1), F32)],
               compiler_params=_cp(("parallel",)))(P, P, P)


def na_ctx_bwd(cfg, name, P, o, lse, dmix, dcol0, dk_in, dv_in):
    off = _offsets(cfg)
    TC, NH = cfg.TC, cfg.NH
    rb = cfg.T // TC
    scale = cfg.NDH ** -0.5

    def body(q_ref, k_ref, v_ref, o_ref, lse_ref, do_ref, dki_ref, dvi_ref, dq_ref, dk_ref, dv_ref):
        q = (q_ref[...] * scale).astype(BF16)
        kb = k_ref[...].astype(BF16)
        vb = v_ref[...].astype(BF16)
        do = do_ref[...]
        dob = do.astype(BF16)
        p = jnp.exp(dot_nt(q, kb) - lse_ref[...])
        delta = jnp.sum(do * o_ref[...], axis=-1, keepdims=True)
        ds = (p * (dot_nt(dob, vb) - delta)).astype(BF16)
        dq_ref[...] = (dot_nn(ds, kb) * scale).astype(dq_ref.dtype)
        dk_ref[...] = (dki_ref[...] + dot_tn(ds, q)).astype(dk_ref.dtype)
        dv_ref[...] = (dvi_ref[...] + dot_tn(p.astype(BF16), dob)).astype(dv_ref.dtype)

    spec = lambda nm: pl.BlockSpec((TC, LANE), functools.partial(lambda h, c0: (rb, c0 + h), c0=off[nm] // LANE))
    hb = pl.BlockSpec((TC, LANE), lambda h: (0, h))
    ctxrow = pl.BlockSpec((TC, LANE), lambda h: (rb, h))
    shp = jax.ShapeDtypeStruct((TC, NH * LANE), BF16)
    return _pc(body, name=name, grid=(NH,),
               in_specs=[spec("nq"), spec("nk"), spec("nv"), hb, pl.BlockSpec((None, TC, 1), lambda h: (h, 0, 0)),
                         pl.BlockSpec((TC, LANE), lambda h: (rb, dcol0 + h)), ctxrow, ctxrow],
               out_specs=[hb, hb, hb], out_shape=[shp, shp, shp],
               compiler_params=_cp(("parallel",)))(P, P, P, o, lse, dmix, dk_in, dv_in)


def na_rpb_grad(cfg, name, dtab):
    NH, GW = cfg.NH, cfg.GW
    na, nd = 2 * cfg.NAR - 1, 2 * cfg.NAC - 1
    oda, odb, _ = _na_col_onehot(cfg)
    E = np.concatenate([oda.T, odb.T], axis=1)
    rows = NH * _NTAB

    def body(z_ref, e_ref, o_ref):
        zv = z_ref[...]
        hi = zv.astype(BF16)
        lo = (zv - hi.astype(F32)).astype(BF16)
        e = e_ref[...]
        o_ref[...] = dot_nn(hi, e) + dot_nn(lo, e)

    g = _pc(body, name=name, out_shape=jax.ShapeDtypeStruct((rows, 2 * LANE), F32),
            compiler_params=_cp())(dtab.reshape(rows, GW * LANE), jnp.asarray(E, BF16))
    g = g.reshape(NH, _NTAB, 2, LANE)
    return g[:, 1:1 + na, 0, :nd] + g[:, 0:na, 1, :nd]


def _seq_tiles(cfg):
    T, TC, TB = cfg.T, cfg.TC, cfg.TB
    tiles = []
    for i in range((T + TC) // TB):
        r0 = i * TB
        tiles.append((r0, r0 == 0 or r0 == T, r0 + TB == T or r0 + TB == T + TC))
    return tiles


def _shift3(ref_get, r0, TB, start, end, width):
    cur = ref_get(r0, TB)
    if start or end:
        rowi = lax.broadcasted_iota(jnp.int32, (TB, width), 0)
    up = jnp.where(rowi == 0, 0.0, pltpu.roll(cur, 1, 0)) if start else ref_get(r0 - 1, TB)
    dn = jnp.where(rowi == TB - 1, 0.0, pltpu.roll(cur, TB - 1, 0)) if end else ref_get(r0 + 1, TB)
    return up, cur, dn


def ffn_act_fwd(cfg, name, U2, w, b):
    _, L, DFF = U2.shape
    TB = cfg.TB
    tiles = _seq_tiles(cfg)

    def body(u_ref, w_ref, b_ref, a_ref):
        def plane(p, r0, st, en):
            up, cur, dn = _shift3(lambda r, n: u_ref[p, r:r + n, :], r0, TB, st, en, LANE)
            wv = w_ref[p]
            return wv[0:1, :] * up + wv[1:2, :] * cur + wv[2:3, :] * dn + b_ref[p]

        for r0, st, en in tiles:
            val = plane(0, r0, st, en)
            gate = plane(1, r0, st, en)
            a_ref[r0:r0 + TB, :] = (_silu(gate) * val).astype(a_ref.dtype)

    return _pc(body, name=name, grid=(DFF // LANE,),
               in_specs=[pl.BlockSpec((2, L, LANE), lambda j: (0, 0, j)),
                         pl.BlockSpec((2, 8, LANE), lambda j: (0, 0, j)),
                         pl.BlockSpec((2, 1, LANE), lambda j: (0, 0, j))],
               out_specs=pl.BlockSpec((L, LANE), lambda j: (0, j)),
               out_shape=jax.ShapeDtypeStruct((L, DFF), BF16),
               compiler_params=_cp(("parallel",)))(U2, w, b)


def ffn_act_bwd(cfg, name, U2, w, b, dA):
    _, L, DFF = U2.shape
    TB = cfg.TB
    tiles = _seq_tiles(cfg)

    def body(u_ref, w_ref, b_ref, da_ref, du_ref, dw_ref, db_ref, dbuf):
        dw_ref[...] = jnp.zeros_like(dw_ref)
        db_ref[...] = jnp.zeros_like(db_ref)
        for r0, st, en in tiles:
            shifted = []
            pre = []
            for p in range(2):
                up, cur, dn = _shift3(lambda r, n: u_ref[p, r:r + n, :], r0, TB, st, en, LANE)
                wv = w_ref[p]
                shifted.append((up, cur, dn))
                pre.append(wv[0:1, :] * up + wv[1:2, :] * cur + wv[2:3, :] * dn + b_ref[p])
            val, gate = pre
            da = da_ref[r0:r0 + TB, :]
            dpre = (da * _silu(gate), da * val * _dsilu(gate))
            for p in range(2):
                dbuf[p, r0:r0 + TB, :] = dpre[p]
                for k in range(3):
                    dw_ref[p, k:k + 1, :] += jnp.sum(dpre[p] * shifted[p][k], axis=0, keepdims=True)
                db_ref[p] += jnp.sum(dpre[p], axis=0, keepdims=True)
        for r0, st, en in tiles:
            for p in range(2):
                up, cur, dn = _shift3(lambda r, n: dbuf[p, r:r + n, :], r0, TB, st, en, LANE)
                wv = w_ref[p]
                du_ref[p, r0:r0 + TB, :] = (wv[0:1, :] * dn + wv[1:2, :] * cur + wv[2:3, :] * up).astype(du_ref.dtype)

    blk = pl.BlockSpec((2, L, LANE), lambda j: (0, 0, j))
    wspec = pl.BlockSpec((2, 8, LANE), lambda j: (0, 0, j))
    bspec = pl.BlockSpec((2, 1, LANE), lambda j: (0, 0, j))
    return _pc(body, name=name, grid=(DFF // LANE,),
               in_specs=[blk, wspec, bspec, pl.BlockSpec((L, LANE), lambda j: (0, j))],
               out_specs=[blk, wspec, bspec],
               out_shape=[jax.ShapeDtypeStruct((2, L, DFF), BF16), jax.ShapeDtypeStruct((2, 8, DFF), F32),
                          jax.ShapeDtypeStruct((2, 1, DFF), F32)],
               scratch_shapes=[pltpu.VMEM((2, L, LANE), F32)],
               compiler_params=_cp(("parallel",)))(U2, w, b, dA)


def _tm(L, parts):
    assert L % parts == 0
    return L // parts


def layer_fwd(cfg, l, entry, mod, wts, small, tabs):
    fused = isinstance(entry, tuple)
    L, D = entry[0].shape if fused else entry.shape
    off = _offsets(cfg)
    DIN = off["end"]
    tmA = _tm(L, 4)
    sv = {"W": {}}

    def weight(name, after):
        sv["W"][name], tok = wts(name, after)
        return sv["W"][name], tok

    Win4, _ = weight("w_in", entry[1] if fused else entry)
    nbi = Win4.shape[3]
    if fused:
        XS, h1 = resid_norm_fwd(cfg, f"resid2_norm1_fwd_{l}", *entry, small["norm1_g"], mod["sc1"], mod["sh1"])
    else:
        XS, h1 = entry, norm_mod_fwd(cfg, f"norm1_fwd_{l}", entry, small["norm1_g"], mod["sc1"], mod["sh1"])
    sv["XS"] = XS
    P = matmul(f"mm_in_{l}", h1, Win4, contract="nn", grid=(4, L // tmA),
               a_spec=pl.BlockSpec((tmA, D), lambda n, m: (m, 0)),
               b_spec=pl.BlockSpec((None, None, D, nbi), lambda n, m: (n, l, 0, 0)),
               out_shape=jax.ShapeDtypeStruct((L, DIN), F32),
               out_spec=pl.BlockSpec((tmA, nbi), lambda n, m: (m, n)), nk=1)
    qr, kr = rope_fwd(cfg, f"rope_fwd_{l}", P, tabs["cos"], tabs["sin"])
    o_f, o_b, st = retention_fwd(cfg, f"ret_fwd_{l}", qr, kr, P, small["lam"])
    o2 = (o_f, o_b)
    ret = ggn_fwd(cfg, f"ggn_fwd_{l}", o2, P, small["ret_gn_g"])
    ycv = glu_dwconv_fwd(cfg, f"dwconv_fwd_{l}", P, small["conv_dw_w"], small["conv_dw_b"])
    act = ln_silu_fwd(cfg, f"ln_silu_fwd_{l}", ycv, small["conv_ln_g"], small["conv_ln_b"])
    Wpw4, _ = weight("conv_pw", act)
    cv = mm_rowsharded(f"mm_pw_{l}", act, Wpw4, l, BF16, cfg.CW)
    bias = na_tables(cfg, f"na_tables_{l}", small["na_rpb"])
    na_l, lse = na_fwd(cfg, f"na_fwd_{l}", P, bias)
    na_c, lse_c = na_ctx_fwd(cfg, f"na_ctx_fwd_{l}", P)
    mix = (ret, cv, jnp.concatenate([na_l, na_c], axis=0).astype(BF16))
    Wout4, tok = weight("w_out", ret)
    Y1 = mm_rowsharded(f"mm_out_{l}", mix, Wout4, l, F32, D)
    XM, h2 = resid_norm_fwd(cfg, f"resid1_norm2_fwd_{l}", XS, Y1, mod["g1"] if tok is None else mod["g1"] + tok,
                            small["norm2_g"], mod["sc2"], mod["sh2"])
    Wup4, _ = weight("ffn_up", h2)
    nbu = Wup4.shape[3]
    tnu = nbu // 2
    U2 = matmul(f"mm_up_{l}", h2, Wup4, contract="nn", grid=(8, L // tmA),
                a_spec=pl.BlockSpec((tmA, D), lambda n, m: (m, 0)),
                b_spec=pl.BlockSpec((None, None, D, tnu), lambda n, m: (n // 2, l, 0, n % 2)),
                out_shape=jax.ShapeDtypeStruct((2, L, cfg.DFF), F32),
                out_spec=pl.BlockSpec((None, tmA, tnu), lambda n, m: (n // 4, m, n % 4)), nk=1)
    A = ffn_act_fwd(cfg, f"ffn_act_fwd_{l}", U2, small["ffn_dw_w"], small["ffn_dw_b"])
    Wdn4, _ = weight("ffn_down", A)
    Y2 = mm_rowsharded(f"mm_down_{l}", A, Wdn4, l, F32, D // 2)
    sv.update(h1=h1, P=P, qr=qr, kr=kr, o2=o2, st=st, ycv=ycv, act=act, bias=bias, na_l=na_l, lse=lse,
              na_c=na_c, lse_c=lse_c, mix=mix, Y1=Y1, XM=XM, h2=h2, U2=U2, A=A, Y2=Y2)
    return (XM, Y2, mod["g2"]), sv


GRAD_GROUPS = (("ffn_down", "ffn_up"), ("w_out", "conv_pw", "w_in"))


def layer_bwd(cfg, l, dXO, sv, mod, wts, small, tabs, gbuf, ready):
    L, D = dXO.shape
    off = _offsets(cfg)
    DIN = off["end"]
    Win4, Wout4, Wup4, Wdn4, Wpw4 = wts["w_in"], wts["w_out"], wts["ffn_up"], wts["ffn_down"], wts["conv_pw"]
    tmA, tmB = _tm(L, 4), _tm(L, 8)
    depth = Win4.shape[1]
    gb, gs, dm = {}, {}, {}
    P = sv["P"]
    dY2, dm["g2"] = resid_bwd(cfg, f"resid2_bwd_{l}", dXO, sv["Y2"], mod["g2"])
    nbd = Wdn4.shape[2]
    dA = matmul(f"mm_down_da_{l}", dY2, Wdn4, contract="nt", grid=(4, L // tmA),
                a_spec=pl.BlockSpec((tmA, D), lambda j, m: (m, 0)),
                b_spec=pl.BlockSpec((None, None, nbd, D), lambda j, m: (j, l, 0, 0)),
                out_shape=jax.ShapeDtypeStruct((L, cfg.DFF), F32),
                out_spec=pl.BlockSpec((tmA, nbd), lambda j, m: (m, j)), nk=1)
    gb["ffn_down"] = wgrad(cfg, f"mm_down_dw_{l}", sv["A"], dY2,
                           lambda rb, ri: pl.BlockSpec((rb, nbd), lambda j, m: (ri(m), j)),
                           lambda rb, ri: pl.BlockSpec((rb, D), lambda j, m: (ri(m), 0)),
                           jax.ShapeDtypeStruct((depth, 4, nbd, D), BF16),
                           pl.BlockSpec((None, None, nbd, D), lambda j, m: (l, j, 0, 0)), 4, gbuf.get("ffn_down"))
    dU2, dfw, dfb = ffn_act_bwd(cfg, f"ffn_act_bwd_{l}", sv["U2"], small["ffn_dw_w"], small["ffn_dw_b"], dA)
    gs["ffn_dw_w"], gs["ffn_dw_b"] = dfw, dfb
    nbu = Wup4.shape[3]
    tnu = nbu // 2
    dH2 = matmul(f"mm_up_dh_{l}", dU2, Wup4, contract="nt", grid=(L // tmA, 8),
                 a_spec=pl.BlockSpec((None, tmA, tnu), lambda m, n: (n // 4, m, n % 4)),
                 b_spec=pl.BlockSpec((None, None, D, tnu), lambda m, n: (n // 2, l, 0, n % 2)),
                 out_shape=jax.ShapeDtypeStruct((L, D), F32),
                 out_spec=pl.BlockSpec((tmA, D), lambda m, n: (m, 0)), nk=8)
    gb["ffn_up"] = wgrad(cfg, f"mm_up_dw_{l}", sv["h2"], dU2,
                         lambda rb, ri: pl.BlockSpec((rb, D), lambda n, m: (ri(m), 0)),
                         lambda rb, ri: pl.BlockSpec((None, rb, tnu), lambda n, m: (n // 4, ri(m), n % 4)),
                         jax.ShapeDtypeStruct((depth, 4, D, nbu), BF16),
                         pl.BlockSpec((None, None, D, tnu), lambda n, m: (l, n // 2, 0, n % 2)), 8, gbuf.get("ffn_up"))
    dXM, dm["sc2"], dm["sh2"], gs["norm2_g"] = norm_mod_bwd(
        cfg, f"norm2_bwd_{l}", dH2, sv["XM"], small["norm2_g"], mod["sc2"], dXO)
    tok = ready(GRAD_GROUPS[0], gb)
    dY1, dm["g1"] = resid_bwd(cfg, f"resid1_bwd_{l}", dXM, sv["Y1"], mod["g1"] if tok is None else mod["g1"] + tok)
    nbo = Wout4.shape[2]
    dmix = matmul(f"mm_out_dmix_{l}", dY1, Wout4, contract="nt", grid=(4, L // tmA),
                  a_spec=pl.BlockSpec((tmA, D), lambda j, m: (m, 0)),
                  b_spec=pl.BlockSpec((None, None, nbo, D), lambda j, m: (j, l, 0, 0)),
                  out_shape=jax.ShapeDtypeStruct((L, D), F32),
                  out_spec=pl.BlockSpec((tmA, nbo), lambda j, m: (m, j)), nk=1)
    gw, j0 = gbuf.get("w_out"), 0
    for pi, piece in enumerate(sv["mix"]):
        nblk = piece.shape[1] // nbo
        gw = wgrad(cfg, f"mm_out_dw_{l}_{pi}", piece, dY1,
                   lambda rb, ri: pl.BlockSpec((rb, nbo), lambda j, m: (ri(m), j)),
                   lambda rb, ri: pl.BlockSpec((rb, D), lambda j, m: (ri(m), 0)),
                   jax.ShapeDtypeStruct((depth, 4, nbo, D), BF16),
                   pl.BlockSpec((None, None, nbo, D), functools.partial(lambda j, m, j0: (l, j + j0, 0, 0), j0=j0)),
                   nblk, gw)
        j0 += nblk
    gb["w_out"] = gw
    RW = cfg.RH * cfg.RDV
    do, dlg, gs["ret_gn_g"] = ggn_bwd(cfg, f"ggn_bwd_{l}", dmix, sv["o2"], P, small["ret_gn_g"], 0)
    dqf, dqb, dkf, dkb, dvf, dvb, dlam = retention_bwd(
        cfg, f"ret_bwd_{l}", sv["qr"], sv["kr"], P, small["lam"], sv["st"], do)
    gs["lam"] = dlam[:, :, 0, 0]
    dlq, dlk = rope_bwd(cfg, f"rope_bwd_{l}", (dqf, dqb), (dkf, dkb), tabs["cos"], tabs["sin"])
    dlv = add_cast(cfg, f"ret_dv_{l}", dvf, dvb)
    dcv = cast_cols(cfg, f"conv_dcv_{l}", dmix, RW // LANE, cfg.CW // LANE, LANE)
    nbp = Wpw4.shape[2]
    dact = matmul(f"mm_pw_dact_{l}", dcv, Wpw4, contract="nt", grid=(4, L // tmA),
                  a_spec=pl.BlockSpec((tmA, cfg.CW), lambda j, m: (m, 0)),
                  b_spec=pl.BlockSpec((None, None, nbp, cfg.CW), lambda j, m: (j, l, 0, 0)),
                  out_shape=jax.ShapeDtypeStruct((L, cfg.CW), F32),
                  out_spec=pl.BlockSpec((tmA, nbp), lambda j, m: (m, j)), nk=1)
    gb["conv_pw"] = wgrad(cfg, f"mm_pw_dw_{l}", sv["act"], dcv,
                          lambda rb, ri: pl.BlockSpec((rb, nbp), lambda j, m: (ri(m), j)),
                          lambda rb, ri: pl.BlockSpec((rb, cfg.CW), lambda j, m: (ri(m), 0)),
                          jax.ShapeDtypeStruct((depth, 4, nbp, cfg.CW), BF16),
                          pl.BlockSpec((None, None, nbp, cfg.CW), lambda j, m: (l, j, 0, 0)), 4, gbuf.get("conv_pw"))
    dycv, gs["conv_ln_g"], gs["conv_ln_b"] = ln_silu_bwd(
        cfg, f"ln_silu_bwd_{l}", dact, sv["ycv"], small["conv_ln_g"], small["conv_ln_b"])
    dla, dlb, gs["conv_dw_w"], gs["conv_dw_b"] = glu_dwconv_bwd(cfg, f"dwconv_bwd_{l}", P, small["conv_dw_w"], dycv)
    nac0 = (RW + cfg.CW) // LANE
    dnq_l, dnk, dnv, dsb = na_bwd(cfg, f"na_bwd_{l}", P, sv["bias"], sv["na_l"], sv["lse"], dmix, nac0)
    dnq_c, dnk_c, dnv_c = na_ctx_bwd(cfg, f"na_ctx_bwd_{l}", P, sv["na_c"], sv["lse_c"], dmix, nac0, dnk, dnv)
    gs["na_rpb"] = na_rpb_grad(cfg, f"na_rpb_{l}", dsb)
    dnq = jnp.concatenate([dnq_l, dnq_c], axis=0)
    dnk = jnp.concatenate([dnk[:cfg.T].astype(BF16), dnk_c], axis=0)
    dnv = jnp.concatenate([dnv[:cfg.T].astype(BF16), dnv_c], axis=0)
    dP = jnp.concatenate([dlq, dlk, dlv, dlg, dla, dlb, dnq, dnk, dnv], axis=1)
    nbi = Win4.shape[3]
    dH1 = matmul(f"mm_in_dh_{l}", dP, Win4, contract="nt", grid=(L // tmA, 4),
                 a_spec=pl.BlockSpec((tmA, nbi), lambda m, n: (m, n)),
                 b_spec=pl.BlockSpec((None, None, D, nbi), lambda m, n: (n, l, 0, 0)),
                 out_shape=jax.ShapeDtypeStruct((L, D), F32),
                 out_spec=pl.BlockSpec((tmA, D), lambda m, n: (m, 0)), nk=4)
    gb["w_in"] = wgrad(cfg, f"mm_in_dw_{l}", sv["h1"], dP,
                       lambda rb, ri: pl.BlockSpec((rb, D), lambda n, m: (ri(m), 0)),
                       lambda rb, ri: pl.BlockSpec((rb, nbi), lambda n, m: (ri(m), n)),
                       jax.ShapeDtypeStruct((depth, 4, D, nbi), BF16),
                       pl.BlockSpec((None, None, D, nbi), lambda n, m: (l, n, 0, 0)), 4, gbuf.get("w_in"))
    dXS, dm["sc1"], dm["sh1"], gs["norm1_g"] = norm_mod_bwd(
        cfg, f"norm1_bwd_{l}", dH1, sv["XS"], small["norm1_g"], mod["sc1"], dXM, latent_only=(l == 0))
    return dXS, gb, gs, dm, ready(GRAD_GROUPS[1], gb)


def _layer_small(cfg, l, sp):
    DFF = cfg.DFF
    fw = sp["ffn_dw_w"][l].reshape(3, 2, DFF).transpose(1, 0, 2)
    fw = jnp.concatenate([fw, jnp.zeros((2, 5, DFF), F32)], axis=1)
    cw = jnp.concatenate([sp["conv_dw_w"][l], jnp.zeros((32 - cfg.CK, cfg.CW), F32)], axis=0)
    return dict(
        norm1_g=sp["norm1_g"][l][None], norm2_g=sp["norm2_g"][l][None],
        lam=jax.nn.log_sigmoid(sp["ret_decay"][l]), ret_gn_g=sp["ret_gn_g"][l][None],
        conv_dw_w=cw, conv_dw_b=sp["conv_dw_b"][l][None], conv_ln_g=sp["conv_ln_g"][l][None],
        conv_ln_b=sp["conv_ln_b"][l][None], na_rpb=sp["na_rpb"][l],
        ffn_dw_w=fw, ffn_dw_b=sp["ffn_dw_b"][l].reshape(2, 1, DFF))


def local_step(cfg, x, ctx, tgt, mods, wts, sp, grads_ready=lambda l, names, gb: None):
    depth = sp["norm1_g"].shape[0]
    cos, sin = rope_tables(cfg)
    tabs = dict(cos=cos, sin=sin)
    XS = jnp.concatenate([x, ctx], axis=0)
    smalls = [_layer_small(cfg, l, sp) for l in range(depth)]
    saves = []
    for l in range(depth):
        XS, sv = layer_fwd(cfg, l, XS, mods[l], functools.partial(wts, l), smalls[l], tabs)
        saves.append(sv)
    xm, y2, g2 = XS
    ls, dX, dfg = final_loss(cfg, "final_loss", xm, y2, g2[0], sp["final_g"][None], tgt)
    gb, gss, dms = {}, [None] * depth, [None] * depth
    token = None
    for l in reversed(range(depth)):
        mod = mods[l] if token is None else {**mods[l], "g2": mods[l]["g2"] + token}
        dX, gb, gss[l], dms[l], token = layer_bwd(cfg, l, dX, saves[l], mod, saves[l]["W"], smalls[l], tabs, gb,
                                                  functools.partial(grads_ready, l))
    return ls[0, 0], dX[:cfg.T], gb, gss, dms, dfg[0], token


MESH = pl.DeviceIdType.MESH
N_DEV = 8
N_CHIP = 4
BIG = ("w_in", "w_out", "ffn_up", "ffn_down", "conv_pw")
_ANY = pl.BlockSpec(memory_space=pl.ANY)


def _place():
    x, y, c = lax.axis_index("x"), lax.axis_index("y"), lax.axis_index("c")
    chips = [(1 - x, y), (x, 1 - y), (1 - x, 1 - y)]
    return x, y, c, chips


def allgather8(name, xs):
    m_per, n = xs.shape

    def body(x_ref, out_ref, send_sems, recv_sems, local_sem):
        x, y, c, chips = _place()
        me, sibling = (x, y, c), (x, y, 1 - c)

        def rows(px, py, pc):
            return out_ref.at[pl.ds((4 * px + 2 * py + pc) * m_per, m_per), :]

        def copy(k, block, to, src=None):
            return pltpu.make_async_remote_copy(
                src_ref=rows(*block) if src is None else src, dst_ref=rows(*block),
                send_sem=send_sems.at[k], recv_sem=recv_sems.at[k], device_id=to, device_id_type=MESH)

        mine = pltpu.make_async_copy(x_ref, rows(*me), local_sem)
        mine.start()
        first = [copy(0, me, sibling, src=x_ref)]
        first += [copy(1 + j, me, (*chip, c), src=x_ref) for j, chip in enumerate(chips)]
        for cp in first:
            cp.start()
        passed = [copy(4 + j, (*chip, c), sibling) for j, chip in enumerate(chips)]
        for j, chip in enumerate(chips):
            copy(1 + j, (*chip, c), me).wait_recv()
            passed[j].start()
        copy(0, sibling, me).wait_recv()
        for j, chip in enumerate(chips):
            copy(4 + j, (*chip, 1 - c), me).wait_recv()
        for cp in first + passed:
            cp.wait_send()
        mine.wait()

    return _pc(body, name=name, out_shape=jax.ShapeDtypeStruct((N_DEV * m_per, n), xs.dtype),
               in_specs=[pl.BlockSpec(memory_space=pltpu.VMEM)], out_specs=pl.BlockSpec(memory_space=pltpu.VMEM),
               scratch_shapes=[pltpu.SemaphoreType.DMA((7,)), pltpu.SemaphoreType.DMA((7,)), pltpu.SemaphoreType.DMA],
               compiler_params=pltpu.CompilerParams(vmem_limit_bytes=VMEM_LIMIT))(xs)


def _wpiece(ref, layer, chip_idx, half):
    rh = ref.shape[2] // 2
    return ref.at[chip_idx, layer, pl.ds(half * rh, rh)]


def _wcopy(ref, layer, chip_idx, half, send_sems, recv_sems, k, to):
    piece = _wpiece(ref, layer, chip_idx, half)
    return pltpu.make_async_remote_copy(src_ref=piece, dst_ref=piece, send_sem=send_sems.at[k],
                                        recv_sem=recv_sems.at[k], device_id=to, device_id_type=MESH)


def _w_ici_sends(outs, layer, send_sems, recv_sems):
    x, y, c, chips = _place()
    return [_wcopy(outs[a], layer, 2 * x + y, c, send_sems, recv_sems, 3 * a + t, (*chip, c))
            for a in range(len(outs)) for t, chip in enumerate(chips)]


def _w_ici_landed(outs, layer, send_sems, recv_sems):
    x, y, c, chips = _place()
    return [_wcopy(outs[a], layer, 2 * chip[0] + chip[1], c, send_sems, recv_sems, 3 * a + t, (x, y, c))
            for a in range(len(outs)) for t, chip in enumerate(chips)]


def _w_forward(outs, layer, send_sems, recv_sems, base):
    x, y, c, chips = _place()
    n = len(outs)
    sends = [_wcopy(outs[a], layer, 2 * chip[0] + chip[1], c, send_sems, recv_sems, base + 3 * a + t, (x, y, 1 - c))
             for a in range(n) for t, chip in enumerate(chips)]
    recvs = [_wcopy(outs[a], layer, 2 * chip[0] + chip[1], 1 - c, send_sems, recv_sems, base + 3 * a + t, (x, y, c))
             for a in range(n) for t, chip in enumerate(chips)]
    return sends, recvs


def allgather_layer(name, bufs, layer):
    n = len(bufs)

    def body(*refs):
        outs = refs[n:2 * n]
        send_sems, recv_sems = refs[2 * n:]
        sent = _w_ici_sends(outs, layer, send_sems, recv_sems)
        for cp in sent:
            cp.start()
        fwd, from_sib = _w_forward(outs, layer, send_sems, recv_sems, 3 * n)
        for landed, fw in zip(_w_ici_landed(outs, layer, send_sems, recv_sems), fwd):
            landed.wait_recv()
            fw.start()
        for cp in from_sib:
            cp.wait_recv()
        for cp in sent + fwd:
            cp.wait_send()

    return _pc(body, name=name, out_shape=[jax.ShapeDtypeStruct(b.shape, b.dtype) for b in bufs],
               in_specs=[_ANY] * n, out_specs=[_ANY] * n, input_output_aliases={a: a for a in range(n)},
               scratch_shapes=[pltpu.SemaphoreType.DMA((6 * n,)), pltpu.SemaphoreType.DMA((6 * n,))])(*bufs)


_HBM = pl.BlockSpec(memory_space=pltpu.HBM)
_SEM = pl.BlockSpec(memory_space=pltpu.SEMAPHORE)
_EFFECT = pltpu.SideEffectType.DATAFLOW_SIDE_EFFECTING


def allgather_layer_start(name, bufs, layer, after):
    n = len(bufs)

    def body(*refs):
        send_sems, recv_sems = refs[n + 1:n + 3]
        outs = refs[n + 3:2 * n + 3]
        token = refs[2 * n + 3]
        for cp in _w_ici_sends(outs, layer, send_sems, recv_sems):
            cp.start()
        token[...] = jnp.zeros_like(token)

    res = _pc(body, name=name,
              out_shape=(pltpu.SemaphoreType.DMA((3 * n,)), pltpu.SemaphoreType.DMA((3 * n,)),
                         *[pltpu.HBM(b.shape, b.dtype) for b in bufs], jax.ShapeDtypeStruct((8, LANE), F32)),
              in_specs=[_HBM] * n + [_ANY],
              out_specs=(_SEM, _SEM, *([_HBM] * n), pl.BlockSpec(memory_space=pltpu.VMEM)),
              input_output_aliases={a: a + 2 for a in range(n)},
              compiler_params=pltpu.CompilerParams(has_side_effects=_EFFECT))(
                  *[pltpu.with_memory_space_constraint(b, pltpu.HBM) for b in bufs], after)
    return res[0], res[1], list(res[2:2 + n]), res[2 + n]


def allgather_layer_wait(name, bufs, send_sems, recv_sems, after, layer):
    n = len(bufs)

    def body(*refs):
        ins = refs[:n]
        send_sems, recv_sems = refs[n:n + 2]
        for cp in _w_ici_sends(ins, layer, send_sems, recv_sems):
            cp.wait_send()
        for cp in _w_ici_landed(ins, layer, send_sems, recv_sems):
            cp.wait_recv()

    return _pc(body, name=name, out_shape=tuple(pltpu.HBM(b.shape, b.dtype) for b in bufs),
               in_specs=[_HBM] * n + [_SEM, _SEM, _ANY], out_specs=tuple([_HBM] * n),
               input_output_aliases={a: a for a in range(n)},
               compiler_params=pltpu.CompilerParams(has_side_effects=_EFFECT))(*bufs, send_sems, recv_sems, after)


def forward_halves(name, bufs, layer):
    n = len(bufs)

    def body(*refs):
        outs = refs[n:2 * n]
        send_sems, recv_sems = refs[2 * n:]
        fwd, from_sib = _w_forward(outs, layer, send_sems, recv_sems, 0)
        for cp in fwd:
            cp.start()
        for cp in from_sib:
            cp.wait_recv()
        for cp in fwd:
            cp.wait_send()

    return _pc(body, name=name, out_shape=[jax.ShapeDtypeStruct(b.shape, b.dtype) for b in bufs],
               in_specs=[_ANY] * n, out_specs=[_ANY] * n, input_output_aliases={a: a for a in range(n)},
               scratch_shapes=[pltpu.SemaphoreType.DMA((3 * n,)), pltpu.SemaphoreType.DMA((3 * n,))])(*bufs)


def exchange_rows(name, grads, layer):
    n = len(grads)

    def body(*refs):
        ins, outs = refs[:n], refs[n:2 * n]
        send_sems, recv_sems = refs[2 * n:]
        x, y, c, _ = _place()
        cps = []
        for a in range(n):
            rh = ins[a].shape[2] // 2
            cps.append(pltpu.make_async_remote_copy(
                src_ref=ins[a].at[layer, pl.ds(0, N_CHIP), pl.ds((1 - c) * rh, rh)], dst_ref=outs[a],
                send_sem=send_sems.at[a], recv_sem=recv_sems.at[a], device_id=(x, y, 1 - c), device_id_type=MESH))
        for cp in cps:
            cp.start()
        for cp in cps:
            cp.wait()

    return _pc(body, name=name,
               out_shape=[jax.ShapeDtypeStruct((N_CHIP, g.shape[2] // 2, g.shape[3]), g.dtype) for g in grads],
               in_specs=[_ANY] * n, out_specs=[_ANY] * n,
               scratch_shapes=[pltpu.SemaphoreType.DMA((n,)), pltpu.SemaphoreType.DMA((n,))])(*grads)


def _scatter_sends(parts, lands, send_sems, recv_sems):
    x, y, c, chips = _place()
    return [pltpu.make_async_remote_copy(
        src_ref=parts[a].at[2 * chip[0] + chip[1]], dst_ref=lands[a].at[2 * x + y], send_sem=send_sems.at[3 * a + t],
        recv_sem=recv_sems.at[3 * a + t], device_id=(*chip, c), device_id_type=MESH)
        for a in range(len(parts)) for t, chip in enumerate(chips)]


def _scatter_landed(lands, send_sems, recv_sems):
    x, y, c, chips = _place()
    return [pltpu.make_async_remote_copy(
        src_ref=lands[a].at[2 * chip[0] + chip[1]], dst_ref=lands[a].at[2 * chip[0] + chip[1]],
        send_sem=send_sems.at[3 * a + t], recv_sem=recv_sems.at[3 * a + t], device_id=(x, y, c), device_id_type=MESH)
        for a in range(len(lands)) for t, chip in enumerate(chips)]


def scatter_slices(name, parts, lands):
    n = len(parts)

    def body(*refs):
        ins, outs = refs[:n], refs[2 * n:3 * n]
        send_sems, recv_sems = refs[3 * n:]
        cps = _scatter_sends(ins, outs, send_sems, recv_sems)
        for cp in cps:
            cp.start()
        for cp in _scatter_landed(outs, send_sems, recv_sems):
            cp.wait_recv()
        for cp in cps:
            cp.wait_send()

    return _pc(body, name=name, out_shape=[jax.ShapeDtypeStruct(p.shape, p.dtype) for p in lands],
               in_specs=[_ANY] * (2 * n), out_specs=[_ANY] * n,
               input_output_aliases={n + a: a for a in range(n)},
               scratch_shapes=[pltpu.SemaphoreType.DMA((3 * n,)), pltpu.SemaphoreType.DMA((3 * n,))])(*parts, *lands)


def scatter_slices_start(name, parts, lands):
    n = len(parts)

    def body(*refs):
        send_sems, recv_sems = refs[2 * n:2 * n + 2]
        p_out, l_out = refs[2 * n + 2:3 * n + 2], refs[3 * n + 2:4 * n + 2]
        token = refs[4 * n + 2]
        for cp in _scatter_sends(p_out, l_out, send_sems, recv_sems):
            cp.start()
        token[...] = jnp.zeros_like(token)

    both = list(parts) + list(lands)
    res = _pc(body, name=name,
              out_shape=(pltpu.SemaphoreType.DMA((3 * n,)), pltpu.SemaphoreType.DMA((3 * n,)),
                         *[pltpu.HBM(b.shape, b.dtype) for b in both], jax.ShapeDtypeStruct((8, LANE), F32)),
              in_specs=[_HBM] * (2 * n),
              out_specs=(_SEM, _SEM, *([_HBM] * (2 * n)), pl.BlockSpec(memory_space=pltpu.VMEM)),
              input_output_aliases={a: a + 2 for a in range(2 * n)},
              compiler_params=pltpu.CompilerParams(has_side_effects=_EFFECT))(
                  *[pltpu.with_memory_space_constraint(b, pltpu.HBM) for b in both])
    return res[0], res[1], list(res[2:2 + n]), list(res[2 + n:2 + 2 * n]), res[2 + 2 * n]


def scatter_slices_wait(name, parts, lands, send_sems, recv_sems, after):
    n = len(parts)

    def body(*refs):
        p_in, l_in = refs[:n], refs[n:2 * n]
        send_sems, recv_sems = refs[2 * n:2 * n + 2]
        for cp in _scatter_sends(p_in, l_in, send_sems, recv_sems):
            cp.wait_send()
        for cp in _scatter_landed(l_in, send_sems, recv_sems):
            cp.wait_recv()

    both = list(parts) + list(lands)
    res = _pc(body, name=name, out_shape=tuple(pltpu.HBM(b.shape, b.dtype) for b in both),
              in_specs=[_HBM] * (2 * n) + [_SEM, _SEM, _ANY], out_specs=tuple([_HBM] * (2 * n)),
              input_output_aliases={a: a for a in range(2 * n)},
              compiler_params=pltpu.CompilerParams(has_side_effects=_EFFECT))(*both, send_sems, recv_sems, after)
    return list(res[n:])


def share_rows(name, bufs):
    n = len(bufs)

    def body(*refs):
        outs = refs[n:2 * n]
        send_sems, recv_sems = refs[2 * n:]
        x, y, c, _ = _place()

        def half(a, h):
            return outs[a].at[pl.ds(0, 2), h]

        cps = [pltpu.make_async_remote_copy(
            src_ref=half(a, c), dst_ref=half(a, c), send_sem=send_sems.at[a], recv_sem=recv_sems.at[a],
            device_id=(x, y, 1 - c), device_id_type=MESH) for a in range(n)]
        for cp in cps:
            cp.start()
        for a in range(n):
            pltpu.make_async_remote_copy(
                src_ref=half(a, 1 - c), dst_ref=half(a, 1 - c), send_sem=send_sems.at[a],
                recv_sem=recv_sems.at[a], device_id=(x, y, c), device_id_type=MESH).wait_recv()
        for cp in cps:
            cp.wait_send()

    return _pc(body, name=name, out_shape=[jax.ShapeDtypeStruct(b.shape, b.dtype) for b in bufs],
               in_specs=[_ANY] * n, out_specs=[_ANY] * n, input_output_aliases={a: a for a in range(n)},
               scratch_shapes=[pltpu.SemaphoreType.DMA((n,)), pltpu.SemaphoreType.DMA((n,))])(*bufs)


def _row_tile(R, C, nbytes=1 << 20):
    t = 8
    while t * 2 <= R and R % (t * 2) == 0 and t * 2 * C * 4 <= nbytes:
        t *= 2
    assert R % t == 0
    return t


def to_bf16_block(name, w, chip_arr, layer, after=None):
    _, R, C = w.shape
    tr = _row_tile(R, C)

    def body(j_ref, w_ref, *rest):
        o_ref = rest[-1]
        o_ref[...] = w_ref[...].astype(o_ref.dtype)

    in_specs, args = [pl.BlockSpec((None, tr, C), lambda i, j_ref: (layer, i, 0))], (chip_arr, w)
    if after is not None:
        in_specs, args = in_specs + [_ANY], args + (after,)
    gs = pltpu.PrefetchScalarGridSpec(
        num_scalar_prefetch=1, grid=(R // tr,), in_specs=in_specs,
        out_specs=pl.BlockSpec((None, None, tr, C), lambda i, j_ref: (j_ref[0], layer, i, 0)))
    return _pc(body, name=name, grid_spec=gs, out_shape=jax.ShapeDtypeStruct((N_CHIP,) + w.shape, BF16),
               compiler_params=_cp(("parallel",)))(*args)


def add_rows(name, g, ra, c_arr, chip_arr, layer):
    _, _, R, C = g.shape
    rh = R // 2
    tr = _row_tile(rh, C)
    nb = rh // tr

    def body(c_ref, j_ref, g_ref, r_ref, o_ref, own_ref):
        s = (g_ref[...].astype(F32) + r_ref[...].astype(F32)).astype(o_ref.dtype)
        o_ref[...] = s

        @pl.when(pl.program_id(1) == j_ref[0])
        def _():
            own_ref[...] = s

    gs = pltpu.PrefetchScalarGridSpec(
        num_scalar_prefetch=2, grid=(nb, N_CHIP),
        in_specs=[pl.BlockSpec((None, None, tr, C), lambda i, j, c_ref, j_ref: (layer, j, c_ref[0] * nb + i, 0)),
                  pl.BlockSpec((None, tr, C), lambda i, j, c_ref, j_ref: (j, i, 0))],
        out_specs=[pl.BlockSpec((None, tr, C), lambda i, j, c_ref, j_ref: (j, i, 0)),
                   pl.BlockSpec((None, tr, C), lambda i, j, c_ref, j_ref: (j_ref[0], i, 0))])
    shp = jax.ShapeDtypeStruct(ra.shape, BF16)
    return _pc(body, name=name, grid_spec=gs, out_shape=[shp, shp],
               compiler_params=_cp(("arbitrary", "arbitrary")))(c_arr, chip_arr, g, ra)


def sum_rows_into(name, landed, c_arr, layer, into):
    n, rh, C = landed.shape
    tr = _row_tile(rh, C, nbytes=1 << 19)

    def body(*refs):
        g_ref, o_ref = refs[1], refs[-1]
        acc = g_ref[0].astype(F32)
        for j in range(1, n):
            acc = acc + g_ref[j].astype(F32)
        o_ref[...] = acc

    in_specs, args, alias = [pl.BlockSpec((n, tr, C), lambda i, c_ref: (0, i, 0))], (c_arr, landed), {}
    if into is not None:
        in_specs, args, alias = in_specs + [_ANY], args + (into,), {2: 0}
    gs = pltpu.PrefetchScalarGridSpec(
        num_scalar_prefetch=1, grid=(rh // tr,), in_specs=in_specs,
        out_specs=pl.BlockSpec((None, None, tr, C), lambda i, c_ref: (layer, c_ref[0], i, 0)))
    return _pc(body, name=name, grid_spec=gs, out_shape=jax.ShapeDtypeStruct((2, 2, rh, C), F32),
               input_output_aliases=alias, compiler_params=_cp(("parallel",)))(*args)


def sum_leading(name, g, plane=None):
    n, R, C = g.shape
    tr = _row_tile(R, C, nbytes=(1 << 21) // n)

    def body(*refs):
        g_ref, o_ref = refs[-2:]
        acc = g_ref[0].astype(F32)
        for j in range(1, n):
            acc = acc + g_ref[j].astype(F32)
        o_ref[...] = acc

    if plane is None:
        return _pc(body, name=name, grid=(R // tr,), in_specs=[pl.BlockSpec((n, tr, C), lambda i: (0, i, 0))],
                   out_specs=pl.BlockSpec((tr, C), lambda i: (i, 0)), out_shape=jax.ShapeDtypeStruct((R, C), F32),
                   compiler_params=_cp(("parallel",)))(g)
    count, idx = plane
    gs = pltpu.PrefetchScalarGridSpec(
        num_scalar_prefetch=1, grid=(R // tr,),
        in_specs=[pl.BlockSpec((n, tr, C), lambda i, p_ref: (0, i, 0))],
        out_specs=pl.BlockSpec((None, tr, C), lambda i, p_ref: (p_ref[0], i, 0)))
    return _pc(body, name=name, grid_spec=gs, out_shape=jax.ShapeDtypeStruct((count, R, C), F32),
               compiler_params=_cp(("parallel",)))(idx, g)


def adamw(name, w, g, m, v, emit_g=False):
    R, C = w.shape
    tr = _row_tile(R, C)

    def body(w_ref, g_ref, m_ref, v_ref, d_ref, mo_ref, vo_ref, *go_ref):
        gv = g_ref[...]
        if emit_g:
            go_ref[0][...] = gv
        mn = ADAM_B1 * m_ref[...] + (1.0 - ADAM_B1) * gv
        vn = ADAM_B2 * v_ref[...] + (1.0 - ADAM_B2) * (gv * gv)
        m_hat = mn / (1.0 - ADAM_B1 ** ADAM_STEP)
        v_hat = vn / (1.0 - ADAM_B2 ** ADAM_STEP)
        d_ref[...] = -ADAM_LR * (m_hat / (jnp.sqrt(v_hat) + ADAM_EPS) + ADAM_WD * w_ref[...])
        mo_ref[...] = mn
        vo_ref[...] = vn

    spec = pl.BlockSpec((tr, C), lambda i: (i, 0))
    shp = jax.ShapeDtypeStruct((R, C), F32)
    nout = 4 if emit_g else 3
    return _pc(body, name=name, grid=(R // tr,), in_specs=[spec] * 4, out_specs=[spec] * nout,
               out_shape=[shp] * nout, compiler_params=_cp(("parallel",)))(w, g, m, v)


_ADA_TN = 512


def adaln_fwd(name, cond, w, b):
    _, D, N = w.shape
    tn = min(_ADA_TN, N)

    def body(c_ref, w_ref, b_ref, o_ref):
        s = _silu(c_ref[...]).astype(BF16)
        o_ref[...] = dot_nn(s, w_ref[...].astype(BF16)) + b_ref[...]

    return _pc(body, name=name, grid=(2, N // tn),
               in_specs=[pl.BlockSpec((16, D), lambda l, n: (0, 0)),
                         pl.BlockSpec((None, D, tn), lambda l, n: (l, 0, n)),
                         pl.BlockSpec((None, 1, tn), lambda l, n: (l, 0, n))],
               out_specs=pl.BlockSpec((None, 16, tn), lambda l, n: (l, 0, n)),
               out_shape=jax.ShapeDtypeStruct((2, 16, N), F32),
               compiler_params=_cp(("parallel", "parallel")))(cond, w, b)


def adaln_bwd(name, cond, w, dm):
    _, D, N = w.shape
    tn = min(_ADA_TN, N)

    def body(c_ref, w_ref, dm_ref, gw_ref, ds_ref):
        first = jnp.logical_and(pl.program_id(0) == 0, pl.program_id(1) == 0)
        s = _silu(c_ref[...]).astype(BF16)
        dmb = dm_ref[...].astype(BF16)
        gw_ref[...] = dot_tn(s, dmb)
        p = dot_nt(dmb, w_ref[...].astype(BF16))

        @pl.when(first)
        def _():
            ds_ref[...] = p

        @pl.when(jnp.logical_not(first))
        def _():
            ds_ref[...] += p

    return _pc(body, name=name, grid=(2, N // tn),
               in_specs=[pl.BlockSpec((16, D), lambda l, n: (0, 0)),
                         pl.BlockSpec((None, D, tn), lambda l, n: (l, 0, n)),
                         pl.BlockSpec((None, 16, tn), lambda l, n: (l, 0, n))],
               out_specs=[pl.BlockSpec((None, D, tn), lambda l, n: (l, 0, n)),
                          pl.BlockSpec((16, D), lambda l, n: (0, 0))],
               out_shape=[jax.ShapeDtypeStruct((2, D, N), F32), jax.ShapeDtypeStruct((16, D), F32)],
               compiler_params=_cp(("arbitrary", "arbitrary")))(cond, w, dm)


def cctx_grad(name, parts, c_ctx):
    def body(p_ref, c_ref, o_ref):
        acc = p_ref[0]
        for j in range(1, N_CHIP):
            acc = acc + p_ref[j]
        o_ref[...] = acc * _dsilu(c_ref[...])

    return _pc(body, name=name, out_shape=jax.ShapeDtypeStruct(c_ctx.shape, F32))(parts, c_ctx)


def _pack(arrs):
    rows = []
    for a in arrs:
        f = a.reshape(-1)
        pad = (-f.shape[0]) % LANE
        rows.append(jnp.pad(f, (0, pad)).reshape(-1, LANE))
    out = jnp.concatenate(rows, axis=0)
    pad = (-out.shape[0]) % 8
    return jnp.pad(out, ((0, pad), (0, 0))) if pad else out


def _unpack(rows, shapes):
    out, r = [], 0
    for s in shapes:
        n = int(np.prod(s))
        nr = -(-n // LANE)
        out.append(rows[r:r + nr].reshape(-1)[:n].reshape(s))
        r += nr
    return out


MOD_NAMES = ("sh1", "sc1", "g1", "sh2", "sc2", "g2")


def kernel(x, c, ctx, c_ctx, w_ada, b_ada, norm1_g, w_in, ret_decay, ret_gn_g, conv_dw_w, conv_dw_b, conv_ln_g, conv_ln_b, conv_pw, na_rpb, w_out, norm2_g, ffn_up, ffn_dw_w, ffn_dw_b, ffn_down, final_g, loss_target, m_c_ctx, m_w_ada, m_b_ada, m_norm1_g, m_w_in, m_ret_decay, m_ret_gn_g, m_conv_dw_w, m_conv_dw_b, m_conv_ln_g, m_conv_ln_b, m_conv_pw, m_na_rpb, m_w_out, m_norm2_g, m_ffn_up, m_ffn_dw_w, m_ffn_dw_b, m_ffn_down, m_final_g, v_c_ctx, v_w_ada, v_b_ada, v_norm1_g, v_w_in, v_ret_decay, v_ret_gn_g, v_conv_dw_w, v_conv_dw_b, v_conv_ln_g, v_conv_ln_b, v_conv_pw, v_na_rpb, v_w_out, v_norm2_g, v_ffn_up, v_ffn_dw_w, v_ffn_dw_b, v_ffn_down, v_final_g):
    cfg = make_cfg(D=x.shape[2], T=x.shape[1], TC=ctx.shape[1], RH=ret_decay.shape[2], CW=conv_dw_b.shape[1],
                   NH=na_rpb.shape[1], DFF=ffn_dw_b.shape[1] // 2)
    D, T = cfg.D, cfg.T
    W = dict(c_ctx=c_ctx, w_ada=w_ada, b_ada=b_ada, norm1_g=norm1_g, w_in=w_in, ret_decay=ret_decay, ret_gn_g=ret_gn_g,
             conv_dw_w=conv_dw_w, conv_dw_b=conv_dw_b, conv_ln_g=conv_ln_g, conv_ln_b=conv_ln_b, conv_pw=conv_pw,
             na_rpb=na_rpb, w_out=w_out, norm2_g=norm2_g, ffn_up=ffn_up, ffn_dw_w=ffn_dw_w, ffn_dw_b=ffn_dw_b,
             ffn_down=ffn_down, final_g=final_g)
    Mo = dict(c_ctx=m_c_ctx, w_ada=m_w_ada, b_ada=m_b_ada, norm1_g=m_norm1_g, w_in=m_w_in, ret_decay=m_ret_decay,
              ret_gn_g=m_ret_gn_g, conv_dw_w=m_conv_dw_w, conv_dw_b=m_conv_dw_b, conv_ln_g=m_conv_ln_g,
              conv_ln_b=m_conv_ln_b, conv_pw=m_conv_pw, na_rpb=m_na_rpb, w_out=m_w_out, norm2_g=m_norm2_g,
              ffn_up=m_ffn_up, ffn_dw_w=m_ffn_dw_w, ffn_dw_b=m_ffn_dw_b, ffn_down=m_ffn_down, final_g=m_final_g)
    Vo = dict(c_ctx=v_c_ctx, w_ada=v_w_ada, b_ada=v_b_ada, norm1_g=v_norm1_g, w_in=v_w_in, ret_decay=v_ret_decay,
              ret_gn_g=v_ret_gn_g, conv_dw_w=v_conv_dw_w, conv_dw_b=v_conv_dw_b, conv_ln_g=v_conv_ln_g,
              conv_ln_b=v_conv_ln_b, conv_pw=v_conv_pw, na_rpb=v_na_rpb, w_out=v_w_out, norm2_g=v_norm2_g,
              ffn_up=v_ffn_up, ffn_dw_w=v_ffn_dw_w, ffn_dw_b=v_ffn_dw_b, ffn_down=v_ffn_down, final_g=v_final_g)
    order = list(W)
    xi, yi, ci = lax.axis_index("x"), lax.axis_index("y"), lax.axis_index("c")
    chip = 2 * xi + yi
    dev = 4 * xi + 2 * yi + ci
    NA = w_ada.shape[2]
    ncw, nfw = conv_dw_w.shape[2], ffn_dw_w.shape[2]

    c_arr = jnp.reshape(ci, (1,)).astype(jnp.int32)
    chip_arr = jnp.reshape(chip, (1,)).astype(jnp.int32)

    g_in = allgather8("ag_small_in", _pack([c[0], conv_dw_w, ffn_dw_w])).reshape(N_DEV, -1, LANE)
    c8 = g_in[:, :D // LANE].reshape(N_DEV, D)
    cw_parts, fw_parts = [], []
    for j in range(N_CHIP):
        _, a, b = _unpack(g_in[2 * j], [(D,), conv_dw_w.shape, ffn_dw_w.shape])
        cw_parts.append(a)
        fw_parts.append(b)
    conv_dw_w_full = jnp.concatenate(cw_parts, axis=2)
    ffn_dw_w_full = jnp.concatenate(fw_parts, axis=2)
    cond = jnp.concatenate([c8, c_ctx[None], jnp.zeros((16 - N_DEV - 1, D), F32)], axis=0)

    b_sh = lax.dynamic_slice(b_ada, (0, chip * NA), (2, NA)).reshape(2, 1, NA)
    m_sh = adaln_fwd("adaln_fwd", cond, w_ada, b_sh)
    m_dev = allgather8("ag_mod", m_sh.reshape(2 * 16, NA)).reshape(N_DEV, 2, 16, NA)
    m_all = jnp.concatenate([m_dev[2 * j] for j in range(N_CHIP)], axis=-1)
    mods = []
    for l in range(2):
        lat = lax.dynamic_slice(m_all[l], (dev, 0), (1, N_CHIP * NA))[0]
        cx = m_all[l, N_DEV]
        mods.append({nm: jnp.stack([lat[k * D:(k + 1) * D], cx[k * D:(k + 1) * D]], 0)[:, None, :]
                     for k, nm in enumerate(MOD_NAMES)})

    first, rest = ("w_in", "conv_pw"), ("w_out", "ffn_up", "ffn_down")
    wb = [{}, {}]
    have, flying_w = {}, {}

    def start_gather(tag, l, names, after):
        s_sem, r_sem, bufs, tok = allgather_layer_start(f"ag_{tag}_start", [wb[l][nm] for nm in names], l, after)
        flying_w[(l, names[0])] = (tag, l, names, bufs, s_sem, r_sem)
        return tok

    def land_gather(key, after):
        tag, l, names, bufs, s_sem, r_sem = flying_w.pop(key)
        landed = allgather_layer_wait(f"ag_{tag}_wait", bufs, s_sem, r_sem, after, l)
        have.update(zip([(l, nm) for nm in names], forward_halves(f"ag_{tag}_fwd", list(landed), l)))

    for nm in first:
        wb[0][nm] = to_bf16_block(f"to_bf16_{nm}_0", W[nm], chip_arr, 0)
    tok_first = start_gather("w0a", 0, first, m_all)
    later = [(l, nm) for l in range(2) for nm in BIG if nm not in wb[l]]
    casts = lax.optimization_barrier(tuple(
        to_bf16_block(f"to_bf16_{nm}_{l}", W[nm], chip_arr, l, after=tok_first) for l, nm in later))
    for (l, nm), cast in zip(later, casts):
        wb[l][nm] = cast
    land_gather((0, first[0]), casts[0])
    mods[0] = {**mods[0], "sc1": mods[0]["sc1"] + start_gather("w0b", 0, rest, have[(0, first[0])])[0, 0]}

    def wts(l, name, after):
        tok = None
        if (l, name) not in have:
            if l == 0:
                land_gather((0, rest[0]), after)
                tok = start_gather("w1", 1, BIG, have[(0, rest[0])])[0, 0]
            else:
                land_gather((1, BIG[0]), after)
        return have[(l, name)], tok

    sp = dict(norm1_g=norm1_g, norm2_g=norm2_g, ret_decay=ret_decay, ret_gn_g=ret_gn_g, conv_dw_w=conv_dw_w_full,
              conv_dw_b=conv_dw_b, conv_ln_g=conv_ln_g, conv_ln_b=conv_ln_b, na_rpb=na_rpb, ffn_dw_w=ffn_dw_w_full,
              ffn_dw_b=ffn_dw_b, final_g=final_g)
    flights = []

    def grads_ready(l, names, gb):
        tag = f"{l}_{names[0]}"
        from_sib = exchange_rows(f"rs_exchange_{tag}", [gb[nm] for nm in names], l)
        both = [add_rows(f"rs_add_{nm}_{l}", gb[nm], r, c_arr, chip_arr, l) for nm, r in zip(names, from_sib)]
        part, lands = [b[0] for b in both], [b[1] for b in both]
        s_sem, r_sem, part, lands, tok = scatter_slices_start(f"rs_scatter_{tag}_start", part, lands)
        flights.append((l, names, tag, part, lands, (s_sem, r_sem)))
        return tok[0, 0]

    loss_l, gx, gb, gss, dms, dfg, tok_last = local_step(
        cfg, x[0], ctx[0], loss_target[0], mods, wts, sp, grads_ready)
    loss = lax.psum(loss_l, ("x", "y", "c"))

    delta, new_m, new_v = {}, {}, {}
    bigs = ("w_ada",) + BIG

    def adamw_big(nm):
        shp = W[nm].shape
        v2 = lambda a: a.reshape(-1, shp[-1])
        d_, m_, v_, *g_ = adamw(f"adamw_{nm}", v2(W[nm]), v2(gfull[nm]), v2(Mo[nm]), v2(Vo[nm]), emit_g=nm in BIG)
        delta[nm], new_m[nm], new_v[nm] = d_.reshape(shp), m_.reshape(shp), v_.reshape(shp)
        if g_:
            gfull[nm] = g_[0].reshape(shp)

    gfull, fin, after = {}, {}, gx
    for names in GRAD_GROUPS:
        for l, _, tag, part, lands, sems in sorted([f for f in flights if f[1] == names], key=lambda f: -f[0]):
            landed = scatter_slices_wait(f"rs_scatter_{tag}_wait", part, lands, *sems, after)
            for nm, p in zip(names, landed):
                fin[nm] = sum_rows_into(f"rs_sum_{nm}_{l}", p, c_arr, l, fin.get(nm))
        for nm, gfin in zip(names, share_rows(f"rs_share_{names[0]}", [fin[nm] for nm in names])):
            gfull[nm] = gfin.reshape(W[nm].shape)
            adamw_big(nm)
        after = delta[names[-1]]

    dmseg = jnp.stack([jnp.stack([jnp.concatenate([dms[l][nm][r, 0] for nm in MOD_NAMES]) for r in range(2)])
                       for l in range(2)])
    dmseg, _ = lax.optimization_barrier((dmseg, fin[GRAD_GROUPS[-1][-1]]))
    gsm = dict(
        norm1_g=jnp.stack([gss[l]["norm1_g"][0] for l in range(2)]),
        ret_decay=jnp.stack([gss[l]["lam"] * jax.nn.sigmoid(-ret_decay[l]) for l in range(2)]),
        ret_gn_g=jnp.stack([gss[l]["ret_gn_g"][0] for l in range(2)]),
        conv_dw_w=jnp.stack([gss[l]["conv_dw_w"][:cfg.CK] for l in range(2)]),
        conv_dw_b=jnp.stack([gss[l]["conv_dw_b"][0] for l in range(2)]),
        conv_ln_g=jnp.stack([gss[l]["conv_ln_g"][0] for l in range(2)]),
        conv_ln_b=jnp.stack([gss[l]["conv_ln_b"][0] for l in range(2)]),
        na_rpb=jnp.stack([gss[l]["na_rpb"] for l in range(2)]),
        norm2_g=jnp.stack([gss[l]["norm2_g"][0] for l in range(2)]),
        ffn_dw_w=jnp.stack([gss[l]["ffn_dw_w"][:, :3].transpose(1, 0, 2).reshape(3, 2 * cfg.DFF) for l in range(2)]),
        ffn_dw_b=jnp.stack([gss[l]["ffn_dw_b"].reshape(-1) for l in range(2)]),
        final_g=dfg)
    snames = list(gsm)
    sshapes = [dmseg.shape] + [gsm[nm].shape for nm in snames]
    packed = _pack([dmseg] + [gsm[nm] for nm in snames])
    g_all = allgather8("ag_small_grads", packed).reshape(N_DEV, packed.shape[0], LANE)
    summed = sum_leading("sum_small_grads", g_all)
    dm_sum, *gsum = _unpack(summed, sshapes)
    gfull.update(zip(snames, gsum))
    ndm = int(np.prod(dmseg.shape))
    dm_all = g_all[:, :ndm // LANE].reshape(N_DEV, 2, 2, 6 * D)
    gfull["b_ada"] = sum_leading("sum_b_ada", dm_all.transpose(0, 2, 1, 3).reshape(2 * N_DEV, 2 * 6 * D // LANE, LANE)
                                 ).reshape(2, 6 * D)

    dm16 = jnp.concatenate([dm_all[:, :, 0].transpose(1, 0, 2), dm_sum[:, 1][:, None],
                            jnp.zeros((2, 16 - N_DEV - 1, 6 * D), F32)], axis=1)
    dm16 = lax.dynamic_slice(dm16, (0, 0, chip * NA), (2, 16, NA))
    gfull["w_ada"], ds16 = adaln_bwd("adaln_bwd", cond, w_ada, dm16)
    ds_all = allgather8("ag_dsilu", ds16[8:16]).reshape(N_DEV, 8, D)[0::2, 0:1]
    gfull["c_ctx"] = cctx_grad("cctx_grad", ds_all, c_ctx[None])[0]
    gfull["conv_dw_w"] = lax.dynamic_slice(gfull["conv_dw_w"], (0, 0, chip * ncw), (2, cfg.CK, ncw))
    gfull["ffn_dw_w"] = lax.dynamic_slice(gfull["ffn_dw_w"], (0, 0, chip * nfw), (2, 3, nfw))

    adamw_big("w_ada")
    smalls = [nm for nm in order if nm not in bigs]
    shapes = [W[nm].shape for nm in smalls]
    d_, m_, v_ = adamw("adamw_small", _pack([W[nm] for nm in smalls]), _pack([gfull[nm] for nm in smalls]),
                       _pack([Mo[nm] for nm in smalls]), _pack([Vo[nm] for nm in smalls]))
    for nm, a, b, e in zip(smalls, _unpack(d_, shapes), _unpack(m_, shapes), _unpack(v_, shapes)):
        delta[nm], new_m[nm], new_v[nm] = a, b, e
    return (loss, gx[None], *[gfull[nm] for nm in order], *[delta[nm] for nm in order],
            *[new_m[nm] for nm in order], *[new_v[nm] for nm in order])
```

```python
import collections
import functools

import numpy as np
import jax
import jax.numpy as jnp
from jax import lax
from jax.experimental import pallas as pl
from jax.experimental.pallas import tpu as pltpu

F32 = jnp.float32
BF16 = jnp.bfloat16
EPS = 1e-6
ROPE_BASE = 10000.0
NEG = -1e30
LANE = 128
VMEM_LIMIT = 56 * 1024 * 1024

ADAM_LR, ADAM_B1, ADAM_B2, ADAM_EPS, ADAM_WD, ADAM_STEP = 0.001, 0.9, 0.999, 1e-08, 0.01, 10

Cfg = collections.namedtuple(
    "Cfg", "D T TC GW RH RDK RDV CW CK NH NDH NAR NAC DFF TB")


def make_cfg(D=2048, T=4096, TC=256, RH=4, CW=512, NH=4, DFF=5632):
    return Cfg(D=D, T=T, TC=TC, GW=64, RH=RH, RDK=128, RDV=256, CW=CW, CK=31, NH=NH, NDH=128,
               NAR=8, NAC=16, DFF=DFF, TB=256)


def _offsets(cfg):
    sizes = [cfg.RH * cfg.RDK, cfg.RH * cfg.RDK, cfg.RH * cfg.RDV, cfg.RH * cfg.RDV, cfg.CW, cfg.CW,
             cfg.NH * cfg.NDH, cfg.NH * cfg.NDH, cfg.NH * cfg.NDH]
    offs = [0]
    for s in sizes:
        offs.append(offs[-1] + s)
    return dict(zip(["lq", "lk", "lv", "lg", "la", "lb", "nq", "nk", "nv", "end"], offs))


def _pc(body, **kw):
    return pl.pallas_call(body, **kw)


def _cp(sem=None):
    return pltpu.CompilerParams(dimension_semantics=sem, vmem_limit_bytes=VMEM_LIMIT)


def _dot(a, b, ca, cb):
    return lax.dot_general(a, b, (((ca,), (cb,)), ((), ())), preferred_element_type=F32)


def dot_nn(a, b):
    return _dot(a, b, 1, 0)


def dot_nt(a, b):
    return _dot(a, b, 1, 1)


def dot_tn(a, b):
    return _dot(a, b, 0, 0)


def _sigmoid(x):
    return 1.0 / (1.0 + jnp.exp(-x))


def _silu(x):
    return x * _sigmoid(x)


def _dsilu(x):
    s = _sigmoid(x)
    return s * (1.0 + x * (1.0 - s))


def matmul(name, a, b, *, contract, grid, a_spec, b_spec, out_shape, out_spec, nk, into=None):
    dot = {"nn": dot_nn, "nt": dot_nt, "tn": dot_tn}[contract]
    direct = nk > 1 and out_shape.dtype == F32
    kax = len(grid) - 1

    def body(a_ref, b_ref, *rest):
        o_ref, *scr = rest[1:] if into is not None else rest
        p = dot(a_ref[...].astype(BF16), b_ref[...].astype(BF16))
        if nk == 1:
            o_ref[...] = p.astype(o_ref.dtype)
            return
        acc = o_ref if direct else scr[0]
        k = pl.program_id(kax)

        @pl.when(k == 0)
        def _():
            acc[...] = p

        @pl.when(k > 0)
        def _():
            acc[...] += p

        if not direct:
            @pl.when(k == nk - 1)
            def _():
                o_ref[...] = acc[...].astype(o_ref.dtype)

    scratch = []
    if nk > 1 and not direct:
        blk = [s for s in out_spec.block_shape if s is not None]
        scratch = [pltpu.VMEM(tuple(blk), F32)]
    sem = ("parallel",) * kax + (("arbitrary",) if nk > 1 else ("parallel",))
    in_specs, args, alias = [a_spec, b_spec], (a, b), {}
    if into is not None:
        in_specs, args, alias = in_specs + [pl.BlockSpec(memory_space=pl.ANY)], (a, b, into), {2: 0}
    return _pc(body, name=name, grid=grid, in_specs=in_specs, out_specs=out_spec, out_shape=out_shape,
               scratch_shapes=scratch, input_output_aliases=alias, compiler_params=_cp(sem))(*args)


_WG_ROWS = 1024


def wgrad(cfg, name, a, dc, a_spec, dc_spec, out_shape, out_spec, ntiles, into):
    T, TC = cfg.T, cfg.TC
    tml = min(_WG_ROWS, T)
    nl = T // tml

    def body(al_ref, ac_ref, dl_ref, dcx_ref, *rest):
        o_ref, acc = rest[-2:]
        m = pl.program_id(1)

        @pl.when(m == 0)
        def _():
            acc[...] = dot_tn(al_ref[...], dl_ref[...])

        @pl.when(jnp.logical_and(m > 0, m < nl))
        def _():
            acc[...] += dot_tn(al_ref[...], dl_ref[...])

        @pl.when(m == nl)
        def _():
            o_ref[...] = (acc[...] + dot_tn(ac_ref[...], dcx_ref[...])).astype(o_ref.dtype)

    lat = lambda m: jnp.minimum(m, nl - 1)
    ctx = lambda m: T // TC
    in_specs = [a_spec(tml, lat), a_spec(TC, ctx), dc_spec(tml, lat), dc_spec(TC, ctx)]
    args, alias = (a, a, dc, dc), {}
    if into is not None:
        in_specs, args, alias = in_specs + [pl.BlockSpec(memory_space=pl.ANY)], args + (into,), {4: 0}
    blk = tuple(s for s in out_spec.block_shape if s is not None)
    return _pc(body, name=name, grid=(ntiles, nl + 1), in_specs=in_specs, out_specs=out_spec, out_shape=out_shape,
               scratch_shapes=[pltpu.VMEM(blk, F32)], input_output_aliases=alias,
               compiler_params=_cp(("parallel", "arbitrary")))(*args)


def mm_rowsharded(name, a, w4, l, out_dtype, tn):
    pieces = a if isinstance(a, tuple) else (a,)
    L = pieces[0].shape[0]
    nch, _, Kb, N = w4.shape
    tm = _tm(L, 8)
    assert all(p.shape[1] % Kb == 0 for p in pieces) and sum(p.shape[1] for p in pieces) == nch * Kb

    def body(*refs):
        w_ref, o_ref = refs[-2:]
        acc, j = None, 0
        for a_ref in refs[:-2]:
            for b in range(a_ref.shape[1] // Kb):
                p = dot_nn(a_ref[:, b * Kb:(b + 1) * Kb], w_ref[j])
                acc = p if acc is None else acc + p
                j += 1
        o_ref[...] = acc.astype(o_ref.dtype)

    return _pc(body, name=name, grid=(N // tn, L // tm),
               in_specs=[pl.BlockSpec((tm, p.shape[1]), lambda n, m: (m, 0)) for p in pieces]
               + [pl.BlockSpec((nch, None, Kb, tn), lambda n, m: (0, l, 0, n))],
               out_specs=pl.BlockSpec((tm, tn), lambda n, m: (m, n)),
               out_shape=jax.ShapeDtypeStruct((L, N), out_dtype),
               compiler_params=_cp(("parallel", "parallel")))(*pieces, w4)


def _region(cfg):
    nlat = cfg.T // cfg.TB
    return lambda i: jnp.minimum(i // nlat, 1)


def norm_mod_fwd(cfg, name, x, ng, sc, sh):
    L, D = x.shape
    TB = cfg.TB
    reg = _region(cfg)

    def body(x_ref, ng_ref, sc_ref, sh_ref, h_ref):
        xv = x_ref[...]
        r = lax.rsqrt(jnp.mean(xv * xv, axis=-1, keepdims=True) + EPS)
        n = xv * r * ng_ref[...]
        h_ref[...] = (n * (1.0 + sc_ref[...]) + sh_ref[...]).astype(h_ref.dtype)

    row = pl.BlockSpec((TB, D), lambda i: (i, 0))
    vec = pl.BlockSpec((1, D), lambda i: (0, 0))
    rvec = pl.BlockSpec((None, 1, D), lambda i: (reg(i), 0, 0))
    return _pc(body, name=name, grid=(L // TB,), in_specs=[row, vec, rvec, rvec], out_specs=row,
               out_shape=jax.ShapeDtypeStruct((L, D), BF16), compiler_params=_cp(("parallel",)))(x, ng, sc, sh)


def norm_mod_bwd(cfg, name, dh, x, ng, sc, dx_in, latent_only=False):
    L, D = x.shape
    TB = cfg.TB
    nlat = cfg.T // TB
    reg = _region(cfg)

    def body(dh_ref, x_ref, ng_ref, sc_ref, dxi_ref, dx_ref, dsc_ref, dsh_ref, dng_ref):
        i = pl.program_id(0)
        xv = x_ref[...]
        r = lax.rsqrt(jnp.mean(xv * xv, axis=-1, keepdims=True) + EPS)
        xh = xv * r
        g = ng_ref[...]
        n = xh * g
        dh = dh_ref[...]
        dn = dh * (1.0 + sc_ref[...])
        dxh = dn * g
        dx = r * (dxh - xh * jnp.mean(dxh * xh, axis=-1, keepdims=True))
        if latent_only:
            @pl.when(i < nlat)
            def _():
                dx_ref[...] = dxi_ref[...] + dx
        else:
            dx_ref[...] = dxi_ref[...] + dx
        s_sh = jnp.sum(dh, axis=0, keepdims=True)
        s_sc = jnp.sum(dh * n, axis=0, keepdims=True)
        s_ng = jnp.sum(dn * xh, axis=0, keepdims=True)
        first = jnp.logical_or(i == 0, i == nlat)

        @pl.when(first)
        def _():
            dsh_ref[...] = s_sh
            dsc_ref[...] = s_sc

        @pl.when(jnp.logical_not(first))
        def _():
            dsh_ref[...] += s_sh
            dsc_ref[...] += s_sc

        @pl.when(i == 0)
        def _():
            dng_ref[...] = s_ng

        @pl.when(i > 0)
        def _():
            dng_ref[...] += s_ng

    row = pl.BlockSpec((TB, D), lambda i: (i, 0))
    vec = pl.BlockSpec((1, D), lambda i: (0, 0))
    rvec = pl.BlockSpec((None, 1, D), lambda i: (reg(i), 0, 0))
    dxs = pl.BlockSpec((TB, D), lambda i: (jnp.minimum(i, nlat - 1), 0)) if latent_only else row
    return _pc(body, name=name, grid=(L // TB,), in_specs=[row, row, vec, rvec, row],
               out_specs=[dxs, rvec, rvec, vec],
               out_shape=[jax.ShapeDtypeStruct((cfg.T if latent_only else L, D), F32),
                          jax.ShapeDtypeStruct((2, 1, D), F32),
                          jax.ShapeDtypeStruct((2, 1, D), F32), jax.ShapeDtypeStruct((1, D), F32)],
               compiler_params=_cp(("arbitrary",)))(dh, x, ng, sc, dx_in)


def resid_norm_fwd(cfg, name, x, y, g, ng, sc, sh):
    L, D = x.shape
    TB = cfg.TB
    reg = _region(cfg)

    def body(x_ref, y_ref, g_ref, ng_ref, sc_ref, sh_ref, xo_ref, h_ref):
        xv = x_ref[...] + g_ref[...] * y_ref[...]
        xo_ref[...] = xv
        r = lax.rsqrt(jnp.mean(xv * xv, axis=-1, keepdims=True) + EPS)
        n = xv * r * ng_ref[...]
        h_ref[...] = (n * (1.0 + sc_ref[...]) + sh_ref[...]).astype(h_ref.dtype)

    row = pl.BlockSpec((TB, D), lambda i: (i, 0))
    vec = pl.BlockSpec((1, D), lambda i: (0, 0))
    rvec = pl.BlockSpec((None, 1, D), lambda i: (reg(i), 0, 0))
    return _pc(body, name=name, grid=(L // TB,), in_specs=[row, row, rvec, vec, rvec, rvec], out_specs=[row, row],
               out_shape=[jax.ShapeDtypeStruct((L, D), F32), jax.ShapeDtypeStruct((L, D), BF16)],
               compiler_params=_cp(("parallel",)))(x, y, g, ng, sc, sh)


def resid_bwd(cfg, name, dxo, y, g):
    L, D = y.shape
    TB = cfg.TB
    nlat = cfg.T // TB
    reg = _region(cfg)

    def body(d_ref, y_ref, g_ref, dy_ref, dg_ref):
        i = pl.program_id(0)
        d = d_ref[...]
        dy_ref[...] = (d * g_ref[...]).astype(dy_ref.dtype)
        s = jnp.sum(d * y_ref[...], axis=0, keepdims=True)
        first = jnp.logical_or(i == 0, i == nlat)

        @pl.when(first)
        def _():
            dg_ref[...] = s

        @pl.when(jnp.logical_not(first))
        def _():
            dg_ref[...] += s

    row = pl.BlockSpec((TB, D), lambda i: (i, 0))
    rvec = pl.BlockSpec((None, 1, D), lambda i: (reg(i), 0, 0))
    return _pc(body, name=name, grid=(L // TB,), in_specs=[row, row, rvec], out_specs=[row, rvec],
               out_shape=[jax.ShapeDtypeStruct((L, D), BF16), jax.ShapeDtypeStruct((2, 1, D), F32)],
               compiler_params=_cp(("arbitrary",)))(dxo, y, g)


def final_loss(cfg, name, xm, y2, g2, fg, tgt):
    L, D = xm.shape
    TB = cfg.TB
    nlat = cfg.T // TB

    def body(x_ref, y_ref, g2_ref, fg_ref, t_ref, ls_ref, dx_ref, dg_ref):
        i = pl.program_id(0)

        @pl.when(i == 0)
        def _():
            ls_ref[...] = jnp.zeros_like(ls_ref)
            dg_ref[...] = jnp.zeros_like(dg_ref)

        @pl.when(i < nlat)
        def _():
            xv = x_ref[...] + g2_ref[...] * y_ref[...]
            r = lax.rsqrt(jnp.mean(xv * xv, axis=-1, keepdims=True) + EPS)
            xh = xv * r
            g = fg_ref[...]
            e = xh * g - t_ref[...]
            ls_ref[...] += 0.5 * jnp.sum(e * e) / D
            dy = e / D
            dg_ref[...] += jnp.sum(dy * xh, axis=0, keepdims=True)
            dxh = dy * g
            dx_ref[...] = r * (dxh - xh * jnp.mean(dxh * xh, axis=-1, keepdims=True))

        @pl.when(i >= nlat)
        def _():
            dx_ref[...] = jnp.zeros_like(dx_ref)

    row = pl.BlockSpec((TB, D), lambda i: (i, 0))
    trow = pl.BlockSpec((TB, D), lambda i: (jnp.minimum(i, nlat - 1), 0))
    vec = pl.BlockSpec((1, D), lambda i: (0, 0))
    return _pc(body, name=name, grid=(L // TB,), in_specs=[row, row, vec, vec, trow],
               out_specs=[pl.BlockSpec((1, LANE), lambda i: (0, 0)), row, vec],
               out_shape=[jax.ShapeDtypeStruct((1, LANE), F32), jax.ShapeDtypeStruct((L, D), F32),
                          jax.ShapeDtypeStruct((1, D), F32)],
               compiler_params=_cp(("arbitrary",)))(xm, y2, g2, fg, tgt)


def rope_tables(cfg):
    half = cfg.RDK // 2
    nf = half // 2
    pos = np.arange(cfg.T)
    row = (pos // cfg.GW).astype(np.float32)
    col = (pos % cfg.GW).astype(np.float32)
    inv = jnp.asarray(ROPE_BASE, F32) ** (-jnp.arange(nf, dtype=F32) / nf)
    ar = jnp.asarray(row)[:, None] * inv[None, :]
    ac = jnp.asarray(col)[:, None] * inv[None, :]
    cos = jnp.concatenate([jnp.cos(ar), jnp.cos(ar), jnp.cos(ac), jnp.cos(ac)], axis=1)
    sin = jnp.concatenate([-jnp.sin(ar), jnp.sin(ar), -jnp.sin(ac), jnp.sin(ac)], axis=1)
    cos = jnp.concatenate([cos, jnp.ones((cfg.TC, cfg.RDK), F32)], axis=0)
    sin = jnp.concatenate([sin, jnp.zeros((cfg.TC, cfg.RDK), F32)], axis=0)
    return cos, sin


def _rb(cfg):
    rb = (cfg.T + cfg.TC) // 4
    assert rb % 16 == 0
    return rb


def _swap32(t):
    lane = lax.broadcasted_iota(jnp.int32, t.shape, 1)
    return jnp.where((lane % 64) < 32, pltpu.roll(t, 96, 1), pltpu.roll(t, 32, 1))


def rope_fwd(cfg, name, P, cos, sin):
    L = P.shape[0]
    TB = _rb(cfg)
    off = _offsets(cfg)
    cq, ck = off["lq"] // LANE, off["lk"] // LANE
    scale = cfg.RDK ** -0.5

    def body(q_ref, k_ref, c_ref, s_ref, qo_ref, ko_ref):
        c = c_ref[...]
        s = s_ref[...]
        q = q_ref[...]
        k = k_ref[...]
        qo_ref[...] = (q * c + _swap32(q) * s) * scale
        ko_ref[...] = k * c + _swap32(k) * s

    tab = pl.BlockSpec((TB, LANE), lambda i, h: (i, 0))
    out = pl.BlockSpec((TB, LANE), lambda i, h: (i, h))
    shp = jax.ShapeDtypeStruct((L, cfg.RH * cfg.RDK), F32)
    return _pc(body, name=name, grid=(L // TB, cfg.RH),
               in_specs=[pl.BlockSpec((TB, LANE), lambda i, h: (i, cq + h)),
                         pl.BlockSpec((TB, LANE), lambda i, h: (i, ck + h)), tab, tab],
               out_specs=[out, out], out_shape=[shp, shp],
               compiler_params=_cp(("parallel", "parallel")))(P, P, cos, sin)


def rope_bwd(cfg, name, dq2, dk2, cos, sin):
    L, W = dq2[0].shape
    TB = _rb(cfg)
    scale = cfg.RDK ** -0.5

    def body(dqf_ref, dqb_ref, dkf_ref, dkb_ref, c_ref, s_ref, qo_ref, ko_ref):
        c = c_ref[...]
        s = s_ref[...]
        dq = dqf_ref[...] + dqb_ref[...]
        dk = dkf_ref[...] + dkb_ref[...]
        qo_ref[...] = ((dq * c - _swap32(dq) * s) * scale).astype(qo_ref.dtype)
        ko_ref[...] = (dk * c - _swap32(dk) * s).astype(ko_ref.dtype)

    tab = pl.BlockSpec((TB, LANE), lambda i, h: (i, 0))
    blk = pl.BlockSpec((TB, LANE), lambda i, h: (i, h))
    shp = jax.ShapeDtypeStruct((L, W), BF16)
    return _pc(body, name=name, grid=(L // TB, cfg.RH), in_specs=[blk, blk, blk, blk, tab, tab],
               out_specs=[blk, blk], out_shape=[shp, shp],
               compiler_params=_cp(("parallel", "parallel")))(*dq2, *dk2, cos, sin)


def _ret_chunk_map(cfg):
    C = cfg.RDK
    n = (cfg.T + cfg.TC) // C
    nlat, nctx = cfg.T // C, cfg.TC // C

    def chunk(d, s):
        if d == 0:
            return jnp.where(s < nctx, nlat + s, s - nctx)
        return n - 1 - s

    return n, chunk


def _ret_decay_terms(d, lam, C):
    ii = lax.broadcasted_iota(jnp.int32, (C, C), 0)
    jj = lax.broadcasted_iota(jnp.int32, (C, C), 1)
    diff = (ii - jj if d == 0 else jj - ii).astype(F32)
    dpos = jnp.maximum(diff, 0.0)
    Dm = jnp.where(diff >= 0, jnp.exp(dpos * lam), 0.0)
    ic = lax.broadcasted_iota(jnp.int32, (C, 1), 0).astype(F32)
    cxi = ic + 1.0 if d == 0 else C - ic
    cze = C - 1.0 - ic if d == 0 else ic
    xi = jnp.exp(cxi * lam)
    ze = jnp.exp(cze * lam)
    g = jnp.exp(jnp.full((1, 1), C, F32) * lam)
    return dpos, Dm, cxi, cze, xi, ze, g


def retention_fwd(cfg, name, qr, kr, P, lam):
    L = P.shape[0]
    C, DV, RH = cfg.RDK, cfg.RDV, cfg.RH
    n, chunk = _ret_chunk_map(cfg)

    def body(lam_ref, qf_ref, qb_ref, kf_ref, kb_ref, vf_ref, vb_ref, of_ref, ob_ref, st_ref, S):
        s = pl.program_id(0)

        @pl.when(s == 0)
        def _():
            S[...] = jnp.zeros_like(S)

        for d, (q_ref, k_ref, v_ref, o_ref) in enumerate(((qf_ref, kf_ref, vf_ref, of_ref),
                                                          (qb_ref, kb_ref, vb_ref, ob_ref))):
            for h in range(RH):
                _, Dm, _, _, xi, ze, g = _ret_decay_terms(d, lam_ref[d, h], C)
                k = k_ref[:, h * C:(h + 1) * C]
                qb = q_ref[:, h * C:(h + 1) * C].astype(BF16)
                kb = k.astype(BF16)
                vb = v_ref[:, h * DV:(h + 1) * DV].astype(BF16)
                Sv = S[d, h]
                st_ref[d, h] = Sv
                A = dot_nt(qb, kb) * Dm
                o_ref[:, h * DV:(h + 1) * DV] = dot_nn(A.astype(BF16), vb) + dot_nn(qb, Sv.astype(BF16)) * xi
                S[d, h] = Sv * g + dot_tn((k * ze).astype(BF16), vb)

    def spec(w, col, d):
        return pl.BlockSpec((C, w), lambda s: (chunk(d, s), col))

    W, WV = RH * C, RH * DV
    return _pc(body, name=name, grid=(n,),
               in_specs=[pl.BlockSpec(memory_space=pltpu.SMEM), spec(W, 0, 0), spec(W, 0, 1), spec(W, 0, 0),
                         spec(W, 0, 1), spec(WV, 1, 0), spec(WV, 1, 1)],
               out_specs=[spec(WV, 0, 0), spec(WV, 0, 1),
                          pl.BlockSpec((2, RH, None, C, DV), lambda s: (0, 0, s, 0, 0))],
               out_shape=[jax.ShapeDtypeStruct((L, WV), F32), jax.ShapeDtypeStruct((L, WV), F32),
                          jax.ShapeDtypeStruct((2, RH, n, C, DV), F32)],
               scratch_shapes=[pltpu.VMEM((2, RH, C, DV), F32)],
               compiler_params=_cp(("arbitrary",)))(lam, qr, qr, kr, kr, P, P)


def retention_bwd(cfg, name, qr, kr, P, lam, st, do):
    L = P.shape[0]
    C, DV, RH = cfg.RDK, cfg.RDV, cfg.RH
    n, chunk = _ret_chunk_map(cfg)

    def body(lam_ref, qf_ref, qb_ref, kf_ref, kb_ref, vf_ref, vb_ref, st_ref, dof_ref, dob_ref,
             dqf_ref, dqb_ref, dkf_ref, dkb_ref, dvf_ref, dvb_ref, dl_ref, dS):
        si = pl.program_id(0)

        @pl.when(si == 0)
        def _():
            dS[...] = jnp.zeros_like(dS)
            dl_ref[...] = jnp.zeros_like(dl_ref)

        dirs = ((qf_ref, kf_ref, vf_ref, dof_ref, dqf_ref, dkf_ref, dvf_ref),
                (qb_ref, kb_ref, vb_ref, dob_ref, dqb_ref, dkb_ref, dvb_ref))
        for d, (q_ref, k_ref, v_ref, do_ref, dq_ref, dk_ref, dv_ref) in enumerate(dirs):
            for h in range(RH):
                dpos, Dm, cxi, cze, xi, ze, g = _ret_decay_terms(d, lam_ref[d, h], C)
                hk = slice(h * C, (h + 1) * C)
                hv = slice(h * DV, (h + 1) * DV)
                k = k_ref[:, hk]
                do = do_ref[:, hv]
                qb = q_ref[:, hk].astype(BF16)
                kb = k.astype(BF16)
                vb = v_ref[:, hv].astype(BF16)
                dob = do.astype(BF16)
                Sn = st_ref[d, h]
                Snb = Sn.astype(BF16)
                dSn = dS[d, h]
                dSb = dSn.astype(BF16)
                A = dot_nt(qb, kb) * Dm
                dA = dot_nt(dob, vb)
                dQK = (dA * Dm).astype(BF16)
                kzb = (k * ze).astype(BF16)
                dv_ref[:, hv] = dot_tn(A.astype(BF16), dob) + dot_nn(kzb, dSb)
                dkz = dot_nt(vb, dSb)
                doxb = (do * xi).astype(BF16)
                dq_ref[:, hk] = dot_nn(dQK, kb) + dot_nt(doxb, Snb)
                dk_ref[:, hk] = dot_tn(dQK, qb) + dkz * ze
                QS = dot_nn(qb, Snb)
                t = (jnp.sum(dA * A * dpos) + jnp.sum(do * QS * (cxi * xi)) + jnp.sum(k * dkz * (cze * ze)))
                t4 = jnp.sum(dSn * Sn, axis=0, keepdims=True)
                t4 = jnp.sum(t4 * (g * C), axis=1, keepdims=True)
                dl_ref[d, h] += t + t4
                dS[d, h] = g * dSn + dot_tn(qb, doxb)

    def spec(w, col, d):
        return pl.BlockSpec((C, w), lambda si: (chunk(d, n - 1 - si), col))

    W, WV = RH * C, RH * DV
    return _pc(body, name=name, grid=(n,),
               in_specs=[pl.BlockSpec(memory_space=pltpu.SMEM), spec(W, 0, 0), spec(W, 0, 1), spec(W, 0, 0),
                         spec(W, 0, 1), spec(WV, 1, 0), spec(WV, 1, 1),
                         pl.BlockSpec((2, RH, None, C, DV), lambda si: (0, 0, n - 1 - si, 0, 0)),
                         spec(WV, 0, 0), spec(WV, 0, 1)],
               out_specs=[spec(W, 0, 0), spec(W, 0, 1), spec(W, 0, 0), spec(W, 0, 1), spec(WV, 0, 0), spec(WV, 0, 1),
                          pl.BlockSpec((2, RH, 8, LANE), lambda si: (0, 0, 0, 0))],
               out_shape=[jax.ShapeDtypeStruct((L, W), F32)] * 4 + [jax.ShapeDtypeStruct((L, WV), F32)] * 2
               + [jax.ShapeDtypeStruct((2, RH, 8, LANE), F32)],
               scratch_shapes=[pltpu.VMEM((2, RH, C, DV), F32)],
               compiler_params=_cp(("arbitrary",)))(lam, qr, qr, kr, kr, P, P, st, do, do)


def add_cast(cfg, name, a, b):
    L, W = a.shape
    TB = _rb(cfg)

    def body(a_ref, b_ref, o_ref):
        o_ref[...] = (a_ref[...] + b_ref[...]).astype(o_ref.dtype)

    spec = pl.BlockSpec((TB, W), lambda i: (i, 0))
    return _pc(body, name=name, grid=(L // TB,), in_specs=[spec, spec], out_specs=spec,
               out_shape=jax.ShapeDtypeStruct((L, W), BF16), compiler_params=_cp(("parallel",)))(a, b)


def ggn_fwd(cfg, name, o2, P, gn_g):
    L = P.shape[0]
    TB, DV, RH = _rb(cfg), cfg.RDV, cfg.RH
    gc0 = _offsets(cfg)["lg"] // DV

    def body(of_ref, ob_ref, gate_ref, g_ref, out_ref):
        o = of_ref[...] + ob_ref[...]
        mu = jnp.mean(o, axis=-1, keepdims=True)
        xc = o - mu
        var = jnp.mean(xc * xc, axis=-1, keepdims=True)
        y = xc * lax.rsqrt(var + EPS) * g_ref[...]
        out_ref[...] = (y * _silu(gate_ref[...])).astype(out_ref.dtype)

    blk = pl.BlockSpec((TB, DV), lambda i, h: (i, h))
    return _pc(body, name=name, grid=(L // TB, RH),
               in_specs=[blk, blk, pl.BlockSpec((TB, DV), lambda i, h: (i, gc0 + h)),
                         pl.BlockSpec((1, DV), lambda i, h: (0, h))],
               out_specs=blk, out_shape=jax.ShapeDtypeStruct((L, RH * DV), BF16),
               compiler_params=_cp(("parallel", "parallel")))(*o2, P, gn_g)


def ggn_bwd(cfg, name, dout, o2, P, gn_g, col0):
    L = P.shape[0]
    TB, DV, RH = _rb(cfg), cfg.RDV, cfg.RH
    gc0 = _offsets(cfg)["lg"] // DV

    def body(d_ref, of_ref, ob_ref, gate_ref, g_ref, do_ref, dgate_ref, dg_ref):
        i = pl.program_id(1)
        o = of_ref[...] + ob_ref[...]
        mu = jnp.mean(o, axis=-1, keepdims=True)
        xc = o - mu
        var = jnp.mean(xc * xc, axis=-1, keepdims=True)
        r = lax.rsqrt(var + EPS)
        y = xc * r
        g = g_ref[...]
        gate = gate_ref[...]
        d = d_ref[...]
        dgate_ref[...] = (d * (y * g) * _dsilu(gate)).astype(dgate_ref.dtype)
        dyg = d * _silu(gate)
        s = jnp.sum(dyg * y, axis=0, keepdims=True)

        @pl.when(i == 0)
        def _():
            dg_ref[...] = s

        @pl.when(i > 0)
        def _():
            dg_ref[...] += s

        dy = dyg * g
        do_ref[...] = r * (dy - jnp.mean(dy, axis=-1, keepdims=True)
                           - y * jnp.mean(dy * y, axis=-1, keepdims=True))

    blk = pl.BlockSpec((TB, DV), lambda h, i: (i, h))
    return _pc(body, name=name, grid=(RH, L // TB),
               in_specs=[pl.BlockSpec((TB, DV), lambda h, i: (i, col0 + h)), blk, blk,
                         pl.BlockSpec((TB, DV), lambda h, i: (i, gc0 + h)),
                         pl.BlockSpec((1, DV), lambda h, i: (0, h))],
               out_specs=[blk, blk, pl.BlockSpec((1, DV), lambda h, i: (0, h))],
               out_shape=[jax.ShapeDtypeStruct((L, RH * DV), F32), jax.ShapeDtypeStruct((L, RH * DV), BF16),
                          jax.ShapeDtypeStruct((1, RH * DV), F32)],
               compiler_params=_cp(("parallel", "arbitrary")))(dout, *o2, P, gn_g)


def cast_cols(cfg, name, src, col0, ncols, width):
    L = src.shape[0]
    TB = _rb(cfg)

    def body(s_ref, o_ref):
        o_ref[...] = s_ref[...].astype(o_ref.dtype)

    spec = pl.BlockSpec((TB, width), lambda i, j: (i, col0 + j))
    return _pc(body, name=name, grid=(L // TB, ncols), in_specs=[spec],
               out_specs=pl.BlockSpec((TB, width), lambda i, j: (i, j)),
               out_shape=jax.ShapeDtypeStruct((L, ncols * width), BF16),
               compiler_params=_cp(("parallel", "parallel")))(src)


_CPAD = 16


def _conv_windows(cfg):
    T, TC, TB = cfg.T, cfg.TC, cfg.TB
    assert TC % TB == 0 and T % TB == 0 and cfg.CK // 2 < _CPAD
    return T // TB, [(T + j * TB, T + _CPAD + j * TB) for j in range(TC // TB)]


def _fill_padded(cfg, pb, get):
    T, TC, TB = cfg.T, cfg.TC, cfg.TB
    z = jnp.zeros((_CPAD, LANE), F32)
    pb[0:_CPAD, :] = z
    pb[_CPAD + T:2 * _CPAD + T, :] = z
    pb[2 * _CPAD + T + TC:3 * _CPAD + T + TC, :] = z

    def fill(i, c):
        r0 = pl.multiple_of(i * TB, TB)
        pb[pl.ds(r0 + _CPAD, TB), :] = get(r0)
        return c

    lax.fori_loop(0, T // TB, fill, 0)
    for j in range(TC // TB):
        pb[2 * _CPAD + T + j * TB:2 * _CPAD + T + (j + 1) * TB, :] = get(T + j * TB)


def _taps(win, TB):
    W = TB + 2 * _CPAD
    rot = {0: win}

    def tap(k):
        a, b = divmod(k + 1, 8)
        if b not in rot:
            rot[b] = pltpu.roll(win, W - b, 0)
        return rot[b][8 * a:8 * a + TB, :]

    return tap


def glu_dwconv_fwd(cfg, name, P, w, b):
    L = P.shape[0]
    T, TC, TB, K = cfg.T, cfg.TC, cfg.TB, cfg.CK
    off = _offsets(cfg)
    ca, cb = off["la"] // LANE, off["lb"] // LANE
    nlat, ctx_tiles = _conv_windows(cfg)
    PBL = 3 * _CPAD + T + TC

    def body(a_ref, b_ref, w_ref, bias_ref, y_ref, pb):
        _fill_padded(cfg, pb, lambda r0: a_ref[pl.ds(r0, TB), :] * _sigmoid(b_ref[pl.ds(r0, TB), :]))
        wv = w_ref[...]
        bias = bias_ref[...]

        def tile(win):
            tap = _taps(win, TB)
            acc = jnp.zeros((TB, LANE), F32) + bias
            for k in range(K):
                acc = acc + wv[k:k + 1, :] * tap(k)
            return acc

        def lat(i, c):
            r0 = pl.multiple_of(i * TB, TB)
            y_ref[pl.ds(r0, TB), :] = tile(pb[pl.ds(r0, TB + 2 * _CPAD), :])
            return c

        lax.fori_loop(0, nlat, lat, 0)
        for r0, w0 in ctx_tiles:
            y_ref[r0:r0 + TB, :] = tile(pb[w0:w0 + TB + 2 * _CPAD, :])

    return _pc(body, name=name, grid=(cfg.CW // LANE,),
               in_specs=[pl.BlockSpec((L, LANE), lambda j: (0, ca + j)),
                         pl.BlockSpec((L, LANE), lambda j: (0, cb + j)),
                         pl.BlockSpec((32, LANE), lambda j: (0, j)),
                         pl.BlockSpec((1, LANE), lambda j: (0, j))],
               out_specs=pl.BlockSpec((L, LANE), lambda j: (0, j)),
               out_shape=jax.ShapeDtypeStruct((L, cfg.CW), F32),
               scratch_shapes=[pltpu.VMEM((PBL, LANE), F32)],
               compiler_params=_cp(("parallel",)))(P, P, w, b)


def glu_dwconv_bwd(cfg, name, P, w, dy):
    L = P.shape[0]
    T, TC, TB, K = cfg.T, cfg.TC, cfg.TB, cfg.CK
    off = _offsets(cfg)
    ca, cb = off["la"] // LANE, off["lb"] // LANE
    nlat, ctx_tiles = _conv_windows(cfg)
    PBL = 3 * _CPAD + T + TC

    def body(a_ref, b_ref, w_ref, dy_ref, da_ref, db_ref, dw_ref, dbias_ref, pbu, pbd):
        _fill_padded(cfg, pbu, lambda r0: a_ref[pl.ds(r0, TB), :] * _sigmoid(b_ref[pl.ds(r0, TB), :]))
        _fill_padded(cfg, pbd, lambda r0: dy_ref[pl.ds(r0, TB), :])
        wv = w_ref[...]
        dw_ref[...] = jnp.zeros_like(dw_ref)
        dbias_ref[...] = jnp.zeros_like(dbias_ref)

        def tile(r0, winu, wind):
            tapu = _taps(winu, TB)
            tapd = _taps(wind, TB)
            dyt = dy_ref[pl.ds(r0, TB), :]
            du = jnp.zeros((TB, LANE), F32)
            for k in range(K):
                du = du + wv[k:k + 1, :] * tapd(K - 1 - k)
                dw_ref[k:k + 1, :] += jnp.sum(dyt * tapu(k), axis=0, keepdims=True)
            dbias_ref[...] += jnp.sum(dyt, axis=0, keepdims=True)
            a = a_ref[pl.ds(r0, TB), :]
            sg = _sigmoid(b_ref[pl.ds(r0, TB), :])
            da_ref[pl.ds(r0, TB), :] = (du * sg).astype(da_ref.dtype)
            db_ref[pl.ds(r0, TB), :] = (du * a * sg * (1.0 - sg)).astype(db_ref.dtype)

        def lat(i, c):
            r0 = pl.multiple_of(i * TB, TB)
            tile(r0, pbu[pl.ds(r0, TB + 2 * _CPAD), :], pbd[pl.ds(r0, TB + 2 * _CPAD), :])
            return c

        lax.fori_loop(0, nlat, lat, 0)
        for r0, w0 in ctx_tiles:
            tile(r0, pbu[w0:w0 + TB + 2 * _CPAD, :], pbd[w0:w0 + TB + 2 * _CPAD, :])

    col = pl.BlockSpec((L, LANE), lambda j: (0, j))
    return _pc(body, name=name, grid=(cfg.CW // LANE,),
               in_specs=[pl.BlockSpec((L, LANE), lambda j: (0, ca + j)),
                         pl.BlockSpec((L, LANE), lambda j: (0, cb + j)),
                         pl.BlockSpec((32, LANE), lambda j: (0, j)), col],
               out_specs=[col, col, pl.BlockSpec((32, LANE), lambda j: (0, j)),
                          pl.BlockSpec((1, LANE), lambda j: (0, j))],
               out_shape=[jax.ShapeDtypeStruct((L, cfg.CW), BF16), jax.ShapeDtypeStruct((L, cfg.CW), BF16),
                          jax.ShapeDtypeStruct((32, cfg.CW), F32), jax.ShapeDtypeStruct((1, cfg.CW), F32)],
               scratch_shapes=[pltpu.VMEM((PBL, LANE), F32), pltpu.VMEM((PBL, LANE), F32)],
               compiler_params=_cp(("parallel",)))(P, P, w, dy)


def ln_silu_fwd(cfg, name, y, g, b):
    L, W = y.shape
    TB = cfg.TB

    def body(y_ref, g_ref, b_ref, o_ref):
        yv = y_ref[...]
        mu = jnp.mean(yv, axis=-1, keepdims=True)
        xc = yv - mu
        var = jnp.mean(xc * xc, axis=-1, keepdims=True)
        z = xc * lax.rsqrt(var + EPS) * g_ref[...] + b_ref[...]
        o_ref[...] = _silu(z).astype(o_ref.dtype)

    row = pl.BlockSpec((TB, W), lambda i: (i, 0))
    vec = pl.BlockSpec((1, W), lambda i: (0, 0))
    return _pc(body, name=name, grid=(L // TB,), in_specs=[row, vec, vec], out_specs=row,
               out_shape=jax.ShapeDtypeStruct((L, W), BF16), compiler_params=_cp(("parallel",)))(y, g, b)


def ln_silu_bwd(cfg, name, dact, y, g, b):
    L, W = y.shape
    TB = cfg.TB

    def body(d_ref, y_ref, g_ref, b_ref, dy_ref, dg_ref, db_ref):
        i = pl.program_id(0)
        yv = y_ref[...]
        mu = jnp.mean(yv, axis=-1, keepdims=True)
        xc = yv - mu
        var = jnp.mean(xc * xc, axis=-1, keepdims=True)
        r = lax.rsqrt(var + EPS)
        yh = xc * r
        g = g_ref[...]
        z = yh * g + b_ref[...]
        dz = d_ref[...] * _dsilu(z)
        sg = jnp.sum(dz * yh, axis=0, keepdims=True)
        sb = jnp.sum(dz, axis=0, keepdims=True)

        @pl.when(i == 0)
        def _():
            dg_ref[...] = sg
            db_ref[...] = sb

        @pl.when(i > 0)
        def _():
            dg_ref[...] += sg
            db_ref[...] += sb

        dh = dz * g
        dy_ref[...] = r * (dh - jnp.mean(dh, axis=-1, keepdims=True)
                           - yh * jnp.mean(dh * yh, axis=-1, keepdims=True))

    row = pl.BlockSpec((TB, W), lambda i: (i, 0))
    vec = pl.BlockSpec((1, W), lambda i: (0, 0))
    return _pc(body, name=name, grid=(L // TB,), in_specs=[row, row, vec, vec], out_specs=[row, vec, vec],
               out_shape=[jax.ShapeDtypeStruct((L, W), F32), jax.ShapeDtypeStruct((1, W), F32),
                          jax.ShapeDtypeStruct((1, W), F32)],
               compiler_params=_cp(("arbitrary",)))(dact, y, g, b)


def _na_geometry(cfg):
    R = cfg.T // cfg.GW
    nb = R // cfg.NAR
    assert nb >= 3 and cfg.GW == 64 and cfg.NAR == 8
    ks = [int(np.clip(8 * b - 4, 0, R - 16)) for b in range(nb)]
    return R, nb, ks


_NTAB = 18


def _split3(x):
    hi = x.astype(BF16)
    r = x - hi.astype(F32)
    mid = r.astype(BF16)
    lo = (r - mid.astype(F32)).astype(BF16)
    return hi, mid, lo


def _na_col_onehot(cfg):
    GW, NAC = cfg.GW, cfg.NAC
    qc = np.arange(GW)[:, None]
    kc = np.arange(GW)[None, :]
    cs = np.clip(qc - NAC // 2, 0, GW - NAC)
    vcol = (kc >= cs) & (kc < cs + NAC)
    dd = np.clip(kc - qc + NAC - 1, 0, 2 * NAC - 2)
    oh = (np.arange(LANE)[:, None, None] == dd[None]).astype(np.float32)
    z = np.zeros_like(oh)
    oda = np.concatenate([oh, z], axis=2).reshape(LANE, GW * LANE)
    odb = np.concatenate([z, oh], axis=2).reshape(LANE, GW * LANE)
    cm = np.where(np.concatenate([vcol, vcol], axis=1), 0.0, NEG).astype(np.float32).reshape(1, GW * LANE)
    return oda, odb, cm


def na_tables(cfg, name, rpb):
    NH, GW = cfg.NH, cfg.GW
    na = rpb.shape[1]
    oda, odb, cm = _na_col_onehot(cfg)
    rp = jnp.zeros((NH, _NTAB + 1, LANE), F32).at[:, 1:1 + na, :rpb.shape[2]].set(rpb.astype(F32))
    r0 = rp[:, :_NTAB].reshape(NH * _NTAB, LANE)
    r1 = rp[:, 1:].reshape(NH * _NTAB, LANE)
    a = np.arange(_NTAB) - 1
    rm0 = np.where((a >= 0) & (a < na), 0.0, NEG).astype(np.float32)
    rm1 = np.where((a + 1 >= 0) & (a + 1 < na), 0.0, NEG).astype(np.float32)
    half = (np.arange(GW * LANE) % LANE >= GW)[None, :]
    rmask = np.where(half, np.tile(rm1, NH)[:, None], np.tile(rm0, NH)[:, None]).astype(np.float32)
    tn = 2048
    rows = NH * _NTAB

    def body(r0_ref, r1_ref, a_ref, b_ref, cm_ref, rm_ref, o_ref):
        acc = cm_ref[...] + rm_ref[...]
        for t in _split3(r0_ref[...]):
            acc = acc + dot_nn(t, a_ref[...])
        for t in _split3(r1_ref[...]):
            acc = acc + dot_nn(t, b_ref[...])
        o_ref[...] = acc

    rs = pl.BlockSpec((rows, LANE), lambda n: (0, 0))
    out = _pc(body, name=name, grid=(GW * LANE // tn,),
              in_specs=[rs, rs, pl.BlockSpec((LANE, tn), lambda n: (0, n)), pl.BlockSpec((LANE, tn), lambda n: (0, n)),
                        pl.BlockSpec((1, tn), lambda n: (0, n)), pl.BlockSpec((rows, tn), lambda n: (0, n))],
              out_specs=pl.BlockSpec((rows, tn), lambda n: (0, n)),
              out_shape=jax.ShapeDtypeStruct((rows, GW * LANE), F32),
              compiler_params=_cp(("parallel",)))(r0, r1, jnp.asarray(oda, BF16), jnp.asarray(odb, BF16),
                                                  jnp.asarray(cm), jnp.asarray(rmask))
    return out.reshape(NH, _NTAB, GW, LANE)


def _na_tiles(cfg, b):
    R, nb, _ = _na_geometry(cfg)
    NAR = cfg.NAR
    ksb = jnp.clip(8 * b - 4, 0, R - 16)
    for i in range(8):
        qr = 8 * b + i
        ws = jnp.clip(qr - NAR // 2, 0, R - NAR)
        for J in range(8):
            kr0 = ksb + 2 * J
            row = jnp.clip(kr0 - qr + NAR - 1, -1, _NTAB - 2) + 1
            v0 = jnp.logical_and(kr0 >= ws, kr0 < ws + NAR)
            v1 = jnp.logical_and(kr0 + 1 >= ws, kr0 + 1 < ws + NAR)
            yield i, J, row, v0, v1


def _na_fill_bias(cfg, tab_ref, bias, b):
    GW = cfg.GW
    first = lax.broadcasted_iota(jnp.int32, (GW, LANE), 1) < GW
    for i, J, row, v0, v1 in _na_tiles(cfg, b):
        ok = jnp.where(first, v0.astype(jnp.int32), v1.astype(jnp.int32))
        bias[i * GW:(i + 1) * GW, J * LANE:(J + 1) * LANE] = jnp.where(ok > 0, tab_ref[row], NEG)


def _na_specs(cfg):
    R, nb, ks = _na_geometry(cfg)
    off = _offsets(cfg)
    TQ = 8 * cfg.GW
    KP = 4 * cfg.GW
    ks4 = [k // 4 for k in ks]
    lat_blocks = cfg.T // KP

    def ks4_of(b):
        return jnp.clip(2 * b - 1, 0, R // 4 - 4)

    assert all(int(np.clip(2 * b - 1, 0, R // 4 - 4)) == ks4[b] for b in range(nb))
    assert cfg.TC == KP

    def col(nm):
        c0 = off[nm] // LANE
        q = pl.BlockSpec((TQ, LANE), lambda h, b: (b, c0 + h))
        parts = [pl.BlockSpec((KP, LANE), functools.partial(lambda h, b, t: (ks4_of(b) + t, c0 + h), t=t))
                 for t in range(4)]
        ctx = pl.BlockSpec((KP, LANE), lambda h, b: (lat_blocks, c0 + h))
        return q, parts, ctx

    return nb, TQ, KP, ks4_of, col


def na_fwd(cfg, name, P, tab):
    nb, TQ, KP, ks4_of, col = _na_specs(cfg)
    NH = cfg.NH
    scale = cfg.NDH ** -0.5
    qs, _, _ = col("nq")
    _, kparts, kctx = col("nk")
    _, vparts, vctx = col("nv")

    def body(q_ref, k0, k1, k2, k3, kc_ref, v0, v1, v2, v3, vc_ref, tab_ref, o_ref, lse_ref, bias_ref):
        _na_fill_bias(cfg, tab_ref, bias_ref, pl.program_id(1))
        q = (q_ref[...] * scale).astype(BF16)
        kl = jnp.concatenate([k0[...], k1[...], k2[...], k3[...]], axis=0).astype(BF16)
        vl = jnp.concatenate([v0[...], v1[...], v2[...], v3[...]], axis=0).astype(BF16)
        kc = kc_ref[...].astype(BF16)
        vc = vc_ref[...].astype(BF16)
        sl = dot_nt(q, kl) + bias_ref[...]
        sc = dot_nt(q, kc)
        m = jnp.maximum(jnp.max(sl, axis=-1, keepdims=True), jnp.max(sc, axis=-1, keepdims=True))
        pl_ = jnp.exp(sl - m)
        pc = jnp.exp(sc - m)
        den = jnp.sum(pl_, axis=-1, keepdims=True) + jnp.sum(pc, axis=-1, keepdims=True)
        o = dot_nn(pl_.astype(BF16), vl) + dot_nn(pc.astype(BF16), vc)
        o_ref[...] = o / den
        lse_ref[...] = m + jnp.log(den)

    return _pc(body, name=name, grid=(NH, nb),
               in_specs=[qs, *kparts, kctx, *vparts, vctx,
                         pl.BlockSpec((None, _NTAB, cfg.GW, LANE), lambda h, b: (h, 0, 0, 0))],
               out_specs=[pl.BlockSpec((TQ, LANE), lambda h, b: (b, h)),
                          pl.BlockSpec((None, TQ, 1), lambda h, b: (h, b, 0))],
               out_shape=[jax.ShapeDtypeStruct((cfg.T, NH * LANE), F32),
                          jax.ShapeDtypeStruct((NH, cfg.T, 1), F32)],
               scratch_shapes=[pltpu.VMEM((TQ, 4 * KP), F32)],
               compiler_params=_cp(("parallel", "parallel")))(P, *([P] * 5), *([P] * 5), tab)


def na_bwd(cfg, name, P, tab, o, lse, dmix, dcol0):
    nb, TQ, KP, ks4_of, col = _na_specs(cfg)
    NH, GW = cfg.NH, cfg.GW
    L = P.shape[0]
    scale = cfg.NDH ** -0.5
    qs, _, _ = col("nq")
    _, kparts, kctx = col("nk")
    _, vparts, vctx = col("nv")

    def body(q_ref, k0, k1, k2, k3, kc_ref, v0, v1, v2, v3, vc_ref, tab_ref, o_ref, lse_ref, do_ref,
             dq_ref, dk_ref, dv_ref, dtab_ref, bias_ref):
        b = pl.program_id(1)

        @pl.when(b == 0)
        def _():
            dk_ref[...] = jnp.zeros_like(dk_ref)
            dv_ref[...] = jnp.zeros_like(dv_ref)
            dtab_ref[...] = jnp.zeros_like(dtab_ref)

        _na_fill_bias(cfg, tab_ref, bias_ref, b)

        q = (q_ref[...] * scale).astype(BF16)
        kl = jnp.concatenate([k0[...], k1[...], k2[...], k3[...]], axis=0).astype(BF16)
        vl = jnp.concatenate([v0[...], v1[...], v2[...], v3[...]], axis=0).astype(BF16)
        kc = kc_ref[...].astype(BF16)
        vc = vc_ref[...].astype(BF16)
        lse = lse_ref[...]
        do = do_ref[...]
        dob = do.astype(BF16)
        p_l = jnp.exp(dot_nt(q, kl) + bias_ref[...] - lse)
        p_c = jnp.exp(dot_nt(q, kc) - lse)
        delta = jnp.sum(do * o_ref[...], axis=-1, keepdims=True)
        ds_l = p_l * (dot_nt(dob, vl) - delta)
        ds_c = p_c * (dot_nt(dob, vc) - delta)
        dslb = ds_l.astype(BF16)
        dscb = ds_c.astype(BF16)
        dq_ref[...] = ((dot_nn(dslb, kl) + dot_nn(dscb, kc)) * scale).astype(dq_ref.dtype)
        r0 = pl.multiple_of(ks4_of(b) * KP, KP)
        dk_ref[pl.ds(r0, 4 * KP), :] += dot_tn(dslb, q)
        dv_ref[pl.ds(r0, 4 * KP), :] += dot_tn(p_l.astype(BF16), dob)
        dk_ref[cfg.T:cfg.T + KP, :] += dot_tn(dscb, q)
        dv_ref[cfg.T:cfg.T + KP, :] += dot_tn(p_c.astype(BF16), dob)
        bias_ref[...] = ds_l
        for i, J, row, _, _ in _na_tiles(cfg, b):
            dtab_ref[row] += bias_ref[i * GW:(i + 1) * GW, J * LANE:(J + 1) * LANE]

    full = pl.BlockSpec((L, LANE), lambda h, b: (0, h))
    tabs = pl.BlockSpec((None, _NTAB, GW, LANE), lambda h, b: (h, 0, 0, 0))
    return _pc(body, name=name, grid=(NH, nb),
               in_specs=[qs, *kparts, kctx, *vparts, vctx, tabs,
                         pl.BlockSpec((TQ, LANE), lambda h, b: (b, h)),
                         pl.BlockSpec((None, TQ, 1), lambda h, b: (h, b, 0)),
                         pl.BlockSpec((TQ, LANE), lambda h, b: (b, dcol0 + h))],
               out_specs=[pl.BlockSpec((TQ, LANE), lambda h, b: (b, h)), full, full, tabs],
               out_shape=[jax.ShapeDtypeStruct((cfg.T, NH * LANE), BF16),
                          jax.ShapeDtypeStruct((L, NH * LANE), F32), jax.ShapeDtypeStruct((L, NH * LANE), F32),
                          jax.ShapeDtypeStruct((NH, _NTAB, GW, LANE), F32)],
               scratch_shapes=[pltpu.VMEM((TQ, 4 * KP), F32)],
               compiler_params=_cp(("parallel", "arbitrary")))(
                   P, *([P] * 5), *([P] * 5), tab, o, lse, dmix)


def na_ctx_fwd(cfg, name, P):
    off = _offsets(cfg)
    TC, NH = cfg.TC, cfg.NH
    rb = cfg.T // TC
    scale = cfg.NDH ** -0.5

    def body(q_ref, k_ref, v_ref, o_ref, lse_ref):
        q = (q_ref[...] * scale).astype(BF16)
        s = dot_nt(q, k_ref[...].astype(BF16))
        m = jnp.max(s, axis=-1, keepdims=True)
        p = jnp.exp(s - m)
        den = jnp.sum(p, axis=-1, keepdims=True)
        o_ref[...] = dot_nn(p.astype(BF16), v_ref[...].astype(BF16)) / den
        lse_ref[...] = m + jnp.log(den)

    spec = lambda nm: pl.BlockSpec((TC, LANE), functools.partial(lambda h, c0: (rb, c0 + h), c0=off[nm] // LANE))
    return _pc(body, name=name, grid=(NH,), in_specs=[spec("nq"), spec("nk"), spec("nv")],
               out_specs=[pl.BlockSpec((TC, LANE), lambda h: (0, h)), pl.BlockSpec((None, TC, 1), lambda h: (h, 0, 0))],
               out_shape=[jax.ShapeDtypeStruct((TC, NH * LANE), F32), jax.ShapeDtypeStruct((NH, TC, 1), F32)],
               compiler_params=_cp(("parallel",)))(P, P, P)


def na_ctx_bwd(cfg, name, P, o, lse, dmix, dcol0, dk_in, dv_in):
    off = _offsets(cfg)
    TC, NH = cfg.TC, cfg.NH
    rb = cfg.T // TC
    scale = cfg.NDH ** -0.5

    def body(q_ref, k_ref, v_ref, o_ref, lse_ref, do_ref, dki_ref, dvi_ref, dq_ref, dk_ref, dv_ref):
        q = (q_ref[...] * scale).astype(BF16)
        kb = k_ref[...].astype(BF16)
        vb = v_ref[...].astype(BF16)
        do = do_ref[...]
        dob = do.astype(BF16)
        p = jnp.exp(dot_nt(q, kb) - lse_ref[...])
        delta = jnp.sum(do * o_ref[...], axis=-1, keepdims=True)
        ds = (p * (dot_nt(dob, vb) - delta)).astype(BF16)
        dq_ref[...] = (dot_nn(ds, kb) * scale).astype(dq_ref.dtype)
        dk_ref[...] = (dki_ref[...] + dot_tn(ds, q)).astype(dk_ref.dtype)
        dv_ref[...] = (dvi_ref[...] + dot_tn(p.astype(BF16), dob)).astype(dv_ref.dtype)

    spec = lambda nm: pl.BlockSpec((TC, LANE), functools.partial(lambda h, c0: (rb, c0 + h), c0=off[nm] // LANE))
    hb = pl.BlockSpec((TC, LANE), lambda h: (0, h))
    ctxrow = pl.BlockSpec((TC, LANE), lambda h: (rb, h))
    shp = jax.ShapeDtypeStruct((TC, NH * LANE), BF16)
    return _pc(body, name=name, grid=(NH,),
               in_specs=[spec("nq"), spec("nk"), spec("nv"), hb, pl.BlockSpec((None, TC, 1), lambda h: (h, 0, 0)),
                         pl.BlockSpec((TC, LANE), lambda h: (rb, dcol0 + h)), ctxrow, ctxrow],
               out_specs=[hb, hb, hb], out_shape=[shp, shp, shp],
               compiler_params=_cp(("parallel",)))(P, P, P, o, lse, dmix, dk_in, dv_in)


def na_rpb_grad(cfg, name, dtab):
    NH, GW = cfg.NH, cfg.GW
    na, nd = 2 * cfg.NAR - 1, 2 * cfg.NAC - 1
    oda, odb, _ = _na_col_onehot(cfg)
    E = np.concatenate([oda.T, odb.T], axis=1)
    rows = NH * _NTAB

    def body(z_ref, e_ref, o_ref):
        zv = z_ref[...]
        hi = zv.astype(BF16)
        lo = (zv - hi.astype(F32)).astype(BF16)
        e = e_ref[...]
        o_ref[...] = dot_nn(hi, e) + dot_nn(lo, e)

    g = _pc(body, name=name, out_shape=jax.ShapeDtypeStruct((rows, 2 * LANE), F32),
            compiler_params=_cp())(dtab.reshape(rows, GW * LANE), jnp.asarray(E, BF16))
    g = g.reshape(NH, _NTAB, 2, LANE)
    return g[:, 1:1 + na, 0, :nd] + g[:, 0:na, 1, :nd]


def _seq_tiles(cfg):
    T, TC, TB = cfg.T, cfg.TC, cfg.TB
    tiles = []
    for i in range((T + TC) // TB):
        r0 = i * TB
        tiles.append((r0, r0 == 0 or r0 == T, r0 + TB == T or r0 + TB == T + TC))
    return tiles


def _shift3(ref_get, r0, TB, start, end, width):
    cur = ref_get(r0, TB)
    if start or end:
        rowi = lax.broadcasted_iota(jnp.int32, (TB, width), 0)
    up = jnp.where(rowi == 0, 0.0, pltpu.roll(cur, 1, 0)) if start else ref_get(r0 - 1, TB)
    dn = jnp.where(rowi == TB - 1, 0.0, pltpu.roll(cur, TB - 1, 0)) if end else ref_get(r0 + 1, TB)
    return up, cur, dn


def ffn_act_fwd(cfg, name, U2, w, b):
    _, L, DFF = U2.shape
    TB = cfg.TB
    tiles = _seq_tiles(cfg)

    def body(u_ref, w_ref, b_ref, a_ref):
        def plane(p, r0, st, en):
            up, cur, dn = _shift3(lambda r, n: u_ref[p, r:r + n, :], r0, TB, st, en, LANE)
            wv = w_ref[p]
            return wv[0:1, :] * up + wv[1:2, :] * cur + wv[2:3, :] * dn + b_ref[p]

        for r0, st, en in tiles:
            val = plane(0, r0, st, en)
            gate = plane(1, r0, st, en)
            a_ref[r0:r0 + TB, :] = (_silu(gate) * val).astype(a_ref.dtype)

    return _pc(body, name=name, grid=(DFF // LANE,),
               in_specs=[pl.BlockSpec((2, L, LANE), lambda j: (0, 0, j)),
                         pl.BlockSpec((2, 8, LANE), lambda j: (0, 0, j)),
                         pl.BlockSpec((2, 1, LANE), lambda j: (0, 0, j))],
               out_specs=pl.BlockSpec((L, LANE), lambda j: (0, j)),
               out_shape=jax.ShapeDtypeStruct((L, DFF), BF16),
               compiler_params=_cp(("parallel",)))(U2, w, b)


def ffn_act_bwd(cfg, name, U2, w, b, dA):
    _, L, DFF = U2.shape
    TB = cfg.TB
    tiles = _seq_tiles(cfg)

    def body(u_ref, w_ref, b_ref, da_ref, du_ref, dw_ref, db_ref, dbuf):
        dw_ref[...] = jnp.zeros_like(dw_ref)
        db_ref[...] = jnp.zeros_like(db_ref)
        for r0, st, en in tiles:
            shifted = []
            pre = []
            for p in range(2):
                up, cur, dn = _shift3(lambda r, n: u_ref[p, r:r + n, :], r0, TB, st, en, LANE)
                wv = w_ref[p]
                shifted.append((up, cur, dn))
                pre.append(wv[0:1, :] * up + wv[1:2, :] * cur + wv[2:3, :] * dn + b_ref[p])
            val, gate = pre
            da = da_ref[r0:r0 + TB, :]
            dpre = (da * _silu(gate), da * val * _dsilu(gate))
            for p in range(2):
                dbuf[p, r0:r0 + TB, :] = dpre[p]
                for k in range(3):
                    dw_ref[p, k:k + 1, :] += jnp.sum(dpre[p] * shifted[p][k], axis=0, keepdims=True)
                db_ref[p] += jnp.sum(dpre[p], axis=0, keepdims=True)
        for r0, st, en in tiles:
            for p in range(2):
                up, cur, dn = _shift3(lambda r, n: dbuf[p, r:r + n, :], r0, TB, st, en, LANE)
                wv = w_ref[p]
                du_ref[p, r0:r0 + TB, :] = (wv[0:1, :] * dn + wv[1:2, :] * cur + wv[2:3, :] * up).astype(du_ref.dtype)

    blk = pl.BlockSpec((2, L, LANE), lambda j: (0, 0, j))
    wspec = pl.BlockSpec((2, 8, LANE), lambda j: (0, 0, j))
    bspec = pl.BlockSpec((2, 1, LANE), lambda j: (0, 0, j))
    return _pc(body, name=name, grid=(DFF // LANE,),
               in_specs=[blk, wspec, bspec, pl.BlockSpec((L, LANE), lambda j: (0, j))],
               out_specs=[blk, wspec, bspec],
               out_shape=[jax.ShapeDtypeStruct((2, L, DFF), BF16), jax.ShapeDtypeStruct((2, 8, DFF), F32),
                          jax.ShapeDtypeStruct((2, 1, DFF), F32)],
               scratch_shapes=[pltpu.VMEM((2, L, LANE), F32)],
               compiler_params=_cp(("parallel",)))(U2, w, b, dA)


def _tm(L, parts):
    assert L % parts == 0
    return L // parts


def layer_fwd(cfg, l, entry, mod, wts, small, tabs):
    fused = isinstance(entry, tuple)
    L, D = entry[0].shape if fused else entry.shape
    off = _offsets(cfg)
    DIN = off["end"]
    tmA = _tm(L, 4)
    sv = {"W": {}}

    def weight(name, after):
        sv["W"][name], tok = wts(name, after)
        return sv["W"][name], tok

    Win4, _ = weight("w_in", entry[1] if fused else entry)
    nbi = Win4.shape[3]
    if fused:
        XS, h1 = resid_norm_fwd(cfg, f"resid2_norm1_fwd_{l}", *entry, small["norm1_g"], mod["sc1"], mod["sh1"])
    else:
        XS, h1 = entry, norm_mod_fwd(cfg, f"norm1_fwd_{l}", entry, small["norm1_g"], mod["sc1"], mod["sh1"])
    sv["XS"] = XS
    P = matmul(f"mm_in_{l}", h1, Win4, contract="nn", grid=(4, L // tmA),
               a_spec=pl.BlockSpec((tmA, D), lambda n, m: (m, 0)),
               b_spec=pl.BlockSpec((None, None, D, nbi), lambda n, m: (n, l, 0, 0)),
               out_shape=jax.ShapeDtypeStruct((L, DIN), F32),
               out_spec=pl.BlockSpec((tmA, nbi), lambda n, m: (m, n)), nk=1)
    qr, kr = rope_fwd(cfg, f"rope_fwd_{l}", P, tabs["cos"], tabs["sin"])
    o_f, o_b, st = retention_fwd(cfg, f"ret_fwd_{l}", qr, kr, P, small["lam"])
    o2 = (o_f, o_b)
    ret = ggn_fwd(cfg, f"ggn_fwd_{l}", o2, P, small["ret_gn_g"])
    ycv = glu_dwconv_fwd(cfg, f"dwconv_fwd_{l}", P, small["conv_dw_w"], small["conv_dw_b"])
    act = ln_silu_fwd(cfg, f"ln_silu_fwd_{l}", ycv, small["conv_ln_g"], small["conv_ln_b"])
    Wpw4, _ = weight("conv_pw", act)
    cv = mm_rowsharded(f"mm_pw_{l}", act, Wpw4, l, BF16, cfg.CW)
    bias = na_tables(cfg, f"na_tables_{l}", small["na_rpb"])
    na_l, lse = na_fwd(cfg, f"na_fwd_{l}", P, bias)
    na_c, lse_c = na_ctx_fwd(cfg, f"na_ctx_fwd_{l}", P)
    mix = (ret, cv, jnp.concatenate([na_l, na_c], axis=0).astype(BF16))
    Wout4, tok = weight("w_out", ret)
    Y1 = mm_rowsharded(f"mm_out_{l}", mix, Wout4, l, F32, D)
    XM, h2 = resid_norm_fwd(cfg, f"resid1_norm2_fwd_{l}", XS, Y1, mod["g1"] if tok is None else mod["g1"] + tok,
                            small["norm2_g"], mod["sc2"], mod["sh2"])
    Wup4, _ = weight("ffn_up", h2)
    nbu = Wup4.shape[3]
    tnu = nbu // 2
    U2 = matmul(f"mm_up_{l}", h2, Wup4, contract="nn", grid=(8, L // tmA),
                a_spec=pl.BlockSpec((tmA, D), lambda n, m: (m, 0)),
                b_spec=pl.BlockSpec((None, None, D, tnu), lambda n, m: (n // 2, l, 0, n % 2)),
                out_shape=jax.ShapeDtypeStruct((2, L, cfg.DFF), F32),
                out_spec=pl.BlockSpec((None, tmA, tnu), lambda n, m: (n // 4, m, n % 4)), nk=1)
    A = ffn_act_fwd(cfg, f"ffn_act_fwd_{l}", U2, small["ffn_dw_w"], small["ffn_dw_b"])
    Wdn4, _ = weight("ffn_down", A)
    Y2 = mm_rowsharded(f"mm_down_{l}", A, Wdn4, l, F32, D // 2)
    sv.update(h1=h1, P=P, qr=qr, kr=kr, o2=o2, st=st, ycv=ycv, act=act, bias=bias, na_l=na_l, lse=lse,
              na_c=na_c, lse_c=lse_c, mix=mix, Y1=Y1, XM=XM, h2=h2, U2=U2, A=A, Y2=Y2)
    return (XM, Y2, mod["g2"]), sv


GRAD_GROUPS = (("ffn_down", "ffn_up"), ("w_out", "conv_pw", "w_in"))


def layer_bwd(cfg, l, dXO, sv, mod, wts, small, tabs, gbuf, ready):
    L, D = dXO.shape
    off = _offsets(cfg)
    DIN = off["end"]
    Win4, Wout4, Wup4, Wdn4, Wpw4 = wts["w_in"], wts["w_out"], wts["ffn_up"], wts["ffn_down"], wts["conv_pw"]
    tmA, tmB = _tm(L, 4), _tm(L, 8)
    depth = Win4.shape[1]
    gb, gs, dm = {}, {}, {}
    P = sv["P"]
    dY2, dm["g2"] = resid_bwd(cfg, f"resid2_bwd_{l}", dXO, sv["Y2"], mod["g2"])
    nbd = Wdn4.shape[2]
    dA = matmul(f"mm_down_da_{l}", dY2, Wdn4, contract="nt", grid=(4, L // tmA),
                a_spec=pl.BlockSpec((tmA, D), lambda j, m: (m, 0)),
                b_spec=pl.BlockSpec((None, None, nbd, D), lambda j, m: (j, l, 0, 0)),
                out_shape=jax.ShapeDtypeStruct((L, cfg.DFF), F32),
                out_spec=pl.BlockSpec((tmA, nbd), lambda j, m: (m, j)), nk=1)
    gb["ffn_down"] = wgrad(cfg, f"mm_down_dw_{l}", sv["A"], dY2,
                           lambda rb, ri: pl.BlockSpec((rb, nbd), lambda j, m: (ri(m), j)),
                           lambda rb, ri: pl.BlockSpec((rb, D), lambda j, m: (ri(m), 0)),
                           jax.ShapeDtypeStruct((depth, 4, nbd, D), BF16),
                           pl.BlockSpec((None, None, nbd, D), lambda j, m: (l, j, 0, 0)), 4, gbuf.get("ffn_down"))
    dU2, dfw, dfb = ffn_act_bwd(cfg, f"ffn_act_bwd_{l}", sv["U2"], small["ffn_dw_w"], small["ffn_dw_b"], dA)
    gs["ffn_dw_w"], gs["ffn_dw_b"] = dfw, dfb
    nbu = Wup4.shape[3]
    tnu = nbu // 2
    dH2 = matmul(f"mm_up_dh_{l}", dU2, Wup4, contract="nt", grid=(L // tmA, 8),
                 a_spec=pl.BlockSpec((None, tmA, tnu), lambda m, n: (n // 4, m, n % 4)),
                 b_spec=pl.BlockSpec((None, None, D, tnu), lambda m, n: (n // 2, l, 0, n % 2)),
                 out_shape=jax.ShapeDtypeStruct((L, D), F32),
                 out_spec=pl.BlockSpec((tmA, D), lambda m, n: (m, 0)), nk=8)
    gb["ffn_up"] = wgrad(cfg, f"mm_up_dw_{l}", sv["h2"], dU2,
                         lambda rb, ri: pl.BlockSpec((rb, D), lambda n, m: (ri(m), 0)),
                         lambda rb, ri: pl.BlockSpec((None, rb, tnu), lambda n, m: (n // 4, ri(m), n % 4)),
                         jax.ShapeDtypeStruct((depth, 4, D, nbu), BF16),
                         pl.BlockSpec((None, None, D, tnu), lambda n, m: (l, n // 2, 0, n % 2)), 8, gbuf.get("ffn_up"))
    dXM, dm["sc2"], dm["sh2"], gs["norm2_g"] = norm_mod_bwd(
        cfg, f"norm2_bwd_{l}", dH2, sv["XM"], small["norm2_g"], mod["sc2"], dXO)
    tok = ready(GRAD_GROUPS[0], gb)
    dY1, dm["g1"] = resid_bwd(cfg, f"resid1_bwd_{l}", dXM, sv["Y1"], mod["g1"] if tok is None else mod["g1"] + tok)
    nbo = Wout4.shape[2]
    dmix = matmul(f"mm_out_dmix_{l}", dY1, Wout4, contract="nt", grid=(4, L // tmA),
                  a_spec=pl.BlockSpec((tmA, D), lambda j, m: (m, 0)),
                  b_spec=pl.BlockSpec((None, None, nbo, D), lambda j, m: (j, l, 0, 0)),
                  out_shape=jax.ShapeDtypeStruct((L, D), F32),
                  out_spec=pl.BlockSpec((tmA, nbo), lambda j, m: (m, j)), nk=1)
    gw, j0 = gbuf.get("w_out"), 0
    for pi, piece in enumerate(sv["mix"]):
        nblk = piece.shape[1] // nbo
        gw = wgrad(cfg, f"mm_out_dw_{l}_{pi}", piece, dY1,
                   lambda rb, ri: pl.BlockSpec((rb, nbo), lambda j, m: (ri(m), j)),
                   lambda rb, ri: pl.BlockSpec((rb, D), lambda j, m: (ri(m), 0)),
                   jax.ShapeDtypeStruct((depth, 4, nbo, D), BF16),
                   pl.BlockSpec((None, None, nbo, D), functools.partial(lambda j, m, j0: (l, j + j0, 0, 0), j0=j0)),
                   nblk, gw)
        j0 += nblk
    gb["w_out"] = gw
    RW = cfg.RH * cfg.RDV
    do, dlg, gs["ret_gn_g"] = ggn_bwd(cfg, f"ggn_bwd_{l}", dmix, sv["o2"], P, small["ret_gn_g"], 0)
    dqf, dqb, dkf, dkb, dvf, dvb, dlam = retention_bwd(
        cfg, f"ret_bwd_{l}", sv["qr"], sv["kr"], P, small["lam"], sv["st"], do)
    gs["lam"] = dlam[:, :, 0, 0]
    dlq, dlk = rope_bwd(cfg, f"rope_bwd_{l}", (dqf, dqb), (dkf, dkb), tabs["cos"], tabs["sin"])
    dlv = add_cast(cfg, f"ret_dv_{l}", dvf, dvb)
    dcv = cast_cols(cfg, f"conv_dcv_{l}", dmix, RW // LANE, cfg.CW // LANE, LANE)
    nbp = Wpw4.shape[2]
    dact = matmul(f"mm_pw_dact_{l}", dcv, Wpw4, contract="nt", grid=(4, L // tmA),
                  a_spec=pl.BlockSpec((tmA, cfg.CW), lambda j, m: (m, 0)),
                  b_spec=pl.BlockSpec((None, None, nbp, cfg.CW), lambda j, m: (j, l, 0, 0)),
                  out_shape=jax.ShapeDtypeStruct((L, cfg.CW), F32),
                  out_spec=pl.BlockSpec((tmA, nbp), lambda j, m: (m, j)), nk=1)
    gb["conv_pw"] = wgrad(cfg, f"mm_pw_dw_{l}", sv["act"], dcv,
                          lambda rb, ri: pl.BlockSpec((rb, nbp), lambda j, m: (ri(m), j)),
                          lambda rb, ri: pl.BlockSpec((rb, cfg.CW), lambda j, m: (ri(m), 0)),
                          jax.ShapeDtypeStruct((depth, 4, nbp, cfg.CW), BF16),
                          pl.BlockSpec((None, None, nbp, cfg.CW), lambda j, m: (l, j, 0, 0)), 4, gbuf.get("conv_pw"))
    dycv, gs["conv_ln_g"], gs["conv_ln_b"] = ln_silu_bwd(
        cfg, f"ln_silu_bwd_{l}", dact, sv["ycv"], small["conv_ln_g"], small["conv_ln_b"])
    dla, dlb, gs["conv_dw_w"], gs["conv_dw_b"] = glu_dwconv_bwd(cfg, f"dwconv_bwd_{l}", P, small["conv_dw_w"], dycv)
    nac0 = (RW + cfg.CW) // LANE
    dnq_l, dnk, dnv, dsb = na_bwd(cfg, f"na_bwd_{l}", P, sv["bias"], sv["na_l"], sv["lse"], dmix, nac0)
    dnq_c, dnk_c, dnv_c = na_ctx_bwd(cfg, f"na_ctx_bwd_{l}", P, sv["na_c"], sv["lse_c"], dmix, nac0, dnk, dnv)
    gs["na_rpb"] = na_rpb_grad(cfg, f"na_rpb_{l}", dsb)
    dnq = jnp.concatenate([dnq_l, dnq_c], axis=0)
    dnk = jnp.concatenate([dnk[:cfg.T].astype(BF16), dnk_c], axis=0)
    dnv = jnp.concatenate([dnv[:cfg.T].astype(BF16), dnv_c], axis=0)
    dP = jnp.concatenate([dlq, dlk, dlv, dlg, dla, dlb, dnq, dnk, dnv], axis=1)
    nbi = Win4.shape[3]
    dH1 = matmul(f"mm_in_dh_{l}", dP, Win4, contract="nt", grid=(L // tmA, 4),
                 a_spec=pl.BlockSpec((tmA, nbi), lambda m, n: (m, n)),
                 b_spec=pl.BlockSpec((None, None, D, nbi), lambda m, n: (n, l, 0, 0)),
                 out_shape=jax.ShapeDtypeStruct((L, D), F32),
                 out_spec=pl.BlockSpec((tmA, D), lambda m, n: (m, 0)), nk=4)
    gb["w_in"] = wgrad(cfg, f"mm_in_dw_{l}", sv["h1"], dP,
                       lambda rb, ri: pl.BlockSpec((rb, D), lambda n, m: (ri(m), 0)),
                       lambda rb, ri: pl.BlockSpec((rb, nbi), lambda n, m: (ri(m), n)),
                       jax.ShapeDtypeStruct((depth, 4, D, nbi), BF16),
                       pl.BlockSpec((None, None, D, nbi), lambda n, m: (l, n, 0, 0)), 4, gbuf.get("w_in"))
    dXS, dm["sc1"], dm["sh1"], gs["norm1_g"] = norm_mod_bwd(
        cfg, f"norm1_bwd_{l}", dH1, sv["XS"], small["norm1_g"], mod["sc1"], dXM, latent_only=(l == 0))
    return dXS, gb, gs, dm, ready(GRAD_GROUPS[1], gb)


def _layer_small(cfg, l, sp):
    DFF = cfg.DFF
    fw = sp["ffn_dw_w"][l].reshape(3, 2, DFF).transpose(1, 0, 2)
    fw = jnp.concatenate([fw, jnp.zeros((2, 5, DFF), F32)], axis=1)
    cw = jnp.concatenate([sp["conv_dw_w"][l], jnp.zeros((32 - cfg.CK, cfg.CW), F32)], axis=0)
    return dict(
        norm1_g=sp["norm1_g"][l][None], norm2_g=sp["norm2_g"][l][None],
        lam=jax.nn.log_sigmoid(sp["ret_decay"][l]), ret_gn_g=sp["ret_gn_g"][l][None],
        conv_dw_w=cw, conv_dw_b=sp["conv_dw_b"][l][None], conv_ln_g=sp["conv_ln_g"][l][None],
        conv_ln_b=sp["conv_ln_b"][l][None], na_rpb=sp["na_rpb"][l],
        ffn_dw_w=fw, ffn_dw_b=sp["ffn_dw_b"][l].reshape(2, 1, DFF))


def local_step(cfg, x, ctx, tgt, mods, wts, sp, grads_ready=lambda l, names, gb: None):
    depth = sp["norm1_g"].shape[0]
    cos, sin = rope_tables(cfg)
    tabs = dict(cos=cos, sin=sin)
    XS = jnp.concatenate([x, ctx], axis=0)
    smalls = [_layer_small(cfg, l, sp) for l in range(depth)]
    saves = []
    for l in range(depth):
        XS, sv = layer_fwd(cfg, l, XS, mods[l], functools.partial(wts, l), smalls[l], tabs)
        saves.append(sv)
    xm, y2, g2 = XS
    ls, dX, dfg = final_loss(cfg, "final_loss", xm, y2, g2[0], sp["final_g"][None], tgt)
    gb, gss, dms = {}, [None] * depth, [None] * depth
    token = None
    for l in reversed(range(depth)):
        mod = mods[l] if token is None else {**mods[l], "g2": mods[l]["g2"] + token}
        dX, gb, gss[l], dms[l], token = layer_bwd(cfg, l, dX, saves[l], mod, saves[l]["W"], smalls[l], tabs, gb,
                                                  functools.partial(grads_ready, l))
    return ls[0, 0], dX[:cfg.T], gb, gss, dms, dfg[0], token


MESH = pl.DeviceIdType.MESH
N_DEV = 8
N_CHIP = 4
BIG = ("w_in", "w_out", "ffn_up", "ffn_down", "conv_pw")
_ANY = pl.BlockSpec(memory_space=pl.ANY)


def _place():
    x, y, c = lax.axis_index("x"), lax.axis_index("y"), lax.axis_index("c")
    chips = [(1 - x, y), (x, 1 - y), (1 - x, 1 - y)]
    return x, y, c, chips


def allgather8(name, xs):
    m_per, n = xs.shape

    def body(x_ref, out_ref, send_sems, recv_sems, local_sem):
        x, y, c, chips = _place()
        me, sibling = (x, y, c), (x, y, 1 - c)

        def rows(px, py, pc):
            return out_ref.at[pl.ds((4 * px + 2 * py + pc) * m_per, m_per), :]

        def copy(k, block, to, src=None):
            return pltpu.make_async_remote_copy(
                src_ref=rows(*block) if src is None else src, dst_ref=rows(*block),
                send_sem=send_sems.at[k], recv_sem=recv_sems.at[k], device_id=to, device_id_type=MESH)

        mine = pltpu.make_async_copy(x_ref, rows(*me), local_sem)
        mine.start()
        first = [copy(0, me, sibling, src=x_ref)]
        first += [copy(1 + j, me, (*chip, c), src=x_ref) for j, chip in enumerate(chips)]
        for cp in first:
            cp.start()
        passed = [copy(4 + j, (*chip, c), sibling) for j, chip in enumerate(chips)]
        for j, chip in enumerate(chips):
            copy(1 + j, (*chip, c), me).wait_recv()
            passed[j].start()
        copy(0, sibling, me).wait_recv()
        for j, chip in enumerate(chips):
            copy(4 + j, (*chip, 1 - c), me).wait_recv()
        for cp in first + passed:
            cp.wait_send()
        mine.wait()

    return _pc(body, name=name, out_shape=jax.ShapeDtypeStruct((N_DEV * m_per, n), xs.dtype),
               in_specs=[pl.BlockSpec(memory_space=pltpu.VMEM)], out_specs=pl.BlockSpec(memory_space=pltpu.VMEM),
               scratch_shapes=[pltpu.SemaphoreType.DMA((7,)), pltpu.SemaphoreType.DMA((7,)), pltpu.SemaphoreType.DMA],
               compiler_params=pltpu.CompilerParams(vmem_limit_bytes=VMEM_LIMIT))(xs)


def _wpiece(ref, layer, chip_idx, half):
    rh = ref.shape[2] // 2
    return ref.at[chip_idx, layer, pl.ds(half * rh, rh)]


def _wcopy(ref, layer, chip_idx, half, send_sems, recv_sems, k, to):
    piece = _wpiece(ref, layer, chip_idx, half)
    return pltpu.make_async_remote_copy(src_ref=piece, dst_ref=piece, send_sem=send_sems.at[k],
                                        recv_sem=recv_sems.at[k], device_id=to, device_id_type=MESH)


def _w_ici_sends(outs, layer, send_sems, recv_sems):
    x, y, c, chips = _place()
    return [_wcopy(outs[a], layer, 2 * x + y, c, send_sems, recv_sems, 3 * a + t, (*chip, c))
            for a in range(len(outs)) for t, chip in enumerate(chips)]


def _w_ici_landed(outs, layer, send_sems, recv_sems):
    x, y, c, chips = _place()
    return [_wcopy(outs[a], layer, 2 * chip[0] + chip[1], c, send_sems, recv_sems, 3 * a + t, (x, y, c))
            for a in range(len(outs)) for t, chip in enumerate(chips)]


def _w_forward(outs, layer, send_sems, recv_sems, base):
    x, y, c, chips = _place()
    n = len(outs)
    sends = [_wcopy(outs[a], layer, 2 * chip[0] + chip[1], c, send_sems, recv_sems, base + 3 * a + t, (x, y, 1 - c))
             for a in range(n) for t, chip in enumerate(chips)]
    recvs = [_wcopy(outs[a], layer, 2 * chip[0] + chip[1], 1 - c, send_sems, recv_sems, base + 3 * a + t, (x, y, c))
             for a in range(n) for t, chip in enumerate(chips)]
    return sends, recvs


def allgather_layer(name, bufs, layer):
    n = len(bufs)

    def body(*refs):
        outs = refs[n:2 * n]
        send_sems, recv_sems = refs[2 * n:]
        sent = _w_ici_sends(outs, layer, send_sems, recv_sems)
        for cp in sent:
            cp.start()
        fwd, from_sib = _w_forward(outs, layer, send_sems, recv_sems, 3 * n)
        for landed, fw in zip(_w_ici_landed(outs, layer, send_sems, recv_sems), fwd):
            landed.wait_recv()
            fw.start()
        for cp in from_sib:
            cp.wait_recv()
        for cp in sent + fwd:
            cp.wait_send()

    return _pc(body, name=name, out_shape=[jax.ShapeDtypeStruct(b.shape, b.dtype) for b in bufs],
               in_specs=[_ANY] * n, out_specs=[_ANY] * n, input_output_aliases={a: a for a in range(n)},
               scratch_shapes=[pltpu.SemaphoreType.DMA((6 * n,)), pltpu.SemaphoreType.DMA((6 * n,))])(*bufs)


_HBM = pl.BlockSpec(memory_space=pltpu.HBM)
_SEM = pl.BlockSpec(memory_space=pltpu.SEMAPHORE)
_EFFECT = pltpu.SideEffectType.DATAFLOW_SIDE_EFFECTING


def allgather_layer_start(name, bufs, layer, after):
    n = len(bufs)

    def body(*refs):
        send_sems, recv_sems = refs[n + 1:n + 3]
        outs = refs[n + 3:2 * n + 3]
        token = refs[2 * n + 3]
        for cp in _w_ici_sends(outs, layer, send_sems, recv_sems):
            cp.start()
        token[...] = jnp.zeros_like(token)

    res = _pc(body, name=name,
              out_shape=(pltpu.SemaphoreType.DMA((3 * n,)), pltpu.SemaphoreType.DMA((3 * n,)),
                         *[pltpu.HBM(b.shape, b.dtype) for b in bufs], jax.ShapeDtypeStruct((8, LANE), F32)),
              in_specs=[_HBM] * n + [_ANY],
              out_specs=(_SEM, _SEM, *([_HBM] * n), pl.BlockSpec(memory_space=pltpu.VMEM)),
              input_output_aliases={a: a + 2 for a in range(n)},
              compiler_params=pltpu.CompilerParams(has_side_effects=_EFFECT))(
                  *[pltpu.with_memory_space_constraint(b, pltpu.HBM) for b in bufs], after)
    return res[0], res[1], list(res[2:2 + n]), res[2 + n]


def allgather_layer_wait(name, bufs, send_sems, recv_sems, after, layer):
    n = len(bufs)

    def body(*refs):
        ins = refs[:n]
        send_sems, recv_sems = refs[n:n + 2]
        for cp in _w_ici_sends(ins, layer, send_sems, recv_sems):
            cp.wait_send()
        for cp in _w_ici_landed(ins, layer, send_sems, recv_sems):
            cp.wait_recv()

    return _pc(body, name=name, out_shape=tuple(pltpu.HBM(b.shape, b.dtype) for b in bufs),
               in_specs=[_HBM] * n + [_SEM, _SEM, _ANY], out_specs=tuple([_HBM] * n),
               input_output_aliases={a: a for a in range(n)},
               compiler_params=pltpu.CompilerParams(has_side_effects=_EFFECT))(*bufs, send_sems, recv_sems, after)


def forward_halves(name, bufs, layer):
    n = len(bufs)

    def body(*refs):
        outs = refs[n:2 * n]
        send_sems, recv_sems = refs[2 * n:]
        fwd, from_sib = _w_forward(outs, layer, send_sems, recv_sems, 0)
        for cp in fwd:
            cp.start()
        for cp in from_sib:
            cp.wait_recv()
        for cp in fwd:
            cp.wait_send()

    return _pc(body, name=name, out_shape=[jax.ShapeDtypeStruct(b.shape, b.dtype) for b in bufs],
               in_specs=[_ANY] * n, out_specs=[_ANY] * n, input_output_aliases={a: a for a in range(n)},
               scratch_shapes=[pltpu.SemaphoreType.DMA((3 * n,)), pltpu.SemaphoreType.DMA((3 * n,))])(*bufs)


def exchange_rows(name, grads, layer):
    n = len(grads)

    def body(*refs):
        ins, outs = refs[:n], refs[n:2 * n]
        send_sems, recv_sems = refs[2 * n:]
        x, y, c, _ = _place()
        cps = []
        for a in range(n):
            rh = ins[a].shape[2] // 2
            cps.append(pltpu.make_async_remote_copy(
                src_ref=ins[a].at[layer, pl.ds(0, N_CHIP), pl.ds((1 - c) * rh, rh)], dst_ref=outs[a],
                send_sem=send_sems.at[a], recv_sem=recv_sems.at[a], device_id=(x, y, 1 - c), device_id_type=MESH))
        for cp in cps:
            cp.start()
        for cp in cps:
            cp.wait()

    return _pc(body, name=name,
               out_shape=[jax.ShapeDtypeStruct((N_CHIP, g.shape[2] // 2, g.shape[3]), g.dtype) for g in grads],
               in_specs=[_ANY] * n, out_specs=[_ANY] * n,
               scratch_shapes=[pltpu.SemaphoreType.DMA((n,)), pltpu.SemaphoreType.DMA((n,))])(*grads)


def _scatter_sends(parts, lands, send_sems, recv_sems):
    x, y, c, chips = _place()
    return [pltpu.make_async_remote_copy(
        src_ref=parts[a].at[2 * chip[0] + chip[1]], dst_ref=lands[a].at[2 * x + y], send_sem=send_sems.at[3 * a + t],
        recv_sem=recv_sems.at[3 * a + t], device_id=(*chip, c), device_id_type=MESH)
        for a in range(len(parts)) for t, chip in enumerate(chips)]


def _scatter_landed(lands, send_sems, recv_sems):
    x, y, c, chips = _place()
    return [pltpu.make_async_remote_copy(
        src_ref=lands[a].at[2 * chip[0] + chip[1]], dst_ref=lands[a].at[2 * chip[0] + chip[1]],
        send_sem=send_sems.at[3 * a + t], recv_sem=recv_sems.at[3 * a + t], device_id=(x, y, c), device_id_type=MESH)
        for a in range(len(lands)) for t, chip in enumerate(chips)]


def scatter_slices(name, parts, lands):
    n = len(parts)

    def body(*refs):
        ins, outs = refs[:n], refs[2 * n:3 * n]
        send_sems, recv_sems = refs[3 * n:]
        cps = _scatter_sends(ins, outs, send_sems, recv_sems)
        for cp in cps:
            cp.start()
        for cp in _scatter_landed(outs, send_sems, recv_sems):
            cp.wait_recv()
        for cp in cps:
            cp.wait_send()

    return _pc(body, name=name, out_shape=[jax.ShapeDtypeStruct(p.shape, p.dtype) for p in lands],
               in_specs=[_ANY] * (2 * n), out_specs=[_ANY] * n,
               input_output_aliases={n + a: a for a in range(n)},
               scratch_shapes=[pltpu.SemaphoreType.DMA((3 * n,)), pltpu.SemaphoreType.DMA((3 * n,))])(*parts, *lands)


def scatter_slices_start(name, parts, lands):
    n = len(parts)

    def body(*refs):
        send_sems, recv_sems = refs[2 * n:2 * n + 2]
        p_out, l_out = refs[2 * n + 2:3 * n + 2], refs[3 * n + 2:4 * n + 2]
        token = refs[4 * n + 2]
        for cp in _scatter_sends(p_out, l_out, send_sems, recv_sems):
            cp.start()
        token[...] = jnp.zeros_like(token)

    both = list(parts) + list(lands)
    res = _pc(body, name=name,
              out_shape=(pltpu.SemaphoreType.DMA((3 * n,)), pltpu.SemaphoreType.DMA((3 * n,)),
                         *[pltpu.HBM(b.shape, b.dtype) for b in both], jax.ShapeDtypeStruct((8, LANE), F32)),
              in_specs=[_HBM] * (2 * n),
              out_specs=(_SEM, _SEM, *([_HBM] * (2 * n)), pl.BlockSpec(memory_space=pltpu.VMEM)),
              input_output_aliases={a: a + 2 for a in range(2 * n)},
              compiler_params=pltpu.CompilerParams(has_side_effects=_EFFECT))(
                  *[pltpu.with_memory_space_constraint(b, pltpu.HBM) for b in both])
    return res[0], res[1], list(res[2:2 + n]), list(res[2 + n:2 + 2 * n]), res[2 + 2 * n]


def scatter_slices_wait(name, parts, lands, send_sems, recv_sems, after):
    n = len(parts)

    def body(*refs):
        p_in, l_in = refs[:n], refs[n:2 * n]
        send_sems, recv_sems = refs[2 * n:2 * n + 2]
        for cp in _scatter_sends(p_in, l_in, send_sems, recv_sems):
            cp.wait_send()
        for cp in _scatter_landed(l_in, send_sems, recv_sems):
            cp.wait_recv()

    both = list(parts) + list(lands)
    res = _pc(body, name=name, out_shape=tuple(pltpu.HBM(b.shape, b.dtype) for b in both),
              in_specs=[_HBM] * (2 * n) + [_SEM, _SEM, _ANY], out_specs=tuple([_HBM] * (2 * n)),
              input_output_aliases={a: a for a in range(2 * n)},
              compiler_params=pltpu.CompilerParams(has_side_effects=_EFFECT))(*both, send_sems, recv_sems, after)
    return list(res[n:])


def share_rows(name, bufs):
    n = len(bufs)

    def body(*refs):
        outs = refs[n:2 * n]
        send_sems, recv_sems = refs[2 * n:]
        x, y, c, _ = _place()

        def half(a, h):
            return outs[a].at[pl.ds(0, 2), h]

        cps = [pltpu.make_async_remote_copy(
            src_ref=half(a, c), dst_ref=half(a, c), send_sem=send_sems.at[a], recv_sem=recv_sems.at[a],
            device_id=(x, y, 1 - c), device_id_type=MESH) for a in range(n)]
        for cp in cps:
            cp.start()
        for a in range(n):
            pltpu.make_async_remote_copy(
                src_ref=half(a, 1 - c), dst_ref=half(a, 1 - c), send_sem=send_sems.at[a],
                recv_sem=recv_sems.at[a], device_id=(x, y, c), device_id_type=MESH).wait_recv()
        for cp in cps:
            cp.wait_send()

    return _pc(body, name=name, out_shape=[jax.ShapeDtypeStruct(b.shape, b.dtype) for b in bufs],
               in_specs=[_ANY] * n, out_specs=[_ANY] * n, input_output_aliases={a: a for a in range(n)},
               scratch_shapes=[pltpu.SemaphoreType.DMA((n,)), pltpu.SemaphoreType.DMA((n,))])(*bufs)


def _row_tile(R, C, nbytes=1 << 20):
    t = 8
    while t * 2 <= R and R % (t * 2) == 0 and t * 2 * C * 4 <= nbytes:
        t *= 2
    assert R % t == 0
    return t


def to_bf16_block(name, w, chip_arr, layer, after=None):
    _, R, C = w.shape
    tr = _row_tile(R, C)

    def body(j_ref, w_ref, *rest):
        o_ref = rest[-1]
        o_ref[...] = w_ref[...].astype(o_ref.dtype)

    in_specs, args = [pl.BlockSpec((None, tr, C), lambda i, j_ref: (layer, i, 0))], (chip_arr, w)
    if after is not None:
        in_specs, args = in_specs + [_ANY], args + (after,)
    gs = pltpu.PrefetchScalarGridSpec(
        num_scalar_prefetch=1, grid=(R // tr,), in_specs=in_specs,
        out_specs=pl.BlockSpec((None, None, tr, C), lambda i, j_ref: (j_ref[0], layer, i, 0)))
    return _pc(body, name=name, grid_spec=gs, out_shape=jax.ShapeDtypeStruct((N_CHIP,) + w.shape, BF16),
               compiler_params=_cp(("parallel",)))(*args)


def add_rows(name, g, ra, c_arr, chip_arr, layer):
    _, _, R, C = g.shape
    rh = R // 2
    tr = _row_tile(rh, C, nbytes=1 << 22)
    nb = rh // tr

    def body(c_ref, j_ref, g_ref, r_ref, o_ref, own_ref):
        s = (g_ref[...].astype(F32) + r_ref[...].astype(F32)).astype(o_ref.dtype)
        o_ref[...] = s

        @pl.when(pl.program_id(1) == j_ref[0])
        def _():
            own_ref[...] = s

    gs = pltpu.PrefetchScalarGridSpec(
        num_scalar_prefetch=2, grid=(nb, N_CHIP),
        in_specs=[pl.BlockSpec((None, None, tr, C), lambda i, j, c_ref, j_ref: (layer, j, c_ref[0] * nb + i, 0)),
                  pl.BlockSpec((None, tr, C), lambda i, j, c_ref, j_ref: (j, i, 0))],
        out_specs=[pl.BlockSpec((None, tr, C), lambda i, j, c_ref, j_ref: (j, i, 0)),
                   pl.BlockSpec((None, tr, C), lambda i, j, c_ref, j_ref: (j_ref[0], i, 0))])
    shp = jax.ShapeDtypeStruct(ra.shape, BF16)
    return _pc(body, name=name, grid_spec=gs, out_shape=[shp, shp],
               compiler_params=_cp(("arbitrary", "arbitrary")))(c_arr, chip_arr, g, ra)


def sum_rows_into(name, landed, c_arr, layer, into):
    n, rh, C = landed.shape
    tr = _row_tile(rh, C, nbytes=1 << 19)

    def body(*refs):
        g_ref, o_ref = refs[1], refs[-1]
        acc = g_ref[0].astype(F32)
        for j in range(1, n):
            acc = acc + g_ref[j].astype(F32)
        o_ref[...] = acc

    in_specs, args, alias = [pl.BlockSpec((n, tr, C), lambda i, c_ref: (0, i, 0))], (c_arr, landed), {}
    if into is not None:
        in_specs, args, alias = in_specs + [_ANY], args + (into,), {2: 0}
    gs = pltpu.PrefetchScalarGridSpec(
        num_scalar_prefetch=1, grid=(rh // tr,), in_specs=in_specs,
        out_specs=pl.BlockSpec((None, None, tr, C), lambda i, c_ref: (layer, c_ref[0], i, 0)))
    return _pc(body, name=name, grid_spec=gs, out_shape=jax.ShapeDtypeStruct((2, 2, rh, C), F32),
               input_output_aliases=alias, compiler_params=_cp(("parallel",)))(*args)


def sum_leading(name, g, plane=None):
    n, R, C = g.shape
    tr = _row_tile(R, C, nbytes=(1 << 21) // n)

    def body(*refs):
        g_ref, o_ref = refs[-2:]
        acc = g_ref[0].astype(F32)
        for j in range(1, n):
            acc = acc + g_ref[j].astype(F32)
        o_ref[...] = acc

    if plane is None:
        return _pc(body, name=name, grid=(R // tr,), in_specs=[pl.BlockSpec((n, tr, C), lambda i: (0, i, 0))],
                   out_specs=pl.BlockSpec((tr, C), lambda i: (i, 0)), out_shape=jax.ShapeDtypeStruct((R, C), F32),
                   compiler_params=_cp(("parallel",)))(g)
    count, idx = plane
    gs = pltpu.PrefetchScalarGridSpec(
        num_scalar_prefetch=1, grid=(R // tr,),
        in_specs=[pl.BlockSpec((n, tr, C), lambda i, p_ref: (0, i, 0))],
        out_specs=pl.BlockSpec((None, tr, C), lambda i, p_ref: (p_ref[0], i, 0)))
    return _pc(body, name=name, grid_spec=gs, out_shape=jax.ShapeDtypeStruct((count, R, C), F32),
               compiler_params=_cp(("parallel",)))(idx, g)


def adamw(name, w, g, m, v, emit_g=False):
    R, C = w.shape
    tr = _row_tile(R, C)

    def body(w_ref, g_ref, m_ref, v_ref, d_ref, mo_ref, vo_ref, *go_ref):
        gv = g_ref[...]
        if emit_g:
            go_ref[0][...] = gv
        mn = ADAM_B1 * m_ref[...] + (1.0 - ADAM_B1) * gv
        vn = ADAM_B2 * v_ref[...] + (1.0 - ADAM_B2) * (gv * gv)
        m_hat = mn / (1.0 - ADAM_B1 ** ADAM_STEP)
        v_hat = vn / (1.0 - ADAM_B2 ** ADAM_STEP)
        d_ref[...] = -ADAM_LR * (m_hat / (jnp.sqrt(v_hat) + ADAM_EPS) + ADAM_WD * w_ref[...])
        mo_ref[...] = mn
        vo_ref[...] = vn

    spec = pl.BlockSpec((tr, C), lambda i: (i, 0))
    shp = jax.ShapeDtypeStruct((R, C), F32)
    nout = 4 if emit_g else 3
    return _pc(body, name=name, grid=(R // tr,), in_specs=[spec] * 4, out_specs=[spec] * nout,
               out_shape=[shp] * nout, compiler_params=_cp(("parallel",)))(w, g, m, v)


_ADA_TN = 512


def adaln_fwd(name, cond, w, b):
    _, D, N = w.shape
    tn = min(_ADA_TN, N)

    def body(c_ref, w_ref, b_ref, o_ref):
        s = _silu(c_ref[...]).astype(BF16)
        o_ref[...] = dot_nn(s, w_ref[...].astype(BF16)) + b_ref[...]

    return _pc(body, name=name, grid=(2, N // tn),
               in_specs=[pl.BlockSpec((16, D), lambda l, n: (0, 0)),
                         pl.BlockSpec((None, D, tn), lambda l, n: (l, 0, n)),
                         pl.BlockSpec((None, 1, tn), lambda l, n: (l, 0, n))],
               out_specs=pl.BlockSpec((None, 16, tn), lambda l, n: (l, 0, n)),
               out_shape=jax.ShapeDtypeStruct((2, 16, N), F32),
               compiler_params=_cp(("parallel", "parallel")))(cond, w, b)


def adaln_bwd(name, cond, w, dm):
    _, D, N = w.shape
    tn = min(_ADA_TN, N)

    def body(c_ref, w_ref, dm_ref, gw_ref, ds_ref):
        first = jnp.logical_and(pl.program_id(0) == 0, pl.program_id(1) == 0)
        s = _silu(c_ref[...]).astype(BF16)
        dmb = dm_ref[...].astype(BF16)
        gw_ref[...] = dot_tn(s, dmb)
        p = dot_nt(dmb, w_ref[...].astype(BF16))

        @pl.when(first)
        def _():
            ds_ref[...] = p

        @pl.when(jnp.logical_not(first))
        def _():
            ds_ref[...] += p

    return _pc(body, name=name, grid=(2, N // tn),
               in_specs=[pl.BlockSpec((16, D), lambda l, n: (0, 0)),
                         pl.BlockSpec((None, D, tn), lambda l, n: (l, 0, n)),
                         pl.BlockSpec((None, 16, tn), lambda l, n: (l, 0, n))],
               out_specs=[pl.BlockSpec((None, D, tn), lambda l, n: (l, 0, n)),
                          pl.BlockSpec((16, D), lambda l, n: (0, 0))],
               out_shape=[jax.ShapeDtypeStruct((2, D, N), F32), jax.ShapeDtypeStruct((16, D), F32)],
               compiler_params=_cp(("arbitrary", "arbitrary")))(cond, w, dm)


def cctx_grad(name, parts, c_ctx):
    def body(p_ref, c_ref, o_ref):
        acc = p_ref[0]
        for j in range(1, N_CHIP):
            acc = acc + p_ref[j]
        o_ref[...] = acc * _dsilu(c_ref[...])

    return _pc(body, name=name, out_shape=jax.ShapeDtypeStruct(c_ctx.shape, F32))(parts, c_ctx)


def _pack(arrs):
    rows = []
    for a in arrs:
        f = a.reshape(-1)
        pad = (-f.shape[0]) % LANE
        rows.append(jnp.pad(f, (0, pad)).reshape(-1, LANE))
    out = jnp.concatenate(rows, axis=0)
    pad = (-out.shape[0]) % 8
    return jnp.pad(out, ((0, pad), (0, 0))) if pad else out


def _unpack(rows, shapes):
    out, r = [], 0
    for s in shapes:
        n = int(np.prod(s))
        nr = -(-n // LANE)
        out.append(rows[r:r + nr].reshape(-1)[:n].reshape(s))
        r += nr
    return out


MOD_NAMES = ("sh1", "sc1", "g1", "sh2", "sc2", "g2")


def kernel(x, c, ctx, c_ctx, w_ada, b_ada, norm1_g, w_in, ret_decay, ret_gn_g, conv_dw_w, conv_dw_b, conv_ln_g, conv_ln_b, conv_pw, na_rpb, w_out, norm2_g, ffn_up, ffn_dw_w, ffn_dw_b, ffn_down, final_g, loss_target, m_c_ctx, m_w_ada, m_b_ada, m_norm1_g, m_w_in, m_ret_decay, m_ret_gn_g, m_conv_dw_w, m_conv_dw_b, m_conv_ln_g, m_conv_ln_b, m_conv_pw, m_na_rpb, m_w_out, m_norm2_g, m_ffn_up, m_ffn_dw_w, m_ffn_dw_b, m_ffn_down, m_final_g, v_c_ctx, v_w_ada, v_b_ada, v_norm1_g, v_w_in, v_ret_decay, v_ret_gn_g, v_conv_dw_w, v_conv_dw_b, v_conv_ln_g, v_conv_ln_b, v_conv_pw, v_na_rpb, v_w_out, v_norm2_g, v_ffn_up, v_ffn_dw_w, v_ffn_dw_b, v_ffn_down, v_final_g):
    cfg = make_cfg(D=x.shape[2], T=x.shape[1], TC=ctx.shape[1], RH=ret_decay.shape[2], CW=conv_dw_b.shape[1],
                   NH=na_rpb.shape[1], DFF=ffn_dw_b.shape[1] // 2)
    D, T = cfg.D, cfg.T
    W = dict(c_ctx=c_ctx, w_ada=w_ada, b_ada=b_ada, norm1_g=norm1_g, w_in=w_in, ret_decay=ret_decay, ret_gn_g=ret_gn_g,
             conv_dw_w=conv_dw_w, conv_dw_b=conv_dw_b, conv_ln_g=conv_ln_g, conv_ln_b=conv_ln_b, conv_pw=conv_pw,
             na_rpb=na_rpb, w_out=w_out, norm2_g=norm2_g, ffn_up=ffn_up, ffn_dw_w=ffn_dw_w, ffn_dw_b=ffn_dw_b,
             ffn_down=ffn_down, final_g=final_g)
    Mo = dict(c_ctx=m_c_ctx, w_ada=m_w_ada, b_ada=m_b_ada, norm1_g=m_norm1_g, w_in=m_w_in, ret_decay=m_ret_decay,
              ret_gn_g=m_ret_gn_g, conv_dw_w=m_conv_dw_w, conv_dw_b=m_conv_dw_b, conv_ln_g=m_conv_ln_g,
              conv_ln_b=m_conv_ln_b, conv_pw=m_conv_pw, na_rpb=m_na_rpb, w_out=m_w_out, norm2_g=m_norm2_g,
              ffn_up=m_ffn_up, ffn_dw_w=m_ffn_dw_w, ffn_dw_b=m_ffn_dw_b, ffn_down=m_ffn_down, final_g=m_final_g)
    Vo = dict(c_ctx=v_c_ctx, w_ada=v_w_ada, b_ada=v_b_ada, norm1_g=v_norm1_g, w_in=v_w_in, ret_decay=v_ret_decay,
              ret_gn_g=v_ret_gn_g, conv_dw_w=v_conv_dw_w, conv_dw_b=v_conv_dw_b, conv_ln_g=v_conv_ln_g,
              conv_ln_b=v_conv_ln_b, conv_pw=v_conv_pw, na_rpb=v_na_rpb, w_out=v_w_out, norm2_g=v_norm2_g,
              ffn_up=v_ffn_up, ffn_dw_w=v_ffn_dw_w, ffn_dw_b=v_ffn_dw_b, ffn_down=v_ffn_down, final_g=v_final_g)
    order = list(W)
    xi, yi, ci = lax.axis_index("x"), lax.axis_index("y"), lax.axis_index("c")
    chip = 2 * xi + yi
    dev = 4 * xi + 2 * yi + ci
    NA = w_ada.shape[2]
    ncw, nfw = conv_dw_w.shape[2], ffn_dw_w.shape[2]

    c_arr = jnp.reshape(ci, (1,)).astype(jnp.int32)
    chip_arr = jnp.reshape(chip, (1,)).astype(jnp.int32)

    g_in = allgather8("ag_small_in", _pack([c[0], conv_dw_w, ffn_dw_w])).reshape(N_DEV, -1, LANE)
    c8 = g_in[:, :D // LANE].reshape(N_DEV, D)
    cw_parts, fw_parts = [], []
    for j in range(N_CHIP):
        _, a, b = _unpack(g_in[2 * j], [(D,), conv_dw_w.shape, ffn_dw_w.shape])
        cw_parts.append(a)
        fw_parts.append(b)
    conv_dw_w_full = jnp.concatenate(cw_parts, axis=2)
    ffn_dw_w_full = jnp.concatenate(fw_parts, axis=2)
    cond = jnp.concatenate([c8, c_ctx[None], jnp.zeros((16 - N_DEV - 1, D), F32)], axis=0)

    b_sh = lax.dynamic_slice(b_ada, (0, chip * NA), (2, NA)).reshape(2, 1, NA)
    m_sh = adaln_fwd("adaln_fwd", cond, w_ada, b_sh)
    m_dev = allgather8("ag_mod", m_sh.reshape(2 * 16, NA)).reshape(N_DEV, 2, 16, NA)
    m_all = jnp.concatenate([m_dev[2 * j] for j in range(N_CHIP)], axis=-1)
    mods = []
    for l in range(2):
        lat = lax.dynamic_slice(m_all[l], (dev, 0), (1, N_CHIP * NA))[0]
        cx = m_all[l, N_DEV]
        mods.append({nm: jnp.stack([lat[k * D:(k + 1) * D], cx[k * D:(k + 1) * D]], 0)[:, None, :]
                     for k, nm in enumerate(MOD_NAMES)})

    first, rest = ("w_in", "conv_pw"), ("w_out", "ffn_up", "ffn_down")
    wb = [{}, {}]
    have, flying_w = {}, {}

    def start_gather(tag, l, names, after):
        s_sem, r_sem, bufs, tok = allgather_layer_start(f"ag_{tag}_start", [wb[l][nm] for nm in names], l, after)
        flying_w[(l, names[0])] = (tag, l, names, bufs, s_sem, r_sem)
        return tok

    def land_gather(key, after):
        tag, l, names, bufs, s_sem, r_sem = flying_w.pop(key)
        landed = allgather_layer_wait(f"ag_{tag}_wait", bufs, s_sem, r_sem, after, l)
        have.update(zip([(l, nm) for nm in names], forward_halves(f"ag_{tag}_fwd", list(landed), l)))

    for nm in first:
        wb[0][nm] = to_bf16_block(f"to_bf16_{nm}_0", W[nm], chip_arr, 0)
    tok_first = start_gather("w0a", 0, first, m_all)
    later = [(l, nm) for l in range(2) for nm in BIG if nm not in wb[l]]
    casts = lax.optimization_barrier(tuple(
        to_bf16_block(f"to_bf16_{nm}_{l}", W[nm], chip_arr, l, after=tok_first) for l, nm in later))
    for (l, nm), cast in zip(later, casts):
        wb[l][nm] = cast
    land_gather((0, first[0]), casts[0])
    mods[0] = {**mods[0], "sc1": mods[0]["sc1"] + start_gather("w0b", 0, rest, have[(0, first[0])])[0, 0]}

    def wts(l, name, after):
        tok = None
        if (l, name) not in have:
            if l == 0:
                land_gather((0, rest[0]), after)
                tok = start_gather("w1", 1, BIG, have[(0, rest[0])])[0, 0]
            else:
                land_gather((1, BIG[0]), after)
        return have[(l, name)], tok

    sp = dict(norm1_g=norm1_g, norm2_g=norm2_g, ret_decay=ret_decay, ret_gn_g=ret_gn_g, conv_dw_w=conv_dw_w_full,
              conv_dw_b=conv_dw_b, conv_ln_g=conv_ln_g, conv_ln_b=conv_ln_b, na_rpb=na_rpb, ffn_dw_w=ffn_dw_w_full,
              ffn_dw_b=ffn_dw_b, final_g=final_g)
    flights = []

    def grads_ready(l, names, gb):
        tag = f"{l}_{names[0]}"
        from_sib = exchange_rows(f"rs_exchange_{tag}", [gb[nm] for nm in names], l)
        both = [add_rows(f"rs_add_{nm}_{l}", gb[nm], r, c_arr, chip_arr, l) for nm, r in zip(names, from_sib)]
        part, lands = [b[0] for b in both], [b[1] for b in both]
        s_sem, r_sem, part, lands, tok = scatter_slices_start(f"rs_scatter_{tag}_start", part, lands)
        flights.append((l, names, tag, part, lands, (s_sem, r_sem)))
        return tok[0, 0]

    loss_l, gx, gb, gss, dms, dfg, tok_last = local_step(
        cfg, x[0], ctx[0], loss_target[0], mods, wts, sp, grads_ready)
    loss = lax.psum(loss_l, ("x", "y", "c"))

    delta, new_m, new_v = {}, {}, {}
    bigs = ("w_ada",) + BIG

    def adamw_big(nm):
        shp = W[nm].shape
        v2 = lambda a: a.reshape(-1, shp[-1])
        d_, m_, v_, *g_ = adamw(f"adamw_{nm}", v2(W[nm]), v2(gfull[nm]), v2(Mo[nm]), v2(Vo[nm]), emit_g=nm in BIG)
        delta[nm], new_m[nm], new_v[nm] = d_.reshape(shp), m_.reshape(shp), v_.reshape(shp)
        if g_:
            gfull[nm] = g_[0].reshape(shp)

    gfull, fin, after = {}, {}, gx
    for names in GRAD_GROUPS:
        for l, _, tag, part, lands, sems in sorted([f for f in flights if f[1] == names], key=lambda f: -f[0]):
            landed = scatter_slices_wait(f"rs_scatter_{tag}_wait", part, lands, *sems, after)
            for nm, p in zip(names, landed):
                fin[nm] = sum_rows_into(f"rs_sum_{nm}_{l}", p, c_arr, l, fin.get(nm))
        for nm, gfin in zip(names, share_rows(f"rs_share_{names[0]}", [fin[nm] for nm in names])):
            gfull[nm] = gfin.reshape(W[nm].shape)
            adamw_big(nm)
        after = delta[names[-1]]

    dmseg = jnp.stack([jnp.stack([jnp.concatenate([dms[l][nm][r, 0] for nm in MOD_NAMES]) for r in range(2)])
                       for l in range(2)])
    dmseg, _ = lax.optimization_barrier((dmseg, fin[GRAD_GROUPS[-1][-1]]))
    gsm = dict(
        norm1_g=jnp.stack([gss[l]["norm1_g"][0] for l in range(2)]),
        ret_decay=jnp.stack([gss[l]["lam"] * jax.nn.sigmoid(-ret_decay[l]) for l in range(2)]),
        ret_gn_g=jnp.stack([gss[l]["ret_gn_g"][0] for l in range(2)]),
        conv_dw_w=jnp.stack([gss[l]["conv_dw_w"][:cfg.CK] for l in range(2)]),
        conv_dw_b=jnp.stack([gss[l]["conv_dw_b"][0] for l in range(2)]),
        conv_ln_g=jnp.stack([gss[l]["conv_ln_g"][0] for l in range(2)]),
        conv_ln_b=jnp.stack([gss[l]["conv_ln_b"][0] for l in range(2)]),
        na_rpb=jnp.stack([gss[l]["na_rpb"] for l in range(2)]),
        norm2_g=jnp.stack([gss[l]["norm2_g"][0] for l in range(2)]),
        ffn_dw_w=jnp.stack([gss[l]["ffn_dw_w"][:, :3].transpose(1, 0, 2).reshape(3, 2 * cfg.DFF) for l in range(2)]),
        ffn_dw_b=jnp.stack([gss[l]["ffn_dw_b"].reshape(-1) for l in range(2)]),
        final_g=dfg)
    snames = list(gsm)
    sshapes = [dmseg.shape] + [gsm[nm].shape for nm in snames]
    packed = _pack([dmseg] + [gsm[nm] for nm in snames])
    g_all = allgather8("ag_small_grads", packed).reshape(N_DEV, packed.shape[0], LANE)
    summed = sum_leading("sum_small_grads", g_all)
    dm_sum, *gsum = _unpack(summed, sshapes)
    gfull.update(zip(snames, gsum))
    ndm = int(np.prod(dmseg.shape))
    dm_all = g_all[:, :ndm // LANE].reshape(N_DEV, 2, 2, 6 * D)
    gfull["b_ada"] = sum_leading("sum_b_ada", dm_all.transpose(0, 2, 1, 3).reshape(2 * N_DEV, 2 * 6 * D // LANE, LANE)
                                 ).reshape(2, 6 * D)

    dm16 = jnp.concatenate([dm_all[:, :, 0].transpose(1, 0, 2), dm_sum[:, 1][:, None],
                            jnp.zeros((2, 16 - N_DEV - 1, 6 * D), F32)], axis=1)
    dm16 = lax.dynamic_slice(dm16, (0, 0, chip * NA), (2, 16, NA))
    gfull["w_ada"], ds16 = adaln_bwd("adaln_bwd", cond, w_ada, dm16)
    ds_all = allgather8("ag_dsilu", ds16[8:16]).reshape(N_DEV, 8, D)[0::2, 0:1]
    gfull["c_ctx"] = cctx_grad("cctx_grad", ds_all, c_ctx[None])[0]
    gfull["conv_dw_w"] = lax.dynamic_slice(gfull["conv_dw_w"], (0, 0, chip * ncw), (2, cfg.CK, ncw))
    gfull["ffn_dw_w"] = lax.dynamic_slice(gfull["ffn_dw_w"], (0, 0, chip * nfw), (2, 3, nfw))

    adamw_big("w_ada")
    smalls = [nm for nm in order if nm not in bigs]
    shapes = [W[nm].shape for nm in smalls]
    d_, m_, v_ = adamw("adamw_small", _pack([W[nm] for nm in smalls]), _pack([gfull[nm] for nm in smalls]),
                       _pack([Mo[nm] for nm in smalls]), _pack([Vo[nm] for nm in smalls]))
    for nm, a, b, e in zip(smalls, _unpack(d_, shapes), _unpack(m_, shapes), _unpack(v_, shapes)):
        delta[nm], new_m[nm], new_v[nm] = a, b, e
    return (loss, gx[None], *[gfull[nm] for nm in order], *[delta[nm] for nm in order],
            *[new_m[nm] for nm in order], *[new_v[nm] for nm in order])
```

```python
import collections
import functools

import numpy as np
import jax
import jax.numpy as jnp
from jax import lax
from jax.experimental import pallas as pl
from jax.experimental.pallas import tpu as pltpu

F32 = jnp.float32
BF16 = jnp.bfloat16
EPS = 1e-6
ROPE_BASE = 10000.0
NEG = -1e30
LANE = 128
VMEM_LIMIT = 56 * 1024 * 1024

ADAM_LR, ADAM_B1, ADAM_B2, ADAM_EPS, ADAM_WD, ADAM_STEP = 0.001, 0.9, 0.999, 1e-08, 0.01, 10

Cfg = collections.namedtuple(
    "Cfg", "D T TC GW RH RDK RDV CW CK NH NDH NAR NAC DFF TB")


def make_cfg(D=2048, T=4096, TC=256, RH=4, CW=512, NH=4, DFF=5632):
    return Cfg(D=D, T=T, TC=TC, GW=64, RH=RH, RDK=128, RDV=256, CW=CW, CK=31, NH=NH, NDH=128,
               NAR=8, NAC=16, DFF=DFF, TB=256)


def _offsets(cfg):
    sizes = [cfg.RH * cfg.RDK, cfg.RH * cfg.RDK, cfg.RH * cfg.RDV, cfg.RH * cfg.RDV, cfg.CW, cfg.CW,
             cfg.NH * cfg.NDH, cfg.NH * cfg.NDH, cfg.NH * cfg.NDH]
    offs = [0]
    for s in sizes:
        offs.append(offs[-1] + s)
    return dict(zip(["lq", "lk", "lv", "lg", "la", "lb", "nq", "nk", "nv", "end"], offs))


def _pc(body, **kw):
    return pl.pallas_call(body, **kw)


def _cp(sem=None):
    return pltpu.CompilerParams(dimension_semantics=sem, vmem_limit_bytes=VMEM_LIMIT)


def _dot(a, b, ca, cb):
    return lax.dot_general(a, b, (((ca,), (cb,)), ((), ())), preferred_element_type=F32)


def dot_nn(a, b):
    return _dot(a, b, 1, 0)


def dot_nt(a, b):
    return _dot(a, b, 1, 1)


def dot_tn(a, b):
    return _dot(a, b, 0, 0)


def _sigmoid(x):
    return 1.0 / (1.0 + jnp.exp(-x))


def _silu(x):
    return x * _sigmoid(x)


def _dsilu(x):
    s = _sigmoid(x)
    return s * (1.0 + x * (1.0 - s))


def matmul(name, a, b, *, contract, grid, a_spec, b_spec, out_shape, out_spec, nk, into=None):
    dot = {"nn": dot_nn, "nt": dot_nt, "tn": dot_tn}[contract]
    direct = nk > 1 and out_shape.dtype == F32
    kax = len(grid) - 1

    def body(a_ref, b_ref, *rest):
        o_ref, *scr = rest[1:] if into is not None else rest
        p = dot(a_ref[...].astype(BF16), b_ref[...].astype(BF16))
        if nk == 1:
            o_ref[...] = p.astype(o_ref.dtype)
            return
        acc = o_ref if direct else scr[0]
        k = pl.program_id(kax)

        @pl.when(k == 0)
        def _():
            acc[...] = p

        @pl.when(k > 0)
        def _():
            acc[...] += p

        if not direct:
            @pl.when(k == nk - 1)
            def _():
                o_ref[...] = acc[...].astype(o_ref.dtype)

    scratch = []
    if nk > 1 and not direct:
        blk = [s for s in out_spec.block_shape if s is not None]
        scratch = [pltpu.VMEM(tuple(blk), F32)]
    sem = ("parallel",) * kax + (("arbitrary",) if nk > 1 else ("parallel",))
    in_specs, args, alias = [a_spec, b_spec], (a, b), {}
    if into is not None:
        in_specs, args, alias = in_specs + [pl.BlockSpec(memory_space=pl.ANY)], (a, b, into), {2: 0}
    return _pc(body, name=name, grid=grid, in_specs=in_specs, out_specs=out_spec, out_shape=out_shape,
               scratch_shapes=scratch, input_output_aliases=alias, compiler_params=_cp(sem))(*args)


_WG_ROWS = 1024


def wgrad(cfg, name, a, dc, a_spec, dc_spec, out_shape, out_spec, ntiles, into):
    T, TC = cfg.T, cfg.TC
    tml = min(_WG_ROWS, T)
    nl = T // tml

    def body(al_ref, ac_ref, dl_ref, dcx_ref, *rest):
        o_ref, acc = rest[-2:]
        m = pl.program_id(1)

        @pl.when(m == 0)
        def _():
            acc[...] = dot_tn(al_ref[...], dl_ref[...])

        @pl.when(jnp.logical_and(m > 0, m < nl))
        def _():
            acc[...] += dot_tn(al_ref[...], dl_ref[...])

        @pl.when(m == nl)
        def _():
            o_ref[...] = (acc[...] + dot_tn(ac_ref[...], dcx_ref[...])).astype(o_ref.dtype)

    lat = lambda m: jnp.minimum(m, nl - 1)
    ctx = lambda m: T // TC
    in_specs = [a_spec(tml, lat), a_spec(TC, ctx), dc_spec(tml, lat), dc_spec(TC, ctx)]
    args, alias = (a, a, dc, dc), {}
    if into is not None:
        in_specs, args, alias = in_specs + [pl.BlockSpec(memory_space=pl.ANY)], args + (into,), {4: 0}
    blk = tuple(s for s in out_spec.block_shape if s is not None)
    return _pc(body, name=name, grid=(ntiles, nl + 1), in_specs=in_specs, out_specs=out_spec, out_shape=out_shape,
               scratch_shapes=[pltpu.VMEM(blk, F32)], input_output_aliases=alias,
               compiler_params=_cp(("parallel", "arbitrary")))(*args)


def mm_rowsharded(name, a, w4, l, out_dtype, tn):
    pieces = a if isinstance(a, tuple) else (a,)
    L = pieces[0].shape[0]
    nch, _, Kb, N = w4.shape
    tm = _tm(L, 8)
    assert all(p.shape[1] % Kb == 0 for p in pieces) and sum(p.shape[1] for p in pieces) == nch * Kb

    def body(*refs):
        w_ref, o_ref = refs[-2:]
        acc, j = None, 0
        for a_ref in refs[:-2]:
            for b in range(a_ref.shape[1] // Kb):
                p = dot_nn(a_ref[:, b * Kb:(b + 1) * Kb], w_ref[j])
                acc = p if acc is None else acc + p
                j += 1
        o_ref[...] = acc.astype(o_ref.dtype)

    return _pc(body, name=name, grid=(N // tn, L // tm),
               in_specs=[pl.BlockSpec((tm, p.shape[1]), lambda n, m: (m, 0)) for p in pieces]
               + [pl.BlockSpec((nch, None, Kb, tn), lambda n, m: (0, l, 0, n))],
               out_specs=pl.BlockSpec((tm, tn), lambda n, m: (m, n)),
               out_shape=jax.ShapeDtypeStruct((L, N), out_dtype),
               compiler_params=_cp(("parallel", "parallel")))(*pieces, w4)


def _region(cfg):
    nlat = cfg.T // cfg.TB
    return lambda i: jnp.minimum(i // nlat, 1)


def norm_mod_fwd(cfg, name, x, ng, sc, sh):
    L, D = x.shape
    TB = cfg.TB
    reg = _region(cfg)

    def body(x_ref, ng_ref, sc_ref, sh_ref, h_ref):
        xv = x_ref[...]
        r = lax.rsqrt(jnp.mean(xv * xv, axis=-1, keepdims=True) + EPS)
        n = xv * r * ng_ref[...]
        h_ref[...] = (n * (1.0 + sc_ref[...]) + sh_ref[...]).astype(h_ref.dtype)

    row = pl.BlockSpec((TB, D), lambda i: (i, 0))
    vec = pl.BlockSpec((1, D), lambda i: (0, 0))
    rvec = pl.BlockSpec((None, 1, D), lambda i: (reg(i), 0, 0))
    return _pc(body, name=name, grid=(L // TB,), in_specs=[row, vec, rvec, rvec], out_specs=row,
               out_shape=jax.ShapeDtypeStruct((L, D), BF16), compiler_params=_cp(("parallel",)))(x, ng, sc, sh)


def norm_mod_bwd(cfg, name, dh, x, ng, sc, dx_in, latent_only=False):
    L, D = x.shape
    TB = cfg.TB
    nlat = cfg.T // TB
    reg = _region(cfg)

    def body(dh_ref, x_ref, ng_ref, sc_ref, dxi_ref, dx_ref, dsc_ref, dsh_ref, dng_ref):
        i = pl.program_id(0)
        xv = x_ref[...]
        r = lax.rsqrt(jnp.mean(xv * xv, axis=-1, keepdims=True) + EPS)
        xh = xv * r
        g = ng_ref[...]
        n = xh * g
        dh = dh_ref[...]
        dn = dh * (1.0 + sc_ref[...])
        dxh = dn * g
        dx = r * (dxh - xh * jnp.mean(dxh * xh, axis=-1, keepdims=True))
        if latent_only:
            @pl.when(i < nlat)
            def _():
                dx_ref[...] = dxi_ref[...] + dx
        else:
            dx_ref[...] = dxi_ref[...] + dx
        s_sh = jnp.sum(dh, axis=0, keepdims=True)
        s_sc = jnp.sum(dh * n, axis=0, keepdims=True)
        s_ng = jnp.sum(dn * xh, axis=0, keepdims=True)
        first = jnp.logical_or(i == 0, i == nlat)

        @pl.when(first)
        def _():
            dsh_ref[...] = s_sh
            dsc_ref[...] = s_sc

        @pl.when(jnp.logical_not(first))
        def _():
            dsh_ref[...] += s_sh
            dsc_ref[...] += s_sc

        @pl.when(i == 0)
        def _():
            dng_ref[...] = s_ng

        @pl.when(i > 0)
        def _():
            dng_ref[...] += s_ng

    row = pl.BlockSpec((TB, D), lambda i: (i, 0))
    vec = pl.BlockSpec((1, D), lambda i: (0, 0))
    rvec = pl.BlockSpec((None, 1, D), lambda i: (reg(i), 0, 0))
    dxs = pl.BlockSpec((TB, D), lambda i: (jnp.minimum(i, nlat - 1), 0)) if latent_only else row
    return _pc(body, name=name, grid=(L // TB,), in_specs=[row, row, vec, rvec, row],
               out_specs=[dxs, rvec, rvec, vec],
               out_shape=[jax.ShapeDtypeStruct((cfg.T if latent_only else L, D), F32),
                          jax.ShapeDtypeStruct((2, 1, D), F32),
                          jax.ShapeDtypeStruct((2, 1, D), F32), jax.ShapeDtypeStruct((1, D), F32)],
               compiler_params=_cp(("arbitrary",)))(dh, x, ng, sc, dx_in)


def resid_norm_fwd(cfg, name, x, y, g, ng, sc, sh):
    L, D = x.shape
    TB = cfg.TB
    reg = _region(cfg)

    def body(x_ref, y_ref, g_ref, ng_ref, sc_ref, sh_ref, xo_ref, h_ref):
        xv = x_ref[...] + g_ref[...] * y_ref[...]
        xo_ref[...] = xv
        r = lax.rsqrt(jnp.mean(xv * xv, axis=-1, keepdims=True) + EPS)
        n = xv * r * ng_ref[...]
        h_ref[...] = (n * (1.0 + sc_ref[...]) + sh_ref[...]).astype(h_ref.dtype)

    row = pl.BlockSpec((TB, D), lambda i: (i, 0))
    vec = pl.BlockSpec((1, D), lambda i: (0, 0))
    rvec = pl.BlockSpec((None, 1, D), lambda i: (reg(i), 0, 0))
    return _pc(body, name=name, grid=(L // TB,), in_specs=[row, row, rvec, vec, rvec, rvec], out_specs=[row, row],
               out_shape=[jax.ShapeDtypeStruct((L, D), F32), jax.ShapeDtypeStruct((L, D), BF16)],
               compiler_params=_cp(("parallel",)))(x, y, g, ng, sc, sh)


def resid_bwd(cfg, name, dxo, y, g):
    L, D = y.shape
    TB = cfg.TB
    nlat = cfg.T // TB
    reg = _region(cfg)

    def body(d_ref, y_ref, g_ref, dy_ref, dg_ref):
        i = pl.program_id(0)
        d = d_ref[...]
        dy_ref[...] = (d * g_ref[...]).astype(dy_ref.dtype)
        s = jnp.sum(d * y_ref[...], axis=0, keepdims=True)
        first = jnp.logical_or(i == 0, i == nlat)

        @pl.when(first)
        def _():
            dg_ref[...] = s

        @pl.when(jnp.logical_not(first))
        def _():
            dg_ref[...] += s

    row = pl.BlockSpec((TB, D), lambda i: (i, 0))
    rvec = pl.BlockSpec((None, 1, D), lambda i: (reg(i), 0, 0))
    return _pc(body, name=name, grid=(L // TB,), in_specs=[row, row, rvec], out_specs=[row, rvec],
               out_shape=[jax.ShapeDtypeStruct((L, D), BF16), jax.ShapeDtypeStruct((2, 1, D), F32)],
               compiler_params=_cp(("arbitrary",)))(dxo, y, g)


def final_loss(cfg, name, xm, y2, g2, fg, tgt):
    L, D = xm.shape
    TB = cfg.TB
    nlat = cfg.T // TB

    def body(x_ref, y_ref, g2_ref, fg_ref, t_ref, ls_ref, dx_ref, dg_ref):
        i = pl.program_id(0)

        @pl.when(i == 0)
        def _():
            ls_ref[...] = jnp.zeros_like(ls_ref)
            dg_ref[...] = jnp.zeros_like(dg_ref)

        @pl.when(i < nlat)
        def _():
            xv = x_ref[...] + g2_ref[...] * y_ref[...]
            r = lax.rsqrt(jnp.mean(xv * xv, axis=-1, keepdims=True) + EPS)
            xh = xv * r
            g = fg_ref[...]
            e = xh * g - t_ref[...]
            ls_ref[...] += 0.5 * jnp.sum(e * e) / D
            dy = e / D
            dg_ref[...] += jnp.sum(dy * xh, axis=0, keepdims=True)
            dxh = dy * g
            dx_ref[...] = r * (dxh - xh * jnp.mean(dxh * xh, axis=-1, keepdims=True))

        @pl.when(i >= nlat)
        def _():
            dx_ref[...] = jnp.zeros_like(dx_ref)

    row = pl.BlockSpec((TB, D), lambda i: (i, 0))
    trow = pl.BlockSpec((TB, D), lambda i: (jnp.minimum(i, nlat - 1), 0))
    vec = pl.BlockSpec((1, D), lambda i: (0, 0))
    return _pc(body, name=name, grid=(L // TB,), in_specs=[row, row, vec, vec, trow],
               out_specs=[pl.BlockSpec((1, LANE), lambda i: (0, 0)), row, vec],
               out_shape=[jax.ShapeDtypeStruct((1, LANE), F32), jax.ShapeDtypeStruct((L, D), F32),
                          jax.ShapeDtypeStruct((1, D), F32)],
               compiler_params=_cp(("arbitrary",)))(xm, y2, g2, fg, tgt)


def rope_tables(cfg):
    half = cfg.RDK // 2
    nf = half // 2
    pos = np.arange(cfg.T)
    row = (pos // cfg.GW).astype(np.float32)
    col = (pos % cfg.GW).astype(np.float32)
    inv = jnp.asarray(ROPE_BASE, F32) ** (-jnp.arange(nf, dtype=F32) / nf)
    ar = jnp.asarray(row)[:, None] * inv[None, :]
    ac = jnp.asarray(col)[:, None] * inv[None, :]
    cos = jnp.concatenate([jnp.cos(ar), jnp.cos(ar), jnp.cos(ac), jnp.cos(ac)], axis=1)
    sin = jnp.concatenate([-jnp.sin(ar), jnp.sin(ar), -jnp.sin(ac), jnp.sin(ac)], axis=1)
    cos = jnp.concatenate([cos, jnp.ones((cfg.TC, cfg.RDK), F32)], axis=0)
    sin = jnp.concatenate([sin, jnp.zeros((cfg.TC, cfg.RDK), F32)], axis=0)
    return cos, sin


def _rb(cfg):
    rb = (cfg.T + cfg.TC) // 4
    assert rb % 16 == 0
    return rb


def _swap32(t):
    lane = lax.broadcasted_iota(jnp.int32, t.shape, 1)
    return jnp.where((lane % 64) < 32, pltpu.roll(t, 96, 1), pltpu.roll(t, 32, 1))


def rope_fwd(cfg, name, P, cos, sin):
    L = P.shape[0]
    TB = _rb(cfg)
    off = _offsets(cfg)
    cq, ck = off["lq"] // LANE, off["lk"] // LANE
    scale = cfg.RDK ** -0.5

    def body(q_ref, k_ref, c_ref, s_ref, qo_ref, ko_ref):
        c = c_ref[...]
        s = s_ref[...]
        q = q_ref[...]
        k = k_ref[...]
        qo_ref[...] = (q * c + _swap32(q) * s) * scale
        ko_ref[...] = k * c + _swap32(k) * s

    tab = pl.BlockSpec((TB, LANE), lambda i, h: (i, 0))
    out = pl.BlockSpec((TB, LANE), lambda i, h: (i, h))
    shp = jax.ShapeDtypeStruct((L, cfg.RH * cfg.RDK), F32)
    return _pc(body, name=name, grid=(L // TB, cfg.RH),
               in_specs=[pl.BlockSpec((TB, LANE), lambda i, h: (i, cq + h)),
                         pl.BlockSpec((TB, LANE), lambda i, h: (i, ck + h)), tab, tab],
               out_specs=[out, out], out_shape=[shp, shp],
               compiler_params=_cp(("parallel", "parallel")))(P, P, cos, sin)


def rope_bwd(cfg, name, dq2, dk2, cos, sin):
    L, W = dq2[0].shape
    TB = _rb(cfg)
    scale = cfg.RDK ** -0.5

    def body(dqf_ref, dqb_ref, dkf_ref, dkb_ref, c_ref, s_ref, qo_ref, ko_ref):
        c = c_ref[...]
        s = s_ref[...]
        dq = dqf_ref[...] + dqb_ref[...]
        dk = dkf_ref[...] + dkb_ref[...]
        qo_ref[...] = ((dq * c - _swap32(dq) * s) * scale).astype(qo_ref.dtype)
        ko_ref[...] = (dk * c - _swap32(dk) * s).astype(ko_ref.dtype)

    tab = pl.BlockSpec((TB, LANE), lambda i, h: (i, 0))
    blk = pl.BlockSpec((TB, LANE), lambda i, h: (i, h))
    shp = jax.ShapeDtypeStruct((L, W), BF16)
    return _pc(body, name=name, grid=(L // TB, cfg.RH), in_specs=[blk, blk, blk, blk, tab, tab],
               out_specs=[blk, blk], out_shape=[shp, shp],
               compiler_params=_cp(("parallel", "parallel")))(*dq2, *dk2, cos, sin)


def _ret_chunk_map(cfg):
    C = cfg.RDK
    n = (cfg.T + cfg.TC) // C
    nlat, nctx = cfg.T // C, cfg.TC // C

    def chunk(d, s):
        if d == 0:
            return jnp.where(s < nctx, nlat + s, s - nctx)
        return n - 1 - s

    return n, chunk


def _ret_decay_terms(d, lam, C):
    ii = lax.broadcasted_iota(jnp.int32, (C, C), 0)
    jj = lax.broadcasted_iota(jnp.int32, (C, C), 1)
    diff = (ii - jj if d == 0 else jj - ii).astype(F32)
    dpos = jnp.maximum(diff, 0.0)
    Dm = jnp.where(diff >= 0, jnp.exp(dpos * lam), 0.0)
    ic = lax.broadcasted_iota(jnp.int32, (C, 1), 0).astype(F32)
    cxi = ic + 1.0 if d == 0 else C - ic
    cze = C - 1.0 - ic if d == 0 else ic
    xi = jnp.exp(cxi * lam)
    ze = jnp.exp(cze * lam)
    g = jnp.exp(jnp.full((1, 1), C, F32) * lam)
    return dpos, Dm, cxi, cze, xi, ze, g


def retention_fwd(cfg, name, qr, kr, P, lam):
    L = P.shape[0]
    C, DV, RH = cfg.RDK, cfg.RDV, cfg.RH
    n, chunk = _ret_chunk_map(cfg)

    def body(lam_ref, qf_ref, qb_ref, kf_ref, kb_ref, vf_ref, vb_ref, of_ref, ob_ref, st_ref, S):
        s = pl.program_id(0)

        @pl.when(s == 0)
        def _():
            S[...] = jnp.zeros_like(S)

        for d, (q_ref, k_ref, v_ref, o_ref) in enumerate(((qf_ref, kf_ref, vf_ref, of_ref),
                                                          (qb_ref, kb_ref, vb_ref, ob_ref))):
            for h in range(RH):
                _, Dm, _, _, xi, ze, g = _ret_decay_terms(d, lam_ref[d, h], C)
                k = k_ref[:, h * C:(h + 1) * C]
                qb = q_ref[:, h * C:(h + 1) * C].astype(BF16)
                kb = k.astype(BF16)
                vb = v_ref[:, h * DV:(h + 1) * DV].astype(BF16)
                Sv = S[d, h]
                st_ref[d, h] = Sv
                A = dot_nt(qb, kb) * Dm
                o_ref[:, h * DV:(h + 1) * DV] = dot_nn(A.astype(BF16), vb) + dot_nn(qb, Sv.astype(BF16)) * xi
                S[d, h] = Sv * g + dot_tn((k * ze).astype(BF16), vb)

    def spec(w, col, d):
        return pl.BlockSpec((C, w), lambda s: (chunk(d, s), col))

    W, WV = RH * C, RH * DV
    return _pc(body, name=name, grid=(n,),
               in_specs=[pl.BlockSpec(memory_space=pltpu.SMEM), spec(W, 0, 0), spec(W, 0, 1), spec(W, 0, 0),
                         spec(W, 0, 1), spec(WV, 1, 0), spec(WV, 1, 1)],
               out_specs=[spec(WV, 0, 0), spec(WV, 0, 1),
                          pl.BlockSpec((2, RH, None, C, DV), lambda s: (0, 0, s, 0, 0))],
               out_shape=[jax.ShapeDtypeStruct((L, WV), F32), jax.ShapeDtypeStruct((L, WV), F32),
                          jax.ShapeDtypeStruct((2, RH, n, C, DV), F32)],
               scratch_shapes=[pltpu.VMEM((2, RH, C, DV), F32)],
               compiler_params=_cp(("arbitrary",)))(lam, qr, qr, kr, kr, P, P)


def retention_bwd(cfg, name, qr, kr, P, lam, st, do):
    L = P.shape[0]
    C, DV, RH = cfg.RDK, cfg.RDV, cfg.RH
    n, chunk = _ret_chunk_map(cfg)

    def body(lam_ref, qf_ref, qb_ref, kf_ref, kb_ref, vf_ref, vb_ref, st_ref, dof_ref, dob_ref,
             dqf_ref, dqb_ref, dkf_ref, dkb_ref, dvf_ref, dvb_ref, dl_ref, dS):
        si = pl.program_id(0)

        @pl.when(si == 0)
        def _():
            dS[...] = jnp.zeros_like(dS)
            dl_ref[...] = jnp.zeros_like(dl_ref)

        dirs = ((qf_ref, kf_ref, vf_ref, dof_ref, dqf_ref, dkf_ref, dvf_ref),
                (qb_ref, kb_ref, vb_ref, dob_ref, dqb_ref, dkb_ref, dvb_ref))
        for d, (q_ref, k_ref, v_ref, do_ref, dq_ref, dk_ref, dv_ref) in enumerate(dirs):
            for h in range(RH):
                dpos, Dm, cxi, cze, xi, ze, g = _ret_decay_terms(d, lam_ref[d, h], C)
                hk = slice(h * C, (h + 1) * C)
                hv = slice(h * DV, (h + 1) * DV)
                k = k_ref[:, hk]
                do = do_ref[:, hv]
                qb = q_ref[:, hk].astype(BF16)
                kb = k.astype(BF16)
                vb = v_ref[:, hv].astype(BF16)
                dob = do.astype(BF16)
                Sn = st_ref[d, h]
                Snb = Sn.astype(BF16)
                dSn = dS[d, h]
                dSb = dSn.astype(BF16)
                A = dot_nt(qb, kb) * Dm
                dA = dot_nt(dob, vb)
                dQK = (dA * Dm).astype(BF16)
                kzb = (k * ze).astype(BF16)
                dv_ref[:, hv] = dot_tn(A.astype(BF16), dob) + dot_nn(kzb, dSb)
                dkz = dot_nt(vb, dSb)
                doxb = (do * xi).astype(BF16)
                dq_ref[:, hk] = dot_nn(dQK, kb) + dot_nt(doxb, Snb)
                dk_ref[:, hk] = dot_tn(dQK, qb) + dkz * ze
                QS = dot_nn(qb, Snb)
                t = (jnp.sum(dA * A * dpos) + jnp.sum(do * QS * (cxi * xi)) + jnp.sum(k * dkz * (cze * ze)))
                t4 = jnp.sum(dSn * Sn, axis=0, keepdims=True)
                t4 = jnp.sum(t4 * (g * C), axis=1, keepdims=True)
                dl_ref[d, h] += t + t4
                dS[d, h] = g * dSn + dot_tn(qb, doxb)

    def spec(w, col, d):
        return pl.BlockSpec((C, w), lambda si: (chunk(d, n - 1 - si), col))

    W, WV = RH * C, RH * DV
    return _pc(body, name=name, grid=(n,),
               in_specs=[pl.BlockSpec(memory_space=pltpu.SMEM), spec(W, 0, 0), spec(W, 0, 1), spec(W, 0, 0),
                         spec(W, 0, 1), spec(WV, 1, 0), spec(WV, 1, 1),
                         pl.BlockSpec((2, RH, None, C, DV), lambda si: (0, 0, n - 1 - si, 0, 0)),
                         spec(WV, 0, 0), spec(WV, 0, 1)],
               out_specs=[spec(W, 0, 0), spec(W, 0, 1), spec(W, 0, 0), spec(W, 0, 1), spec(WV, 0, 0), spec(WV, 0, 1),
                          pl.BlockSpec((2, RH, 8, LANE), lambda si: (0, 0, 0, 0))],
               out_shape=[jax.ShapeDtypeStruct((L, W), F32)] * 4 + [jax.ShapeDtypeStruct((L, WV), F32)] * 2
               + [jax.ShapeDtypeStruct((2, RH, 8, LANE), F32)],
               scratch_shapes=[pltpu.VMEM((2, RH, C, DV), F32)],
               compiler_params=_cp(("arbitrary",)))(lam, qr, qr, kr, kr, P, P, st, do, do)


def add_cast(cfg, name, a, b):
    L, W = a.shape
    TB = _rb(cfg)

    def body(a_ref, b_ref, o_ref):
        o_ref[...] = (a_ref[...] + b_ref[...]).astype(o_ref.dtype)

    spec = pl.BlockSpec((TB, W), lambda i: (i, 0))
    return _pc(body, name=name, grid=(L // TB,), in_specs=[spec, spec], out_specs=spec,
               out_shape=jax.ShapeDtypeStruct((L, W), BF16), compiler_params=_cp(("parallel",)))(a, b)


def ggn_fwd(cfg, name, o2, P, gn_g):
    L = P.shape[0]
    TB, DV, RH = _rb(cfg), cfg.RDV, cfg.RH
    gc0 = _offsets(cfg)["lg"] // DV

    def body(of_ref, ob_ref, gate_ref, g_ref, out_ref):
        o = of_ref[...] + ob_ref[...]
        mu = jnp.mean(o, axis=-1, keepdims=True)
        xc = o - mu
        var = jnp.mean(xc * xc, axis=-1, keepdims=True)
        y = xc * lax.rsqrt(var + EPS) * g_ref[...]
        out_ref[...] = (y * _silu(gate_ref[...])).astype(out_ref.dtype)

    blk = pl.BlockSpec((TB, DV), lambda i, h: (i, h))
    return _pc(body, name=name, grid=(L // TB, RH),
               in_specs=[blk, blk, pl.BlockSpec((TB, DV), lambda i, h: (i, gc0 + h)),
                         pl.BlockSpec((1, DV), lambda i, h: (0, h))],
               out_specs=blk, out_shape=jax.ShapeDtypeStruct((L, RH * DV), BF16),
               compiler_params=_cp(("parallel", "parallel")))(*o2, P, gn_g)


def ggn_bwd(cfg, name, dout, o2, P, gn_g, col0):
    L = P.shape[0]
    TB, DV, RH = _rb(cfg), cfg.RDV, cfg.RH
    gc0 = _offsets(cfg)["lg"] // DV

    def body(d_ref, of_ref, ob_ref, gate_ref, g_ref, do_ref, dgate_ref, dg_ref):
        i = pl.program_id(1)
        o = of_ref[...] + ob_ref[...]
        mu = jnp.mean(o, axis=-1, keepdims=True)
        xc = o - mu
        var = jnp.mean(xc * xc, axis=-1, keepdims=True)
        r = lax.rsqrt(var + EPS)
        y = xc * r
        g = g_ref[...]
        gate = gate_ref[...]
        d = d_ref[...]
        dgate_ref[...] = (d * (y * g) * _dsilu(gate)).astype(dgate_ref.dtype)
        dyg = d * _silu(gate)
        s = jnp.sum(dyg * y, axis=0, keepdims=True)

        @pl.when(i == 0)
        def _():
            dg_ref[...] = s

        @pl.when(i > 0)
        def _():
            dg_ref[...] += s

        dy = dyg * g
        do_ref[...] = r * (dy - jnp.mean(dy, axis=-1, keepdims=True)
                           - y * jnp.mean(dy * y, axis=-1, keepdims=True))

    blk = pl.BlockSpec((TB, DV), lambda h, i: (i, h))
    return _pc(body, name=name, grid=(RH, L // TB),
               in_specs=[pl.BlockSpec((TB, DV), lambda h, i: (i, col0 + h)), blk, blk,
                         pl.BlockSpec((TB, DV), lambda h, i: (i, gc0 + h)),
                         pl.BlockSpec((1, DV), lambda h, i: (0, h))],
               out_specs=[blk, blk, pl.BlockSpec((1, DV), lambda h, i: (0, h))],
               out_shape=[jax.ShapeDtypeStruct((L, RH * DV), F32), jax.ShapeDtypeStruct((L, RH * DV), BF16),
                          jax.ShapeDtypeStruct((1, RH * DV), F32)],
               compiler_params=_cp(("parallel", "arbitrary")))(dout, *o2, P, gn_g)


def cast_cols(cfg, name, src, col0, ncols, width):
    L = src.shape[0]
    TB = _rb(cfg)

    def body(s_ref, o_ref):
        o_ref[...] = s_ref[...].astype(o_ref.dtype)

    spec = pl.BlockSpec((TB, width), lambda i, j: (i, col0 + j))
    return _pc(body, name=name, grid=(L // TB, ncols), in_specs=[spec],
               out_specs=pl.BlockSpec((TB, width), lambda i, j: (i, j)),
               out_shape=jax.ShapeDtypeStruct((L, ncols * width), BF16),
               compiler_params=_cp(("parallel", "parallel")))(src)


_CPAD = 16


def _conv_windows(cfg):
    T, TC, TB = cfg.T, cfg.TC, cfg.TB
    assert TC % TB == 0 and T % TB == 0 and cfg.CK // 2 < _CPAD
    return T // TB, [(T + j * TB, T + _CPAD + j * TB) for j in range(TC // TB)]


def _fill_padded(cfg, pb, get):
    T, TC, TB = cfg.T, cfg.TC, cfg.TB
    z = jnp.zeros((_CPAD, LANE), F32)
    pb[0:_CPAD, :] = z
    pb[_CPAD + T:2 * _CPAD + T, :] = z
    pb[2 * _CPAD + T + TC:3 * _CPAD + T + TC, :] = z

    def fill(i, c):
        r0 = pl.multiple_of(i * TB, TB)
        pb[pl.ds(r0 + _CPAD, TB), :] = get(r0)
        return c

    lax.fori_loop(0, T // TB, fill, 0)
    for j in range(TC // TB):
        pb[2 * _CPAD + T + j * TB:2 * _CPAD + T + (j + 1) * TB, :] = get(T + j * TB)


def _taps(win, TB):
    W = TB + 2 * _CPAD
    rot = {0: win}

    def tap(k):
        a, b = divmod(k + 1, 8)
        if b not in rot:
            rot[b] = pltpu.roll(win, W - b, 0)
        return rot[b][8 * a:8 * a + TB, :]

    return tap


def glu_dwconv_fwd(cfg, name, P, w, b):
    L = P.shape[0]
    T, TC, TB, K = cfg.T, cfg.TC, cfg.TB, cfg.CK
    off = _offsets(cfg)
    ca, cb = off["la"] // LANE, off["lb"] // LANE
    nlat, ctx_tiles = _conv_windows(cfg)
    PBL = 3 * _CPAD + T + TC

    def body(a_ref, b_ref, w_ref, bias_ref, y_ref, pb):
        _fill_padded(cfg, pb, lambda r0: a_ref[pl.ds(r0, TB), :] * _sigmoid(b_ref[pl.ds(r0, TB), :]))
        wv = w_ref[...]
        bias = bias_ref[...]

        def tile(win):
            tap = _taps(win, TB)
            acc = jnp.zeros((TB, LANE), F32) + bias
            for k in range(K):
                acc = acc + wv[k:k + 1, :] * tap(k)
            return acc

        def lat(i, c):
            r0 = pl.multiple_of(i * TB, TB)
            y_ref[pl.ds(r0, TB), :] = tile(pb[pl.ds(r0, TB + 2 * _CPAD), :])
            return c

        lax.fori_loop(0, nlat, lat, 0)
        for r0, w0 in ctx_tiles:
            y_ref[r0:r0 + TB, :] = tile(pb[w0:w0 + TB + 2 * _CPAD, :])

    return _pc(body, name=name, grid=(cfg.CW // LANE,),
               in_specs=[pl.BlockSpec((L, LANE), lambda j: (0, ca + j)),
                         pl.BlockSpec((L, LANE), lambda j: (0, cb + j)),
                         pl.BlockSpec((32, LANE), lambda j: (0, j)),
                         pl.BlockSpec((1, LANE), lambda j: (0, j))],
               out_specs=pl.BlockSpec((L, LANE), lambda j: (0, j)),
               out_shape=jax.ShapeDtypeStruct((L, cfg.CW), F32),
               scratch_shapes=[pltpu.VMEM((PBL, LANE), F32)],
               compiler_params=_cp(("parallel",)))(P, P, w, b)


def glu_dwconv_bwd(cfg, name, P, w, dy):
    L = P.shape[0]
    T, TC, TB, K = cfg.T, cfg.TC, cfg.TB, cfg.CK
    off = _offsets(cfg)
    ca, cb = off["la"] // LANE, off["lb"] // LANE
    nlat, ctx_tiles = _conv_windows(cfg)
    PBL = 3 * _CPAD + T + TC

    def body(a_ref, b_ref, w_ref, dy_ref, da_ref, db_ref, dw_ref, dbias_ref, pbu, pbd):
        _fill_padded(cfg, pbu, lambda r0: a_ref[pl.ds(r0, TB), :] * _sigmoid(b_ref[pl.ds(r0, TB), :]))
        _fill_padded(cfg, pbd, lambda r0: dy_ref[pl.ds(r0, TB), :])
        wv = w_ref[...]
        dw_ref[...] = jnp.zeros_like(dw_ref)
        dbias_ref[...] = jnp.zeros_like(dbias_ref)

        def tile(r0, winu, wind):
            tapu = _taps(winu, TB)
            tapd = _taps(wind, TB)
            dyt = dy_ref[pl.ds(r0, TB), :]
            du = jnp.zeros((TB, LANE), F32)
            for k in range(K):
                du = du + wv[k:k + 1, :] * tapd(K - 1 - k)
                dw_ref[k:k + 1, :] += jnp.sum(dyt * tapu(k), axis=0, keepdims=True)
            dbias_ref[...] += jnp.sum(dyt, axis=0, keepdims=True)
            a = a_ref[pl.ds(r0, TB), :]
            sg = _sigmoid(b_ref[pl.ds(r0, TB), :])
            da_ref[pl.ds(r0, TB), :] = (du * sg).astype(da_ref.dtype)
            db_ref[pl.ds(r0, TB), :] = (du * a * sg * (1.0 - sg)).astype(db_ref.dtype)

        def lat(i, c):
            r0 = pl.multiple_of(i * TB, TB)
            tile(r0, pbu[pl.ds(r0, TB + 2 * _CPAD), :], pbd[pl.ds(r0, TB + 2 * _CPAD), :])
            return c

        lax.fori_loop(0, nlat, lat, 0)
        for r0, w0 in ctx_tiles:
            tile(r0, pbu[w0:w0 + TB + 2 * _CPAD, :], pbd[w0:w0 + TB + 2 * _CPAD, :])

    col = pl.BlockSpec((L, LANE), lambda j: (0, j))
    return _pc(body, name=name, grid=(cfg.CW // LANE,),
               in_specs=[pl.BlockSpec((L, LANE), lambda j: (0, ca + j)),
                         pl.BlockSpec((L, LANE), lambda j: (0, cb + j)),
                         pl.BlockSpec((32, LANE), lambda j: (0, j)), col],
               out_specs=[col, col, pl.BlockSpec((32, LANE), lambda j: (0, j)),
                          pl.BlockSpec((1, LANE), lambda j: (0, j))],
               out_shape=[jax.ShapeDtypeStruct((L, cfg.CW), BF16), jax.ShapeDtypeStruct((L, cfg.CW), BF16),
                          jax.ShapeDtypeStruct((32, cfg.CW), F32), jax.ShapeDtypeStruct((1, cfg.CW), F32)],
               scratch_shapes=[pltpu.VMEM((PBL, LANE), F32), pltpu.VMEM((PBL, LANE), F32)],
               compiler_params=_cp(("parallel",)))(P, P, w, dy)


def ln_silu_fwd(cfg, name, y, g, b):
    L, W = y.shape
    TB = cfg.TB

    def body(y_ref, g_ref, b_ref, o_ref):
        yv = y_ref[...]
        mu = jnp.mean(yv, axis=-1, keepdims=True)
        xc = yv - mu
        var = jnp.mean(xc * xc, axis=-1, keepdims=True)
        z = xc * lax.rsqrt(var + EPS) * g_ref[...] + b_ref[...]
        o_ref[...] = _silu(z).astype(o_ref.dtype)

    row = pl.BlockSpec((TB, W), lambda i: (i, 0))
    vec = pl.BlockSpec((1, W), lambda i: (0, 0))
    return _pc(body, name=name, grid=(L // TB,), in_specs=[row, vec, vec], out_specs=row,
               out_shape=jax.ShapeDtypeStruct((L, W), BF16), compiler_params=_cp(("parallel",)))(y, g, b)


def ln_silu_bwd(cfg, name, dact, y, g, b):
    L, W = y.shape
    TB = cfg.TB

    def body(d_ref, y_ref, g_ref, b_ref, dy_ref, dg_ref, db_ref):
        i = pl.program_id(0)
        yv = y_ref[...]
        mu = jnp.mean(yv, axis=-1, keepdims=True)
        xc = yv - mu
        var = jnp.mean(xc * xc, axis=-1, keepdims=True)
        r = lax.rsqrt(var + EPS)
        yh = xc * r
        g = g_ref[...]
        z = yh * g + b_ref[...]
        dz = d_ref[...] * _dsilu(z)
        sg = jnp.sum(dz * yh, axis=0, keepdims=True)
        sb = jnp.sum(dz, axis=0, keepdims=True)

        @pl.when(i == 0)
        def _():
            dg_ref[...] = sg
            db_ref[...] = sb

        @pl.when(i > 0)
        def _():
            dg_ref[...] += sg
            db_ref[...] += sb

        dh = dz * g
        dy_ref[...] = r * (dh - jnp.mean(dh, axis=-1, keepdims=True)
                           - yh * jnp.mean(dh * yh, axis=-1, keepdims=True))

    row = pl.BlockSpec((TB, W), lambda i: (i, 0))
    vec = pl.BlockSpec((1, W), lambda i: (0, 0))
    return _pc(body, name=name, grid=(L // TB,), in_specs=[row, row, vec, vec], out_specs=[row, vec, vec],
               out_shape=[jax.ShapeDtypeStruct((L, W), F32), jax.ShapeDtypeStruct((1, W), F32),
                          jax.ShapeDtypeStruct((1, W), F32)],
               compiler_params=_cp(("arbitrary",)))(dact, y, g, b)


def _na_geometry(cfg):
    R = cfg.T // cfg.GW
    nb = R // cfg.NAR
    assert nb >= 3 and cfg.GW == 64 and cfg.NAR == 8
    ks = [int(np.clip(8 * b - 4, 0, R - 16)) for b in range(nb)]
    return R, nb, ks


_NTAB = 18


def _split3(x):
    hi = x.astype(BF16)
    r = x - hi.astype(F32)
    mid = r.astype(BF16)
    lo = (r - mid.astype(F32)).astype(BF16)
    return hi, mid, lo


def _na_col_onehot(cfg):
    GW, NAC = cfg.GW, cfg.NAC
    qc = np.arange(GW)[:, None]
    kc = np.arange(GW)[None, :]
    cs = np.clip(qc - NAC // 2, 0, GW - NAC)
    vcol = (kc >= cs) & (kc < cs + NAC)
    dd = np.clip(kc - qc + NAC - 1, 0, 2 * NAC - 2)
    oh = (np.arange(LANE)[:, None, None] == dd[None]).astype(np.float32)
    z = np.zeros_like(oh)
    oda = np.concatenate([oh, z], axis=2).reshape(LANE, GW * LANE)
    odb = np.concatenate([z, oh], axis=2).reshape(LANE, GW * LANE)
    cm = np.where(np.concatenate([vcol, vcol], axis=1), 0.0, NEG).astype(np.float32).reshape(1, GW * LANE)
    return oda, odb, cm


def na_tables(cfg, name, rpb):
    NH, GW = cfg.NH, cfg.GW
    na = rpb.shape[1]
    oda, odb, cm = _na_col_onehot(cfg)
    rp = jnp.zeros((NH, _NTAB + 1, LANE), F32).at[:, 1:1 + na, :rpb.shape[2]].set(rpb.astype(F32))
    r0 = rp[:, :_NTAB].reshape(NH * _NTAB, LANE)
    r1 = rp[:, 1:].reshape(NH * _NTAB, LANE)
    a = np.arange(_NTAB) - 1
    rm0 = np.where((a >= 0) & (a < na), 0.0, NEG).astype(np.float32)
    rm1 = np.where((a + 1 >= 0) & (a + 1 < na), 0.0, NEG).astype(np.float32)
    half = (np.arange(GW * LANE) % LANE >= GW)[None, :]
    rmask = np.where(half, np.tile(rm1, NH)[:, None], np.tile(rm0, NH)[:, None]).astype(np.float32)
    tn = 2048
    rows = NH * _NTAB

    def body(r0_ref, r1_ref, a_ref, b_ref, cm_ref, rm_ref, o_ref):
        acc = cm_ref[...] + rm_ref[...]
        for t in _split3(r0_ref[...]):
            acc = acc + dot_nn(t, a_ref[...])
        for t in _split3(r1_ref[...]):
            acc = acc + dot_nn(t, b_ref[...])
        o_ref[...] = acc

    rs = pl.BlockSpec((rows, LANE), lambda n: (0, 0))
    out = _pc(body, name=name, grid=(GW * LANE // tn,),
              in_specs=[rs, rs, pl.BlockSpec((LANE, tn), lambda n: (0, n)), pl.BlockSpec((LANE, tn), lambda n: (0, n)),
                        pl.BlockSpec((1, tn), lambda n: (0, n)), pl.BlockSpec((rows, tn), lambda n: (0, n))],
              out_specs=pl.BlockSpec((rows, tn), lambda n: (0, n)),
              out_shape=jax.ShapeDtypeStruct((rows, GW * LANE), F32),
              compiler_params=_cp(("parallel",)))(r0, r1, jnp.asarray(oda, BF16), jnp.asarray(odb, BF16),
                                                  jnp.asarray(cm), jnp.asarray(rmask))
    return out.reshape(NH, _NTAB, GW, LANE)


def _na_tiles(cfg, b):
    R, nb, _ = _na_geometry(cfg)
    NAR = cfg.NAR
    ksb = jnp.clip(8 * b - 4, 0, R - 16)
    for i in range(8):
        qr = 8 * b + i
        ws = jnp.clip(qr - NAR // 2, 0, R - NAR)
        for J in range(8):
            kr0 = ksb + 2 * J
            row = jnp.clip(kr0 - qr + NAR - 1, -1, _NTAB - 2) + 1
            v0 = jnp.logical_and(kr0 >= ws, kr0 < ws + NAR)
            v1 = jnp.logical_and(kr0 + 1 >= ws, kr0 + 1 < ws + NAR)
            yield i, J, row, v0, v1


def _na_fill_bias(cfg, tab_ref, bias, b):
    GW = cfg.GW
    first = lax.broadcasted_iota(jnp.int32, (GW, LANE), 1) < GW
    for i, J, row, v0, v1 in _na_tiles(cfg, b):
        ok = jnp.where(first, v0.astype(jnp.int32), v1.astype(jnp.int32))
        bias[i * GW:(i + 1) * GW, J * LANE:(J + 1) * LANE] = jnp.where(ok > 0, tab_ref[row], NEG)


def _na_specs(cfg):
    R, nb, ks = _na_geometry(cfg)
    off = _offsets(cfg)
    TQ = 8 * cfg.GW
    KP = 4 * cfg.GW
    ks4 = [k // 4 for k in ks]
    lat_blocks = cfg.T // KP

    def ks4_of(b):
        return jnp.clip(2 * b - 1, 0, R // 4 - 4)

    assert all(int(np.clip(2 * b - 1, 0, R // 4 - 4)) == ks4[b] for b in range(nb))
    assert cfg.TC == KP

    def col(nm):
        c0 = off[nm] // LANE
        q = pl.BlockSpec((TQ, LANE), lambda h, b: (b, c0 + h))
        parts = [pl.BlockSpec((KP, LANE), functools.partial(lambda h, b, t: (ks4_of(b) + t, c0 + h), t=t))
                 for t in range(4)]
        ctx = pl.BlockSpec((KP, LANE), lambda h, b: (lat_blocks, c0 + h))
        return q, parts, ctx

    return nb, TQ, KP, ks4_of, col


def na_fwd(cfg, name, P, tab):
    nb, TQ, KP, ks4_of, col = _na_specs(cfg)
    NH = cfg.NH
    scale = cfg.NDH ** -0.5
    qs, _, _ = col("nq")
    _, kparts, kctx = col("nk")
    _, vparts, vctx = col("nv")

    def body(q_ref, k0, k1, k2, k3, kc_ref, v0, v1, v2, v3, vc_ref, tab_ref, o_ref, lse_ref, bias_ref):
        _na_fill_bias(cfg, tab_ref, bias_ref, pl.program_id(1))
        q = (q_ref[...] * scale).astype(BF16)
        kl = jnp.concatenate([k0[...], k1[...], k2[...], k3[...]], axis=0).astype(BF16)
        vl = jnp.concatenate([v0[...], v1[...], v2[...], v3[...]], axis=0).astype(BF16)
        kc = kc_ref[...].astype(BF16)
        vc = vc_ref[...].astype(BF16)
        sl = dot_nt(q, kl) + bias_ref[...]
        sc = dot_nt(q, kc)
        m = jnp.maximum(jnp.max(sl, axis=-1, keepdims=True), jnp.max(sc, axis=-1, keepdims=True))
        pl_ = jnp.exp(sl - m)
        pc = jnp.exp(sc - m)
        den = jnp.sum(pl_, axis=-1, keepdims=True) + jnp.sum(pc, axis=-1, keepdims=True)
        o = dot_nn(pl_.astype(BF16), vl) + dot_nn(pc.astype(BF16), vc)
        o_ref[...] = o / den
        lse_ref[...] = m + jnp.log(den)

    return _pc(body, name=name, grid=(NH, nb),
               in_specs=[qs, *kparts, kctx, *vparts, vctx,
                         pl.BlockSpec((None, _NTAB, cfg.GW, LANE), lambda h, b: (h, 0, 0, 0))],
               out_specs=[pl.BlockSpec((TQ, LANE), lambda h, b: (b, h)),
                          pl.BlockSpec((None, TQ, 1), lambda h, b: (h, b, 0))],
               out_shape=[jax.ShapeDtypeStruct((cfg.T, NH * LANE), F32),
                          jax.ShapeDtypeStruct((NH, cfg.T, 1), F32)],
               scratch_shapes=[pltpu.VMEM((TQ, 4 * KP), F32)],
               compiler_params=_cp(("parallel", "parallel")))(P, *([P] * 5), *([P] * 5), tab)


def na_bwd(cfg, name, P, tab, o, lse, dmix, dcol0):
    nb, TQ, KP, ks4_of, col = _na_specs(cfg)
    NH, GW = cfg.NH, cfg.GW
    L = P.shape[0]
    scale = cfg.NDH ** -0.5
    qs, _, _ = col("nq")
    _, kparts, kctx = col("nk")
    _, vparts, vctx = col("nv")

    def body(q_ref, k0, k1, k2, k3, kc_ref, v0, v1, v2, v3, vc_ref, tab_ref, o_ref, lse_ref, do_ref,
             dq_ref, dk_ref, dv_ref, dtab_ref, bias_ref):
        b = pl.program_id(1)

        @pl.when(b == 0)
        def _():
            dk_ref[...] = jnp.zeros_like(dk_ref)
            dv_ref[...] = jnp.zeros_like(dv_ref)
            dtab_ref[...] = jnp.zeros_like(dtab_ref)

        _na_fill_bias(cfg, tab_ref, bias_ref, b)

        q = (q_ref[...] * scale).astype(BF16)
        kl = jnp.concatenate([k0[...], k1[...], k2[...], k3[...]], axis=0).astype(BF16)
        vl = jnp.concatenate([v0[...], v1[...], v2[...], v3[...]], axis=0).astype(BF16)
        kc = kc_ref[...].astype(BF16)
        vc = vc_ref[...].astype(BF16)
        lse = lse_ref[...]
        do = do_ref[...]
        dob = do.astype(BF16)
        p_l = jnp.exp(dot_nt(q, kl) + bias_ref[...] - lse)
        p_c = jnp.exp(dot_nt(q, kc) - lse)
        delta = jnp.sum(do * o_ref[...], axis=-1, keepdims=True)
        ds_l = p_l * (dot_nt(dob, vl) - delta)
        ds_c = p_c * (dot_nt(dob, vc) - delta)
        dslb = ds_l.astype(BF16)
        dscb = ds_c.astype(BF16)
        dq_ref[...] = ((dot_nn(dslb, kl) + dot_nn(dscb, kc)) * scale).astype(dq_ref.dtype)
        r0 = pl.multiple_of(ks4_of(b) * KP, KP)
        dk_ref[pl.ds(r0, 4 * KP), :] += dot_tn(dslb, q)
        dv_ref[pl.ds(r0, 4 * KP), :] += dot_tn(p_l.astype(BF16), dob)
        dk_ref[cfg.T:cfg.T + KP, :] += dot_tn(dscb, q)
        dv_ref[cfg.T:cfg.T + KP, :] += dot_tn(p_c.astype(BF16), dob)
        bias_ref[...] = ds_l
        for i, J, row, _, _ in _na_tiles(cfg, b):
            dtab_ref[row] += bias_ref[i * GW:(i + 1) * GW, J * LANE:(J + 1) * LANE]

    full = pl.BlockSpec((L, LANE), lambda h, b: (0, h))
    tabs = pl.BlockSpec((None, _NTAB, GW, LANE), lambda h, b: (h, 0, 0, 0))
    return _pc(body, name=name, grid=(NH, nb),
               in_specs=[qs, *kparts, kctx, *vparts, vctx, tabs,
                         pl.BlockSpec((TQ, LANE), lambda h, b: (b, h)),
                         pl.BlockSpec((None, TQ, 1), lambda h, b: (h, b, 0)),
                         pl.BlockSpec((TQ, LANE), lambda h, b: (b, dcol0 + h))],
               out_specs=[pl.BlockSpec((TQ, LANE), lambda h, b: (b, h)), full, full, tabs],
               out_shape=[jax.ShapeDtypeStruct((cfg.T, NH * LANE), BF16),
                          jax.ShapeDtypeStruct((L, NH * LANE), F32), jax.ShapeDtypeStruct((L, NH * LANE), F32),
                          jax.ShapeDtypeStruct((NH, _NTAB, GW, LANE), F32)],
               scratch_shapes=[pltpu.VMEM((TQ, 4 * KP), F32)],
               compiler_params=_cp(("parallel", "arbitrary")))(
                   P, *([P] * 5), *([P] * 5), tab, o, lse, dmix)


def na_ctx_fwd(cfg, name, P):
    off = _offsets(cfg)
    TC, NH = cfg.TC, cfg.NH
    rb = cfg.T // TC
    scale = cfg.NDH ** -0.5

    def body(q_ref, k_ref, v_ref, o_ref, lse_ref):
        q = (q_ref[...] * scale).astype(BF16)
        s = dot_nt(q, k_ref[...].astype(BF16))
        m = jnp.max(s, axis=-1, keepdims=True)
        p = jnp.exp(s - m)
        den = jnp.sum(p, axis=-1, keepdims=True)
        o_ref[...] = dot_nn(p.astype(BF16), v_ref[...].astype(BF16)) / den
        lse_ref[...] = m + jnp.log(den)

    spec = lambda nm: pl.BlockSpec((TC, LANE), functools.partial(lambda h, c0: (rb, c0 + h), c0=off[nm] // LANE))
    return _pc(body, name=name, grid=(NH,), in_specs=[spec("nq"), spec("nk"), spec("nv")],
               out_specs=[pl.BlockSpec((TC, LANE), lambda h: (0, h)), pl.BlockSpec((None, TC, 1), lambda h: (h, 0, 0))],
               out_shape=[jax.ShapeDtypeStruct((TC, NH * LANE), F32), jax.ShapeDtypeStruct((NH, TC, 1), F32)],
               compiler_params=_cp(("parallel",)))(P, P, P)


def na_ctx_bwd(cfg, name, P, o, lse, dmix, dcol0, dk_in, dv_in):
    off = _offsets(cfg)
    TC, NH = cfg.TC, cfg.NH
    rb = cfg.T // TC
    scale = cfg.NDH ** -0.5

    def body(q_ref, k_ref, v_ref, o_ref, lse_ref, do_ref, dki_ref, dvi_ref, dq_ref, dk_ref, dv_ref):
        q = (q_ref[...] * scale).astype(BF16)
        kb = k_ref[...].astype(BF16)
        vb = v_ref[...].astype(BF16)
        do = do_ref[...]
        dob = do.astype(BF16)
        p = jnp.exp(dot_nt(q, kb) - lse_ref[...])
        delta = jnp.sum(do * o_ref[...], axis=-1, keepdims=True)
        ds = (p * (dot_nt(dob, vb) - delta)).astype(BF16)
        dq_ref[...] = (dot_nn(ds, kb) * scale).astype(dq_ref.dtype)
        dk_ref[...] = (dki_ref[...] + dot_tn(ds, q)).astype(dk_ref.dtype)
        dv_ref[...] = (dvi_ref[...] + dot_tn(p.astype(BF16), dob)).astype(dv_ref.dtype)

    spec = lambda nm: pl.BlockSpec((TC, LANE), functools.partial(lambda h, c0: (rb, c0 + h), c0=off[nm] // LANE))
    hb = pl.BlockSpec((TC, LANE), lambda h: (0, h))
    ctxrow = pl.BlockSpec((TC, LANE), lambda h: (rb, h))
    shp = jax.ShapeDtypeStruct((TC, NH * LANE), BF16)
    return _pc(body, name=name, grid=(NH,),
               in_specs=[spec("nq"), spec("nk"), spec("nv"), hb, pl.BlockSpec((None, TC, 1), lambda h: (h, 0, 0)),
                         pl.BlockSpec((TC, LANE), lambda h: (rb, dcol0 + h)), ctxrow, ctxrow],
               out_specs=[hb, hb, hb], out_shape=[shp, shp, shp],
               compiler_params=_cp(("parallel",)))(P, P, P, o, lse, dmix, dk_in, dv_in)


def na_rpb_grad(cfg, name, dtab):
    NH, GW = cfg.NH, cfg.GW
    na, nd = 2 * cfg.NAR - 1, 2 * cfg.NAC - 1
    oda, odb, _ = _na_col_onehot(cfg)
    E = np.concatenate([oda.T, odb.T], axis=1)
    rows = NH * _NTAB

    def body(z_ref, e_ref, o_ref):
        zv = z_ref[...]
        hi = zv.astype(BF16)
        lo = (zv - hi.astype(F32)).astype(BF16)
        e = e_ref[...]
        o_ref[...] = dot_nn(hi, e) + dot_nn(lo, e)

    g = _pc(body, name=name, out_shape=jax.ShapeDtypeStruct((rows, 2 * LANE), F32),
            compiler_params=_cp())(dtab.reshape(rows, GW * LANE), jnp.asarray(E, BF16))
    g = g.reshape(NH, _NTAB, 2, LANE)
    return g[:, 1:1 + na, 0, :nd] + g[:, 0:na, 1, :nd]


def _seq_tiles(cfg):
    T, TC, TB = cfg.T, cfg.TC, cfg.TB
    tiles = []
    for i in range((T + TC) // TB):
        r0 = i * TB
        tiles.append((r0, r0 == 0 or r0 == T, r0 + TB == T or r0 + TB == T + TC))
    return tiles


def _shift3(ref_get, r0, TB, start, end, width):
    cur = ref_get(r0, TB)
    if start or end:
        rowi = lax.broadcasted_iota(jnp.int32, (TB, width), 0)
    up = jnp.where(rowi == 0, 0.0, pltpu.roll(cur, 1, 0)) if start else ref_get(r0 - 1, TB)
    dn = jnp.where(rowi == TB - 1, 0.0, pltpu.roll(cur, TB - 1, 0)) if end else ref_get(r0 + 1, TB)
    return up, cur, dn


def ffn_act_fwd(cfg, name, U2, w, b):
    _, L, DFF = U2.shape
    TB = cfg.TB
    tiles = _seq_tiles(cfg)

    def body(u_ref, w_ref, b_ref, a_ref):
        def plane(p, r0, st, en):
            up, cur, dn = _shift3(lambda r, n: u_ref[p, r:r + n, :], r0, TB, st, en, LANE)
            wv = w_ref[p]
            return wv[0:1, :] * up + wv[1:2, :] * cur + wv[2:3, :] * dn + b_ref[p]

        for r0, st, en in tiles:
            val = plane(0, r0, st, en)
            gate = plane(1, r0, st, en)
            a_ref[r0:r0 + TB, :] = (_silu(gate) * val).astype(a_ref.dtype)

    return _pc(body, name=name, grid=(DFF // LANE,),
               in_specs=[pl.BlockSpec((2, L, LANE), lambda j: (0, 0, j)),
                         pl.BlockSpec((2, 8, LANE), lambda j: (0, 0, j)),
                         pl.BlockSpec((2, 1, LANE), lambda j: (0, 0, j))],
               out_specs=pl.BlockSpec((L, LANE), lambda j: (0, j)),
               out_shape=jax.ShapeDtypeStruct((L, DFF), BF16),
               compiler_params=_cp(("parallel",)))(U2, w, b)


def ffn_act_bwd(cfg, name, U2, w, b, dA):
    _, L, DFF = U2.shape
    TB = cfg.TB
    tiles = _seq_tiles(cfg)

    def body(u_ref, w_ref, b_ref, da_ref, du_ref, dw_ref, db_ref, dbuf):
        dw_ref[...] = jnp.zeros_like(dw_ref)
        db_ref[...] = jnp.zeros_like(db_ref)
        for r0, st, en in tiles:
            shifted = []
            pre = []
            for p in range(2):
                up, cur, dn = _shift3(lambda r, n: u_ref[p, r:r + n, :], r0, TB, st, en, LANE)
                wv = w_ref[p]
                shifted.append((up, cur, dn))
                pre.append(wv[0:1, :] * up + wv[1:2, :] * cur + wv[2:3, :] * dn + b_ref[p])
            val, gate = pre
            da = da_ref[r0:r0 + TB, :]
            dpre = (da * _silu(gate), da * val * _dsilu(gate))
            for p in range(2):
                dbuf[p, r0:r0 + TB, :] = dpre[p]
                for k in range(3):
                    dw_ref[p, k:k + 1, :] += jnp.sum(dpre[p] * shifted[p][k], axis=0, keepdims=True)
                db_ref[p] += jnp.sum(dpre[p], axis=0, keepdims=True)
        for r0, st, en in tiles:
            for p in range(2):
                up, cur, dn = _shift3(lambda r, n: dbuf[p, r:r + n, :], r0, TB, st, en, LANE)
                wv = w_ref[p]
                du_ref[p, r0:r0 + TB, :] = (wv[0:1, :] * dn + wv[1:2, :] * cur + wv[2:3, :] * up).astype(du_ref.dtype)

    blk = pl.BlockSpec((2, L, LANE), lambda j: (0, 0, j))
    wspec = pl.BlockSpec((2, 8, LANE), lambda j: (0, 0, j))
    bspec = pl.BlockSpec((2, 1, LANE), lambda j: (0, 0, j))
    return _pc(body, name=name, grid=(DFF // LANE,),
               in_specs=[blk, wspec, bspec, pl.BlockSpec((L, LANE), lambda j: (0, j))],
               out_specs=[blk, wspec, bspec],
               out_shape=[jax.ShapeDtypeStruct((2, L, DFF), BF16), jax.ShapeDtypeStruct((2, 8, DFF), F32),
                          jax.ShapeDtypeStruct((2, 1, DFF), F32)],
               scratch_shapes=[pltpu.VMEM((2, L, LANE), F32)],
               compiler_params=_cp(("parallel",)))(U2, w, b, dA)


def _tm(L, parts):
    assert L % parts == 0
    return L // parts


def layer_fwd(cfg, l, entry, mod, wts, small, tabs):
    fused = isinstance(entry, tuple)
    L, D = entry[0].shape if fused else entry.shape
    off = _offsets(cfg)
    DIN = off["end"]
    tmA = _tm(L, 4)
    sv = {"W": {}}

    def weight(name, after):
        sv["W"][name], tok = wts(name, after)
        return sv["W"][name], tok

    Win4, _ = weight("w_in", entry[1] if fused else entry)
    nbi = Win4.shape[3]
    if fused:
        XS, h1 = resid_norm_fwd(cfg, f"resid2_norm1_fwd_{l}", *entry, small["norm1_g"], mod["sc1"], mod["sh1"])
    else:
        XS, h1 = entry, norm_mod_fwd(cfg, f"norm1_fwd_{l}", entry, small["norm1_g"], mod["sc1"], mod["sh1"])
    sv["XS"] = XS
    P = matmul(f"mm_in_{l}", h1, Win4, contract="nn", grid=(4, L // tmA),
               a_spec=pl.BlockSpec((tmA, D), lambda n, m: (m, 0)),
               b_spec=pl.BlockSpec((None, None, D, nbi), lambda n, m: (n, l, 0, 0)),
               out_shape=jax.ShapeDtypeStruct((L, DIN), F32),
               out_spec=pl.BlockSpec((tmA, nbi), lambda n, m: (m, n)), nk=1)
    qr, kr = rope_fwd(cfg, f"rope_fwd_{l}", P, tabs["cos"], tabs["sin"])
    o_f, o_b, st = retention_fwd(cfg, f"ret_fwd_{l}", qr, kr, P, small["lam"])
    o2 = (o_f, o_b)
    ret = ggn_fwd(cfg, f"ggn_fwd_{l}", o2, P, small["ret_gn_g"])
    ycv = glu_dwconv_fwd(cfg, f"dwconv_fwd_{l}", P, small["conv_dw_w"], small["conv_dw_b"])
    act = ln_silu_fwd(cfg, f"ln_silu_fwd_{l}", ycv, small["conv_ln_g"], small["conv_ln_b"])
    Wpw4, _ = weight("conv_pw", act)
    cv = mm_rowsharded(f"mm_pw_{l}", act, Wpw4, l, BF16, cfg.CW)
    bias = na_tables(cfg, f"na_tables_{l}", small["na_rpb"])
    na_l, lse = na_fwd(cfg, f"na_fwd_{l}", P, bias)
    na_c, lse_c = na_ctx_fwd(cfg, f"na_ctx_fwd_{l}", P)
    mix = (ret, cv, jnp.concatenate([na_l, na_c], axis=0).astype(BF16))
    Wout4, tok = weight("w_out", ret)
    Y1 = mm_rowsharded(f"mm_out_{l}", mix, Wout4, l, F32, D)
    XM, h2 = resid_norm_fwd(cfg, f"resid1_norm2_fwd_{l}", XS, Y1, mod["g1"] if tok is None else mod["g1"] + tok,
                            small["norm2_g"], mod["sc2"], mod["sh2"])
    Wup4, _ = weight("ffn_up", h2)
    nbu = Wup4.shape[3]
    tnu = nbu // 2
    U2 = matmul(f"mm_up_{l}", h2, Wup4, contract="nn", grid=(8, L // tmA),
                a_spec=pl.BlockSpec((tmA, D), lambda n, m: (m, 0)),
                b_spec=pl.BlockSpec((None, None, D, tnu), lambda n, m: (n // 2, l, 0, n % 2)),
                out_shape=jax.ShapeDtypeStruct((2, L, cfg.DFF), F32),
                out_spec=pl.BlockSpec((None, tmA, tnu), lambda n, m: (n // 4, m, n % 4)), nk=1)
    A = ffn_act_fwd(cfg, f"ffn_act_fwd_{l}", U2, small["ffn_dw_w"], small["ffn_dw_b"])
    Wdn4, _ = weight("ffn_down", A)
    Y2 = mm_rowsharded(f"mm_down_{l}", A, Wdn4, l, F32, D // 2)
    sv.update(h1=h1, P=P, qr=qr, kr=kr, o2=o2, st=st, ycv=ycv, act=act, bias=bias, na_l=na_l, lse=lse,
              na_c=na_c, lse_c=lse_c, mix=mix, Y1=Y1, XM=XM, h2=h2, U2=U2, A=A, Y2=Y2)
    return (XM, Y2, mod["g2"]), sv


GRAD_GROUPS = (("ffn_down", "ffn_up"), ("w_out", "conv_pw", "w_in"))


def layer_bwd(cfg, l, dXO, sv, mod, wts, small, tabs, gbuf, ready, late=lambda gb, after: None):
    L, D = dXO.shape
    off = _offsets(cfg)
    DIN = off["end"]
    Win4, Wout4, Wup4, Wdn4, Wpw4 = wts["w_in"], wts["w_out"], wts["ffn_up"], wts["ffn_down"], wts["conv_pw"]
    tmA, tmB = _tm(L, 4), _tm(L, 8)
    depth = Win4.shape[1]
    gb, gs, dm = {}, {}, {}
    P = sv["P"]
    dY2, dm["g2"] = resid_bwd(cfg, f"resid2_bwd_{l}", dXO, sv["Y2"], mod["g2"])
    nbd = Wdn4.shape[2]
    dA = matmul(f"mm_down_da_{l}", dY2, Wdn4, contract="nt", grid=(4, L // tmA),
                a_spec=pl.BlockSpec((tmA, D), lambda j, m: (m, 0)),
                b_spec=pl.BlockSpec((None, None, nbd, D), lambda j, m: (j, l, 0, 0)),
                out_shape=jax.ShapeDtypeStruct((L, cfg.DFF), F32),
                out_spec=pl.BlockSpec((tmA, nbd), lambda j, m: (m, j)), nk=1)
    gb["ffn_down"] = wgrad(cfg, f"mm_down_dw_{l}", sv["A"], dY2,
                           lambda rb, ri: pl.BlockSpec((rb, nbd), lambda j, m: (ri(m), j)),
                           lambda rb, ri: pl.BlockSpec((rb, D), lambda j, m: (ri(m), 0)),
                           jax.ShapeDtypeStruct((depth, 4, nbd, D), BF16),
                           pl.BlockSpec((None, None, nbd, D), lambda j, m: (l, j, 0, 0)), 4, gbuf.get("ffn_down"))
    dU2, dfw, dfb = ffn_act_bwd(cfg, f"ffn_act_bwd_{l}", sv["U2"], small["ffn_dw_w"], small["ffn_dw_b"], dA)
    gs["ffn_dw_w"], gs["ffn_dw_b"] = dfw, dfb
    nbu = Wup4.shape[3]
    tnu = nbu // 2
    dH2 = matmul(f"mm_up_dh_{l}", dU2, Wup4, contract="nt", grid=(L // tmA, 8),
                 a_spec=pl.BlockSpec((None, tmA, tnu), lambda m, n: (n // 4, m, n % 4)),
                 b_spec=pl.BlockSpec((None, None, D, tnu), lambda m, n: (n // 2, l, 0, n % 2)),
                 out_shape=jax.ShapeDtypeStruct((L, D), F32),
                 out_spec=pl.BlockSpec((tmA, D), lambda m, n: (m, 0)), nk=8)
    gb["ffn_up"] = wgrad(cfg, f"mm_up_dw_{l}", sv["h2"], dU2,
                         lambda rb, ri: pl.BlockSpec((rb, D), lambda n, m: (ri(m), 0)),
                         lambda rb, ri: pl.BlockSpec((None, rb, tnu), lambda n, m: (n // 4, ri(m), n % 4)),
                         jax.ShapeDtypeStruct((depth, 4, D, nbu), BF16),
                         pl.BlockSpec((None, None, D, tnu), lambda n, m: (l, n // 2, 0, n % 2)), 8, gbuf.get("ffn_up"))
    dXM, dm["sc2"], dm["sh2"], gs["norm2_g"] = norm_mod_bwd(
        cfg, f"norm2_bwd_{l}", dH2, sv["XM"], small["norm2_g"], mod["sc2"], dXO)
    tok = ready(GRAD_GROUPS[0], gb)
    dY1, dm["g1"] = resid_bwd(cfg, f"resid1_bwd_{l}", dXM, sv["Y1"], mod["g1"] if tok is None else mod["g1"] + tok)
    nbo = Wout4.shape[2]
    dmix = matmul(f"mm_out_dmix_{l}", dY1, Wout4, contract="nt", grid=(4, L // tmA),
                  a_spec=pl.BlockSpec((tmA, D), lambda j, m: (m, 0)),
                  b_spec=pl.BlockSpec((None, None, nbo, D), lambda j, m: (j, l, 0, 0)),
                  out_shape=jax.ShapeDtypeStruct((L, D), F32),
                  out_spec=pl.BlockSpec((tmA, nbo), lambda j, m: (m, j)), nk=1)
    gw, j0 = gbuf.get("w_out"), 0
    for pi, piece in enumerate(sv["mix"]):
        nblk = piece.shape[1] // nbo
        gw = wgrad(cfg, f"mm_out_dw_{l}_{pi}", piece, dY1,
                   lambda rb, ri: pl.BlockSpec((rb, nbo), lambda j, m: (ri(m), j)),
                   lambda rb, ri: pl.BlockSpec((rb, D), lambda j, m: (ri(m), 0)),
                   jax.ShapeDtypeStruct((depth, 4, nbo, D), BF16),
                   pl.BlockSpec((None, None, nbo, D), functools.partial(lambda j, m, j0: (l, j + j0, 0, 0), j0=j0)),
                   nblk, gw)
        j0 += nblk
    gb["w_out"] = gw
    RW = cfg.RH * cfg.RDV
    tok = late(gb, dmix)
    do, dlg, gs["ret_gn_g"] = ggn_bwd(cfg, f"ggn_bwd_{l}", dmix, sv["o2"], P,
                                      small["ret_gn_g"] if tok is None else small["ret_gn_g"] + tok, 0)
    dqf, dqb, dkf, dkb, dvf, dvb, dlam = retention_bwd(
        cfg, f"ret_bwd_{l}", sv["qr"], sv["kr"], P, small["lam"], sv["st"], do)
    gs["lam"] = dlam[:, :, 0, 0]
    dlq, dlk = rope_bwd(cfg, f"rope_bwd_{l}", (dqf, dqb), (dkf, dkb), tabs["cos"], tabs["sin"])
    dlv = add_cast(cfg, f"ret_dv_{l}", dvf, dvb)
    dcv = cast_cols(cfg, f"conv_dcv_{l}", dmix, RW // LANE, cfg.CW // LANE, LANE)
    nbp = Wpw4.shape[2]
    dact = matmul(f"mm_pw_dact_{l}", dcv, Wpw4, contract="nt", grid=(4, L // tmA),
                  a_spec=pl.BlockSpec((tmA, cfg.CW), lambda j, m: (m, 0)),
                  b_spec=pl.BlockSpec((None, None, nbp, cfg.CW), lambda j, m: (j, l, 0, 0)),
                  out_shape=jax.ShapeDtypeStruct((L, cfg.CW), F32),
                  out_spec=pl.BlockSpec((tmA, nbp), lambda j, m: (m, j)), nk=1)
    gb["conv_pw"] = wgrad(cfg, f"mm_pw_dw_{l}", sv["act"], dcv,
                          lambda rb, ri: pl.BlockSpec((rb, nbp), lambda j, m: (ri(m), j)),
                          lambda rb, ri: pl.BlockSpec((rb, cfg.CW), lambda j, m: (ri(m), 0)),
                          jax.ShapeDtypeStruct((depth, 4, nbp, cfg.CW), BF16),
                          pl.BlockSpec((None, None, nbp, cfg.CW), lambda j, m: (l, j, 0, 0)), 4, gbuf.get("conv_pw"))
    dycv, gs["conv_ln_g"], gs["conv_ln_b"] = ln_silu_bwd(
        cfg, f"ln_silu_bwd_{l}", dact, sv["ycv"], small["conv_ln_g"], small["conv_ln_b"])
    dla, dlb, gs["conv_dw_w"], gs["conv_dw_b"] = glu_dwconv_bwd(cfg, f"dwconv_bwd_{l}", P, small["conv_dw_w"], dycv)
    nac0 = (RW + cfg.CW) // LANE
    dnq_l, dnk, dnv, dsb = na_bwd(cfg, f"na_bwd_{l}", P, sv["bias"], sv["na_l"], sv["lse"], dmix, nac0)
    dnq_c, dnk_c, dnv_c = na_ctx_bwd(cfg, f"na_ctx_bwd_{l}", P, sv["na_c"], sv["lse_c"], dmix, nac0, dnk, dnv)
    gs["na_rpb"] = na_rpb_grad(cfg, f"na_rpb_{l}", dsb)
    dnq = jnp.concatenate([dnq_l, dnq_c], axis=0)
    dnk = jnp.concatenate([dnk[:cfg.T].astype(BF16), dnk_c], axis=0)
    dnv = jnp.concatenate([dnv[:cfg.T].astype(BF16), dnv_c], axis=0)
    dP = jnp.concatenate([dlq, dlk, dlv, dlg, dla, dlb, dnq, dnk, dnv], axis=1)
    nbi = Win4.shape[3]
    dH1 = matmul(f"mm_in_dh_{l}", dP, Win4, contract="nt", grid=(L // tmA, 4),
                 a_spec=pl.BlockSpec((tmA, nbi), lambda m, n: (m, n)),
                 b_spec=pl.BlockSpec((None, None, D, nbi), lambda m, n: (n, l, 0, 0)),
                 out_shape=jax.ShapeDtypeStruct((L, D), F32),
                 out_spec=pl.BlockSpec((tmA, D), lambda m, n: (m, 0)), nk=4)
    gb["w_in"] = wgrad(cfg, f"mm_in_dw_{l}", sv["h1"], dP,
                       lambda rb, ri: pl.BlockSpec((rb, D), lambda n, m: (ri(m), 0)),
                       lambda rb, ri: pl.BlockSpec((rb, nbi), lambda n, m: (ri(m), n)),
                       jax.ShapeDtypeStruct((depth, 4, D, nbi), BF16),
                       pl.BlockSpec((None, None, D, nbi), lambda n, m: (l, n, 0, 0)), 4, gbuf.get("w_in"))
    dXS, dm["sc1"], dm["sh1"], gs["norm1_g"] = norm_mod_bwd(
        cfg, f"norm1_bwd_{l}", dH1, sv["XS"], small["norm1_g"], mod["sc1"], dXM, latent_only=(l == 0))
    return dXS, gb, gs, dm, ready(GRAD_GROUPS[1], gb)


def _layer_small(cfg, l, sp):
    DFF = cfg.DFF
    fw = sp["ffn_dw_w"][l].reshape(3, 2, DFF).transpose(1, 0, 2)
    fw = jnp.concatenate([fw, jnp.zeros((2, 5, DFF), F32)], axis=1)
    cw = jnp.concatenate([sp["conv_dw_w"][l], jnp.zeros((32 - cfg.CK, cfg.CW), F32)], axis=0)
    return dict(
        norm1_g=sp["norm1_g"][l][None], norm2_g=sp["norm2_g"][l][None],
        lam=jax.nn.log_sigmoid(sp["ret_decay"][l]), ret_gn_g=sp["ret_gn_g"][l][None],
        conv_dw_w=cw, conv_dw_b=sp["conv_dw_b"][l][None], conv_ln_g=sp["conv_ln_g"][l][None],
        conv_ln_b=sp["conv_ln_b"][l][None], na_rpb=sp["na_rpb"][l],
        ffn_dw_w=fw, ffn_dw_b=sp["ffn_dw_b"][l].reshape(2, 1, DFF))


def local_step(cfg, x, ctx, tgt, mods, wts, sp, grads_ready=lambda l, names, gb: None,
               grads_late=lambda l, gb, after: None):
    depth = sp["norm1_g"].shape[0]
    cos, sin = rope_tables(cfg)
    tabs = dict(cos=cos, sin=sin)
    XS = jnp.concatenate([x, ctx], axis=0)
    smalls = [_layer_small(cfg, l, sp) for l in range(depth)]
    saves = []
    for l in range(depth):
        XS, sv = layer_fwd(cfg, l, XS, mods[l], functools.partial(wts, l), smalls[l], tabs)
        saves.append(sv)
    xm, y2, g2 = XS
    ls, dX, dfg = final_loss(cfg, "final_loss", xm, y2, g2[0], sp["final_g"][None], tgt)
    gb, gss, dms = {}, [None] * depth, [None] * depth
    token = None
    for l in reversed(range(depth)):
        mod = mods[l] if token is None else {**mods[l], "g2": mods[l]["g2"] + token}
        dX, gb, gss[l], dms[l], token = layer_bwd(cfg, l, dX, saves[l], mod, saves[l]["W"], smalls[l], tabs, gb,
                                                  functools.partial(grads_ready, l), functools.partial(grads_late, l))
    return ls[0, 0], dX[:cfg.T], gb, gss, dms, dfg[0], token


MESH = pl.DeviceIdType.MESH
N_DEV = 8
N_CHIP = 4
BIG = ("w_in", "w_out", "ffn_up", "ffn_down", "conv_pw")
_ANY = pl.BlockSpec(memory_space=pl.ANY)


def _place():
    x, y, c = lax.axis_index("x"), lax.axis_index("y"), lax.axis_index("c")
    chips = [(1 - x, y), (x, 1 - y), (1 - x, 1 - y)]
    return x, y, c, chips


def allgather8(name, xs):
    m_per, n = xs.shape

    def body(x_ref, out_ref, send_sems, recv_sems, local_sem):
        x, y, c, chips = _place()
        me, sibling = (x, y, c), (x, y, 1 - c)

        def rows(px, py, pc):
            return out_ref.at[pl.ds((4 * px + 2 * py + pc) * m_per, m_per), :]

        def copy(k, block, to, src=None):
            return pltpu.make_async_remote_copy(
                src_ref=rows(*block) if src is None else src, dst_ref=rows(*block),
                send_sem=send_sems.at[k], recv_sem=recv_sems.at[k], device_id=to, device_id_type=MESH)

        mine = pltpu.make_async_copy(x_ref, rows(*me), local_sem)
        mine.start()
        first = [copy(0, me, sibling, src=x_ref)]
        first += [copy(1 + j, me, (*chip, c), src=x_ref) for j, chip in enumerate(chips)]
        for cp in first:
            cp.start()
        passed = [copy(4 + j, (*chip, c), sibling) for j, chip in enumerate(chips)]
        for j, chip in enumerate(chips):
            copy(1 + j, (*chip, c), me).wait_recv()
            passed[j].start()
        copy(0, sibling, me).wait_recv()
        for j, chip in enumerate(chips):
            copy(4 + j, (*chip, 1 - c), me).wait_recv()
        for cp in first + passed:
            cp.wait_send()
        mine.wait()

    return _pc(body, name=name, out_shape=jax.ShapeDtypeStruct((N_DEV * m_per, n), xs.dtype),
               in_specs=[pl.BlockSpec(memory_space=pltpu.VMEM)], out_specs=pl.BlockSpec(memory_space=pltpu.VMEM),
               scratch_shapes=[pltpu.SemaphoreType.DMA((7,)), pltpu.SemaphoreType.DMA((7,)), pltpu.SemaphoreType.DMA],
               compiler_params=pltpu.CompilerParams(vmem_limit_bytes=VMEM_LIMIT))(xs)


def _wpiece(ref, layer, chip_idx, half):
    rh = ref.shape[2] // 2
    return ref.at[chip_idx, layer, pl.ds(half * rh, rh)]


def _wcopy(ref, layer, chip_idx, half, send_sems, recv_sems, k, to):
    piece = _wpiece(ref, layer, chip_idx, half)
    return pltpu.make_async_remote_copy(src_ref=piece, dst_ref=piece, send_sem=send_sems.at[k],
                                        recv_sem=recv_sems.at[k], device_id=to, device_id_type=MESH)


def _w_ici_sends(outs, layer, send_sems, recv_sems):
    x, y, c, chips = _place()
    return [_wcopy(outs[a], layer, 2 * x + y, c, send_sems, recv_sems, 3 * a + t, (*chip, c))
            for a in range(len(outs)) for t, chip in enumerate(chips)]


def _w_ici_landed(outs, layer, send_sems, recv_sems):
    x, y, c, chips = _place()
    return [_wcopy(outs[a], layer, 2 * chip[0] + chip[1], c, send_sems, recv_sems, 3 * a + t, (x, y, c))
            for a in range(len(outs)) for t, chip in enumerate(chips)]


def _w_forward(outs, layer, send_sems, recv_sems, base):
    x, y, c, chips = _place()
    n = len(outs)
    sends = [_wcopy(outs[a], layer, 2 * chip[0] + chip[1], c, send_sems, recv_sems, base + 3 * a + t, (x, y, 1 - c))
             for a in range(n) for t, chip in enumerate(chips)]
    recvs = [_wcopy(outs[a], layer, 2 * chip[0] + chip[1], 1 - c, send_sems, recv_sems, base + 3 * a + t, (x, y, c))
             for a in range(n) for t, chip in enumerate(chips)]
    return sends, recvs


def allgather_layer(name, bufs, layer):
    n = len(bufs)

    def body(*refs):
        outs = refs[n:2 * n]
        send_sems, recv_sems = refs[2 * n:]
        sent = _w_ici_sends(outs, layer, send_sems, recv_sems)
        for cp in sent:
            cp.start()
        fwd, from_sib = _w_forward(outs, layer, send_sems, recv_sems, 3 * n)
        for landed, fw in zip(_w_ici_landed(outs, layer, send_sems, recv_sems), fwd):
            landed.wait_recv()
            fw.start()
        for cp in from_sib:
            cp.wait_recv()
        for cp in sent + fwd:
            cp.wait_send()

    return _pc(body, name=name, out_shape=[jax.ShapeDtypeStruct(b.shape, b.dtype) for b in bufs],
               in_specs=[_ANY] * n, out_specs=[_ANY] * n, input_output_aliases={a: a for a in range(n)},
               scratch_shapes=[pltpu.SemaphoreType.DMA((6 * n,)), pltpu.SemaphoreType.DMA((6 * n,))])(*bufs)


_HBM = pl.BlockSpec(memory_space=pltpu.HBM)
_SEM = pl.BlockSpec(memory_space=pltpu.SEMAPHORE)
_EFFECT = pltpu.SideEffectType.DATAFLOW_SIDE_EFFECTING


def allgather_layer_start(name, bufs, layer, after):
    n = len(bufs)

    def body(*refs):
        send_sems, recv_sems = refs[n + 1:n + 3]
        outs = refs[n + 3:2 * n + 3]
        token = refs[2 * n + 3]
        for cp in _w_ici_sends(outs, layer, send_sems, recv_sems):
            cp.start()
        token[...] = jnp.zeros_like(token)

    res = _pc(body, name=name,
              out_shape=(pltpu.SemaphoreType.DMA((3 * n,)), pltpu.SemaphoreType.DMA((3 * n,)),
                         *[pltpu.HBM(b.shape, b.dtype) for b in bufs], jax.ShapeDtypeStruct((8, LANE), F32)),
              in_specs=[_HBM] * n + [_ANY],
              out_specs=(_SEM, _SEM, *([_HBM] * n), pl.BlockSpec(memory_space=pltpu.VMEM)),
              input_output_aliases={a: a + 2 for a in range(n)},
              compiler_params=pltpu.CompilerParams(has_side_effects=_EFFECT))(
                  *[pltpu.with_memory_space_constraint(b, pltpu.HBM) for b in bufs], after)
    return res[0], res[1], list(res[2:2 + n]), res[2 + n]


def allgather_layer_wait(name, bufs, send_sems, recv_sems, after, layer):
    n = len(bufs)

    def body(*refs):
        ins = refs[:n]
        send_sems, recv_sems = refs[n:n + 2]
        for cp in _w_ici_sends(ins, layer, send_sems, recv_sems):
            cp.wait_send()
        for cp in _w_ici_landed(ins, layer, send_sems, recv_sems):
            cp.wait_recv()

    return _pc(body, name=name, out_shape=tuple(pltpu.HBM(b.shape, b.dtype) for b in bufs),
               in_specs=[_HBM] * n + [_SEM, _SEM, _ANY], out_specs=tuple([_HBM] * n),
               input_output_aliases={a: a for a in range(n)},
               compiler_params=pltpu.CompilerParams(has_side_effects=_EFFECT))(*bufs, send_sems, recv_sems, after)


def forward_halves(name, bufs, layer):
    n = len(bufs)

    def body(*refs):
        outs = refs[n:2 * n]
        send_sems, recv_sems = refs[2 * n:]
        fwd, from_sib = _w_forward(outs, layer, send_sems, recv_sems, 0)
        for cp in fwd:
            cp.start()
        for cp in from_sib:
            cp.wait_recv()
        for cp in fwd:
            cp.wait_send()

    return _pc(body, name=name, out_shape=[jax.ShapeDtypeStruct(b.shape, b.dtype) for b in bufs],
               in_specs=[_ANY] * n, out_specs=[_ANY] * n, input_output_aliases={a: a for a in range(n)},
               scratch_shapes=[pltpu.SemaphoreType.DMA((3 * n,)), pltpu.SemaphoreType.DMA((3 * n,))])(*bufs)


def exchange_rows(name, grads, layer):
    n = len(grads)

    def body(*refs):
        ins, outs = refs[:n], refs[n:2 * n]
        send_sems, recv_sems = refs[2 * n:]
        x, y, c, _ = _place()
        cps = []
        for a in range(n):
            rh = ins[a].shape[2] // 2
            cps.append(pltpu.make_async_remote_copy(
                src_ref=ins[a].at[layer, pl.ds(0, N_CHIP), pl.ds((1 - c) * rh, rh)], dst_ref=outs[a],
                send_sem=send_sems.at[a], recv_sem=recv_sems.at[a], device_id=(x, y, 1 - c), device_id_type=MESH))
        for cp in cps:
            cp.start()
        for cp in cps:
            cp.wait()

    return _pc(body, name=name,
               out_shape=[jax.ShapeDtypeStruct((N_CHIP, g.shape[2] // 2, g.shape[3]), g.dtype) for g in grads],
               in_specs=[_ANY] * n, out_specs=[_ANY] * n,
               scratch_shapes=[pltpu.SemaphoreType.DMA((n,)), pltpu.SemaphoreType.DMA((n,))])(*grads)


def _exchange_copies(grads, lands, layer, send_sems, recv_sems):
    x, y, c, _ = _place()
    cps = []
    for a in range(len(grads)):
        rh = grads[a].shape[2] // 2
        cps.append(pltpu.make_async_remote_copy(
            src_ref=grads[a].at[layer, pl.ds(0, N_CHIP), pl.ds((1 - c) * rh, rh)], dst_ref=lands[a],
            send_sem=send_sems.at[a], recv_sem=recv_sems.at[a], device_id=(x, y, 1 - c), device_id_type=MESH))
    return cps


def exchange_rows_start(name, grads, layer):
    n = len(grads)
    lands = [lax.empty((N_CHIP, g.shape[2] // 2, g.shape[3]), g.dtype) for g in grads]

    def body(*refs):
        send_sems, recv_sems = refs[2 * n:2 * n + 2]
        g_out, l_out = refs[2 * n + 2:3 * n + 2], refs[3 * n + 2:4 * n + 2]
        token = refs[4 * n + 2]
        for cp in _exchange_copies(g_out, l_out, layer, send_sems, recv_sems):
            cp.start()
        token[...] = jnp.zeros_like(token)

    both = list(grads) + lands
    res = _pc(body, name=name,
              out_shape=(pltpu.SemaphoreType.DMA((n,)), pltpu.SemaphoreType.DMA((n,)),
                         *[pltpu.HBM(b.shape, b.dtype) for b in both], jax.ShapeDtypeStruct((8, LANE), F32)),
              in_specs=[_HBM] * (2 * n),
              out_specs=(_SEM, _SEM, *([_HBM] * (2 * n)), pl.BlockSpec(memory_space=pltpu.VMEM)),
              input_output_aliases={a: a + 2 for a in range(2 * n)},
              compiler_params=pltpu.CompilerParams(has_side_effects=_EFFECT))(
                  *[pltpu.with_memory_space_constraint(b, pltpu.HBM) for b in both])
    return res[0], res[1], list(res[2:2 + n]), list(res[2 + n:2 + 2 * n]), res[2 + 2 * n]


def exchange_rows_wait(name, grads, lands, send_sems, recv_sems, after, layer):
    n = len(grads)

    def body(*refs):
        g_in, l_in = refs[:n], refs[n:2 * n]
        send_sems, recv_sems = refs[2 * n:2 * n + 2]
        for cp in _exchange_copies(g_in, l_in, layer, send_sems, recv_sems):
            cp.wait_send()
            cp.wait_recv()

    both = list(grads) + list(lands)
    res = _pc(body, name=name, out_shape=tuple(pltpu.HBM(b.shape, b.dtype) for b in both),
              in_specs=[_HBM] * (2 * n) + [_SEM, _SEM, _ANY], out_specs=tuple([_HBM] * (2 * n)),
              input_output_aliases={a: a for a in range(2 * n)},
              compiler_params=pltpu.CompilerParams(has_side_effects=_EFFECT))(*both, send_sems, recv_sems, after)
    return list(res[:n]), list(res[n:])


def _scatter_sends(parts, lands, send_sems, recv_sems):
    x, y, c, chips = _place()
    return [pltpu.make_async_remote_copy(
        src_ref=parts[a].at[2 * chip[0] + chip[1]], dst_ref=lands[a].at[2 * x + y], send_sem=send_sems.at[3 * a + t],
        recv_sem=recv_sems.at[3 * a + t], device_id=(*chip, c), device_id_type=MESH)
        for a in range(len(parts)) for t, chip in enumerate(chips)]


def _scatter_landed(lands, send_sems, recv_sems):
    x, y, c, chips = _place()
    return [pltpu.make_async_remote_copy(
        src_ref=lands[a].at[2 * chip[0] + chip[1]], dst_ref=lands[a].at[2 * chip[0] + chip[1]],
        send_sem=send_sems.at[3 * a + t], recv_sem=recv_sems.at[3 * a + t], device_id=(x, y, c), device_id_type=MESH)
        for a in range(len(lands)) for t, chip in enumerate(chips)]


def scatter_slices(name, parts, lands):
    n = len(parts)

    def body(*refs):
        ins, outs = refs[:n], refs[2 * n:3 * n]
        send_sems, recv_sems = refs[3 * n:]
        cps = _scatter_sends(ins, outs, send_sems, recv_sems)
        for cp in cps:
            cp.start()
        for cp in _scatter_landed(outs, send_sems, recv_sems):
            cp.wait_recv()
        for cp in cps:
            cp.wait_send()

    return _pc(body, name=name, out_shape=[jax.ShapeDtypeStruct(p.shape, p.dtype) for p in lands],
               in_specs=[_ANY] * (2 * n), out_specs=[_ANY] * n,
               input_output_aliases={n + a: a for a in range(n)},
               scratch_shapes=[pltpu.SemaphoreType.DMA((3 * n,)), pltpu.SemaphoreType.DMA((3 * n,))])(*parts, *lands)


def scatter_slices_start(name, parts, lands):
    n = len(parts)

    def body(*refs):
        send_sems, recv_sems = refs[2 * n:2 * n + 2]
        p_out, l_out = refs[2 * n + 2:3 * n + 2], refs[3 * n + 2:4 * n + 2]
        token = refs[4 * n + 2]
        for cp in _scatter_sends(p_out, l_out, send_sems, recv_sems):
            cp.start()
        token[...] = jnp.zeros_like(token)

    both = list(parts) + list(lands)
    res = _pc(body, name=name,
              out_shape=(pltpu.SemaphoreType.DMA((3 * n,)), pltpu.SemaphoreType.DMA((3 * n,)),
                         *[pltpu.HBM(b.shape, b.dtype) for b in both], jax.ShapeDtypeStruct((8, LANE), F32)),
              in_specs=[_HBM] * (2 * n),
              out_specs=(_SEM, _SEM, *([_HBM] * (2 * n)), pl.BlockSpec(memory_space=pltpu.VMEM)),
              input_output_aliases={a: a + 2 for a in range(2 * n)},
              compiler_params=pltpu.CompilerParams(has_side_effects=_EFFECT))(
                  *[pltpu.with_memory_space_constraint(b, pltpu.HBM) for b in both])
    return res[0], res[1], list(res[2:2 + n]), list(res[2 + n:2 + 2 * n]), res[2 + 2 * n]


def scatter_slices_wait(name, parts, lands, send_sems, recv_sems, after):
    n = len(parts)

    def body(*refs):
        p_in, l_in = refs[:n], refs[n:2 * n]
        send_sems, recv_sems = refs[2 * n:2 * n + 2]
        for cp in _scatter_sends(p_in, l_in, send_sems, recv_sems):
            cp.wait_send()
        for cp in _scatter_landed(l_in, send_sems, recv_sems):
            cp.wait_recv()

    both = list(parts) + list(lands)
    res = _pc(body, name=name, out_shape=tuple(pltpu.HBM(b.shape, b.dtype) for b in both),
              in_specs=[_HBM] * (2 * n) + [_SEM, _SEM, _ANY], out_specs=tuple([_HBM] * (2 * n)),
              input_output_aliases={a: a for a in range(2 * n)},
              compiler_params=pltpu.CompilerParams(has_side_effects=_EFFECT))(*both, send_sems, recv_sems, after)
    return list(res[n:])


def share_rows(name, bufs):
    n = len(bufs)

    def body(*refs):
        outs = refs[n:2 * n]
        send_sems, recv_sems = refs[2 * n:]
        x, y, c, _ = _place()

        def half(a, h):
            return outs[a].at[pl.ds(0, 2), h]

        cps = [pltpu.make_async_remote_copy(
            src_ref=half(a, c), dst_ref=half(a, c), send_sem=send_sems.at[a], recv_sem=recv_sems.at[a],
            device_id=(x, y, 1 - c), device_id_type=MESH) for a in range(n)]
        for cp in cps:
            cp.start()
        for a in range(n):
            pltpu.make_async_remote_copy(
                src_ref=half(a, 1 - c), dst_ref=half(a, 1 - c), send_sem=send_sems.at[a],
                recv_sem=recv_sems.at[a], device_id=(x, y, c), device_id_type=MESH).wait_recv()
        for cp in cps:
            cp.wait_send()

    return _pc(body, name=name, out_shape=[jax.ShapeDtypeStruct(b.shape, b.dtype) for b in bufs],
               in_specs=[_ANY] * n, out_specs=[_ANY] * n, input_output_aliases={a: a for a in range(n)},
               scratch_shapes=[pltpu.SemaphoreType.DMA((n,)), pltpu.SemaphoreType.DMA((n,))])(*bufs)


def _row_tile(R, C, nbytes=1 << 20):
    t = 8
    while t * 2 <= R and R % (t * 2) == 0 and t * 2 * C * 4 <= nbytes:
        t *= 2
    assert R % t == 0
    return t


def to_bf16_block(name, w, chip_arr, layer, after=None):
    _, R, C = w.shape
    tr = _row_tile(R, C)

    def body(j_ref, w_ref, *rest):
        o_ref = rest[-1]
        o_ref[...] = w_ref[...].astype(o_ref.dtype)

    in_specs, args = [pl.BlockSpec((None, tr, C), lambda i, j_ref: (layer, i, 0))], (chip_arr, w)
    if after is not None:
        in_specs, args = in_specs + [_ANY], args + (after,)
    gs = pltpu.PrefetchScalarGridSpec(
        num_scalar_prefetch=1, grid=(R // tr,), in_specs=in_specs,
        out_specs=pl.BlockSpec((None, None, tr, C), lambda i, j_ref: (j_ref[0], layer, i, 0)))
    return _pc(body, name=name, grid_spec=gs, out_shape=jax.ShapeDtypeStruct((N_CHIP,) + w.shape, BF16),
               compiler_params=_cp(("parallel",)))(*args)


def add_rows(name, g, ra, c_arr, chip_arr, layer):
    _, _, R, C = g.shape
    rh = R // 2
    tr = _row_tile(rh, C, nbytes=1 << 22)
    nb = rh // tr

    def body(c_ref, j_ref, g_ref, r_ref, o_ref, own_ref):
        s = (g_ref[...].astype(F32) + r_ref[...].astype(F32)).astype(o_ref.dtype)
        o_ref[...] = s

        @pl.when(pl.program_id(1) == j_ref[0])
        def _():
            own_ref[...] = s

    gs = pltpu.PrefetchScalarGridSpec(
        num_scalar_prefetch=2, grid=(nb, N_CHIP),
        in_specs=[pl.BlockSpec((None, None, tr, C), lambda i, j, c_ref, j_ref: (layer, j, c_ref[0] * nb + i, 0)),
                  pl.BlockSpec((None, tr, C), lambda i, j, c_ref, j_ref: (j, i, 0))],
        out_specs=[pl.BlockSpec((None, tr, C), lambda i, j, c_ref, j_ref: (j, i, 0)),
                   pl.BlockSpec((None, tr, C), lambda i, j, c_ref, j_ref: (j_ref[0], i, 0))])
    shp = jax.ShapeDtypeStruct(ra.shape, BF16)
    return _pc(body, name=name, grid_spec=gs, out_shape=[shp, shp],
               compiler_params=_cp(("arbitrary", "arbitrary")))(c_arr, chip_arr, g, ra)


def sum_rows_into(name, landed, c_arr, layer, into):
    n, rh, C = landed.shape
    tr = _row_tile(rh, C, nbytes=1 << 19)

    def body(*refs):
        g_ref, o_ref = refs[1], refs[-1]
        acc = g_ref[0].astype(F32)
        for j in range(1, n):
            acc = acc + g_ref[j].astype(F32)
        o_ref[...] = acc

    in_specs, args, alias = [pl.BlockSpec((n, tr, C), lambda i, c_ref: (0, i, 0))], (c_arr, landed), {}
    if into is not None:
        in_specs, args, alias = in_specs + [_ANY], args + (into,), {2: 0}
    gs = pltpu.PrefetchScalarGridSpec(
        num_scalar_prefetch=1, grid=(rh // tr,), in_specs=in_specs,
        out_specs=pl.BlockSpec((None, None, tr, C), lambda i, c_ref: (layer, c_ref[0], i, 0)))
    return _pc(body, name=name, grid_spec=gs, out_shape=jax.ShapeDtypeStruct((2, 2, rh, C), F32),
               input_output_aliases=alias, compiler_params=_cp(("parallel",)))(*args)


def sum_leading(name, g, plane=None):
    n, R, C = g.shape
    tr = _row_tile(R, C, nbytes=(1 << 21) // n)

    def body(*refs):
        g_ref, o_ref = refs[-2:]
        acc = g_ref[0].astype(F32)
        for j in range(1, n):
            acc = acc + g_ref[j].astype(F32)
        o_ref[...] = acc

    if plane is None:
        return _pc(body, name=name, grid=(R // tr,), in_specs=[pl.BlockSpec((n, tr, C), lambda i: (0, i, 0))],
                   out_specs=pl.BlockSpec((tr, C), lambda i: (i, 0)), out_shape=jax.ShapeDtypeStruct((R, C), F32),
                   compiler_params=_cp(("parallel",)))(g)
    count, idx = plane
    gs = pltpu.PrefetchScalarGridSpec(
        num_scalar_prefetch=1, grid=(R // tr,),
        in_specs=[pl.BlockSpec((n, tr, C), lambda i, p_ref: (0, i, 0))],
        out_specs=pl.BlockSpec((None, tr, C), lambda i, p_ref: (p_ref[0], i, 0)))
    return _pc(body, name=name, grid_spec=gs, out_shape=jax.ShapeDtypeStruct((count, R, C), F32),
               compiler_params=_cp(("parallel",)))(idx, g)


def adamw(name, w, g, m, v, emit_g=False):
    R, C = w.shape
    tr = _row_tile(R, C)

    def body(w_ref, g_ref, m_ref, v_ref, d_ref, mo_ref, vo_ref, *go_ref):
        gv = g_ref[...]
        if emit_g:
            go_ref[0][...] = gv
        mn = ADAM_B1 * m_ref[...] + (1.0 - ADAM_B1) * gv
        vn = ADAM_B2 * v_ref[...] + (1.0 - ADAM_B2) * (gv * gv)
        m_hat = mn / (1.0 - ADAM_B1 ** ADAM_STEP)
        v_hat = vn / (1.0 - ADAM_B2 ** ADAM_STEP)
        d_ref[...] = -ADAM_LR * (m_hat / (jnp.sqrt(v_hat) + ADAM_EPS) + ADAM_WD * w_ref[...])
        mo_ref[...] = mn
        vo_ref[...] = vn

    spec = pl.BlockSpec((tr, C), lambda i: (i, 0))
    shp = jax.ShapeDtypeStruct((R, C), F32)
    nout = 4 if emit_g else 3
    return _pc(body, name=name, grid=(R // tr,), in_specs=[spec] * 4, out_specs=[spec] * nout,
               out_shape=[shp] * nout, compiler_params=_cp(("parallel",)))(w, g, m, v)


_ADA_TN = 512


def adaln_fwd(name, cond, w, b):
    _, D, N = w.shape
    tn = min(_ADA_TN, N)

    def body(c_ref, w_ref, b_ref, o_ref):
        s = _silu(c_ref[...]).astype(BF16)
        o_ref[...] = dot_nn(s, w_ref[...].astype(BF16)) + b_ref[...]

    return _pc(body, name=name, grid=(2, N // tn),
               in_specs=[pl.BlockSpec((16, D), lambda l, n: (0, 0)),
                         pl.BlockSpec((None, D, tn), lambda l, n: (l, 0, n)),
                         pl.BlockSpec((None, 1, tn), lambda l, n: (l, 0, n))],
               out_specs=pl.BlockSpec((None, 16, tn), lambda l, n: (l, 0, n)),
               out_shape=jax.ShapeDtypeStruct((2, 16, N), F32),
               compiler_params=_cp(("parallel", "parallel")))(cond, w, b)


def adaln_bwd(name, cond, w, dm):
    _, D, N = w.shape
    tn = min(_ADA_TN, N)

    def body(c_ref, w_ref, dm_ref, gw_ref, ds_ref):
        first = jnp.logical_and(pl.program_id(0) == 0, pl.program_id(1) == 0)
        s = _silu(c_ref[...]).astype(BF16)
        dmb = dm_ref[...].astype(BF16)
        gw_ref[...] = dot_tn(s, dmb)
        p = dot_nt(dmb, w_ref[...].astype(BF16))

        @pl.when(first)
        def _():
            ds_ref[...] = p

        @pl.when(jnp.logical_not(first))
        def _():
            ds_ref[...] += p

    return _pc(body, name=name, grid=(2, N // tn),
               in_specs=[pl.BlockSpec((16, D), lambda l, n: (0, 0)),
                         pl.BlockSpec((None, D, tn), lambda l, n: (l, 0, n)),
                         pl.BlockSpec((None, 16, tn), lambda l, n: (l, 0, n))],
               out_specs=[pl.BlockSpec((None, D, tn), lambda l, n: (l, 0, n)),
                          pl.BlockSpec((16, D), lambda l, n: (0, 0))],
               out_shape=[jax.ShapeDtypeStruct((2, D, N), F32), jax.ShapeDtypeStruct((16, D), F32)],
               compiler_params=_cp(("arbitrary", "arbitrary")))(cond, w, dm)


def cctx_grad(name, parts, c_ctx):
    def body(p_ref, c_ref, o_ref):
        acc = p_ref[0]
        for j in range(1, N_CHIP):
            acc = acc + p_ref[j]
        o_ref[...] = acc * _dsilu(c_ref[...])

    return _pc(body, name=name, out_shape=jax.ShapeDtypeStruct(c_ctx.shape, F32))(parts, c_ctx)


def _pack(arrs):
    rows = []
    for a in arrs:
        f = a.reshape(-1)
        pad = (-f.shape[0]) % LANE
        rows.append(jnp.pad(f, (0, pad)).reshape(-1, LANE))
    out = jnp.concatenate(rows, axis=0)
    pad = (-out.shape[0]) % 8
    return jnp.pad(out, ((0, pad), (0, 0))) if pad else out


def _unpack(rows, shapes):
    out, r = [], 0
    for s in shapes:
        n = int(np.prod(s))
        nr = -(-n // LANE)
        out.append(rows[r:r + nr].reshape(-1)[:n].reshape(s))
        r += nr
    return out


MOD_NAMES = ("sh1", "sc1", "g1", "sh2", "sc2", "g2")


def kernel(x, c, ctx, c_ctx, w_ada, b_ada, norm1_g, w_in, ret_decay, ret_gn_g, conv_dw_w, conv_dw_b, conv_ln_g, conv_ln_b, conv_pw, na_rpb, w_out, norm2_g, ffn_up, ffn_dw_w, ffn_dw_b, ffn_down, final_g, loss_target, m_c_ctx, m_w_ada, m_b_ada, m_norm1_g, m_w_in, m_ret_decay, m_ret_gn_g, m_conv_dw_w, m_conv_dw_b, m_conv_ln_g, m_conv_ln_b, m_conv_pw, m_na_rpb, m_w_out, m_norm2_g, m_ffn_up, m_ffn_dw_w, m_ffn_dw_b, m_ffn_down, m_final_g, v_c_ctx, v_w_ada, v_b_ada, v_norm1_g, v_w_in, v_ret_decay, v_ret_gn_g, v_conv_dw_w, v_conv_dw_b, v_conv_ln_g, v_conv_ln_b, v_conv_pw, v_na_rpb, v_w_out, v_norm2_g, v_ffn_up, v_ffn_dw_w, v_ffn_dw_b, v_ffn_down, v_final_g):
    cfg = make_cfg(D=x.shape[2], T=x.shape[1], TC=ctx.shape[1], RH=ret_decay.shape[2], CW=conv_dw_b.shape[1],
                   NH=na_rpb.shape[1], DFF=ffn_dw_b.shape[1] // 2)
    D, T = cfg.D, cfg.T
    W = dict(c_ctx=c_ctx, w_ada=w_ada, b_ada=b_ada, norm1_g=norm1_g, w_in=w_in, ret_decay=ret_decay, ret_gn_g=ret_gn_g,
             conv_dw_w=conv_dw_w, conv_dw_b=conv_dw_b, conv_ln_g=conv_ln_g, conv_ln_b=conv_ln_b, conv_pw=conv_pw,
             na_rpb=na_rpb, w_out=w_out, norm2_g=norm2_g, ffn_up=ffn_up, ffn_dw_w=ffn_dw_w, ffn_dw_b=ffn_dw_b,
             ffn_down=ffn_down, final_g=final_g)
    Mo = dict(c_ctx=m_c_ctx, w_ada=m_w_ada, b_ada=m_b_ada, norm1_g=m_norm1_g, w_in=m_w_in, ret_decay=m_ret_decay,
              ret_gn_g=m_ret_gn_g, conv_dw_w=m_conv_dw_w, conv_dw_b=m_conv_dw_b, conv_ln_g=m_conv_ln_g,
              conv_ln_b=m_conv_ln_b, conv_pw=m_conv_pw, na_rpb=m_na_rpb, w_out=m_w_out, norm2_g=m_norm2_g,
              ffn_up=m_ffn_up, ffn_dw_w=m_ffn_dw_w, ffn_dw_b=m_ffn_dw_b, ffn_down=m_ffn_down, final_g=m_final_g)
    Vo = dict(c_ctx=v_c_ctx, w_ada=v_w_ada, b_ada=v_b_ada, norm1_g=v_norm1_g, w_in=v_w_in, ret_decay=v_ret_decay,
              ret_gn_g=v_ret_gn_g, conv_dw_w=v_conv_dw_w, conv_dw_b=v_conv_dw_b, conv_ln_g=v_conv_ln_g,
              conv_ln_b=v_conv_ln_b, conv_pw=v_conv_pw, na_rpb=v_na_rpb, w_out=v_w_out, norm2_g=v_norm2_g,
              ffn_up=v_ffn_up, ffn_dw_w=v_ffn_dw_w, ffn_dw_b=v_ffn_dw_b, ffn_down=v_ffn_down, final_g=v_final_g)
    order = list(W)
    xi, yi, ci = lax.axis_index("x"), lax.axis_index("y"), lax.axis_index("c")
    chip = 2 * xi + yi
    dev = 4 * xi + 2 * yi + ci
    NA = w_ada.shape[2]
    ncw, nfw = conv_dw_w.shape[2], ffn_dw_w.shape[2]

    c_arr = jnp.reshape(ci, (1,)).astype(jnp.int32)
    chip_arr = jnp.reshape(chip, (1,)).astype(jnp.int32)

    g_in = allgather8("ag_small_in", _pack([c[0], conv_dw_w, ffn_dw_w])).reshape(N_DEV, -1, LANE)
    c8 = g_in[:, :D // LANE].reshape(N_DEV, D)
    cw_parts, fw_parts = [], []
    for j in range(N_CHIP):
        _, a, b = _unpack(g_in[2 * j], [(D,), conv_dw_w.shape, ffn_dw_w.shape])
        cw_parts.append(a)
        fw_parts.append(b)
    conv_dw_w_full = jnp.concatenate(cw_parts, axis=2)
    ffn_dw_w_full = jnp.concatenate(fw_parts, axis=2)
    cond = jnp.concatenate([c8, c_ctx[None], jnp.zeros((16 - N_DEV - 1, D), F32)], axis=0)

    b_sh = lax.dynamic_slice(b_ada, (0, chip * NA), (2, NA)).reshape(2, 1, NA)
    m_sh = adaln_fwd("adaln_fwd", cond, w_ada, b_sh)
    m_dev = allgather8("ag_mod", m_sh.reshape(2 * 16, NA)).reshape(N_DEV, 2, 16, NA)
    m_all = jnp.concatenate([m_dev[2 * j] for j in range(N_CHIP)], axis=-1)
    mods = []
    for l in range(2):
        lat = lax.dynamic_slice(m_all[l], (dev, 0), (1, N_CHIP * NA))[0]
        cx = m_all[l, N_DEV]
        mods.append({nm: jnp.stack([lat[k * D:(k + 1) * D], cx[k * D:(k + 1) * D]], 0)[:, None, :]
                     for k, nm in enumerate(MOD_NAMES)})

    first, rest = ("w_in", "conv_pw"), ("w_out", "ffn_up", "ffn_down")
    wb = [{}, {}]
    have, flying_w = {}, {}

    def start_gather(tag, l, names, after):
        s_sem, r_sem, bufs, tok = allgather_layer_start(f"ag_{tag}_start", [wb[l][nm] for nm in names], l, after)
        flying_w[(l, names[0])] = (tag, l, names, bufs, s_sem, r_sem)
        return tok

    def land_gather(key, after):
        tag, l, names, bufs, s_sem, r_sem = flying_w.pop(key)
        landed = allgather_layer_wait(f"ag_{tag}_wait", bufs, s_sem, r_sem, after, l)
        have.update(zip([(l, nm) for nm in names], forward_halves(f"ag_{tag}_fwd", list(landed), l)))

    for nm in first:
        wb[0][nm] = to_bf16_block(f"to_bf16_{nm}_0", W[nm], chip_arr, 0)
    tok_first = start_gather("w0a", 0, first, m_all)
    later = [(l, nm) for l in range(2) for nm in BIG if nm not in wb[l]]
    casts = lax.optimization_barrier(tuple(
        to_bf16_block(f"to_bf16_{nm}_{l}", W[nm], chip_arr, l, after=tok_first) for l, nm in later))
    for (l, nm), cast in zip(later, casts):
        wb[l][nm] = cast
    land_gather((0, first[0]), casts[0])
    mods[0] = {**mods[0], "sc1": mods[0]["sc1"] + start_gather("w0b", 0, rest, have[(0, first[0])])[0, 0]}

    def wts(l, name, after):
        tok = None
        if (l, name) not in have:
            if l == 0:
                land_gather((0, rest[0]), after)
                tok = start_gather("w1", 1, BIG, have[(0, rest[0])])[0, 0]
            else:
                land_gather((1, BIG[0]), after)
        return have[(l, name)], tok

    sp = dict(norm1_g=norm1_g, norm2_g=norm2_g, ret_decay=ret_decay, ret_gn_g=ret_gn_g, conv_dw_w=conv_dw_w_full,
              conv_dw_b=conv_dw_b, conv_ln_g=conv_ln_g, conv_ln_b=conv_ln_b, na_rpb=na_rpb, ffn_dw_w=ffn_dw_w_full,
              ffn_dw_b=ffn_dw_b, final_g=final_g)
    flights = []

    def grads_ready(l, names, gb):
        tag = f"{l}_{names[0]}"
        if names == GRAD_GROUPS[0]:
            s_sem, r_sem, thru, lands, tok = exchange_rows_start(f"rs_exchange_{tag}_start", [gb[nm] for nm in names], l)
            gb.update(zip(names, thru))
            exchanging[l] = (names, tag, lands, s_sem, r_sem)
            return tok[0, 0]
        return scatter_group(l, names, tag, gb, exchange_rows(f"rs_exchange_{tag}", [gb[nm] for nm in names], l))

    def scatter_group(l, names, tag, gb, from_sib):
        both = [add_rows(f"rs_add_{nm}_{l}", gb[nm], r, c_arr, chip_arr, l) for nm, r in zip(names, from_sib)]
        part, lands = [b[0] for b in both], [b[1] for b in both]
        s_sem, r_sem, part, lands, tok = scatter_slices_start(f"rs_scatter_{tag}_start", part, lands)
        flights.append((l, names, tag, part, lands, (s_sem, r_sem)))
        return tok[0, 0]

    exchanging = {}

    def grads_late(l, gb, after):
        names, tag, lands, s_sem, r_sem = exchanging.pop(l)
        thru, from_sib = exchange_rows_wait(f"rs_exchange_{tag}_wait", [gb[nm] for nm in names], lands,
                                            s_sem, r_sem, after, l)
        gb.update(zip(names, thru))
        return scatter_group(l, names, tag, gb, from_sib)

    loss_l, gx, gb, gss, dms, dfg, tok_last = local_step(
        cfg, x[0], ctx[0], loss_target[0], mods, wts, sp, grads_ready, grads_late)
    loss = lax.psum(loss_l, ("x", "y", "c"))

    delta, new_m, new_v = {}, {}, {}
    bigs = ("w_ada",) + BIG

    def adamw_big(nm):
        shp = W[nm].shape
        v2 = lambda a: a.reshape(-1, shp[-1])
        d_, m_, v_, *g_ = adamw(f"adamw_{nm}", v2(W[nm]), v2(gfull[nm]), v2(Mo[nm]), v2(Vo[nm]), emit_g=nm in BIG)
        delta[nm], new_m[nm], new_v[nm] = d_.reshape(shp), m_.reshape(shp), v_.reshape(shp)
        if g_:
            gfull[nm] = g_[0].reshape(shp)

    gfull, fin, after = {}, {}, gx
    for names in GRAD_GROUPS:
        for l, _, tag, part, lands, sems in sorted([f for f in flights if f[1] == names], key=lambda f: -f[0]):
            landed = scatter_slices_wait(f"rs_scatter_{tag}_wait", part, lands, *sems, after)
            for nm, p in zip(names, landed):
                fin[nm] = sum_rows_into(f"rs_sum_{nm}_{l}", p, c_arr, l, fin.get(nm))
        for nm, gfin in zip(names, share_rows(f"rs_share_{names[0]}", [fin[nm] for nm in names])):
            gfull[nm] = gfin.reshape(W[nm].shape)
            adamw_big(nm)
        after = delta[names[-1]]

    dmseg = jnp.stack([jnp.stack([jnp.concatenate([dms[l][nm][r, 0] for nm in MOD_NAMES]) for r in range(2)])
                       for l in range(2)])
    dmseg, _ = lax.optimization_barrier((dmseg, fin[GRAD_GROUPS[-1][-1]]))
    gsm = dict(
        norm1_g=jnp.stack([gss[l]["norm1_g"][0] for l in range(2)]),
        ret_decay=jnp.stack([gss[l]["lam"] * jax.nn.sigmoid(-ret_decay[l]) for l in range(2)]),
        ret_gn_g=jnp.stack([gss[l]["ret_gn_g"][0] for l in range(2)]),
        conv_dw_w=jnp.stack([gss[l]["conv_dw_w"][:cfg.CK] for l in range(2)]),
        conv_dw_b=jnp.stack([gss[l]["conv_dw_b"][0] for l in range(2)]),
        conv_ln_g=jnp.stack([gss[l]["conv_ln_g"][0] for l in range(2)]),
        conv_ln_b=jnp.stack([gss[l]["conv_ln_b"][0] for l in range(2)]),
        na_rpb=jnp.stack([gss[l]["na_rpb"] for l in range(2)]),
        norm2_g=jnp.stack([gss[l]["norm2_g"][0] for l in range(2)]),
        ffn_dw_w=jnp.stack([gss[l]["ffn_dw_w"][:, :3].transpose(1, 0, 2).reshape(3, 2 * cfg.DFF) for l in range(2)]),
        ffn_dw_b=jnp.stack([gss[l]["ffn_dw_b"].reshape(-1) for l in range(2)]),
        final_g=dfg)
    snames = list(gsm)
    sshapes = [dmseg.shape] + [gsm[nm].shape for nm in snames]
    packed = _pack([dmseg] + [gsm[nm] for nm in snames])
    g_all = allgather8("ag_small_grads", packed).reshape(N_DEV, packed.shape[0], LANE)
    summed = sum_leading("sum_small_grads", g_all)
    dm_sum, *gsum = _unpack(summed, sshapes)
    gfull.update(zip(snames, gsum))
    ndm = int(np.prod(dmseg.shape))
    dm_all = g_all[:, :ndm // LANE].reshape(N_DEV, 2, 2, 6 * D)
    gfull["b_ada"] = sum_leading("sum_b_ada", dm_all.transpose(0, 2, 1, 3).reshape(2 * N_DEV, 2 * 6 * D // LANE, LANE)
                                 ).reshape(2, 6 * D)

    dm16 = jnp.concatenate([dm_all[:, :, 0].transpose(1, 0, 2), dm_sum[:, 1][:, None],
                            jnp.zeros((2, 16 - N_DEV - 1, 6 * D), F32)], axis=1)
    dm16 = lax.dynamic_slice(dm16, (0, 0, chip * NA), (2, 16, NA))
    gfull["w_ada"], ds16 = adaln_bwd("adaln_bwd", cond, w_ada, dm16)
    ds_all = allgather8("ag_dsilu", ds16[8:16]).reshape(N_DEV, 8, D)[0::2, 0:1]
    gfull["c_ctx"] = cctx_grad("cctx_grad", ds_all, c_ctx[None])[0]
    gfull["conv_dw_w"] = lax.dynamic_slice(gfull["conv_dw_w"], (0, 0, chip * ncw), (2, cfg.CK, ncw))
    gfull["ffn_dw_w"] = lax.dynamic_slice(gfull["ffn_dw_w"], (0, 0, chip * nfw), (2, 3, nfw))

    adamw_big("w_ada")
    smalls = [nm for nm in order if nm not in bigs]
    shapes = [W[nm].shape for nm in smalls]
    d_, m_, v_ = adamw("adamw_small", _pack([W[nm] for nm in smalls]), _pack([gfull[nm] for nm in smalls]),
                       _pack([Mo[nm] for nm in smalls]), _pack([Vo[nm] for nm in smalls]))
    for nm, a, b, e in zip(smalls, _unpack(d_, shapes), _unpack(m_, shapes), _unpack(v_, shapes)):
        delta[nm], new_m[nm], new_v[nm] = a, b, e
    return (loss, gx[None], *[gfull[nm] for nm in order], *[delta[nm] for nm in order],
            *[new_m[nm] for nm in order], *[new_v[nm] for nm in order])
```
